```python
import jax, jax.numpy as jnp
from jax import lax
import numpy as np

D_MODEL = 1024
BATCH = 8
SEQ = 8192
DEPTH = 4

CHUNK = 64
Q_BLOCK = 128
N_A_LAYERS = DEPTH // 2
N_B_LAYERS = DEPTH - N_A_LAYERS
SB_HEADS = 16
SB_HEAD_DIM = D_MODEL // SB_HEADS
MLA_HEADS = 16
MLA_NOPE_DIM = 64
MLA_ROPE_DIM = 32
MLA_V_DIM = 64
MLA_Q_RANK = 384
MLA_KV_RANK = 256
D_FF = 4 * D_MODEL
ROPE_THETA = 10000.0
NORM_EPS = 1e-6
MAX_POS_OFFSET = 4096

kernel_name = "yoco_stickbreak_mla_hybrid"


def rms_norm(x, g):
    xf = x.astype(jnp.float32)
    y = xf * lax.rsqrt(jnp.mean(xf * xf, axis=-1, keepdims=True) + NORM_EPS)
    return (y * g.astype(jnp.float32)).astype(x.dtype)


def rope_tables(positions):
    inv_freq = ROPE_THETA ** (-jnp.arange(0, MLA_ROPE_DIM, 2, dtype=jnp.float32) / MLA_ROPE_DIM)
    ang = positions.astype(jnp.float32)[..., None] * inv_freq
    return jnp.cos(ang), jnp.sin(ang)


def apply_rope(t, cos, sin):
    half = MLA_ROPE_DIM // 2
    t1 = t[..., :half].astype(jnp.float32)
    t2 = t[..., half:].astype(jnp.float32)
    out = jnp.concatenate([t1 * cos - t2 * sin, t2 * cos + t1 * sin], axis=-1)
    return out.astype(t.dtype)


def to_query_blocks(t):
    b, s, h, d = t.shape
    return t.reshape(b, s // Q_BLOCK, Q_BLOCK, h, d).transpose(1, 0, 3, 2, 4)


def from_query_blocks(t):
    nb, b, qb, h, d = t.shape
    return t.transpose(1, 0, 2, 3, 4).reshape(b, nb * qb, h * d)


def stick_breaking_attention(h, w_qkv, w_o):
    b, s, _ = h.shape
    qkv = (h @ w_qkv).reshape(b, s, 3, SB_HEADS, SB_HEAD_DIM)
    q, k, v = qkv[:, :, 0], qkv[:, :, 1], qkv[:, :, 2]
    k_h = k.transpose(0, 2, 1, 3)
    v_h = v.transpose(0, 2, 1, 3)
    q_blocks = to_query_blocks(q)
    scale = SB_HEAD_DIM ** -0.5
    key_pos = jnp.arange(s)

    def block(args):
        q_blk, blk_idx = args
        q_pos = blk_idx * Q_BLOCK + jnp.arange(Q_BLOCK)
        z = jnp.einsum('bhqd,bhkd->bhqk', q_blk, k_h).astype(jnp.float32) * scale
        causal = key_pos[None, :] < q_pos[:, None]
        log_beta = jax.nn.log_sigmoid(z)
        log_keep = jnp.where(causal, jax.nn.log_sigmoid(-z), 0.0)
        shifted = jnp.concatenate([log_keep[..., 1:], jnp.zeros_like(log_keep[..., :1])], axis=-1)
        log_survive = lax.cumsum(shifted, axis=3, reverse=True)
        weights = jnp.where(causal, jnp.exp(log_beta + log_survive), 0.0)
        return jnp.einsum('bhqk,bhkd->bqhd', weights.astype(v_h.dtype), v_h)

    out = lax.map(block, (q_blocks, jnp.arange(s // Q_BLOCK)))
    return from_query_blocks(out) @ w_o


def mla_shared_kv(h, w_dkv, kv_lat_norm, w_ukv, cos, sin):
    b, s, _ = h.shape
    down = h @ w_dkv
    c_kv = rms_norm(down[..., :MLA_KV_RANK], kv_lat_norm)
    k_rope = apply_rope(down[..., MLA_KV_RANK:], cos, sin)
    kv = (c_kv @ w_ukv).reshape(b, s, MLA_HEADS, MLA_NOPE_DIM + MLA_V_DIM)
    k_nope = kv[..., :MLA_NOPE_DIM].transpose(0, 2, 1, 3)
    v = kv[..., MLA_NOPE_DIM:].transpose(0, 2, 1, 3)
    return k_nope, k_rope, v


def mla_attention(h, w_dq, q_lat_norm, w_uq, w_o, k_nope, k_rope, v, cos, sin):
    b, s, _ = h.shape
    c_q = rms_norm(h @ w_dq, q_lat_norm)
    q = (c_q @ w_uq).reshape(b, s, MLA_HEADS, MLA_NOPE_DIM + MLA_ROPE_DIM)
    q_nope = q[..., :MLA_NOPE_DIM]
    q_rope = apply_rope(q[..., MLA_NOPE_DIM:], cos[:, :, None, :], sin[:, :, None, :])
    qn_blocks = to_query_blocks(q_nope)
    qr_blocks = to_query_blocks(q_rope)
    scale = (MLA_NOPE_DIM + MLA_ROPE_DIM) ** -0.5
    key_chunk = jnp.arange(s) // CHUNK

    def block(args):
        qn, qr, blk_idx = args
        q_chunk = (blk_idx * Q_BLOCK + jnp.arange(Q_BLOCK)) // CHUNK
        scores = (jnp.einsum('bhqd,bhkd->bhqk', qn, k_nope)
                  + jnp.einsum('bhqr,bkr->bhqk', qr, k_rope)).astype(jnp.float32) * scale
        allowed = key_chunk[None, :] <= q_chunk[:, None]
        probs = jax.nn.softmax(jnp.where(allowed, scores, -jnp.inf), axis=-1)
        return jnp.einsum('bhqk,bhkd->bqhd', probs.astype(v.dtype), v)

    out = lax.map(block, (qn_blocks, qr_blocks, jnp.arange(s // Q_BLOCK)))
    return from_query_blocks(out) @ w_o


def squared_relu_mlp(h, w1, w2):
    return jnp.square(jax.nn.relu(h @ w1)) @ w2


def _fwd_setup_inputs(seed: int = 0) -> dict:
    key = jax.random.key(seed)
    ks = jax.random.split(key, 20)

    def w(k, shape, fan_in, gain=1.0):
        return jax.random.normal(k, shape, jnp.float32) * (gain * fan_in ** -0.5)

    def gains(k, shape):
        return 1.0 + 0.02 * jax.random.normal(k, shape, jnp.float32)

    out_gain = (2.0 * DEPTH) ** -0.5
    x = jax.random.normal(ks[0], (BATCH, SEQ, D_MODEL), jnp.float32)
    offsets = jax.random.randint(ks[1], (BATCH, 1), 0, MAX_POS_OFFSET, dtype=jnp.int32)
    positions = offsets + jnp.arange(SEQ, dtype=jnp.int32)[None, :]
    return {
        "x": x,
        "positions": positions,
        "attn_norm": gains(ks[2], (DEPTH, D_MODEL)),
        "mlp_norm": gains(ks[3], (DEPTH, D_MODEL)),
        "sb_w_qkv": w(ks[4], (N_A_LAYERS, D_MODEL, 3 * D_MODEL), D_MODEL),
        "sb_w_o": w(ks[5], (N_A_LAYERS, D_MODEL, D_MODEL), D_MODEL, out_gain),
        "kv_norm": gains(ks[6], (D_MODEL,)),
        "mla_w_dkv": w(ks[7], (D_MODEL, MLA_KV_RANK + MLA_ROPE_DIM), D_MODEL),
        "mla_kv_lat_norm": gains(ks[8], (MLA_KV_RANK,)),
        "mla_w_ukv": w(ks[9], (MLA_KV_RANK, MLA_HEADS * (MLA_NOPE_DIM + MLA_V_DIM)), MLA_KV_RANK),
        "mla_w_dq": w(ks[10], (N_B_LAYERS, D_MODEL, MLA_Q_RANK), D_MODEL),
        "mla_q_lat_norm": gains(ks[11], (N_B_LAYERS, MLA_Q_RANK)),
        "mla_w_uq": w(ks[12], (N_B_LAYERS, MLA_Q_RANK, MLA_HEADS * (MLA_NOPE_DIM + MLA_ROPE_DIM)), MLA_Q_RANK),
        "mla_w_o": w(ks[13], (N_B_LAYERS, MLA_HEADS * MLA_V_DIM, D_MODEL), MLA_HEADS * MLA_V_DIM, out_gain),
        "mlp_w1": w(ks[14], (DEPTH, D_MODEL, D_FF), D_MODEL),
        "mlp_w2": w(ks[15], (DEPTH, D_FF, D_MODEL), D_FF, out_gain),
        "final_norm": gains(ks[16], (D_MODEL,)),
    }


def _fwd_reference(x, positions, attn_norm, mlp_norm, sb_w_qkv, sb_w_o, kv_norm, mla_w_dkv,
              mla_kv_lat_norm, mla_w_ukv, mla_w_dq, mla_q_lat_norm, mla_w_uq, mla_w_o,
              mlp_w1, mlp_w2, final_norm):
    cos, sin = rope_tables(positions)
    k_nope = k_rope = v = None
    for layer in range(DEPTH):
        if layer < N_A_LAYERS:
            x = x + stick_breaking_attention(rms_norm(x, attn_norm[layer]),
                                             sb_w_qkv[layer], sb_w_o[layer])
        else:
            if layer == N_A_LAYERS:
                k_nope, k_rope, v = mla_shared_kv(rms_norm(x, kv_norm), mla_w_dkv,
                                                  mla_kv_lat_norm, mla_w_ukv, cos, sin)
            j = layer - N_A_LAYERS
            x = x + mla_attention(rms_norm(x, attn_norm[layer]), mla_w_dq[j], mla_q_lat_norm[j],
                                  mla_w_uq[j], mla_w_o[j], k_nope, k_rope, v, cos, sin)
        x = x + squared_relu_mlp(rms_norm(x, mlp_norm[layer]), mlp_w1[layer], mlp_w2[layer])
    return rms_norm(x, final_norm)


import jax as _jax
import jax.numpy as _jnp

TWIN_FORMAT = 'train_step'
FWD_PARAMS = ['x', 'positions', 'attn_norm', 'mlp_norm', 'sb_w_qkv', 'sb_w_o', 'kv_norm', 'mla_w_dkv', 'mla_kv_lat_norm', 'mla_w_ukv', 'mla_w_dq', 'mla_q_lat_norm', 'mla_w_uq', 'mla_w_o', 'mlp_w1', 'mlp_w2', 'final_norm']
TWIN_WEIGHTS = ['attn_norm', 'mlp_norm', 'sb_w_qkv', 'sb_w_o', 'kv_norm', 'mla_w_dkv', 'mla_kv_lat_norm', 'mla_w_ukv', 'mla_w_dq', 'mla_q_lat_norm', 'mla_w_uq', 'mla_w_o', 'mlp_w1', 'mlp_w2', 'final_norm']
TWIN_DIFF_INPUT = 'x'
TWIN_INPUTS = ['x', 'positions', 'attn_norm', 'mlp_norm', 'sb_w_qkv', 'sb_w_o', 'kv_norm', 'mla_w_dkv', 'mla_kv_lat_norm', 'mla_w_ukv', 'mla_w_dq', 'mla_q_lat_norm', 'mla_w_uq', 'mla_w_o', 'mlp_w1', 'mlp_w2', 'final_norm', 'loss_target', 'm_attn_norm', 'm_mlp_norm', 'm_sb_w_qkv', 'm_sb_w_o', 'm_kv_norm', 'm_mla_w_dkv', 'm_mla_kv_lat_norm', 'm_mla_w_ukv', 'm_mla_w_dq', 'm_mla_q_lat_norm', 'm_mla_w_uq', 'm_mla_w_o', 'm_mlp_w1', 'm_mlp_w2', 'm_final_norm', 'v_attn_norm', 'v_mlp_norm', 'v_sb_w_qkv', 'v_sb_w_o', 'v_kv_norm', 'v_mla_w_dkv', 'v_mla_kv_lat_norm', 'v_mla_w_ukv', 'v_mla_w_dq', 'v_mla_q_lat_norm', 'v_mla_w_uq', 'v_mla_w_o', 'v_mlp_w1', 'v_mlp_w2', 'v_final_norm']
TWIN_OUTPUTS = ['loss', 'grad_x', 'grad_attn_norm', 'grad_mlp_norm', 'grad_sb_w_qkv', 'grad_sb_w_o', 'grad_kv_norm', 'grad_mla_w_dkv', 'grad_mla_kv_lat_norm', 'grad_mla_w_ukv', 'grad_mla_w_dq', 'grad_mla_q_lat_norm', 'grad_mla_w_uq', 'grad_mla_w_o', 'grad_mlp_w1', 'grad_mlp_w2', 'grad_final_norm', 'delta_attn_norm', 'delta_mlp_norm', 'delta_sb_w_qkv', 'delta_sb_w_o', 'delta_kv_norm', 'delta_mla_w_dkv', 'delta_mla_kv_lat_norm', 'delta_mla_w_ukv', 'delta_mla_w_dq', 'delta_mla_q_lat_norm', 'delta_mla_w_uq', 'delta_mla_w_o', 'delta_mlp_w1', 'delta_mlp_w2', 'delta_final_norm', 'new_m_attn_norm', 'new_m_mlp_norm', 'new_m_sb_w_qkv', 'new_m_sb_w_o', 'new_m_kv_norm', 'new_m_mla_w_dkv', 'new_m_mla_kv_lat_norm', 'new_m_mla_w_ukv', 'new_m_mla_w_dq', 'new_m_mla_q_lat_norm', 'new_m_mla_w_uq', 'new_m_mla_w_o', 'new_m_mlp_w1', 'new_m_mlp_w2', 'new_m_final_norm', 'new_v_attn_norm', 'new_v_mlp_norm', 'new_v_sb_w_qkv', 'new_v_sb_w_o', 'new_v_kv_norm', 'new_v_mla_w_dkv', 'new_v_mla_kv_lat_norm', 'new_v_mla_w_ukv', 'new_v_mla_w_dq', 'new_v_mla_q_lat_norm', 'new_v_mla_w_uq', 'new_v_mla_w_o', 'new_v_mlp_w1', 'new_v_mlp_w2', 'new_v_final_norm']
TWIN_LEAF_KINDS = {'loss': 'loss', 'grad_x': 'grad_x', 'grad_attn_norm': 'grad_w', 'grad_mlp_norm': 'grad_w', 'grad_sb_w_qkv': 'grad_w', 'grad_sb_w_o': 'grad_w', 'grad_kv_norm': 'grad_w', 'grad_mla_w_dkv': 'grad_w', 'grad_mla_kv_lat_norm': 'grad_w', 'grad_mla_w_ukv': 'grad_w', 'grad_mla_w_dq': 'grad_w', 'grad_mla_q_lat_norm': 'grad_w', 'grad_mla_w_uq': 'grad_w', 'grad_mla_w_o': 'grad_w', 'grad_mlp_w1': 'grad_w', 'grad_mlp_w2': 'grad_w', 'grad_final_norm': 'grad_w', 'delta_attn_norm': 'delta_w', 'delta_mlp_norm': 'delta_w', 'delta_sb_w_qkv': 'delta_w', 'delta_sb_w_o': 'delta_w', 'delta_kv_norm': 'delta_w', 'delta_mla_w_dkv': 'delta_w', 'delta_mla_kv_lat_norm': 'delta_w', 'delta_mla_w_ukv': 'delta_w', 'delta_mla_w_dq': 'delta_w', 'delta_mla_q_lat_norm': 'delta_w', 'delta_mla_w_uq': 'delta_w', 'delta_mla_w_o': 'delta_w', 'delta_mlp_w1': 'delta_w', 'delta_mlp_w2': 'delta_w', 'delta_final_norm': 'delta_w', 'new_m_attn_norm': 'new_m', 'new_m_mlp_norm': 'new_m', 'new_m_sb_w_qkv': 'new_m', 'new_m_sb_w_o': 'new_m', 'new_m_kv_norm': 'new_m', 'new_m_mla_w_dkv': 'new_m', 'new_m_mla_kv_lat_norm': 'new_m', 'new_m_mla_w_ukv': 'new_m', 'new_m_mla_w_dq': 'new_m', 'new_m_mla_q_lat_norm': 'new_m', 'new_m_mla_w_uq': 'new_m', 'new_m_mla_w_o': 'new_m', 'new_m_mlp_w1': 'new_m', 'new_m_mlp_w2': 'new_m', 'new_m_final_norm': 'new_m', 'new_v_attn_norm': 'new_v', 'new_v_mlp_norm': 'new_v', 'new_v_sb_w_qkv': 'new_v', 'new_v_sb_w_o': 'new_v', 'new_v_kv_norm': 'new_v', 'new_v_mla_w_dkv': 'new_v', 'new_v_mla_kv_lat_norm': 'new_v', 'new_v_mla_w_ukv': 'new_v', 'new_v_mla_w_dq': 'new_v', 'new_v_mla_q_lat_norm': 'new_v', 'new_v_mla_w_uq': 'new_v', 'new_v_mla_w_o': 'new_v', 'new_v_mlp_w1': 'new_v', 'new_v_mlp_w2': 'new_v', 'new_v_final_norm': 'new_v'}


def _forward(args):
    return _fwd_reference(*[args[k] for k in FWD_PARAMS])


def _output_shape():
    def fwd():
        inp = _fwd_setup_inputs(0)
        return _fwd_reference(*[inp[k] for k in FWD_PARAMS])
    out = _jax.eval_shape(fwd)
    return out.shape, out.dtype

N_MICROBATCH = 1
ADAM_LR = 0.001
ADAM_B1 = 0.9
ADAM_B2 = 0.999
ADAM_EPS = 1e-08
ADAM_WD = 0.01
ADAM_STEP = 10
PER_EXAMPLE_BATCH_AXIS = {'x': 0, 'positions': 0, 'loss_target': 0}
SHARED_INPUTS = []
_WEIGHT_DTYPES = {'attn_norm': _jnp.float32, 'mlp_norm': _jnp.float32, 'sb_w_qkv': _jnp.float32, 'sb_w_o': _jnp.float32, 'kv_norm': _jnp.float32, 'mla_w_dkv': _jnp.float32, 'mla_kv_lat_norm': _jnp.float32, 'mla_w_ukv': _jnp.float32, 'mla_w_dq': _jnp.float32, 'mla_q_lat_norm': _jnp.float32, 'mla_w_uq': _jnp.float32, 'mla_w_o': _jnp.float32, 'mlp_w1': _jnp.float32, 'mlp_w2': _jnp.float32, 'final_norm': _jnp.float32}
MOMENT_SCALE = {'attn_norm': 5.043298e-02, 'mlp_norm': 1.070834e-01, 'sb_w_qkv': 4.020037e-02, 'sb_w_o': 1.664287e-01, 'kv_norm': 3.120316e-02, 'mla_w_dkv': 5.919534e-02, 'mla_kv_lat_norm': 7.620972e-02, 'mla_w_ukv': 2.414448e-02, 'mla_w_dq': 1.672300e-02, 'mla_q_lat_norm': 1.725472e-02, 'mla_w_uq': 8.357067e-03, 'mla_w_o': 7.577933e-02, 'mlp_w1': 5.292691e-02, 'mlp_w2': 3.004388e-01, 'final_norm': 6.448401e+01}


def _to_microbatches(a, axis):
    t = _jnp.moveaxis(a, axis, 0)
    t = t.reshape((N_MICROBATCH, t.shape[0] // N_MICROBATCH) + t.shape[1:])
    return _jnp.moveaxis(t, 1, axis + 1)


def setup_inputs(seed: int = 0) -> dict:
    inp = _fwd_setup_inputs(seed)
    key = _jax.random.fold_in(_jax.random.key(seed), 7919)
    shape, _ = _output_shape()
    out = dict(inp)
    out["loss_target"] = _jax.random.normal(_jax.random.fold_in(key, 0), shape, _jnp.float32)
    for i, name in enumerate(TWIN_WEIGHTS):
        w = inp[name].astype(_jnp.float32)
        if MOMENT_SCALE is None:
            s = _jnp.sqrt(_jnp.mean(_jnp.square(w)) + 1e-30)
        else:
            s = MOMENT_SCALE[name]
        km, kv = _jax.random.split(_jax.random.fold_in(key, i + 1))
        out[name] = w
        out["m_" + name] = s * _jax.random.normal(km, w.shape, _jnp.float32)
        out["v_" + name] = (s * s) * _jax.random.uniform(kv, w.shape, _jnp.float32, 0.5, 1.5)
    if N_MICROBATCH > 1:
        for name, axis in PER_EXAMPLE_BATCH_AXIS.items():
            out[name] = _to_microbatches(out[name], axis)
    return {'x': out['x'], 'positions': out['positions'], 'attn_norm': out['attn_norm'], 'mlp_norm': out['mlp_norm'], 'sb_w_qkv': out['sb_w_qkv'], 'sb_w_o': out['sb_w_o'], 'kv_norm': out['kv_norm'], 'mla_w_dkv': out['mla_w_dkv'], 'mla_kv_lat_norm': out['mla_kv_lat_norm'], 'mla_w_ukv': out['mla_w_ukv'], 'mla_w_dq': out['mla_w_dq'], 'mla_q_lat_norm': out['mla_q_lat_norm'], 'mla_w_uq': out['mla_w_uq'], 'mla_w_o': out['mla_w_o'], 'mlp_w1': out['mlp_w1'], 'mlp_w2': out['mlp_w2'], 'final_norm': out['final_norm'], 'loss_target': out['loss_target'], 'm_attn_norm': out['m_attn_norm'], 'm_mlp_norm': out['m_mlp_norm'], 'm_sb_w_qkv': out['m_sb_w_qkv'], 'm_sb_w_o': out['m_sb_w_o'], 'm_kv_norm': out['m_kv_norm'], 'm_mla_w_dkv': out['m_mla_w_dkv'], 'm_mla_kv_lat_norm': out['m_mla_kv_lat_norm'], 'm_mla_w_ukv': out['m_mla_w_ukv'], 'm_mla_w_dq': out['m_mla_w_dq'], 'm_mla_q_lat_norm': out['m_mla_q_lat_norm'], 'm_mla_w_uq': out['m_mla_w_uq'], 'm_mla_w_o': out['m_mla_w_o'], 'm_mlp_w1': out['m_mlp_w1'], 'm_mlp_w2': out['m_mlp_w2'], 'm_final_norm': out['m_final_norm'], 'v_attn_norm': out['v_attn_norm'], 'v_mlp_norm': out['v_mlp_norm'], 'v_sb_w_qkv': out['v_sb_w_qkv'], 'v_sb_w_o': out['v_sb_w_o'], 'v_kv_norm': out['v_kv_norm'], 'v_mla_w_dkv': out['v_mla_w_dkv'], 'v_mla_kv_lat_norm': out['v_mla_kv_lat_norm'], 'v_mla_w_ukv': out['v_mla_w_ukv'], 'v_mla_w_dq': out['v_mla_w_dq'], 'v_mla_q_lat_norm': out['v_mla_q_lat_norm'], 'v_mla_w_uq': out['v_mla_w_uq'], 'v_mla_w_o': out['v_mla_w_o'], 'v_mlp_w1': out['v_mlp_w1'], 'v_mlp_w2': out['v_mlp_w2'], 'v_final_norm': out['v_final_norm']}


def _loss(weights, diff, rest, loss_target):
    with _jax.named_scope("forward"):
        args = {**rest, TWIN_DIFF_INPUT: diff, **{k: w.astype(_WEIGHT_DTYPES[k]) for k, w in weights.items()}}
        y = _forward(args)
    with _jax.named_scope("loss_head"):
        err = _jnp.square(y.astype(_jnp.float32) - loss_target)
        return 0.5 * _jnp.sum(_jnp.mean(err, axis=-1)) if err.ndim else 0.5 * err


def _adamw(w, g, m, v):
    m = ADAM_B1 * m + (1.0 - ADAM_B1) * g
    v = ADAM_B2 * v + (1.0 - ADAM_B2) * _jnp.square(g)
    m_hat = m / (1.0 - ADAM_B1 ** ADAM_STEP)
    v_hat = v / (1.0 - ADAM_B2 ** ADAM_STEP)
    delta = -ADAM_LR * (m_hat / (_jnp.sqrt(v_hat) + ADAM_EPS) + ADAM_WD * w)
    return delta, m, v


def reference(x, positions, attn_norm, mlp_norm, sb_w_qkv, sb_w_o, kv_norm, mla_w_dkv, mla_kv_lat_norm, mla_w_ukv, mla_w_dq, mla_q_lat_norm, mla_w_uq, mla_w_o, mlp_w1, mlp_w2, final_norm, loss_target, m_attn_norm, m_mlp_norm, m_sb_w_qkv, m_sb_w_o, m_kv_norm, m_mla_w_dkv, m_mla_kv_lat_norm, m_mla_w_ukv, m_mla_w_dq, m_mla_q_lat_norm, m_mla_w_uq, m_mla_w_o, m_mlp_w1, m_mlp_w2, m_final_norm, v_attn_norm, v_mlp_norm, v_sb_w_qkv, v_sb_w_o, v_kv_norm, v_mla_w_dkv, v_mla_kv_lat_norm, v_mla_w_ukv, v_mla_w_dq, v_mla_q_lat_norm, v_mla_w_uq, v_mla_w_o, v_mlp_w1, v_mlp_w2, v_final_norm):
    given = dict(x=x, positions=positions, attn_norm=attn_norm, mlp_norm=mlp_norm, sb_w_qkv=sb_w_qkv, sb_w_o=sb_w_o, kv_norm=kv_norm, mla_w_dkv=mla_w_dkv, mla_kv_lat_norm=mla_kv_lat_norm, mla_w_ukv=mla_w_ukv, mla_w_dq=mla_w_dq, mla_q_lat_norm=mla_q_lat_norm, mla_w_uq=mla_w_uq, mla_w_o=mla_w_o, mlp_w1=mlp_w1, mlp_w2=mlp_w2, final_norm=final_norm, loss_target=loss_target, m_attn_norm=m_attn_norm, m_mlp_norm=m_mlp_norm, m_sb_w_qkv=m_sb_w_qkv, m_sb_w_o=m_sb_w_o, m_kv_norm=m_kv_norm, m_mla_w_dkv=m_mla_w_dkv, m_mla_kv_lat_norm=m_mla_kv_lat_norm, m_mla_w_ukv=m_mla_w_ukv, m_mla_w_dq=m_mla_w_dq, m_mla_q_lat_norm=m_mla_q_lat_norm, m_mla_w_uq=m_mla_w_uq, m_mla_w_o=m_mla_w_o, m_mlp_w1=m_mlp_w1, m_mlp_w2=m_mlp_w2, m_final_norm=m_final_norm, v_attn_norm=v_attn_norm, v_mlp_norm=v_mlp_norm, v_sb_w_qkv=v_sb_w_qkv, v_sb_w_o=v_sb_w_o, v_kv_norm=v_kv_norm, v_mla_w_dkv=v_mla_w_dkv, v_mla_kv_lat_norm=v_mla_kv_lat_norm, v_mla_w_ukv=v_mla_w_ukv, v_mla_w_dq=v_mla_w_dq, v_mla_q_lat_norm=v_mla_q_lat_norm, v_mla_w_uq=v_mla_w_uq, v_mla_w_o=v_mla_w_o, v_mlp_w1=v_mlp_w1, v_mlp_w2=v_mlp_w2, v_final_norm=v_final_norm)
    weights = {n: given[n] for n in TWIN_WEIGHTS}
    shared = {n: given[n] for n in SHARED_INPUTS}
    per_example = {n: given[n] for n in ['x', 'positions']}
    grad_fn = _jax.value_and_grad(_loss, argnums=(0, 1))

    def one_microbatch(ex, loss_target):
        ex = dict(ex)
        diff = ex.pop(TWIN_DIFF_INPUT)
        return grad_fn(weights, diff, {**shared, **ex}, loss_target)

    if N_MICROBATCH == 1:
        loss, (grad_w, grad_x) = one_microbatch(per_example, given["loss_target"])
    else:
        def body(carry, xs):
            loss_sum, grad_sum = carry
            l_k, (gw_k, gx_k) = one_microbatch(xs[0], xs[1])
            with _jax.named_scope("update"):
                return (loss_sum + l_k, _jax.tree.map(_jnp.add, grad_sum, gw_k)), gx_k

        init = (_jnp.zeros((), _jnp.float32), _jax.tree.map(_jnp.zeros_like, weights))
        (loss, grad_w), grad_x = _jax.lax.scan(body, init, (per_example, given["loss_target"]))
    with _jax.named_scope("update"):
        delta_w, new_m, new_v = {}, {}, {}
        for n in TWIN_WEIGHTS:
            delta_w[n], new_m[n], new_v[n] = _adamw(weights[n], grad_w[n], given["m_" + n], given["v_" + n])
    return (loss, grad_x, *[grad_w[n] for n in TWIN_WEIGHTS], *[delta_w[n] for n in TWIN_WEIGHTS],
            *[new_m[n] for n in TWIN_WEIGHTS], *[new_v[n] for n in TWIN_WEIGHTS])
```

```python
import functools

import jax
import jax.numpy as jnp
from jax import lax
from jax.experimental import pallas as pl
from jax.experimental.pallas import tpu as pltpu

F32 = jnp.float32
BF16 = jnp.bfloat16

LANES = 128
SB_HEAD_DIM = 64
MLA_NOPE = 64
MLA_ROPE = 32
MLA_V = 64
MLA_Q_RANK = 384
MLA_KV_RANK = 256
CHUNK = 64
ROPE_THETA = 10000.0
NORM_EPS = 1e-6
SB_SCALE = SB_HEAD_DIM ** -0.5
MLA_SCALE = (MLA_NOPE + MLA_ROPE) ** -0.5
ROPE_LO = MLA_NOPE
ROPE_HALF = MLA_ROPE // 2
ATT_BLOCK = 256
NEG_BIG = -1e30
VMEM_LIMIT = 56 * 1024 * 1024

ADAM_LR = 0.001
ADAM_B1 = 0.9
ADAM_B2 = 0.999
ADAM_EPS = 1e-08
ADAM_WD = 0.01
ADAM_STEP = 10

FLAT_COLS = 1024
FLAT_ROW_BLOCK = 256
N_CHIPS = 4
MESH = pl.DeviceIdType.MESH

BIG_WEIGHTS = ["sb_w_qkv", "sb_w_o", "mla_w_dkv", "mla_w_ukv", "mla_w_dq", "mla_w_uq", "mla_w_o", "mlp_w1", "mlp_w2"]
SHARD_AXIS = {"sb_w_qkv": 2, "sb_w_o": 1, "mla_w_dkv": 0, "mla_w_ukv": 1, "mla_w_dq": 1, "mla_w_uq": 2,
              "mla_w_o": 1, "mlp_w1": 2, "mlp_w2": 1}
SMALL_WEIGHTS = ["attn_norm", "mlp_norm", "kv_norm", "mla_kv_lat_norm", "mla_q_lat_norm", "final_norm"]
ALL_WEIGHTS = ["attn_norm", "mlp_norm", "sb_w_qkv", "sb_w_o", "kv_norm", "mla_w_dkv", "mla_kv_lat_norm", "mla_w_ukv",
               "mla_w_dq", "mla_q_lat_norm", "mla_w_uq", "mla_w_o", "mlp_w1", "mlp_w2", "final_norm"]


def _dot(a, b, dims):
    return lax.dot_general(a, b, (dims, ((), ())), preferred_element_type=F32)


def _dot_nn(a, b):
    return _dot(a, b, ((1,), (0,)))


def _dot_nt(a, b):
    return _dot(a, b, ((1,), (1,)))


def _dot_tn(a, b):
    return _dot(a, b, ((0,), (0,)))


def _pick_block(n, target):
    if n <= target:
        return n
    best = max(b for b in range(LANES, target + 1, LANES) if n % b == 0)
    return best


def _mm(a, b, *, name, dims="nn", epilogue=None, extras=(), out_dtypes=(BF16,), bm=512, bn=1024, bk=512):
    if dims == "nn":
        (m, k), (k2, n) = a.shape, b.shape
    elif dims == "nt":
        (m, k), (n, k2) = a.shape, b.shape
    else:
        (k, m), (k2, n) = a.shape, b.shape
    assert k == k2, (name, a.shape, b.shape)
    bm, bn, bk = _pick_block(m, bm), _pick_block(n, bn), _pick_block(k, bk)
    nk = k // bk
    if dims == "tn":
        a_spec = pl.BlockSpec((bk, bm), lambda i, j, kk: (kk, i))
    else:
        a_spec = pl.BlockSpec((bm, bk), lambda i, j, kk: (i, kk))
    if dims == "nt":
        b_spec = pl.BlockSpec((bn, bk), lambda i, j, kk: (j, kk))
    else:
        b_spec = pl.BlockSpec((bk, bn), lambda i, j, kk: (kk, j))
    extra_specs = []
    for arr, kind in extras:
        if kind == "tile":
            assert arr.shape == (m, n), (name, arr.shape)
            extra_specs.append(pl.BlockSpec((bm, bn), lambda i, j, kk: (i, j)))
        else:
            assert arr.shape == (m, LANES), (name, arr.shape)
            extra_specs.append(pl.BlockSpec((bm, LANES), lambda i, j, kk: (i, 0)))
    n_extra = len(extras)
    n_out = len(out_dtypes)
    dot = {"nn": _dot_nn, "nt": _dot_nt, "tn": _dot_tn}[dims]

    def body(*refs):
        a_ref, b_ref = refs[0], refs[1]
        extra_refs = refs[2:2 + n_extra]
        out_refs = refs[2 + n_extra:2 + n_extra + n_out]
        acc_ref = refs[-1]
        kk = pl.program_id(2)

        @pl.when(kk == 0)
        def _():
            acc_ref[...] = jnp.zeros_like(acc_ref)

        acc_ref[...] += dot(a_ref[...].astype(BF16), b_ref[...].astype(BF16))

        @pl.when(kk == nk - 1)
        def _():
            acc = acc_ref[...]
            outs = (acc,) if epilogue is None else epilogue(acc, *[r[...] for r in extra_refs])
            for o_ref, o in zip(out_refs, outs):
                o_ref[...] = o.astype(o_ref.dtype)

    outs = pl.pallas_call(
        body, name=name, grid=(m // bm, n // bn, nk),
        in_specs=[a_spec, b_spec] + extra_specs,
        out_specs=[pl.BlockSpec((bm, bn), lambda i, j, kk: (i, j)) for _ in range(n_out)],
        out_shape=[jax.ShapeDtypeStruct((m, n), dt) for dt in out_dtypes],
        scratch_shapes=[pltpu.VMEM((bm, bn), F32)],
        compiler_params=pltpu.CompilerParams(dimension_semantics=("parallel", "parallel", "arbitrary"),
                                             vmem_limit_bytes=VMEM_LIMIT),
    )(a, b, *[arr for arr, _ in extras])
    return outs[0] if n_out == 1 else outs


def _epi_add(acc, res):
    return (res + acc,)


def _epi_relu2(acc):
    r = jnp.maximum(acc, 0.0)
    return acc, r * r


def _epi_relu2_grad(acc, u):
    return (acc * (2.0 * jnp.maximum(u.astype(F32), 0.0)),)


def _rope_slab(t, cos_t, sin_t):
    lane = lax.broadcasted_iota(jnp.int32, t.shape, 1)
    partner = jnp.where(lane < ROPE_LO + ROPE_HALF, pltpu.roll(t, LANES - ROPE_HALF, 1), pltpu.roll(t, ROPE_HALF, 1))
    return t * cos_t + partner * sin_t


def _rope_slab_bwd(d, cos_t, sin_t):
    ds = d * sin_t
    lane = lax.broadcasted_iota(jnp.int32, d.shape, 1)
    partner = jnp.where(lane < ROPE_LO + ROPE_HALF, pltpu.roll(ds, LANES - ROPE_HALF, 1), pltpu.roll(ds, ROPE_HALF, 1))
    in_rope = (lane >= ROPE_LO) & (lane < ROPE_LO + MLA_ROPE)
    return d * cos_t + jnp.where(in_rope, partner, 0.0)


def _epi_rope_heads(acc, cos_t, sin_t):
    slabs = [_rope_slab(acc[:, j * LANES:(j + 1) * LANES], cos_t, sin_t) for j in range(acc.shape[1] // LANES)]
    return (jnp.concatenate(slabs, axis=1),)


def _row_block(s):
    return min(512, s)


def _rms_fwd(x, g, *, name):
    s, d = x.shape
    bm = _row_block(s)

    def body(x_ref, g_ref, o_ref):
        xv = x_ref[...]
        r = lax.rsqrt(jnp.mean(xv * xv, axis=-1, keepdims=True) + NORM_EPS)
        o_ref[...] = ((xv * r) * g_ref[...]).astype(o_ref.dtype)

    return pl.pallas_call(
        body, name=name, grid=(s // bm,),
        in_specs=[pl.BlockSpec((bm, d), lambda i: (i, 0)), pl.BlockSpec((1, d), lambda i: (0, 0))],
        out_specs=pl.BlockSpec((bm, d), lambda i: (i, 0)),
        out_shape=jax.ShapeDtypeStruct((s, d), BF16),
        compiler_params=pltpu.CompilerParams(dimension_semantics=("parallel",), vmem_limit_bytes=VMEM_LIMIT),
    )(x, g.reshape(1, d))


def _rms_bwd_math(xv, gv, dy):
    r = lax.rsqrt(jnp.mean(xv * xv, axis=-1, keepdims=True) + NORM_EPS)
    xhat = xv * r
    dyg = dy * gv
    mdot = jnp.mean(dyg * xhat, axis=-1, keepdims=True)
    dx = r * (dyg - xhat * mdot)
    dg = jnp.sum(dy * xhat, axis=0, keepdims=True)
    return dx, dg


def _rms_bwd(x, g, dy, dres, *, name):
    s, d = x.shape
    bm = _row_block(s)
    has_res = dres is not None

    def body(*refs):
        x_ref, g_ref, dy_ref = refs[:3]
        dres_ref = refs[3] if has_res else None
        dx_ref, dxb_ref, dg_ref = refs[-3:]
        dx, dg = _rms_bwd_math(x_ref[...], g_ref[...], dy_ref[...].astype(F32))
        if has_res:
            dx = dx + dres_ref[...]
        dx_ref[...] = dx
        dxb_ref[...] = dx.astype(BF16)

        @pl.when(pl.program_id(0) == 0)
        def _():
            dg_ref[...] = jnp.zeros_like(dg_ref)

        dg_ref[...] += dg

    row = pl.BlockSpec((bm, d), lambda i: (i, 0))
    vec = pl.BlockSpec((1, d), lambda i: (0, 0))
    ins = [x, g.reshape(1, d), dy] + ([dres] if has_res else [])
    return pl.pallas_call(
        body, name=name, grid=(s // bm,),
        in_specs=[row, vec, row] + ([row] if has_res else []),
        out_specs=[row, row, vec],
        out_shape=[jax.ShapeDtypeStruct((s, d), F32), jax.ShapeDtypeStruct((s, d), BF16),
                   jax.ShapeDtypeStruct((1, d), F32)],
        compiler_params=pltpu.CompilerParams(dimension_semantics=("arbitrary",), vmem_limit_bytes=VMEM_LIMIT),
    )(*ins)


def _loss_bwd(x, g, target, *, name):
    s, d = x.shape
    bm = _row_block(s)

    def body(x_ref, g_ref, t_ref, loss_ref, dx_ref, dxb_ref, dg_ref):
        xv, gv = x_ref[...], g_ref[...]
        r = lax.rsqrt(jnp.mean(xv * xv, axis=-1, keepdims=True) + NORM_EPS)
        err = (xv * r) * gv - t_ref[...]
        dx, dg = _rms_bwd_math(xv, gv, err * (1.0 / d))
        dx_ref[...] = dx
        dxb_ref[...] = dx.astype(BF16)

        @pl.when(pl.program_id(0) == 0)
        def _():
            dg_ref[...] = jnp.zeros_like(dg_ref)
            loss_ref[...] = jnp.zeros_like(loss_ref)

        dg_ref[...] += dg
        loss_ref[...] += jnp.sum(jnp.mean(err * err, axis=-1, keepdims=True), axis=0, keepdims=True) * 0.5

    row = pl.BlockSpec((bm, d), lambda i: (i, 0))
    vec = pl.BlockSpec((1, d), lambda i: (0, 0))
    return pl.pallas_call(
        body, name=name, grid=(s // bm,),
        in_specs=[row, vec, row],
        out_specs=[pl.BlockSpec((8, LANES), lambda i: (0, 0)), row, row, vec],
        out_shape=[jax.ShapeDtypeStruct((8, LANES), F32), jax.ShapeDtypeStruct((s, d), F32),
                   jax.ShapeDtypeStruct((s, d), BF16), jax.ShapeDtypeStruct((1, d), F32)],
        compiler_params=pltpu.CompilerParams(dimension_semantics=("arbitrary",), vmem_limit_bytes=VMEM_LIMIT),
    )(x, g.reshape(1, d), target)


def _kv_prep(down, g, cos_t, sin_t, *, name):
    s, w = down.shape
    bm = _row_block(s)

    def body(d_ref, g_ref, c_ref, s_ref, o_ref):
        lat = d_ref[:, :MLA_KV_RANK]
        r = lax.rsqrt(jnp.mean(lat * lat, axis=-1, keepdims=True) + NORM_EPS)
        o_ref[:, :MLA_KV_RANK] = ((lat * r) * g_ref[...]).astype(BF16)
        o_ref[:, MLA_KV_RANK:] = _rope_slab(d_ref[:, MLA_KV_RANK:], c_ref[...], s_ref[...]).astype(BF16)

    row = pl.BlockSpec((bm, w), lambda i: (i, 0))
    tab = pl.BlockSpec((bm, LANES), lambda i: (i, 0))
    return pl.pallas_call(
        body, name=name, grid=(s // bm,),
        in_specs=[row, pl.BlockSpec((1, MLA_KV_RANK), lambda i: (0, 0)), tab, tab],
        out_specs=row, out_shape=jax.ShapeDtypeStruct((s, w), BF16),
        compiler_params=pltpu.CompilerParams(dimension_semantics=("parallel",), vmem_limit_bytes=VMEM_LIMIT),
    )(down, g.reshape(1, MLA_KV_RANK), cos_t, sin_t)


def _kv_prep_bwd(down, g, cos_t, sin_t, dcat, *, name):
    s, w = down.shape
    bm = _row_block(s)

    def body(d_ref, g_ref, c_ref, s_ref, dc_ref, o_ref, dg_ref):
        dlat, dg = _rms_bwd_math(d_ref[:, :MLA_KV_RANK], g_ref[...], dc_ref[:, :MLA_KV_RANK])
        o_ref[:, :MLA_KV_RANK] = dlat.astype(BF16)
        o_ref[:, MLA_KV_RANK:] = _rope_slab_bwd(dc_ref[:, MLA_KV_RANK:], c_ref[...], s_ref[...]).astype(BF16)

        @pl.when(pl.program_id(0) == 0)
        def _():
            dg_ref[...] = jnp.zeros_like(dg_ref)

        dg_ref[...] += dg

    row = pl.BlockSpec((bm, w), lambda i: (i, 0))
    tab = pl.BlockSpec((bm, LANES), lambda i: (i, 0))
    vec = pl.BlockSpec((1, MLA_KV_RANK), lambda i: (0, 0))
    return pl.pallas_call(
        body, name=name, grid=(s // bm,),
        in_specs=[row, vec, tab, tab, row],
        out_specs=[row, vec],
        out_shape=[jax.ShapeDtypeStruct((s, w), BF16), jax.ShapeDtypeStruct((1, MLA_KV_RANK), F32)],
        compiler_params=pltpu.CompilerParams(dimension_semantics=("arbitrary",), vmem_limit_bytes=VMEM_LIMIT),
    )(down, g.reshape(1, MLA_KV_RANK), cos_t, sin_t, dcat)


def _split_bf16(v):
    hi = v.astype(BF16)
    lo = (v - hi.astype(F32)).astype(BF16)
    return hi, lo


def _tri_masks(n):
    row = lax.broadcasted_iota(jnp.int32, (n, n), 0)
    col = lax.broadcasted_iota(jnp.int32, (n, n), 1)
    return row > col, row >= col


def _sb_scores(q, k, tri, diag):
    z = _dot_nt(q, k)
    t = jnp.exp(-jnp.abs(z))
    sp = jnp.log(1.0 + t)
    lb = jnp.minimum(z, 0.0) - sp
    lk = -jnp.maximum(z, 0.0) - sp
    if diag:
        lk = jnp.where(tri, lk, 0.0)
    return z, t, lb, lk


def _sb_fwd(qkv, heads, *, name):
    s = qkv.shape[0]
    bq = min(ATT_BLOCK, s)
    nq = s // bq

    def body(q_ref, k_ref, v_ref, o_ref, acc_ref, c_ref):
        qi = pl.program_id(1)
        q = q_ref[...] * SB_SCALE
        tri, _ = _tri_masks(bq)
        m_strict = tri.astype(BF16)
        acc_ref[...] = jnp.zeros_like(acc_ref)
        c_ref[...] = jnp.zeros_like(c_ref)

        def step(kb, diag):
            rows = pl.ds(pl.multiple_of(kb * bq, bq), bq)
            k, v = k_ref[rows, :], v_ref[rows, :]
            _, _, lb, lk = _sb_scores(q, k, tri, diag)
            hi, lo = _split_bf16(lk)
            suffix = _dot_nn(hi, m_strict) + _dot_nn(lo, m_strict)
            w = jnp.exp(lb + suffix + c_ref[...])
            if diag:
                w = jnp.where(tri, w, 0.0)
            acc_ref[...] += _dot_nn(w.astype(BF16), v)
            c_ref[...] += suffix[:, :1] + lk[:, :1]

        step(qi, True)

        def loop(i, carry):
            step(qi - 1 - i, False)
            return carry

        lax.fori_loop(0, qi, loop, 0)
        o_ref[...] = acc_ref[...].astype(o_ref.dtype)

    return pl.pallas_call(
        body, name=name, grid=(heads, nq),
        in_specs=[pl.BlockSpec((bq, LANES), lambda h, i: (i, h)),
                  pl.BlockSpec((s, LANES), lambda h, i: (0, heads + h)),
                  pl.BlockSpec((s, LANES), lambda h, i: (0, 2 * heads + h))],
        out_specs=pl.BlockSpec((bq, LANES), lambda h, i: (i, h)),
        out_shape=jax.ShapeDtypeStruct((s, heads * LANES), F32),
        scratch_shapes=[pltpu.VMEM((bq, LANES), F32), pltpu.VMEM((bq, 1), F32)],
        compiler_params=pltpu.CompilerParams(dimension_semantics=("parallel", "arbitrary"),
                                             vmem_limit_bytes=VMEM_LIMIT),
    )(qkv, qkv, qkv)


def _sb_bwd(qkv, o, do, heads, *, name):
    s = qkv.shape[0]
    bq = min(ATT_BLOCK, s)
    nq = s // bq

    def body(q_ref, k_ref, v_ref, o_ref, do_ref, dq_ref, dk_ref, dv_ref, dq_acc, dk_acc, dv_acc, c_ref, e_ref):
        qi = pl.program_id(1)

        @pl.when(qi == 0)
        def _():
            dk_acc[...] = jnp.zeros_like(dk_acc)
            dv_acc[...] = jnp.zeros_like(dv_acc)

        q = q_ref[...] * SB_SCALE
        do = do_ref[...]
        total = jnp.sum(do.astype(F32) * o_ref[...].astype(F32), axis=-1, keepdims=True)
        tri, tri_incl = _tri_masks(bq)
        m_strict = tri.astype(BF16)
        m_incl = tri_incl.astype(BF16)
        dq_acc[...] = jnp.zeros_like(dq_acc)
        c_ref[...] = jnp.zeros_like(c_ref)
        e_ref[...] = jnp.zeros_like(e_ref)

        def step(kb, diag):
            rows = pl.ds(pl.multiple_of(kb * bq, bq), bq)
            k, v = k_ref[rows, :], v_ref[rows, :]
            z, t, lb, lk = _sb_scores(q, k, tri, diag)
            hi, lo = _split_bf16(lk)
            suffix = _dot_nn(hi, m_strict) + _dot_nn(lo, m_strict)
            w = jnp.exp(lb + suffix + c_ref[...])
            if diag:
                w = jnp.where(tri, w, 0.0)
            wb = w.astype(BF16)
            g = wb.astype(F32) * _dot_nt(do, v)
            ghi, glo = _split_bf16(g)
            g_suffix = _dot_nn(ghi, m_incl) + _dot_nn(glo, m_incl) + e_ref[...]
            r = 1.0 / (1.0 + t)
            tr = t * r
            pos = z >= 0.0
            beta = jnp.where(pos, r, tr)
            one_minus_beta = jnp.where(pos, tr, r)
            da = g * one_minus_beta - beta * (total - g_suffix)
            if diag:
                da = jnp.where(tri, da, 0.0)
            dab = da.astype(BF16)
            dq_acc[...] += _dot_nn(dab, k)
            dk_acc[rows, :] += _dot_tn(dab, q)
            dv_acc[rows, :] += _dot_tn(wb, do)
            e_ref[...] = g_suffix[:, :1]
            c_ref[...] += suffix[:, :1] + lk[:, :1]

        step(qi, True)

        def loop(i, carry):
            step(qi - 1 - i, False)
            return carry

        lax.fori_loop(0, qi, loop, 0)
        dq_ref[...] = (dq_acc[...] * SB_SCALE).astype(dq_ref.dtype)

        @pl.when(qi == nq - 1)
        def _():
            dk_ref[...] = dk_acc[...].astype(dk_ref.dtype)
            dv_ref[...] = dv_acc[...].astype(dv_ref.dtype)

    blk = pl.BlockSpec((bq, LANES), lambda h, i: (i, h))
    full = pl.BlockSpec((s, LANES), lambda h, i: (0, h))
    shape = jax.ShapeDtypeStruct((s, heads * LANES), BF16)
    return pl.pallas_call(
        body, name=name, grid=(heads, nq),
        in_specs=[blk,
                  pl.BlockSpec((s, LANES), lambda h, i: (0, heads + h)),
                  pl.BlockSpec((s, LANES), lambda h, i: (0, 2 * heads + h)),
                  blk, blk],
        out_specs=[blk, full, full],
        out_shape=[shape, shape, shape],
        scratch_shapes=[pltpu.VMEM((bq, LANES), F32), pltpu.VMEM((s, LANES), F32), pltpu.VMEM((s, LANES), F32),
                        pltpu.VMEM((bq, 1), F32), pltpu.VMEM((bq, 1), F32)],
        compiler_params=pltpu.CompilerParams(dimension_semantics=("arbitrary", "arbitrary"),
                                             vmem_limit_bytes=VMEM_LIMIT),
    )(qkv, qkv, qkv, o, do)


def _chunk_mask(n):
    row = lax.broadcasted_iota(jnp.int32, (n, n), 0)
    col = lax.broadcasted_iota(jnp.int32, (n, n), 1)
    return (col // CHUNK) <= (row // CHUNK)


def _mla_fwd(q, kv, heads, *, name):
    s = q.shape[0]
    bq = min(ATT_BLOCK, s)
    nq = s // bq

    def body(q_ref, k_ref, v_ref, o_ref, lse_ref, acc_ref, m_ref, l_ref):
        qi = pl.program_id(1)
        qv = q_ref[...]
        allowed = _chunk_mask(bq)
        acc_ref[...] = jnp.zeros_like(acc_ref)
        m_ref[...] = jnp.full_like(m_ref, NEG_BIG)
        l_ref[...] = jnp.zeros_like(l_ref)

        def step(kb, diag):
            rows = pl.ds(pl.multiple_of(kb * bq, bq), bq)
            k, v = k_ref[rows, :], v_ref[rows, :]
            sc = _dot_nt(qv, k) * MLA_SCALE
            if diag:
                sc = jnp.where(allowed, sc, NEG_BIG)
            m_old = m_ref[...]
            m_new = jnp.maximum(m_old, jnp.max(sc, axis=-1, keepdims=True))
            p = jnp.exp(sc - m_new)
            alpha = jnp.exp(m_old - m_new)
            l_ref[...] = alpha * l_ref[...] + jnp.sum(p, axis=-1, keepdims=True)
            acc_ref[...] = alpha * acc_ref[...] + _dot_nn(p.astype(BF16), v)
            m_ref[...] = m_new

        step(qi, True)

        def loop(i, carry):
            step(i, False)
            return carry

        lax.fori_loop(0, qi, loop, 0)
        o_ref[...] = (acc_ref[...] / l_ref[...]).astype(o_ref.dtype)
        lse_ref[...] = jnp.broadcast_to(m_ref[...] + jnp.log(l_ref[...]), lse_ref.shape)

    blk = pl.BlockSpec((bq, LANES), lambda h, i: (i, h))
    return pl.pallas_call(
        body, name=name, grid=(heads, nq),
        in_specs=[blk,
                  pl.BlockSpec((s, LANES), lambda h, i: (0, h)),
                  pl.BlockSpec((s, LANES), lambda h, i: (0, heads + h))],
        out_specs=[blk, blk],
        out_shape=[jax.ShapeDtypeStruct((s, heads * LANES), BF16), jax.ShapeDtypeStruct((s, heads * LANES), F32)],
        scratch_shapes=[pltpu.VMEM((bq, LANES), F32), pltpu.VMEM((bq, 1), F32), pltpu.VMEM((bq, 1), F32)],
        compiler_params=pltpu.CompilerParams(dimension_semantics=("parallel", "arbitrary"),
                                             vmem_limit_bytes=VMEM_LIMIT),
    )(q, kv, kv)


def _mla_bwd(q, kv, o, do, lse, cos_t, sin_t, dkv_init, heads, *, name):
    s = q.shape[0]
    bq = min(ATT_BLOCK, s)
    nq = s // bq
    has_init = dkv_init is not None

    def body(*refs):
        q_ref, k_ref, v_ref, o_ref, do_ref, lse_ref, c_ref, s_ref = refs[:8]
        ki_ref, vi_ref = (refs[8], refs[9]) if has_init else (None, None)
        dq_ref, dk_ref, dv_ref, dq_acc, dk_acc, dv_acc = refs[-6:]
        qi = pl.program_id(1)

        @pl.when(qi == 0)
        def _():
            if has_init:
                dk_acc[...] = ki_ref[...].astype(F32)
                dv_acc[...] = vi_ref[...].astype(F32)
            else:
                dk_acc[...] = jnp.zeros_like(dk_acc)
                dv_acc[...] = jnp.zeros_like(dv_acc)

        qv = q_ref[...]
        do = do_ref[...]
        delta = jnp.sum(do.astype(F32) * o_ref[...].astype(F32), axis=-1, keepdims=True)
        lse_col = lse_ref[:, :1]
        allowed = _chunk_mask(bq)
        dq_acc[...] = jnp.zeros_like(dq_acc)

        def step(kb, diag):
            rows = pl.ds(pl.multiple_of(kb * bq, bq), bq)
            k, v = k_ref[rows, :], v_ref[rows, :]
            sc = _dot_nt(qv, k) * MLA_SCALE
            p = jnp.exp(sc - lse_col)
            if diag:
                p = jnp.where(allowed, p, 0.0)
            ds = (p * (_dot_nt(do, v) - delta) * MLA_SCALE).astype(BF16)
            dq_acc[...] += _dot_nn(ds, k)
            dk_acc[rows, :] += _dot_tn(ds, qv)
            dv_acc[rows, :] += _dot_tn(p.astype(BF16), do)

        step(qi, True)

        def loop(i, carry):
            step(i, False)
            return carry

        lax.fori_loop(0, qi, loop, 0)
        dq_ref[...] = _rope_slab_bwd(dq_acc[...], c_ref[...], s_ref[...]).astype(dq_ref.dtype)

        @pl.when(qi == nq - 1)
        def _():
            dk_ref[...] = dk_acc[...].astype(dk_ref.dtype)
            dv_ref[...] = dv_acc[...].astype(dv_ref.dtype)

    blk = pl.BlockSpec((bq, LANES), lambda h, i: (i, h))
    tab = pl.BlockSpec((bq, LANES), lambda h, i: (i, 0))
    k_full = pl.BlockSpec((s, LANES), lambda h, i: (0, h))
    v_full = pl.BlockSpec((s, LANES), lambda h, i: (0, heads + h))
    shape = jax.ShapeDtypeStruct((s, heads * LANES), BF16)
    ins = [q, kv, kv, o, do, lse, cos_t, sin_t] + ([dkv_init, dkv_init] if has_init else [])
    dq, dk, dv = pl.pallas_call(
        body, name=name, grid=(heads, nq),
        in_specs=[blk, k_full, v_full, blk, blk, blk, tab, tab] + ([k_full, v_full] if has_init else []),
        out_specs=[blk, k_full, k_full],
        out_shape=[shape, shape, shape],
        scratch_shapes=[pltpu.VMEM((bq, LANES), F32), pltpu.VMEM((s, LANES), F32), pltpu.VMEM((s, LANES), F32)],
        compiler_params=pltpu.CompilerParams(dimension_semantics=("arbitrary", "arbitrary"),
                                             vmem_limit_bytes=VMEM_LIMIT),
    )(*ins)
    return dq, jnp.concatenate([dk, dv], axis=1)


def _pad_last(a, width):
    return jnp.pad(a, [(0, 0)] * (a.ndim - 1) + [(0, width - a.shape[-1])])


def _pad_qkv(w, heads):
    d = w.shape[0]
    return _pad_last(w.reshape(d, 3 * heads, SB_HEAD_DIM), LANES).reshape(d, 3 * heads * LANES)


def _unpad_qkv(g, heads):
    d = g.shape[0]
    return g.reshape(d, 3 * heads, LANES)[:, :, :SB_HEAD_DIM].reshape(d, 3 * heads * SB_HEAD_DIM)


def _pad_o(w, heads):
    d = w.shape[1]
    w = w.reshape(heads, SB_HEAD_DIM, d)
    return jnp.pad(w, [(0, 0), (0, LANES - SB_HEAD_DIM), (0, 0)]).reshape(heads * LANES, d)


def _unpad_o(g, heads):
    d = g.shape[1]
    return g.reshape(heads, LANES, d)[:, :SB_HEAD_DIM, :].reshape(heads * SB_HEAD_DIM, d)


def _pad_uq(w, heads):
    r = w.shape[0]
    return _pad_last(w.reshape(r, heads, MLA_NOPE + MLA_ROPE), LANES).reshape(r, heads * LANES)


def _unpad_uq(g, heads):
    r = g.shape[0]
    return g.reshape(r, heads, LANES)[:, :, :MLA_NOPE + MLA_ROPE].reshape(r, heads * (MLA_NOPE + MLA_ROPE))


def _pad_dkv(w):
    d = w.shape[0]
    rope = jnp.zeros((d, LANES), w.dtype).at[:, ROPE_LO:ROPE_LO + MLA_ROPE].set(w[:, MLA_KV_RANK:])
    return jnp.concatenate([w[:, :MLA_KV_RANK], rope], axis=1)


def _unpad_dkv(g):
    return jnp.concatenate([g[:, :MLA_KV_RANK], g[:, MLA_KV_RANK + ROPE_LO:MLA_KV_RANK + ROPE_LO + MLA_ROPE]], axis=1)


def _pad_ukv(w, heads):
    w = w.reshape(MLA_KV_RANK, heads, 2, MLA_NOPE)
    k_part = _pad_last(w[:, :, 0, :], LANES).reshape(MLA_KV_RANK, heads * LANES)
    v_part = _pad_last(w[:, :, 1, :], LANES).reshape(MLA_KV_RANK, heads * LANES)
    lane = jnp.arange(LANES)
    place = ((lane[:, None] == lane[None, :]) & (lane[:, None] >= ROPE_LO) & (lane[:, None] < ROPE_LO + MLA_ROPE))
    place = jnp.tile(place.astype(w.dtype), (1, heads))
    top = jnp.concatenate([k_part, v_part], axis=1)
    bottom = jnp.concatenate([place, jnp.zeros_like(place)], axis=1)
    return jnp.concatenate([top, bottom], axis=0)


def _unpad_ukv(g, heads):
    g = g[:MLA_KV_RANK]
    k_part = g[:, :heads * LANES].reshape(MLA_KV_RANK, heads, LANES)[:, :, :MLA_NOPE]
    v_part = g[:, heads * LANES:].reshape(MLA_KV_RANK, heads, LANES)[:, :, :MLA_V]
    return jnp.stack([k_part, v_part], axis=2).reshape(MLA_KV_RANK, heads * (MLA_NOPE + MLA_V))


def _rope_tables(positions):
    inv_freq = ROPE_THETA ** (-jnp.arange(0, MLA_ROPE, 2, dtype=F32) / MLA_ROPE)
    ang = positions.astype(F32)[:, None] * inv_freq
    cos, sin = jnp.cos(ang), jnp.sin(ang)
    s = positions.shape[0]
    cos_t = jnp.ones((s, LANES), F32).at[:, ROPE_LO:ROPE_LO + MLA_ROPE].set(jnp.concatenate([cos, cos], axis=1))
    sin_t = jnp.zeros((s, LANES), F32).at[:, ROPE_LO:ROPE_LO + MLA_ROPE].set(jnp.concatenate([-sin, sin], axis=1))
    return cos_t, sin_t


def _local_step(x, positions, target, w, norms):
    s, d = x.shape
    heads = d // SB_HEAD_DIM
    n_a = w["sb_w_qkv"].shape[0]
    n_b = w["mla_w_dq"].shape[0]
    depth = n_a + n_b
    cos_t, sin_t = _rope_tables(positions)

    wqkv = [_pad_qkv(w["sb_w_qkv"][l], heads) for l in range(n_a)]
    wo_a = [_pad_o(w["sb_w_o"][l], heads) for l in range(n_a)]
    wdkv = _pad_dkv(w["mla_w_dkv"])
    wkv = _pad_ukv(w["mla_w_ukv"], heads)
    wdq = [w["mla_w_dq"][j] for j in range(n_b)]
    wuq = [_pad_uq(w["mla_w_uq"][j], heads) for j in range(n_b)]
    wo_b = [_pad_o(w["mla_w_o"][j], heads) for j in range(n_b)]
    w1 = [w["mlp_w1"][l] for l in range(depth)]
    w2 = [w["mlp_w2"][l] for l in range(depth)]

    saved = []
    kv_saved = None
    kv = None
    for l in range(depth):
        t = f"l{l}"
        sv = {"x_in": x}
        h = _rms_fwd(x, norms["attn_norm"][l], name=f"{t}_attn_norm")
        sv["h"] = h
        if l < n_a:
            qkv = _mm(h, wqkv[l], name=f"{t}_qkv")
            o = _sb_fwd(qkv, heads, name=f"{t}_sb_fwd")
            sv["qkv"], sv["o"] = qkv, o
            x = _mm(o, wo_a[l], name=f"{t}_attn_out", epilogue=_epi_add, extras=[(x, "tile")], out_dtypes=(F32,))
        else:
            j = l - n_a
            if j == 0:
                hk = _rms_fwd(x, norms["kv_norm"], name="kv_norm")
                down = _mm(hk, wdkv, name="kv_down", out_dtypes=(F32,))
                cat = _kv_prep(down, norms["mla_kv_lat_norm"], cos_t, sin_t, name="kv_prep")
                kv = _mm(cat, wkv, name="kv_up")
                kv_saved = {"x_in": x, "hk": hk, "down": down, "cat": cat}
            cq0 = _mm(h, wdq[j], name=f"{t}_q_down", out_dtypes=(F32,))
            cq = _rms_fwd(cq0, norms["mla_q_lat_norm"][j], name=f"{t}_q_lat_norm")
            q = _mm(cq, wuq[j], name=f"{t}_q_up", epilogue=_epi_rope_heads, extras=[(cos_t, "row"), (sin_t, "row")])
            o, lse = _mla_fwd(q, kv, heads, name=f"{t}_mla_fwd")
            sv.update(cq0=cq0, cq=cq, q=q, o=o, lse=lse)
            x = _mm(o, wo_b[j], name=f"{t}_attn_out", epilogue=_epi_add, extras=[(x, "tile")], out_dtypes=(F32,))
        sv["x_mid"] = x
        h2 = _rms_fwd(x, norms["mlp_norm"][l], name=f"{t}_mlp_norm")
        u, a = _mm(h2, w1[l], name=f"{t}_mlp_up", epilogue=_epi_relu2, out_dtypes=(BF16, BF16))
        sv.update(h2=h2, u=u, a=a)
        x = _mm(a, w2[l], name=f"{t}_mlp_down", epilogue=_epi_add, extras=[(x, "tile")], out_dtypes=(F32,))
        saved.append(sv)

    loss_slab, dx, dxb, dg_final = _loss_bwd(x, norms["final_norm"], target, name="loss")
    loss = loss_slab[0, 0]

    g_attn_norm, g_mlp_norm = [None] * depth, [None] * depth
    g_qkv, g_o_a = [None] * n_a, [None] * n_a
    g_dq, g_uq, g_o_b, g_qlat = [None] * n_b, [None] * n_b, [None] * n_b, [None] * n_b
    g_w1, g_w2 = [None] * depth, [None] * depth
    dkv = None
    g_kv_norm = g_kv_lat = g_dkv = g_ukv = None

    for l in reversed(range(depth)):
        t = f"l{l}"
        sv = saved[l]
        du = _mm(dxb, w2[l], name=f"{t}_mlp_down_dx", dims="nt", epilogue=_epi_relu2_grad, extras=[(sv["u"], "tile")])
        g_w2[l] = _mm(sv["a"], dxb, name=f"{t}_mlp_down_dw", dims="tn", out_dtypes=(F32,))
        g_w1[l] = _mm(sv["h2"], du, name=f"{t}_mlp_up_dw", dims="tn", out_dtypes=(F32,))
        dh2 = _mm(du, w1[l], name=f"{t}_mlp_up_dx", dims="nt", out_dtypes=(F32,))
        dx, dxb, g_mlp_norm[l] = _rms_bwd(sv["x_mid"], norms["mlp_norm"][l], dh2, dx, name=f"{t}_mlp_norm_bwd")
        if l < n_a:
            do = _mm(dxb, wo_a[l], name=f"{t}_attn_out_dx", dims="nt")
            g_o_a[l] = _unpad_o(_mm(sv["o"], dxb, name=f"{t}_attn_out_dw", dims="tn", out_dtypes=(F32,)), heads)
            dq, dk, dv = _sb_bwd(sv["qkv"], sv["o"], do, heads, name=f"{t}_sb_bwd")
            dqkv = jnp.concatenate([dq, dk, dv], axis=1)
            g_qkv[l] = _unpad_qkv(_mm(sv["h"], dqkv, name=f"{t}_qkv_dw", dims="tn", out_dtypes=(F32,)), heads)
            dh = _mm(dqkv, wqkv[l], name=f"{t}_qkv_dx", dims="nt", out_dtypes=(F32,))
        else:
            j = l - n_a
            do = _mm(dxb, wo_b[j], name=f"{t}_attn_out_dx", dims="nt")
            g_o_b[j] = _unpad_o(_mm(sv["o"], dxb, name=f"{t}_attn_out_dw", dims="tn", out_dtypes=(F32,)), heads)
            dq, dkv = _mla_bwd(sv["q"], kv, sv["o"], do, sv["lse"], cos_t, sin_t, dkv, heads, name=f"{t}_mla_bwd")
            g_uq[j] = _unpad_uq(_mm(sv["cq"], dq, name=f"{t}_q_up_dw", dims="tn", out_dtypes=(F32,)), heads)
            dcq = _mm(dq, wuq[j], name=f"{t}_q_up_dx", dims="nt", out_dtypes=(F32,))
            _, dcq0, g_qlat[j] = _rms_bwd(sv["cq0"], norms["mla_q_lat_norm"][j], dcq, None, name=f"{t}_q_lat_norm_bwd")
            g_dq[j] = _mm(sv["h"], dcq0, name=f"{t}_q_down_dw", dims="tn", out_dtypes=(F32,))
            dh = _mm(dcq0, wdq[j], name=f"{t}_q_down_dx", dims="nt", out_dtypes=(F32,))
        dx, dxb, g_attn_norm[l] = _rms_bwd(sv["x_in"], norms["attn_norm"][l], dh, dx, name=f"{t}_attn_norm_bwd")
        if l == n_a:
            ks = kv_saved
            dcat = _mm(dkv, wkv, name="kv_up_dx", dims="nt", out_dtypes=(F32,))
            g_ukv = _unpad_ukv(_mm(ks["cat"], dkv, name="kv_up_dw", dims="tn", out_dtypes=(F32,)), heads)
            ddown, g_kv_lat = _kv_prep_bwd(ks["down"], norms["mla_kv_lat_norm"], cos_t, sin_t, dcat, name="kv_prep_bwd")
            g_dkv = _unpad_dkv(_mm(ks["hk"], ddown, name="kv_down_dw", dims="tn", out_dtypes=(F32,)))
            dhk = _mm(ddown, wdkv, name="kv_down_dx", dims="nt", out_dtypes=(F32,))
            dx, dxb, g_kv_norm = _rms_bwd(ks["x_in"], norms["kv_norm"], dhk, dx, name="kv_norm_bwd")

    grads = {
        "attn_norm": jnp.concatenate(g_attn_norm, axis=0), "mlp_norm": jnp.concatenate(g_mlp_norm, axis=0),
        "sb_w_qkv": jnp.stack(g_qkv), "sb_w_o": jnp.stack(g_o_a),
        "kv_norm": g_kv_norm[0], "mla_w_dkv": g_dkv, "mla_kv_lat_norm": g_kv_lat[0], "mla_w_ukv": g_ukv,
        "mla_w_dq": jnp.stack(g_dq), "mla_q_lat_norm": jnp.concatenate(g_qlat, axis=0),
        "mla_w_uq": jnp.stack(g_uq), "mla_w_o": jnp.stack(g_o_b),
        "mlp_w1": jnp.stack(g_w1), "mlp_w2": jnp.stack(g_w2), "final_norm": dg_final[0],
    }
    return loss, dx, grads


def _flat_rows(n_elems):
    per_block = FLAT_COLS * FLAT_ROW_BLOCK
    return -(-n_elems // per_block) * FLAT_ROW_BLOCK


def _pack(arrays, dtype):
    flat = jnp.concatenate([a.reshape(-1).astype(dtype) for a in arrays])
    rows = _flat_rows(flat.shape[0])
    flat = jnp.pad(flat, (0, rows * FLAT_COLS - flat.shape[0]))
    return flat.reshape(rows, FLAT_COLS)


def _unpack(flat, shapes):
    flat = flat.reshape(-1)
    out, off = [], 0
    for shp in shapes:
        n = 1
        for v in shp:
            n *= v
        out.append(flat[off:off + n].reshape(shp))
        off += n
    return out


def _pack_small(arrays):
    rows = []
    for a in arrays:
        a = a.reshape(-1, a.shape[-1]) if a.shape[-1] == FLAT_COLS else a.reshape(1, -1)
        rows.append(_pad_last(a, FLAT_COLS))
    flat = jnp.concatenate(rows, axis=0)
    return jnp.pad(flat, [(0, -flat.shape[0] % 8), (0, 0)])


def _unpack_small(flat, shapes):
    out, row = [], 0
    for shp in shapes:
        if shp[-1] == FLAT_COLS:
            n = 1
            for v in shp[:-1]:
                n *= v
            out.append(flat[row:row + n].reshape(shp))
            row += n
        else:
            n = 1
            for v in shp:
                n *= v
            out.append(flat[row, :n].reshape(shp))
            row += 1
    return out


def _other_chips(x, y):
    return [(1 - x, y), (x, 1 - y), (1 - x, 1 - y)]


def _all_gather_chips(flat, *, name):
    rows, cols = flat.shape

    def body(x_ref, out_ref, send_sems, recv_sems, local_sem):
        x, y, c = lax.axis_index("x"), lax.axis_index("y"), lax.axis_index("c")
        me = 2 * x + y
        mine = pltpu.make_async_copy(x_ref, out_ref.at[me], local_sem)
        mine.start()
        sends = []
        for k, (px, py) in enumerate(_other_chips(x, y)):
            cp = pltpu.make_async_remote_copy(src_ref=x_ref, dst_ref=out_ref.at[me], send_sem=send_sems.at[k],
                                              recv_sem=recv_sems.at[k], device_id=(px, py, c), device_id_type=MESH)
            cp.start()
            sends.append(cp)
        for k, (px, py) in enumerate(_other_chips(x, y)):
            pltpu.make_async_remote_copy(src_ref=x_ref, dst_ref=out_ref.at[2 * px + py], send_sem=send_sems.at[k],
                                         recv_sem=recv_sems.at[k], device_id=(px, py, c),
                                         device_id_type=MESH).wait_recv()
        for cp in sends:
            cp.wait_send()
        mine.wait()

    return pl.pallas_call(
        body, name=name,
        in_specs=[pl.BlockSpec(memory_space=pltpu.HBM)],
        out_specs=pl.BlockSpec(memory_space=pltpu.HBM),
        out_shape=jax.ShapeDtypeStruct((N_CHIPS, rows, cols), flat.dtype),
        scratch_shapes=[pltpu.SemaphoreType.DMA((3,)), pltpu.SemaphoreType.DMA((3,)), pltpu.SemaphoreType.DMA],
        compiler_params=pltpu.CompilerParams(has_side_effects=True),
    )(flat)


def _exchange_chips(parts, *, name):
    def body(g_ref, out_ref, send_sems, recv_sems, local_sem):
        x, y, c = lax.axis_index("x"), lax.axis_index("y"), lax.axis_index("c")
        me = 2 * x + y
        mine = pltpu.make_async_copy(g_ref.at[me], out_ref.at[me], local_sem)
        mine.start()
        sends = []
        for k, (px, py) in enumerate(_other_chips(x, y)):
            cp = pltpu.make_async_remote_copy(src_ref=g_ref.at[2 * px + py], dst_ref=out_ref.at[me],
                                              send_sem=send_sems.at[k], recv_sem=recv_sems.at[k],
                                              device_id=(px, py, c), device_id_type=MESH)
            cp.start()
            sends.append(cp)
        for k, (px, py) in enumerate(_other_chips(x, y)):
            pltpu.make_async_remote_copy(src_ref=g_ref.at[me], dst_ref=out_ref.at[2 * px + py],
                                         send_sem=send_sems.at[k], recv_sem=recv_sems.at[k],
                                         device_id=(px, py, c), device_id_type=MESH).wait_recv()
        for cp in sends:
            cp.wait_send()
        mine.wait()

    return pl.pallas_call(
        body, name=name,
        in_specs=[pl.BlockSpec(memory_space=pltpu.HBM)],
        out_specs=pl.BlockSpec(memory_space=pltpu.HBM),
        out_shape=jax.ShapeDtypeStruct(parts.shape, parts.dtype),
        scratch_shapes=[pltpu.SemaphoreType.DMA((3,)), pltpu.SemaphoreType.DMA((3,)), pltpu.SemaphoreType.DMA],
        compiler_params=pltpu.CompilerParams(has_side_effects=True),
    )(parts)


def _swap_cores(v, *, name):
    def body(v_ref, out_ref, send_sem, recv_sem):
        peer = (lax.axis_index("x"), lax.axis_index("y"), 1 - lax.axis_index("c"))
        cp = pltpu.make_async_remote_copy(src_ref=v_ref, dst_ref=out_ref, send_sem=send_sem, recv_sem=recv_sem,
                                          device_id=peer, device_id_type=MESH)
        cp.start()
        cp.wait()

    return pl.pallas_call(
        body, name=name,
        in_specs=[pl.BlockSpec(memory_space=pltpu.HBM)],
        out_specs=pl.BlockSpec(memory_space=pltpu.HBM),
        out_shape=jax.ShapeDtypeStruct(v.shape, v.dtype),
        scratch_shapes=[pltpu.SemaphoreType.DMA, pltpu.SemaphoreType.DMA],
        compiler_params=pltpu.CompilerParams(has_side_effects=True),
    )(v)


def _sum_chips(parts, *, name):
    _, rows, cols = parts.shape

    def body(p_ref, o_ref):
        o_ref[...] = ((p_ref[0] + p_ref[1]) + p_ref[2]) + p_ref[3]

    return pl.pallas_call(
        body, name=name, grid=(rows // FLAT_ROW_BLOCK,),
        in_specs=[pl.BlockSpec((N_CHIPS, FLAT_ROW_BLOCK, cols), lambda i: (0, i, 0))],
        out_specs=pl.BlockSpec((FLAT_ROW_BLOCK, cols), lambda i: (i, 0)),
        out_shape=jax.ShapeDtypeStruct((rows, cols), parts.dtype),
        compiler_params=pltpu.CompilerParams(dimension_semantics=("parallel",), vmem_limit_bytes=VMEM_LIMIT),
    )(parts)


def _all_reduce_small(v, *, name):
    rows, cols = v.shape
    flips = [(fx, fy, fc) for fx in (0, 1) for fy in (0, 1) for fc in (0, 1)][1:]

    def body(v_ref, out_ref, gath_ref, send_sems, recv_sems):
        x, y, c = lax.axis_index("x"), lax.axis_index("y"), lax.axis_index("c")
        me = 4 * x + 2 * y + c
        gath_ref[me] = v_ref[...]
        peers = [((1 - x) if fx else x, (1 - y) if fy else y, (1 - c) if fc else c) for fx, fy, fc in flips]
        sends = []
        for k, peer in enumerate(peers):
            cp = pltpu.make_async_remote_copy(src_ref=v_ref, dst_ref=gath_ref.at[me], send_sem=send_sems.at[k],
                                              recv_sem=recv_sems.at[k], device_id=peer, device_id_type=MESH)
            cp.start()
            sends.append(cp)
        for k, (px, py, pc) in enumerate(peers):
            pltpu.make_async_remote_copy(src_ref=v_ref, dst_ref=gath_ref.at[4 * px + 2 * py + pc],
                                         send_sem=send_sems.at[k], recv_sem=recv_sems.at[k],
                                         device_id=(px, py, pc), device_id_type=MESH).wait_recv()
        for cp in sends:
            cp.wait_send()
        total = gath_ref[0]
        for k in range(1, 8):
            total = total + gath_ref[k]
        out_ref[...] = total

    total, _ = pl.pallas_call(
        body, name=name,
        in_specs=[pl.BlockSpec(memory_space=pltpu.VMEM)],
        out_specs=[pl.BlockSpec(memory_space=pltpu.VMEM), pl.BlockSpec(memory_space=pltpu.VMEM)],
        out_shape=[jax.ShapeDtypeStruct((rows, cols), v.dtype), jax.ShapeDtypeStruct((8, rows, cols), v.dtype)],
        scratch_shapes=[pltpu.SemaphoreType.DMA((7,)), pltpu.SemaphoreType.DMA((7,))],
        compiler_params=pltpu.CompilerParams(has_side_effects=True),
    )(v)
    return total


def _adamw(w, g_parts, m, v, *, name):
    rows, cols = w.shape
    br = min(FLAT_ROW_BLOCK, rows)
    n_parts = len(g_parts)

    def body(*refs):
        w_ref = refs[0]
        g_refs = refs[1:1 + n_parts]
        m_ref, v_ref = refs[1 + n_parts], refs[2 + n_parts]
        g_out, d_out, m_out, v_out = refs[-4:]
        g = g_refs[0][...]
        for r in g_refs[1:]:
            g = g + r[...]
        m_new = ADAM_B1 * m_ref[...] + (1.0 - ADAM_B1) * g
        v_new = ADAM_B2 * v_ref[...] + (1.0 - ADAM_B2) * jnp.square(g)
        m_hat = m_new / (1.0 - ADAM_B1 ** ADAM_STEP)
        v_hat = v_new / (1.0 - ADAM_B2 ** ADAM_STEP)
        g_out[...] = g
        d_out[...] = -ADAM_LR * (m_hat / (jnp.sqrt(v_hat) + ADAM_EPS) + ADAM_WD * w_ref[...])
        m_out[...] = m_new
        v_out[...] = v_new

    blk = pl.BlockSpec((br, cols), lambda i: (i, 0))
    shape = jax.ShapeDtypeStruct((rows, cols), F32)
    return pl.pallas_call(
        body, name=name, grid=(rows // br,),
        in_specs=[blk] * (3 + n_parts), out_specs=[blk] * 4, out_shape=[shape] * 4,
        compiler_params=pltpu.CompilerParams(dimension_semantics=("parallel",), vmem_limit_bytes=VMEM_LIMIT),
    )(w, *g_parts, m, v)


def _assemble(gathered_shards, name):
    return jnp.concatenate(gathered_shards, axis=SHARD_AXIS[name])


def _chip_shard(full, name, j):
    axis = SHARD_AXIS[name]
    n = full.shape[axis] // N_CHIPS
    return lax.slice_in_dim(full, j * n, (j + 1) * n, axis=axis)


def kernel(x, positions, attn_norm, mlp_norm, sb_w_qkv, sb_w_o, kv_norm, mla_w_dkv, mla_kv_lat_norm, mla_w_ukv, mla_w_dq, mla_q_lat_norm, mla_w_uq, mla_w_o, mlp_w1, mlp_w2, final_norm, loss_target, m_attn_norm, m_mlp_norm, m_sb_w_qkv, m_sb_w_o, m_kv_norm, m_mla_w_dkv, m_mla_kv_lat_norm, m_mla_w_ukv, m_mla_w_dq, m_mla_q_lat_norm, m_mla_w_uq, m_mla_w_o, m_mlp_w1, m_mlp_w2, m_final_norm, v_attn_norm, v_mlp_norm, v_sb_w_qkv, v_sb_w_o, v_kv_norm, v_mla_w_dkv, v_mla_kv_lat_norm, v_mla_w_ukv, v_mla_w_dq, v_mla_q_lat_norm, v_mla_w_uq, v_mla_w_o, v_mlp_w1, v_mlp_w2, v_final_norm):
    weights = dict(attn_norm=attn_norm, mlp_norm=mlp_norm, sb_w_qkv=sb_w_qkv, sb_w_o=sb_w_o, kv_norm=kv_norm,
                   mla_w_dkv=mla_w_dkv, mla_kv_lat_norm=mla_kv_lat_norm, mla_w_ukv=mla_w_ukv, mla_w_dq=mla_w_dq,
                   mla_q_lat_norm=mla_q_lat_norm, mla_w_uq=mla_w_uq, mla_w_o=mla_w_o, mlp_w1=mlp_w1, mlp_w2=mlp_w2,
                   final_norm=final_norm)
    m_in = dict(attn_norm=m_attn_norm, mlp_norm=m_mlp_norm, sb_w_qkv=m_sb_w_qkv, sb_w_o=m_sb_w_o, kv_norm=m_kv_norm,
                mla_w_dkv=m_mla_w_dkv, mla_kv_lat_norm=m_mla_kv_lat_norm, mla_w_ukv=m_mla_w_ukv, mla_w_dq=m_mla_w_dq,
                mla_q_lat_norm=m_mla_q_lat_norm, mla_w_uq=m_mla_w_uq, mla_w_o=m_mla_w_o, mlp_w1=m_mlp_w1,
                mlp_w2=m_mlp_w2, final_norm=m_final_norm)
    v_in = dict(attn_norm=v_attn_norm, mlp_norm=v_mlp_norm, sb_w_qkv=v_sb_w_qkv, sb_w_o=v_sb_w_o, kv_norm=v_kv_norm,
                mla_w_dkv=v_mla_w_dkv, mla_kv_lat_norm=v_mla_kv_lat_norm, mla_w_ukv=v_mla_w_ukv, mla_w_dq=v_mla_w_dq,
                mla_q_lat_norm=v_mla_q_lat_norm, mla_w_uq=v_mla_w_uq, mla_w_o=v_mla_w_o, mlp_w1=v_mlp_w1,
                mlp_w2=v_mlp_w2, final_norm=v_final_norm)
    shard_shapes = [weights[n].shape for n in BIG_WEIGHTS]
    small_shapes = [weights[n].shape for n in SMALL_WEIGHTS]

    gathered = _all_gather_chips(_pack([weights[n] for n in BIG_WEIGHTS], BF16), name="weights_all_gather")
    per_chip = [_unpack(gathered[j], shard_shapes) for j in range(N_CHIPS)]
    full_w = {n: _assemble([per_chip[j][i] for j in range(N_CHIPS)], n) for i, n in enumerate(BIG_WEIGHTS)}
    norms = {n: weights[n] for n in SMALL_WEIGHTS}

    loss, dx, grads = _local_step(x[0], positions[0], loss_target[0], full_w, norms)
    loss = lax.psum(loss, ("x", "y", "c"))

    parts = jnp.stack([_pack([_chip_shard(grads[n], n, j) for n in BIG_WEIGHTS], F32) for j in range(N_CHIPS)])
    received = _exchange_chips(parts, name="grads_exchange")
    core_sum = _sum_chips(received, name="grads_sum_chips")
    other_sum = _swap_cores(core_sum, name="grads_swap_cores")
    g_flat, d_flat, m_flat, v_flat = _adamw(
        _pack([weights[n] for n in BIG_WEIGHTS], F32), [core_sum, other_sum],
        _pack([m_in[n] for n in BIG_WEIGHTS], F32), _pack([v_in[n] for n in BIG_WEIGHTS], F32), name="adamw_big")
    out_g = dict(zip(BIG_WEIGHTS, _unpack(g_flat, shard_shapes)))
    out_d = dict(zip(BIG_WEIGHTS, _unpack(d_flat, shard_shapes)))
    out_m = dict(zip(BIG_WEIGHTS, _unpack(m_flat, shard_shapes)))
    out_v = dict(zip(BIG_WEIGHTS, _unpack(v_flat, shard_shapes)))

    small_sum = _all_reduce_small(_pack_small([grads[n] for n in SMALL_WEIGHTS]), name="gains_all_reduce")
    sg, sd, sm, sv = _adamw(_pack_small([weights[n] for n in SMALL_WEIGHTS]), [small_sum],
                            _pack_small([m_in[n] for n in SMALL_WEIGHTS]),
                            _pack_small([v_in[n] for n in SMALL_WEIGHTS]), name="adamw_gains")
    out_g.update(zip(SMALL_WEIGHTS, _unpack_small(sg, small_shapes)))
    out_d.update(zip(SMALL_WEIGHTS, _unpack_small(sd, small_shapes)))
    out_m.update(zip(SMALL_WEIGHTS, _unpack_small(sm, small_shapes)))
    out_v.update(zip(SMALL_WEIGHTS, _unpack_small(sv, small_shapes)))

    return (loss, dx[None], *[out_g[n] for n in ALL_WEIGHTS], *[out_d[n] for n in ALL_WEIGHTS],
            *[out_m[n] for n in ALL_WEIGHTS], *[out_v[n] for n in ALL_WEIGHTS])
```

```python
import functools

import jax
import jax.numpy as jnp
from jax import lax
from jax.experimental import pallas as pl
from jax.experimental.pallas import tpu as pltpu

F32 = jnp.float32
BF16 = jnp.bfloat16

LANES = 128
SB_HEAD_DIM = 64
MLA_NOPE = 64
MLA_ROPE = 32
MLA_V = 64
MLA_Q_RANK = 384
MLA_KV_RANK = 256
CHUNK = 64
ROPE_THETA = 10000.0
NORM_EPS = 1e-6
SB_SCALE = SB_HEAD_DIM ** -0.5
MLA_SCALE = (MLA_NOPE + MLA_ROPE) ** -0.5
ROPE_LO = MLA_NOPE
ROPE_HALF = MLA_ROPE // 2
ATT_Q_BLOCK = 1024
ATT_K_BLOCK = 256
NEG_BIG = -1e30
VMEM_LIMIT = 56 * 1024 * 1024

ADAM_LR = 0.001
ADAM_B1 = 0.9
ADAM_B2 = 0.999
ADAM_EPS = 1e-08
ADAM_WD = 0.01
ADAM_STEP = 10

FLAT_COLS = 1024
FLAT_ROW_BLOCK = 256
N_CHIPS = 4
MESH = pl.DeviceIdType.MESH

BIG_WEIGHTS = ["sb_w_qkv", "sb_w_o", "mla_w_dkv", "mla_w_ukv", "mla_w_dq", "mla_w_uq", "mla_w_o", "mlp_w1", "mlp_w2"]
SHARD_AXIS = {"sb_w_qkv": 2, "sb_w_o": 1, "mla_w_dkv": 0, "mla_w_ukv": 1, "mla_w_dq": 1, "mla_w_uq": 2,
              "mla_w_o": 1, "mlp_w1": 2, "mlp_w2": 1}
SMALL_WEIGHTS = ["attn_norm", "mlp_norm", "kv_norm", "mla_kv_lat_norm", "mla_q_lat_norm", "final_norm"]
ALL_WEIGHTS = ["attn_norm", "mlp_norm", "sb_w_qkv", "sb_w_o", "kv_norm", "mla_w_dkv", "mla_kv_lat_norm", "mla_w_ukv",
               "mla_w_dq", "mla_q_lat_norm", "mla_w_uq", "mla_w_o", "mlp_w1", "mlp_w2", "final_norm"]


def _dot(a, b, dims):
    return lax.dot_general(a, b, (dims, ((), ())), preferred_element_type=F32)


def _dot_nn(a, b):
    return _dot(a, b, ((1,), (0,)))


def _dot_nt(a, b):
    return _dot(a, b, ((1,), (1,)))


def _dot_tn(a, b):
    return _dot(a, b, ((0,), (0,)))


def _pick_block(n, target):
    if n <= target:
        return n
    best = max(b for b in range(LANES, target + 1, LANES) if n % b == 0)
    return best


def _mm(a, b, *, name, dims="nn", epilogue=None, extras=(), out_dtypes=(BF16,), bm=512, bn=1024, bk=512):
    if dims == "nn":
        (m, k), (k2, n) = a.shape, b.shape
    elif dims == "nt":
        (m, k), (n, k2) = a.shape, b.shape
    else:
        (k, m), (k2, n) = a.shape, b.shape
    assert k == k2, (name, a.shape, b.shape)
    bm, bn, bk = _pick_block(m, bm), _pick_block(n, bn), _pick_block(k, bk)
    nk = k // bk
    if dims == "tn":
        a_spec = pl.BlockSpec((bk, bm), lambda i, j, kk: (kk, i))
    else:
        a_spec = pl.BlockSpec((bm, bk), lambda i, j, kk: (i, kk))
    if dims == "nt":
        b_spec = pl.BlockSpec((bn, bk), lambda i, j, kk: (j, kk))
    else:
        b_spec = pl.BlockSpec((bk, bn), lambda i, j, kk: (kk, j))
    extra_specs = []
    for arr, kind in extras:
        if kind == "tile":
            assert arr.shape == (m, n), (name, arr.shape)
            extra_specs.append(pl.BlockSpec((bm, bn), lambda i, j, kk: (i, j)))
        else:
            assert arr.shape == (m, LANES), (name, arr.shape)
            extra_specs.append(pl.BlockSpec((bm, LANES), lambda i, j, kk: (i, 0)))
    n_extra = len(extras)
    n_out = len(out_dtypes)
    dot = {"nn": _dot_nn, "nt": _dot_nt, "tn": _dot_tn}[dims]

    def body(*refs):
        a_ref, b_ref = refs[0], refs[1]
        extra_refs = refs[2:2 + n_extra]
        out_refs = refs[2 + n_extra:2 + n_extra + n_out]
        acc_ref = refs[-1]
        kk = pl.program_id(2)

        @pl.when(kk == 0)
        def _():
            acc_ref[...] = jnp.zeros_like(acc_ref)

        acc_ref[...] += dot(a_ref[...].astype(BF16), b_ref[...].astype(BF16))

        @pl.when(kk == nk - 1)
        def _():
            acc = acc_ref[...]
            outs = (acc,) if epilogue is None else epilogue(acc, *[r[...] for r in extra_refs])
            for o_ref, o in zip(out_refs, outs):
                o_ref[...] = o.astype(o_ref.dtype)

    outs = pl.pallas_call(
        body, name=name, grid=(m // bm, n // bn, nk),
        in_specs=[a_spec, b_spec] + extra_specs,
        out_specs=[pl.BlockSpec((bm, bn), lambda i, j, kk: (i, j)) for _ in range(n_out)],
        out_shape=[jax.ShapeDtypeStruct((m, n), dt) for dt in out_dtypes],
        scratch_shapes=[pltpu.VMEM((bm, bn), F32)],
        compiler_params=pltpu.CompilerParams(dimension_semantics=("parallel", "parallel", "arbitrary"),
                                             vmem_limit_bytes=VMEM_LIMIT),
    )(a, b, *[arr for arr, _ in extras])
    return outs[0] if n_out == 1 else outs


def _epi_add(acc, res):
    return (res + acc,)


def _epi_relu2(acc):
    r = jnp.maximum(acc, 0.0)
    return acc, r * r


def _epi_relu2_grad(acc, u):
    return (acc * (2.0 * jnp.maximum(u.astype(F32), 0.0)),)


def _rope_slab(t, cos_t, sin_t):
    lane = lax.broadcasted_iota(jnp.int32, t.shape, 1)
    partner = jnp.where(lane < ROPE_LO + ROPE_HALF, pltpu.roll(t, LANES - ROPE_HALF, 1), pltpu.roll(t, ROPE_HALF, 1))
    return t * cos_t + partner * sin_t


def _rope_slab_bwd(d, cos_t, sin_t):
    ds = d * sin_t
    lane = lax.broadcasted_iota(jnp.int32, d.shape, 1)
    partner = jnp.where(lane < ROPE_LO + ROPE_HALF, pltpu.roll(ds, LANES - ROPE_HALF, 1), pltpu.roll(ds, ROPE_HALF, 1))
    in_rope = (lane >= ROPE_LO) & (lane < ROPE_LO + MLA_ROPE)
    return d * cos_t + jnp.where(in_rope, partner, 0.0)


def _epi_rope_heads(acc, cos_t, sin_t):
    slabs = [_rope_slab(acc[:, j * LANES:(j + 1) * LANES], cos_t, sin_t) for j in range(acc.shape[1] // LANES)]
    return (jnp.concatenate(slabs, axis=1),)


def _row_block(s):
    return min(512, s)


def _rms_fwd(x, g, *, name):
    s, d = x.shape
    bm = _row_block(s)

    def body(x_ref, g_ref, o_ref):
        xv = x_ref[...]
        r = lax.rsqrt(jnp.mean(xv * xv, axis=-1, keepdims=True) + NORM_EPS)
        o_ref[...] = ((xv * r) * g_ref[...]).astype(o_ref.dtype)

    return pl.pallas_call(
        body, name=name, grid=(s // bm,),
        in_specs=[pl.BlockSpec((bm, d), lambda i: (i, 0)), pl.BlockSpec((1, d), lambda i: (0, 0))],
        out_specs=pl.BlockSpec((bm, d), lambda i: (i, 0)),
        out_shape=jax.ShapeDtypeStruct((s, d), BF16),
        compiler_params=pltpu.CompilerParams(dimension_semantics=("parallel",), vmem_limit_bytes=VMEM_LIMIT),
    )(x, g.reshape(1, d))


def _rms_bwd_math(xv, gv, dy):
    r = lax.rsqrt(jnp.mean(xv * xv, axis=-1, keepdims=True) + NORM_EPS)
    xhat = xv * r
    dyg = dy * gv
    mdot = jnp.mean(dyg * xhat, axis=-1, keepdims=True)
    dx = r * (dyg - xhat * mdot)
    dg = jnp.sum(dy * xhat, axis=0, keepdims=True)
    return dx, dg


def _rms_bwd(x, g, dy, dres, *, name):
    s, d = x.shape
    bm = _row_block(s)
    has_res = dres is not None

    def body(*refs):
        x_ref, g_ref, dy_ref = refs[:3]
        dres_ref = refs[3] if has_res else None
        dx_ref, dxb_ref, dg_ref = refs[-3:]
        dx, dg = _rms_bwd_math(x_ref[...], g_ref[...], dy_ref[...].astype(F32))
        if has_res:
            dx = dx + dres_ref[...]
        dx_ref[...] = dx
        dxb_ref[...] = dx.astype(BF16)

        @pl.when(pl.program_id(0) == 0)
        def _():
            dg_ref[...] = jnp.zeros_like(dg_ref)

        dg_ref[...] += dg

    row = pl.BlockSpec((bm, d), lambda i: (i, 0))
    vec = pl.BlockSpec((1, d), lambda i: (0, 0))
    ins = [x, g.reshape(1, d), dy] + ([dres] if has_res else [])
    return pl.pallas_call(
        body, name=name, grid=(s // bm,),
        in_specs=[row, vec, row] + ([row] if has_res else []),
        out_specs=[row, row, vec],
        out_shape=[jax.ShapeDtypeStruct((s, d), F32), jax.ShapeDtypeStruct((s, d), BF16),
                   jax.ShapeDtypeStruct((1, d), F32)],
        compiler_params=pltpu.CompilerParams(dimension_semantics=("arbitrary",), vmem_limit_bytes=VMEM_LIMIT),
    )(*ins)


def _loss_bwd(x, g, target, *, name):
    s, d = x.shape
    bm = _row_block(s)

    def body(x_ref, g_ref, t_ref, loss_ref, dx_ref, dxb_ref, dg_ref):
        xv, gv = x_ref[...], g_ref[...]
        r = lax.rsqrt(jnp.mean(xv * xv, axis=-1, keepdims=True) + NORM_EPS)
        err = (xv * r) * gv - t_ref[...]
        dx, dg = _rms_bwd_math(xv, gv, err * (1.0 / d))
        dx_ref[...] = dx
        dxb_ref[...] = dx.astype(BF16)

        @pl.when(pl.program_id(0) == 0)
        def _():
            dg_ref[...] = jnp.zeros_like(dg_ref)
            loss_ref[...] = jnp.zeros_like(loss_ref)

        dg_ref[...] += dg
        loss_ref[...] += jnp.sum(jnp.mean(err * err, axis=-1, keepdims=True), axis=0, keepdims=True) * 0.5

    row = pl.BlockSpec((bm, d), lambda i: (i, 0))
    vec = pl.BlockSpec((1, d), lambda i: (0, 0))
    return pl.pallas_call(
        body, name=name, grid=(s // bm,),
        in_specs=[row, vec, row],
        out_specs=[pl.BlockSpec((8, LANES), lambda i: (0, 0)), row, row, vec],
        out_shape=[jax.ShapeDtypeStruct((8, LANES), F32), jax.ShapeDtypeStruct((s, d), F32),
                   jax.ShapeDtypeStruct((s, d), BF16), jax.ShapeDtypeStruct((1, d), F32)],
        compiler_params=pltpu.CompilerParams(dimension_semantics=("arbitrary",), vmem_limit_bytes=VMEM_LIMIT),
    )(x, g.reshape(1, d), target)


def _kv_prep(down, g, cos_t, sin_t, *, name):
    s, w = down.shape
    bm = _row_block(s)

    def body(d_ref, g_ref, c_ref, s_ref, o_ref):
        lat = d_ref[:, :MLA_KV_RANK]
        r = lax.rsqrt(jnp.mean(lat * lat, axis=-1, keepdims=True) + NORM_EPS)
        o_ref[:, :MLA_KV_RANK] = ((lat * r) * g_ref[...]).astype(BF16)
        o_ref[:, MLA_KV_RANK:] = _rope_slab(d_ref[:, MLA_KV_RANK:], c_ref[...], s_ref[...]).astype(BF16)

    row = pl.BlockSpec((bm, w), lambda i: (i, 0))
    tab = pl.BlockSpec((bm, LANES), lambda i: (i, 0))
    return pl.pallas_call(
        body, name=name, grid=(s // bm,),
        in_specs=[row, pl.BlockSpec((1, MLA_KV_RANK), lambda i: (0, 0)), tab, tab],
        out_specs=row, out_shape=jax.ShapeDtypeStruct((s, w), BF16),
        compiler_params=pltpu.CompilerParams(dimension_semantics=("parallel",), vmem_limit_bytes=VMEM_LIMIT),
    )(down, g.reshape(1, MLA_KV_RANK), cos_t, sin_t)


def _kv_prep_bwd(down, g, cos_t, sin_t, dcat, *, name):
    s, w = down.shape
    bm = _row_block(s)

    def body(d_ref, g_ref, c_ref, s_ref, dc_ref, o_ref, dg_ref):
        dlat, dg = _rms_bwd_math(d_ref[:, :MLA_KV_RANK], g_ref[...], dc_ref[:, :MLA_KV_RANK])
        o_ref[:, :MLA_KV_RANK] = dlat.astype(BF16)
        o_ref[:, MLA_KV_RANK:] = _rope_slab_bwd(dc_ref[:, MLA_KV_RANK:], c_ref[...], s_ref[...]).astype(BF16)

        @pl.when(pl.program_id(0) == 0)
        def _():
            dg_ref[...] = jnp.zeros_like(dg_ref)

        dg_ref[...] += dg

    row = pl.BlockSpec((bm, w), lambda i: (i, 0))
    tab = pl.BlockSpec((bm, LANES), lambda i: (i, 0))
    vec = pl.BlockSpec((1, MLA_KV_RANK), lambda i: (0, 0))
    return pl.pallas_call(
        body, name=name, grid=(s // bm,),
        in_specs=[row, vec, tab, tab, row],
        out_specs=[row, vec],
        out_shape=[jax.ShapeDtypeStruct((s, w), BF16), jax.ShapeDtypeStruct((1, MLA_KV_RANK), F32)],
        compiler_params=pltpu.CompilerParams(dimension_semantics=("arbitrary",), vmem_limit_bytes=VMEM_LIMIT),
    )(down, g.reshape(1, MLA_KV_RANK), cos_t, sin_t, dcat)


def _split_bf16(v):
    hi = v.astype(BF16)
    lo = (v - hi.astype(F32)).astype(BF16)
    return hi, lo


def _suffix_matrices(n):
    row = lax.broadcasted_iota(jnp.int32, (n, n), 0)
    col = lax.broadcasted_iota(jnp.int32, (n, n), 1)
    strict = (row > col).astype(BF16)
    incl = (row >= col).astype(BF16)
    return jnp.concatenate([strict, strict], axis=0), jnp.concatenate([incl, incl], axis=0)


def _suffix_sum(v, matrix):
    hi, lo = _split_bf16(v)
    return _dot_nn(jnp.concatenate([hi, lo], axis=1), matrix)


def _block_positions(qi, kb, bq, bk):
    row = qi * bq + lax.broadcasted_iota(jnp.int32, (bq, bk), 0)
    col = kb * bk + lax.broadcasted_iota(jnp.int32, (bq, bk), 1)
    return row, col


def _att_blocks(s):
    bq, bk = min(ATT_Q_BLOCK, s), min(ATT_K_BLOCK, s)
    return bq, bk, s // bq, bq // bk


def _sweep(qi, ratio, step):
    for d in range(ratio):
        step((qi + 1) * ratio - 1 - d, True)

    def loop(i, carry):
        step(qi * ratio - 1 - i, False)
        return carry

    lax.fori_loop(0, qi * ratio, loop, 0)


def _sb_scores(q, k, causal):
    z = _dot_nt(q, k)
    t = jnp.exp(-jnp.abs(z))
    sp = jnp.log(1.0 + t)
    lb = jnp.minimum(z, 0.0) - sp
    lk = -jnp.maximum(z, 0.0) - sp
    if causal is not None:
        lk = jnp.where(causal, lk, 0.0)
    return z, t, lb, lk


def _sb_fwd(qkv, heads, *, name):
    s = qkv.shape[0]
    bq, bk, nq, ratio = _att_blocks(s)

    def body(q_ref, k_ref, v_ref, o_ref, acc_ref, c_ref):
        qi = pl.program_id(1)
        q = q_ref[...] * SB_SCALE
        m_strict, _ = _suffix_matrices(bk)
        acc_ref[...] = jnp.zeros_like(acc_ref)
        c_ref[...] = jnp.zeros_like(c_ref)

        def step(kb, masked):
            rows = pl.ds(pl.multiple_of(kb * bk, bk), bk)
            k, v = k_ref[rows, :], v_ref[rows, :]
            causal = None
            if masked:
                row, col = _block_positions(qi, kb, bq, bk)
                causal = col < row
            _, _, lb, lk = _sb_scores(q, k, causal)
            c = c_ref[...]
            w = jnp.exp(lb + _suffix_sum(lk, m_strict) + jnp.tile(c, (1, bk // LANES)))
            if masked:
                w = jnp.where(causal, w, 0.0)
            acc_ref[...] += _dot_nn(w.astype(BF16), v)
            c_ref[...] = c + jnp.sum(lk, axis=-1, keepdims=True)

        _sweep(qi, ratio, step)
        o_ref[...] = acc_ref[...].astype(o_ref.dtype)

    return pl.pallas_call(
        body, name=name, grid=(heads, nq),
        in_specs=[pl.BlockSpec((bq, LANES), lambda h, i: (i, h)),
                  pl.BlockSpec((s, LANES), lambda h, i: (0, heads + h)),
                  pl.BlockSpec((s, LANES), lambda h, i: (0, 2 * heads + h))],
        out_specs=pl.BlockSpec((bq, LANES), lambda h, i: (i, h)),
        out_shape=jax.ShapeDtypeStruct((s, heads * LANES), F32),
        scratch_shapes=[pltpu.VMEM((bq, LANES), F32), pltpu.VMEM((bq, LANES), F32)],
        compiler_params=pltpu.CompilerParams(dimension_semantics=("parallel", "arbitrary"),
                                             vmem_limit_bytes=VMEM_LIMIT),
    )(qkv, qkv, qkv)


def _sb_bwd(qkv, o, do, heads, *, name):
    s = qkv.shape[0]
    bq, bk, nq, ratio = _att_blocks(s)

    def body(q_ref, k_ref, v_ref, o_ref, do_ref, dq_ref, dk_ref, dv_ref, dq_acc, dk_acc, dv_acc, c_ref, e_ref):
        qi = pl.program_id(1)

        @pl.when(qi == 0)
        def _():
            dk_acc[...] = jnp.zeros_like(dk_acc)
            dv_acc[...] = jnp.zeros_like(dv_acc)

        q = q_ref[...] * SB_SCALE
        do = do_ref[...]
        total = jnp.sum(do.astype(F32) * o_ref[...].astype(F32), axis=-1, keepdims=True)
        m_strict, m_incl = _suffix_matrices(bk)
        dq_acc[...] = jnp.zeros_like(dq_acc)
        c_ref[...] = jnp.zeros_like(c_ref)
        e_ref[...] = jnp.broadcast_to(total, e_ref.shape)
        reps = (1, bk // LANES)

        def step(kb, masked):
            rows = pl.ds(pl.multiple_of(kb * bk, bk), bk)
            k, v = k_ref[rows, :], v_ref[rows, :]
            causal = None
            if masked:
                row, col = _block_positions(qi, kb, bq, bk)
                causal = col < row
            z, t, lb, lk = _sb_scores(q, k, causal)
            c = c_ref[...]
            w = jnp.exp(lb + _suffix_sum(lk, m_strict) + jnp.tile(c, reps))
            if masked:
                w = jnp.where(causal, w, 0.0)
            wb = w.astype(BF16)
            g = wb.astype(F32) * _dot_nt(do, v)
            e = e_ref[...]
            g_left = jnp.tile(e, reps) - _suffix_sum(g, m_incl)
            r = 1.0 / (1.0 + t)
            tr = t * r
            pos = z >= 0.0
            beta = jnp.where(pos, r, tr)
            one_minus_beta = jnp.where(pos, tr, r)
            da = g * one_minus_beta - beta * g_left
            if masked:
                da = jnp.where(causal, da, 0.0)
            dab = da.astype(BF16)
            dq_acc[...] += _dot_nn(dab, k)
            dk_acc[rows, :] += _dot_tn(dab, q)
            dv_acc[rows, :] += _dot_tn(wb, do)
            e_ref[...] = e - jnp.sum(g, axis=-1, keepdims=True)
            c_ref[...] = c + jnp.sum(lk, axis=-1, keepdims=True)

        _sweep(qi, ratio, step)
        dq_ref[...] = (dq_acc[...] * SB_SCALE).astype(dq_ref.dtype)

        @pl.when(qi == nq - 1)
        def _():
            dk_ref[...] = dk_acc[...].astype(dk_ref.dtype)
            dv_ref[...] = dv_acc[...].astype(dv_ref.dtype)

    blk = pl.BlockSpec((bq, LANES), lambda h, i: (i, h))
    full = pl.BlockSpec((s, LANES), lambda h, i: (0, h))
    shape = jax.ShapeDtypeStruct((s, heads * LANES), BF16)
    return pl.pallas_call(
        body, name=name, grid=(heads, nq),
        in_specs=[blk,
                  pl.BlockSpec((s, LANES), lambda h, i: (0, heads + h)),
                  pl.BlockSpec((s, LANES), lambda h, i: (0, 2 * heads + h)),
                  blk, blk],
        out_specs=[blk, full, full],
        out_shape=[shape, shape, shape],
        scratch_shapes=[pltpu.VMEM((bq, LANES), F32), pltpu.VMEM((s, LANES), F32), pltpu.VMEM((s, LANES), F32),
                        pltpu.VMEM((bq, LANES), F32), pltpu.VMEM((bq, LANES), F32)],
        compiler_params=pltpu.CompilerParams(dimension_semantics=("arbitrary", "arbitrary"),
                                             vmem_limit_bytes=VMEM_LIMIT),
    )(qkv, qkv, qkv, o, do)


def _chunk_allowed(qi, kb, bq, bk):
    row, col = _block_positions(qi, kb, bq, bk)
    return (col // CHUNK) <= (row // CHUNK)


def _mla_fwd(q, kv, heads, *, name):
    s = q.shape[0]
    bq, bk, nq, ratio = _att_blocks(s)
    reps = (1, bk // LANES)

    def body(q_ref, k_ref, v_ref, o_ref, lse_ref, acc_ref, m_ref, l_ref):
        qi = pl.program_id(1)
        qv = q_ref[...]
        acc_ref[...] = jnp.zeros_like(acc_ref)
        m_ref[...] = jnp.full_like(m_ref, NEG_BIG)
        l_ref[...] = jnp.zeros_like(l_ref)

        def step(kb, masked):
            rows = pl.ds(pl.multiple_of(kb * bk, bk), bk)
            k, v = k_ref[rows, :], v_ref[rows, :]
            sc = _dot_nt(qv, k) * MLA_SCALE
            if masked:
                allowed = _chunk_allowed(qi, kb, bq, bk)
                sc = jnp.where(allowed, sc, NEG_BIG)
            m_old = m_ref[...]
            m_new = jnp.maximum(m_old, jnp.max(sc, axis=-1, keepdims=True))
            p = jnp.exp(sc - jnp.tile(m_new, reps))
            if masked:
                p = jnp.where(allowed, p, 0.0)
            alpha = jnp.exp(m_old - m_new)
            l_ref[...] = alpha * l_ref[...] + jnp.sum(p, axis=-1, keepdims=True)
            acc_ref[...] = alpha * acc_ref[...] + _dot_nn(p.astype(BF16), v)
            m_ref[...] = m_new

        _sweep(qi, ratio, step)
        o_ref[...] = (acc_ref[...] / l_ref[...]).astype(o_ref.dtype)
        lse_ref[...] = m_ref[...] + jnp.log(l_ref[...])

    blk = pl.BlockSpec((bq, LANES), lambda h, i: (i, h))
    return pl.pallas_call(
        body, name=name, grid=(heads, nq),
        in_specs=[blk,
                  pl.BlockSpec((s, LANES), lambda h, i: (0, h)),
                  pl.BlockSpec((s, LANES), lambda h, i: (0, heads + h))],
        out_specs=[blk, blk],
        out_shape=[jax.ShapeDtypeStruct((s, heads * LANES), BF16), jax.ShapeDtypeStruct((s, heads * LANES), F32)],
        scratch_shapes=[pltpu.VMEM((bq, LANES), F32), pltpu.VMEM((bq, LANES), F32), pltpu.VMEM((bq, LANES), F32)],
        compiler_params=pltpu.CompilerParams(dimension_semantics=("parallel", "arbitrary"),
                                             vmem_limit_bytes=VMEM_LIMIT),
    )(q, kv, kv)


def _mla_bwd(q, kv, o, do, lse, cos_t, sin_t, dkv_init, heads, *, name):
    s = q.shape[0]
    bq, bk, nq, ratio = _att_blocks(s)
    reps = (1, bk // LANES)
    has_init = dkv_init is not None

    def body(*refs):
        q_ref, k_ref, v_ref, o_ref, do_ref, lse_ref, c_ref, s_ref = refs[:8]
        ki_ref, vi_ref = (refs[8], refs[9]) if has_init else (None, None)
        dq_ref, dk_ref, dv_ref, dq_acc, dk_acc, dv_acc = refs[-6:]
        qi = pl.program_id(1)

        @pl.when(qi == 0)
        def _():
            if has_init:
                dk_acc[...] = ki_ref[...].astype(F32)
                dv_acc[...] = vi_ref[...].astype(F32)
            else:
                dk_acc[...] = jnp.zeros_like(dk_acc)
                dv_acc[...] = jnp.zeros_like(dv_acc)

        qv = q_ref[...]
        do = do_ref[...]
        delta = jnp.sum(do.astype(F32) * o_ref[...].astype(F32), axis=-1, keepdims=True)
        lse_wide = jnp.tile(lse_ref[...], reps)
        dq_acc[...] = jnp.zeros_like(dq_acc)

        def step(kb, masked):
            rows = pl.ds(pl.multiple_of(kb * bk, bk), bk)
            k, v = k_ref[rows, :], v_ref[rows, :]
            sc = _dot_nt(qv, k) * MLA_SCALE
            p = jnp.exp(sc - lse_wide)
            if masked:
                p = jnp.where(_chunk_allowed(qi, kb, bq, bk), p, 0.0)
            ds = (p * (_dot_nt(do, v) - delta) * MLA_SCALE).astype(BF16)
            dq_acc[...] += _dot_nn(ds, k)
            dk_acc[rows, :] += _dot_tn(ds, qv)
            dv_acc[rows, :] += _dot_tn(p.astype(BF16), do)

        _sweep(qi, ratio, step)
        dq_ref[...] = _rope_slab_bwd(dq_acc[...], c_ref[...], s_ref[...]).astype(dq_ref.dtype)

        @pl.when(qi == nq - 1)
        def _():
            dk_ref[...] = dk_acc[...].astype(dk_ref.dtype)
            dv_ref[...] = dv_acc[...].astype(dv_ref.dtype)

    blk = pl.BlockSpec((bq, LANES), lambda h, i: (i, h))
    tab = pl.BlockSpec((bq, LANES), lambda h, i: (i, 0))
    k_full = pl.BlockSpec((s, LANES), lambda h, i: (0, h))
    v_full = pl.BlockSpec((s, LANES), lambda h, i: (0, heads + h))
    shape = jax.ShapeDtypeStruct((s, heads * LANES), BF16)
    ins = [q, kv, kv, o, do, lse, cos_t, sin_t] + ([dkv_init, dkv_init] if has_init else [])
    dq, dk, dv = pl.pallas_call(
        body, name=name, grid=(heads, nq),
        in_specs=[blk, k_full, v_full, blk, blk, blk, tab, tab] + ([k_full, v_full] if has_init else []),
        out_specs=[blk, k_full, k_full],
        out_shape=[shape, shape, shape],
        scratch_shapes=[pltpu.VMEM((bq, LANES), F32), pltpu.VMEM((s, LANES), F32), pltpu.VMEM((s, LANES), F32)],
        compiler_params=pltpu.CompilerParams(dimension_semantics=("arbitrary", "arbitrary"),
                                             vmem_limit_bytes=VMEM_LIMIT),
    )(*ins)
    return dq, jnp.concatenate([dk, dv], axis=1)


def _pad_last(a, width):
    return jnp.pad(a, [(0, 0)] * (a.ndim - 1) + [(0, width - a.shape[-1])])


def _pad_qkv(w, heads):
    d = w.shape[0]
    return _pad_last(w.reshape(d, 3 * heads, SB_HEAD_DIM), LANES).reshape(d, 3 * heads * LANES)


def _unpad_qkv(g, heads):
    d = g.shape[0]
    return g.reshape(d, 3 * heads, LANES)[:, :, :SB_HEAD_DIM].reshape(d, 3 * heads * SB_HEAD_DIM)


def _pad_o(w, heads):
    d = w.shape[1]
    w = w.reshape(heads, SB_HEAD_DIM, d)
    return jnp.pad(w, [(0, 0), (0, LANES - SB_HEAD_DIM), (0, 0)]).reshape(heads * LANES, d)


def _unpad_o(g, heads):
    d = g.shape[1]
    return g.reshape(heads, LANES, d)[:, :SB_HEAD_DIM, :].reshape(heads * SB_HEAD_DIM, d)


def _pad_uq(w, heads):
    r = w.shape[0]
    return _pad_last(w.reshape(r, heads, MLA_NOPE + MLA_ROPE), LANES).reshape(r, heads * LANES)


def _unpad_uq(g, heads):
    r = g.shape[0]
    return g.reshape(r, heads, LANES)[:, :, :MLA_NOPE + MLA_ROPE].reshape(r, heads * (MLA_NOPE + MLA_ROPE))


def _pad_dkv(w):
    d = w.shape[0]
    rope = jnp.zeros((d, LANES), w.dtype).at[:, ROPE_LO:ROPE_LO + MLA_ROPE].set(w[:, MLA_KV_RANK:])
    return jnp.concatenate([w[:, :MLA_KV_RANK], rope], axis=1)


def _unpad_dkv(g):
    return jnp.concatenate([g[:, :MLA_KV_RANK], g[:, MLA_KV_RANK + ROPE_LO:MLA_KV_RANK + ROPE_LO + MLA_ROPE]], axis=1)


def _pad_ukv(w, heads):
    w = w.reshape(MLA_KV_RANK, heads, 2, MLA_NOPE)
    k_part = _pad_last(w[:, :, 0, :], LANES).reshape(MLA_KV_RANK, heads * LANES)
    v_part = _pad_last(w[:, :, 1, :], LANES).reshape(MLA_KV_RANK, heads * LANES)
    lane = jnp.arange(LANES)
    place = ((lane[:, None] == lane[None, :]) & (lane[:, None] >= ROPE_LO) & (lane[:, None] < ROPE_LO + MLA_ROPE))
    place = jnp.tile(place.astype(w.dtype), (1, heads))
    top = jnp.concatenate([k_part, v_part], axis=1)
    bottom = jnp.concatenate([place, jnp.zeros_like(place)], axis=1)
    return jnp.concatenate([top, bottom], axis=0)


def _unpad_ukv(g, heads):
    g = g[:MLA_KV_RANK]
    k_part = g[:, :heads * LANES].reshape(MLA_KV_RANK, heads, LANES)[:, :, :MLA_NOPE]
    v_part = g[:, heads * LANES:].reshape(MLA_KV_RANK, heads, LANES)[:, :, :MLA_V]
    return jnp.stack([k_part, v_part], axis=2).reshape(MLA_KV_RANK, heads * (MLA_NOPE + MLA_V))


def _rope_tables(positions):
    inv_freq = ROPE_THETA ** (-jnp.arange(0, MLA_ROPE, 2, dtype=F32) / MLA_ROPE)
    ang = positions.astype(F32)[:, None] * inv_freq
    cos, sin = jnp.cos(ang), jnp.sin(ang)
    s = positions.shape[0]
    cos_t = jnp.ones((s, LANES), F32).at[:, ROPE_LO:ROPE_LO + MLA_ROPE].set(jnp.concatenate([cos, cos], axis=1))
    sin_t = jnp.zeros((s, LANES), F32).at[:, ROPE_LO:ROPE_LO + MLA_ROPE].set(jnp.concatenate([-sin, sin], axis=1))
    return cos_t, sin_t


def _local_step(x, positions, target, w, norms):
    s, d = x.shape
    heads = d // SB_HEAD_DIM
    n_a = w["sb_w_qkv"].shape[0]
    n_b = w["mla_w_dq"].shape[0]
    depth = n_a + n_b
    cos_t, sin_t = _rope_tables(positions)

    wqkv = [_pad_qkv(w["sb_w_qkv"][l], heads) for l in range(n_a)]
    wo_a = [_pad_o(w["sb_w_o"][l], heads) for l in range(n_a)]
    wdkv = _pad_dkv(w["mla_w_dkv"])
    wkv = _pad_ukv(w["mla_w_ukv"], heads)
    wdq = [w["mla_w_dq"][j] for j in range(n_b)]
    wuq = [_pad_uq(w["mla_w_uq"][j], heads) for j in range(n_b)]
    wo_b = [_pad_o(w["mla_w_o"][j], heads) for j in range(n_b)]
    w1 = [w["mlp_w1"][l] for l in range(depth)]
    w2 = [w["mlp_w2"][l] for l in range(depth)]

    saved = []
    kv_saved = None
    kv = None
    for l in range(depth):
        t = f"l{l}"
        sv = {"x_in": x}
        h = _rms_fwd(x, norms["attn_norm"][l], name=f"{t}_attn_norm")
        sv["h"] = h
        if l < n_a:
            qkv = _mm(h, wqkv[l], name=f"{t}_qkv")
            o = _sb_fwd(qkv, heads, name=f"{t}_sb_fwd")
            sv["qkv"], sv["o"] = qkv, o
            x = _mm(o, wo_a[l], name=f"{t}_attn_out", epilogue=_epi_add, extras=[(x, "tile")], out_dtypes=(F32,))
        else:
            j = l - n_a
            if j == 0:
                hk = _rms_fwd(x, norms["kv_norm"], name="kv_norm")
                down = _mm(hk, wdkv, name="kv_down", out_dtypes=(F32,))
                cat = _kv_prep(down, norms["mla_kv_lat_norm"], cos_t, sin_t, name="kv_prep")
                kv = _mm(cat, wkv, name="kv_up")
                kv_saved = {"x_in": x, "hk": hk, "down": down, "cat": cat}
            cq0 = _mm(h, wdq[j], name=f"{t}_q_down", out_dtypes=(F32,))
            cq = _rms_fwd(cq0, norms["mla_q_lat_norm"][j], name=f"{t}_q_lat_norm")
            q = _mm(cq, wuq[j], name=f"{t}_q_up", epilogue=_epi_rope_heads, extras=[(cos_t, "row"), (sin_t, "row")])
            o, lse = _mla_fwd(q, kv, heads, name=f"{t}_mla_fwd")
            sv.update(cq0=cq0, cq=cq, q=q, o=o, lse=lse)
            x = _mm(o, wo_b[j], name=f"{t}_attn_out", epilogue=_epi_add, extras=[(x, "tile")], out_dtypes=(F32,))
        sv["x_mid"] = x
        h2 = _rms_fwd(x, norms["mlp_norm"][l], name=f"{t}_mlp_norm")
        u, a = _mm(h2, w1[l], name=f"{t}_mlp_up", epilogue=_epi_relu2, out_dtypes=(BF16, BF16))
        sv.update(h2=h2, u=u, a=a)
        x = _mm(a, w2[l], name=f"{t}_mlp_down", epilogue=_epi_add, extras=[(x, "tile")], out_dtypes=(F32,))
        saved.append(sv)

    loss_slab, dx, dxb, dg_final = _loss_bwd(x, norms["final_norm"], target, name="loss")
    loss = loss_slab[0, 0]

    g_attn_norm, g_mlp_norm = [None] * depth, [None] * depth
    g_qkv, g_o_a = [None] * n_a, [None] * n_a
    g_dq, g_uq, g_o_b, g_qlat = [None] * n_b, [None] * n_b, [None] * n_b, [None] * n_b
    g_w1, g_w2 = [None] * depth, [None] * depth
    dkv = None
    g_kv_norm = g_kv_lat = g_dkv = g_ukv = None

    for l in reversed(range(depth)):
        t = f"l{l}"
        sv = saved[l]
        du = _mm(dxb, w2[l], name=f"{t}_mlp_down_dx", dims="nt", epilogue=_epi_relu2_grad, extras=[(sv["u"], "tile")])
        g_w2[l] = _mm(sv["a"], dxb, name=f"{t}_mlp_down_dw", dims="tn", out_dtypes=(F32,))
        g_w1[l] = _mm(sv["h2"], du, name=f"{t}_mlp_up_dw", dims="tn", out_dtypes=(F32,))
        dh2 = _mm(du, w1[l], name=f"{t}_mlp_up_dx", dims="nt", out_dtypes=(F32,))
        dx, dxb, g_mlp_norm[l] = _rms_bwd(sv["x_mid"], norms["mlp_norm"][l], dh2, dx, name=f"{t}_mlp_norm_bwd")
        if l < n_a:
            do = _mm(dxb, wo_a[l], name=f"{t}_attn_out_dx", dims="nt")
            g_o_a[l] = _unpad_o(_mm(sv["o"], dxb, name=f"{t}_attn_out_dw", dims="tn", out_dtypes=(F32,)), heads)
            dq, dk, dv = _sb_bwd(sv["qkv"], sv["o"], do, heads, name=f"{t}_sb_bwd")
            dqkv = jnp.concatenate([dq, dk, dv], axis=1)
            g_qkv[l] = _unpad_qkv(_mm(sv["h"], dqkv, name=f"{t}_qkv_dw", dims="tn", out_dtypes=(F32,)), heads)
            dh = _mm(dqkv, wqkv[l], name=f"{t}_qkv_dx", dims="nt", out_dtypes=(F32,))
        else:
            j = l - n_a
            do = _mm(dxb, wo_b[j], name=f"{t}_attn_out_dx", dims="nt")
            g_o_b[j] = _unpad_o(_mm(sv["o"], dxb, name=f"{t}_attn_out_dw", dims="tn", out_dtypes=(F32,)), heads)
            dq, dkv = _mla_bwd(sv["q"], kv, sv["o"], do, sv["lse"], cos_t, sin_t, dkv, heads, name=f"{t}_mla_bwd")
            g_uq[j] = _unpad_uq(_mm(sv["cq"], dq, name=f"{t}_q_up_dw", dims="tn", out_dtypes=(F32,)), heads)
            dcq = _mm(dq, wuq[j], name=f"{t}_q_up_dx", dims="nt", out_dtypes=(F32,))
            _, dcq0, g_qlat[j] = _rms_bwd(sv["cq0"], norms["mla_q_lat_norm"][j], dcq, None, name=f"{t}_q_lat_norm_bwd")
            g_dq[j] = _mm(sv["h"], dcq0, name=f"{t}_q_down_dw", dims="tn", out_dtypes=(F32,))
            dh = _mm(dcq0, wdq[j], name=f"{t}_q_down_dx", dims="nt", out_dtypes=(F32,))
        dx, dxb, g_attn_norm[l] = _rms_bwd(sv["x_in"], norms["attn_norm"][l], dh, dx, name=f"{t}_attn_norm_bwd")
        if l == n_a:
            ks = kv_saved
            dcat = _mm(dkv, wkv, name="kv_up_dx", dims="nt", out_dtypes=(F32,))
            g_ukv = _unpad_ukv(_mm(ks["cat"], dkv, name="kv_up_dw", dims="tn", out_dtypes=(F32,)), heads)
            ddown, g_kv_lat = _kv_prep_bwd(ks["down"], norms["mla_kv_lat_norm"], cos_t, sin_t, dcat, name="kv_prep_bwd")
            g_dkv = _unpad_dkv(_mm(ks["hk"], ddown, name="kv_down_dw", dims="tn", out_dtypes=(F32,)))
            dhk = _mm(ddown, wdkv, name="kv_down_dx", dims="nt", out_dtypes=(F32,))
            dx, dxb, g_kv_norm = _rms_bwd(ks["x_in"], norms["kv_norm"], dhk, dx, name="kv_norm_bwd")

    grads = {
        "attn_norm": jnp.concatenate(g_attn_norm, axis=0), "mlp_norm": jnp.concatenate(g_mlp_norm, axis=0),
        "sb_w_qkv": jnp.stack(g_qkv), "sb_w_o": jnp.stack(g_o_a),
        "kv_norm": g_kv_norm[0], "mla_w_dkv": g_dkv, "mla_kv_lat_norm": g_kv_lat[0], "mla_w_ukv": g_ukv,
        "mla_w_dq": jnp.stack(g_dq), "mla_q_lat_norm": jnp.concatenate(g_qlat, axis=0),
        "mla_w_uq": jnp.stack(g_uq), "mla_w_o": jnp.stack(g_o_b),
        "mlp_w1": jnp.stack(g_w1), "mlp_w2": jnp.stack(g_w2), "final_norm": dg_final[0],
    }
    return loss, dx, grads


def _flat_rows(n_elems):
    per_block = FLAT_COLS * FLAT_ROW_BLOCK
    return -(-n_elems // per_block) * FLAT_ROW_BLOCK


def _pack(arrays, dtype):
    flat = jnp.concatenate([a.reshape(-1).astype(dtype) for a in arrays])
    rows = _flat_rows(flat.shape[0])
    flat = jnp.pad(flat, (0, rows * FLAT_COLS - flat.shape[0]))
    return flat.reshape(rows, FLAT_COLS)


def _unpack(flat, shapes):
    flat = flat.reshape(-1)
    out, off = [], 0
    for shp in shapes:
        n = 1
        for v in shp:
            n *= v
        out.append(flat[off:off + n].reshape(shp))
        off += n
    return out


def _pack_small(arrays):
    rows = []
    for a in arrays:
        a = a.reshape(-1, a.shape[-1]) if a.shape[-1] == FLAT_COLS else a.reshape(1, -1)
        rows.append(_pad_last(a, FLAT_COLS))
    flat = jnp.concatenate(rows, axis=0)
    return jnp.pad(flat, [(0, -flat.shape[0] % 8), (0, 0)])


def _unpack_small(flat, shapes):
    out, row = [], 0
    for shp in shapes:
        if shp[-1] == FLAT_COLS:
            n = 1
            for v in shp[:-1]:
                n *= v
            out.append(flat[row:row + n].reshape(shp))
            row += n
        else:
            n = 1
            for v in shp:
                n *= v
            out.append(flat[row, :n].reshape(shp))
            row += 1
    return out


def _other_chips(x, y):
    return [(1 - x, y), (x, 1 - y), (1 - x, 1 - y)]


def _all_gather_chips(flat, *, name):
    rows, cols = flat.shape

    def body(x_ref, out_ref, send_sems, recv_sems, local_sem):
        x, y, c = lax.axis_index("x"), lax.axis_index("y"), lax.axis_index("c")
        me = 2 * x + y
        mine = pltpu.make_async_copy(x_ref, out_ref.at[me], local_sem)
        mine.start()
        sends = []
        for k, (px, py) in enumerate(_other_chips(x, y)):
            cp = pltpu.make_async_remote_copy(src_ref=x_ref, dst_ref=out_ref.at[me], send_sem=send_sems.at[k],
                                              recv_sem=recv_sems.at[k], device_id=(px, py, c), device_id_type=MESH)
            cp.start()
            sends.append(cp)
        for k, (px, py) in enumerate(_other_chips(x, y)):
            pltpu.make_async_remote_copy(src_ref=x_ref, dst_ref=out_ref.at[2 * px + py], send_sem=send_sems.at[k],
                                         recv_sem=recv_sems.at[k], device_id=(px, py, c),
                                         device_id_type=MESH).wait_recv()
        for cp in sends:
            cp.wait_send()
        mine.wait()

    return pl.pallas_call(
        body, name=name,
        in_specs=[pl.BlockSpec(memory_space=pltpu.HBM)],
        out_specs=pl.BlockSpec(memory_space=pltpu.HBM),
        out_shape=jax.ShapeDtypeStruct((N_CHIPS, rows, cols), flat.dtype),
        scratch_shapes=[pltpu.SemaphoreType.DMA((3,)), pltpu.SemaphoreType.DMA((3,)), pltpu.SemaphoreType.DMA],
        compiler_params=pltpu.CompilerParams(has_side_effects=True),
    )(flat)


def _exchange_chips(parts, *, name):
    def body(g_ref, out_ref, send_sems, recv_sems, local_sem):
        x, y, c = lax.axis_index("x"), lax.axis_index("y"), lax.axis_index("c")
        me = 2 * x + y
        mine = pltpu.make_async_copy(g_ref.at[me], out_ref.at[me], local_sem)
        mine.start()
        sends = []
        for k, (px, py) in enumerate(_other_chips(x, y)):
            cp = pltpu.make_async_remote_copy(src_ref=g_ref.at[2 * px + py], dst_ref=out_ref.at[me],
                                              send_sem=send_sems.at[k], recv_sem=recv_sems.at[k],
                                              device_id=(px, py, c), device_id_type=MESH)
            cp.start()
            sends.append(cp)
        for k, (px, py) in enumerate(_other_chips(x, y)):
            pltpu.make_async_remote_copy(src_ref=g_ref.at[me], dst_ref=out_ref.at[2 * px + py],
                                         send_sem=send_sems.at[k], recv_sem=recv_sems.at[k],
                                         device_id=(px, py, c), device_id_type=MESH).wait_recv()
        for cp in sends:
            cp.wait_send()
        mine.wait()

    return pl.pallas_call(
        body, name=name,
        in_specs=[pl.BlockSpec(memory_space=pltpu.HBM)],
        out_specs=pl.BlockSpec(memory_space=pltpu.HBM),
        out_shape=jax.ShapeDtypeStruct(parts.shape, parts.dtype),
        scratch_shapes=[pltpu.SemaphoreType.DMA((3,)), pltpu.SemaphoreType.DMA((3,)), pltpu.SemaphoreType.DMA],
        compiler_params=pltpu.CompilerParams(has_side_effects=True),
    )(parts)


def _swap_cores(v, *, name):
    def body(v_ref, out_ref, send_sem, recv_sem):
        peer = (lax.axis_index("x"), lax.axis_index("y"), 1 - lax.axis_index("c"))
        cp = pltpu.make_async_remote_copy(src_ref=v_ref, dst_ref=out_ref, send_sem=send_sem, recv_sem=recv_sem,
                                          device_id=peer, device_id_type=MESH)
        cp.start()
        cp.wait()

    return pl.pallas_call(
        body, name=name,
        in_specs=[pl.BlockSpec(memory_space=pltpu.HBM)],
        out_specs=pl.BlockSpec(memory_space=pltpu.HBM),
        out_shape=jax.ShapeDtypeStruct(v.shape, v.dtype),
        scratch_shapes=[pltpu.SemaphoreType.DMA, pltpu.SemaphoreType.DMA],
        compiler_params=pltpu.CompilerParams(has_side_effects=True),
    )(v)


def _sum_chips(parts, *, name):
    _, rows, cols = parts.shape

    def body(p_ref, o_ref):
        o_ref[...] = ((p_ref[0] + p_ref[1]) + p_ref[2]) + p_ref[3]

    return pl.pallas_call(
        body, name=name, grid=(rows // FLAT_ROW_BLOCK,),
        in_specs=[pl.BlockSpec((N_CHIPS, FLAT_ROW_BLOCK, cols), lambda i: (0, i, 0))],
        out_specs=pl.BlockSpec((FLAT_ROW_BLOCK, cols), lambda i: (i, 0)),
        out_shape=jax.ShapeDtypeStruct((rows, cols), parts.dtype),
        compiler_params=pltpu.CompilerParams(dimension_semantics=("parallel",), vmem_limit_bytes=VMEM_LIMIT),
    )(parts)


def _all_reduce_small(v, *, name):
    rows, cols = v.shape
    flips = [(fx, fy, fc) for fx in (0, 1) for fy in (0, 1) for fc in (0, 1)][1:]

    def body(v_ref, out_ref, gath_ref, send_sems, recv_sems):
        x, y, c = lax.axis_index("x"), lax.axis_index("y"), lax.axis_index("c")
        me = 4 * x + 2 * y + c
        gath_ref[me] = v_ref[...]
        peers = [((1 - x) if fx else x, (1 - y) if fy else y, (1 - c) if fc else c) for fx, fy, fc in flips]
        sends = []
        for k, peer in enumerate(peers):
            cp = pltpu.make_async_remote_copy(src_ref=v_ref, dst_ref=gath_ref.at[me], send_sem=send_sems.at[k],
                                              recv_sem=recv_sems.at[k], device_id=peer, device_id_type=MESH)
            cp.start()
            sends.append(cp)
        for k, (px, py, pc) in enumerate(peers):
            pltpu.make_async_remote_copy(src_ref=v_ref, dst_ref=gath_ref.at[4 * px + 2 * py + pc],
                                         send_sem=send_sems.at[k], recv_sem=recv_sems.at[k],
                                         device_id=(px, py, pc), device_id_type=MESH).wait_recv()
        for cp in sends:
            cp.wait_send()
        total = gath_ref[0]
        for k in range(1, 8):
            total = total + gath_ref[k]
        out_ref[...] = total

    total, _ = pl.pallas_call(
        body, name=name,
        in_specs=[pl.BlockSpec(memory_space=pltpu.VMEM)],
        out_specs=[pl.BlockSpec(memory_space=pltpu.VMEM), pl.BlockSpec(memory_space=pltpu.VMEM)],
        out_shape=[jax.ShapeDtypeStruct((rows, cols), v.dtype), jax.ShapeDtypeStruct((8, rows, cols), v.dtype)],
        scratch_shapes=[pltpu.SemaphoreType.DMA((7,)), pltpu.SemaphoreType.DMA((7,))],
        compiler_params=pltpu.CompilerParams(has_side_effects=True),
    )(v)
    return total


def _adamw(w, g_parts, m, v, *, name):
    rows, cols = w.shape
    br = min(FLAT_ROW_BLOCK, rows)
    n_parts = len(g_parts)

    def body(*refs):
        w_ref = refs[0]
        g_refs = refs[1:1 + n_parts]
        m_ref, v_ref = refs[1 + n_parts], refs[2 + n_parts]
        g_out, d_out, m_out, v_out = refs[-4:]
        g = g_refs[0][...]
        for r in g_refs[1:]:
            g = g + r[...]
        m_new = ADAM_B1 * m_ref[...] + (1.0 - ADAM_B1) * g
        v_new = ADAM_B2 * v_ref[...] + (1.0 - ADAM_B2) * jnp.square(g)
        m_hat = m_new / (1.0 - ADAM_B1 ** ADAM_STEP)
        v_hat = v_new / (1.0 - ADAM_B2 ** ADAM_STEP)
        g_out[...] = g
        d_out[...] = -ADAM_LR * (m_hat / (jnp.sqrt(v_hat) + ADAM_EPS) + ADAM_WD * w_ref[...])
        m_out[...] = m_new
        v_out[...] = v_new

    blk = pl.BlockSpec((br, cols), lambda i: (i, 0))
    shape = jax.ShapeDtypeStruct((rows, cols), F32)
    return pl.pallas_call(
        body, name=name, grid=(rows // br,),
        in_specs=[blk] * (3 + n_parts), out_specs=[blk] * 4, out_shape=[shape] * 4,
        compiler_params=pltpu.CompilerParams(dimension_semantics=("parallel",), vmem_limit_bytes=VMEM_LIMIT),
    )(w, *g_parts, m, v)


def _assemble(gathered_shards, name):
    return jnp.concatenate(gathered_shards, axis=SHARD_AXIS[name])


def _chip_shard(full, name, j):
    axis = SHARD_AXIS[name]
    n = full.shape[axis] // N_CHIPS
    return lax.slice_in_dim(full, j * n, (j + 1) * n, axis=axis)


def kernel(x, positions, attn_norm, mlp_norm, sb_w_qkv, sb_w_o, kv_norm, mla_w_dkv, mla_kv_lat_norm, mla_w_ukv, mla_w_dq, mla_q_lat_norm, mla_w_uq, mla_w_o, mlp_w1, mlp_w2, final_norm, loss_target, m_attn_norm, m_mlp_norm, m_sb_w_qkv, m_sb_w_o, m_kv_norm, m_mla_w_dkv, m_mla_kv_lat_norm, m_mla_w_ukv, m_mla_w_dq, m_mla_q_lat_norm, m_mla_w_uq, m_mla_w_o, m_mlp_w1, m_mlp_w2, m_final_norm, v_attn_norm, v_mlp_norm, v_sb_w_qkv, v_sb_w_o, v_kv_norm, v_mla_w_dkv, v_mla_kv_lat_norm, v_mla_w_ukv, v_mla_w_dq, v_mla_q_lat_norm, v_mla_w_uq, v_mla_w_o, v_mlp_w1, v_mlp_w2, v_final_norm):
    weights = dict(attn_norm=attn_norm, mlp_norm=mlp_norm, sb_w_qkv=sb_w_qkv, sb_w_o=sb_w_o, kv_norm=kv_norm,
                   mla_w_dkv=mla_w_dkv, mla_kv_lat_norm=mla_kv_lat_norm, mla_w_ukv=mla_w_ukv, mla_w_dq=mla_w_dq,
                   mla_q_lat_norm=mla_q_lat_norm, mla_w_uq=mla_w_uq, mla_w_o=mla_w_o, mlp_w1=mlp_w1, mlp_w2=mlp_w2,
                   final_norm=final_norm)
    m_in = dict(attn_norm=m_attn_norm, mlp_norm=m_mlp_norm, sb_w_qkv=m_sb_w_qkv, sb_w_o=m_sb_w_o, kv_norm=m_kv_norm,
                mla_w_dkv=m_mla_w_dkv, mla_kv_lat_norm=m_mla_kv_lat_norm, mla_w_ukv=m_mla_w_ukv, mla_w_dq=m_mla_w_dq,
                mla_q_lat_norm=m_mla_q_lat_norm, mla_w_uq=m_mla_w_uq, mla_w_o=m_mla_w_o, mlp_w1=m_mlp_w1,
                mlp_w2=m_mlp_w2, final_norm=m_final_norm)
    v_in = dict(attn_norm=v_attn_norm, mlp_norm=v_mlp_norm, sb_w_qkv=v_sb_w_qkv, sb_w_o=v_sb_w_o, kv_norm=v_kv_norm,
                mla_w_dkv=v_mla_w_dkv, mla_kv_lat_norm=v_mla_kv_lat_norm, mla_w_ukv=v_mla_w_ukv, mla_w_dq=v_mla_w_dq,
                mla_q_lat_norm=v_mla_q_lat_norm, mla_w_uq=v_mla_w_uq, mla_w_o=v_mla_w_o, mlp_w1=v_mlp_w1,
                mlp_w2=v_mlp_w2, final_norm=v_final_norm)
    shard_shapes = [weights[n].shape for n in BIG_WEIGHTS]
    small_shapes = [weights[n].shape for n in SMALL_WEIGHTS]

    gathered = _all_gather_chips(_pack([weights[n] for n in BIG_WEIGHTS], BF16), name="weights_all_gather")
    per_chip = [_unpack(gathered[j], shard_shapes) for j in range(N_CHIPS)]
    full_w = {n: _assemble([per_chip[j][i] for j in range(N_CHIPS)], n) for i, n in enumerate(BIG_WEIGHTS)}
    norms = {n: weights[n] for n in SMALL_WEIGHTS}

    loss, dx, grads = _local_step(x[0], positions[0], loss_target[0], full_w, norms)
    loss = lax.psum(loss, ("x", "y", "c"))

    parts = jnp.stack([_pack([_chip_shard(grads[n], n, j) for n in BIG_WEIGHTS], F32) for j in range(N_CHIPS)])
    received = _exchange_chips(parts, name="grads_exchange")
    core_sum = _sum_chips(received, name="grads_sum_chips")
    other_sum = _swap_cores(core_sum, name="grads_swap_cores")
    g_flat, d_flat, m_flat, v_flat = _adamw(
        _pack([weights[n] for n in BIG_WEIGHTS], F32), [core_sum, other_sum],
        _pack([m_in[n] for n in BIG_WEIGHTS], F32), _pack([v_in[n] for n in BIG_WEIGHTS], F32), name="adamw_big")
    out_g = dict(zip(BIG_WEIGHTS, _unpack(g_flat, shard_shapes)))
    out_d = dict(zip(BIG_WEIGHTS, _unpack(d_flat, shard_shapes)))
    out_m = dict(zip(BIG_WEIGHTS, _unpack(m_flat, shard_shapes)))
    out_v = dict(zip(BIG_WEIGHTS, _unpack(v_flat, shard_shapes)))

    small_sum = _all_reduce_small(_pack_small([grads[n] for n in SMALL_WEIGHTS]), name="gains_all_reduce")
    sg, sd, sm, sv = _adamw(_pack_small([weights[n] for n in SMALL_WEIGHTS]), [small_sum],
                            _pack_small([m_in[n] for n in SMALL_WEIGHTS]),
                            _pack_small([v_in[n] for n in SMALL_WEIGHTS]), name="adamw_gains")
    out_g.update(zip(SMALL_WEIGHTS, _unpack_small(sg, small_shapes)))
    out_d.update(zip(SMALL_WEIGHTS, _unpack_small(sd, small_shapes)))
    out_m.update(zip(SMALL_WEIGHTS, _unpack_small(sm, small_shapes)))
    out_v.update(zip(SMALL_WEIGHTS, _unpack_small(sv, small_shapes)))

    return (loss, dx[None], *[out_g[n] for n in ALL_WEIGHTS], *[out_d[n] for n in ALL_WEIGHTS],
            *[out_m[n] for n in ALL_WEIGHTS], *[out_v[n] for n in ALL_WEIGHTS])
```

```python
import functools

import jax
import jax.numpy as jnp
from jax import lax
from jax.experimental import pallas as pl
from jax.experimental.pallas import tpu as pltpu

F32 = jnp.float32
BF16 = jnp.bfloat16

LANES = 128
SB_HEAD_DIM = 64
MLA_NOPE = 64
MLA_ROPE = 32
MLA_V = 64
MLA_Q_RANK = 384
MLA_KV_RANK = 256
CHUNK = 64
ROPE_THETA = 10000.0
NORM_EPS = 1e-6
SB_SCALE = SB_HEAD_DIM ** -0.5
MLA_SCALE = (MLA_NOPE + MLA_ROPE) ** -0.5
ROPE_LO = MLA_NOPE
ROPE_HALF = MLA_ROPE // 2
ATT_Q_BLOCK = 1024
ATT_K_BLOCK = 256
NEG_BIG = -1e30
VMEM_LIMIT = 56 * 1024 * 1024

ADAM_LR = 0.001
ADAM_B1 = 0.9
ADAM_B2 = 0.999
ADAM_EPS = 1e-08
ADAM_WD = 0.01
ADAM_STEP = 10

FLAT_COLS = 1024
FLAT_ROW_BLOCK = 256
N_CHIPS = 4
MESH = pl.DeviceIdType.MESH

BIG_WEIGHTS = ["sb_w_qkv", "sb_w_o", "mla_w_dkv", "mla_w_ukv", "mla_w_dq", "mla_w_uq", "mla_w_o", "mlp_w1", "mlp_w2"]
SHARD_AXIS = {"sb_w_qkv": 2, "sb_w_o": 1, "mla_w_dkv": 0, "mla_w_ukv": 1, "mla_w_dq": 1, "mla_w_uq": 2,
              "mla_w_o": 1, "mlp_w1": 2, "mlp_w2": 1}
SMALL_WEIGHTS = ["attn_norm", "mlp_norm", "kv_norm", "mla_kv_lat_norm", "mla_q_lat_norm", "final_norm"]
ALL_WEIGHTS = ["attn_norm", "mlp_norm", "sb_w_qkv", "sb_w_o", "kv_norm", "mla_w_dkv", "mla_kv_lat_norm", "mla_w_ukv",
               "mla_w_dq", "mla_q_lat_norm", "mla_w_uq", "mla_w_o", "mlp_w1", "mlp_w2", "final_norm"]


def _dot(a, b, dims):
    return lax.dot_general(a, b, (dims, ((), ())), preferred_element_type=F32)


def _dot_nn(a, b):
    return _dot(a, b, ((1,), (0,)))


def _dot_nt(a, b):
    return _dot(a, b, ((1,), (1,)))


def _dot_tn(a, b):
    return _dot(a, b, ((0,), (0,)))


def _pick_block(n, target):
    if n <= target:
        return n
    best = max(b for b in range(LANES, target + 1, LANES) if n % b == 0)
    return best


MM_ROWS = 512
MM_COLS = 1024
MM_DEPTH = 4096
MM_DEPTH_TN = 1024


def _mm(a, b, *, name, dims="nn", epilogue=None, extras=(), out_dtypes=(BF16,)):
    if dims == "nn":
        (m, k), (k2, n) = a.shape, b.shape
    elif dims == "nt":
        (m, k), (n, k2) = a.shape, b.shape
    else:
        (k, m), (k2, n) = a.shape, b.shape
    assert k == k2, (name, a.shape, b.shape)
    if dims == "tn":
        bm, bn, bk = _pick_block(m, MM_COLS), _pick_block(n, MM_COLS), _pick_block(k, MM_DEPTH_TN)
    else:
        bm, bn, bk = _pick_block(m, MM_ROWS), _pick_block(n, MM_COLS), _pick_block(k, MM_DEPTH)
    nk = k // bk
    if dims == "tn":
        a_spec = pl.BlockSpec((bk, bm), lambda j, i, kk: (kk, i))
    else:
        a_spec = pl.BlockSpec((bm, bk), lambda j, i, kk: (i, kk))
    if dims == "nt":
        b_spec = pl.BlockSpec((bn, bk), lambda j, i, kk: (j, kk))
    else:
        b_spec = pl.BlockSpec((bk, bn), lambda j, i, kk: (kk, j))
    extra_specs = []
    for arr, kind in extras:
        if kind == "tile":
            assert arr.shape == (m, n), (name, arr.shape)
            extra_specs.append(pl.BlockSpec((bm, bn), lambda j, i, kk: (i, j)))
        else:
            assert arr.shape == (m, LANES), (name, arr.shape)
            extra_specs.append(pl.BlockSpec((bm, LANES), lambda j, i, kk: (i, 0)))
    n_extra = len(extras)
    n_out = len(out_dtypes)
    dot = {"nn": _dot_nn, "nt": _dot_nt, "tn": _dot_tn}[dims]

    def body(*refs):
        a_ref, b_ref = refs[0], refs[1]
        extra_refs = refs[2:2 + n_extra]
        out_refs = refs[2 + n_extra:2 + n_extra + n_out]

        def finish(acc):
            outs = (acc,) if epilogue is None else epilogue(acc, *[r[...] for r in extra_refs])
            for o_ref, o in zip(out_refs, outs):
                o_ref[...] = o.astype(o_ref.dtype)

        part = dot(a_ref[...].astype(BF16), b_ref[...].astype(BF16))
        if nk == 1:
            finish(part)
            return
        acc_ref = refs[-1]
        kk = pl.program_id(2)

        @pl.when(kk == 0)
        def _():
            acc_ref[...] = part

        @pl.when(kk > 0)
        def _():
            acc_ref[...] += part

        @pl.when(kk == nk - 1)
        def _():
            finish(acc_ref[...])

    outs = pl.pallas_call(
        body, name=name, grid=(n // bn, m // bm, nk),
        in_specs=[a_spec, b_spec] + extra_specs,
        out_specs=[pl.BlockSpec((bm, bn), lambda j, i, kk: (i, j)) for _ in range(n_out)],
        out_shape=[jax.ShapeDtypeStruct((m, n), dt) for dt in out_dtypes],
        scratch_shapes=[pltpu.VMEM((bm, bn), F32)] if nk > 1 else [],
        compiler_params=pltpu.CompilerParams(dimension_semantics=("parallel", "parallel", "arbitrary"),
                                             vmem_limit_bytes=VMEM_LIMIT),
    )(a, b, *[arr for arr, _ in extras])
    return outs[0] if n_out == 1 else outs


def _epi_add(acc, res):
    return (res + acc,)


def _epi_relu2(acc):
    r = jnp.maximum(acc, 0.0)
    return acc, r * r


def _epi_relu2_grad(acc, u):
    return (acc * (2.0 * jnp.maximum(u.astype(F32), 0.0)),)


def _rope_slab(t, cos_t, sin_t):
    lane = lax.broadcasted_iota(jnp.int32, t.shape, 1)
    partner = jnp.where(lane < ROPE_LO + ROPE_HALF, pltpu.roll(t, LANES - ROPE_HALF, 1), pltpu.roll(t, ROPE_HALF, 1))
    return t * cos_t + partner * sin_t


def _rope_slab_bwd(d, cos_t, sin_t):
    ds = d * sin_t
    lane = lax.broadcasted_iota(jnp.int32, d.shape, 1)
    partner = jnp.where(lane < ROPE_LO + ROPE_HALF, pltpu.roll(ds, LANES - ROPE_HALF, 1), pltpu.roll(ds, ROPE_HALF, 1))
    in_rope = (lane >= ROPE_LO) & (lane < ROPE_LO + MLA_ROPE)
    return d * cos_t + jnp.where(in_rope, partner, 0.0)


def _epi_rope_heads(acc, cos_t, sin_t):
    slabs = [_rope_slab(acc[:, j * LANES:(j + 1) * LANES], cos_t, sin_t) for j in range(acc.shape[1] // LANES)]
    return (jnp.concatenate(slabs, axis=1),)


def _row_block(s):
    return min(512, s)


def _rms_fwd(x, g, *, name):
    s, d = x.shape
    bm = _row_block(s)

    def body(x_ref, g_ref, o_ref):
        xv = x_ref[...]
        r = lax.rsqrt(jnp.mean(xv * xv, axis=-1, keepdims=True) + NORM_EPS)
        o_ref[...] = ((xv * r) * g_ref[...]).astype(o_ref.dtype)

    return pl.pallas_call(
        body, name=name, grid=(s // bm,),
        in_specs=[pl.BlockSpec((bm, d), lambda i: (i, 0)), pl.BlockSpec((1, d), lambda i: (0, 0))],
        out_specs=pl.BlockSpec((bm, d), lambda i: (i, 0)),
        out_shape=jax.ShapeDtypeStruct((s, d), BF16),
        compiler_params=pltpu.CompilerParams(dimension_semantics=("parallel",), vmem_limit_bytes=VMEM_LIMIT),
    )(x, g.reshape(1, d))


def _rms_bwd_math(xv, gv, dy):
    r = lax.rsqrt(jnp.mean(xv * xv, axis=-1, keepdims=True) + NORM_EPS)
    xhat = xv * r
    dyg = dy * gv
    mdot = jnp.mean(dyg * xhat, axis=-1, keepdims=True)
    dx = r * (dyg - xhat * mdot)
    dg = jnp.sum(dy * xhat, axis=0, keepdims=True)
    return dx, dg


def _rms_bwd(x, g, dy, dres, *, name):
    s, d = x.shape
    bm = _row_block(s)
    has_res = dres is not None

    def body(*refs):
        x_ref, g_ref, dy_ref = refs[:3]
        dres_ref = refs[3] if has_res else None
        dx_ref, dxb_ref, dg_ref = refs[-3:]
        dx, dg = _rms_bwd_math(x_ref[...], g_ref[...], dy_ref[...].astype(F32))
        if has_res:
            dx = dx + dres_ref[...]
        dx_ref[...] = dx
        dxb_ref[...] = dx.astype(BF16)

        @pl.when(pl.program_id(0) == 0)
        def _():
            dg_ref[...] = jnp.zeros_like(dg_ref)

        dg_ref[...] += dg

    row = pl.BlockSpec((bm, d), lambda i: (i, 0))
    vec = pl.BlockSpec((1, d), lambda i: (0, 0))
    ins = [x, g.reshape(1, d), dy] + ([dres] if has_res else [])
    return pl.pallas_call(
        body, name=name, grid=(s // bm,),
        in_specs=[row, vec, row] + ([row] if has_res else []),
        out_specs=[row, row, vec],
        out_shape=[jax.ShapeDtypeStruct((s, d), F32), jax.ShapeDtypeStruct((s, d), BF16),
                   jax.ShapeDtypeStruct((1, d), F32)],
        compiler_params=pltpu.CompilerParams(dimension_semantics=("arbitrary",), vmem_limit_bytes=VMEM_LIMIT),
    )(*ins)


def _loss_bwd(x, g, target, *, name):
    s, d = x.shape
    bm = _row_block(s)

    def body(x_ref, g_ref, t_ref, loss_ref, dx_ref, dxb_ref, dg_ref):
        xv, gv = x_ref[...], g_ref[...]
        r = lax.rsqrt(jnp.mean(xv * xv, axis=-1, keepdims=True) + NORM_EPS)
        err = (xv * r) * gv - t_ref[...]
        dx, dg = _rms_bwd_math(xv, gv, err * (1.0 / d))
        dx_ref[...] = dx
        dxb_ref[...] = dx.astype(BF16)

        @pl.when(pl.program_id(0) == 0)
        def _():
            dg_ref[...] = jnp.zeros_like(dg_ref)
            loss_ref[...] = jnp.zeros_like(loss_ref)

        dg_ref[...] += dg
        loss_ref[...] += jnp.sum(jnp.mean(err * err, axis=-1, keepdims=True), axis=0, keepdims=True) * 0.5

    row = pl.BlockSpec((bm, d), lambda i: (i, 0))
    vec = pl.BlockSpec((1, d), lambda i: (0, 0))
    return pl.pallas_call(
        body, name=name, grid=(s // bm,),
        in_specs=[row, vec, row],
        out_specs=[pl.BlockSpec((8, LANES), lambda i: (0, 0)), row, row, vec],
        out_shape=[jax.ShapeDtypeStruct((8, LANES), F32), jax.ShapeDtypeStruct((s, d), F32),
                   jax.ShapeDtypeStruct((s, d), BF16), jax.ShapeDtypeStruct((1, d), F32)],
        compiler_params=pltpu.CompilerParams(dimension_semantics=("arbitrary",), vmem_limit_bytes=VMEM_LIMIT),
    )(x, g.reshape(1, d), target)


def _kv_prep(down, g, cos_t, sin_t, *, name):
    s, w = down.shape
    bm = _row_block(s)

    def body(d_ref, g_ref, c_ref, s_ref, o_ref):
        lat = d_ref[:, :MLA_KV_RANK]
        r = lax.rsqrt(jnp.mean(lat * lat, axis=-1, keepdims=True) + NORM_EPS)
        o_ref[:, :MLA_KV_RANK] = ((lat * r) * g_ref[...]).astype(BF16)
        o_ref[:, MLA_KV_RANK:] = _rope_slab(d_ref[:, MLA_KV_RANK:], c_ref[...], s_ref[...]).astype(BF16)

    row = pl.BlockSpec((bm, w), lambda i: (i, 0))
    tab = pl.BlockSpec((bm, LANES), lambda i: (i, 0))
    return pl.pallas_call(
        body, name=name, grid=(s // bm,),
        in_specs=[row, pl.BlockSpec((1, MLA_KV_RANK), lambda i: (0, 0)), tab, tab],
        out_specs=row, out_shape=jax.ShapeDtypeStruct((s, w), BF16),
        compiler_params=pltpu.CompilerParams(dimension_semantics=("parallel",), vmem_limit_bytes=VMEM_LIMIT),
    )(down, g.reshape(1, MLA_KV_RANK), cos_t, sin_t)


def _kv_prep_bwd(down, g, cos_t, sin_t, dcat, *, name):
    s, w = down.shape
    bm = _row_block(s)

    def body(d_ref, g_ref, c_ref, s_ref, dc_ref, o_ref, dg_ref):
        dlat, dg = _rms_bwd_math(d_ref[:, :MLA_KV_RANK], g_ref[...], dc_ref[:, :MLA_KV_RANK])
        o_ref[:, :MLA_KV_RANK] = dlat.astype(BF16)
        o_ref[:, MLA_KV_RANK:] = _rope_slab_bwd(dc_ref[:, MLA_KV_RANK:], c_ref[...], s_ref[...]).astype(BF16)

        @pl.when(pl.program_id(0) == 0)
        def _():
            dg_ref[...] = jnp.zeros_like(dg_ref)

        dg_ref[...] += dg

    row = pl.BlockSpec((bm, w), lambda i: (i, 0))
    tab = pl.BlockSpec((bm, LANES), lambda i: (i, 0))
    vec = pl.BlockSpec((1, MLA_KV_RANK), lambda i: (0, 0))
    return pl.pallas_call(
        body, name=name, grid=(s // bm,),
        in_specs=[row, vec, tab, tab, row],
        out_specs=[row, vec],
        out_shape=[jax.ShapeDtypeStruct((s, w), BF16), jax.ShapeDtypeStruct((1, MLA_KV_RANK), F32)],
        compiler_params=pltpu.CompilerParams(dimension_semantics=("arbitrary",), vmem_limit_bytes=VMEM_LIMIT),
    )(down, g.reshape(1, MLA_KV_RANK), cos_t, sin_t, dcat)


def _split_bf16(v):
    hi = v.astype(BF16)
    lo = (v - hi.astype(F32)).astype(BF16)
    return hi, lo


def _suffix_matrices(n):
    row = lax.broadcasted_iota(jnp.int32, (n, n), 0)
    col = lax.broadcasted_iota(jnp.int32, (n, n), 1)
    strict = (row > col).astype(BF16)
    incl = (row >= col).astype(BF16)
    return jnp.concatenate([strict, strict], axis=0), jnp.concatenate([incl, incl], axis=0)


def _suffix_sum(v, matrix):
    hi, lo = _split_bf16(v)
    return _dot_nn(jnp.concatenate([hi, lo], axis=1), matrix)


def _block_positions(qi, kb, bq, bk):
    row = qi * bq + lax.broadcasted_iota(jnp.int32, (bq, bk), 0)
    col = kb * bk + lax.broadcasted_iota(jnp.int32, (bq, bk), 1)
    return row, col


def _att_blocks(s):
    bq, bk = min(ATT_Q_BLOCK, s), min(ATT_K_BLOCK, s)
    return bq, bk, s // bq, bq // bk


def _sweep(qi, ratio, step):
    for d in range(ratio):
        step((qi + 1) * ratio - 1 - d, True)

    def loop(i, carry):
        step(qi * ratio - 1 - i, False)
        return carry

    lax.fori_loop(0, qi * ratio, loop, 0)


def _sb_scores(q, k, causal):
    z = _dot_nt(q, k)
    t = jnp.exp(-jnp.abs(z))
    sp = jnp.log(1.0 + t)
    lb = jnp.minimum(z, 0.0) - sp
    lk = -jnp.maximum(z, 0.0) - sp
    if causal is not None:
        lk = jnp.where(causal, lk, 0.0)
    return z, t, lb, lk


def _sb_fwd(qkv, heads, *, name):
    s = qkv.shape[0]
    bq, bk, nq, ratio = _att_blocks(s)

    def body(q_ref, k_ref, v_ref, o_ref, acc_ref, c_ref):
        qi = pl.program_id(1)
        q = q_ref[...] * SB_SCALE
        m_strict, _ = _suffix_matrices(bk)
        acc_ref[...] = jnp.zeros_like(acc_ref)
        c_ref[...] = jnp.zeros_like(c_ref)

        def step(kb, masked):
            rows = pl.ds(pl.multiple_of(kb * bk, bk), bk)
            k, v = k_ref[rows, :], v_ref[rows, :]
            causal = None
            if masked:
                row, col = _block_positions(qi, kb, bq, bk)
                causal = col < row
            _, _, lb, lk = _sb_scores(q, k, causal)
            c = c_ref[...]
            w = jnp.exp(lb + _suffix_sum(lk, m_strict) + jnp.tile(c, (1, bk // LANES)))
            if masked:
                w = jnp.where(causal, w, 0.0)
            acc_ref[...] += _dot_nn(w.astype(BF16), v)
            c_ref[...] = c + jnp.sum(lk, axis=-1, keepdims=True)

        _sweep(qi, ratio, step)
        o_ref[...] = acc_ref[...].astype(o_ref.dtype)

    return pl.pallas_call(
        body, name=name, grid=(heads, nq),
        in_specs=[pl.BlockSpec((bq, LANES), lambda h, i: (i, h)),
                  pl.BlockSpec((s, LANES), lambda h, i: (0, heads + h)),
                  pl.BlockSpec((s, LANES), lambda h, i: (0, 2 * heads + h))],
        out_specs=pl.BlockSpec((bq, LANES), lambda h, i: (i, h)),
        out_shape=jax.ShapeDtypeStruct((s, heads * LANES), F32),
        scratch_shapes=[pltpu.VMEM((bq, LANES), F32), pltpu.VMEM((bq, LANES), F32)],
        compiler_params=pltpu.CompilerParams(dimension_semantics=("parallel", "arbitrary"),
                                             vmem_limit_bytes=VMEM_LIMIT),
    )(qkv, qkv, qkv)


def _sb_bwd(qkv, o, do, heads, *, name):
    s = qkv.shape[0]
    bq, bk, nq, ratio = _att_blocks(s)

    def body(q_ref, k_ref, v_ref, o_ref, do_ref, dq_ref, dk_ref, dv_ref, dq_acc, dk_acc, dv_acc, c_ref, e_ref):
        qi = pl.program_id(1)

        @pl.when(qi == 0)
        def _():
            dk_acc[...] = jnp.zeros_like(dk_acc)
            dv_acc[...] = jnp.zeros_like(dv_acc)

        q = q_ref[...] * SB_SCALE
        do = do_ref[...]
        total = jnp.sum(do.astype(F32) * o_ref[...].astype(F32), axis=-1, keepdims=True)
        m_strict, m_incl = _suffix_matrices(bk)
        dq_acc[...] = jnp.zeros_like(dq_acc)
        c_ref[...] = jnp.zeros_like(c_ref)
        e_ref[...] = jnp.broadcast_to(total, e_ref.shape)
        reps = (1, bk // LANES)

        def step(kb, masked):
            rows = pl.ds(pl.multiple_of(kb * bk, bk), bk)
            k, v = k_ref[rows, :], v_ref[rows, :]
            causal = None
            if masked:
                row, col = _block_positions(qi, kb, bq, bk)
                causal = col < row
            z, t, lb, lk = _sb_scores(q, k, causal)
            c = c_ref[...]
            w = jnp.exp(lb + _suffix_sum(lk, m_strict) + jnp.tile(c, reps))
            if masked:
                w = jnp.where(causal, w, 0.0)
            wb = w.astype(BF16)
            g = wb.astype(F32) * _dot_nt(do, v)
            e = e_ref[...]
            g_left = jnp.tile(e, reps) - _suffix_sum(g, m_incl)
            r = 1.0 / (1.0 + t)
            tr = t * r
            pos = z >= 0.0
            beta = jnp.where(pos, r, tr)
            one_minus_beta = jnp.where(pos, tr, r)
            da = g * one_minus_beta - beta * g_left
            if masked:
                da = jnp.where(causal, da, 0.0)
            dab = da.astype(BF16)
            dq_acc[...] += _dot_nn(dab, k)
            dk_acc[rows, :] += _dot_tn(dab, q)
            dv_acc[rows, :] += _dot_tn(wb, do)
            e_ref[...] = e - jnp.sum(g, axis=-1, keepdims=True)
            c_ref[...] = c + jnp.sum(lk, axis=-1, keepdims=True)

        _sweep(qi, ratio, step)
        dq_ref[...] = (dq_acc[...] * SB_SCALE).astype(dq_ref.dtype)

        @pl.when(qi == nq - 1)
        def _():
            dk_ref[...] = dk_acc[...].astype(dk_ref.dtype)
            dv_ref[...] = dv_acc[...].astype(dv_ref.dtype)

    blk = pl.BlockSpec((bq, LANES), lambda h, i: (i, h))
    full = pl.BlockSpec((s, LANES), lambda h, i: (0, h))
    shape = jax.ShapeDtypeStruct((s, heads * LANES), BF16)
    return pl.pallas_call(
        body, name=name, grid=(heads, nq),
        in_specs=[blk,
                  pl.BlockSpec((s, LANES), lambda h, i: (0, heads + h)),
                  pl.BlockSpec((s, LANES), lambda h, i: (0, 2 * heads + h)),
                  blk, blk],
        out_specs=[blk, full, full],
        out_shape=[shape, shape, shape],
        scratch_shapes=[pltpu.VMEM((bq, LANES), F32), pltpu.VMEM((s, LANES), F32), pltpu.VMEM((s, LANES), F32),
                        pltpu.VMEM((bq, LANES), F32), pltpu.VMEM((bq, LANES), F32)],
        compiler_params=pltpu.CompilerParams(dimension_semantics=("arbitrary", "arbitrary"),
                                             vmem_limit_bytes=VMEM_LIMIT),
    )(qkv, qkv, qkv, o, do)


def _chunk_allowed(qi, kb, bq, bk):
    row, col = _block_positions(qi, kb, bq, bk)
    return (col // CHUNK) <= (row // CHUNK)


def _mla_fwd(q, kv, heads, *, name):
    s = q.shape[0]
    bq, bk, nq, ratio = _att_blocks(s)
    reps = (1, bk // LANES)

    def body(q_ref, k_ref, v_ref, o_ref, lse_ref, acc_ref, m_ref, l_ref):
        qi = pl.program_id(1)
        qv = q_ref[...]
        acc_ref[...] = jnp.zeros_like(acc_ref)
        m_ref[...] = jnp.full_like(m_ref, NEG_BIG)
        l_ref[...] = jnp.zeros_like(l_ref)

        def step(kb, masked):
            rows = pl.ds(pl.multiple_of(kb * bk, bk), bk)
            k, v = k_ref[rows, :], v_ref[rows, :]
            sc = _dot_nt(qv, k) * MLA_SCALE
            if masked:
                allowed = _chunk_allowed(qi, kb, bq, bk)
                sc = jnp.where(allowed, sc, NEG_BIG)
            m_old = m_ref[...]
            m_new = jnp.maximum(m_old, jnp.max(sc, axis=-1, keepdims=True))
            p = jnp.exp(sc - jnp.tile(m_new, reps))
            if masked:
                p = jnp.where(allowed, p, 0.0)
            alpha = jnp.exp(m_old - m_new)
            l_ref[...] = alpha * l_ref[...] + jnp.sum(p, axis=-1, keepdims=True)
            acc_ref[...] = alpha * acc_ref[...] + _dot_nn(p.astype(BF16), v)
            m_ref[...] = m_new

        _sweep(qi, ratio, step)
        o_ref[...] = (acc_ref[...] / l_ref[...]).astype(o_ref.dtype)
        lse_ref[...] = m_ref[...] + jnp.log(l_ref[...])

    blk = pl.BlockSpec((bq, LANES), lambda h, i: (i, h))
    return pl.pallas_call(
        body, name=name, grid=(heads, nq),
        in_specs=[blk,
                  pl.BlockSpec((s, LANES), lambda h, i: (0, h)),
                  pl.BlockSpec((s, LANES), lambda h, i: (0, heads + h))],
        out_specs=[blk, blk],
        out_shape=[jax.ShapeDtypeStruct((s, heads * LANES), BF16), jax.ShapeDtypeStruct((s, heads * LANES), F32)],
        scratch_shapes=[pltpu.VMEM((bq, LANES), F32), pltpu.VMEM((bq, LANES), F32), pltpu.VMEM((bq, LANES), F32)],
        compiler_params=pltpu.CompilerParams(dimension_semantics=("parallel", "arbitrary"),
                                             vmem_limit_bytes=VMEM_LIMIT),
    )(q, kv, kv)


def _mla_bwd(q, kv, o, do, lse, cos_t, sin_t, dkv_init, heads, *, name):
    s = q.shape[0]
    bq, bk, nq, ratio = _att_blocks(s)
    reps = (1, bk // LANES)
    has_init = dkv_init is not None

    def body(*refs):
        q_ref, k_ref, v_ref, o_ref, do_ref, lse_ref, c_ref, s_ref = refs[:8]
        ki_ref, vi_ref = (refs[8], refs[9]) if has_init else (None, None)
        dq_ref, dk_ref, dv_ref, dq_acc, dk_acc, dv_acc = refs[-6:]
        qi = pl.program_id(1)

        @pl.when(qi == 0)
        def _():
            if has_init:
                dk_acc[...] = ki_ref[...].astype(F32)
                dv_acc[...] = vi_ref[...].astype(F32)
            else:
                dk_acc[...] = jnp.zeros_like(dk_acc)
                dv_acc[...] = jnp.zeros_like(dv_acc)

        qv = q_ref[...]
        do = do_ref[...]
        delta = jnp.sum(do.astype(F32) * o_ref[...].astype(F32), axis=-1, keepdims=True)
        lse_wide = jnp.tile(lse_ref[...], reps)
        dq_acc[...] = jnp.zeros_like(dq_acc)

        def step(kb, masked):
            rows = pl.ds(pl.multiple_of(kb * bk, bk), bk)
            k, v = k_ref[rows, :], v_ref[rows, :]
            sc = _dot_nt(qv, k) * MLA_SCALE
            p = jnp.exp(sc - lse_wide)
            if masked:
                p = jnp.where(_chunk_allowed(qi, kb, bq, bk), p, 0.0)
            ds = (p * (_dot_nt(do, v) - delta) * MLA_SCALE).astype(BF16)
            dq_acc[...] += _dot_nn(ds, k)
            dk_acc[rows, :] += _dot_tn(ds, qv)
            dv_acc[rows, :] += _dot_tn(p.astype(BF16), do)

        _sweep(qi, ratio, step)
        dq_ref[...] = _rope_slab_bwd(dq_acc[...], c_ref[...], s_ref[...]).astype(dq_ref.dtype)

        @pl.when(qi == nq - 1)
        def _():
            dk_ref[...] = dk_acc[...].astype(dk_ref.dtype)
            dv_ref[...] = dv_acc[...].astype(dv_ref.dtype)

    blk = pl.BlockSpec((bq, LANES), lambda h, i: (i, h))
    tab = pl.BlockSpec((bq, LANES), lambda h, i: (i, 0))
    k_full = pl.BlockSpec((s, LANES), lambda h, i: (0, h))
    v_full = pl.BlockSpec((s, LANES), lambda h, i: (0, heads + h))
    shape = jax.ShapeDtypeStruct((s, heads * LANES), BF16)
    ins = [q, kv, kv, o, do, lse, cos_t, sin_t] + ([dkv_init, dkv_init] if has_init else [])
    dq, dk, dv = pl.pallas_call(
        body, name=name, grid=(heads, nq),
        in_specs=[blk, k_full, v_full, blk, blk, blk, tab, tab] + ([k_full, v_full] if has_init else []),
        out_specs=[blk, k_full, k_full],
        out_shape=[shape, shape, shape],
        scratch_shapes=[pltpu.VMEM((bq, LANES), F32), pltpu.VMEM((s, LANES), F32), pltpu.VMEM((s, LANES), F32)],
        compiler_params=pltpu.CompilerParams(dimension_semantics=("arbitrary", "arbitrary"),
                                             vmem_limit_bytes=VMEM_LIMIT),
    )(*ins)
    return dq, jnp.concatenate([dk, dv], axis=1)


def _pad_last(a, width):
    return jnp.pad(a, [(0, 0)] * (a.ndim - 1) + [(0, width - a.shape[-1])])


def _pad_qkv(w, heads):
    d = w.shape[0]
    return _pad_last(w.reshape(d, 3 * heads, SB_HEAD_DIM), LANES).reshape(d, 3 * heads * LANES)


def _unpad_qkv(g, heads):
    d = g.shape[0]
    return g.reshape(d, 3 * heads, LANES)[:, :, :SB_HEAD_DIM].reshape(d, 3 * heads * SB_HEAD_DIM)


def _pad_o(w, heads):
    d = w.shape[1]
    w = w.reshape(heads, SB_HEAD_DIM, d)
    return jnp.pad(w, [(0, 0), (0, LANES - SB_HEAD_DIM), (0, 0)]).reshape(heads * LANES, d)


def _unpad_o(g, heads):
    d = g.shape[1]
    return g.reshape(heads, LANES, d)[:, :SB_HEAD_DIM, :].reshape(heads * SB_HEAD_DIM, d)


def _pad_uq(w, heads):
    r = w.shape[0]
    return _pad_last(w.reshape(r, heads, MLA_NOPE + MLA_ROPE), LANES).reshape(r, heads * LANES)


def _unpad_uq(g, heads):
    r = g.shape[0]
    return g.reshape(r, heads, LANES)[:, :, :MLA_NOPE + MLA_ROPE].reshape(r, heads * (MLA_NOPE + MLA_ROPE))


def _pad_dkv(w):
    d = w.shape[0]
    rope = jnp.zeros((d, LANES), w.dtype).at[:, ROPE_LO:ROPE_LO + MLA_ROPE].set(w[:, MLA_KV_RANK:])
    return jnp.concatenate([w[:, :MLA_KV_RANK], rope], axis=1)


def _unpad_dkv(g):
    return jnp.concatenate([g[:, :MLA_KV_RANK], g[:, MLA_KV_RANK + ROPE_LO:MLA_KV_RANK + ROPE_LO + MLA_ROPE]], axis=1)


def _pad_ukv(w, heads):
    w = w.reshape(MLA_KV_RANK, heads, 2, MLA_NOPE)
    k_part = _pad_last(w[:, :, 0, :], LANES).reshape(MLA_KV_RANK, heads * LANES)
    v_part = _pad_last(w[:, :, 1, :], LANES).reshape(MLA_KV_RANK, heads * LANES)
    lane = jnp.arange(LANES)
    place = ((lane[:, None] == lane[None, :]) & (lane[:, None] >= ROPE_LO) & (lane[:, None] < ROPE_LO + MLA_ROPE))
    place = jnp.tile(place.astype(w.dtype), (1, heads))
    top = jnp.concatenate([k_part, v_part], axis=1)
    bottom = jnp.concatenate([place, jnp.zeros_like(place)], axis=1)
    return jnp.concatenate([top, bottom], axis=0)


def _unpad_ukv(g, heads):
    g = g[:MLA_KV_RANK]
    k_part = g[:, :heads * LANES].reshape(MLA_KV_RANK, heads, LANES)[:, :, :MLA_NOPE]
    v_part = g[:, heads * LANES:].reshape(MLA_KV_RANK, heads, LANES)[:, :, :MLA_V]
    return jnp.stack([k_part, v_part], axis=2).reshape(MLA_KV_RANK, heads * (MLA_NOPE + MLA_V))


def _rope_tables(positions):
    inv_freq = ROPE_THETA ** (-jnp.arange(0, MLA_ROPE, 2, dtype=F32) / MLA_ROPE)
    ang = positions.astype(F32)[:, None] * inv_freq
    cos, sin = jnp.cos(ang), jnp.sin(ang)
    s = positions.shape[0]
    cos_t = jnp.ones((s, LANES), F32).at[:, ROPE_LO:ROPE_LO + MLA_ROPE].set(jnp.concatenate([cos, cos], axis=1))
    sin_t = jnp.zeros((s, LANES), F32).at[:, ROPE_LO:ROPE_LO + MLA_ROPE].set(jnp.concatenate([-sin, sin], axis=1))
    return cos_t, sin_t


def _local_step(x, positions, target, w, norms):
    s, d = x.shape
    heads = d // SB_HEAD_DIM
    n_a = w["sb_w_qkv"].shape[0]
    n_b = w["mla_w_dq"].shape[0]
    depth = n_a + n_b
    cos_t, sin_t = _rope_tables(positions)

    wqkv = [_pad_qkv(w["sb_w_qkv"][l], heads) for l in range(n_a)]
    wo_a = [_pad_o(w["sb_w_o"][l], heads) for l in range(n_a)]
    wdkv = _pad_dkv(w["mla_w_dkv"])
    wkv = _pad_ukv(w["mla_w_ukv"], heads)
    wdq = [w["mla_w_dq"][j] for j in range(n_b)]
    wuq = [_pad_uq(w["mla_w_uq"][j], heads) for j in range(n_b)]
    wo_b = [_pad_o(w["mla_w_o"][j], heads) for j in range(n_b)]
    w1 = [w["mlp_w1"][l] for l in range(depth)]
    w2 = [w["mlp_w2"][l] for l in range(depth)]

    saved = []
    kv_saved = None
    kv = None
    for l in range(depth):
        t = f"l{l}"
        sv = {"x_in": x}
        h = _rms_fwd(x, norms["attn_norm"][l], name=f"{t}_attn_norm")
        sv["h"] = h
        if l < n_a:
            qkv = _mm(h, wqkv[l], name=f"{t}_qkv")
            o = _sb_fwd(qkv, heads, name=f"{t}_sb_fwd")
            sv["qkv"], sv["o"] = qkv, o
            x = _mm(o, wo_a[l], name=f"{t}_attn_out", epilogue=_epi_add, extras=[(x, "tile")], out_dtypes=(F32,))
        else:
            j = l - n_a
            if j == 0:
                hk = _rms_fwd(x, norms["kv_norm"], name="kv_norm")
                down = _mm(hk, wdkv, name="kv_down", out_dtypes=(F32,))
                cat = _kv_prep(down, norms["mla_kv_lat_norm"], cos_t, sin_t, name="kv_prep")
                kv = _mm(cat, wkv, name="kv_up")
                kv_saved = {"x_in": x, "hk": hk, "down": down, "cat": cat}
            cq0 = _mm(h, wdq[j], name=f"{t}_q_down", out_dtypes=(F32,))
            cq = _rms_fwd(cq0, norms["mla_q_lat_norm"][j], name=f"{t}_q_lat_norm")
            q = _mm(cq, wuq[j], name=f"{t}_q_up", epilogue=_epi_rope_heads, extras=[(cos_t, "row"), (sin_t, "row")])
            o, lse = _mla_fwd(q, kv, heads, name=f"{t}_mla_fwd")
            sv.update(cq0=cq0, cq=cq, q=q, o=o, lse=lse)
            x = _mm(o, wo_b[j], name=f"{t}_attn_out", epilogue=_epi_add, extras=[(x, "tile")], out_dtypes=(F32,))
        sv["x_mid"] = x
        h2 = _rms_fwd(x, norms["mlp_norm"][l], name=f"{t}_mlp_norm")
        u, a = _mm(h2, w1[l], name=f"{t}_mlp_up", epilogue=_epi_relu2, out_dtypes=(BF16, BF16))
        sv.update(h2=h2, u=u, a=a)
        x = _mm(a, w2[l], name=f"{t}_mlp_down", epilogue=_epi_add, extras=[(x, "tile")], out_dtypes=(F32,))
        saved.append(sv)

    loss_slab, dx, dxb, dg_final = _loss_bwd(x, norms["final_norm"], target, name="loss")
    loss = loss_slab[0, 0]

    g_attn_norm, g_mlp_norm = [None] * depth, [None] * depth
    g_qkv, g_o_a = [None] * n_a, [None] * n_a
    g_dq, g_uq, g_o_b, g_qlat = [None] * n_b, [None] * n_b, [None] * n_b, [None] * n_b
    g_w1, g_w2 = [None] * depth, [None] * depth
    dkv = None
    g_kv_norm = g_kv_lat = g_dkv = g_ukv = None

    for l in reversed(range(depth)):
        t = f"l{l}"
        sv = saved[l]
        du = _mm(dxb, w2[l], name=f"{t}_mlp_down_dx", dims="nt", epilogue=_epi_relu2_grad, extras=[(sv["u"], "tile")])
        g_w2[l] = _mm(sv["a"], dxb, name=f"{t}_mlp_down_dw", dims="tn", out_dtypes=(F32,))
        g_w1[l] = _mm(sv["h2"], du, name=f"{t}_mlp_up_dw", dims="tn", out_dtypes=(F32,))
        dh2 = _mm(du, w1[l], name=f"{t}_mlp_up_dx", dims="nt", out_dtypes=(F32,))
        dx, dxb, g_mlp_norm[l] = _rms_bwd(sv["x_mid"], norms["mlp_norm"][l], dh2, dx, name=f"{t}_mlp_norm_bwd")
        if l < n_a:
            do = _mm(dxb, wo_a[l], name=f"{t}_attn_out_dx", dims="nt")
            g_o_a[l] = _unpad_o(_mm(sv["o"], dxb, name=f"{t}_attn_out_dw", dims="tn", out_dtypes=(F32,)), heads)
            dq, dk, dv = _sb_bwd(sv["qkv"], sv["o"], do, heads, name=f"{t}_sb_bwd")
            dqkv = jnp.concatenate([dq, dk, dv], axis=1)
            g_qkv[l] = _unpad_qkv(_mm(sv["h"], dqkv, name=f"{t}_qkv_dw", dims="tn", out_dtypes=(F32,)), heads)
            dh = _mm(dqkv, wqkv[l], name=f"{t}_qkv_dx", dims="nt", out_dtypes=(F32,))
        else:
            j = l - n_a
            do = _mm(dxb, wo_b[j], name=f"{t}_attn_out_dx", dims="nt")
            g_o_b[j] = _unpad_o(_mm(sv["o"], dxb, name=f"{t}_attn_out_dw", dims="tn", out_dtypes=(F32,)), heads)
            dq, dkv = _mla_bwd(sv["q"], kv, sv["o"], do, sv["lse"], cos_t, sin_t, dkv, heads, name=f"{t}_mla_bwd")
            g_uq[j] = _unpad_uq(_mm(sv["cq"], dq, name=f"{t}_q_up_dw", dims="tn", out_dtypes=(F32,)), heads)
            dcq = _mm(dq, wuq[j], name=f"{t}_q_up_dx", dims="nt", out_dtypes=(F32,))
            _, dcq0, g_qlat[j] = _rms_bwd(sv["cq0"], norms["mla_q_lat_norm"][j], dcq, None, name=f"{t}_q_lat_norm_bwd")
            g_dq[j] = _mm(sv["h"], dcq0, name=f"{t}_q_down_dw", dims="tn", out_dtypes=(F32,))
            dh = _mm(dcq0, wdq[j], name=f"{t}_q_down_dx", dims="nt", out_dtypes=(F32,))
        dx, dxb, g_attn_norm[l] = _rms_bwd(sv["x_in"], norms["attn_norm"][l], dh, dx, name=f"{t}_attn_norm_bwd")
        if l == n_a:
            ks = kv_saved
            dcat = _mm(dkv, wkv, name="kv_up_dx", dims="nt", out_dtypes=(F32,))
            g_ukv = _unpad_ukv(_mm(ks["cat"], dkv, name="kv_up_dw", dims="tn", out_dtypes=(F32,)), heads)
            ddown, g_kv_lat = _kv_prep_bwd(ks["down"], norms["mla_kv_lat_norm"], cos_t, sin_t, dcat, name="kv_prep_bwd")
            g_dkv = _unpad_dkv(_mm(ks["hk"], ddown, name="kv_down_dw", dims="tn", out_dtypes=(F32,)))
            dhk = _mm(ddown, wdkv, name="kv_down_dx", dims="nt", out_dtypes=(F32,))
            dx, dxb, g_kv_norm = _rms_bwd(ks["x_in"], norms["kv_norm"], dhk, dx, name="kv_norm_bwd")

    grads = {
        "attn_norm": jnp.concatenate(g_attn_norm, axis=0), "mlp_norm": jnp.concatenate(g_mlp_norm, axis=0),
        "sb_w_qkv": jnp.stack(g_qkv), "sb_w_o": jnp.stack(g_o_a),
        "kv_norm": g_kv_norm[0], "mla_w_dkv": g_dkv, "mla_kv_lat_norm": g_kv_lat[0], "mla_w_ukv": g_ukv,
        "mla_w_dq": jnp.stack(g_dq), "mla_q_lat_norm": jnp.concatenate(g_qlat, axis=0),
        "mla_w_uq": jnp.stack(g_uq), "mla_w_o": jnp.stack(g_o_b),
        "mlp_w1": jnp.stack(g_w1), "mlp_w2": jnp.stack(g_w2), "final_norm": dg_final[0],
    }
    return loss, dx, grads


def _flat_rows(n_elems):
    per_block = FLAT_COLS * FLAT_ROW_BLOCK * 2
    return -(-n_elems // per_block) * FLAT_ROW_BLOCK * 2


def _pack(arrays, dtype):
    flat = jnp.concatenate([a.reshape(-1).astype(dtype) for a in arrays])
    rows = _flat_rows(flat.shape[0])
    flat = jnp.pad(flat, (0, rows * FLAT_COLS - flat.shape[0]))
    return flat.reshape(rows, FLAT_COLS)


def _unpack(flat, shapes):
    flat = flat.reshape(-1)
    out, off = [], 0
    for shp in shapes:
        n = 1
        for v in shp:
            n *= v
        out.append(flat[off:off + n].reshape(shp))
        off += n
    return out


def _pack_small(arrays):
    rows = []
    for a in arrays:
        a = a.reshape(-1, a.shape[-1]) if a.shape[-1] == FLAT_COLS else a.reshape(1, -1)
        rows.append(_pad_last(a, FLAT_COLS))
    flat = jnp.concatenate(rows, axis=0)
    return jnp.pad(flat, [(0, -flat.shape[0] % 8), (0, 0)])


def _unpack_small(flat, shapes):
    out, row = [], 0
    for shp in shapes:
        if shp[-1] == FLAT_COLS:
            n = 1
            for v in shp[:-1]:
                n *= v
            out.append(flat[row:row + n].reshape(shp))
            row += n
        else:
            n = 1
            for v in shp:
                n *= v
            out.append(flat[row, :n].reshape(shp))
            row += 1
    return out


def _other_chips(x, y):
    return [(1 - x, y), (x, 1 - y), (1 - x, 1 - y)]


def _all_gather_chips(flat, *, name):
    rows, cols = flat.shape

    def body(x_ref, out_ref, send_sems, recv_sems, pass_send_sems, pass_recv_sems, local_sem):
        x, y, c = lax.axis_index("x"), lax.axis_index("y"), lax.axis_index("c")
        me = 2 * x + y
        my_rows, sib_rows = _half_rows(rows)
        chips = _other_chips(x, y)
        mine = pltpu.make_async_copy(x_ref, out_ref.at[me], local_sem)
        mine.start()
        sends = []
        for k, (px, py) in enumerate(chips):
            cp = pltpu.make_async_remote_copy(src_ref=x_ref.at[my_rows, :], dst_ref=out_ref.at[me, my_rows, :],
                                              send_sem=send_sems.at[k], recv_sem=recv_sems.at[k],
                                              device_id=(px, py, c), device_id_type=MESH)
            cp.start()
            sends.append(cp)
        for k, (px, py) in enumerate(chips):
            landed = out_ref.at[2 * px + py, my_rows, :]
            pltpu.make_async_remote_copy(src_ref=landed, dst_ref=landed, send_sem=send_sems.at[k],
                                         recv_sem=recv_sems.at[k], device_id=(px, py, c),
                                         device_id_type=MESH).wait_recv()
            cp = pltpu.make_async_remote_copy(src_ref=landed, dst_ref=landed, send_sem=pass_send_sems.at[k],
                                              recv_sem=pass_recv_sems.at[k], device_id=_sibling(),
                                              device_id_type=MESH)
            cp.start()
            sends.append(cp)
        for k, (px, py) in enumerate(chips):
            passed = out_ref.at[2 * px + py, sib_rows, :]
            pltpu.make_async_remote_copy(src_ref=passed, dst_ref=passed, send_sem=pass_send_sems.at[k],
                                         recv_sem=pass_recv_sems.at[k], device_id=_sibling(),
                                         device_id_type=MESH).wait_recv()
        for cp in sends:
            cp.wait_send()
        mine.wait()

    return pl.pallas_call(
        body, name=name,
        in_specs=[pl.BlockSpec(memory_space=pltpu.HBM)],
        out_specs=pl.BlockSpec(memory_space=pltpu.HBM),
        out_shape=jax.ShapeDtypeStruct((N_CHIPS, rows, cols), flat.dtype),
        scratch_shapes=[pltpu.SemaphoreType.DMA((3,)), pltpu.SemaphoreType.DMA((3,)), pltpu.SemaphoreType.DMA((3,)),
                        pltpu.SemaphoreType.DMA((3,)), pltpu.SemaphoreType.DMA],
        compiler_params=pltpu.CompilerParams(has_side_effects=True),
    )(flat)


def _exchange_chips(parts, *, name):
    def body(g_ref, out_ref, send_sems, recv_sems, local_sem):
        x, y, c = lax.axis_index("x"), lax.axis_index("y"), lax.axis_index("c")
        me = 2 * x + y
        mine = pltpu.make_async_copy(g_ref.at[me], out_ref.at[me], local_sem)
        mine.start()
        sends = []
        for k, (px, py) in enumerate(_other_chips(x, y)):
            cp = pltpu.make_async_remote_copy(src_ref=g_ref.at[2 * px + py], dst_ref=out_ref.at[me],
                                              send_sem=send_sems.at[k], recv_sem=recv_sems.at[k],
                                              device_id=(px, py, c), device_id_type=MESH)
            cp.start()
            sends.append(cp)
        for k, (px, py) in enumerate(_other_chips(x, y)):
            pltpu.make_async_remote_copy(src_ref=g_ref.at[me], dst_ref=out_ref.at[2 * px + py],
                                         send_sem=send_sems.at[k], recv_sem=recv_sems.at[k],
                                         device_id=(px, py, c), device_id_type=MESH).wait_recv()
        for cp in sends:
            cp.wait_send()
        mine.wait()

    return pl.pallas_call(
        body, name=name,
        in_specs=[pl.BlockSpec(memory_space=pltpu.HBM)],
        out_specs=pl.BlockSpec(memory_space=pltpu.HBM),
        out_shape=jax.ShapeDtypeStruct(parts.shape, parts.dtype),
        scratch_shapes=[pltpu.SemaphoreType.DMA((3,)), pltpu.SemaphoreType.DMA((3,)), pltpu.SemaphoreType.DMA],
        compiler_params=pltpu.CompilerParams(has_side_effects=True),
    )(parts)


def _half_rows(rows):
    c = lax.axis_index("c")
    half = rows // 2
    return pl.ds(pl.multiple_of(c * half, 8), half), pl.ds(pl.multiple_of((1 - c) * half, 8), half)


def _sibling():
    return (lax.axis_index("x"), lax.axis_index("y"), 1 - lax.axis_index("c"))


def _pair_exchange(parts, *, name):
    n, rows, cols = parts.shape

    def body(p_ref, mine_ref, theirs_ref, send_sem, recv_sem, local_sem):
        my_rows, sib_rows = _half_rows(rows)
        keep = pltpu.make_async_copy(p_ref.at[:, my_rows, :], mine_ref, local_sem)
        keep.start()
        cp = pltpu.make_async_remote_copy(src_ref=p_ref.at[:, sib_rows, :], dst_ref=theirs_ref, send_sem=send_sem,
                                          recv_sem=recv_sem, device_id=_sibling(), device_id_type=MESH)
        cp.start()
        cp.wait()
        keep.wait()

    shape = jax.ShapeDtypeStruct((n, rows // 2, cols), parts.dtype)
    return pl.pallas_call(
        body, name=name,
        in_specs=[pl.BlockSpec(memory_space=pltpu.HBM)],
        out_specs=[pl.BlockSpec(memory_space=pltpu.HBM), pl.BlockSpec(memory_space=pltpu.HBM)],
        out_shape=[shape, shape],
        scratch_shapes=[pltpu.SemaphoreType.DMA, pltpu.SemaphoreType.DMA, pltpu.SemaphoreType.DMA],
        compiler_params=pltpu.CompilerParams(has_side_effects=True),
    )(parts)


def _pair_sum(mine, theirs, *, name):
    n, rows, cols = mine.shape

    def body(a_ref, b_ref, o_ref):
        o_ref[...] = (a_ref[...].astype(F32) + b_ref[...].astype(F32)).astype(o_ref.dtype)

    blk = pl.BlockSpec((n, FLAT_ROW_BLOCK, cols), lambda i: (0, i, 0))
    return pl.pallas_call(
        body, name=name, grid=(rows // FLAT_ROW_BLOCK,),
        in_specs=[blk, blk], out_specs=blk, out_shape=jax.ShapeDtypeStruct(mine.shape, mine.dtype),
        compiler_params=pltpu.CompilerParams(dimension_semantics=("parallel",), vmem_limit_bytes=VMEM_LIMIT),
    )(mine, theirs)


def _sum_chips(parts, *, name):
    _, rows, cols = parts.shape

    def body(p_ref, o_ref):
        o_ref[...] = ((p_ref[0].astype(F32) + p_ref[1].astype(F32)) + p_ref[2].astype(F32)) + p_ref[3].astype(F32)

    return pl.pallas_call(
        body, name=name, grid=(rows // FLAT_ROW_BLOCK,),
        in_specs=[pl.BlockSpec((N_CHIPS, FLAT_ROW_BLOCK, cols), lambda i: (0, i, 0))],
        out_specs=pl.BlockSpec((FLAT_ROW_BLOCK, cols), lambda i: (i, 0)),
        out_shape=jax.ShapeDtypeStruct((rows, cols), F32),
        compiler_params=pltpu.CompilerParams(dimension_semantics=("parallel",), vmem_limit_bytes=VMEM_LIMIT),
    )(parts)


def _join_cores(half, *, name):
    rows2, cols = half.shape

    def body(h_ref, out_ref, send_sem, recv_sem, local_sem):
        my_rows, sib_rows = _half_rows(2 * rows2)
        keep = pltpu.make_async_copy(h_ref, out_ref.at[my_rows, :], local_sem)
        keep.start()
        cp = pltpu.make_async_remote_copy(src_ref=h_ref, dst_ref=out_ref.at[my_rows, :], send_sem=send_sem,
                                          recv_sem=recv_sem, device_id=_sibling(), device_id_type=MESH)
        cp.start()
        cp.wait_send()
        pltpu.make_async_remote_copy(src_ref=h_ref, dst_ref=out_ref.at[sib_rows, :], send_sem=send_sem,
                                     recv_sem=recv_sem, device_id=_sibling(), device_id_type=MESH).wait_recv()
        keep.wait()

    return pl.pallas_call(
        body, name=name,
        in_specs=[pl.BlockSpec(memory_space=pltpu.HBM)],
        out_specs=pl.BlockSpec(memory_space=pltpu.HBM),
        out_shape=jax.ShapeDtypeStruct((2 * rows2, cols), half.dtype),
        scratch_shapes=[pltpu.SemaphoreType.DMA, pltpu.SemaphoreType.DMA, pltpu.SemaphoreType.DMA],
        compiler_params=pltpu.CompilerParams(has_side_effects=True),
    )(half)


def _all_reduce_small(v, *, name):
    rows, cols = v.shape
    flips = [(fx, fy, fc) for fx in (0, 1) for fy in (0, 1) for fc in (0, 1)][1:]

    def body(v_ref, out_ref, gath_ref, send_sems, recv_sems):
        x, y, c = lax.axis_index("x"), lax.axis_index("y"), lax.axis_index("c")
        me = 4 * x + 2 * y + c
        gath_ref[me] = v_ref[...]
        peers = [((1 - x) if fx else x, (1 - y) if fy else y, (1 - c) if fc else c) for fx, fy, fc in flips]
        sends = []
        for k, peer in enumerate(peers):
            cp = pltpu.make_async_remote_copy(src_ref=v_ref, dst_ref=gath_ref.at[me], send_sem=send_sems.at[k],
                                              recv_sem=recv_sems.at[k], device_id=peer, device_id_type=MESH)
            cp.start()
            sends.append(cp)
        for k, (px, py, pc) in enumerate(peers):
            pltpu.make_async_remote_copy(src_ref=v_ref, dst_ref=gath_ref.at[4 * px + 2 * py + pc],
                                         send_sem=send_sems.at[k], recv_sem=recv_sems.at[k],
                                         device_id=(px, py, pc), device_id_type=MESH).wait_recv()
        for cp in sends:
            cp.wait_send()
        total = gath_ref[0]
        for k in range(1, 8):
            total = total + gath_ref[k]
        out_ref[...] = total

    total, _ = pl.pallas_call(
        body, name=name,
        in_specs=[pl.BlockSpec(memory_space=pltpu.VMEM)],
        out_specs=[pl.BlockSpec(memory_space=pltpu.VMEM), pl.BlockSpec(memory_space=pltpu.VMEM)],
        out_shape=[jax.ShapeDtypeStruct((rows, cols), v.dtype), jax.ShapeDtypeStruct((8, rows, cols), v.dtype)],
        scratch_shapes=[pltpu.SemaphoreType.DMA((7,)), pltpu.SemaphoreType.DMA((7,))],
        compiler_params=pltpu.CompilerParams(has_side_effects=True),
    )(v)
    return total


def _adamw(w, g_parts, m, v, *, name):
    rows, cols = w.shape
    br = min(FLAT_ROW_BLOCK, rows)
    n_parts = len(g_parts)

    def body(*refs):
        w_ref = refs[0]
        g_refs = refs[1:1 + n_parts]
        m_ref, v_ref = refs[1 + n_parts], refs[2 + n_parts]
        g_out, d_out, m_out, v_out = refs[-4:]
        g = g_refs[0][...]
        for r in g_refs[1:]:
            g = g + r[...]
        m_new = ADAM_B1 * m_ref[...] + (1.0 - ADAM_B1) * g
        v_new = ADAM_B2 * v_ref[...] + (1.0 - ADAM_B2) * jnp.square(g)
        m_hat = m_new / (1.0 - ADAM_B1 ** ADAM_STEP)
        v_hat = v_new / (1.0 - ADAM_B2 ** ADAM_STEP)
        g_out[...] = g
        d_out[...] = -ADAM_LR * (m_hat / (jnp.sqrt(v_hat) + ADAM_EPS) + ADAM_WD * w_ref[...])
        m_out[...] = m_new
        v_out[...] = v_new

    blk = pl.BlockSpec((br, cols), lambda i: (i, 0))
    shape = jax.ShapeDtypeStruct((rows, cols), F32)
    return pl.pallas_call(
        body, name=name, grid=(rows // br,),
        in_specs=[blk] * (3 + n_parts), out_specs=[blk] * 4, out_shape=[shape] * 4,
        compiler_params=pltpu.CompilerParams(dimension_semantics=("parallel",), vmem_limit_bytes=VMEM_LIMIT),
    )(w, *g_parts, m, v)


def _assemble(gathered_shards, name):
    return jnp.concatenate(gathered_shards, axis=SHARD_AXIS[name])


def _chip_shard(full, name, j):
    axis = SHARD_AXIS[name]
    n = full.shape[axis] // N_CHIPS
    return lax.slice_in_dim(full, j * n, (j + 1) * n, axis=axis)


def kernel(x, positions, attn_norm, mlp_norm, sb_w_qkv, sb_w_o, kv_norm, mla_w_dkv, mla_kv_lat_norm, mla_w_ukv, mla_w_dq, mla_q_lat_norm, mla_w_uq, mla_w_o, mlp_w1, mlp_w2, final_norm, loss_target, m_attn_norm, m_mlp_norm, m_sb_w_qkv, m_sb_w_o, m_kv_norm, m_mla_w_dkv, m_mla_kv_lat_norm, m_mla_w_ukv, m_mla_w_dq, m_mla_q_lat_norm, m_mla_w_uq, m_mla_w_o, m_mlp_w1, m_mlp_w2, m_final_norm, v_attn_norm, v_mlp_norm, v_sb_w_qkv, v_sb_w_o, v_kv_norm, v_mla_w_dkv, v_mla_kv_lat_norm, v_mla_w_ukv, v_mla_w_dq, v_mla_q_lat_norm, v_mla_w_uq, v_mla_w_o, v_mlp_w1, v_mlp_w2, v_final_norm):
    weights = dict(attn_norm=attn_norm, mlp_norm=mlp_norm, sb_w_qkv=sb_w_qkv, sb_w_o=sb_w_o, kv_norm=kv_norm,
                   mla_w_dkv=mla_w_dkv, mla_kv_lat_norm=mla_kv_lat_norm, mla_w_ukv=mla_w_ukv, mla_w_dq=mla_w_dq,
                   mla_q_lat_norm=mla_q_lat_norm, mla_w_uq=mla_w_uq, mla_w_o=mla_w_o, mlp_w1=mlp_w1, mlp_w2=mlp_w2,
                   final_norm=final_norm)
    m_in = dict(attn_norm=m_attn_norm, mlp_norm=m_mlp_norm, sb_w_qkv=m_sb_w_qkv, sb_w_o=m_sb_w_o, kv_norm=m_kv_norm,
                mla_w_dkv=m_mla_w_dkv, mla_kv_lat_norm=m_mla_kv_lat_norm, mla_w_ukv=m_mla_w_ukv, mla_w_dq=m_mla_w_dq,
                mla_q_lat_norm=m_mla_q_lat_norm, mla_w_uq=m_mla_w_uq, mla_w_o=m_mla_w_o, mlp_w1=m_mlp_w1,
                mlp_w2=m_mlp_w2, final_norm=m_final_norm)
    v_in = dict(attn_norm=v_attn_norm, mlp_norm=v_mlp_norm, sb_w_qkv=v_sb_w_qkv, sb_w_o=v_sb_w_o, kv_norm=v_kv_norm,
                mla_w_dkv=v_mla_w_dkv, mla_kv_lat_norm=v_mla_kv_lat_norm, mla_w_ukv=v_mla_w_ukv, mla_w_dq=v_mla_w_dq,
                mla_q_lat_norm=v_mla_q_lat_norm, mla_w_uq=v_mla_w_uq, mla_w_o=v_mla_w_o, mlp_w1=v_mlp_w1,
                mlp_w2=v_mlp_w2, final_norm=v_final_norm)
    shard_shapes = [weights[n].shape for n in BIG_WEIGHTS]
    small_shapes = [weights[n].shape for n in SMALL_WEIGHTS]

    gathered = _all_gather_chips(_pack([weights[n] for n in BIG_WEIGHTS], BF16), name="weights_all_gather")
    per_chip = [_unpack(gathered[j], shard_shapes) for j in range(N_CHIPS)]
    full_w = {n: _assemble([per_chip[j][i] for j in range(N_CHIPS)], n) for i, n in enumerate(BIG_WEIGHTS)}
    norms = {n: weights[n] for n in SMALL_WEIGHTS}

    loss, dx, grads = _local_step(x[0], positions[0], loss_target[0], full_w, norms)
    loss = lax.psum(loss, ("x", "y", "c"))

    parts = jnp.stack([_pack([_chip_shard(grads[n], n, j) for n in BIG_WEIGHTS], BF16) for j in range(N_CHIPS)])
    mine, theirs = _pair_exchange(parts, name="grads_pair_exchange")
    chip_part = _pair_sum(mine, theirs, name="grads_pair_sum")
    received = _exchange_chips(chip_part, name="grads_exchange")
    g_half = _sum_chips(received, name="grads_sum_chips")
    g_sum = _join_cores(g_half, name="grads_join_cores")
    g_flat, d_flat, m_flat, v_flat = _adamw(
        _pack([weights[n] for n in BIG_WEIGHTS], F32), [g_sum],
        _pack([m_in[n] for n in BIG_WEIGHTS], F32), _pack([v_in[n] for n in BIG_WEIGHTS], F32), name="adamw_big")
    out_g = dict(zip(BIG_WEIGHTS, _unpack(g_flat, shard_shapes)))
    out_d = dict(zip(BIG_WEIGHTS, _unpack(d_flat, shard_shapes)))
    out_m = dict(zip(BIG_WEIGHTS, _unpack(m_flat, shard_shapes)))
    out_v = dict(zip(BIG_WEIGHTS, _unpack(v_flat, shard_shapes)))

    small_sum = _all_reduce_small(_pack_small([grads[n] for n in SMALL_WEIGHTS]), name="gains_all_reduce")
    sg, sd, sm, sv = _adamw(_pack_small([weights[n] for n in SMALL_WEIGHTS]), [small_sum],
                            _pack_small([m_in[n] for n in SMALL_WEIGHTS]),
                            _pack_small([v_in[n] for n in SMALL_WEIGHTS]), name="adamw_gains")
    out_g.update(zip(SMALL_WEIGHTS, _unpack_small(sg, small_shapes)))
    out_d.update(zip(SMALL_WEIGHTS, _unpack_small(sd, small_shapes)))
    out_m.update(zip(SMALL_WEIGHTS, _unpack_small(sm, small_shapes)))
    out_v.update(zip(SMALL_WEIGHTS, _unpack_small(sv, small_shapes)))

    return (loss, dx[None], *[out_g[n] for n in ALL_WEIGHTS], *[out_d[n] for n in ALL_WEIGHTS],
            *[out_m[n] for n in ALL_WEIGHTS], *[out_v[n] for n in ALL_WEIGHTS])
```

```python
import functools

import jax
import jax.numpy as jnp
from jax import lax
from jax.experimental import pallas as pl
from jax.experimental.pallas import tpu as pltpu

F32 = jnp.float32
BF16 = jnp.bfloat16

LANES = 128
SB_HEAD_DIM = 64
MLA_NOPE = 64
MLA_ROPE = 32
MLA_V = 64
MLA_Q_RANK = 384
MLA_KV_RANK = 256
CHUNK = 64
ROPE_THETA = 10000.0
NORM_EPS = 1e-6
SB_SCALE = SB_HEAD_DIM ** -0.5
LOG2E = 1.4426950408889634
MLA_SCALE = (MLA_NOPE + MLA_ROPE) ** -0.5
ROPE_LO = MLA_NOPE
ROPE_HALF = MLA_ROPE // 2
ATT_Q_BLOCK = 1024
ATT_K_BLOCK = 256
NEG_BIG = -1e30
VMEM_LIMIT = 56 * 1024 * 1024

ADAM_LR = 0.001
ADAM_B1 = 0.9
ADAM_B2 = 0.999
ADAM_EPS = 1e-08
ADAM_WD = 0.01
ADAM_STEP = 10

FLAT_COLS = 1024
FLAT_ROW_BLOCK = 256
N_CHIPS = 4
MESH = pl.DeviceIdType.MESH

BIG_WEIGHTS = ["sb_w_qkv", "sb_w_o", "mla_w_dkv", "mla_w_ukv", "mla_w_dq", "mla_w_uq", "mla_w_o", "mlp_w1", "mlp_w2"]
SHARD_AXIS = {"sb_w_qkv": 2, "sb_w_o": 1, "mla_w_dkv": 0, "mla_w_ukv": 1, "mla_w_dq": 1, "mla_w_uq": 2,
              "mla_w_o": 1, "mlp_w1": 2, "mlp_w2": 1}
SMALL_WEIGHTS = ["attn_norm", "mlp_norm", "kv_norm", "mla_kv_lat_norm", "mla_q_lat_norm", "final_norm"]
ALL_WEIGHTS = ["attn_norm", "mlp_norm", "sb_w_qkv", "sb_w_o", "kv_norm", "mla_w_dkv", "mla_kv_lat_norm", "mla_w_ukv",
               "mla_w_dq", "mla_q_lat_norm", "mla_w_uq", "mla_w_o", "mlp_w1", "mlp_w2", "final_norm"]


def _dot(a, b, dims):
    return lax.dot_general(a, b, (dims, ((), ())), preferred_element_type=F32)


def _dot_nn(a, b):
    return _dot(a, b, ((1,), (0,)))


def _dot_nt(a, b):
    return _dot(a, b, ((1,), (1,)))


def _dot_tn(a, b):
    return _dot(a, b, ((0,), (0,)))


def _pick_block(n, target):
    if n <= target:
        return n
    best = max(b for b in range(LANES, target + 1, LANES) if n % b == 0)
    return best


MM_ROWS = 512
MM_COLS = 1024
MM_DEPTH = 4096
MM_DEPTH_TN = 1024


def _mm(a, b, *, name, dims="nn", epilogue=None, extras=(), out_dtypes=(BF16,)):
    if dims == "nn":
        (m, k), (k2, n) = a.shape, b.shape
    elif dims == "nt":
        (m, k), (n, k2) = a.shape, b.shape
    else:
        (k, m), (k2, n) = a.shape, b.shape
    assert k == k2, (name, a.shape, b.shape)
    if dims == "tn":
        bm, bn, bk = _pick_block(m, MM_COLS), _pick_block(n, MM_COLS), _pick_block(k, MM_DEPTH_TN)
    else:
        bm, bn, bk = _pick_block(m, MM_ROWS), _pick_block(n, MM_COLS), _pick_block(k, MM_DEPTH)
    nk = k // bk
    if dims == "tn":
        a_spec = pl.BlockSpec((bk, bm), lambda j, i, kk: (kk, i))
    else:
        a_spec = pl.BlockSpec((bm, bk), lambda j, i, kk: (i, kk))
    if dims == "nt":
        b_spec = pl.BlockSpec((bn, bk), lambda j, i, kk: (j, kk))
    else:
        b_spec = pl.BlockSpec((bk, bn), lambda j, i, kk: (kk, j))
    extra_specs = []
    for arr, kind in extras:
        if kind == "tile":
            assert arr.shape == (m, n), (name, arr.shape)
            extra_specs.append(pl.BlockSpec((bm, bn), lambda j, i, kk: (i, j)))
        else:
            assert arr.shape == (m, LANES), (name, arr.shape)
            extra_specs.append(pl.BlockSpec((bm, LANES), lambda j, i, kk: (i, 0)))
    n_extra = len(extras)
    n_out = len(out_dtypes)
    dot = {"nn": _dot_nn, "nt": _dot_nt, "tn": _dot_tn}[dims]

    def body(*refs):
        a_ref, b_ref = refs[0], refs[1]
        extra_refs = refs[2:2 + n_extra]
        out_refs = refs[2 + n_extra:2 + n_extra + n_out]

        def finish(acc):
            outs = (acc,) if epilogue is None else epilogue(acc, *[r[...] for r in extra_refs])
            for o_ref, o in zip(out_refs, outs):
                o_ref[...] = o.astype(o_ref.dtype)

        part = dot(a_ref[...].astype(BF16), b_ref[...].astype(BF16))
        if nk == 1:
            finish(part)
            return
        acc_ref = refs[-1]
        kk = pl.program_id(2)

        @pl.when(kk == 0)
        def _():
            acc_ref[...] = part

        @pl.when(kk > 0)
        def _():
            acc_ref[...] += part

        @pl.when(kk == nk - 1)
        def _():
            finish(acc_ref[...])

    outs = pl.pallas_call(
        body, name=name, grid=(n // bn, m // bm, nk),
        in_specs=[a_spec, b_spec] + extra_specs,
        out_specs=[pl.BlockSpec((bm, bn), lambda j, i, kk: (i, j)) for _ in range(n_out)],
        out_shape=[jax.ShapeDtypeStruct((m, n), dt) for dt in out_dtypes],
        scratch_shapes=[pltpu.VMEM((bm, bn), F32)] if nk > 1 else [],
        compiler_params=pltpu.CompilerParams(dimension_semantics=("parallel", "parallel", "arbitrary"),
                                             vmem_limit_bytes=VMEM_LIMIT),
    )(a, b, *[arr for arr, _ in extras])
    return outs[0] if n_out == 1 else outs


def _epi_add(acc, res):
    return (res + acc,)


def _epi_relu2(acc):
    r = jnp.maximum(acc, 0.0)
    return acc, r * r


def _epi_relu2_grad(acc, u):
    return (acc * (2.0 * jnp.maximum(u.astype(F32), 0.0)),)


def _rope_slab(t, cos_t, sin_t):
    lane = lax.broadcasted_iota(jnp.int32, t.shape, 1)
    partner = jnp.where(lane < ROPE_LO + ROPE_HALF, pltpu.roll(t, LANES - ROPE_HALF, 1), pltpu.roll(t, ROPE_HALF, 1))
    return t * cos_t + partner * sin_t


def _rope_slab_bwd(d, cos_t, sin_t):
    ds = d * sin_t
    lane = lax.broadcasted_iota(jnp.int32, d.shape, 1)
    partner = jnp.where(lane < ROPE_LO + ROPE_HALF, pltpu.roll(ds, LANES - ROPE_HALF, 1), pltpu.roll(ds, ROPE_HALF, 1))
    in_rope = (lane >= ROPE_LO) & (lane < ROPE_LO + MLA_ROPE)
    return d * cos_t + jnp.where(in_rope, partner, 0.0)


def _epi_rope_heads(acc, cos_t, sin_t):
    slabs = [_rope_slab(acc[:, j * LANES:(j + 1) * LANES], cos_t, sin_t) for j in range(acc.shape[1] // LANES)]
    return (jnp.concatenate(slabs, axis=1),)


def _row_block(s):
    return min(512, s)


def _rms_fwd(x, g, *, name):
    s, d = x.shape
    bm = _row_block(s)

    def body(x_ref, g_ref, o_ref):
        xv = x_ref[...]
        r = lax.rsqrt(jnp.mean(xv * xv, axis=-1, keepdims=True) + NORM_EPS)
        o_ref[...] = ((xv * r) * g_ref[...]).astype(o_ref.dtype)

    return pl.pallas_call(
        body, name=name, grid=(s // bm,),
        in_specs=[pl.BlockSpec((bm, d), lambda i: (i, 0)), pl.BlockSpec((1, d), lambda i: (0, 0))],
        out_specs=pl.BlockSpec((bm, d), lambda i: (i, 0)),
        out_shape=jax.ShapeDtypeStruct((s, d), BF16),
        compiler_params=pltpu.CompilerParams(dimension_semantics=("parallel",), vmem_limit_bytes=VMEM_LIMIT),
    )(x, g.reshape(1, d))


def _rms_bwd_math(xv, gv, dy):
    r = lax.rsqrt(jnp.mean(xv * xv, axis=-1, keepdims=True) + NORM_EPS)
    xhat = xv * r
    dyg = dy * gv
    mdot = jnp.mean(dyg * xhat, axis=-1, keepdims=True)
    dx = r * (dyg - xhat * mdot)
    dg = jnp.sum(dy * xhat, axis=0, keepdims=True)
    return dx, dg


def _rms_bwd(x, g, dy, dres, *, name):
    s, d = x.shape
    bm = _row_block(s)
    has_res = dres is not None

    def body(*refs):
        x_ref, g_ref, dy_ref = refs[:3]
        dres_ref = refs[3] if has_res else None
        dx_ref, dxb_ref, dg_ref = refs[-3:]
        dx, dg = _rms_bwd_math(x_ref[...], g_ref[...], dy_ref[...].astype(F32))
        if has_res:
            dx = dx + dres_ref[...]
        dx_ref[...] = dx
        dxb_ref[...] = dx.astype(BF16)

        @pl.when(pl.program_id(0) == 0)
        def _():
            dg_ref[...] = jnp.zeros_like(dg_ref)

        dg_ref[...] += dg

    row = pl.BlockSpec((bm, d), lambda i: (i, 0))
    vec = pl.BlockSpec((1, d), lambda i: (0, 0))
    ins = [x, g.reshape(1, d), dy] + ([dres] if has_res else [])
    return pl.pallas_call(
        body, name=name, grid=(s // bm,),
        in_specs=[row, vec, row] + ([row] if has_res else []),
        out_specs=[row, row, vec],
        out_shape=[jax.ShapeDtypeStruct((s, d), F32), jax.ShapeDtypeStruct((s, d), BF16),
                   jax.ShapeDtypeStruct((1, d), F32)],
        compiler_params=pltpu.CompilerParams(dimension_semantics=("arbitrary",), vmem_limit_bytes=VMEM_LIMIT),
    )(*ins)


def _loss_bwd(x, g, target, *, name):
    s, d = x.shape
    bm = _row_block(s)

    def body(x_ref, g_ref, t_ref, loss_ref, dx_ref, dxb_ref, dg_ref):
        xv, gv = x_ref[...], g_ref[...]
        r = lax.rsqrt(jnp.mean(xv * xv, axis=-1, keepdims=True) + NORM_EPS)
        err = (xv * r) * gv - t_ref[...]
        dx, dg = _rms_bwd_math(xv, gv, err * (1.0 / d))
        dx_ref[...] = dx
        dxb_ref[...] = dx.astype(BF16)

        @pl.when(pl.program_id(0) == 0)
        def _():
            dg_ref[...] = jnp.zeros_like(dg_ref)
            loss_ref[...] = jnp.zeros_like(loss_ref)

        dg_ref[...] += dg
        loss_ref[...] += jnp.sum(jnp.mean(err * err, axis=-1, keepdims=True), axis=0, keepdims=True) * 0.5

    row = pl.BlockSpec((bm, d), lambda i: (i, 0))
    vec = pl.BlockSpec((1, d), lambda i: (0, 0))
    return pl.pallas_call(
        body, name=name, grid=(s // bm,),
        in_specs=[row, vec, row],
        out_specs=[pl.BlockSpec((8, LANES), lambda i: (0, 0)), row, row, vec],
        out_shape=[jax.ShapeDtypeStruct((8, LANES), F32), jax.ShapeDtypeStruct((s, d), F32),
                   jax.ShapeDtypeStruct((s, d), BF16), jax.ShapeDtypeStruct((1, d), F32)],
        compiler_params=pltpu.CompilerParams(dimension_semantics=("arbitrary",), vmem_limit_bytes=VMEM_LIMIT),
    )(x, g.reshape(1, d), target)


def _kv_prep(down, g, cos_t, sin_t, *, name):
    s, w = down.shape
    bm = _row_block(s)

    def body(d_ref, g_ref, c_ref, s_ref, o_ref):
        lat = d_ref[:, :MLA_KV_RANK]
        r = lax.rsqrt(jnp.mean(lat * lat, axis=-1, keepdims=True) + NORM_EPS)
        o_ref[:, :MLA_KV_RANK] = ((lat * r) * g_ref[...]).astype(BF16)
        o_ref[:, MLA_KV_RANK:] = _rope_slab(d_ref[:, MLA_KV_RANK:], c_ref[...], s_ref[...]).astype(BF16)

    row = pl.BlockSpec((bm, w), lambda i: (i, 0))
    tab = pl.BlockSpec((bm, LANES), lambda i: (i, 0))
    return pl.pallas_call(
        body, name=name, grid=(s // bm,),
        in_specs=[row, pl.BlockSpec((1, MLA_KV_RANK), lambda i: (0, 0)), tab, tab],
        out_specs=row, out_shape=jax.ShapeDtypeStruct((s, w), BF16),
        compiler_params=pltpu.CompilerParams(dimension_semantics=("parallel",), vmem_limit_bytes=VMEM_LIMIT),
    )(down, g.reshape(1, MLA_KV_RANK), cos_t, sin_t)


def _kv_prep_bwd(down, g, cos_t, sin_t, dcat, *, name):
    s, w = down.shape
    bm = _row_block(s)

    def body(d_ref, g_ref, c_ref, s_ref, dc_ref, o_ref, dg_ref):
        dlat, dg = _rms_bwd_math(d_ref[:, :MLA_KV_RANK], g_ref[...], dc_ref[:, :MLA_KV_RANK])
        o_ref[:, :MLA_KV_RANK] = dlat.astype(BF16)
        o_ref[:, MLA_KV_RANK:] = _rope_slab_bwd(dc_ref[:, MLA_KV_RANK:], c_ref[...], s_ref[...]).astype(BF16)

        @pl.when(pl.program_id(0) == 0)
        def _():
            dg_ref[...] = jnp.zeros_like(dg_ref)

        dg_ref[...] += dg

    row = pl.BlockSpec((bm, w), lambda i: (i, 0))
    tab = pl.BlockSpec((bm, LANES), lambda i: (i, 0))
    vec = pl.BlockSpec((1, MLA_KV_RANK), lambda i: (0, 0))
    return pl.pallas_call(
        body, name=name, grid=(s // bm,),
        in_specs=[row, vec, tab, tab, row],
        out_specs=[row, vec],
        out_shape=[jax.ShapeDtypeStruct((s, w), BF16), jax.ShapeDtypeStruct((1, MLA_KV_RANK), F32)],
        compiler_params=pltpu.CompilerParams(dimension_semantics=("arbitrary",), vmem_limit_bytes=VMEM_LIMIT),
    )(down, g.reshape(1, MLA_KV_RANK), cos_t, sin_t, dcat)


def _split_bf16(v):
    hi = v.astype(BF16)
    lo = (v - hi.astype(F32)).astype(BF16)
    return hi, lo


def _suffix_matrices(n):
    row = lax.broadcasted_iota(jnp.int32, (n, n), 0)
    col = lax.broadcasted_iota(jnp.int32, (n, n), 1)
    incl = (row >= col).astype(BF16)
    return (row > col).astype(BF16), jnp.concatenate([incl, incl], axis=0)


def _suffix_sum(v, matrix):
    hi, lo = _split_bf16(v)
    return _dot_nn(jnp.concatenate([hi, lo], axis=1), matrix)


def _block_positions(qi, kb, bq, bk):
    row = qi * bq + lax.broadcasted_iota(jnp.int32, (bq, bk), 0)
    col = kb * bk + lax.broadcasted_iota(jnp.int32, (bq, bk), 1)
    return row, col


def _att_blocks(s):
    bq, bk = min(ATT_Q_BLOCK, s), min(ATT_K_BLOCK, s)
    return bq, bk, s // bq, bq // bk


def _sweep(qi, ratio, step):
    for d in range(ratio):
        step((qi + 1) * ratio - 1 - d, True)

    def loop(i, carry):
        step(qi * ratio - 1 - i, False)
        return carry

    lax.fori_loop(0, qi * ratio, loop, 0)


def _sb_logs(q, k):
    z = _dot_nt(q, k) * LOG2E
    lb = jnp.minimum(z, 0.0) - jnp.log2(1.0 + jnp.exp2(-jnp.abs(z)))
    return lb, lb - z


def _sb_fwd(qkv, heads, *, name):
    s = qkv.shape[0]
    bq, bk, nq, ratio = _att_blocks(s)

    def body(q_ref, k_ref, v_ref, o_ref, acc_ref, c_ref):
        qi = pl.program_id(1)
        q = q_ref[...] * SB_SCALE
        m_strict, _ = _suffix_matrices(bk)
        acc_ref[...] = jnp.zeros_like(acc_ref)
        c_ref[...] = jnp.zeros_like(c_ref)

        def step(kb, masked):
            rows = pl.ds(pl.multiple_of(kb * bk, bk), bk)
            k, v = k_ref[rows, :], v_ref[rows, :]
            lb, lk = _sb_logs(q, k)
            if masked:
                row, col = _block_positions(qi, kb, bq, bk)
                causal = col < row
                lk = jnp.where(causal, lk, 0.0)
            c = c_ref[...]
            w = jnp.exp2(lb + _dot_nn(lk.astype(BF16), m_strict) + jnp.tile(c, (1, bk // LANES)))
            if masked:
                w = jnp.where(causal, w, 0.0)
            acc_ref[...] += _dot_nn(w.astype(BF16), v)
            c_ref[...] = c + jnp.sum(lk, axis=-1, keepdims=True)

        _sweep(qi, ratio, step)
        o_ref[...] = acc_ref[...].astype(o_ref.dtype)

    return pl.pallas_call(
        body, name=name, grid=(heads, nq),
        in_specs=[pl.BlockSpec((bq, LANES), lambda h, i: (i, h)),
                  pl.BlockSpec((s, LANES), lambda h, i: (0, heads + h)),
                  pl.BlockSpec((s, LANES), lambda h, i: (0, 2 * heads + h))],
        out_specs=pl.BlockSpec((bq, LANES), lambda h, i: (i, h)),
        out_shape=jax.ShapeDtypeStruct((s, heads * LANES), F32),
        scratch_shapes=[pltpu.VMEM((bq, LANES), F32), pltpu.VMEM((bq, LANES), F32)],
        compiler_params=pltpu.CompilerParams(dimension_semantics=("parallel", "arbitrary"),
                                             vmem_limit_bytes=VMEM_LIMIT),
    )(qkv, qkv, qkv)


def _sb_bwd(qkv, o, do, heads, *, name):
    s = qkv.shape[0]
    bq, bk, nq, ratio = _att_blocks(s)

    def body(q_ref, k_ref, v_ref, o_ref, do_ref, dq_ref, dk_ref, dv_ref, dq_acc, dk_acc, dv_acc, c_ref, e_ref):
        qi = pl.program_id(1)

        @pl.when(qi == 0)
        def _():
            dk_acc[...] = jnp.zeros_like(dk_acc)
            dv_acc[...] = jnp.zeros_like(dv_acc)

        q = q_ref[...] * SB_SCALE
        do = do_ref[...]
        total = jnp.sum(do.astype(F32) * o_ref[...].astype(F32), axis=-1, keepdims=True)
        m_strict, m_incl = _suffix_matrices(bk)
        dq_acc[...] = jnp.zeros_like(dq_acc)
        c_ref[...] = jnp.zeros_like(c_ref)
        e_ref[...] = jnp.broadcast_to(total, e_ref.shape)
        reps = (1, bk // LANES)

        def step(kb, masked):
            rows = pl.ds(pl.multiple_of(kb * bk, bk), bk)
            k, v = k_ref[rows, :], v_ref[rows, :]
            lb, lk_all = _sb_logs(q, k)
            lk = lk_all
            if masked:
                row, col = _block_positions(qi, kb, bq, bk)
                causal = col < row
                lk = jnp.where(causal, lk_all, 0.0)
            c = c_ref[...]
            w = jnp.exp2(lb + _dot_nn(lk.astype(BF16), m_strict) + jnp.tile(c, reps))
            if masked:
                w = jnp.where(causal, w, 0.0)
            wb = w.astype(BF16)
            g = wb.astype(F32) * _dot_nt(do, v)
            e = e_ref[...]
            g_left = jnp.tile(e, reps) - _suffix_sum(g, m_incl)
            da = g * jnp.exp2(lk_all) - jnp.exp2(lb) * g_left
            if masked:
                da = jnp.where(causal, da, 0.0)
            dab = da.astype(BF16)
            dq_acc[...] += _dot_nn(dab, k)
            dk_acc[rows, :] += _dot_tn(dab, q)
            dv_acc[rows, :] += _dot_tn(wb, do)
            e_ref[...] = e - jnp.sum(g, axis=-1, keepdims=True)
            c_ref[...] = c + jnp.sum(lk, axis=-1, keepdims=True)

        _sweep(qi, ratio, step)
        dq_ref[...] = (dq_acc[...] * SB_SCALE).astype(dq_ref.dtype)

        @pl.when(qi == nq - 1)
        def _():
            dk_ref[...] = dk_acc[...].astype(dk_ref.dtype)
            dv_ref[...] = dv_acc[...].astype(dv_ref.dtype)

    blk = pl.BlockSpec((bq, LANES), lambda h, i: (i, h))
    full = pl.BlockSpec((s, LANES), lambda h, i: (0, h))
    shape = jax.ShapeDtypeStruct((s, heads * LANES), BF16)
    return pl.pallas_call(
        body, name=name, grid=(heads, nq),
        in_specs=[blk,
                  pl.BlockSpec((s, LANES), lambda h, i: (0, heads + h)),
                  pl.BlockSpec((s, LANES), lambda h, i: (0, 2 * heads + h)),
                  blk, blk],
        out_specs=[blk, full, full],
        out_shape=[shape, shape, shape],
        scratch_shapes=[pltpu.VMEM((bq, LANES), F32), pltpu.VMEM((s, LANES), F32), pltpu.VMEM((s, LANES), F32),
                        pltpu.VMEM((bq, LANES), F32), pltpu.VMEM((bq, LANES), F32)],
        compiler_params=pltpu.CompilerParams(dimension_semantics=("arbitrary", "arbitrary"),
                                             vmem_limit_bytes=VMEM_LIMIT),
    )(qkv, qkv, qkv, o, do)


def _chunk_allowed(qi, kb, bq, bk):
    row, col = _block_positions(qi, kb, bq, bk)
    return (col // CHUNK) <= (row // CHUNK)


def _mla_fwd(q, kv, heads, *, name):
    s = q.shape[0]
    bq, bk, nq, ratio = _att_blocks(s)
    reps = (1, bk // LANES)

    def body(q_ref, k_ref, v_ref, o_ref, lse_ref, acc_ref, m_ref, l_ref):
        qi = pl.program_id(1)
        qv = q_ref[...]
        acc_ref[...] = jnp.zeros_like(acc_ref)
        m_ref[...] = jnp.full_like(m_ref, NEG_BIG)
        l_ref[...] = jnp.zeros_like(l_ref)

        def step(kb, masked):
            rows = pl.ds(pl.multiple_of(kb * bk, bk), bk)
            k, v = k_ref[rows, :], v_ref[rows, :]
            sc = _dot_nt(qv, k) * (MLA_SCALE * LOG2E)
            if masked:
                allowed = _chunk_allowed(qi, kb, bq, bk)
                sc = jnp.where(allowed, sc, NEG_BIG)
            m_old = m_ref[...]
            m_new = jnp.maximum(m_old, jnp.max(sc, axis=-1, keepdims=True))
            p = jnp.exp2(sc - jnp.tile(m_new, reps))
            if masked:
                p = jnp.where(allowed, p, 0.0)
            alpha = jnp.exp2(m_old - m_new)
            l_ref[...] = alpha * l_ref[...] + jnp.sum(p, axis=-1, keepdims=True)
            acc_ref[...] = alpha * acc_ref[...] + _dot_nn(p.astype(BF16), v)
            m_ref[...] = m_new

        _sweep(qi, ratio, step)
        o_ref[...] = (acc_ref[...] / l_ref[...]).astype(o_ref.dtype)
        lse_ref[...] = m_ref[...] + jnp.log2(l_ref[...])

    blk = pl.BlockSpec((bq, LANES), lambda h, i: (i, h))
    return pl.pallas_call(
        body, name=name, grid=(heads, nq),
        in_specs=[blk,
                  pl.BlockSpec((s, LANES), lambda h, i: (0, h)),
                  pl.BlockSpec((s, LANES), lambda h, i: (0, heads + h))],
        out_specs=[blk, blk],
        out_shape=[jax.ShapeDtypeStruct((s, heads * LANES), BF16), jax.ShapeDtypeStruct((s, heads * LANES), F32)],
        scratch_shapes=[pltpu.VMEM((bq, LANES), F32), pltpu.VMEM((bq, LANES), F32), pltpu.VMEM((bq, LANES), F32)],
        compiler_params=pltpu.CompilerParams(dimension_semantics=("parallel", "arbitrary"),
                                             vmem_limit_bytes=VMEM_LIMIT),
    )(q, kv, kv)


def _mla_bwd(q, kv, o, do, lse, cos_t, sin_t, dkv_init, heads, *, name):
    s = q.shape[0]
    bq, bk, nq, ratio = _att_blocks(s)
    reps = (1, bk // LANES)
    has_init = dkv_init is not None

    def body(*refs):
        q_ref, k_ref, v_ref, o_ref, do_ref, lse_ref, c_ref, s_ref = refs[:8]
        ki_ref, vi_ref = (refs[8], refs[9]) if has_init else (None, None)
        dq_ref, dk_ref, dv_ref, dq_acc, dk_acc, dv_acc = refs[-6:]
        qi = pl.program_id(1)

        @pl.when(qi == 0)
        def _():
            if has_init:
                dk_acc[...] = ki_ref[...].astype(F32)
                dv_acc[...] = vi_ref[...].astype(F32)
            else:
                dk_acc[...] = jnp.zeros_like(dk_acc)
                dv_acc[...] = jnp.zeros_like(dv_acc)

        qv = q_ref[...]
        do = do_ref[...]
        delta = jnp.sum(do.astype(F32) * o_ref[...].astype(F32), axis=-1, keepdims=True)
        lse_wide = jnp.tile(lse_ref[...], reps)
        dq_acc[...] = jnp.zeros_like(dq_acc)

        def step(kb, masked):
            rows = pl.ds(pl.multiple_of(kb * bk, bk), bk)
            k, v = k_ref[rows, :], v_ref[rows, :]
            sc = _dot_nt(qv, k) * (MLA_SCALE * LOG2E)
            p = jnp.exp2(sc - lse_wide)
            if masked:
                p = jnp.where(_chunk_allowed(qi, kb, bq, bk), p, 0.0)
            ds = (p * (_dot_nt(do, v) - delta) * MLA_SCALE).astype(BF16)
            dq_acc[...] += _dot_nn(ds, k)
            dk_acc[rows, :] += _dot_tn(ds, qv)
            dv_acc[rows, :] += _dot_tn(p.astype(BF16), do)

        _sweep(qi, ratio, step)
        dq_ref[...] = _rope_slab_bwd(dq_acc[...], c_ref[...], s_ref[...]).astype(dq_ref.dtype)

        @pl.when(qi == nq - 1)
        def _():
            dk_ref[...] = dk_acc[...].astype(dk_ref.dtype)
            dv_ref[...] = dv_acc[...].astype(dv_ref.dtype)

    blk = pl.BlockSpec((bq, LANES), lambda h, i: (i, h))
    tab = pl.BlockSpec((bq, LANES), lambda h, i: (i, 0))
    k_full = pl.BlockSpec((s, LANES), lambda h, i: (0, h))
    v_full = pl.BlockSpec((s, LANES), lambda h, i: (0, heads + h))
    shape = jax.ShapeDtypeStruct((s, heads * LANES), BF16)
    ins = [q, kv, kv, o, do, lse, cos_t, sin_t] + ([dkv_init, dkv_init] if has_init else [])
    dq, dk, dv = pl.pallas_call(
        body, name=name, grid=(heads, nq),
        in_specs=[blk, k_full, v_full, blk, blk, blk, tab, tab] + ([k_full, v_full] if has_init else []),
        out_specs=[blk, k_full, k_full],
        out_shape=[shape, shape, shape],
        scratch_shapes=[pltpu.VMEM((bq, LANES), F32), pltpu.VMEM((s, LANES), F32), pltpu.VMEM((s, LANES), F32)],
        compiler_params=pltpu.CompilerParams(dimension_semantics=("arbitrary", "arbitrary"),
                                             vmem_limit_bytes=VMEM_LIMIT),
    )(*ins)
    return dq, jnp.concatenate([dk, dv], axis=1)


def _pad_last(a, width):
    return jnp.pad(a, [(0, 0)] * (a.ndim - 1) + [(0, width - a.shape[-1])])


def _pad_qkv(w, heads):
    d = w.shape[0]
    return _pad_last(w.reshape(d, 3 * heads, SB_HEAD_DIM), LANES).reshape(d, 3 * heads * LANES)


def _unpad_qkv(g, heads):
    d = g.shape[0]
    return g.reshape(d, 3 * heads, LANES)[:, :, :SB_HEAD_DIM].reshape(d, 3 * heads * SB_HEAD_DIM)


def _pad_o(w, heads):
    d = w.shape[1]
    w = w.reshape(heads, SB_HEAD_DIM, d)
    return jnp.pad(w, [(0, 0), (0, LANES - SB_HEAD_DIM), (0, 0)]).reshape(heads * LANES, d)


def _unpad_o(g, heads):
    d = g.shape[1]
    return g.reshape(heads, LANES, d)[:, :SB_HEAD_DIM, :].reshape(heads * SB_HEAD_DIM, d)


def _pad_uq(w, heads):
    r = w.shape[0]
    return _pad_last(w.reshape(r, heads, MLA_NOPE + MLA_ROPE), LANES).reshape(r, heads * LANES)


def _unpad_uq(g, heads):
    r = g.shape[0]
    return g.reshape(r, heads, LANES)[:, :, :MLA_NOPE + MLA_ROPE].reshape(r, heads * (MLA_NOPE + MLA_ROPE))


def _pad_dkv(w):
    d = w.shape[0]
    rope = jnp.zeros((d, LANES), w.dtype).at[:, ROPE_LO:ROPE_LO + MLA_ROPE].set(w[:, MLA_KV_RANK:])
    return jnp.concatenate([w[:, :MLA_KV_RANK], rope], axis=1)


def _unpad_dkv(g):
    return jnp.concatenate([g[:, :MLA_KV_RANK], g[:, MLA_KV_RANK + ROPE_LO:MLA_KV_RANK + ROPE_LO + MLA_ROPE]], axis=1)


def _pad_ukv(w, heads):
    w = w.reshape(MLA_KV_RANK, heads, 2, MLA_NOPE)
    k_part = _pad_last(w[:, :, 0, :], LANES).reshape(MLA_KV_RANK, heads * LANES)
    v_part = _pad_last(w[:, :, 1, :], LANES).reshape(MLA_KV_RANK, heads * LANES)
    lane = jnp.arange(LANES)
    place = ((lane[:, None] == lane[None, :]) & (lane[:, None] >= ROPE_LO) & (lane[:, None] < ROPE_LO + MLA_ROPE))
    place = jnp.tile(place.astype(w.dtype), (1, heads))
    top = jnp.concatenate([k_part, v_part], axis=1)
    bottom = jnp.concatenate([place, jnp.zeros_like(place)], axis=1)
    return jnp.concatenate([top, bottom], axis=0)


def _unpad_ukv(g, heads):
    g = g[:MLA_KV_RANK]
    k_part = g[:, :heads * LANES].reshape(MLA_KV_RANK, heads, LANES)[:, :, :MLA_NOPE]
    v_part = g[:, heads * LANES:].reshape(MLA_KV_RANK, heads, LANES)[:, :, :MLA_V]
    return jnp.stack([k_part, v_part], axis=2).reshape(MLA_KV_RANK, heads * (MLA_NOPE + MLA_V))


def _rope_tables(positions):
    inv_freq = ROPE_THETA ** (-jnp.arange(0, MLA_ROPE, 2, dtype=F32) / MLA_ROPE)
    ang = positions.astype(F32)[:, None] * inv_freq
    cos, sin = jnp.cos(ang), jnp.sin(ang)
    s = positions.shape[0]
    cos_t = jnp.ones((s, LANES), F32).at[:, ROPE_LO:ROPE_LO + MLA_ROPE].set(jnp.concatenate([cos, cos], axis=1))
    sin_t = jnp.zeros((s, LANES), F32).at[:, ROPE_LO:ROPE_LO + MLA_ROPE].set(jnp.concatenate([-sin, sin], axis=1))
    return cos_t, sin_t


def _local_step(x, positions, target, w, norms):
    s, d = x.shape
    heads = d // SB_HEAD_DIM
    n_a = w["sb_w_qkv"].shape[0]
    n_b = w["mla_w_dq"].shape[0]
    depth = n_a + n_b
    cos_t, sin_t = _rope_tables(positions)

    wqkv = [_pad_qkv(w["sb_w_qkv"][l], heads) for l in range(n_a)]
    wo_a = [_pad_o(w["sb_w_o"][l], heads) for l in range(n_a)]
    wdkv = _pad_dkv(w["mla_w_dkv"])
    wkv = _pad_ukv(w["mla_w_ukv"], heads)
    wdq = [w["mla_w_dq"][j] for j in range(n_b)]
    wuq = [_pad_uq(w["mla_w_uq"][j], heads) for j in range(n_b)]
    wo_b = [_pad_o(w["mla_w_o"][j], heads) for j in range(n_b)]
    w1 = [w["mlp_w1"][l] for l in range(depth)]
    w2 = [w["mlp_w2"][l] for l in range(depth)]

    saved = []
    kv_saved = None
    kv = None
    for l in range(depth):
        t = f"l{l}"
        sv = {"x_in": x}
        h = _rms_fwd(x, norms["attn_norm"][l], name=f"{t}_attn_norm")
        sv["h"] = h
        if l < n_a:
            qkv = _mm(h, wqkv[l], name=f"{t}_qkv")
            o = _sb_fwd(qkv, heads, name=f"{t}_sb_fwd")
            sv["qkv"], sv["o"] = qkv, o
            x = _mm(o, wo_a[l], name=f"{t}_attn_out", epilogue=_epi_add, extras=[(x, "tile")], out_dtypes=(F32,))
        else:
            j = l - n_a
            if j == 0:
                hk = _rms_fwd(x, norms["kv_norm"], name="kv_norm")
                down = _mm(hk, wdkv, name="kv_down", out_dtypes=(F32,))
                cat = _kv_prep(down, norms["mla_kv_lat_norm"], cos_t, sin_t, name="kv_prep")
                kv = _mm(cat, wkv, name="kv_up")
                kv_saved = {"x_in": x, "hk": hk, "down": down, "cat": cat}
            cq0 = _mm(h, wdq[j], name=f"{t}_q_down", out_dtypes=(F32,))
            cq = _rms_fwd(cq0, norms["mla_q_lat_norm"][j], name=f"{t}_q_lat_norm")
            q = _mm(cq, wuq[j], name=f"{t}_q_up", epilogue=_epi_rope_heads, extras=[(cos_t, "row"), (sin_t, "row")])
            o, lse = _mla_fwd(q, kv, heads, name=f"{t}_mla_fwd")
            sv.update(cq0=cq0, cq=cq, q=q, o=o, lse=lse)
            x = _mm(o, wo_b[j], name=f"{t}_attn_out", epilogue=_epi_add, extras=[(x, "tile")], out_dtypes=(F32,))
        sv["x_mid"] = x
        h2 = _rms_fwd(x, norms["mlp_norm"][l], name=f"{t}_mlp_norm")
        u, a = _mm(h2, w1[l], name=f"{t}_mlp_up", epilogue=_epi_relu2, out_dtypes=(BF16, BF16))
        sv.update(h2=h2, u=u, a=a)
        x = _mm(a, w2[l], name=f"{t}_mlp_down", epilogue=_epi_add, extras=[(x, "tile")], out_dtypes=(F32,))
        saved.append(sv)

    loss_slab, dx, dxb, dg_final = _loss_bwd(x, norms["final_norm"], target, name="loss")
    loss = loss_slab[0, 0]

    g_attn_norm, g_mlp_norm = [None] * depth, [None] * depth
    g_qkv, g_o_a = [None] * n_a, [None] * n_a
    g_dq, g_uq, g_o_b, g_qlat = [None] * n_b, [None] * n_b, [None] * n_b, [None] * n_b
    g_w1, g_w2 = [None] * depth, [None] * depth
    dkv = None
    g_kv_norm = g_kv_lat = g_dkv = g_ukv = None

    for l in reversed(range(depth)):
        t = f"l{l}"
        sv = saved[l]
        du = _mm(dxb, w2[l], name=f"{t}_mlp_down_dx", dims="nt", epilogue=_epi_relu2_grad, extras=[(sv["u"], "tile")])
        g_w2[l] = _mm(sv["a"], dxb, name=f"{t}_mlp_down_dw", dims="tn", out_dtypes=(F32,))
        g_w1[l] = _mm(sv["h2"], du, name=f"{t}_mlp_up_dw", dims="tn", out_dtypes=(F32,))
        dh2 = _mm(du, w1[l], name=f"{t}_mlp_up_dx", dims="nt", out_dtypes=(F32,))
        dx, dxb, g_mlp_norm[l] = _rms_bwd(sv["x_mid"], norms["mlp_norm"][l], dh2, dx, name=f"{t}_mlp_norm_bwd")
        if l < n_a:
            do = _mm(dxb, wo_a[l], name=f"{t}_attn_out_dx", dims="nt")
            g_o_a[l] = _unpad_o(_mm(sv["o"], dxb, name=f"{t}_attn_out_dw", dims="tn", out_dtypes=(F32,)), heads)
            dq, dk, dv = _sb_bwd(sv["qkv"], sv["o"], do, heads, name=f"{t}_sb_bwd")
            dqkv = jnp.concatenate([dq, dk, dv], axis=1)
            g_qkv[l] = _unpad_qkv(_mm(sv["h"], dqkv, name=f"{t}_qkv_dw", dims="tn", out_dtypes=(F32,)), heads)
            dh = _mm(dqkv, wqkv[l], name=f"{t}_qkv_dx", dims="nt", out_dtypes=(F32,))
        else:
            j = l - n_a
            do = _mm(dxb, wo_b[j], name=f"{t}_attn_out_dx", dims="nt")
            g_o_b[j] = _unpad_o(_mm(sv["o"], dxb, name=f"{t}_attn_out_dw", dims="tn", out_dtypes=(F32,)), heads)
            dq, dkv = _mla_bwd(sv["q"], kv, sv["o"], do, sv["lse"], cos_t, sin_t, dkv, heads, name=f"{t}_mla_bwd")
            g_uq[j] = _unpad_uq(_mm(sv["cq"], dq, name=f"{t}_q_up_dw", dims="tn", out_dtypes=(F32,)), heads)
            dcq = _mm(dq, wuq[j], name=f"{t}_q_up_dx", dims="nt", out_dtypes=(F32,))
            _, dcq0, g_qlat[j] = _rms_bwd(sv["cq0"], norms["mla_q_lat_norm"][j], dcq, None, name=f"{t}_q_lat_norm_bwd")
            g_dq[j] = _mm(sv["h"], dcq0, name=f"{t}_q_down_dw", dims="tn", out_dtypes=(F32,))
            dh = _mm(dcq0, wdq[j], name=f"{t}_q_down_dx", dims="nt", out_dtypes=(F32,))
        dx, dxb, g_attn_norm[l] = _rms_bwd(sv["x_in"], norms["attn_norm"][l], dh, dx, name=f"{t}_attn_norm_bwd")
        if l == n_a:
            ks = kv_saved
            dcat = _mm(dkv, wkv, name="kv_up_dx", dims="nt", out_dtypes=(F32,))
            g_ukv = _unpad_ukv(_mm(ks["cat"], dkv, name="kv_up_dw", dims="tn", out_dtypes=(F32,)), heads)
            ddown, g_kv_lat = _kv_prep_bwd(ks["down"], norms["mla_kv_lat_norm"], cos_t, sin_t, dcat, name="kv_prep_bwd")
            g_dkv = _unpad_dkv(_mm(ks["hk"], ddown, name="kv_down_dw", dims="tn", out_dtypes=(F32,)))
            dhk = _mm(ddown, wdkv, name="kv_down_dx", dims="nt", out_dtypes=(F32,))
            dx, dxb, g_kv_norm = _rms_bwd(ks["x_in"], norms["kv_norm"], dhk, dx, name="kv_norm_bwd")

    grads = {
        "attn_norm": jnp.concatenate(g_attn_norm, axis=0), "mlp_norm": jnp.concatenate(g_mlp_norm, axis=0),
        "sb_w_qkv": jnp.stack(g_qkv), "sb_w_o": jnp.stack(g_o_a),
        "kv_norm": g_kv_norm[0], "mla_w_dkv": g_dkv, "mla_kv_lat_norm": g_kv_lat[0], "mla_w_ukv": g_ukv,
        "mla_w_dq": jnp.stack(g_dq), "mla_q_lat_norm": jnp.concatenate(g_qlat, axis=0),
        "mla_w_uq": jnp.stack(g_uq), "mla_w_o": jnp.stack(g_o_b),
        "mlp_w1": jnp.stack(g_w1), "mlp_w2": jnp.stack(g_w2), "final_norm": dg_final[0],
    }
    return loss, dx, grads


def _flat_rows(n_elems):
    per_block = FLAT_COLS * FLAT_ROW_BLOCK * 2
    return -(-n_elems // per_block) * FLAT_ROW_BLOCK * 2


def _pack(arrays, dtype):
    flat = jnp.concatenate([a.reshape(-1).astype(dtype) for a in arrays])
    rows = _flat_rows(flat.shape[0])
    flat = jnp.pad(flat, (0, rows * FLAT_COLS - flat.shape[0]))
    return flat.reshape(rows, FLAT_COLS)


def _unpack(flat, shapes):
    flat = flat.reshape(-1)
    out, off = [], 0
    for shp in shapes:
        n = 1
        for v in shp:
            n *= v
        out.append(flat[off:off + n].reshape(shp))
        off += n
    return out


def _pack_small(arrays):
    rows = []
    for a in arrays:
        a = a.reshape(-1, a.shape[-1]) if a.shape[-1] == FLAT_COLS else a.reshape(1, -1)
        rows.append(_pad_last(a, FLAT_COLS))
    flat = jnp.concatenate(rows, axis=0)
    return jnp.pad(flat, [(0, -flat.shape[0] % 8), (0, 0)])


def _unpack_small(flat, shapes):
    out, row = [], 0
    for shp in shapes:
        if shp[-1] == FLAT_COLS:
            n = 1
            for v in shp[:-1]:
                n *= v
            out.append(flat[row:row + n].reshape(shp))
            row += n
        else:
            n = 1
            for v in shp:
                n *= v
            out.append(flat[row, :n].reshape(shp))
            row += 1
    return out


def _other_chips(x, y):
    return [(1 - x, y), (x, 1 - y), (1 - x, 1 - y)]


def _all_gather_chips(flat, *, name):
    rows, cols = flat.shape

    def body(x_ref, out_ref, send_sems, recv_sems, pass_send_sems, pass_recv_sems):
        x, y, c = lax.axis_index("x"), lax.axis_index("y"), lax.axis_index("c")
        me = 2 * x + y
        my_rows, sib_rows = _half_rows(rows)
        chips = _other_chips(x, y)
        sends = []
        for k, (px, py) in enumerate(chips):
            cp = pltpu.make_async_remote_copy(src_ref=x_ref.at[my_rows, :], dst_ref=out_ref.at[me, my_rows, :],
                                              send_sem=send_sems.at[k], recv_sem=recv_sems.at[k],
                                              device_id=(px, py, c), device_id_type=MESH)
            cp.start()
            sends.append(cp)
        for k, (px, py) in enumerate(chips):
            landed = out_ref.at[2 * px + py, my_rows, :]
            pltpu.make_async_remote_copy(src_ref=landed, dst_ref=landed, send_sem=send_sems.at[k],
                                         recv_sem=recv_sems.at[k], device_id=(px, py, c),
                                         device_id_type=MESH).wait_recv()
            cp = pltpu.make_async_remote_copy(src_ref=landed, dst_ref=landed, send_sem=pass_send_sems.at[k],
                                              recv_sem=pass_recv_sems.at[k], device_id=_sibling(),
                                              device_id_type=MESH)
            cp.start()
            sends.append(cp)
        for k, (px, py) in enumerate(chips):
            passed = out_ref.at[2 * px + py, sib_rows, :]
            pltpu.make_async_remote_copy(src_ref=passed, dst_ref=passed, send_sem=pass_send_sems.at[k],
                                         recv_sem=pass_recv_sems.at[k], device_id=_sibling(),
                                         device_id_type=MESH).wait_recv()
        for cp in sends:
            cp.wait_send()

    out = pl.pallas_call(
        body, name=name,
        in_specs=[pl.BlockSpec(memory_space=pltpu.HBM)],
        out_specs=pl.BlockSpec(memory_space=pltpu.HBM),
        out_shape=jax.ShapeDtypeStruct((N_CHIPS, rows, cols), flat.dtype),
        scratch_shapes=[pltpu.SemaphoreType.DMA((3,)), pltpu.SemaphoreType.DMA((3,)), pltpu.SemaphoreType.DMA((3,)),
                        pltpu.SemaphoreType.DMA((3,))],
        compiler_params=pltpu.CompilerParams(has_side_effects=True),
    )(flat)
    return lax.dynamic_update_index_in_dim(out, flat, _my_chip(), 0)


def _exchange_chips(parts, *, name):
    def body(g_ref, out_ref, send_sems, recv_sems):
        x, y, c = lax.axis_index("x"), lax.axis_index("y"), lax.axis_index("c")
        me = 2 * x + y
        sends = []
        for k, (px, py) in enumerate(_other_chips(x, y)):
            cp = pltpu.make_async_remote_copy(src_ref=g_ref.at[2 * px + py], dst_ref=out_ref.at[me],
                                              send_sem=send_sems.at[k], recv_sem=recv_sems.at[k],
                                              device_id=(px, py, c), device_id_type=MESH)
            cp.start()
            sends.append(cp)
        for k, (px, py) in enumerate(_other_chips(x, y)):
            pltpu.make_async_remote_copy(src_ref=g_ref.at[me], dst_ref=out_ref.at[2 * px + py],
                                         send_sem=send_sems.at[k], recv_sem=recv_sems.at[k],
                                         device_id=(px, py, c), device_id_type=MESH).wait_recv()
        for cp in sends:
            cp.wait_send()

    out = pl.pallas_call(
        body, name=name,
        in_specs=[pl.BlockSpec(memory_space=pltpu.HBM)],
        out_specs=pl.BlockSpec(memory_space=pltpu.HBM),
        out_shape=jax.ShapeDtypeStruct(parts.shape, parts.dtype),
        scratch_shapes=[pltpu.SemaphoreType.DMA((3,)), pltpu.SemaphoreType.DMA((3,))],
        compiler_params=pltpu.CompilerParams(has_side_effects=True),
    )(parts)
    me = _my_chip()
    return lax.dynamic_update_index_in_dim(out, lax.dynamic_index_in_dim(parts, me, 0, keepdims=False), me, 0)


def _my_chip():
    return 2 * lax.axis_index("x") + lax.axis_index("y")


def _half_rows(rows):
    c = lax.axis_index("c")
    half = rows // 2
    return pl.ds(pl.multiple_of(c * half, 8), half), pl.ds(pl.multiple_of((1 - c) * half, 8), half)


def _sibling():
    return (lax.axis_index("x"), lax.axis_index("y"), 1 - lax.axis_index("c"))


def _pair_exchange(parts, *, name):
    n, rows, cols = parts.shape

    def body(p_ref, theirs_ref, send_sem, recv_sem):
        _, sib_rows = _half_rows(rows)
        cp = pltpu.make_async_remote_copy(src_ref=p_ref.at[:, sib_rows, :], dst_ref=theirs_ref, send_sem=send_sem,
                                          recv_sem=recv_sem, device_id=_sibling(), device_id_type=MESH)
        cp.start()
        cp.wait()

    half = rows // 2
    theirs = pl.pallas_call(
        body, name=name,
        in_specs=[pl.BlockSpec(memory_space=pltpu.HBM)],
        out_specs=pl.BlockSpec(memory_space=pltpu.HBM),
        out_shape=jax.ShapeDtypeStruct((n, half, cols), parts.dtype),
        scratch_shapes=[pltpu.SemaphoreType.DMA, pltpu.SemaphoreType.DMA],
        compiler_params=pltpu.CompilerParams(has_side_effects=True),
    )(parts)
    mine = lax.dynamic_slice_in_dim(parts, lax.axis_index("c") * half, half, axis=1)
    return mine, theirs


def _pair_sum(mine, theirs, *, name):
    n, rows, cols = mine.shape

    def body(a_ref, b_ref, o_ref):
        o_ref[...] = (a_ref[...].astype(F32) + b_ref[...].astype(F32)).astype(o_ref.dtype)

    blk = pl.BlockSpec((n, FLAT_ROW_BLOCK, cols), lambda i: (0, i, 0))
    return pl.pallas_call(
        body, name=name, grid=(rows // FLAT_ROW_BLOCK,),
        in_specs=[blk, blk], out_specs=blk, out_shape=jax.ShapeDtypeStruct(mine.shape, mine.dtype),
        compiler_params=pltpu.CompilerParams(dimension_semantics=("parallel",), vmem_limit_bytes=VMEM_LIMIT),
    )(mine, theirs)


def _sum_chips(parts, *, name):
    _, rows, cols = parts.shape

    def body(p_ref, o_ref):
        o_ref[...] = ((p_ref[0].astype(F32) + p_ref[1].astype(F32)) + p_ref[2].astype(F32)) + p_ref[3].astype(F32)

    return pl.pallas_call(
        body, name=name, grid=(rows // FLAT_ROW_BLOCK,),
        in_specs=[pl.BlockSpec((N_CHIPS, FLAT_ROW_BLOCK, cols), lambda i: (0, i, 0))],
        out_specs=pl.BlockSpec((FLAT_ROW_BLOCK, cols), lambda i: (i, 0)),
        out_shape=jax.ShapeDtypeStruct((rows, cols), F32),
        compiler_params=pltpu.CompilerParams(dimension_semantics=("parallel",), vmem_limit_bytes=VMEM_LIMIT),
    )(parts)


def _join_cores(half, *, name):
    rows2, cols = half.shape

    def body(h_ref, out_ref, send_sem, recv_sem):
        my_rows, sib_rows = _half_rows(2 * rows2)
        cp = pltpu.make_async_remote_copy(src_ref=h_ref, dst_ref=out_ref.at[my_rows, :], send_sem=send_sem,
                                          recv_sem=recv_sem, device_id=_sibling(), device_id_type=MESH)
        cp.start()
        cp.wait_send()
        pltpu.make_async_remote_copy(src_ref=h_ref, dst_ref=out_ref.at[sib_rows, :], send_sem=send_sem,
                                     recv_sem=recv_sem, device_id=_sibling(), device_id_type=MESH).wait_recv()

    out = pl.pallas_call(
        body, name=name,
        in_specs=[pl.BlockSpec(memory_space=pltpu.HBM)],
        out_specs=pl.BlockSpec(memory_space=pltpu.HBM),
        out_shape=jax.ShapeDtypeStruct((2 * rows2, cols), half.dtype),
        scratch_shapes=[pltpu.SemaphoreType.DMA, pltpu.SemaphoreType.DMA],
        compiler_params=pltpu.CompilerParams(has_side_effects=True),
    )(half)
    return lax.dynamic_update_slice_in_dim(out, half, lax.axis_index("c") * rows2, axis=0)


def _all_reduce_small(v, *, name):
    rows, cols = v.shape
    flips = [(fx, fy, fc) for fx in (0, 1) for fy in (0, 1) for fc in (0, 1)][1:]

    def body(v_ref, out_ref, gath_ref, send_sems, recv_sems):
        x, y, c = lax.axis_index("x"), lax.axis_index("y"), lax.axis_index("c")
        me = 4 * x + 2 * y + c
        gath_ref[me] = v_ref[...]
        peers = [((1 - x) if fx else x, (1 - y) if fy else y, (1 - c) if fc else c) for fx, fy, fc in flips]
        sends = []
        for k, peer in enumerate(peers):
            cp = pltpu.make_async_remote_copy(src_ref=v_ref, dst_ref=gath_ref.at[me], send_sem=send_sems.at[k],
                                              recv_sem=recv_sems.at[k], device_id=peer, device_id_type=MESH)
            cp.start()
            sends.append(cp)
        for k, (px, py, pc) in enumerate(peers):
            pltpu.make_async_remote_copy(src_ref=v_ref, dst_ref=gath_ref.at[4 * px + 2 * py + pc],
                                         send_sem=send_sems.at[k], recv_sem=recv_sems.at[k],
                                         device_id=(px, py, pc), device_id_type=MESH).wait_recv()
        for cp in sends:
            cp.wait_send()
        total = gath_ref[0]
        for k in range(1, 8):
            total = total + gath_ref[k]
        out_ref[...] = total

    total, _ = pl.pallas_call(
        body, name=name,
        in_specs=[pl.BlockSpec(memory_space=pltpu.VMEM)],
        out_specs=[pl.BlockSpec(memory_space=pltpu.VMEM), pl.BlockSpec(memory_space=pltpu.VMEM)],
        out_shape=[jax.ShapeDtypeStruct((rows, cols), v.dtype), jax.ShapeDtypeStruct((8, rows, cols), v.dtype)],
        scratch_shapes=[pltpu.SemaphoreType.DMA((7,)), pltpu.SemaphoreType.DMA((7,))],
        compiler_params=pltpu.CompilerParams(has_side_effects=True),
    )(v)
    return total


def _adamw(w, g_parts, m, v, *, name):
    rows, cols = w.shape
    br = min(FLAT_ROW_BLOCK, rows)
    n_parts = len(g_parts)

    def body(*refs):
        w_ref = refs[0]
        g_refs = refs[1:1 + n_parts]
        m_ref, v_ref = refs[1 + n_parts], refs[2 + n_parts]
        g_out, d_out, m_out, v_out = refs[-4:]
        g = g_refs[0][...]
        for r in g_refs[1:]:
            g = g + r[...]
        m_new = ADAM_B1 * m_ref[...] + (1.0 - ADAM_B1) * g
        v_new = ADAM_B2 * v_ref[...] + (1.0 - ADAM_B2) * jnp.square(g)
        m_hat = m_new / (1.0 - ADAM_B1 ** ADAM_STEP)
        v_hat = v_new / (1.0 - ADAM_B2 ** ADAM_STEP)
        g_out[...] = g
        d_out[...] = -ADAM_LR * (m_hat / (jnp.sqrt(v_hat) + ADAM_EPS) + ADAM_WD * w_ref[...])
        m_out[...] = m_new
        v_out[...] = v_new

    blk = pl.BlockSpec((br, cols), lambda i: (i, 0))
    shape = jax.ShapeDtypeStruct((rows, cols), F32)
    return pl.pallas_call(
        body, name=name, grid=(rows // br,),
        in_specs=[blk] * (3 + n_parts), out_specs=[blk] * 4, out_shape=[shape] * 4,
        compiler_params=pltpu.CompilerParams(dimension_semantics=("parallel",), vmem_limit_bytes=VMEM_LIMIT),
    )(w, *g_parts, m, v)


def _assemble(gathered_shards, name):
    return jnp.concatenate(gathered_shards, axis=SHARD_AXIS[name])


def _chip_shard(full, name, j):
    axis = SHARD_AXIS[name]
    n = full.shape[axis] // N_CHIPS
    return lax.slice_in_dim(full, j * n, (j + 1) * n, axis=axis)


def kernel(x, positions, attn_norm, mlp_norm, sb_w_qkv, sb_w_o, kv_norm, mla_w_dkv, mla_kv_lat_norm, mla_w_ukv, mla_w_dq, mla_q_lat_norm, mla_w_uq, mla_w_o, mlp_w1, mlp_w2, final_norm, loss_target, m_attn_norm, m_mlp_norm, m_sb_w_qkv, m_sb_w_o, m_kv_norm, m_mla_w_dkv, m_mla_kv_lat_norm, m_mla_w_ukv, m_mla_w_dq, m_mla_q_lat_norm, m_mla_w_uq, m_mla_w_o, m_mlp_w1, m_mlp_w2, m_final_norm, v_attn_norm, v_mlp_norm, v_sb_w_qkv, v_sb_w_o, v_kv_norm, v_mla_w_dkv, v_mla_kv_lat_norm, v_mla_w_ukv, v_mla_w_dq, v_mla_q_lat_norm, v_mla_w_uq, v_mla_w_o, v_mlp_w1, v_mlp_w2, v_final_norm):
    weights = dict(attn_norm=attn_norm, mlp_norm=mlp_norm, sb_w_qkv=sb_w_qkv, sb_w_o=sb_w_o, kv_norm=kv_norm,
                   mla_w_dkv=mla_w_dkv, mla_kv_lat_norm=mla_kv_lat_norm, mla_w_ukv=mla_w_ukv, mla_w_dq=mla_w_dq,
                   mla_q_lat_norm=mla_q_lat_norm, mla_w_uq=mla_w_uq, mla_w_o=mla_w_o, mlp_w1=mlp_w1, mlp_w2=mlp_w2,
                   final_norm=final_norm)
    m_in = dict(attn_norm=m_attn_norm, mlp_norm=m_mlp_norm, sb_w_qkv=m_sb_w_qkv, sb_w_o=m_sb_w_o, kv_norm=m_kv_norm,
                mla_w_dkv=m_mla_w_dkv, mla_kv_lat_norm=m_mla_kv_lat_norm, mla_w_ukv=m_mla_w_ukv, mla_w_dq=m_mla_w_dq,
                mla_q_lat_norm=m_mla_q_lat_norm, mla_w_uq=m_mla_w_uq, mla_w_o=m_mla_w_o, mlp_w1=m_mlp_w1,
                mlp_w2=m_mlp_w2, final_norm=m_final_norm)
    v_in = dict(attn_norm=v_attn_norm, mlp_norm=v_mlp_norm, sb_w_qkv=v_sb_w_qkv, sb_w_o=v_sb_w_o, kv_norm=v_kv_norm,
                mla_w_dkv=v_mla_w_dkv, mla_kv_lat_norm=v_mla_kv_lat_norm, mla_w_ukv=v_mla_w_ukv, mla_w_dq=v_mla_w_dq,
                mla_q_lat_norm=v_mla_q_lat_norm, mla_w_uq=v_mla_w_uq, mla_w_o=v_mla_w_o, mlp_w1=v_mlp_w1,
                mlp_w2=v_mlp_w2, final_norm=v_final_norm)
    shard_shapes = [weights[n].shape for n in BIG_WEIGHTS]
    small_shapes = [weights[n].shape for n in SMALL_WEIGHTS]

    gathered = _all_gather_chips(_pack([weights[n] for n in BIG_WEIGHTS], BF16), name="weights_all_gather")
    per_chip = [_unpack(gathered[j], shard_shapes) for j in range(N_CHIPS)]
    full_w = {n: _assemble([per_chip[j][i] for j in range(N_CHIPS)], n) for i, n in enumerate(BIG_WEIGHTS)}
    norms = {n: weights[n] for n in SMALL_WEIGHTS}

    loss, dx, grads = _local_step(x[0], positions[0], loss_target[0], full_w, norms)
    loss = lax.psum(loss, ("x", "y", "c"))

    parts = jnp.stack([_pack([_chip_shard(grads[n], n, j) for n in BIG_WEIGHTS], BF16) for j in range(N_CHIPS)])
    mine, theirs = _pair_exchange(parts, name="grads_pair_exchange")
    chip_part = _pair_sum(mine, theirs, name="grads_pair_sum")
    received = _exchange_chips(chip_part, name="grads_exchange")
    g_half = _sum_chips(received, name="grads_sum_chips")
    g_sum = _join_cores(g_half, name="grads_join_cores")
    g_flat, d_flat, m_flat, v_flat = _adamw(
        _pack([weights[n] for n in BIG_WEIGHTS], F32), [g_sum],
        _pack([m_in[n] for n in BIG_WEIGHTS], F32), _pack([v_in[n] for n in BIG_WEIGHTS], F32), name="adamw_big")
    out_g = dict(zip(BIG_WEIGHTS, _unpack(g_flat, shard_shapes)))
    out_d = dict(zip(BIG_WEIGHTS, _unpack(d_flat, shard_shapes)))
    out_m = dict(zip(BIG_WEIGHTS, _unpack(m_flat, shard_shapes)))
    out_v = dict(zip(BIG_WEIGHTS, _unpack(v_flat, shard_shapes)))

    small_sum = _all_reduce_small(_pack_small([grads[n] for n in SMALL_WEIGHTS]), name="gains_all_reduce")
    sg, sd, sm, sv = _adamw(_pack_small([weights[n] for n in SMALL_WEIGHTS]), [small_sum],
                            _pack_small([m_in[n] for n in SMALL_WEIGHTS]),
                            _pack_small([v_in[n] for n in SMALL_WEIGHTS]), name="adamw_gains")
    out_g.update(zip(SMALL_WEIGHTS, _unpack_small(sg, small_shapes)))
    out_d.update(zip(SMALL_WEIGHTS, _unpack_small(sd, small_shapes)))
    out_m.update(zip(SMALL_WEIGHTS, _unpack_small(sm, small_shapes)))
    out_v.update(zip(SMALL_WEIGHTS, _unpack_small(sv, small_shapes)))

    return (loss, dx[None], *[out_g[n] for n in ALL_WEIGHTS], *[out_d[n] for n in ALL_WEIGHTS],
            *[out_m[n] for n in ALL_WEIGHTS], *[out_v[n] for n in ALL_WEIGHTS])
```

```python
import functools

import jax
import jax.numpy as jnp
from jax import lax
from jax.experimental import pallas as pl
from jax.experimental.pallas import tpu as pltpu

F32 = jnp.float32
BF16 = jnp.bfloat16

LANES = 128
SB_HEAD_DIM = 64
MLA_NOPE = 64
MLA_ROPE = 32
MLA_V = 64
MLA_Q_RANK = 384
MLA_KV_RANK = 256
CHUNK = 64
ROPE_THETA = 10000.0
NORM_EPS = 1e-6
SB_SCALE = SB_HEAD_DIM ** -0.5
MLA_SCALE = (MLA_NOPE + MLA_ROPE) ** -0.5
ROPE_LO = MLA_NOPE
ROPE_HALF = MLA_ROPE // 2
ATT_Q_BLOCK = 1024
ATT_K_BLOCK = 256
SWEEP_UNROLL = 2
NEG_BIG = -1e30
VMEM_LIMIT = 56 * 1024 * 1024

ADAM_LR = 0.001
ADAM_B1 = 0.9
ADAM_B2 = 0.999
ADAM_EPS = 1e-08
ADAM_WD = 0.01
ADAM_STEP = 10

FLAT_COLS = 1024
FLAT_ROW_BLOCK = 256
N_CHIPS = 4
MESH = pl.DeviceIdType.MESH

BIG_WEIGHTS = ["sb_w_qkv", "sb_w_o", "mla_w_dkv", "mla_w_ukv", "mla_w_dq", "mla_w_uq", "mla_w_o", "mlp_w1", "mlp_w2"]
SHARD_AXIS = {"sb_w_qkv": 2, "sb_w_o": 1, "mla_w_dkv": 0, "mla_w_ukv": 1, "mla_w_dq": 1, "mla_w_uq": 2,
              "mla_w_o": 1, "mlp_w1": 2, "mlp_w2": 1}
SMALL_WEIGHTS = ["attn_norm", "mlp_norm", "kv_norm", "mla_kv_lat_norm", "mla_q_lat_norm", "final_norm"]
ALL_WEIGHTS = ["attn_norm", "mlp_norm", "sb_w_qkv", "sb_w_o", "kv_norm", "mla_w_dkv", "mla_kv_lat_norm", "mla_w_ukv",
               "mla_w_dq", "mla_q_lat_norm", "mla_w_uq", "mla_w_o", "mlp_w1", "mlp_w2", "final_norm"]


def _dot(a, b, dims):
    return lax.dot_general(a, b, (dims, ((), ())), preferred_element_type=F32)


def _dot_nn(a, b):
    return _dot(a, b, ((1,), (0,)))


def _dot_nt(a, b):
    return _dot(a, b, ((1,), (1,)))


def _dot_tn(a, b):
    return _dot(a, b, ((0,), (0,)))


def _pick_block(n, target):
    if n <= target:
        return n
    best = max(b for b in range(LANES, target + 1, LANES) if n % b == 0)
    return best


MM_ROWS = 512
MM_COLS = 1024
MM_DEPTH = 4096
MM_DEPTH_TN = 1024


def _mm(a, b, *, name, dims="nn", epilogue=None, extras=(), out_dtypes=(BF16,)):
    if dims == "nn":
        (m, k), (k2, n) = a.shape, b.shape
    elif dims == "nt":
        (m, k), (n, k2) = a.shape, b.shape
    else:
        (k, m), (k2, n) = a.shape, b.shape
    assert k == k2, (name, a.shape, b.shape)
    if dims == "tn":
        bm, bn, bk = _pick_block(m, MM_COLS), _pick_block(n, MM_COLS), _pick_block(k, MM_DEPTH_TN)
    else:
        bm, bn, bk = _pick_block(m, MM_ROWS), _pick_block(n, MM_COLS), _pick_block(k, MM_DEPTH)
    nk = k // bk
    if dims == "tn":
        a_spec = pl.BlockSpec((bk, bm), lambda j, i, kk: (kk, i))
    else:
        a_spec = pl.BlockSpec((bm, bk), lambda j, i, kk: (i, kk))
    if dims == "nt":
        b_spec = pl.BlockSpec((bn, bk), lambda j, i, kk: (j, kk))
    else:
        b_spec = pl.BlockSpec((bk, bn), lambda j, i, kk: (kk, j))
    extra_specs = []
    for arr, kind in extras:
        if kind == "tile":
            assert arr.shape == (m, n), (name, arr.shape)
            extra_specs.append(pl.BlockSpec((bm, bn), lambda j, i, kk: (i, j)))
        else:
            assert arr.shape == (m, LANES), (name, arr.shape)
            extra_specs.append(pl.BlockSpec((bm, LANES), lambda j, i, kk: (i, 0)))
    n_extra = len(extras)
    n_out = len(out_dtypes)
    dot = {"nn": _dot_nn, "nt": _dot_nt, "tn": _dot_tn}[dims]

    def body(*refs):
        a_ref, b_ref = refs[0], refs[1]
        extra_refs = refs[2:2 + n_extra]
        out_refs = refs[2 + n_extra:2 + n_extra + n_out]

        def finish(acc):
            outs = (acc,) if epilogue is None else epilogue(acc, *[r[...] for r in extra_refs])
            for o_ref, o in zip(out_refs, outs):
                o_ref[...] = o.astype(o_ref.dtype)

        part = dot(a_ref[...].astype(BF16), b_ref[...].astype(BF16))
        if nk == 1:
            finish(part)
            return
        acc_ref = refs[-1]
        kk = pl.program_id(2)

        @pl.when(kk == 0)
        def _():
            acc_ref[...] = part

        @pl.when(kk > 0)
        def _():
            acc_ref[...] += part

        @pl.when(kk == nk - 1)
        def _():
            finish(acc_ref[...])

    outs = pl.pallas_call(
        body, name=name, grid=(n // bn, m // bm, nk),
        in_specs=[a_spec, b_spec] + extra_specs,
        out_specs=[pl.BlockSpec((bm, bn), lambda j, i, kk: (i, j)) for _ in range(n_out)],
        out_shape=[jax.ShapeDtypeStruct((m, n), dt) for dt in out_dtypes],
        scratch_shapes=[pltpu.VMEM((bm, bn), F32)] if nk > 1 else [],
        compiler_params=pltpu.CompilerParams(dimension_semantics=("parallel", "parallel", "arbitrary"),
                                             vmem_limit_bytes=VMEM_LIMIT),
    )(a, b, *[arr for arr, _ in extras])
    return outs[0] if n_out == 1 else outs


def _epi_add(acc, res):
    return (res + acc,)


def _epi_relu2(acc):
    r = jnp.maximum(acc, 0.0)
    return acc, r * r


def _epi_relu2_grad(acc, u):
    return (acc * (2.0 * jnp.maximum(u.astype(F32), 0.0)),)


def _rope_slab(t, cos_t, sin_t):
    lane = lax.broadcasted_iota(jnp.int32, t.shape, 1)
    partner = jnp.where(lane < ROPE_LO + ROPE_HALF, pltpu.roll(t, LANES - ROPE_HALF, 1), pltpu.roll(t, ROPE_HALF, 1))
    return t * cos_t + partner * sin_t


def _rope_slab_bwd(d, cos_t, sin_t):
    ds = d * sin_t
    lane = lax.broadcasted_iota(jnp.int32, d.shape, 1)
    partner = jnp.where(lane < ROPE_LO + ROPE_HALF, pltpu.roll(ds, LANES - ROPE_HALF, 1), pltpu.roll(ds, ROPE_HALF, 1))
    in_rope = (lane >= ROPE_LO) & (lane < ROPE_LO + MLA_ROPE)
    return d * cos_t + jnp.where(in_rope, partner, 0.0)


def _epi_rope_heads(acc, cos_t, sin_t):
    slabs = [_rope_slab(acc[:, j * LANES:(j + 1) * LANES], cos_t, sin_t) for j in range(acc.shape[1] // LANES)]
    return (jnp.concatenate(slabs, axis=1) * MLA_SCALE,)


def _row_block(s):
    return min(512, s)


def _rms_fwd(x, g, *, name):
    s, d = x.shape
    bm = _row_block(s)

    def body(x_ref, g_ref, o_ref):
        xv = x_ref[...]
        r = lax.rsqrt(jnp.mean(xv * xv, axis=-1, keepdims=True) + NORM_EPS)
        o_ref[...] = ((xv * r) * g_ref[...]).astype(o_ref.dtype)

    return pl.pallas_call(
        body, name=name, grid=(s // bm,),
        in_specs=[pl.BlockSpec((bm, d), lambda i: (i, 0)), pl.BlockSpec((1, d), lambda i: (0, 0))],
        out_specs=pl.BlockSpec((bm, d), lambda i: (i, 0)),
        out_shape=jax.ShapeDtypeStruct((s, d), BF16),
        compiler_params=pltpu.CompilerParams(dimension_semantics=("parallel",), vmem_limit_bytes=VMEM_LIMIT),
    )(x, g.reshape(1, d))


def _rms_bwd_math(xv, gv, dy):
    r = lax.rsqrt(jnp.mean(xv * xv, axis=-1, keepdims=True) + NORM_EPS)
    xhat = xv * r
    dyg = dy * gv
    mdot = jnp.mean(dyg * xhat, axis=-1, keepdims=True)
    dx = r * (dyg - xhat * mdot)
    dg = jnp.sum(dy * xhat, axis=0, keepdims=True)
    return dx, dg


def _rms_bwd(x, g, dy, dres, *, name):
    s, d = x.shape
    bm = _row_block(s)
    has_res = dres is not None

    def body(*refs):
        x_ref, g_ref, dy_ref = refs[:3]
        dres_ref = refs[3] if has_res else None
        dx_ref, dxb_ref, dg_ref = refs[-3:]
        dx, dg = _rms_bwd_math(x_ref[...], g_ref[...], dy_ref[...].astype(F32))
        if has_res:
            dx = dx + dres_ref[...]
        dx_ref[...] = dx
        dxb_ref[...] = dx.astype(BF16)

        @pl.when(pl.program_id(0) == 0)
        def _():
            dg_ref[...] = jnp.zeros_like(dg_ref)

        dg_ref[...] += dg

    row = pl.BlockSpec((bm, d), lambda i: (i, 0))
    vec = pl.BlockSpec((1, d), lambda i: (0, 0))
    ins = [x, g.reshape(1, d), dy] + ([dres] if has_res else [])
    return pl.pallas_call(
        body, name=name, grid=(s // bm,),
        in_specs=[row, vec, row] + ([row] if has_res else []),
        out_specs=[row, row, vec],
        out_shape=[jax.ShapeDtypeStruct((s, d), F32), jax.ShapeDtypeStruct((s, d), BF16),
                   jax.ShapeDtypeStruct((1, d), F32)],
        compiler_params=pltpu.CompilerParams(dimension_semantics=("arbitrary",), vmem_limit_bytes=VMEM_LIMIT),
    )(*ins)


def _loss_bwd(x, g, target, *, name):
    s, d = x.shape
    bm = _row_block(s)

    def body(x_ref, g_ref, t_ref, loss_ref, dx_ref, dxb_ref, dg_ref):
        xv, gv = x_ref[...], g_ref[...]
        r = lax.rsqrt(jnp.mean(xv * xv, axis=-1, keepdims=True) + NORM_EPS)
        err = (xv * r) * gv - t_ref[...]
        dx, dg = _rms_bwd_math(xv, gv, err * (1.0 / d))
        dx_ref[...] = dx
        dxb_ref[...] = dx.astype(BF16)

        @pl.when(pl.program_id(0) == 0)
        def _():
            dg_ref[...] = jnp.zeros_like(dg_ref)
            loss_ref[...] = jnp.zeros_like(loss_ref)

        dg_ref[...] += dg
        loss_ref[...] += jnp.sum(jnp.mean(err * err, axis=-1, keepdims=True), axis=0, keepdims=True) * 0.5

    row = pl.BlockSpec((bm, d), lambda i: (i, 0))
    vec = pl.BlockSpec((1, d), lambda i: (0, 0))
    return pl.pallas_call(
        body, name=name, grid=(s // bm,),
        in_specs=[row, vec, row],
        out_specs=[pl.BlockSpec((8, LANES), lambda i: (0, 0)), row, row, vec],
        out_shape=[jax.ShapeDtypeStruct((8, LANES), F32), jax.ShapeDtypeStruct((s, d), F32),
                   jax.ShapeDtypeStruct((s, d), BF16), jax.ShapeDtypeStruct((1, d), F32)],
        compiler_params=pltpu.CompilerParams(dimension_semantics=("arbitrary",), vmem_limit_bytes=VMEM_LIMIT),
    )(x, g.reshape(1, d), target)


def _kv_prep(down, g, cos_t, sin_t, *, name):
    s, w = down.shape
    bm = _row_block(s)

    def body(d_ref, g_ref, c_ref, s_ref, o_ref):
        lat = d_ref[:, :MLA_KV_RANK]
        r = lax.rsqrt(jnp.mean(lat * lat, axis=-1, keepdims=True) + NORM_EPS)
        o_ref[:, :MLA_KV_RANK] = ((lat * r) * g_ref[...]).astype(BF16)
        o_ref[:, MLA_KV_RANK:] = _rope_slab(d_ref[:, MLA_KV_RANK:], c_ref[...], s_ref[...]).astype(BF16)

    row = pl.BlockSpec((bm, w), lambda i: (i, 0))
    tab = pl.BlockSpec((bm, LANES), lambda i: (i, 0))
    return pl.pallas_call(
        body, name=name, grid=(s // bm,),
        in_specs=[row, pl.BlockSpec((1, MLA_KV_RANK), lambda i: (0, 0)), tab, tab],
        out_specs=row, out_shape=jax.ShapeDtypeStruct((s, w), BF16),
        compiler_params=pltpu.CompilerParams(dimension_semantics=("parallel",), vmem_limit_bytes=VMEM_LIMIT),
    )(down, g.reshape(1, MLA_KV_RANK), cos_t, sin_t)


def _kv_prep_bwd(down, g, cos_t, sin_t, dcat, *, name):
    s, w = down.shape
    bm = _row_block(s)

    def body(d_ref, g_ref, c_ref, s_ref, dc_ref, o_ref, dg_ref):
        dlat, dg = _rms_bwd_math(d_ref[:, :MLA_KV_RANK], g_ref[...], dc_ref[:, :MLA_KV_RANK])
        o_ref[:, :MLA_KV_RANK] = dlat.astype(BF16)
        o_ref[:, MLA_KV_RANK:] = _rope_slab_bwd(dc_ref[:, MLA_KV_RANK:], c_ref[...], s_ref[...]).astype(BF16)

        @pl.when(pl.program_id(0) == 0)
        def _():
            dg_ref[...] = jnp.zeros_like(dg_ref)

        dg_ref[...] += dg

    row = pl.BlockSpec((bm, w), lambda i: (i, 0))
    tab = pl.BlockSpec((bm, LANES), lambda i: (i, 0))
    vec = pl.BlockSpec((1, MLA_KV_RANK), lambda i: (0, 0))
    return pl.pallas_call(
        body, name=name, grid=(s // bm,),
        in_specs=[row, vec, tab, tab, row],
        out_specs=[row, vec],
        out_shape=[jax.ShapeDtypeStruct((s, w), BF16), jax.ShapeDtypeStruct((1, MLA_KV_RANK), F32)],
        compiler_params=pltpu.CompilerParams(dimension_semantics=("arbitrary",), vmem_limit_bytes=VMEM_LIMIT),
    )(down, g.reshape(1, MLA_KV_RANK), cos_t, sin_t, dcat)


def _split_bf16(v):
    hi = v.astype(BF16)
    lo = (v - hi.astype(F32)).astype(BF16)
    return hi, lo


def _suffix_matrices(n):
    row = lax.broadcasted_iota(jnp.int32, (n, n), 0)
    col = lax.broadcasted_iota(jnp.int32, (n, n), 1)
    incl = (row >= col).astype(BF16)
    return (row > col).astype(BF16), jnp.concatenate([incl, incl], axis=0)


def _suffix_sum(v, matrix):
    hi, lo = _split_bf16(v)
    return _dot_nn(jnp.concatenate([hi, lo], axis=1), matrix)


def _block_positions(qi, kb, bq, bk):
    row = qi * bq + lax.broadcasted_iota(jnp.int32, (bq, bk), 0)
    col = kb * bk + lax.broadcasted_iota(jnp.int32, (bq, bk), 1)
    return row, col


def _att_blocks(s):
    bq, bk = min(ATT_Q_BLOCK, s), min(ATT_K_BLOCK, s)
    return bq, bk, s // bq, bq // bk


def _sweep(qi, ratio, step):
    for d in range(ratio):
        step((qi + 1) * ratio - 1 - d, True)
    unroll = SWEEP_UNROLL if ratio % SWEEP_UNROLL == 0 else 1

    def loop(i, carry):
        for u in range(unroll):
            step(qi * ratio - 1 - (i * unroll + u), False)
        return carry

    lax.fori_loop(0, qi * (ratio // unroll), loop, 0)


def _sb_logs(q, k):
    z = _dot_nt(q, k)
    lb = jnp.minimum(z, 0.0) - jnp.log(1.0 + jnp.exp(-jnp.abs(z)))
    return lb, lb - z


def _sb_fwd(qkv, heads, *, name):
    s = qkv.shape[0]
    bq, bk, nq, ratio = _att_blocks(s)

    def body(q_ref, k_ref, v_ref, o_ref, acc_ref, c_ref):
        qi = pl.program_id(1)
        q = q_ref[...] * SB_SCALE
        m_strict, _ = _suffix_matrices(bk)
        acc_ref[...] = jnp.zeros_like(acc_ref)
        c_ref[...] = jnp.zeros_like(c_ref)

        def step(kb, masked):
            rows = pl.ds(pl.multiple_of(kb * bk, bk), bk)
            k, v = k_ref[rows, :], v_ref[rows, :]
            lb, lk = _sb_logs(q, k)
            if masked:
                row, col = _block_positions(qi, kb, bq, bk)
                causal = col < row
                lk = jnp.where(causal, lk, 0.0)
            c = c_ref[...]
            w = jnp.exp(lb + _dot_nn(lk.astype(BF16), m_strict) + jnp.tile(c, (1, bk // LANES)))
            if masked:
                w = jnp.where(causal, w, 0.0)
            acc_ref[...] += _dot_nn(w.astype(BF16), v)
            c_ref[...] = c + jnp.sum(lk, axis=-1, keepdims=True)

        _sweep(qi, ratio, step)
        o_ref[...] = acc_ref[...].astype(o_ref.dtype)

    return pl.pallas_call(
        body, name=name, grid=(heads, nq),
        in_specs=[pl.BlockSpec((bq, LANES), lambda h, i: (i, h)),
                  pl.BlockSpec((s, LANES), lambda h, i: (0, heads + h)),
                  pl.BlockSpec((s, LANES), lambda h, i: (0, 2 * heads + h))],
        out_specs=pl.BlockSpec((bq, LANES), lambda h, i: (i, h)),
        out_shape=jax.ShapeDtypeStruct((s, heads * LANES), F32),
        scratch_shapes=[pltpu.VMEM((bq, LANES), F32), pltpu.VMEM((bq, LANES), F32)],
        compiler_params=pltpu.CompilerParams(dimension_semantics=("parallel", "arbitrary"),
                                             vmem_limit_bytes=VMEM_LIMIT),
    )(qkv, qkv, qkv)


def _sb_bwd(qkv, o, do, heads, *, name):
    s = qkv.shape[0]
    bq, bk, nq, ratio = _att_blocks(s)

    def body(q_ref, k_ref, v_ref, o_ref, do_ref, dq_ref, dk_ref, dv_ref, dq_acc, dk_acc, dv_acc, c_ref, e_ref):
        qi = pl.program_id(1)

        @pl.when(qi == 0)
        def _():
            dk_acc[...] = jnp.zeros_like(dk_acc)
            dv_acc[...] = jnp.zeros_like(dv_acc)

        q = q_ref[...] * SB_SCALE
        do = do_ref[...]
        total = jnp.sum(do.astype(F32) * o_ref[...].astype(F32), axis=-1, keepdims=True)
        m_strict, m_incl = _suffix_matrices(bk)
        dq_acc[...] = jnp.zeros_like(dq_acc)
        c_ref[...] = jnp.zeros_like(c_ref)
        e_ref[...] = jnp.broadcast_to(total, e_ref.shape)
        reps = (1, bk // LANES)

        def step(kb, masked):
            rows = pl.ds(pl.multiple_of(kb * bk, bk), bk)
            k, v = k_ref[rows, :], v_ref[rows, :]
            lb, lk_all = _sb_logs(q, k)
            lk = lk_all
            if masked:
                row, col = _block_positions(qi, kb, bq, bk)
                causal = col < row
                lk = jnp.where(causal, lk_all, 0.0)
            c = c_ref[...]
            w = jnp.exp(lb + _dot_nn(lk.astype(BF16), m_strict) + jnp.tile(c, reps))
            if masked:
                w = jnp.where(causal, w, 0.0)
            wb = w.astype(BF16)
            g = wb.astype(F32) * _dot_nt(do, v)
            e = e_ref[...]
            g_left = jnp.tile(e, reps) - _suffix_sum(g, m_incl)
            da = g * jnp.exp(lk_all) - jnp.exp(lb) * g_left
            if masked:
                da = jnp.where(causal, da, 0.0)
            dab = da.astype(BF16)
            dq_acc[...] += _dot_nn(dab, k)
            dk_acc[rows, :] += _dot_tn(dab, q)
            dv_acc[rows, :] += _dot_tn(wb, do)
            e_ref[...] = e - jnp.sum(g, axis=-1, keepdims=True)
            c_ref[...] = c + jnp.sum(lk, axis=-1, keepdims=True)

        _sweep(qi, ratio, step)
        dq_ref[...] = (dq_acc[...] * SB_SCALE).astype(dq_ref.dtype)

        @pl.when(qi == nq - 1)
        def _():
            dk_ref[...] = dk_acc[...].astype(dk_ref.dtype)
            dv_ref[...] = dv_acc[...].astype(dv_ref.dtype)

    blk = pl.BlockSpec((bq, LANES), lambda h, i: (i, h))
    full = pl.BlockSpec((s, LANES), lambda h, i: (0, h))
    shape = jax.ShapeDtypeStruct((s, heads * LANES), BF16)
    return pl.pallas_call(
        body, name=name, grid=(heads, nq),
        in_specs=[blk,
                  pl.BlockSpec((s, LANES), lambda h, i: (0, heads + h)),
                  pl.BlockSpec((s, LANES), lambda h, i: (0, 2 * heads + h)),
                  blk, blk],
        out_specs=[blk, full, full],
        out_shape=[shape, shape, shape],
        scratch_shapes=[pltpu.VMEM((bq, LANES), F32), pltpu.VMEM((s, LANES), F32), pltpu.VMEM((s, LANES), F32),
                        pltpu.VMEM((bq, LANES), F32), pltpu.VMEM((bq, LANES), F32)],
        compiler_params=pltpu.CompilerParams(dimension_semantics=("arbitrary", "arbitrary"),
                                             vmem_limit_bytes=VMEM_LIMIT),
    )(qkv, qkv, qkv, o, do)


def _chunk_allowed(qi, kb, bq, bk):
    row, col = _block_positions(qi, kb, bq, bk)
    return (col // CHUNK) <= (row // CHUNK)


def _mla_fwd(q, kv, heads, *, name):
    s = q.shape[0]
    bq, bk, nq, ratio = _att_blocks(s)
    reps = (1, bk // LANES)

    def body(q_ref, k_ref, v_ref, o_ref, lse_ref, acc_ref, m_ref, l_ref):
        qi = pl.program_id(1)
        qv = q_ref[...]
        acc_ref[...] = jnp.zeros_like(acc_ref)
        m_ref[...] = jnp.full_like(m_ref, NEG_BIG)
        l_ref[...] = jnp.zeros_like(l_ref)

        def step(kb, masked):
            rows = pl.ds(pl.multiple_of(kb * bk, bk), bk)
            k, v = k_ref[rows, :], v_ref[rows, :]
            sc = _dot_nt(qv, k)
            if masked:
                allowed = _chunk_allowed(qi, kb, bq, bk)
                sc = jnp.where(allowed, sc, NEG_BIG)
            m_old = m_ref[...]
            m_new = jnp.maximum(m_old, jnp.max(sc, axis=-1, keepdims=True))
            p = jnp.exp(sc - jnp.tile(m_new, reps))
            if masked:
                p = jnp.where(allowed, p, 0.0)
            alpha = jnp.exp(m_old - m_new)
            l_ref[...] = alpha * l_ref[...] + jnp.sum(p, axis=-1, keepdims=True)
            acc_ref[...] = alpha * acc_ref[...] + _dot_nn(p.astype(BF16), v)
            m_ref[...] = m_new

        _sweep(qi, ratio, step)
        o_ref[...] = (acc_ref[...] / l_ref[...]).astype(o_ref.dtype)
        lse_ref[...] = m_ref[...] + jnp.log(l_ref[...])

    blk = pl.BlockSpec((bq, LANES), lambda h, i: (i, h))
    return pl.pallas_call(
        body, name=name, grid=(heads, nq),
        in_specs=[blk,
                  pl.BlockSpec((s, LANES), lambda h, i: (0, h)),
                  pl.BlockSpec((s, LANES), lambda h, i: (0, heads + h))],
        out_specs=[blk, blk],
        out_shape=[jax.ShapeDtypeStruct((s, heads * LANES), BF16), jax.ShapeDtypeStruct((s, heads * LANES), F32)],
        scratch_shapes=[pltpu.VMEM((bq, LANES), F32), pltpu.VMEM((bq, LANES), F32), pltpu.VMEM((bq, LANES), F32)],
        compiler_params=pltpu.CompilerParams(dimension_semantics=("parallel", "arbitrary"),
                                             vmem_limit_bytes=VMEM_LIMIT),
    )(q, kv, kv)


def _mla_bwd(q, kv, o, do, lse, cos_t, sin_t, dkv_init, heads, *, name):
    s = q.shape[0]
    bq, bk, nq, ratio = _att_blocks(s)
    reps = (1, bk // LANES)
    has_init = dkv_init is not None

    def body(*refs):
        q_ref, k_ref, v_ref, o_ref, do_ref, lse_ref, c_ref, s_ref = refs[:8]
        ki_ref, vi_ref = (refs[8], refs[9]) if has_init else (None, None)
        dq_ref, dk_ref, dv_ref, dq_acc, dk_acc, dv_acc = refs[-6:]
        qi = pl.program_id(1)

        @pl.when(qi == 0)
        def _():
            if has_init:
                dk_acc[...] = ki_ref[...].astype(F32)
                dv_acc[...] = vi_ref[...].astype(F32)
            else:
                dk_acc[...] = jnp.zeros_like(dk_acc)
                dv_acc[...] = jnp.zeros_like(dv_acc)

        qv = q_ref[...]
        do = do_ref[...]
        delta = jnp.sum(do.astype(F32) * o_ref[...].astype(F32), axis=-1, keepdims=True)
        lse_wide = jnp.tile(lse_ref[...], reps)
        dq_acc[...] = jnp.zeros_like(dq_acc)

        def step(kb, masked):
            rows = pl.ds(pl.multiple_of(kb * bk, bk), bk)
            k, v = k_ref[rows, :], v_ref[rows, :]
            p = jnp.exp(_dot_nt(qv, k) - lse_wide)
            if masked:
                p = jnp.where(_chunk_allowed(qi, kb, bq, bk), p, 0.0)
            ds = (p * (_dot_nt(do, v) - delta)).astype(BF16)
            dq_acc[...] += _dot_nn(ds, k)
            dk_acc[rows, :] += _dot_tn(ds, qv)
            dv_acc[rows, :] += _dot_tn(p.astype(BF16), do)

        _sweep(qi, ratio, step)
        dq_ref[...] = _rope_slab_bwd(dq_acc[...] * MLA_SCALE, c_ref[...], s_ref[...]).astype(dq_ref.dtype)

        @pl.when(qi == nq - 1)
        def _():
            dk_ref[...] = dk_acc[...].astype(dk_ref.dtype)
            dv_ref[...] = dv_acc[...].astype(dv_ref.dtype)

    blk = pl.BlockSpec((bq, LANES), lambda h, i: (i, h))
    tab = pl.BlockSpec((bq, LANES), lambda h, i: (i, 0))
    k_full = pl.BlockSpec((s, LANES), lambda h, i: (0, h))
    v_full = pl.BlockSpec((s, LANES), lambda h, i: (0, heads + h))
    shape = jax.ShapeDtypeStruct((s, heads * LANES), BF16)
    ins = [q, kv, kv, o, do, lse, cos_t, sin_t] + ([dkv_init, dkv_init] if has_init else [])
    dq, dk, dv = pl.pallas_call(
        body, name=name, grid=(heads, nq),
        in_specs=[blk, k_full, v_full, blk, blk, blk, tab, tab] + ([k_full, v_full] if has_init else []),
        out_specs=[blk, k_full, k_full],
        out_shape=[shape, shape, shape],
        scratch_shapes=[pltpu.VMEM((bq, LANES), F32), pltpu.VMEM((s, LANES), F32), pltpu.VMEM((s, LANES), F32)],
        compiler_params=pltpu.CompilerParams(dimension_semantics=("arbitrary", "arbitrary"),
                                             vmem_limit_bytes=VMEM_LIMIT),
    )(*ins)
    return dq, jnp.concatenate([dk, dv], axis=1)


def _pad_last(a, width):
    return jnp.pad(a, [(0, 0)] * (a.ndim - 1) + [(0, width - a.shape[-1])])


def _pad_qkv(w, heads):
    d = w.shape[0]
    return _pad_last(w.reshape(d, 3 * heads, SB_HEAD_DIM), LANES).reshape(d, 3 * heads * LANES)


def _unpad_qkv(g, heads):
    d = g.shape[0]
    return g.reshape(d, 3 * heads, LANES)[:, :, :SB_HEAD_DIM].reshape(d, 3 * heads * SB_HEAD_DIM)


def _pad_o(w, heads):
    d = w.shape[1]
    w = w.reshape(heads, SB_HEAD_DIM, d)
    return jnp.pad(w, [(0, 0), (0, LANES - SB_HEAD_DIM), (0, 0)]).reshape(heads * LANES, d)


def _unpad_o(g, heads):
    d = g.shape[1]
    return g.reshape(heads, LANES, d)[:, :SB_HEAD_DIM, :].reshape(heads * SB_HEAD_DIM, d)


def _pad_uq(w, heads):
    r = w.shape[0]
    return _pad_last(w.reshape(r, heads, MLA_NOPE + MLA_ROPE), LANES).reshape(r, heads * LANES)


def _unpad_uq(g, heads):
    r = g.shape[0]
    return g.reshape(r, heads, LANES)[:, :, :MLA_NOPE + MLA_ROPE].reshape(r, heads * (MLA_NOPE + MLA_ROPE))


def _pad_dkv(w):
    d = w.shape[0]
    rope = jnp.zeros((d, LANES), w.dtype).at[:, ROPE_LO:ROPE_LO + MLA_ROPE].set(w[:, MLA_KV_RANK:])
    return jnp.concatenate([w[:, :MLA_KV_RANK], rope], axis=1)


def _unpad_dkv(g):
    return jnp.concatenate([g[:, :MLA_KV_RANK], g[:, MLA_KV_RANK + ROPE_LO:MLA_KV_RANK + ROPE_LO + MLA_ROPE]], axis=1)


def _pad_ukv(w, heads):
    w = w.reshape(MLA_KV_RANK, heads, 2, MLA_NOPE)
    k_part = _pad_last(w[:, :, 0, :], LANES).reshape(MLA_KV_RANK, heads * LANES)
    v_part = _pad_last(w[:, :, 1, :], LANES).reshape(MLA_KV_RANK, heads * LANES)
    lane = jnp.arange(LANES)
    place = ((lane[:, None] == lane[None, :]) & (lane[:, None] >= ROPE_LO) & (lane[:, None] < ROPE_LO + MLA_ROPE))
    place = jnp.tile(place.astype(w.dtype), (1, heads))
    top = jnp.concatenate([k_part, v_part], axis=1)
    bottom = jnp.concatenate([place, jnp.zeros_like(place)], axis=1)
    return jnp.concatenate([top, bottom], axis=0)


def _unpad_ukv(g, heads):
    g = g[:MLA_KV_RANK]
    k_part = g[:, :heads * LANES].reshape(MLA_KV_RANK, heads, LANES)[:, :, :MLA_NOPE]
    v_part = g[:, heads * LANES:].reshape(MLA_KV_RANK, heads, LANES)[:, :, :MLA_V]
    return jnp.stack([k_part, v_part], axis=2).reshape(MLA_KV_RANK, heads * (MLA_NOPE + MLA_V))


def _rope_tables(positions):
    inv_freq = ROPE_THETA ** (-jnp.arange(0, MLA_ROPE, 2, dtype=F32) / MLA_ROPE)
    ang = positions.astype(F32)[:, None] * inv_freq
    cos, sin = jnp.cos(ang), jnp.sin(ang)
    s = positions.shape[0]
    cos_t = jnp.ones((s, LANES), F32).at[:, ROPE_LO:ROPE_LO + MLA_ROPE].set(jnp.concatenate([cos, cos], axis=1))
    sin_t = jnp.zeros((s, LANES), F32).at[:, ROPE_LO:ROPE_LO + MLA_ROPE].set(jnp.concatenate([-sin, sin], axis=1))
    return cos_t, sin_t


def _local_step(x, positions, target, w, norms):
    s, d = x.shape
    heads = d // SB_HEAD_DIM
    n_a = w["sb_w_qkv"].shape[0]
    n_b = w["mla_w_dq"].shape[0]
    depth = n_a + n_b
    cos_t, sin_t = _rope_tables(positions)

    wqkv = [_pad_qkv(w["sb_w_qkv"][l], heads) for l in range(n_a)]
    wo_a = [_pad_o(w["sb_w_o"][l], heads) for l in range(n_a)]
    wdkv = _pad_dkv(w["mla_w_dkv"])
    wkv = _pad_ukv(w["mla_w_ukv"], heads)
    wdq = [w["mla_w_dq"][j] for j in range(n_b)]
    wuq = [_pad_uq(w["mla_w_uq"][j], heads) for j in range(n_b)]
    wo_b = [_pad_o(w["mla_w_o"][j], heads) for j in range(n_b)]
    w1 = [w["mlp_w1"][l] for l in range(depth)]
    w2 = [w["mlp_w2"][l] for l in range(depth)]

    saved = []
    kv_saved = None
    kv = None
    for l in range(depth):
        t = f"l{l}"
        sv = {"x_in": x}
        h = _rms_fwd(x, norms["attn_norm"][l], name=f"{t}_attn_norm")
        sv["h"] = h
        if l < n_a:
            qkv = _mm(h, wqkv[l], name=f"{t}_qkv")
            o = _sb_fwd(qkv, heads, name=f"{t}_sb_fwd")
            sv["qkv"], sv["o"] = qkv, o
            x = _mm(o, wo_a[l], name=f"{t}_attn_out", epilogue=_epi_add, extras=[(x, "tile")], out_dtypes=(F32,))
        else:
            j = l - n_a
            if j == 0:
                hk = _rms_fwd(x, norms["kv_norm"], name="kv_norm")
                down = _mm(hk, wdkv, name="kv_down", out_dtypes=(F32,))
                cat = _kv_prep(down, norms["mla_kv_lat_norm"], cos_t, sin_t, name="kv_prep")
                kv = _mm(cat, wkv, name="kv_up")
                kv_saved = {"x_in": x, "hk": hk, "down": down, "cat": cat}
            cq0 = _mm(h, wdq[j], name=f"{t}_q_down", out_dtypes=(F32,))
            cq = _rms_fwd(cq0, norms["mla_q_lat_norm"][j], name=f"{t}_q_lat_norm")
            q = _mm(cq, wuq[j], name=f"{t}_q_up", epilogue=_epi_rope_heads, extras=[(cos_t, "row"), (sin_t, "row")])
            o, lse = _mla_fwd(q, kv, heads, name=f"{t}_mla_fwd")
            sv.update(cq0=cq0, cq=cq, q=q, o=o, lse=lse)
            x = _mm(o, wo_b[j], name=f"{t}_attn_out", epilogue=_epi_add, extras=[(x, "tile")], out_dtypes=(F32,))
        sv["x_mid"] = x
        h2 = _rms_fwd(x, norms["mlp_norm"][l], name=f"{t}_mlp_norm")
        u, a = _mm(h2, w1[l], name=f"{t}_mlp_up", epilogue=_epi_relu2, out_dtypes=(BF16, BF16))
        sv.update(h2=h2, u=u, a=a)
        x = _mm(a, w2[l], name=f"{t}_mlp_down", epilogue=_epi_add, extras=[(x, "tile")], out_dtypes=(F32,))
        saved.append(sv)

    loss_slab, dx, dxb, dg_final = _loss_bwd(x, norms["final_norm"], target, name="loss")
    loss = loss_slab[0, 0]

    g_attn_norm, g_mlp_norm = [None] * depth, [None] * depth
    g_qkv, g_o_a = [None] * n_a, [None] * n_a
    g_dq, g_uq, g_o_b, g_qlat = [None] * n_b, [None] * n_b, [None] * n_b, [None] * n_b
    g_w1, g_w2 = [None] * depth, [None] * depth
    dkv = None
    g_kv_norm = g_kv_lat = g_dkv = g_ukv = None

    for l in reversed(range(depth)):
        t = f"l{l}"
        sv = saved[l]
        du = _mm(dxb, w2[l], name=f"{t}_mlp_down_dx", dims="nt", epilogue=_epi_relu2_grad, extras=[(sv["u"], "tile")])
        g_w2[l] = _mm(sv["a"], dxb, name=f"{t}_mlp_down_dw", dims="tn", out_dtypes=(F32,))
        g_w1[l] = _mm(sv["h2"], du, name=f"{t}_mlp_up_dw", dims="tn", out_dtypes=(F32,))
        dh2 = _mm(du, w1[l], name=f"{t}_mlp_up_dx", dims="nt", out_dtypes=(F32,))
        dx, dxb, g_mlp_norm[l] = _rms_bwd(sv["x_mid"], norms["mlp_norm"][l], dh2, dx, name=f"{t}_mlp_norm_bwd")
        if l < n_a:
            do = _mm(dxb, wo_a[l], name=f"{t}_attn_out_dx", dims="nt")
            g_o_a[l] = _unpad_o(_mm(sv["o"], dxb, name=f"{t}_attn_out_dw", dims="tn", out_dtypes=(F32,)), heads)
            dq, dk, dv = _sb_bwd(sv["qkv"], sv["o"], do, heads, name=f"{t}_sb_bwd")
            dqkv = jnp.concatenate([dq, dk, dv], axis=1)
            g_qkv[l] = _unpad_qkv(_mm(sv["h"], dqkv, name=f"{t}_qkv_dw", dims="tn", out_dtypes=(F32,)), heads)
            dh = _mm(dqkv, wqkv[l], name=f"{t}_qkv_dx", dims="nt", out_dtypes=(F32,))
        else:
            j = l - n_a
            do = _mm(dxb, wo_b[j], name=f"{t}_attn_out_dx", dims="nt")
            g_o_b[j] = _unpad_o(_mm(sv["o"], dxb, name=f"{t}_attn_out_dw", dims="tn", out_dtypes=(F32,)), heads)
            dq, dkv = _mla_bwd(sv["q"], kv, sv["o"], do, sv["lse"], cos_t, sin_t, dkv, heads, name=f"{t}_mla_bwd")
            g_uq[j] = _unpad_uq(_mm(sv["cq"], dq, name=f"{t}_q_up_dw", dims="tn", out_dtypes=(F32,)), heads)
            dcq = _mm(dq, wuq[j], name=f"{t}_q_up_dx", dims="nt", out_dtypes=(F32,))
            _, dcq0, g_qlat[j] = _rms_bwd(sv["cq0"], norms["mla_q_lat_norm"][j], dcq, None, name=f"{t}_q_lat_norm_bwd")
            g_dq[j] = _mm(sv["h"], dcq0, name=f"{t}_q_down_dw", dims="tn", out_dtypes=(F32,))
            dh = _mm(dcq0, wdq[j], name=f"{t}_q_down_dx", dims="nt", out_dtypes=(F32,))
        dx, dxb, g_attn_norm[l] = _rms_bwd(sv["x_in"], norms["attn_norm"][l], dh, dx, name=f"{t}_attn_norm_bwd")
        if l == n_a:
            ks = kv_saved
            dcat = _mm(dkv, wkv, name="kv_up_dx", dims="nt", out_dtypes=(F32,))
            g_ukv = _unpad_ukv(_mm(ks["cat"], dkv, name="kv_up_dw", dims="tn", out_dtypes=(F32,)), heads)
            ddown, g_kv_lat = _kv_prep_bwd(ks["down"], norms["mla_kv_lat_norm"], cos_t, sin_t, dcat, name="kv_prep_bwd")
            g_dkv = _unpad_dkv(_mm(ks["hk"], ddown, name="kv_down_dw", dims="tn", out_dtypes=(F32,)))
            dhk = _mm(ddown, wdkv, name="kv_down_dx", dims="nt", out_dtypes=(F32,))
            dx, dxb, g_kv_norm = _rms_bwd(ks["x_in"], norms["kv_norm"], dhk, dx, name="kv_norm_bwd")

    grads = {
        "attn_norm": jnp.concatenate(g_attn_norm, axis=0), "mlp_norm": jnp.concatenate(g_mlp_norm, axis=0),
        "sb_w_qkv": jnp.stack(g_qkv), "sb_w_o": jnp.stack(g_o_a),
        "kv_norm": g_kv_norm[0], "mla_w_dkv": g_dkv, "mla_kv_lat_norm": g_kv_lat[0], "mla_w_ukv": g_ukv,
        "mla_w_dq": jnp.stack(g_dq), "mla_q_lat_norm": jnp.concatenate(g_qlat, axis=0),
        "mla_w_uq": jnp.stack(g_uq), "mla_w_o": jnp.stack(g_o_b),
        "mlp_w1": jnp.stack(g_w1), "mlp_w2": jnp.stack(g_w2), "final_norm": dg_final[0],
    }
    return loss, dx, grads


def _flat_rows(n_elems):
    per_block = FLAT_COLS * FLAT_ROW_BLOCK * 2
    return -(-n_elems // per_block) * FLAT_ROW_BLOCK * 2


def _pack(arrays, dtype):
    flat = jnp.concatenate([a.reshape(-1).astype(dtype) for a in arrays])
    rows = _flat_rows(flat.shape[0])
    flat = jnp.pad(flat, (0, rows * FLAT_COLS - flat.shape[0]))
    return flat.reshape(rows, FLAT_COLS)


def _unpack(flat, shapes):
    flat = flat.reshape(-1)
    out, off = [], 0
    for shp in shapes:
        n = 1
        for v in shp:
            n *= v
        out.append(flat[off:off + n].reshape(shp))
        off += n
    return out


def _pack_small(arrays):
    rows = []
    for a in arrays:
        a = a.reshape(-1, a.shape[-1]) if a.shape[-1] == FLAT_COLS else a.reshape(1, -1)
        rows.append(_pad_last(a, FLAT_COLS))
    flat = jnp.concatenate(rows, axis=0)
    return jnp.pad(flat, [(0, -flat.shape[0] % 8), (0, 0)])


def _unpack_small(flat, shapes):
    out, row = [], 0
    for shp in shapes:
        if shp[-1] == FLAT_COLS:
            n = 1
            for v in shp[:-1]:
                n *= v
            out.append(flat[row:row + n].reshape(shp))
            row += n
        else:
            n = 1
            for v in shp:
                n *= v
            out.append(flat[row, :n].reshape(shp))
            row += 1
    return out


def _other_chips(x, y):
    return [(1 - x, y), (x, 1 - y), (1 - x, 1 - y)]


def _all_gather_chips(flat, *, name):
    rows, cols = flat.shape

    def body(x_ref, out_ref, send_sems, recv_sems, pass_send_sems, pass_recv_sems):
        x, y, c = lax.axis_index("x"), lax.axis_index("y"), lax.axis_index("c")
        me = 2 * x + y
        my_rows, sib_rows = _half_rows(rows)
        chips = _other_chips(x, y)
        sends = []
        for k, (px, py) in enumerate(chips):
            cp = pltpu.make_async_remote_copy(src_ref=x_ref.at[my_rows, :], dst_ref=out_ref.at[me, my_rows, :],
                                              send_sem=send_sems.at[k], recv_sem=recv_sems.at[k],
                                              device_id=(px, py, c), device_id_type=MESH)
            cp.start()
            sends.append(cp)
        for k, (px, py) in enumerate(chips):
            landed = out_ref.at[2 * px + py, my_rows, :]
            pltpu.make_async_remote_copy(src_ref=landed, dst_ref=landed, send_sem=send_sems.at[k],
                                         recv_sem=recv_sems.at[k], device_id=(px, py, c),
                                         device_id_type=MESH).wait_recv()
            cp = pltpu.make_async_remote_copy(src_ref=landed, dst_ref=landed, send_sem=pass_send_sems.at[k],
                                              recv_sem=pass_recv_sems.at[k], device_id=_sibling(),
                                              device_id_type=MESH)
            cp.start()
            sends.append(cp)
        for k, (px, py) in enumerate(chips):
            passed = out_ref.at[2 * px + py, sib_rows, :]
            pltpu.make_async_remote_copy(src_ref=passed, dst_ref=passed, send_sem=pass_send_sems.at[k],
                                         recv_sem=pass_recv_sems.at[k], device_id=_sibling(),
                                         device_id_type=MESH).wait_recv()
        for cp in sends:
            cp.wait_send()

    out = pl.pallas_call(
        body, name=name,
        in_specs=[pl.BlockSpec(memory_space=pltpu.HBM)],
        out_specs=pl.BlockSpec(memory_space=pltpu.HBM),
        out_shape=jax.ShapeDtypeStruct((N_CHIPS, rows, cols), flat.dtype),
        scratch_shapes=[pltpu.SemaphoreType.DMA((3,)), pltpu.SemaphoreType.DMA((3,)), pltpu.SemaphoreType.DMA((3,)),
                        pltpu.SemaphoreType.DMA((3,))],
        compiler_params=pltpu.CompilerParams(has_side_effects=True),
    )(flat)
    return lax.dynamic_update_index_in_dim(out, flat, _my_chip(), 0)


def _exchange_chips(parts, *, name):
    def body(g_ref, out_ref, send_sems, recv_sems):
        x, y, c = lax.axis_index("x"), lax.axis_index("y"), lax.axis_index("c")
        me = 2 * x + y
        sends = []
        for k, (px, py) in enumerate(_other_chips(x, y)):
            cp = pltpu.make_async_remote_copy(src_ref=g_ref.at[2 * px + py], dst_ref=out_ref.at[me],
                                              send_sem=send_sems.at[k], recv_sem=recv_sems.at[k],
                                              device_id=(px, py, c), device_id_type=MESH)
            cp.start()
            sends.append(cp)
        for k, (px, py) in enumerate(_other_chips(x, y)):
            pltpu.make_async_remote_copy(src_ref=g_ref.at[me], dst_ref=out_ref.at[2 * px + py],
                                         send_sem=send_sems.at[k], recv_sem=recv_sems.at[k],
                                         device_id=(px, py, c), device_id_type=MESH).wait_recv()
        for cp in sends:
            cp.wait_send()

    out = pl.pallas_call(
        body, name=name,
        in_specs=[pl.BlockSpec(memory_space=pltpu.HBM)],
        out_specs=pl.BlockSpec(memory_space=pltpu.HBM),
        out_shape=jax.ShapeDtypeStruct(parts.shape, parts.dtype),
        scratch_shapes=[pltpu.SemaphoreType.DMA((3,)), pltpu.SemaphoreType.DMA((3,))],
        compiler_params=pltpu.CompilerParams(has_side_effects=True),
    )(parts)
    me = _my_chip()
    return lax.dynamic_update_index_in_dim(out, lax.dynamic_index_in_dim(parts, me, 0, keepdims=False), me, 0)


def _my_chip():
    return 2 * lax.axis_index("x") + lax.axis_index("y")


def _half_rows(rows):
    c = lax.axis_index("c")
    half = rows // 2
    return pl.ds(pl.multiple_of(c * half, 8), half), pl.ds(pl.multiple_of((1 - c) * half, 8), half)


def _sibling():
    return (lax.axis_index("x"), lax.axis_index("y"), 1 - lax.axis_index("c"))


def _pair_exchange(parts, *, name):
    n, rows, cols = parts.shape

    def body(p_ref, theirs_ref, send_sem, recv_sem):
        _, sib_rows = _half_rows(rows)
        cp = pltpu.make_async_remote_copy(src_ref=p_ref.at[:, sib_rows, :], dst_ref=theirs_ref, send_sem=send_sem,
                                          recv_sem=recv_sem, device_id=_sibling(), device_id_type=MESH)
        cp.start()
        cp.wait()

    half = rows // 2
    theirs = pl.pallas_call(
        body, name=name,
        in_specs=[pl.BlockSpec(memory_space=pltpu.HBM)],
        out_specs=pl.BlockSpec(memory_space=pltpu.HBM),
        out_shape=jax.ShapeDtypeStruct((n, half, cols), parts.dtype),
        scratch_shapes=[pltpu.SemaphoreType.DMA, pltpu.SemaphoreType.DMA],
        compiler_params=pltpu.CompilerParams(has_side_effects=True),
    )(parts)
    mine = lax.dynamic_slice_in_dim(parts, lax.axis_index("c") * half, half, axis=1)
    return mine, theirs


def _pair_sum(mine, theirs, *, name):
    n, rows, cols = mine.shape

    def body(a_ref, b_ref, o_ref):
        o_ref[...] = (a_ref[...].astype(F32) + b_ref[...].astype(F32)).astype(o_ref.dtype)

    blk = pl.BlockSpec((n, FLAT_ROW_BLOCK, cols), lambda i: (0, i, 0))
    return pl.pallas_call(
        body, name=name, grid=(rows // FLAT_ROW_BLOCK,),
        in_specs=[blk, blk], out_specs=blk, out_shape=jax.ShapeDtypeStruct(mine.shape, mine.dtype),
        compiler_params=pltpu.CompilerParams(dimension_semantics=("parallel",), vmem_limit_bytes=VMEM_LIMIT),
    )(mine, theirs)


def _sum_chips(parts, *, name):
    _, rows, cols = parts.shape

    def body(p_ref, o_ref):
        o_ref[...] = ((p_ref[0].astype(F32) + p_ref[1].astype(F32)) + p_ref[2].astype(F32)) + p_ref[3].astype(F32)

    return pl.pallas_call(
        body, name=name, grid=(rows // FLAT_ROW_BLOCK,),
        in_specs=[pl.BlockSpec((N_CHIPS, FLAT_ROW_BLOCK, cols), lambda i: (0, i, 0))],
        out_specs=pl.BlockSpec((FLAT_ROW_BLOCK, cols), lambda i: (i, 0)),
        out_shape=jax.ShapeDtypeStruct((rows, cols), F32),
        compiler_params=pltpu.CompilerParams(dimension_semantics=("parallel",), vmem_limit_bytes=VMEM_LIMIT),
    )(parts)


def _join_cores(half, *, name):
    rows2, cols = half.shape

    def body(h_ref, out_ref, send_sem, recv_sem):
        my_rows, sib_rows = _half_rows(2 * rows2)
        cp = pltpu.make_async_remote_copy(src_ref=h_ref, dst_ref=out_ref.at[my_rows, :], send_sem=send_sem,
                                          recv_sem=recv_sem, device_id=_sibling(), device_id_type=MESH)
        cp.start()
        cp.wait_send()
        pltpu.make_async_remote_copy(src_ref=h_ref, dst_ref=out_ref.at[sib_rows, :], send_sem=send_sem,
                                     recv_sem=recv_sem, device_id=_sibling(), device_id_type=MESH).wait_recv()

    out = pl.pallas_call(
        body, name=name,
        in_specs=[pl.BlockSpec(memory_space=pltpu.HBM)],
        out_specs=pl.BlockSpec(memory_space=pltpu.HBM),
        out_shape=jax.ShapeDtypeStruct((2 * rows2, cols), half.dtype),
        scratch_shapes=[pltpu.SemaphoreType.DMA, pltpu.SemaphoreType.DMA],
        compiler_params=pltpu.CompilerParams(has_side_effects=True),
    )(half)
    return lax.dynamic_update_slice_in_dim(out, half, lax.axis_index("c") * rows2, axis=0)


def _all_reduce_small(v, *, name):
    rows, cols = v.shape
    flips = [(fx, fy, fc) for fx in (0, 1) for fy in (0, 1) for fc in (0, 1)][1:]

    def body(v_ref, out_ref, gath_ref, send_sems, recv_sems):
        x, y, c = lax.axis_index("x"), lax.axis_index("y"), lax.axis_index("c")
        me = 4 * x + 2 * y + c
        gath_ref[me] = v_ref[...]
        peers = [((1 - x) if fx else x, (1 - y) if fy else y, (1 - c) if fc else c) for fx, fy, fc in flips]
        sends = []
        for k, peer in enumerate(peers):
            cp = pltpu.make_async_remote_copy(src_ref=v_ref, dst_ref=gath_ref.at[me], send_sem=send_sems.at[k],
                                              recv_sem=recv_sems.at[k], device_id=peer, device_id_type=MESH)
            cp.start()
            sends.append(cp)
        for k, (px, py, pc) in enumerate(peers):
            pltpu.make_async_remote_copy(src_ref=v_ref, dst_ref=gath_ref.at[4 * px + 2 * py + pc],
                                         send_sem=send_sems.at[k], recv_sem=recv_sems.at[k],
                                         device_id=(px, py, pc), device_id_type=MESH).wait_recv()
        for cp in sends:
            cp.wait_send()
        total = gath_ref[0]
        for k in range(1, 8):
            total = total + gath_ref[k]
        out_ref[...] = total

    total, _ = pl.pallas_call(
        body, name=name,
        in_specs=[pl.BlockSpec(memory_space=pltpu.VMEM)],
        out_specs=[pl.BlockSpec(memory_space=pltpu.VMEM), pl.BlockSpec(memory_space=pltpu.VMEM)],
        out_shape=[jax.ShapeDtypeStruct((rows, cols), v.dtype), jax.ShapeDtypeStruct((8, rows, cols), v.dtype)],
        scratch_shapes=[pltpu.SemaphoreType.DMA((7,)), pltpu.SemaphoreType.DMA((7,))],
        compiler_params=pltpu.CompilerParams(has_side_effects=True),
    )(v)
    return total


def _adamw(w, g_parts, m, v, *, name):
    rows, cols = w.shape
    br = min(FLAT_ROW_BLOCK, rows)
    n_parts = len(g_parts)

    def body(*refs):
        w_ref = refs[0]
        g_refs = refs[1:1 + n_parts]
        m_ref, v_ref = refs[1 + n_parts], refs[2 + n_parts]
        g_out, d_out, m_out, v_out = refs[-4:]
        g = g_refs[0][...]
        for r in g_refs[1:]:
            g = g + r[...]
        m_new = ADAM_B1 * m_ref[...] + (1.0 - ADAM_B1) * g
        v_new = ADAM_B2 * v_ref[...] + (1.0 - ADAM_B2) * jnp.square(g)
        m_hat = m_new / (1.0 - ADAM_B1 ** ADAM_STEP)
        v_hat = v_new / (1.0 - ADAM_B2 ** ADAM_STEP)
        g_out[...] = g
        d_out[...] = -ADAM_LR * (m_hat / (jnp.sqrt(v_hat) + ADAM_EPS) + ADAM_WD * w_ref[...])
        m_out[...] = m_new
        v_out[...] = v_new

    blk = pl.BlockSpec((br, cols), lambda i: (i, 0))
    shape = jax.ShapeDtypeStruct((rows, cols), F32)
    return pl.pallas_call(
        body, name=name, grid=(rows // br,),
        in_specs=[blk] * (3 + n_parts), out_specs=[blk] * 4, out_shape=[shape] * 4,
        compiler_params=pltpu.CompilerParams(dimension_semantics=("parallel",), vmem_limit_bytes=VMEM_LIMIT),
    )(w, *g_parts, m, v)


def _assemble(gathered_shards, name):
    return jnp.concatenate(gathered_shards, axis=SHARD_AXIS[name])


def _chip_shard(full, name, j):
    axis = SHARD_AXIS[name]
    n = full.shape[axis] // N_CHIPS
    return lax.slice_in_dim(full, j * n, (j + 1) * n, axis=axis)


def kernel(x, positions, attn_norm, mlp_norm, sb_w_qkv, sb_w_o, kv_norm, mla_w_dkv, mla_kv_lat_norm, mla_w_ukv, mla_w_dq, mla_q_lat_norm, mla_w_uq, mla_w_o, mlp_w1, mlp_w2, final_norm, loss_target, m_attn_norm, m_mlp_norm, m_sb_w_qkv, m_sb_w_o, m_kv_norm, m_mla_w_dkv, m_mla_kv_lat_norm, m_mla_w_ukv, m_mla_w_dq, m_mla_q_lat_norm, m_mla_w_uq, m_mla_w_o, m_mlp_w1, m_mlp_w2, m_final_norm, v_attn_norm, v_mlp_norm, v_sb_w_qkv, v_sb_w_o, v_kv_norm, v_mla_w_dkv, v_mla_kv_lat_norm, v_mla_w_ukv, v_mla_w_dq, v_mla_q_lat_norm, v_mla_w_uq, v_mla_w_o, v_mlp_w1, v_mlp_w2, v_final_norm):
    weights = dict(attn_norm=attn_norm, mlp_norm=mlp_norm, sb_w_qkv=sb_w_qkv, sb_w_o=sb_w_o, kv_norm=kv_norm,
                   mla_w_dkv=mla_w_dkv, mla_kv_lat_norm=mla_kv_lat_norm, mla_w_ukv=mla_w_ukv, mla_w_dq=mla_w_dq,
                   mla_q_lat_norm=mla_q_lat_norm, mla_w_uq=mla_w_uq, mla_w_o=mla_w_o, mlp_w1=mlp_w1, mlp_w2=mlp_w2,
                   final_norm=final_norm)
    m_in = dict(attn_norm=m_attn_norm, mlp_norm=m_mlp_norm, sb_w_qkv=m_sb_w_qkv, sb_w_o=m_sb_w_o, kv_norm=m_kv_norm,
                mla_w_dkv=m_mla_w_dkv, mla_kv_lat_norm=m_mla_kv_lat_norm, mla_w_ukv=m_mla_w_ukv, mla_w_dq=m_mla_w_dq,
                mla_q_lat_norm=m_mla_q_lat_norm, mla_w_uq=m_mla_w_uq, mla_w_o=m_mla_w_o, mlp_w1=m_mlp_w1,
                mlp_w2=m_mlp_w2, final_norm=m_final_norm)
    v_in = dict(attn_norm=v_attn_norm, mlp_norm=v_mlp_norm, sb_w_qkv=v_sb_w_qkv, sb_w_o=v_sb_w_o, kv_norm=v_kv_norm,
                mla_w_dkv=v_mla_w_dkv, mla_kv_lat_norm=v_mla_kv_lat_norm, mla_w_ukv=v_mla_w_ukv, mla_w_dq=v_mla_w_dq,
                mla_q_lat_norm=v_mla_q_lat_norm, mla_w_uq=v_mla_w_uq, mla_w_o=v_mla_w_o, mlp_w1=v_mlp_w1,
                mlp_w2=v_mlp_w2, final_norm=v_final_norm)
    shard_shapes = [weights[n].shape for n in BIG_WEIGHTS]
    small_shapes = [weights[n].shape for n in SMALL_WEIGHTS]

    gathered = _all_gather_chips(_pack([weights[n] for n in BIG_WEIGHTS], BF16), name="weights_all_gather")
    per_chip = [_unpack(gathered[j], shard_shapes) for j in range(N_CHIPS)]
    full_w = {n: _assemble([per_chip[j][i] for j in range(N_CHIPS)], n) for i, n in enumerate(BIG_WEIGHTS)}
    norms = {n: weights[n] for n in SMALL_WEIGHTS}

    loss, dx, grads = _local_step(x[0], positions[0], loss_target[0], full_w, norms)
    loss = lax.psum(loss, ("x", "y", "c"))

    parts = jnp.stack([_pack([_chip_shard(grads[n], n, j) for n in BIG_WEIGHTS], BF16) for j in range(N_CHIPS)])
    mine, theirs = _pair_exchange(parts, name="grads_pair_exchange")
    chip_part = _pair_sum(mine, theirs, name="grads_pair_sum")
    received = _exchange_chips(chip_part, name="grads_exchange")
    g_half = _sum_chips(received, name="grads_sum_chips")
    g_sum = _join_cores(g_half, name="grads_join_cores")
    g_flat, d_flat, m_flat, v_flat = _adamw(
        _pack([weights[n] for n in BIG_WEIGHTS], F32), [g_sum],
        _pack([m_in[n] for n in BIG_WEIGHTS], F32), _pack([v_in[n] for n in BIG_WEIGHTS], F32), name="adamw_big")
    out_g = dict(zip(BIG_WEIGHTS, _unpack(g_flat, shard_shapes)))
    out_d = dict(zip(BIG_WEIGHTS, _unpack(d_flat, shard_shapes)))
    out_m = dict(zip(BIG_WEIGHTS, _unpack(m_flat, shard_shapes)))
    out_v = dict(zip(BIG_WEIGHTS, _unpack(v_flat, shard_shapes)))

    small_sum = _all_reduce_small(_pack_small([grads[n] for n in SMALL_WEIGHTS]), name="gains_all_reduce")
    sg, sd, sm, sv = _adamw(_pack_small([weights[n] for n in SMALL_WEIGHTS]), [small_sum],
                            _pack_small([m_in[n] for n in SMALL_WEIGHTS]),
                            _pack_small([v_in[n] for n in SMALL_WEIGHTS]), name="adamw_gains")
    out_g.update(zip(SMALL_WEIGHTS, _unpack_small(sg, small_shapes)))
    out_d.update(zip(SMALL_WEIGHTS, _unpack_small(sd, small_shapes)))
    out_m.update(zip(SMALL_WEIGHTS, _unpack_small(sm, small_shapes)))
    out_v.update(zip(SMALL_WEIGHTS, _unpack_small(sv, small_shapes)))

    return (loss, dx[None], *[out_g[n] for n in ALL_WEIGHTS], *[out_d[n] for n in ALL_WEIGHTS],
            *[out_m[n] for n in ALL_WEIGHTS], *[out_v[n] for n in ALL_WEIGHTS])
```

```python
import functools

import jax
import jax.numpy as jnp
from jax import lax
from jax.experimental import pallas as pl
from jax.experimental.pallas import tpu as pltpu

F32 = jnp.float32
BF16 = jnp.bfloat16

LANES = 128
SB_HEAD_DIM = 64
MLA_NOPE = 64
MLA_ROPE = 32
MLA_V = 64
MLA_Q_RANK = 384
MLA_KV_RANK = 256
CHUNK = 64
ROPE_THETA = 10000.0
NORM_EPS = 1e-6
SB_SCALE = SB_HEAD_DIM ** -0.5
MLA_SCALE = (MLA_NOPE + MLA_ROPE) ** -0.5
ROPE_LO = MLA_NOPE
ROPE_HALF = MLA_ROPE // 2
ATT_Q_BLOCK = 1024
ATT_K_BLOCK = 256
MLA_FWD_UNROLL = 4
NEG_BIG = -1e30
SB_DEAD_LOG = -110.0
VMEM_LIMIT = 56 * 1024 * 1024

ADAM_LR = 0.001
ADAM_B1 = 0.9
ADAM_B2 = 0.999
ADAM_EPS = 1e-08
ADAM_WD = 0.01
ADAM_STEP = 10

FLAT_COLS = 1024
FLAT_ROW_BLOCK = 256
N_CHIPS = 4
MESH = pl.DeviceIdType.MESH

BIG_WEIGHTS = ["sb_w_qkv", "sb_w_o", "mla_w_dkv", "mla_w_ukv", "mla_w_dq", "mla_w_uq", "mla_w_o", "mlp_w1", "mlp_w2"]
SHARD_AXIS = {"sb_w_qkv": 2, "sb_w_o": 1, "mla_w_dkv": 0, "mla_w_ukv": 1, "mla_w_dq": 1, "mla_w_uq": 2,
              "mla_w_o": 1, "mlp_w1": 2, "mlp_w2": 1}
SMALL_WEIGHTS = ["attn_norm", "mlp_norm", "kv_norm", "mla_kv_lat_norm", "mla_q_lat_norm", "final_norm"]
ALL_WEIGHTS = ["attn_norm", "mlp_norm", "sb_w_qkv", "sb_w_o", "kv_norm", "mla_w_dkv", "mla_kv_lat_norm", "mla_w_ukv",
               "mla_w_dq", "mla_q_lat_norm", "mla_w_uq", "mla_w_o", "mlp_w1", "mlp_w2", "final_norm"]


def _dot(a, b, dims):
    return lax.dot_general(a, b, (dims, ((), ())), preferred_element_type=F32)


def _dot_nn(a, b):
    return _dot(a, b, ((1,), (0,)))


def _dot_nt(a, b):
    return _dot(a, b, ((1,), (1,)))


def _dot_tn(a, b):
    return _dot(a, b, ((0,), (0,)))


def _pick_block(n, target):
    if n <= target:
        return n
    best = max(b for b in range(LANES, target + 1, LANES) if n % b == 0)
    return best


MM_ROWS = 512
MM_COLS = 1024
MM_DEPTH = 4096
MM_DEPTH_TN = 1024


def _mm(a, b, *, name, dims="nn", epilogue=None, extras=(), out_dtypes=(BF16,)):
    if dims == "nn":
        (m, k), (k2, n) = a.shape, b.shape
    elif dims == "nt":
        (m, k), (n, k2) = a.shape, b.shape
    else:
        (k, m), (k2, n) = a.shape, b.shape
    assert k == k2, (name, a.shape, b.shape)
    if dims == "tn":
        bm, bn, bk = _pick_block(m, MM_COLS), _pick_block(n, MM_COLS), _pick_block(k, MM_DEPTH_TN)
    else:
        bm, bn, bk = _pick_block(m, MM_ROWS), _pick_block(n, MM_COLS), _pick_block(k, MM_DEPTH)
    nk = k // bk
    if dims == "tn":
        a_spec = pl.BlockSpec((bk, bm), lambda j, i, kk: (kk, i))
    else:
        a_spec = pl.BlockSpec((bm, bk), lambda j, i, kk: (i, kk))
    if dims == "nt":
        b_spec = pl.BlockSpec((bn, bk), lambda j, i, kk: (j, kk))
    else:
        b_spec = pl.BlockSpec((bk, bn), lambda j, i, kk: (kk, j))
    extra_specs = []
    for arr, kind in extras:
        if kind == "tile":
            assert arr.shape == (m, n), (name, arr.shape)
            extra_specs.append(pl.BlockSpec((bm, bn), lambda j, i, kk: (i, j)))
        else:
            assert arr.shape == (m, LANES), (name, arr.shape)
            extra_specs.append(pl.BlockSpec((bm, LANES), lambda j, i, kk: (i, 0)))
    n_extra = len(extras)
    n_out = len(out_dtypes)
    dot = {"nn": _dot_nn, "nt": _dot_nt, "tn": _dot_tn}[dims]

    def body(*refs):
        a_ref, b_ref = refs[0], refs[1]
        extra_refs = refs[2:2 + n_extra]
        out_refs = refs[2 + n_extra:2 + n_extra + n_out]

        def finish(acc):
            outs = (acc,) if epilogue is None else epilogue(acc, *[r[...] for r in extra_refs])
            for o_ref, o in zip(out_refs, outs):
                o_ref[...] = o.astype(o_ref.dtype)

        part = dot(a_ref[...].astype(BF16), b_ref[...].astype(BF16))
        if nk == 1:
            finish(part)
            return
        acc_ref = refs[-1]
        kk = pl.program_id(2)

        @pl.when(kk == 0)
        def _():
            acc_ref[...] = part

        @pl.when(kk > 0)
        def _():
            acc_ref[...] += part

        @pl.when(kk == nk - 1)
        def _():
            finish(acc_ref[...])

    outs = pl.pallas_call(
        body, name=name, grid=(n // bn, m // bm, nk),
        in_specs=[a_spec, b_spec] + extra_specs,
        out_specs=[pl.BlockSpec((bm, bn), lambda j, i, kk: (i, j)) for _ in range(n_out)],
        out_shape=[jax.ShapeDtypeStruct((m, n), dt) for dt in out_dtypes],
        scratch_shapes=[pltpu.VMEM((bm, bn), F32)] if nk > 1 else [],
        compiler_params=pltpu.CompilerParams(dimension_semantics=("parallel", "parallel", "arbitrary"),
                                             vmem_limit_bytes=VMEM_LIMIT),
    )(a, b, *[arr for arr, _ in extras])
    return outs[0] if n_out == 1 else outs


def _epi_add(acc, res):
    return (res + acc,)


def _epi_relu2(acc):
    r = jnp.maximum(acc, 0.0)
    return acc, r * r


def _epi_relu2_grad(acc, u):
    return (acc * (2.0 * jnp.maximum(u.astype(F32), 0.0)),)


def _rope_slab(t, cos_t, sin_t):
    lane = lax.broadcasted_iota(jnp.int32, t.shape, 1)
    partner = jnp.where(lane < ROPE_LO + ROPE_HALF, pltpu.roll(t, LANES - ROPE_HALF, 1), pltpu.roll(t, ROPE_HALF, 1))
    return t * cos_t + partner * sin_t


def _rope_slab_bwd(d, cos_t, sin_t):
    ds = d * sin_t
    lane = lax.broadcasted_iota(jnp.int32, d.shape, 1)
    partner = jnp.where(lane < ROPE_LO + ROPE_HALF, pltpu.roll(ds, LANES - ROPE_HALF, 1), pltpu.roll(ds, ROPE_HALF, 1))
    in_rope = (lane >= ROPE_LO) & (lane < ROPE_LO + MLA_ROPE)
    return d * cos_t + jnp.where(in_rope, partner, 0.0)


def _epi_rope_heads(acc, cos_t, sin_t):
    slabs = [_rope_slab(acc[:, j * LANES:(j + 1) * LANES], cos_t, sin_t) for j in range(acc.shape[1] // LANES)]
    return (jnp.concatenate(slabs, axis=1) * MLA_SCALE,)


def _row_block(s):
    return min(512, s)


def _rms_fwd(x, g, *, name):
    s, d = x.shape
    bm = _row_block(s)

    def body(x_ref, g_ref, o_ref):
        xv = x_ref[...]
        r = lax.rsqrt(jnp.mean(xv * xv, axis=-1, keepdims=True) + NORM_EPS)
        o_ref[...] = ((xv * r) * g_ref[...]).astype(o_ref.dtype)

    return pl.pallas_call(
        body, name=name, grid=(s // bm,),
        in_specs=[pl.BlockSpec((bm, d), lambda i: (i, 0)), pl.BlockSpec((1, d), lambda i: (0, 0))],
        out_specs=pl.BlockSpec((bm, d), lambda i: (i, 0)),
        out_shape=jax.ShapeDtypeStruct((s, d), BF16),
        compiler_params=pltpu.CompilerParams(dimension_semantics=("parallel",), vmem_limit_bytes=VMEM_LIMIT),
    )(x, g.reshape(1, d))


def _rms_bwd_math(xv, gv, dy):
    r = lax.rsqrt(jnp.mean(xv * xv, axis=-1, keepdims=True) + NORM_EPS)
    xhat = xv * r
    dyg = dy * gv
    mdot = jnp.mean(dyg * xhat, axis=-1, keepdims=True)
    dx = r * (dyg - xhat * mdot)
    dg = jnp.sum(dy * xhat, axis=0, keepdims=True)
    return dx, dg


def _rms_bwd(x, g, dy, dres, *, name):
    s, d = x.shape
    bm = _row_block(s)
    has_res = dres is not None

    def body(*refs):
        x_ref, g_ref, dy_ref = refs[:3]
        dres_ref = refs[3] if has_res else None
        dx_ref, dxb_ref, dg_ref = refs[-3:]
        dx, dg = _rms_bwd_math(x_ref[...], g_ref[...], dy_ref[...].astype(F32))
        if has_res:
            dx = dx + dres_ref[...]
        dx_ref[...] = dx
        dxb_ref[...] = dx.astype(BF16)

        @pl.when(pl.program_id(0) == 0)
        def _():
            dg_ref[...] = jnp.zeros_like(dg_ref)

        dg_ref[...] += dg

    row = pl.BlockSpec((bm, d), lambda i: (i, 0))
    vec = pl.BlockSpec((1, d), lambda i: (0, 0))
    ins = [x, g.reshape(1, d), dy] + ([dres] if has_res else [])
    return pl.pallas_call(
        body, name=name, grid=(s // bm,),
        in_specs=[row, vec, row] + ([row] if has_res else []),
        out_specs=[row, row, vec],
        out_shape=[jax.ShapeDtypeStruct((s, d), F32), jax.ShapeDtypeStruct((s, d), BF16),
                   jax.ShapeDtypeStruct((1, d), F32)],
        compiler_params=pltpu.CompilerParams(dimension_semantics=("arbitrary",), vmem_limit_bytes=VMEM_LIMIT),
    )(*ins)


def _loss_bwd(x, g, target, *, name):
    s, d = x.shape
    bm = _row_block(s)

    def body(x_ref, g_ref, t_ref, loss_ref, dx_ref, dxb_ref, dg_ref):
        xv, gv = x_ref[...], g_ref[...]
        r = lax.rsqrt(jnp.mean(xv * xv, axis=-1, keepdims=True) + NORM_EPS)
        err = (xv * r) * gv - t_ref[...]
        dx, dg = _rms_bwd_math(xv, gv, err * (1.0 / d))
        dx_ref[...] = dx
        dxb_ref[...] = dx.astype(BF16)

        @pl.when(pl.program_id(0) == 0)
        def _():
            dg_ref[...] = jnp.zeros_like(dg_ref)
            loss_ref[...] = jnp.zeros_like(loss_ref)

        dg_ref[...] += dg
        loss_ref[...] += jnp.sum(jnp.mean(err * err, axis=-1, keepdims=True), axis=0, keepdims=True) * 0.5

    row = pl.BlockSpec((bm, d), lambda i: (i, 0))
    vec = pl.BlockSpec((1, d), lambda i: (0, 0))
    return pl.pallas_call(
        body, name=name, grid=(s // bm,),
        in_specs=[row, vec, row],
        out_specs=[pl.BlockSpec((8, LANES), lambda i: (0, 0)), row, row, vec],
        out_shape=[jax.ShapeDtypeStruct((8, LANES), F32), jax.ShapeDtypeStruct((s, d), F32),
                   jax.ShapeDtypeStruct((s, d), BF16), jax.ShapeDtypeStruct((1, d), F32)],
        compiler_params=pltpu.CompilerParams(dimension_semantics=("arbitrary",), vmem_limit_bytes=VMEM_LIMIT),
    )(x, g.reshape(1, d), target)


def _kv_prep(down, g, cos_t, sin_t, *, name):
    s, w = down.shape
    bm = _row_block(s)

    def body(d_ref, g_ref, c_ref, s_ref, o_ref):
        lat = d_ref[:, :MLA_KV_RANK]
        r = lax.rsqrt(jnp.mean(lat * lat, axis=-1, keepdims=True) + NORM_EPS)
        o_ref[:, :MLA_KV_RANK] = ((lat * r) * g_ref[...]).astype(BF16)
        o_ref[:, MLA_KV_RANK:] = _rope_slab(d_ref[:, MLA_KV_RANK:], c_ref[...], s_ref[...]).astype(BF16)

    row = pl.BlockSpec((bm, w), lambda i: (i, 0))
    tab = pl.BlockSpec((bm, LANES), lambda i: (i, 0))
    return pl.pallas_call(
        body, name=name, grid=(s // bm,),
        in_specs=[row, pl.BlockSpec((1, MLA_KV_RANK), lambda i: (0, 0)), tab, tab],
        out_specs=row, out_shape=jax.ShapeDtypeStruct((s, w), BF16),
        compiler_params=pltpu.CompilerParams(dimension_semantics=("parallel",), vmem_limit_bytes=VMEM_LIMIT),
    )(down, g.reshape(1, MLA_KV_RANK), cos_t, sin_t)


def _kv_prep_bwd(down, g, cos_t, sin_t, dcat, *, name):
    s, w = down.shape
    bm = _row_block(s)

    def body(d_ref, g_ref, c_ref, s_ref, dc_ref, o_ref, dg_ref):
        dlat, dg = _rms_bwd_math(d_ref[:, :MLA_KV_RANK], g_ref[...], dc_ref[:, :MLA_KV_RANK])
        o_ref[:, :MLA_KV_RANK] = dlat.astype(BF16)
        o_ref[:, MLA_KV_RANK:] = _rope_slab_bwd(dc_ref[:, MLA_KV_RANK:], c_ref[...], s_ref[...]).astype(BF16)

        @pl.when(pl.program_id(0) == 0)
        def _():
            dg_ref[...] = jnp.zeros_like(dg_ref)

        dg_ref[...] += dg

    row = pl.BlockSpec((bm, w), lambda i: (i, 0))
    tab = pl.BlockSpec((bm, LANES), lambda i: (i, 0))
    vec = pl.BlockSpec((1, MLA_KV_RANK), lambda i: (0, 0))
    return pl.pallas_call(
        body, name=name, grid=(s // bm,),
        in_specs=[row, vec, tab, tab, row],
        out_specs=[row, vec],
        out_shape=[jax.ShapeDtypeStruct((s, w), BF16), jax.ShapeDtypeStruct((1, MLA_KV_RANK), F32)],
        compiler_params=pltpu.CompilerParams(dimension_semantics=("arbitrary",), vmem_limit_bytes=VMEM_LIMIT),
    )(down, g.reshape(1, MLA_KV_RANK), cos_t, sin_t, dcat)


def _split_bf16(v):
    hi = v.astype(BF16)
    lo = (v - hi.astype(F32)).astype(BF16)
    return hi, lo


def _suffix_matrices(n):
    row = lax.broadcasted_iota(jnp.int32, (n, n), 0)
    col = lax.broadcasted_iota(jnp.int32, (n, n), 1)
    incl = (row >= col).astype(BF16)
    return (row > col).astype(BF16), jnp.concatenate([incl, incl], axis=0)


def _suffix_sum(v, matrix):
    hi, lo = _split_bf16(v)
    return _dot_nn(jnp.concatenate([hi, lo], axis=1), matrix)


def _block_positions(qi, kb, bq, bk, r0):
    row = qi * bq + r0 + lax.broadcasted_iota(jnp.int32, (bq - r0, bk), 0)
    col = kb * bk + lax.broadcasted_iota(jnp.int32, (bq - r0, bk), 1)
    return row, col


def _att_blocks(s):
    bq, bk = min(ATT_Q_BLOCK, s), min(ATT_K_BLOCK, s)
    return bq, bk, s // bq, bq // bk


def _sweep(qi, ratio, bk, step, unroll=2, alive=None):
    for d in range(ratio):
        step((qi + 1) * ratio - 1 - d, True, (ratio - 1 - d) * bk)
    unroll = unroll if ratio % unroll == 0 else 1
    trips = qi * (ratio // unroll)

    def trip(i):
        for u in range(unroll):
            step(qi * ratio - 1 - (i * unroll + u), False, 0)

    if alive is None:
        lax.fori_loop(0, trips, lambda i, carry: (trip(i), carry)[1], 0)
    else:
        lax.while_loop(lambda i: jnp.logical_and(i < trips, alive()), lambda i: (trip(i), i + 1)[1], 0)


def _stick_left(c_ref):
    return jnp.max(c_ref[...]) > SB_DEAD_LOG


def _sb_logs(q, k):
    z = _dot_nt(q, k)
    lb = jnp.minimum(z, 0.0) - jnp.log(1.0 + jnp.exp(-jnp.abs(z)))
    return lb, lb - z


def _sb_fwd(qkv, heads, *, name):
    s = qkv.shape[0]
    bq, bk, nq, ratio = _att_blocks(s)

    def body(q_ref, k_ref, v_ref, o_ref, acc_ref, c_ref):
        qi = pl.program_id(1)
        q = q_ref[...] * SB_SCALE
        m_strict, _ = _suffix_matrices(bk)
        acc_ref[...] = jnp.zeros_like(acc_ref)
        c_ref[...] = jnp.zeros_like(c_ref)

        def step(kb, masked, r0):
            rows = pl.ds(pl.multiple_of(kb * bk, bk), bk)
            mine = pl.ds(r0, bq - r0)
            k, v = k_ref[rows, :], v_ref[rows, :]
            lb, lk = _sb_logs(q[r0:], k)
            if masked:
                row, col = _block_positions(qi, kb, bq, bk, r0)
                causal = col < row
                lk = jnp.where(causal, lk, 0.0)
            c = c_ref[mine, :]
            w = jnp.exp(lb + _dot_nn(lk.astype(BF16), m_strict) + jnp.tile(c, (1, bk // LANES)))
            if masked:
                w = jnp.where(causal, w, 0.0)
            acc_ref[mine, :] += _dot_nn(w.astype(BF16), v)
            c_ref[mine, :] = c + jnp.sum(lk, axis=-1, keepdims=True)

        _sweep(qi, ratio, bk, step, alive=lambda: _stick_left(c_ref))
        o_ref[...] = acc_ref[...].astype(o_ref.dtype)

    return pl.pallas_call(
        body, name=name, grid=(heads, nq),
        in_specs=[pl.BlockSpec((bq, LANES), lambda h, i: (i, h)),
                  pl.BlockSpec((s, LANES), lambda h, i: (0, heads + h)),
                  pl.BlockSpec((s, LANES), lambda h, i: (0, 2 * heads + h))],
        out_specs=pl.BlockSpec((bq, LANES), lambda h, i: (i, h)),
        out_shape=jax.ShapeDtypeStruct((s, heads * LANES), F32),
        scratch_shapes=[pltpu.VMEM((bq, LANES), F32), pltpu.VMEM((bq, LANES), F32)],
        compiler_params=pltpu.CompilerParams(dimension_semantics=("parallel", "arbitrary"),
                                             vmem_limit_bytes=VMEM_LIMIT),
    )(qkv, qkv, qkv)


def _sb_bwd(qkv, o, do, heads, *, name):
    s = qkv.shape[0]
    bq, bk, nq, ratio = _att_blocks(s)

    def body(q_ref, k_ref, v_ref, o_ref, do_ref, dq_ref, dk_ref, dv_ref, dq_acc, dk_acc, dv_acc, c_ref, e_ref):
        qi = pl.program_id(1)

        @pl.when(qi == 0)
        def _():
            dk_acc[...] = jnp.zeros_like(dk_acc)
            dv_acc[...] = jnp.zeros_like(dv_acc)

        q = q_ref[...] * SB_SCALE
        do = do_ref[...]
        total = jnp.sum(do.astype(F32) * o_ref[...].astype(F32), axis=-1, keepdims=True)
        m_strict, m_incl = _suffix_matrices(bk)
        dq_acc[...] = jnp.zeros_like(dq_acc)
        c_ref[...] = jnp.zeros_like(c_ref)
        e_ref[...] = jnp.broadcast_to(total, e_ref.shape)
        reps = (1, bk // LANES)

        def step(kb, masked, r0):
            rows = pl.ds(pl.multiple_of(kb * bk, bk), bk)
            mine = pl.ds(r0, bq - r0)
            k, v = k_ref[rows, :], v_ref[rows, :]
            qs, dos = q[r0:], do[r0:]
            lb, lk_all = _sb_logs(qs, k)
            lk = lk_all
            if masked:
                row, col = _block_positions(qi, kb, bq, bk, r0)
                causal = col < row
                lk = jnp.where(causal, lk_all, 0.0)
            c = c_ref[mine, :]
            w = jnp.exp(lb + _dot_nn(lk.astype(BF16), m_strict) + jnp.tile(c, reps))
            if masked:
                w = jnp.where(causal, w, 0.0)
            wb = w.astype(BF16)
            g = wb.astype(F32) * _dot_nt(dos, v)
            e = e_ref[mine, :]
            g_left = jnp.tile(e, reps) - _suffix_sum(g, m_incl)
            da = g * jnp.exp(lk_all) - jnp.exp(lb) * g_left
            if masked:
                da = jnp.where(causal, da, 0.0)
            dab = da.astype(BF16)
            dq_acc[mine, :] += _dot_nn(dab, k)
            dk_acc[rows, :] += _dot_tn(dab, qs)
            dv_acc[rows, :] += _dot_tn(wb, dos)
            e_ref[mine, :] = e - jnp.sum(g, axis=-1, keepdims=True)
            c_ref[mine, :] = c + jnp.sum(lk, axis=-1, keepdims=True)

        _sweep(qi, ratio, bk, step, alive=lambda: _stick_left(c_ref))
        dq_ref[...] = (dq_acc[...] * SB_SCALE).astype(dq_ref.dtype)

        @pl.when(qi == nq - 1)
        def _():
            dk_ref[...] = dk_acc[...].astype(dk_ref.dtype)
            dv_ref[...] = dv_acc[...].astype(dv_ref.dtype)

    blk = pl.BlockSpec((bq, LANES), lambda h, i: (i, h))
    full = pl.BlockSpec((s, LANES), lambda h, i: (0, h))
    shape = jax.ShapeDtypeStruct((s, heads * LANES), BF16)
    return pl.pallas_call(
        body, name=name, grid=(heads, nq),
        in_specs=[blk,
                  pl.BlockSpec((s, LANES), lambda h, i: (0, heads + h)),
                  pl.BlockSpec((s, LANES), lambda h, i: (0, 2 * heads + h)),
                  blk, blk],
        out_specs=[blk, full, full],
        out_shape=[shape, shape, shape],
        scratch_shapes=[pltpu.VMEM((bq, LANES), F32), pltpu.VMEM((s, LANES), F32), pltpu.VMEM((s, LANES), F32),
                        pltpu.VMEM((bq, LANES), F32), pltpu.VMEM((bq, LANES), F32)],
        compiler_params=pltpu.CompilerParams(dimension_semantics=("arbitrary", "arbitrary"),
                                             vmem_limit_bytes=VMEM_LIMIT),
    )(qkv, qkv, qkv, o, do)


def _chunk_allowed(qi, kb, bq, bk, r0):
    row, col = _block_positions(qi, kb, bq, bk, r0)
    return (col // CHUNK) <= (row // CHUNK)


def _mla_fwd(q, kv, heads, *, name):
    s = q.shape[0]
    bq, bk, nq, ratio = _att_blocks(s)
    reps = (1, bk // LANES)

    def body(q_ref, k_ref, v_ref, o_ref, lse_ref, acc_ref, m_ref, l_ref):
        qi = pl.program_id(1)
        qv = q_ref[...]
        acc_ref[...] = jnp.zeros_like(acc_ref)
        m_ref[...] = jnp.full_like(m_ref, NEG_BIG)
        l_ref[...] = jnp.zeros_like(l_ref)

        def step(kb, masked, r0):
            rows = pl.ds(pl.multiple_of(kb * bk, bk), bk)
            mine = pl.ds(r0, bq - r0)
            k, v = k_ref[rows, :], v_ref[rows, :]
            sc = _dot_nt(qv[r0:], k)
            if masked:
                allowed = _chunk_allowed(qi, kb, bq, bk, r0)
                sc = jnp.where(allowed, sc, NEG_BIG)
            m_old = m_ref[mine, :]
            m_new = jnp.maximum(m_old, jnp.max(sc, axis=-1, keepdims=True))
            p = jnp.exp(sc - jnp.tile(m_new, reps))
            alpha = jnp.exp(m_old - m_new)
            l_ref[mine, :] = alpha * l_ref[mine, :] + jnp.sum(p, axis=-1, keepdims=True)
            acc_ref[mine, :] = alpha * acc_ref[mine, :] + _dot_nn(p.astype(BF16), v)
            m_ref[mine, :] = m_new

        _sweep(qi, ratio, bk, step, unroll=MLA_FWD_UNROLL)
        o_ref[...] = (acc_ref[...] / l_ref[...]).astype(o_ref.dtype)
        lse_ref[...] = m_ref[...] + jnp.log(l_ref[...])

    blk = pl.BlockSpec((bq, LANES), lambda h, i: (i, h))
    return pl.pallas_call(
        body, name=name, grid=(heads, nq),
        in_specs=[blk,
                  pl.BlockSpec((s, LANES), lambda h, i: (0, h)),
                  pl.BlockSpec((s, LANES), lambda h, i: (0, heads + h))],
        out_specs=[blk, blk],
        out_shape=[jax.ShapeDtypeStruct((s, heads * LANES), BF16), jax.ShapeDtypeStruct((s, heads * LANES), F32)],
        scratch_shapes=[pltpu.VMEM((bq, LANES), F32), pltpu.VMEM((bq, LANES), F32), pltpu.VMEM((bq, LANES), F32)],
        compiler_params=pltpu.CompilerParams(dimension_semantics=("parallel", "arbitrary"),
                                             vmem_limit_bytes=VMEM_LIMIT),
    )(q, kv, kv)


def _mla_bwd(q, kv, o, do, lse, cos_t, sin_t, dkv_init, heads, *, name):
    s = q.shape[0]
    bq, bk, nq, ratio = _att_blocks(s)
    reps = (1, bk // LANES)
    has_init = dkv_init is not None

    def body(*refs):
        q_ref, k_ref, v_ref, o_ref, do_ref, lse_ref, c_ref, s_ref = refs[:8]
        ki_ref, vi_ref = (refs[8], refs[9]) if has_init else (None, None)
        dq_ref, dk_ref, dv_ref, dq_acc, dk_acc, dv_acc = refs[-6:]
        qi = pl.program_id(1)

        @pl.when(qi == 0)
        def _():
            if has_init:
                dk_acc[...] = ki_ref[...].astype(F32)
                dv_acc[...] = vi_ref[...].astype(F32)
            else:
                dk_acc[...] = jnp.zeros_like(dk_acc)
                dv_acc[...] = jnp.zeros_like(dv_acc)

        qv = q_ref[...]
        do = do_ref[...]
        delta = jnp.sum(do.astype(F32) * o_ref[...].astype(F32), axis=-1, keepdims=True)
        lse_wide = jnp.tile(lse_ref[...], reps)
        dq_acc[...] = jnp.zeros_like(dq_acc)

        def step(kb, masked, r0):
            rows = pl.ds(pl.multiple_of(kb * bk, bk), bk)
            k, v = k_ref[rows, :], v_ref[rows, :]
            qs, dos = qv[r0:], do[r0:]
            p = jnp.exp(_dot_nt(qs, k) - lse_wide[r0:])
            if masked:
                p = jnp.where(_chunk_allowed(qi, kb, bq, bk, r0), p, 0.0)
            ds = (p * (_dot_nt(dos, v) - delta[r0:])).astype(BF16)
            dq_acc[pl.ds(r0, bq - r0), :] += _dot_nn(ds, k)
            dk_acc[rows, :] += _dot_tn(ds, qs)
            dv_acc[rows, :] += _dot_tn(p.astype(BF16), dos)

        _sweep(qi, ratio, bk, step)
        dq_ref[...] = _rope_slab_bwd(dq_acc[...] * MLA_SCALE, c_ref[...], s_ref[...]).astype(dq_ref.dtype)

        @pl.when(qi == nq - 1)
        def _():
            dk_ref[...] = dk_acc[...].astype(dk_ref.dtype)
            dv_ref[...] = dv_acc[...].astype(dv_ref.dtype)

    blk = pl.BlockSpec((bq, LANES), lambda h, i: (i, h))
    tab = pl.BlockSpec((bq, LANES), lambda h, i: (i, 0))
    k_full = pl.BlockSpec((s, LANES), lambda h, i: (0, h))
    v_full = pl.BlockSpec((s, LANES), lambda h, i: (0, heads + h))
    shape = jax.ShapeDtypeStruct((s, heads * LANES), BF16)
    ins = [q, kv, kv, o, do, lse, cos_t, sin_t] + ([dkv_init, dkv_init] if has_init else [])
    dq, dk, dv = pl.pallas_call(
        body, name=name, grid=(heads, nq),
        in_specs=[blk, k_full, v_full, blk, blk, blk, tab, tab] + ([k_full, v_full] if has_init else []),
        out_specs=[blk, k_full, k_full],
        out_shape=[shape, shape, shape],
        scratch_shapes=[pltpu.VMEM((bq, LANES), F32), pltpu.VMEM((s, LANES), F32), pltpu.VMEM((s, LANES), F32)],
        compiler_params=pltpu.CompilerParams(dimension_semantics=("arbitrary", "arbitrary"),
                                             vmem_limit_bytes=VMEM_LIMIT),
    )(*ins)
    return dq, jnp.concatenate([dk, dv], axis=1)


def _pad_last(a, width):
    return jnp.pad(a, [(0, 0)] * (a.ndim - 1) + [(0, width - a.shape[-1])])


def _pad_qkv(w, heads):
    d = w.shape[0]
    return _pad_last(w.reshape(d, 3 * heads, SB_HEAD_DIM), LANES).reshape(d, 3 * heads * LANES)


def _unpad_qkv(g, heads):
    d = g.shape[0]
    return g.reshape(d, 3 * heads, LANES)[:, :, :SB_HEAD_DIM].reshape(d, 3 * heads * SB_HEAD_DIM)


def _pad_o(w, heads):
    d = w.shape[1]
    w = w.reshape(heads, SB_HEAD_DIM, d)
    return jnp.pad(w, [(0, 0), (0, LANES - SB_HEAD_DIM), (0, 0)]).reshape(heads * LANES, d)


def _unpad_o(g, heads):
    d = g.shape[1]
    return g.reshape(heads, LANES, d)[:, :SB_HEAD_DIM, :].reshape(heads * SB_HEAD_DIM, d)


def _pad_uq(w, heads):
    r = w.shape[0]
    return _pad_last(w.reshape(r, heads, MLA_NOPE + MLA_ROPE), LANES).reshape(r, heads * LANES)


def _unpad_uq(g, heads):
    r = g.shape[0]
    return g.reshape(r, heads, LANES)[:, :, :MLA_NOPE + MLA_ROPE].reshape(r, heads * (MLA_NOPE + MLA_ROPE))


def _pad_dkv(w):
    d = w.shape[0]
    rope = jnp.zeros((d, LANES), w.dtype).at[:, ROPE_LO:ROPE_LO + MLA_ROPE].set(w[:, MLA_KV_RANK:])
    return jnp.concatenate([w[:, :MLA_KV_RANK], rope], axis=1)


def _unpad_dkv(g):
    return jnp.concatenate([g[:, :MLA_KV_RANK], g[:, MLA_KV_RANK + ROPE_LO:MLA_KV_RANK + ROPE_LO + MLA_ROPE]], axis=1)


def _pad_ukv(w, heads):
    w = w.reshape(MLA_KV_RANK, heads, 2, MLA_NOPE)
    k_part = _pad_last(w[:, :, 0, :], LANES).reshape(MLA_KV_RANK, heads * LANES)
    v_part = _pad_last(w[:, :, 1, :], LANES).reshape(MLA_KV_RANK, heads * LANES)
    lane = jnp.arange(LANES)
    place = ((lane[:, None] == lane[None, :]) & (lane[:, None] >= ROPE_LO) & (lane[:, None] < ROPE_LO + MLA_ROPE))
    place = jnp.tile(place.astype(w.dtype), (1, heads))
    top = jnp.concatenate([k_part, v_part], axis=1)
    bottom = jnp.concatenate([place, jnp.zeros_like(place)], axis=1)
    return jnp.concatenate([top, bottom], axis=0)


def _unpad_ukv(g, heads):
    g = g[:MLA_KV_RANK]
    k_part = g[:, :heads * LANES].reshape(MLA_KV_RANK, heads, LANES)[:, :, :MLA_NOPE]
    v_part = g[:, heads * LANES:].reshape(MLA_KV_RANK, heads, LANES)[:, :, :MLA_V]
    return jnp.stack([k_part, v_part], axis=2).reshape(MLA_KV_RANK, heads * (MLA_NOPE + MLA_V))


def _rope_tables(positions):
    inv_freq = ROPE_THETA ** (-jnp.arange(0, MLA_ROPE, 2, dtype=F32) / MLA_ROPE)
    ang = positions.astype(F32)[:, None] * inv_freq
    cos, sin = jnp.cos(ang), jnp.sin(ang)
    s = positions.shape[0]
    cos_t = jnp.ones((s, LANES), F32).at[:, ROPE_LO:ROPE_LO + MLA_ROPE].set(jnp.concatenate([cos, cos], axis=1))
    sin_t = jnp.zeros((s, LANES), F32).at[:, ROPE_LO:ROPE_LO + MLA_ROPE].set(jnp.concatenate([-sin, sin], axis=1))
    return cos_t, sin_t


def _local_step(x, positions, target, w, norms):
    s, d = x.shape
    heads = d // SB_HEAD_DIM
    n_a = w["sb_w_qkv"].shape[0]
    n_b = w["mla_w_dq"].shape[0]
    depth = n_a + n_b
    cos_t, sin_t = _rope_tables(positions)

    wqkv = [_pad_qkv(w["sb_w_qkv"][l], heads) for l in range(n_a)]
    wo_a = [_pad_o(w["sb_w_o"][l], heads) for l in range(n_a)]
    wdkv = _pad_dkv(w["mla_w_dkv"])
    wkv = _pad_ukv(w["mla_w_ukv"], heads)
    wdq = [w["mla_w_dq"][j] for j in range(n_b)]
    wuq = [_pad_uq(w["mla_w_uq"][j], heads) for j in range(n_b)]
    wo_b = [_pad_o(w["mla_w_o"][j], heads) for j in range(n_b)]
    w1 = [w["mlp_w1"][l] for l in range(depth)]
    w2 = [w["mlp_w2"][l] for l in range(depth)]

    saved = []
    kv_saved = None
    kv = None
    for l in range(depth):
        t = f"l{l}"
        sv = {"x_in": x}
        h = _rms_fwd(x, norms["attn_norm"][l], name=f"{t}_attn_norm")
        sv["h"] = h
        if l < n_a:
            qkv = _mm(h, wqkv[l], name=f"{t}_qkv")
            o = _sb_fwd(qkv, heads, name=f"{t}_sb_fwd")
            sv["qkv"], sv["o"] = qkv, o
            x = _mm(o, wo_a[l], name=f"{t}_attn_out", epilogue=_epi_add, extras=[(x, "tile")], out_dtypes=(F32,))
        else:
            j = l - n_a
            if j == 0:
                hk = _rms_fwd(x, norms["kv_norm"], name="kv_norm")
                down = _mm(hk, wdkv, name="kv_down", out_dtypes=(F32,))
                cat = _kv_prep(down, norms["mla_kv_lat_norm"], cos_t, sin_t, name="kv_prep")
                kv = _mm(cat, wkv, name="kv_up")
                kv_saved = {"x_in": x, "hk": hk, "down": down, "cat": cat}
            cq0 = _mm(h, wdq[j], name=f"{t}_q_down", out_dtypes=(F32,))
            cq = _rms_fwd(cq0, norms["mla_q_lat_norm"][j], name=f"{t}_q_lat_norm")
            q = _mm(cq, wuq[j], name=f"{t}_q_up", epilogue=_epi_rope_heads, extras=[(cos_t, "row"), (sin_t, "row")])
            o, lse = _mla_fwd(q, kv, heads, name=f"{t}_mla_fwd")
            sv.update(cq0=cq0, cq=cq, q=q, o=o, lse=lse)
            x = _mm(o, wo_b[j], name=f"{t}_attn_out", epilogue=_epi_add, extras=[(x, "tile")], out_dtypes=(F32,))
        sv["x_mid"] = x
        h2 = _rms_fwd(x, norms["mlp_norm"][l], name=f"{t}_mlp_norm")
        u, a = _mm(h2, w1[l], name=f"{t}_mlp_up", epilogue=_epi_relu2, out_dtypes=(BF16, BF16))
        sv.update(h2=h2, u=u, a=a)
        x = _mm(a, w2[l], name=f"{t}_mlp_down", epilogue=_epi_add, extras=[(x, "tile")], out_dtypes=(F32,))
        saved.append(sv)

    loss_slab, dx, dxb, dg_final = _loss_bwd(x, norms["final_norm"], target, name="loss")
    loss = loss_slab[0, 0]

    g_attn_norm, g_mlp_norm = [None] * depth, [None] * depth
    g_qkv, g_o_a = [None] * n_a, [None] * n_a
    g_dq, g_uq, g_o_b, g_qlat = [None] * n_b, [None] * n_b, [None] * n_b, [None] * n_b
    g_w1, g_w2 = [None] * depth, [None] * depth
    dkv = None
    g_kv_norm = g_kv_lat = g_dkv = g_ukv = None

    for l in reversed(range(depth)):
        t = f"l{l}"
        sv = saved[l]
        du = _mm(dxb, w2[l], name=f"{t}_mlp_down_dx", dims="nt", epilogue=_epi_relu2_grad, extras=[(sv["u"], "tile")])
        g_w2[l] = _mm(sv["a"], dxb, name=f"{t}_mlp_down_dw", dims="tn", out_dtypes=(F32,))
        g_w1[l] = _mm(sv["h2"], du, name=f"{t}_mlp_up_dw", dims="tn", out_dtypes=(F32,))
        dh2 = _mm(du, w1[l], name=f"{t}_mlp_up_dx", dims="nt", out_dtypes=(F32,))
        dx, dxb, g_mlp_norm[l] = _rms_bwd(sv["x_mid"], norms["mlp_norm"][l], dh2, dx, name=f"{t}_mlp_norm_bwd")
        if l < n_a:
            do = _mm(dxb, wo_a[l], name=f"{t}_attn_out_dx", dims="nt")
            g_o_a[l] = _unpad_o(_mm(sv["o"], dxb, name=f"{t}_attn_out_dw", dims="tn", out_dtypes=(F32,)), heads)
            dq, dk, dv = _sb_bwd(sv["qkv"], sv["o"], do, heads, name=f"{t}_sb_bwd")
            dqkv = jnp.concatenate([dq, dk, dv], axis=1)
            g_qkv[l] = _unpad_qkv(_mm(sv["h"], dqkv, name=f"{t}_qkv_dw", dims="tn", out_dtypes=(F32,)), heads)
            dh = _mm(dqkv, wqkv[l], name=f"{t}_qkv_dx", dims="nt", out_dtypes=(F32,))
        else:
            j = l - n_a
            do = _mm(dxb, wo_b[j], name=f"{t}_attn_out_dx", dims="nt")
            g_o_b[j] = _unpad_o(_mm(sv["o"], dxb, name=f"{t}_attn_out_dw", dims="tn", out_dtypes=(F32,)), heads)
            dq, dkv = _mla_bwd(sv["q"], kv, sv["o"], do, sv["lse"], cos_t, sin_t, dkv, heads, name=f"{t}_mla_bwd")
            g_uq[j] = _unpad_uq(_mm(sv["cq"], dq, name=f"{t}_q_up_dw", dims="tn", out_dtypes=(F32,)), heads)
            dcq = _mm(dq, wuq[j], name=f"{t}_q_up_dx", dims="nt", out_dtypes=(F32,))
            _, dcq0, g_qlat[j] = _rms_bwd(sv["cq0"], norms["mla_q_lat_norm"][j], dcq, None, name=f"{t}_q_lat_norm_bwd")
            g_dq[j] = _mm(sv["h"], dcq0, name=f"{t}_q_down_dw", dims="tn", out_dtypes=(F32,))
            dh = _mm(dcq0, wdq[j], name=f"{t}_q_down_dx", dims="nt", out_dtypes=(F32,))
        dx, dxb, g_attn_norm[l] = _rms_bwd(sv["x_in"], norms["attn_norm"][l], dh, dx, name=f"{t}_attn_norm_bwd")
        if l == n_a:
            ks = kv_saved
            dcat = _mm(dkv, wkv, name="kv_up_dx", dims="nt", out_dtypes=(F32,))
            g_ukv = _unpad_ukv(_mm(ks["cat"], dkv, name="kv_up_dw", dims="tn", out_dtypes=(F32,)), heads)
            ddown, g_kv_lat = _kv_prep_bwd(ks["down"], norms["mla_kv_lat_norm"], cos_t, sin_t, dcat, name="kv_prep_bwd")
            g_dkv = _unpad_dkv(_mm(ks["hk"], ddown, name="kv_down_dw", dims="tn", out_dtypes=(F32,)))
            dhk = _mm(ddown, wdkv, name="kv_down_dx", dims="nt", out_dtypes=(F32,))
            dx, dxb, g_kv_norm = _rms_bwd(ks["x_in"], norms["kv_norm"], dhk, dx, name="kv_norm_bwd")

    grads = {
        "attn_norm": jnp.concatenate(g_attn_norm, axis=0), "mlp_norm": jnp.concatenate(g_mlp_norm, axis=0),
        "sb_w_qkv": jnp.stack(g_qkv), "sb_w_o": jnp.stack(g_o_a),
        "kv_norm": g_kv_norm[0], "mla_w_dkv": g_dkv, "mla_kv_lat_norm": g_kv_lat[0], "mla_w_ukv": g_ukv,
        "mla_w_dq": jnp.stack(g_dq), "mla_q_lat_norm": jnp.concatenate(g_qlat, axis=0),
        "mla_w_uq": jnp.stack(g_uq), "mla_w_o": jnp.stack(g_o_b),
        "mlp_w1": jnp.stack(g_w1), "mlp_w2": jnp.stack(g_w2), "final_norm": dg_final[0],
    }
    return loss, dx, grads


def _flat_rows(n_elems):
    per_block = FLAT_COLS * FLAT_ROW_BLOCK * 2
    return -(-n_elems // per_block) * FLAT_ROW_BLOCK * 2


def _pack(arrays, dtype):
    flat = jnp.concatenate([a.reshape(-1).astype(dtype) for a in arrays])
    rows = _flat_rows(flat.shape[0])
    flat = jnp.pad(flat, (0, rows * FLAT_COLS - flat.shape[0]))
    return flat.reshape(rows, FLAT_COLS)


def _unpack(flat, shapes):
    flat = flat.reshape(-1)
    out, off = [], 0
    for shp in shapes:
        n = 1
        for v in shp:
            n *= v
        out.append(flat[off:off + n].reshape(shp))
        off += n
    return out


def _pack_small(arrays):
    rows = []
    for a in arrays:
        a = a.reshape(-1, a.shape[-1]) if a.shape[-1] == FLAT_COLS else a.reshape(1, -1)
        rows.append(_pad_last(a, FLAT_COLS))
    flat = jnp.concatenate(rows, axis=0)
    return jnp.pad(flat, [(0, -flat.shape[0] % 8), (0, 0)])


def _unpack_small(flat, shapes):
    out, row = [], 0
    for shp in shapes:
        if shp[-1] == FLAT_COLS:
            n = 1
            for v in shp[:-1]:
                n *= v
            out.append(flat[row:row + n].reshape(shp))
            row += n
        else:
            n = 1
            for v in shp:
                n *= v
            out.append(flat[row, :n].reshape(shp))
            row += 1
    return out


def _other_chips(x, y):
    return [(1 - x, y), (x, 1 - y), (1 - x, 1 - y)]


def _all_gather_chips(flat, *, name):
    rows, cols = flat.shape

    def body(x_ref, out_ref, send_sems, recv_sems, pass_send_sems, pass_recv_sems):
        x, y, c = lax.axis_index("x"), lax.axis_index("y"), lax.axis_index("c")
        me = 2 * x + y
        my_rows, sib_rows = _half_rows(rows)
        chips = _other_chips(x, y)
        sends = []
        for k, (px, py) in enumerate(chips):
            cp = pltpu.make_async_remote_copy(src_ref=x_ref.at[my_rows, :], dst_ref=out_ref.at[me, my_rows, :],
                                              send_sem=send_sems.at[k], recv_sem=recv_sems.at[k],
                                              device_id=(px, py, c), device_id_type=MESH)
            cp.start()
            sends.append(cp)
        for k, (px, py) in enumerate(chips):
            landed = out_ref.at[2 * px + py, my_rows, :]
            pltpu.make_async_remote_copy(src_ref=landed, dst_ref=landed, send_sem=send_sems.at[k],
                                         recv_sem=recv_sems.at[k], device_id=(px, py, c),
                                         device_id_type=MESH).wait_recv()
            cp = pltpu.make_async_remote_copy(src_ref=landed, dst_ref=landed, send_sem=pass_send_sems.at[k],
                                              recv_sem=pass_recv_sems.at[k], device_id=_sibling(),
                                              device_id_type=MESH)
            cp.start()
            sends.append(cp)
        for k, (px, py) in enumerate(chips):
            passed = out_ref.at[2 * px + py, sib_rows, :]
            pltpu.make_async_remote_copy(src_ref=passed, dst_ref=passed, send_sem=pass_send_sems.at[k],
                                         recv_sem=pass_recv_sems.at[k], device_id=_sibling(),
                                         device_id_type=MESH).wait_recv()
        for cp in sends:
            cp.wait_send()

    out = pl.pallas_call(
        body, name=name,
        in_specs=[pl.BlockSpec(memory_space=pltpu.HBM)],
        out_specs=pl.BlockSpec(memory_space=pltpu.HBM),
        out_shape=jax.ShapeDtypeStruct((N_CHIPS, rows, cols), flat.dtype),
        scratch_shapes=[pltpu.SemaphoreType.DMA((3,)), pltpu.SemaphoreType.DMA((3,)), pltpu.SemaphoreType.DMA((3,)),
                        pltpu.SemaphoreType.DMA((3,))],
        compiler_params=pltpu.CompilerParams(has_side_effects=True),
    )(flat)
    return lax.dynamic_update_index_in_dim(out, flat, _my_chip(), 0)


def _exchange_chips(parts, *, name):
    def body(g_ref, out_ref, send_sems, recv_sems):
        x, y, c = lax.axis_index("x"), lax.axis_index("y"), lax.axis_index("c")
        me = 2 * x + y
        sends = []
        for k, (px, py) in enumerate(_other_chips(x, y)):
            cp = pltpu.make_async_remote_copy(src_ref=g_ref.at[2 * px + py], dst_ref=out_ref.at[me],
                                              send_sem=send_sems.at[k], recv_sem=recv_sems.at[k],
                                              device_id=(px, py, c), device_id_type=MESH)
            cp.start()
            sends.append(cp)
        for k, (px, py) in enumerate(_other_chips(x, y)):
            pltpu.make_async_remote_copy(src_ref=g_ref.at[me], dst_ref=out_ref.at[2 * px + py],
                                         send_sem=send_sems.at[k], recv_sem=recv_sems.at[k],
                                         device_id=(px, py, c), device_id_type=MESH).wait_recv()
        for cp in sends:
            cp.wait_send()

    out = pl.pallas_call(
        body, name=name,
        in_specs=[pl.BlockSpec(memory_space=pltpu.HBM)],
        out_specs=pl.BlockSpec(memory_space=pltpu.HBM),
        out_shape=jax.ShapeDtypeStruct(parts.shape, parts.dtype),
        scratch_shapes=[pltpu.SemaphoreType.DMA((3,)), pltpu.SemaphoreType.DMA((3,))],
        compiler_params=pltpu.CompilerParams(has_side_effects=True),
    )(parts)
    me = _my_chip()
    return lax.dynamic_update_index_in_dim(out, lax.dynamic_index_in_dim(parts, me, 0, keepdims=False), me, 0)


def _my_chip():
    return 2 * lax.axis_index("x") + lax.axis_index("y")


def _half_rows(rows):
    c = lax.axis_index("c")
    half = rows // 2
    return pl.ds(pl.multiple_of(c * half, 8), half), pl.ds(pl.multiple_of((1 - c) * half, 8), half)


def _sibling():
    return (lax.axis_index("x"), lax.axis_index("y"), 1 - lax.axis_index("c"))


def _pair_exchange(parts, *, name):
    n, rows, cols = parts.shape

    def body(p_ref, theirs_ref, send_sem, recv_sem):
        _, sib_rows = _half_rows(rows)
        cp = pltpu.make_async_remote_copy(src_ref=p_ref.at[:, sib_rows, :], dst_ref=theirs_ref, send_sem=send_sem,
                                          recv_sem=recv_sem, device_id=_sibling(), device_id_type=MESH)
        cp.start()
        cp.wait()

    half = rows // 2
    theirs = pl.pallas_call(
        body, name=name,
        in_specs=[pl.BlockSpec(memory_space=pltpu.HBM)],
        out_specs=pl.BlockSpec(memory_space=pltpu.HBM),
        out_shape=jax.ShapeDtypeStruct((n, half, cols), parts.dtype),
        scratch_shapes=[pltpu.SemaphoreType.DMA, pltpu.SemaphoreType.DMA],
        compiler_params=pltpu.CompilerParams(has_side_effects=True),
    )(parts)
    mine = lax.dynamic_slice_in_dim(parts, lax.axis_index("c") * half, half, axis=1)
    return mine, theirs


def _pair_sum(mine, theirs, *, name):
    n, rows, cols = mine.shape

    def body(a_ref, b_ref, o_ref):
        o_ref[...] = (a_ref[...].astype(F32) + b_ref[...].astype(F32)).astype(o_ref.dtype)

    blk = pl.BlockSpec((n, FLAT_ROW_BLOCK, cols), lambda i: (0, i, 0))
    return pl.pallas_call(
        body, name=name, grid=(rows // FLAT_ROW_BLOCK,),
        in_specs=[blk, blk], out_specs=blk, out_shape=jax.ShapeDtypeStruct(mine.shape, mine.dtype),
        compiler_params=pltpu.CompilerParams(dimension_semantics=("parallel",), vmem_limit_bytes=VMEM_LIMIT),
    )(mine, theirs)


def _sum_chips(parts, *, name):
    _, rows, cols = parts.shape

    def body(p_ref, o_ref):
        o_ref[...] = ((p_ref[0].astype(F32) + p_ref[1].astype(F32)) + p_ref[2].astype(F32)) + p_ref[3].astype(F32)

    return pl.pallas_call(
        body, name=name, grid=(rows // FLAT_ROW_BLOCK,),
        in_specs=[pl.BlockSpec((N_CHIPS, FLAT_ROW_BLOCK, cols), lambda i: (0, i, 0))],
        out_specs=pl.BlockSpec((FLAT_ROW_BLOCK, cols), lambda i: (i, 0)),
        out_shape=jax.ShapeDtypeStruct((rows, cols), F32),
        compiler_params=pltpu.CompilerParams(dimension_semantics=("parallel",), vmem_limit_bytes=VMEM_LIMIT),
    )(parts)


def _join_cores(half, *, name):
    rows2, cols = half.shape

    def body(h_ref, out_ref, send_sem, recv_sem):
        my_rows, sib_rows = _half_rows(2 * rows2)
        cp = pltpu.make_async_remote_copy(src_ref=h_ref, dst_ref=out_ref.at[my_rows, :], send_sem=send_sem,
                                          recv_sem=recv_sem, device_id=_sibling(), device_id_type=MESH)
        cp.start()
        cp.wait_send()
        pltpu.make_async_remote_copy(src_ref=h_ref, dst_ref=out_ref.at[sib_rows, :], send_sem=send_sem,
                                     recv_sem=recv_sem, device_id=_sibling(), device_id_type=MESH).wait_recv()

    out = pl.pallas_call(
        body, name=name,
        in_specs=[pl.BlockSpec(memory_space=pltpu.HBM)],
        out_specs=pl.BlockSpec(memory_space=pltpu.HBM),
        out_shape=jax.ShapeDtypeStruct((2 * rows2, cols), half.dtype),
        scratch_shapes=[pltpu.SemaphoreType.DMA, pltpu.SemaphoreType.DMA],
        compiler_params=pltpu.CompilerParams(has_side_effects=True),
    )(half)
    return lax.dynamic_update_slice_in_dim(out, half, lax.axis_index("c") * rows2, axis=0)


def _all_reduce_small(v, *, name):
    rows, cols = v.shape
    flips = [(fx, fy, fc) for fx in (0, 1) for fy in (0, 1) for fc in (0, 1)][1:]

    def body(v_ref, out_ref, gath_ref, send_sems, recv_sems):
        x, y, c = lax.axis_index("x"), lax.axis_index("y"), lax.axis_index("c")
        me = 4 * x + 2 * y + c
        gath_ref[me] = v_ref[...]
        peers = [((1 - x) if fx else x, (1 - y) if fy else y, (1 - c) if fc else c) for fx, fy, fc in flips]
        sends = []
        for k, peer in enumerate(peers):
            cp = pltpu.make_async_remote_copy(src_ref=v_ref, dst_ref=gath_ref.at[me], send_sem=send_sems.at[k],
                                              recv_sem=recv_sems.at[k], device_id=peer, device_id_type=MESH)
            cp.start()
            sends.append(cp)
        for k, (px, py, pc) in enumerate(peers):
            pltpu.make_async_remote_copy(src_ref=v_ref, dst_ref=gath_ref.at[4 * px + 2 * py + pc],
                                         send_sem=send_sems.at[k], recv_sem=recv_sems.at[k],
                                         device_id=(px, py, pc), device_id_type=MESH).wait_recv()
        for cp in sends:
            cp.wait_send()
        total = gath_ref[0]
        for k in range(1, 8):
            total = total + gath_ref[k]
        out_ref[...] = total

    total, _ = pl.pallas_call(
        body, name=name,
        in_specs=[pl.BlockSpec(memory_space=pltpu.VMEM)],
        out_specs=[pl.BlockSpec(memory_space=pltpu.VMEM), pl.BlockSpec(memory_space=pltpu.VMEM)],
        out_shape=[jax.ShapeDtypeStruct((rows, cols), v.dtype), jax.ShapeDtypeStruct((8, rows, cols), v.dtype)],
        scratch_shapes=[pltpu.SemaphoreType.DMA((7,)), pltpu.SemaphoreType.DMA((7,))],
        compiler_params=pltpu.CompilerParams(has_side_effects=True),
    )(v)
    return total


def _adamw(w, g_parts, m, v, *, name):
    rows, cols = w.shape
    br = min(FLAT_ROW_BLOCK, rows)
    n_parts = len(g_parts)

    def body(*refs):
        w_ref = refs[0]
        g_refs = refs[1:1 + n_parts]
        m_ref, v_ref = refs[1 + n_parts], refs[2 + n_parts]
        g_out, d_out, m_out, v_out = refs[-4:]
        g = g_refs[0][...]
        for r in g_refs[1:]:
            g = g + r[...]
        m_new = ADAM_B1 * m_ref[...] + (1.0 - ADAM_B1) * g
        v_new = ADAM_B2 * v_ref[...] + (1.0 - ADAM_B2) * jnp.square(g)
        m_hat = m_new / (1.0 - ADAM_B1 ** ADAM_STEP)
        v_hat = v_new / (1.0 - ADAM_B2 ** ADAM_STEP)
        g_out[...] = g
        d_out[...] = -ADAM_LR * (m_hat / (jnp.sqrt(v_hat) + ADAM_EPS) + ADAM_WD * w_ref[...])
        m_out[...] = m_new
        v_out[...] = v_new

    blk = pl.BlockSpec((br, cols), lambda i: (i, 0))
    shape = jax.ShapeDtypeStruct((rows, cols), F32)
    return pl.pallas_call(
        body, name=name, grid=(rows // br,),
        in_specs=[blk] * (3 + n_parts), out_specs=[blk] * 4, out_shape=[shape] * 4,
        compiler_params=pltpu.CompilerParams(dimension_semantics=("parallel",), vmem_limit_bytes=VMEM_LIMIT),
    )(w, *g_parts, m, v)


def _assemble(gathered_shards, name):
    return jnp.concatenate(gathered_shards, axis=SHARD_AXIS[name])


def _chip_shard(full, name, j):
    axis = SHARD_AXIS[name]
    n = full.shape[axis] // N_CHIPS
    return lax.slice_in_dim(full, j * n, (j + 1) * n, axis=axis)


def kernel(x, positions, attn_norm, mlp_norm, sb_w_qkv, sb_w_o, kv_norm, mla_w_dkv, mla_kv_lat_norm, mla_w_ukv, mla_w_dq, mla_q_lat_norm, mla_w_uq, mla_w_o, mlp_w1, mlp_w2, final_norm, loss_target, m_attn_norm, m_mlp_norm, m_sb_w_qkv, m_sb_w_o, m_kv_norm, m_mla_w_dkv, m_mla_kv_lat_norm, m_mla_w_ukv, m_mla_w_dq, m_mla_q_lat_norm, m_mla_w_uq, m_mla_w_o, m_mlp_w1, m_mlp_w2, m_final_norm, v_attn_norm, v_mlp_norm, v_sb_w_qkv, v_sb_w_o, v_kv_norm, v_mla_w_dkv, v_mla_kv_lat_norm, v_mla_w_ukv, v_mla_w_dq, v_mla_q_lat_norm, v_mla_w_uq, v_mla_w_o, v_mlp_w1, v_mlp_w2, v_final_norm):
    weights = dict(attn_norm=attn_norm, mlp_norm=mlp_norm, sb_w_qkv=sb_w_qkv, sb_w_o=sb_w_o, kv_norm=kv_norm,
                   mla_w_dkv=mla_w_dkv, mla_kv_lat_norm=mla_kv_lat_norm, mla_w_ukv=mla_w_ukv, mla_w_dq=mla_w_dq,
                   mla_q_lat_norm=mla_q_lat_norm, mla_w_uq=mla_w_uq, mla_w_o=mla_w_o, mlp_w1=mlp_w1, mlp_w2=mlp_w2,
                   final_norm=final_norm)
    m_in = dict(attn_norm=m_attn_norm, mlp_norm=m_mlp_norm, sb_w_qkv=m_sb_w_qkv, sb_w_o=m_sb_w_o, kv_norm=m_kv_norm,
                mla_w_dkv=m_mla_w_dkv, mla_kv_lat_norm=m_mla_kv_lat_norm, mla_w_ukv=m_mla_w_ukv, mla_w_dq=m_mla_w_dq,
                mla_q_lat_norm=m_mla_q_lat_norm, mla_w_uq=m_mla_w_uq, mla_w_o=m_mla_w_o, mlp_w1=m_mlp_w1,
                mlp_w2=m_mlp_w2, final_norm=m_final_norm)
    v_in = dict(attn_norm=v_attn_norm, mlp_norm=v_mlp_norm, sb_w_qkv=v_sb_w_qkv, sb_w_o=v_sb_w_o, kv_norm=v_kv_norm,
                mla_w_dkv=v_mla_w_dkv, mla_kv_lat_norm=v_mla_kv_lat_norm, mla_w_ukv=v_mla_w_ukv, mla_w_dq=v_mla_w_dq,
                mla_q_lat_norm=v_mla_q_lat_norm, mla_w_uq=v_mla_w_uq, mla_w_o=v_mla_w_o, mlp_w1=v_mlp_w1,
                mlp_w2=v_mlp_w2, final_norm=v_final_norm)
    shard_shapes = [weights[n].shape for n in BIG_WEIGHTS]
    small_shapes = [weights[n].shape for n in SMALL_WEIGHTS]

    gathered = _all_gather_chips(_pack([weights[n] for n in BIG_WEIGHTS], BF16), name="weights_all_gather")
    per_chip = [_unpack(gathered[j], shard_shapes) for j in range(N_CHIPS)]
    full_w = {n: _assemble([per_chip[j][i] for j in range(N_CHIPS)], n) for i, n in enumerate(BIG_WEIGHTS)}
    norms = {n: weights[n] for n in SMALL_WEIGHTS}

    loss, dx, grads = _local_step(x[0], positions[0], loss_target[0], full_w, norms)
    loss = lax.psum(loss, ("x", "y", "c"))

    parts = jnp.stack([_pack([_chip_shard(grads[n], n, j) for n in BIG_WEIGHTS], BF16) for j in range(N_CHIPS)])
    mine, theirs = _pair_exchange(parts, name="grads_pair_exchange")
    chip_part = _pair_sum(mine, theirs, name="grads_pair_sum")
    received = _exchange_chips(chip_part, name="grads_exchange")
    g_half = _sum_chips(received, name="grads_sum_chips")
    g_sum = _join_cores(g_half, name="grads_join_cores")
    g_flat, d_flat, m_flat, v_flat = _adamw(
        _pack([weights[n] for n in BIG_WEIGHTS], F32), [g_sum],
        _pack([m_in[n] for n in BIG_WEIGHTS], F32), _pack([v_in[n] for n in BIG_WEIGHTS], F32), name="adamw_big")
    out_g = dict(zip(BIG_WEIGHTS, _unpack(g_flat, shard_shapes)))
    out_d = dict(zip(BIG_WEIGHTS, _unpack(d_flat, shard_shapes)))
    out_m = dict(zip(BIG_WEIGHTS, _unpack(m_flat, shard_shapes)))
    out_v = dict(zip(BIG_WEIGHTS, _unpack(v_flat, shard_shapes)))

    small_sum = _all_reduce_small(_pack_small([grads[n] for n in SMALL_WEIGHTS]), name="gains_all_reduce")
    sg, sd, sm, sv = _adamw(_pack_small([weights[n] for n in SMALL_WEIGHTS]), [small_sum],
                            _pack_small([m_in[n] for n in SMALL_WEIGHTS]),
                            _pack_small([v_in[n] for n in SMALL_WEIGHTS]), name="adamw_gains")
    out_g.update(zip(SMALL_WEIGHTS, _unpack_small(sg, small_shapes)))
    out_d.update(zip(SMALL_WEIGHTS, _unpack_small(sd, small_shapes)))
    out_m.update(zip(SMALL_WEIGHTS, _unpack_small(sm, small_shapes)))
    out_v.update(zip(SMALL_WEIGHTS, _unpack_small(sv, small_shapes)))

    return (loss, dx[None], *[out_g[n] for n in ALL_WEIGHTS], *[out_d[n] for n in ALL_WEIGHTS],
            *[out_m[n] for n in ALL_WEIGHTS], *[out_v[n] for n in ALL_WEIGHTS])
```

```python
import functools

import jax
import jax.numpy as jnp
from jax import lax
from jax.experimental import pallas as pl
from jax.experimental.pallas import tpu as pltpu

F32 = jnp.float32
BF16 = jnp.bfloat16

LANES = 128
SB_HEAD_DIM = 64
MLA_NOPE = 64
MLA_ROPE = 32
MLA_V = 64
MLA_Q_RANK = 384
MLA_KV_RANK = 256
CHUNK = 64
ROPE_THETA = 10000.0
NORM_EPS = 1e-6
SB_SCALE = SB_HEAD_DIM ** -0.5
MLA_SCALE = (MLA_NOPE + MLA_ROPE) ** -0.5
ROPE_LO = MLA_NOPE
ROPE_HALF = MLA_ROPE // 2
ATT_Q_BLOCK = 1024
ATT_K_BLOCK = 256
MLA_FWD_K_BLOCK = 512
NEG_BIG = -1e30
SB_DEAD_LOG = -110.0
VMEM_LIMIT = 56 * 1024 * 1024

ADAM_LR = 0.001
ADAM_B1 = 0.9
ADAM_B2 = 0.999
ADAM_EPS = 1e-08
ADAM_WD = 0.01
ADAM_STEP = 10

FLAT_COLS = 1024
FLAT_ROW_BLOCK = 256
N_CHIPS = 4
MESH = pl.DeviceIdType.MESH

BIG_WEIGHTS = ["sb_w_qkv", "sb_w_o", "mla_w_dkv", "mla_w_ukv", "mla_w_dq", "mla_w_uq", "mla_w_o", "mlp_w1", "mlp_w2"]
SHARD_AXIS = {"sb_w_qkv": 2, "sb_w_o": 1, "mla_w_dkv": 0, "mla_w_ukv": 1, "mla_w_dq": 1, "mla_w_uq": 2,
              "mla_w_o": 1, "mlp_w1": 2, "mlp_w2": 1}
SMALL_WEIGHTS = ["attn_norm", "mlp_norm", "kv_norm", "mla_kv_lat_norm", "mla_q_lat_norm", "final_norm"]
ALL_WEIGHTS = ["attn_norm", "mlp_norm", "sb_w_qkv", "sb_w_o", "kv_norm", "mla_w_dkv", "mla_kv_lat_norm", "mla_w_ukv",
               "mla_w_dq", "mla_q_lat_norm", "mla_w_uq", "mla_w_o", "mlp_w1", "mlp_w2", "final_norm"]


def _dot(a, b, dims):
    return lax.dot_general(a, b, (dims, ((), ())), preferred_element_type=F32)


def _dot_nn(a, b):
    return _dot(a, b, ((1,), (0,)))


def _dot_nt(a, b):
    return _dot(a, b, ((1,), (1,)))


def _dot_tn(a, b):
    return _dot(a, b, ((0,), (0,)))


def _pick_block(n, target):
    if n <= target:
        return n
    best = max(b for b in range(LANES, target + 1, LANES) if n % b == 0)
    return best


MM_ROWS = 512
MM_COLS = 1024
MM_DEPTH = 4096
MM_DEPTH_TN = 1024


def _mm(a, b, *, name, dims="nn", epilogue=None, extras=(), out_dtypes=(BF16,)):
    if dims == "nn":
        (m, k), (k2, n) = a.shape, b.shape
    elif dims == "nt":
        (m, k), (n, k2) = a.shape, b.shape
    else:
        (k, m), (k2, n) = a.shape, b.shape
    assert k == k2, (name, a.shape, b.shape)
    if dims == "tn":
        bm, bn, bk = _pick_block(m, MM_COLS), _pick_block(n, MM_COLS), _pick_block(k, MM_DEPTH_TN)
    else:
        rows = MM_ROWS if k > MM_DEPTH // 2 else 2 * MM_ROWS
        bm, bn, bk = _pick_block(m, rows), _pick_block(n, MM_COLS), _pick_block(k, MM_DEPTH)
    nk = k // bk
    if dims == "tn":
        a_spec = pl.BlockSpec((bk, bm), lambda j, i, kk: (kk, i))
    else:
        a_spec = pl.BlockSpec((bm, bk), lambda j, i, kk: (i, kk))
    if dims == "nt":
        b_spec = pl.BlockSpec((bn, bk), lambda j, i, kk: (j, kk))
    else:
        b_spec = pl.BlockSpec((bk, bn), lambda j, i, kk: (kk, j))
    extra_specs = []
    for arr, kind in extras:
        if kind == "tile":
            assert arr.shape == (m, n), (name, arr.shape)
            extra_specs.append(pl.BlockSpec((bm, bn), lambda j, i, kk: (i, j)))
        else:
            assert arr.shape == (m, LANES), (name, arr.shape)
            extra_specs.append(pl.BlockSpec((bm, LANES), lambda j, i, kk: (i, 0)))
    n_extra = len(extras)
    n_out = len(out_dtypes)
    dot = {"nn": _dot_nn, "nt": _dot_nt, "tn": _dot_tn}[dims]

    def body(*refs):
        a_ref, b_ref = refs[0], refs[1]
        extra_refs = refs[2:2 + n_extra]
        out_refs = refs[2 + n_extra:2 + n_extra + n_out]

        def finish(acc):
            outs = (acc,) if epilogue is None else epilogue(acc, *[r[...] for r in extra_refs])
            for o_ref, o in zip(out_refs, outs):
                o_ref[...] = o.astype(o_ref.dtype)

        part = dot(a_ref[...].astype(BF16), b_ref[...].astype(BF16))
        if nk == 1:
            finish(part)
            return
        acc_ref = refs[-1]
        kk = pl.program_id(2)

        @pl.when(kk == 0)
        def _():
            acc_ref[...] = part

        @pl.when(kk > 0)
        def _():
            acc_ref[...] += part

        @pl.when(kk == nk - 1)
        def _():
            finish(acc_ref[...])

    outs = pl.pallas_call(
        body, name=name, grid=(n // bn, m // bm, nk),
        in_specs=[a_spec, b_spec] + extra_specs,
        out_specs=[pl.BlockSpec((bm, bn), lambda j, i, kk: (i, j)) for _ in range(n_out)],
        out_shape=[jax.ShapeDtypeStruct((m, n), dt) for dt in out_dtypes],
        scratch_shapes=[pltpu.VMEM((bm, bn), F32)] if nk > 1 else [],
        compiler_params=pltpu.CompilerParams(dimension_semantics=("parallel", "parallel", "arbitrary"),
                                             vmem_limit_bytes=VMEM_LIMIT),
    )(a, b, *[arr for arr, _ in extras])
    return outs[0] if n_out == 1 else outs


def _epi_add(acc, res):
    return (res + acc,)


def _epi_relu2(acc):
    r = jnp.maximum(acc, 0.0)
    return acc, r * r


def _epi_relu2_grad(acc, u):
    return (acc * (2.0 * jnp.maximum(u.astype(F32), 0.0)),)


def _rope_slab(t, cos_t, sin_t):
    lane = lax.broadcasted_iota(jnp.int32, t.shape, 1)
    partner = jnp.where(lane < ROPE_LO + ROPE_HALF, pltpu.roll(t, LANES - ROPE_HALF, 1), pltpu.roll(t, ROPE_HALF, 1))
    return t * cos_t + partner * sin_t


def _rope_slab_bwd(d, cos_t, sin_t):
    ds = d * sin_t
    lane = lax.broadcasted_iota(jnp.int32, d.shape, 1)
    partner = jnp.where(lane < ROPE_LO + ROPE_HALF, pltpu.roll(ds, LANES - ROPE_HALF, 1), pltpu.roll(ds, ROPE_HALF, 1))
    in_rope = (lane >= ROPE_LO) & (lane < ROPE_LO + MLA_ROPE)
    return d * cos_t + jnp.where(in_rope, partner, 0.0)


def _epi_rope_heads(acc, cos_t, sin_t):
    slabs = [_rope_slab(acc[:, j * LANES:(j + 1) * LANES], cos_t, sin_t) for j in range(acc.shape[1] // LANES)]
    return (jnp.concatenate(slabs, axis=1) * MLA_SCALE,)


def _row_block(s):
    return min(512, s)


def _rms_fwd(x, g, *, name):
    s, d = x.shape
    bm = _row_block(s)

    def body(x_ref, g_ref, o_ref):
        xv = x_ref[...]
        r = lax.rsqrt(jnp.mean(xv * xv, axis=-1, keepdims=True) + NORM_EPS)
        o_ref[...] = ((xv * r) * g_ref[...]).astype(o_ref.dtype)

    return pl.pallas_call(
        body, name=name, grid=(s // bm,),
        in_specs=[pl.BlockSpec((bm, d), lambda i: (i, 0)), pl.BlockSpec((1, d), lambda i: (0, 0))],
        out_specs=pl.BlockSpec((bm, d), lambda i: (i, 0)),
        out_shape=jax.ShapeDtypeStruct((s, d), BF16),
        compiler_params=pltpu.CompilerParams(dimension_semantics=("parallel",), vmem_limit_bytes=VMEM_LIMIT),
    )(x, g.reshape(1, d))


def _rms_bwd_math(xv, gv, dy):
    r = lax.rsqrt(jnp.mean(xv * xv, axis=-1, keepdims=True) + NORM_EPS)
    xhat = xv * r
    dyg = dy * gv
    mdot = jnp.mean(dyg * xhat, axis=-1, keepdims=True)
    dx = r * (dyg - xhat * mdot)
    dg = jnp.sum(dy * xhat, axis=0, keepdims=True)
    return dx, dg


def _rms_bwd(x, g, dy, dres, *, name):
    s, d = x.shape
    bm = _row_block(s)
    has_res = dres is not None

    def body(*refs):
        x_ref, g_ref, dy_ref = refs[:3]
        dres_ref = refs[3] if has_res else None
        dx_ref, dxb_ref, dg_ref = refs[-3:]
        dx, dg = _rms_bwd_math(x_ref[...], g_ref[...], dy_ref[...].astype(F32))
        if has_res:
            dx = dx + dres_ref[...]
        dx_ref[...] = dx
        dxb_ref[...] = dx.astype(BF16)

        @pl.when(pl.program_id(0) == 0)
        def _():
            dg_ref[...] = jnp.zeros_like(dg_ref)

        dg_ref[...] += dg

    row = pl.BlockSpec((bm, d), lambda i: (i, 0))
    vec = pl.BlockSpec((1, d), lambda i: (0, 0))
    ins = [x, g.reshape(1, d), dy] + ([dres] if has_res else [])
    return pl.pallas_call(
        body, name=name, grid=(s // bm,),
        in_specs=[row, vec, row] + ([row] if has_res else []),
        out_specs=[row, row, vec],
        out_shape=[jax.ShapeDtypeStruct((s, d), F32), jax.ShapeDtypeStruct((s, d), BF16),
                   jax.ShapeDtypeStruct((1, d), F32)],
        compiler_params=pltpu.CompilerParams(dimension_semantics=("arbitrary",), vmem_limit_bytes=VMEM_LIMIT),
    )(*ins)


def _loss_bwd(x, g, target, *, name):
    s, d = x.shape
    bm = _row_block(s)

    def body(x_ref, g_ref, t_ref, loss_ref, dx_ref, dxb_ref, dg_ref):
        xv, gv = x_ref[...], g_ref[...]
        r = lax.rsqrt(jnp.mean(xv * xv, axis=-1, keepdims=True) + NORM_EPS)
        err = (xv * r) * gv - t_ref[...]
        dx, dg = _rms_bwd_math(xv, gv, err * (1.0 / d))
        dx_ref[...] = dx
        dxb_ref[...] = dx.astype(BF16)

        @pl.when(pl.program_id(0) == 0)
        def _():
            dg_ref[...] = jnp.zeros_like(dg_ref)
            loss_ref[...] = jnp.zeros_like(loss_ref)

        dg_ref[...] += dg
        loss_ref[...] += jnp.sum(jnp.mean(err * err, axis=-1, keepdims=True), axis=0, keepdims=True) * 0.5

    row = pl.BlockSpec((bm, d), lambda i: (i, 0))
    vec = pl.BlockSpec((1, d), lambda i: (0, 0))
    return pl.pallas_call(
        body, name=name, grid=(s // bm,),
        in_specs=[row, vec, row],
        out_specs=[pl.BlockSpec((8, LANES), lambda i: (0, 0)), row, row, vec],
        out_shape=[jax.ShapeDtypeStruct((8, LANES), F32), jax.ShapeDtypeStruct((s, d), F32),
                   jax.ShapeDtypeStruct((s, d), BF16), jax.ShapeDtypeStruct((1, d), F32)],
        compiler_params=pltpu.CompilerParams(dimension_semantics=("arbitrary",), vmem_limit_bytes=VMEM_LIMIT),
    )(x, g.reshape(1, d), target)


def _kv_prep(down, g, cos_t, sin_t, *, name):
    s, w = down.shape
    bm = _row_block(s)

    def body(d_ref, g_ref, c_ref, s_ref, o_ref):
        lat = d_ref[:, :MLA_KV_RANK]
        r = lax.rsqrt(jnp.mean(lat * lat, axis=-1, keepdims=True) + NORM_EPS)
        o_ref[:, :MLA_KV_RANK] = ((lat * r) * g_ref[...]).astype(BF16)
        o_ref[:, MLA_KV_RANK:] = _rope_slab(d_ref[:, MLA_KV_RANK:], c_ref[...], s_ref[...]).astype(BF16)

    row = pl.BlockSpec((bm, w), lambda i: (i, 0))
    tab = pl.BlockSpec((bm, LANES), lambda i: (i, 0))
    return pl.pallas_call(
        body, name=name, grid=(s // bm,),
        in_specs=[row, pl.BlockSpec((1, MLA_KV_RANK), lambda i: (0, 0)), tab, tab],
        out_specs=row, out_shape=jax.ShapeDtypeStruct((s, w), BF16),
        compiler_params=pltpu.CompilerParams(dimension_semantics=("parallel",), vmem_limit_bytes=VMEM_LIMIT),
    )(down, g.reshape(1, MLA_KV_RANK), cos_t, sin_t)


def _kv_prep_bwd(down, g, cos_t, sin_t, dcat, *, name):
    s, w = down.shape
    bm = _row_block(s)

    def body(d_ref, g_ref, c_ref, s_ref, dc_ref, o_ref, dg_ref):
        dlat, dg = _rms_bwd_math(d_ref[:, :MLA_KV_RANK], g_ref[...], dc_ref[:, :MLA_KV_RANK])
        o_ref[:, :MLA_KV_RANK] = dlat.astype(BF16)
        o_ref[:, MLA_KV_RANK:] = _rope_slab_bwd(dc_ref[:, MLA_KV_RANK:], c_ref[...], s_ref[...]).astype(BF16)

        @pl.when(pl.program_id(0) == 0)
        def _():
            dg_ref[...] = jnp.zeros_like(dg_ref)

        dg_ref[...] += dg

    row = pl.BlockSpec((bm, w), lambda i: (i, 0))
    tab = pl.BlockSpec((bm, LANES), lambda i: (i, 0))
    vec = pl.BlockSpec((1, MLA_KV_RANK), lambda i: (0, 0))
    return pl.pallas_call(
        body, name=name, grid=(s // bm,),
        in_specs=[row, vec, tab, tab, row],
        out_specs=[row, vec],
        out_shape=[jax.ShapeDtypeStruct((s, w), BF16), jax.ShapeDtypeStruct((1, MLA_KV_RANK), F32)],
        compiler_params=pltpu.CompilerParams(dimension_semantics=("arbitrary",), vmem_limit_bytes=VMEM_LIMIT),
    )(down, g.reshape(1, MLA_KV_RANK), cos_t, sin_t, dcat)


def _split_bf16(v):
    hi = v.astype(BF16)
    lo = (v - hi.astype(F32)).astype(BF16)
    return hi, lo


def _suffix_matrices(n):
    row = lax.broadcasted_iota(jnp.int32, (n, n), 0)
    col = lax.broadcasted_iota(jnp.int32, (n, n), 1)
    incl = (row >= col).astype(BF16)
    return (row > col).astype(BF16), jnp.concatenate([incl, incl], axis=0)


def _suffix_sum(v, matrix):
    hi, lo = _split_bf16(v)
    return _dot_nn(jnp.concatenate([hi, lo], axis=1), matrix)


def _block_positions(qi, kb, bq, bk, r0):
    row = qi * bq + r0 + lax.broadcasted_iota(jnp.int32, (bq - r0, bk), 0)
    col = kb * bk + lax.broadcasted_iota(jnp.int32, (bq - r0, bk), 1)
    return row, col


def _att_blocks(s, key_block=ATT_K_BLOCK):
    bq, bk = min(ATT_Q_BLOCK, s), min(key_block, s)
    return bq, bk, s // bq, bq // bk


def _sweep(qi, ratio, bk, step, unroll=2, alive=None):
    for d in range(ratio):
        step((qi + 1) * ratio - 1 - d, True, (ratio - 1 - d) * bk)
    unroll = unroll if ratio % unroll == 0 else 1
    trips = qi * (ratio // unroll)

    def trip(i):
        for u in range(unroll):
            step(qi * ratio - 1 - (i * unroll + u), False, 0)

    if alive is None:
        lax.fori_loop(0, trips, lambda i, carry: (trip(i), carry)[1], 0)
    else:
        lax.while_loop(lambda i: jnp.logical_and(i < trips, alive()), lambda i: (trip(i), i + 1)[1], 0)


def _stick_left(c_ref):
    return jnp.max(c_ref[...]) > SB_DEAD_LOG


def _sb_logs(q, k):
    z = _dot_nt(q, k)
    lb = jnp.minimum(z, 0.0) - jnp.log(1.0 + jnp.exp(-jnp.abs(z)))
    return lb, lb - z


def _sb_fwd(qkv, heads, *, name):
    s = qkv.shape[0]
    bq, bk, nq, ratio = _att_blocks(s)

    def body(q_ref, k_ref, v_ref, o_ref, acc_ref, c_ref):
        qi = pl.program_id(1)
        q = q_ref[...] * SB_SCALE
        m_strict, _ = _suffix_matrices(bk)
        acc_ref[...] = jnp.zeros_like(acc_ref)
        c_ref[...] = jnp.zeros_like(c_ref)

        def step(kb, masked, r0):
            rows = pl.ds(pl.multiple_of(kb * bk, bk), bk)
            mine = pl.ds(r0, bq - r0)
            k, v = k_ref[rows, :], v_ref[rows, :]
            lb, lk = _sb_logs(q[r0:], k)
            if masked:
                row, col = _block_positions(qi, kb, bq, bk, r0)
                causal = col < row
                lk = jnp.where(causal, lk, 0.0)
            c = c_ref[mine, :]
            w = jnp.exp(lb + _dot_nn(lk.astype(BF16), m_strict) + jnp.tile(c, (1, bk // LANES)))
            if masked:
                w = jnp.where(causal, w, 0.0)
            acc_ref[mine, :] += _dot_nn(w.astype(BF16), v)
            c_ref[mine, :] = c + jnp.sum(lk, axis=-1, keepdims=True)

        _sweep(qi, ratio, bk, step, alive=lambda: _stick_left(c_ref))
        o_ref[...] = acc_ref[...].astype(o_ref.dtype)

    return pl.pallas_call(
        body, name=name, grid=(heads, nq),
        in_specs=[pl.BlockSpec((bq, LANES), lambda h, i: (i, h)),
                  pl.BlockSpec((s, LANES), lambda h, i: (0, heads + h)),
                  pl.BlockSpec((s, LANES), lambda h, i: (0, 2 * heads + h))],
        out_specs=pl.BlockSpec((bq, LANES), lambda h, i: (i, h)),
        out_shape=jax.ShapeDtypeStruct((s, heads * LANES), F32),
        scratch_shapes=[pltpu.VMEM((bq, LANES), F32), pltpu.VMEM((bq, LANES), F32)],
        compiler_params=pltpu.CompilerParams(dimension_semantics=("parallel", "arbitrary"),
                                             vmem_limit_bytes=VMEM_LIMIT),
    )(qkv, qkv, qkv)


def _sb_bwd(qkv, o, do, heads, *, name):
    s = qkv.shape[0]
    bq, bk, nq, ratio = _att_blocks(s)

    def body(q_ref, k_ref, v_ref, o_ref, do_ref, dq_ref, dk_ref, dv_ref, dq_acc, dk_acc, dv_acc, c_ref, e_ref):
        qi = pl.program_id(1)

        @pl.when(qi == 0)
        def _():
            dk_acc[...] = jnp.zeros_like(dk_acc)
            dv_acc[...] = jnp.zeros_like(dv_acc)

        q = q_ref[...] * SB_SCALE
        do = do_ref[...]
        total = jnp.sum(do.astype(F32) * o_ref[...].astype(F32), axis=-1, keepdims=True)
        m_strict, m_incl = _suffix_matrices(bk)
        dq_acc[...] = jnp.zeros_like(dq_acc)
        c_ref[...] = jnp.zeros_like(c_ref)
        e_ref[...] = jnp.broadcast_to(total, e_ref.shape)
        reps = (1, bk // LANES)

        def step(kb, masked, r0):
            rows = pl.ds(pl.multiple_of(kb * bk, bk), bk)
            mine = pl.ds(r0, bq - r0)
            k, v = k_ref[rows, :], v_ref[rows, :]
            qs, dos = q[r0:], do[r0:]
            lb, lk_all = _sb_logs(qs, k)
            lk = lk_all
            if masked:
                row, col = _block_positions(qi, kb, bq, bk, r0)
                causal = col < row
                lk = jnp.where(causal, lk_all, 0.0)
            c = c_ref[mine, :]
            w = jnp.exp(lb + _dot_nn(lk.astype(BF16), m_strict) + jnp.tile(c, reps))
            if masked:
                w = jnp.where(causal, w, 0.0)
            wb = w.astype(BF16)
            g = wb.astype(F32) * _dot_nt(dos, v)
            e = e_ref[mine, :]
            g_left = jnp.tile(e, reps) - _suffix_sum(g, m_incl)
            da = g * jnp.exp(lk_all) - jnp.exp(lb) * g_left
            if masked:
                da = jnp.where(causal, da, 0.0)
            dab = da.astype(BF16)
            dq_acc[mine, :] += _dot_nn(dab, k)
            dk_acc[rows, :] += _dot_tn(dab, qs)
            dv_acc[rows, :] += _dot_tn(wb, dos)
            e_ref[mine, :] = e - jnp.sum(g, axis=-1, keepdims=True)
            c_ref[mine, :] = c + jnp.sum(lk, axis=-1, keepdims=True)

        _sweep(qi, ratio, bk, step, alive=lambda: _stick_left(c_ref))
        dq_ref[...] = (dq_acc[...] * SB_SCALE).astype(dq_ref.dtype)

        @pl.when(qi == nq - 1)
        def _():
            dk_ref[...] = dk_acc[...].astype(dk_ref.dtype)
            dv_ref[...] = dv_acc[...].astype(dv_ref.dtype)

    blk = pl.BlockSpec((bq, LANES), lambda h, i: (i, h))
    full = pl.BlockSpec((s, LANES), lambda h, i: (0, h))
    shape = jax.ShapeDtypeStruct((s, heads * LANES), BF16)
    return pl.pallas_call(
        body, name=name, grid=(heads, nq),
        in_specs=[blk,
                  pl.BlockSpec((s, LANES), lambda h, i: (0, heads + h)),
                  pl.BlockSpec((s, LANES), lambda h, i: (0, 2 * heads + h)),
                  blk, blk],
        out_specs=[blk, full, full],
        out_shape=[shape, shape, shape],
        scratch_shapes=[pltpu.VMEM((bq, LANES), F32), pltpu.VMEM((s, LANES), F32), pltpu.VMEM((s, LANES), F32),
                        pltpu.VMEM((bq, LANES), F32), pltpu.VMEM((bq, LANES), F32)],
        compiler_params=pltpu.CompilerParams(dimension_semantics=("arbitrary", "arbitrary"),
                                             vmem_limit_bytes=VMEM_LIMIT),
    )(qkv, qkv, qkv, o, do)


def _chunk_allowed(qi, kb, bq, bk, r0):
    row, col = _block_positions(qi, kb, bq, bk, r0)
    return (col // CHUNK) <= (row // CHUNK)


def _mla_fwd(q, kv, heads, *, name):
    s = q.shape[0]
    bq, bk, nq, ratio = _att_blocks(s, MLA_FWD_K_BLOCK)
    reps = (1, bk // LANES)

    def body(q_ref, k_ref, v_ref, o_ref, lse_ref, acc_ref, m_ref, l_ref):
        qi = pl.program_id(1)
        qv = q_ref[...]
        acc_ref[...] = jnp.zeros_like(acc_ref)
        m_ref[...] = jnp.full_like(m_ref, NEG_BIG)
        l_ref[...] = jnp.zeros_like(l_ref)

        def step(kb, masked, r0):
            rows = pl.ds(pl.multiple_of(kb * bk, bk), bk)
            mine = pl.ds(r0, bq - r0)
            k, v = k_ref[rows, :], v_ref[rows, :]
            sc = _dot_nt(qv[r0:], k)
            if masked:
                allowed = _chunk_allowed(qi, kb, bq, bk, r0)
                sc = jnp.where(allowed, sc, NEG_BIG)
            m_old = m_ref[mine, :]
            m_new = jnp.maximum(m_old, jnp.max(sc, axis=-1, keepdims=True))
            p = jnp.exp(sc - jnp.tile(m_new, reps))
            alpha = jnp.exp(m_old - m_new)
            l_ref[mine, :] = alpha * l_ref[mine, :] + jnp.sum(p, axis=-1, keepdims=True)
            acc_ref[mine, :] = alpha * acc_ref[mine, :] + _dot_nn(p.astype(BF16), v)
            m_ref[mine, :] = m_new

        _sweep(qi, ratio, bk, step)
        o_ref[...] = (acc_ref[...] / l_ref[...]).astype(o_ref.dtype)
        lse_ref[...] = m_ref[...] + jnp.log(l_ref[...])

    blk = pl.BlockSpec((bq, LANES), lambda h, i: (i, h))
    return pl.pallas_call(
        body, name=name, grid=(heads, nq),
        in_specs=[blk,
                  pl.BlockSpec((s, LANES), lambda h, i: (0, h)),
                  pl.BlockSpec((s, LANES), lambda h, i: (0, heads + h))],
        out_specs=[blk, blk],
        out_shape=[jax.ShapeDtypeStruct((s, heads * LANES), BF16), jax.ShapeDtypeStruct((s, heads * LANES), F32)],
        scratch_shapes=[pltpu.VMEM((bq, LANES), F32), pltpu.VMEM((bq, LANES), F32), pltpu.VMEM((bq, LANES), F32)],
        compiler_params=pltpu.CompilerParams(dimension_semantics=("parallel", "arbitrary"),
                                             vmem_limit_bytes=VMEM_LIMIT),
    )(q, kv, kv)


def _mla_bwd(q, kv, o, do, lse, cos_t, sin_t, dkv_init, heads, *, name):
    s = q.shape[0]
    bq, bk, nq, ratio = _att_blocks(s)
    reps = (1, bk // LANES)
    has_init = dkv_init is not None

    def body(*refs):
        q_ref, k_ref, v_ref, o_ref, do_ref, lse_ref, c_ref, s_ref = refs[:8]
        ki_ref, vi_ref = (refs[8], refs[9]) if has_init else (None, None)
        dq_ref, dk_ref, dv_ref, dq_acc, dk_acc, dv_acc = refs[-6:]
        qi = pl.program_id(1)

        @pl.when(qi == 0)
        def _():
            if has_init:
                dk_acc[...] = ki_ref[...].astype(F32)
                dv_acc[...] = vi_ref[...].astype(F32)
            else:
                dk_acc[...] = jnp.zeros_like(dk_acc)
                dv_acc[...] = jnp.zeros_like(dv_acc)

        qv = q_ref[...]
        do = do_ref[...]
        delta = jnp.sum(do.astype(F32) * o_ref[...].astype(F32), axis=-1, keepdims=True)
        lse_wide = jnp.tile(lse_ref[...], reps)
        dq_acc[...] = jnp.zeros_like(dq_acc)

        def step(kb, masked, r0):
            rows = pl.ds(pl.multiple_of(kb * bk, bk), bk)
            k, v = k_ref[rows, :], v_ref[rows, :]
            qs, dos = qv[r0:], do[r0:]
            p = jnp.exp(_dot_nt(qs, k) - lse_wide[r0:])
            if masked:
                p = jnp.where(_chunk_allowed(qi, kb, bq, bk, r0), p, 0.0)
            ds = (p * (_dot_nt(dos, v) - delta[r0:])).astype(BF16)
            dq_acc[pl.ds(r0, bq - r0), :] += _dot_nn(ds, k)
            dk_acc[rows, :] += _dot_tn(ds, qs)
            dv_acc[rows, :] += _dot_tn(p.astype(BF16), dos)

        _sweep(qi, ratio, bk, step)
        dq_ref[...] = _rope_slab_bwd(dq_acc[...] * MLA_SCALE, c_ref[...], s_ref[...]).astype(dq_ref.dtype)

        @pl.when(qi == nq - 1)
        def _():
            dk_ref[...] = dk_acc[...].astype(dk_ref.dtype)
            dv_ref[...] = dv_acc[...].astype(dv_ref.dtype)

    blk = pl.BlockSpec((bq, LANES), lambda h, i: (i, h))
    tab = pl.BlockSpec((bq, LANES), lambda h, i: (i, 0))
    k_full = pl.BlockSpec((s, LANES), lambda h, i: (0, h))
    v_full = pl.BlockSpec((s, LANES), lambda h, i: (0, heads + h))
    shape = jax.ShapeDtypeStruct((s, heads * LANES), BF16)
    ins = [q, kv, kv, o, do, lse, cos_t, sin_t] + ([dkv_init, dkv_init] if has_init else [])
    dq, dk, dv = pl.pallas_call(
        body, name=name, grid=(heads, nq),
        in_specs=[blk, k_full, v_full, blk, blk, blk, tab, tab] + ([k_full, v_full] if has_init else []),
        out_specs=[blk, k_full, k_full],
        out_shape=[shape, shape, shape],
        scratch_shapes=[pltpu.VMEM((bq, LANES), F32), pltpu.VMEM((s, LANES), F32), pltpu.VMEM((s, LANES), F32)],
        compiler_params=pltpu.CompilerParams(dimension_semantics=("arbitrary", "arbitrary"),
                                             vmem_limit_bytes=VMEM_LIMIT),
    )(*ins)
    return dq, jnp.concatenate([dk, dv], axis=1)


def _pad_last(a, width):
    return jnp.pad(a, [(0, 0)] * (a.ndim - 1) + [(0, width - a.shape[-1])])


def _pad_qkv(w, heads):
    d = w.shape[0]
    return _pad_last(w.reshape(d, 3 * heads, SB_HEAD_DIM), LANES).reshape(d, 3 * heads * LANES)


def _unpad_qkv(g, heads):
    d = g.shape[0]
    return g.reshape(d, 3 * heads, LANES)[:, :, :SB_HEAD_DIM].reshape(d, 3 * heads * SB_HEAD_DIM)


def _pad_o(w, heads):
    d = w.shape[1]
    w = w.reshape(heads, SB_HEAD_DIM, d)
    return jnp.pad(w, [(0, 0), (0, LANES - SB_HEAD_DIM), (0, 0)]).reshape(heads * LANES, d)


def _unpad_o(g, heads):
    d = g.shape[1]
    return g.reshape(heads, LANES, d)[:, :SB_HEAD_DIM, :].reshape(heads * SB_HEAD_DIM, d)


def _pad_uq(w, heads):
    r = w.shape[0]
    return _pad_last(w.reshape(r, heads, MLA_NOPE + MLA_ROPE), LANES).reshape(r, heads * LANES)


def _unpad_uq(g, heads):
    r = g.shape[0]
    return g.reshape(r, heads, LANES)[:, :, :MLA_NOPE + MLA_ROPE].reshape(r, heads * (MLA_NOPE + MLA_ROPE))


def _pad_dkv(w):
    d = w.shape[0]
    rope = jnp.zeros((d, LANES), w.dtype).at[:, ROPE_LO:ROPE_LO + MLA_ROPE].set(w[:, MLA_KV_RANK:])
    return jnp.concatenate([w[:, :MLA_KV_RANK], rope], axis=1)


def _unpad_dkv(g):
    return jnp.concatenate([g[:, :MLA_KV_RANK], g[:, MLA_KV_RANK + ROPE_LO:MLA_KV_RANK + ROPE_LO + MLA_ROPE]], axis=1)


def _pad_ukv(w, heads):
    w = w.reshape(MLA_KV_RANK, heads, 2, MLA_NOPE)
    k_part = _pad_last(w[:, :, 0, :], LANES).reshape(MLA_KV_RANK, heads * LANES)
    v_part = _pad_last(w[:, :, 1, :], LANES).reshape(MLA_KV_RANK, heads * LANES)
    lane = jnp.arange(LANES)
    place = ((lane[:, None] == lane[None, :]) & (lane[:, None] >= ROPE_LO) & (lane[:, None] < ROPE_LO + MLA_ROPE))
    place = jnp.tile(place.astype(w.dtype), (1, heads))
    top = jnp.concatenate([k_part, v_part], axis=1)
    bottom = jnp.concatenate([place, jnp.zeros_like(place)], axis=1)
    return jnp.concatenate([top, bottom], axis=0)


def _unpad_ukv(g, heads):
    g = g[:MLA_KV_RANK]
    k_part = g[:, :heads * LANES].reshape(MLA_KV_RANK, heads, LANES)[:, :, :MLA_NOPE]
    v_part = g[:, heads * LANES:].reshape(MLA_KV_RANK, heads, LANES)[:, :, :MLA_V]
    return jnp.stack([k_part, v_part], axis=2).reshape(MLA_KV_RANK, heads * (MLA_NOPE + MLA_V))


def _rope_tables(positions):
    inv_freq = ROPE_THETA ** (-jnp.arange(0, MLA_ROPE, 2, dtype=F32) / MLA_ROPE)
    ang = positions.astype(F32)[:, None] * inv_freq
    cos, sin = jnp.cos(ang), jnp.sin(ang)
    s = positions.shape[0]
    cos_t = jnp.ones((s, LANES), F32).at[:, ROPE_LO:ROPE_LO + MLA_ROPE].set(jnp.concatenate([cos, cos], axis=1))
    sin_t = jnp.zeros((s, LANES), F32).at[:, ROPE_LO:ROPE_LO + MLA_ROPE].set(jnp.concatenate([-sin, sin], axis=1))
    return cos_t, sin_t


def _local_step(x, positions, target, w, norms):
    s, d = x.shape
    heads = d // SB_HEAD_DIM
    n_a = w["sb_w_qkv"].shape[0]
    n_b = w["mla_w_dq"].shape[0]
    depth = n_a + n_b
    cos_t, sin_t = _rope_tables(positions)

    wqkv = [_pad_qkv(w["sb_w_qkv"][l], heads) for l in range(n_a)]
    wo_a = [_pad_o(w["sb_w_o"][l], heads) for l in range(n_a)]
    wdkv = _pad_dkv(w["mla_w_dkv"])
    wkv = _pad_ukv(w["mla_w_ukv"], heads)
    wdq = [w["mla_w_dq"][j] for j in range(n_b)]
    wuq = [_pad_uq(w["mla_w_uq"][j], heads) for j in range(n_b)]
    wo_b = [_pad_o(w["mla_w_o"][j], heads) for j in range(n_b)]
    w1 = [w["mlp_w1"][l] for l in range(depth)]
    w2 = [w["mlp_w2"][l] for l in range(depth)]

    saved = []
    kv_saved = None
    kv = None
    for l in range(depth):
        t = f"l{l}"
        sv = {"x_in": x}
        h = _rms_fwd(x, norms["attn_norm"][l], name=f"{t}_attn_norm")
        sv["h"] = h
        if l < n_a:
            qkv = _mm(h, wqkv[l], name=f"{t}_qkv")
            o = _sb_fwd(qkv, heads, name=f"{t}_sb_fwd")
            sv["qkv"], sv["o"] = qkv, o
            x = _mm(o, wo_a[l], name=f"{t}_attn_out", epilogue=_epi_add, extras=[(x, "tile")], out_dtypes=(F32,))
        else:
            j = l - n_a
            if j == 0:
                hk = _rms_fwd(x, norms["kv_norm"], name="kv_norm")
                down = _mm(hk, wdkv, name="kv_down", out_dtypes=(F32,))
                cat = _kv_prep(down, norms["mla_kv_lat_norm"], cos_t, sin_t, name="kv_prep")
                kv = _mm(cat, wkv, name="kv_up")
                kv_saved = {"x_in": x, "hk": hk, "down": down, "cat": cat}
            cq0 = _mm(h, wdq[j], name=f"{t}_q_down", out_dtypes=(F32,))
            cq = _rms_fwd(cq0, norms["mla_q_lat_norm"][j], name=f"{t}_q_lat_norm")
            q = _mm(cq, wuq[j], name=f"{t}_q_up", epilogue=_epi_rope_heads, extras=[(cos_t, "row"), (sin_t, "row")])
            o, lse = _mla_fwd(q, kv, heads, name=f"{t}_mla_fwd")
            sv.update(cq0=cq0, cq=cq, q=q, o=o, lse=lse)
            x = _mm(o, wo_b[j], name=f"{t}_attn_out", epilogue=_epi_add, extras=[(x, "tile")], out_dtypes=(F32,))
        sv["x_mid"] = x
        h2 = _rms_fwd(x, norms["mlp_norm"][l], name=f"{t}_mlp_norm")
        u, a = _mm(h2, w1[l], name=f"{t}_mlp_up", epilogue=_epi_relu2, out_dtypes=(BF16, BF16))
        sv.update(h2=h2, u=u, a=a)
        x = _mm(a, w2[l], name=f"{t}_mlp_down", epilogue=_epi_add, extras=[(x, "tile")], out_dtypes=(F32,))
        saved.append(sv)

    loss_slab, dx, dxb, dg_final = _loss_bwd(x, norms["final_norm"], target, name="loss")
    loss = loss_slab[0, 0]

    g_attn_norm, g_mlp_norm = [None] * depth, [None] * depth
    g_qkv, g_o_a = [None] * n_a, [None] * n_a
    g_dq, g_uq, g_o_b, g_qlat = [None] * n_b, [None] * n_b, [None] * n_b, [None] * n_b
    g_w1, g_w2 = [None] * depth, [None] * depth
    dkv = None
    g_kv_norm = g_kv_lat = g_dkv = g_ukv = None

    for l in reversed(range(depth)):
        t = f"l{l}"
        sv = saved[l]
        du = _mm(dxb, w2[l], name=f"{t}_mlp_down_dx", dims="nt", epilogue=_epi_relu2_grad, extras=[(sv["u"], "tile")])
        g_w2[l] = _mm(sv["a"], dxb, name=f"{t}_mlp_down_dw", dims="tn", out_dtypes=(F32,))
        g_w1[l] = _mm(sv["h2"], du, name=f"{t}_mlp_up_dw", dims="tn", out_dtypes=(F32,))
        dh2 = _mm(du, w1[l], name=f"{t}_mlp_up_dx", dims="nt", out_dtypes=(F32,))
        dx, dxb, g_mlp_norm[l] = _rms_bwd(sv["x_mid"], norms["mlp_norm"][l], dh2, dx, name=f"{t}_mlp_norm_bwd")
        if l < n_a:
            do = _mm(dxb, wo_a[l], name=f"{t}_attn_out_dx", dims="nt")
            g_o_a[l] = _unpad_o(_mm(sv["o"], dxb, name=f"{t}_attn_out_dw", dims="tn", out_dtypes=(F32,)), heads)
            dq, dk, dv = _sb_bwd(sv["qkv"], sv["o"], do, heads, name=f"{t}_sb_bwd")
            dqkv = jnp.concatenate([dq, dk, dv], axis=1)
            g_qkv[l] = _unpad_qkv(_mm(sv["h"], dqkv, name=f"{t}_qkv_dw", dims="tn", out_dtypes=(F32,)), heads)
            dh = _mm(dqkv, wqkv[l], name=f"{t}_qkv_dx", dims="nt", out_dtypes=(F32,))
        else:
            j = l - n_a
            do = _mm(dxb, wo_b[j], name=f"{t}_attn_out_dx", dims="nt")
            g_o_b[j] = _unpad_o(_mm(sv["o"], dxb, name=f"{t}_attn_out_dw", dims="tn", out_dtypes=(F32,)), heads)
            dq, dkv = _mla_bwd(sv["q"], kv, sv["o"], do, sv["lse"], cos_t, sin_t, dkv, heads, name=f"{t}_mla_bwd")
            g_uq[j] = _unpad_uq(_mm(sv["cq"], dq, name=f"{t}_q_up_dw", dims="tn", out_dtypes=(F32,)), heads)
            dcq = _mm(dq, wuq[j], name=f"{t}_q_up_dx", dims="nt", out_dtypes=(F32,))
            _, dcq0, g_qlat[j] = _rms_bwd(sv["cq0"], norms["mla_q_lat_norm"][j], dcq, None, name=f"{t}_q_lat_norm_bwd")
            g_dq[j] = _mm(sv["h"], dcq0, name=f"{t}_q_down_dw", dims="tn", out_dtypes=(F32,))
            dh = _mm(dcq0, wdq[j], name=f"{t}_q_down_dx", dims="nt", out_dtypes=(F32,))
        dx, dxb, g_attn_norm[l] = _rms_bwd(sv["x_in"], norms["attn_norm"][l], dh, dx, name=f"{t}_attn_norm_bwd")
        if l == n_a:
            ks = kv_saved
            dcat = _mm(dkv, wkv, name="kv_up_dx", dims="nt", out_dtypes=(F32,))
            g_ukv = _unpad_ukv(_mm(ks["cat"], dkv, name="kv_up_dw", dims="tn", out_dtypes=(F32,)), heads)
            ddown, g_kv_lat = _kv_prep_bwd(ks["down"], norms["mla_kv_lat_norm"], cos_t, sin_t, dcat, name="kv_prep_bwd")
            g_dkv = _unpad_dkv(_mm(ks["hk"], ddown, name="kv_down_dw", dims="tn", out_dtypes=(F32,)))
            dhk = _mm(ddown, wdkv, name="kv_down_dx", dims="nt", out_dtypes=(F32,))
            dx, dxb, g_kv_norm = _rms_bwd(ks["x_in"], norms["kv_norm"], dhk, dx, name="kv_norm_bwd")

    grads = {
        "attn_norm": jnp.concatenate(g_attn_norm, axis=0), "mlp_norm": jnp.concatenate(g_mlp_norm, axis=0),
        "sb_w_qkv": g_qkv, "sb_w_o": g_o_a,
        "kv_norm": g_kv_norm[0], "mla_w_dkv": g_dkv, "mla_kv_lat_norm": g_kv_lat[0], "mla_w_ukv": g_ukv,
        "mla_w_dq": g_dq, "mla_q_lat_norm": jnp.concatenate(g_qlat, axis=0),
        "mla_w_uq": g_uq, "mla_w_o": g_o_b,
        "mlp_w1": g_w1, "mlp_w2": g_w2, "final_norm": dg_final[0],
    }
    return loss, dx, grads


def _flat_rows(n_elems):
    per_block = FLAT_COLS * FLAT_ROW_BLOCK * 2
    return -(-n_elems // per_block) * FLAT_ROW_BLOCK * 2


def _pack(arrays, dtype):
    flat = jnp.concatenate([a.reshape(-1).astype(dtype) for a in arrays])
    rows = _flat_rows(flat.shape[0])
    flat = jnp.pad(flat, (0, rows * FLAT_COLS - flat.shape[0]))
    return flat.reshape(rows, FLAT_COLS)


def _unpack(flat, shapes):
    flat = flat.reshape(-1)
    out, off = [], 0
    for shp in shapes:
        n = 1
        for v in shp:
            n *= v
        out.append(flat[off:off + n].reshape(shp))
        off += n
    return out


def _pack_small(arrays):
    rows = []
    for a in arrays:
        a = a.reshape(-1, a.shape[-1]) if a.shape[-1] == FLAT_COLS else a.reshape(1, -1)
        rows.append(_pad_last(a, FLAT_COLS))
    flat = jnp.concatenate(rows, axis=0)
    return jnp.pad(flat, [(0, -flat.shape[0] % 8), (0, 0)])


def _unpack_small(flat, shapes):
    out, row = [], 0
    for shp in shapes:
        if shp[-1] == FLAT_COLS:
            n = 1
            for v in shp[:-1]:
                n *= v
            out.append(flat[row:row + n].reshape(shp))
            row += n
        else:
            n = 1
            for v in shp:
                n *= v
            out.append(flat[row, :n].reshape(shp))
            row += 1
    return out


def _other_chips(x, y):
    return [(1 - x, y), (x, 1 - y), (1 - x, 1 - y)]


def _all_gather_chips(flat, *, name):
    rows, cols = flat.shape

    def body(x_ref, out_ref, send_sems, recv_sems, pass_send_sems, pass_recv_sems):
        x, y, c = lax.axis_index("x"), lax.axis_index("y"), lax.axis_index("c")
        me = 2 * x + y
        my_rows, sib_rows = _half_rows(rows)
        chips = _other_chips(x, y)
        sends = []
        for k, (px, py) in enumerate(chips):
            cp = pltpu.make_async_remote_copy(src_ref=x_ref.at[my_rows, :], dst_ref=out_ref.at[me, my_rows, :],
                                              send_sem=send_sems.at[k], recv_sem=recv_sems.at[k],
                                              device_id=(px, py, c), device_id_type=MESH)
            cp.start()
            sends.append(cp)
        for k, (px, py) in enumerate(chips):
            landed = out_ref.at[2 * px + py, my_rows, :]
            pltpu.make_async_remote_copy(src_ref=landed, dst_ref=landed, send_sem=send_sems.at[k],
                                         recv_sem=recv_sems.at[k], device_id=(px, py, c),
                                         device_id_type=MESH).wait_recv()
            cp = pltpu.make_async_remote_copy(src_ref=landed, dst_ref=landed, send_sem=pass_send_sems.at[k],
                                              recv_sem=pass_recv_sems.at[k], device_id=_sibling(),
                                              device_id_type=MESH)
            cp.start()
            sends.append(cp)
        for k, (px, py) in enumerate(chips):
            passed = out_ref.at[2 * px + py, sib_rows, :]
            pltpu.make_async_remote_copy(src_ref=passed, dst_ref=passed, send_sem=pass_send_sems.at[k],
                                         recv_sem=pass_recv_sems.at[k], device_id=_sibling(),
                                         device_id_type=MESH).wait_recv()
        for cp in sends:
            cp.wait_send()

    out = pl.pallas_call(
        body, name=name,
        in_specs=[pl.BlockSpec(memory_space=pltpu.HBM)],
        out_specs=pl.BlockSpec(memory_space=pltpu.HBM),
        out_shape=jax.ShapeDtypeStruct((N_CHIPS, rows, cols), flat.dtype),
        scratch_shapes=[pltpu.SemaphoreType.DMA((3,)), pltpu.SemaphoreType.DMA((3,)), pltpu.SemaphoreType.DMA((3,)),
                        pltpu.SemaphoreType.DMA((3,))],
        compiler_params=pltpu.CompilerParams(has_side_effects=True),
    )(flat)
    return out


def _exchange_chips(parts, *, name):
    def body(g_ref, out_ref, send_sems, recv_sems):
        x, y, c = lax.axis_index("x"), lax.axis_index("y"), lax.axis_index("c")
        me = 2 * x + y
        sends = []
        for k, (px, py) in enumerate(_other_chips(x, y)):
            cp = pltpu.make_async_remote_copy(src_ref=g_ref.at[2 * px + py], dst_ref=out_ref.at[me],
                                              send_sem=send_sems.at[k], recv_sem=recv_sems.at[k],
                                              device_id=(px, py, c), device_id_type=MESH)
            cp.start()
            sends.append(cp)
        for k, (px, py) in enumerate(_other_chips(x, y)):
            pltpu.make_async_remote_copy(src_ref=g_ref.at[me], dst_ref=out_ref.at[2 * px + py],
                                         send_sem=send_sems.at[k], recv_sem=recv_sems.at[k],
                                         device_id=(px, py, c), device_id_type=MESH).wait_recv()
        for cp in sends:
            cp.wait_send()

    out = pl.pallas_call(
        body, name=name,
        in_specs=[pl.BlockSpec(memory_space=pltpu.HBM)],
        out_specs=pl.BlockSpec(memory_space=pltpu.HBM),
        out_shape=jax.ShapeDtypeStruct(parts.shape, parts.dtype),
        scratch_shapes=[pltpu.SemaphoreType.DMA((3,)), pltpu.SemaphoreType.DMA((3,))],
        compiler_params=pltpu.CompilerParams(has_side_effects=True),
    )(parts)
    me = _my_chip()
    return lax.dynamic_update_index_in_dim(out, lax.dynamic_index_in_dim(parts, me, 0, keepdims=False), me, 0)


def _my_chip():
    return 2 * lax.axis_index("x") + lax.axis_index("y")


def _half_rows(rows):
    c = lax.axis_index("c")
    half = rows // 2
    return pl.ds(pl.multiple_of(c * half, 8), half), pl.ds(pl.multiple_of((1 - c) * half, 8), half)


def _sibling():
    return (lax.axis_index("x"), lax.axis_index("y"), 1 - lax.axis_index("c"))


def _pair_exchange(parts, *, name):
    n, rows, cols = parts.shape

    def body(p_ref, theirs_ref, send_sem, recv_sem):
        _, sib_rows = _half_rows(rows)
        cp = pltpu.make_async_remote_copy(src_ref=p_ref.at[:, sib_rows, :], dst_ref=theirs_ref, send_sem=send_sem,
                                          recv_sem=recv_sem, device_id=_sibling(), device_id_type=MESH)
        cp.start()
        cp.wait()

    half = rows // 2
    theirs = pl.pallas_call(
        body, name=name,
        in_specs=[pl.BlockSpec(memory_space=pltpu.HBM)],
        out_specs=pl.BlockSpec(memory_space=pltpu.HBM),
        out_shape=jax.ShapeDtypeStruct((n, half, cols), parts.dtype),
        scratch_shapes=[pltpu.SemaphoreType.DMA, pltpu.SemaphoreType.DMA],
        compiler_params=pltpu.CompilerParams(has_side_effects=True),
    )(parts)
    mine = lax.dynamic_slice_in_dim(parts, lax.axis_index("c") * half, half, axis=1)
    return mine, theirs


def _pair_sum(mine, theirs, *, name):
    n, rows, cols = mine.shape

    def body(a_ref, b_ref, o_ref):
        o_ref[...] = (a_ref[...].astype(F32) + b_ref[...].astype(F32)).astype(o_ref.dtype)

    blk = pl.BlockSpec((n, FLAT_ROW_BLOCK, cols), lambda i: (0, i, 0))
    return pl.pallas_call(
        body, name=name, grid=(rows // FLAT_ROW_BLOCK,),
        in_specs=[blk, blk], out_specs=blk, out_shape=jax.ShapeDtypeStruct(mine.shape, mine.dtype),
        compiler_params=pltpu.CompilerParams(dimension_semantics=("parallel",), vmem_limit_bytes=VMEM_LIMIT),
    )(mine, theirs)


def _sum_chips(parts, *, name):
    _, rows, cols = parts.shape

    def body(p_ref, o_ref):
        o_ref[...] = ((p_ref[0].astype(F32) + p_ref[1].astype(F32)) + p_ref[2].astype(F32)) + p_ref[3].astype(F32)

    return pl.pallas_call(
        body, name=name, grid=(rows // FLAT_ROW_BLOCK,),
        in_specs=[pl.BlockSpec((N_CHIPS, FLAT_ROW_BLOCK, cols), lambda i: (0, i, 0))],
        out_specs=pl.BlockSpec((FLAT_ROW_BLOCK, cols), lambda i: (i, 0)),
        out_shape=jax.ShapeDtypeStruct((rows, cols), F32),
        compiler_params=pltpu.CompilerParams(dimension_semantics=("parallel",), vmem_limit_bytes=VMEM_LIMIT),
    )(parts)


def _join_cores(half, *, name):
    rows2, cols = half.shape

    def body(h_ref, out_ref, send_sem, recv_sem):
        my_rows, sib_rows = _half_rows(2 * rows2)
        cp = pltpu.make_async_remote_copy(src_ref=h_ref, dst_ref=out_ref.at[my_rows, :], send_sem=send_sem,
                                          recv_sem=recv_sem, device_id=_sibling(), device_id_type=MESH)
        cp.start()
        cp.wait_send()
        pltpu.make_async_remote_copy(src_ref=h_ref, dst_ref=out_ref.at[sib_rows, :], send_sem=send_sem,
                                     recv_sem=recv_sem, device_id=_sibling(), device_id_type=MESH).wait_recv()

    out = pl.pallas_call(
        body, name=name,
        in_specs=[pl.BlockSpec(memory_space=pltpu.HBM)],
        out_specs=pl.BlockSpec(memory_space=pltpu.HBM),
        out_shape=jax.ShapeDtypeStruct((2 * rows2, cols), half.dtype),
        scratch_shapes=[pltpu.SemaphoreType.DMA, pltpu.SemaphoreType.DMA],
        compiler_params=pltpu.CompilerParams(has_side_effects=True),
    )(half)
    return lax.dynamic_update_slice_in_dim(out, half, lax.axis_index("c") * rows2, axis=0)


def _all_reduce_small(v, *, name):
    rows, cols = v.shape
    flips = [(fx, fy, fc) for fx in (0, 1) for fy in (0, 1) for fc in (0, 1)][1:]

    def body(v_ref, out_ref, gath_ref, send_sems, recv_sems):
        x, y, c = lax.axis_index("x"), lax.axis_index("y"), lax.axis_index("c")
        me = 4 * x + 2 * y + c
        gath_ref[me] = v_ref[...]
        peers = [((1 - x) if fx else x, (1 - y) if fy else y, (1 - c) if fc else c) for fx, fy, fc in flips]
        sends = []
        for k, peer in enumerate(peers):
            cp = pltpu.make_async_remote_copy(src_ref=v_ref, dst_ref=gath_ref.at[me], send_sem=send_sems.at[k],
                                              recv_sem=recv_sems.at[k], device_id=peer, device_id_type=MESH)
            cp.start()
            sends.append(cp)
        for k, (px, py, pc) in enumerate(peers):
            pltpu.make_async_remote_copy(src_ref=v_ref, dst_ref=gath_ref.at[4 * px + 2 * py + pc],
                                         send_sem=send_sems.at[k], recv_sem=recv_sems.at[k],
                                         device_id=(px, py, pc), device_id_type=MESH).wait_recv()
        for cp in sends:
            cp.wait_send()
        total = gath_ref[0]
        for k in range(1, 8):
            total = total + gath_ref[k]
        out_ref[...] = total

    total, _ = pl.pallas_call(
        body, name=name,
        in_specs=[pl.BlockSpec(memory_space=pltpu.VMEM)],
        out_specs=[pl.BlockSpec(memory_space=pltpu.VMEM), pl.BlockSpec(memory_space=pltpu.VMEM)],
        out_shape=[jax.ShapeDtypeStruct((rows, cols), v.dtype), jax.ShapeDtypeStruct((8, rows, cols), v.dtype)],
        scratch_shapes=[pltpu.SemaphoreType.DMA((7,)), pltpu.SemaphoreType.DMA((7,))],
        compiler_params=pltpu.CompilerParams(has_side_effects=True),
    )(v)
    return total


def _adamw(w, g_parts, m, v, *, name):
    rows, cols = w.shape
    br = min(FLAT_ROW_BLOCK, rows)
    n_parts = len(g_parts)

    def body(*refs):
        w_ref = refs[0]
        g_refs = refs[1:1 + n_parts]
        m_ref, v_ref = refs[1 + n_parts], refs[2 + n_parts]
        g_out, d_out, m_out, v_out = refs[-4:]
        g = g_refs[0][...]
        for r in g_refs[1:]:
            g = g + r[...]
        m_new = ADAM_B1 * m_ref[...] + (1.0 - ADAM_B1) * g
        v_new = ADAM_B2 * v_ref[...] + (1.0 - ADAM_B2) * jnp.square(g)
        m_hat = m_new / (1.0 - ADAM_B1 ** ADAM_STEP)
        v_hat = v_new / (1.0 - ADAM_B2 ** ADAM_STEP)
        g_out[...] = g
        d_out[...] = -ADAM_LR * (m_hat / (jnp.sqrt(v_hat) + ADAM_EPS) + ADAM_WD * w_ref[...])
        m_out[...] = m_new
        v_out[...] = v_new

    blk = pl.BlockSpec((br, cols), lambda i: (i, 0))
    shape = jax.ShapeDtypeStruct((rows, cols), F32)
    return pl.pallas_call(
        body, name=name, grid=(rows // br,),
        in_specs=[blk] * (3 + n_parts), out_specs=[blk] * 4, out_shape=[shape] * 4,
        compiler_params=pltpu.CompilerParams(dimension_semantics=("parallel",), vmem_limit_bytes=VMEM_LIMIT),
    )(w, *g_parts, m, v)


def _assemble(gathered_shards, name):
    return jnp.concatenate(gathered_shards, axis=SHARD_AXIS[name])


def _chip_shard(full, name, j):
    if isinstance(full, list):
        axis = SHARD_AXIS[name] - 1
        layers = full
    else:
        axis = SHARD_AXIS[name]
        layers = [full]
    n = layers[0].shape[axis] // N_CHIPS
    return [lax.slice_in_dim(g, j * n, (j + 1) * n, axis=axis) for g in layers]


def kernel(x, positions, attn_norm, mlp_norm, sb_w_qkv, sb_w_o, kv_norm, mla_w_dkv, mla_kv_lat_norm, mla_w_ukv, mla_w_dq, mla_q_lat_norm, mla_w_uq, mla_w_o, mlp_w1, mlp_w2, final_norm, loss_target, m_attn_norm, m_mlp_norm, m_sb_w_qkv, m_sb_w_o, m_kv_norm, m_mla_w_dkv, m_mla_kv_lat_norm, m_mla_w_ukv, m_mla_w_dq, m_mla_q_lat_norm, m_mla_w_uq, m_mla_w_o, m_mlp_w1, m_mlp_w2, m_final_norm, v_attn_norm, v_mlp_norm, v_sb_w_qkv, v_sb_w_o, v_kv_norm, v_mla_w_dkv, v_mla_kv_lat_norm, v_mla_w_ukv, v_mla_w_dq, v_mla_q_lat_norm, v_mla_w_uq, v_mla_w_o, v_mlp_w1, v_mlp_w2, v_final_norm):
    weights = dict(attn_norm=attn_norm, mlp_norm=mlp_norm, sb_w_qkv=sb_w_qkv, sb_w_o=sb_w_o, kv_norm=kv_norm,
                   mla_w_dkv=mla_w_dkv, mla_kv_lat_norm=mla_kv_lat_norm, mla_w_ukv=mla_w_ukv, mla_w_dq=mla_w_dq,
                   mla_q_lat_norm=mla_q_lat_norm, mla_w_uq=mla_w_uq, mla_w_o=mla_w_o, mlp_w1=mlp_w1, mlp_w2=mlp_w2,
                   final_norm=final_norm)
    m_in = dict(attn_norm=m_attn_norm, mlp_norm=m_mlp_norm, sb_w_qkv=m_sb_w_qkv, sb_w_o=m_sb_w_o, kv_norm=m_kv_norm,
                mla_w_dkv=m_mla_w_dkv, mla_kv_lat_norm=m_mla_kv_lat_norm, mla_w_ukv=m_mla_w_ukv, mla_w_dq=m_mla_w_dq,
                mla_q_lat_norm=m_mla_q_lat_norm, mla_w_uq=m_mla_w_uq, mla_w_o=m_mla_w_o, mlp_w1=m_mlp_w1,
                mlp_w2=m_mlp_w2, final_norm=m_final_norm)
    v_in = dict(attn_norm=v_attn_norm, mlp_norm=v_mlp_norm, sb_w_qkv=v_sb_w_qkv, sb_w_o=v_sb_w_o, kv_norm=v_kv_norm,
                mla_w_dkv=v_mla_w_dkv, mla_kv_lat_norm=v_mla_kv_lat_norm, mla_w_ukv=v_mla_w_ukv, mla_w_dq=v_mla_w_dq,
                mla_q_lat_norm=v_mla_q_lat_norm, mla_w_uq=v_mla_w_uq, mla_w_o=v_mla_w_o, mlp_w1=v_mlp_w1,
                mlp_w2=v_mlp_w2, final_norm=v_final_norm)
    shard_shapes = [weights[n].shape for n in BIG_WEIGHTS]
    small_shapes = [weights[n].shape for n in SMALL_WEIGHTS]

    flat_w = _pack([weights[n] for n in BIG_WEIGHTS], BF16)
    gathered = _all_gather_chips(flat_w, name="weights_all_gather")
    per_chip = [_unpack(jnp.where(_my_chip() == j, flat_w, gathered[j]), shard_shapes) for j in range(N_CHIPS)]
    full_w = {n: _assemble([per_chip[j][i] for j in range(N_CHIPS)], n) for i, n in enumerate(BIG_WEIGHTS)}
    norms = {n: weights[n] for n in SMALL_WEIGHTS}

    loss, dx, grads = _local_step(x[0], positions[0], loss_target[0], full_w, norms)
    loss = lax.psum(loss, ("x", "y", "c"))

    parts = jnp.stack([_pack([piece for n in BIG_WEIGHTS for piece in _chip_shard(grads[n], n, j)], BF16)
                       for j in range(N_CHIPS)])
    mine, theirs = _pair_exchange(parts, name="grads_pair_exchange")
    chip_part = _pair_sum(mine, theirs, name="grads_pair_sum")
    received = _exchange_chips(chip_part, name="grads_exchange")
    g_half = _sum_chips(received, name="grads_sum_chips")
    g_sum = _join_cores(g_half, name="grads_join_cores")
    g_flat, d_flat, m_flat, v_flat = _adamw(
        _pack([weights[n] for n in BIG_WEIGHTS], F32), [g_sum],
        _pack([m_in[n] for n in BIG_WEIGHTS], F32), _pack([v_in[n] for n in BIG_WEIGHTS], F32), name="adamw_big")
    out_g = dict(zip(BIG_WEIGHTS, _unpack(g_flat, shard_shapes)))
    out_d = dict(zip(BIG_WEIGHTS, _unpack(d_flat, shard_shapes)))
    out_m = dict(zip(BIG_WEIGHTS, _unpack(m_flat, shard_shapes)))
    out_v = dict(zip(BIG_WEIGHTS, _unpack(v_flat, shard_shapes)))

    small_sum = _all_reduce_small(_pack_small([grads[n] for n in SMALL_WEIGHTS]), name="gains_all_reduce")
    sg, sd, sm, sv = _adamw(_pack_small([weights[n] for n in SMALL_WEIGHTS]), [small_sum],
                            _pack_small([m_in[n] for n in SMALL_WEIGHTS]),
                            _pack_small([v_in[n] for n in SMALL_WEIGHTS]), name="adamw_gains")
    out_g.update(zip(SMALL_WEIGHTS, _unpack_small(sg, small_shapes)))
    out_d.update(zip(SMALL_WEIGHTS, _unpack_small(sd, small_shapes)))
    out_m.update(zip(SMALL_WEIGHTS, _unpack_small(sm, small_shapes)))
    out_v.update(zip(SMALL_WEIGHTS, _unpack_small(sv, small_shapes)))

    return (loss, dx[None], *[out_g[n] for n in ALL_WEIGHTS], *[out_d[n] for n in ALL_WEIGHTS],
            *[out_m[n] for n in ALL_WEIGHTS], *[out_v[n] for n in ALL_WEIGHTS])
```

```python
import functools

import jax
import jax.numpy as jnp
from jax import lax
from jax.experimental import pallas as pl
from jax.experimental.pallas import tpu as pltpu

F32 = jnp.float32
BF16 = jnp.bfloat16

LANES = 128
SB_HEAD_DIM = 64
MLA_NOPE = 64
MLA_ROPE = 32
MLA_V = 64
MLA_Q_RANK = 384
MLA_KV_RANK = 256
CHUNK = 64
ROPE_THETA = 10000.0
NORM_EPS = 1e-6
SB_SCALE = SB_HEAD_DIM ** -0.5
MLA_SCALE = (MLA_NOPE + MLA_ROPE) ** -0.5
ROPE_LO = MLA_NOPE
ROPE_HALF = MLA_ROPE // 2
ATT_Q_BLOCK = 1024
ATT_K_BLOCK = 256
MLA_FWD_K_BLOCK = 512
NEG_BIG = -1e30
SB_DEAD_LOG = -110.0
VMEM_LIMIT = 56 * 1024 * 1024

ADAM_LR = 0.001
ADAM_B1 = 0.9
ADAM_B2 = 0.999
ADAM_EPS = 1e-08
ADAM_WD = 0.01
ADAM_STEP = 10

FLAT_COLS = 1024
FLAT_ROW_BLOCK = 256
N_CHIPS = 4
MESH = pl.DeviceIdType.MESH

BIG_WEIGHTS = ["sb_w_qkv", "sb_w_o", "mla_w_dkv", "mla_w_ukv", "mla_w_dq", "mla_w_uq", "mla_w_o", "mlp_w1", "mlp_w2"]
SHARD_AXIS = {"sb_w_qkv": 2, "sb_w_o": 1, "mla_w_dkv": 0, "mla_w_ukv": 1, "mla_w_dq": 1, "mla_w_uq": 2,
              "mla_w_o": 1, "mlp_w1": 2, "mlp_w2": 1}
SMALL_WEIGHTS = ["attn_norm", "mlp_norm", "kv_norm", "mla_kv_lat_norm", "mla_q_lat_norm", "final_norm"]
ALL_WEIGHTS = ["attn_norm", "mlp_norm", "sb_w_qkv", "sb_w_o", "kv_norm", "mla_w_dkv", "mla_kv_lat_norm", "mla_w_ukv",
               "mla_w_dq", "mla_q_lat_norm", "mla_w_uq", "mla_w_o", "mlp_w1", "mlp_w2", "final_norm"]


def _dot(a, b, dims):
    return lax.dot_general(a, b, (dims, ((), ())), preferred_element_type=F32)


def _dot_nn(a, b):
    return _dot(a, b, ((1,), (0,)))


def _dot_nt(a, b):
    return _dot(a, b, ((1,), (1,)))


def _dot_tn(a, b):
    return _dot(a, b, ((0,), (0,)))


def _pick_block(n, target):
    if n <= target:
        return n
    best = max(b for b in range(LANES, target + 1, LANES) if n % b == 0)
    return best


MM_ROWS = 512
MM_COLS = 1024
MM_DEPTH = 4096
MM_DEPTH_TN = 1024


def _mm(a, b, *, name, dims="nn", epilogue=None, extras=(), out_dtypes=(BF16,)):
    if dims == "nn":
        (m, k), (k2, n) = a.shape, b.shape
    elif dims == "nt":
        (m, k), (n, k2) = a.shape, b.shape
    else:
        (k, m), (k2, n) = a.shape, b.shape
    assert k == k2, (name, a.shape, b.shape)
    if dims == "tn":
        bm, bn, bk = _pick_block(m, MM_COLS), _pick_block(n, MM_COLS), _pick_block(k, MM_DEPTH_TN)
    else:
        rows = MM_ROWS if k > MM_DEPTH // 2 else 2 * MM_ROWS
        bm, bn, bk = _pick_block(m, rows), _pick_block(n, MM_COLS), _pick_block(k, MM_DEPTH)
    nk = k // bk
    if dims == "tn":
        a_spec = pl.BlockSpec((bk, bm), lambda j, i, kk: (kk, i))
    else:
        a_spec = pl.BlockSpec((bm, bk), lambda j, i, kk: (i, kk))
    if dims == "nt":
        b_spec = pl.BlockSpec((bn, bk), lambda j, i, kk: (j, kk))
    else:
        b_spec = pl.BlockSpec((bk, bn), lambda j, i, kk: (kk, j))
    extra_specs = []
    for arr, kind in extras:
        if kind == "tile":
            assert arr.shape == (m, n), (name, arr.shape)
            extra_specs.append(pl.BlockSpec((bm, bn), lambda j, i, kk: (i, j)))
        else:
            assert arr.shape == (m, LANES), (name, arr.shape)
            extra_specs.append(pl.BlockSpec((bm, LANES), lambda j, i, kk: (i, 0)))
    n_extra = len(extras)
    n_out = len(out_dtypes)
    dot = {"nn": _dot_nn, "nt": _dot_nt, "tn": _dot_tn}[dims]

    def body(*refs):
        a_ref, b_ref = refs[0], refs[1]
        extra_refs = refs[2:2 + n_extra]
        out_refs = refs[2 + n_extra:2 + n_extra + n_out]

        def finish(acc):
            outs = (acc,) if epilogue is None else epilogue(acc, *[r[...] for r in extra_refs])
            for o_ref, o in zip(out_refs, outs):
                o_ref[...] = o.astype(o_ref.dtype)

        part = dot(a_ref[...].astype(BF16), b_ref[...].astype(BF16))
        if nk == 1:
            finish(part)
            return
        acc_ref = refs[-1]
        kk = pl.program_id(2)

        @pl.when(kk == 0)
        def _():
            acc_ref[...] = part

        @pl.when(kk > 0)
        def _():
            acc_ref[...] += part

        @pl.when(kk == nk - 1)
        def _():
            finish(acc_ref[...])

    outs = pl.pallas_call(
        body, name=name, grid=(n // bn, m // bm, nk),
        in_specs=[a_spec, b_spec] + extra_specs,
        out_specs=[pl.BlockSpec((bm, bn), lambda j, i, kk: (i, j)) for _ in range(n_out)],
        out_shape=[jax.ShapeDtypeStruct((m, n), dt) for dt in out_dtypes],
        scratch_shapes=[pltpu.VMEM((bm, bn), F32)] if nk > 1 else [],
        compiler_params=pltpu.CompilerParams(dimension_semantics=("parallel", "parallel", "arbitrary"),
                                             vmem_limit_bytes=VMEM_LIMIT),
    )(a, b, *[arr for arr, _ in extras])
    return outs[0] if n_out == 1 else outs


def _epi_add(acc, res):
    return (res + acc,)


def _epi_relu2(acc):
    r = jnp.maximum(acc, 0.0)
    return acc, r * r


def _epi_relu2_grad(acc, u):
    return (acc * (2.0 * jnp.maximum(u.astype(F32), 0.0)),)


def _rope_slab(t, cos_t, sin_t):
    lane = lax.broadcasted_iota(jnp.int32, t.shape, 1)
    partner = jnp.where(lane < ROPE_LO + ROPE_HALF, pltpu.roll(t, LANES - ROPE_HALF, 1), pltpu.roll(t, ROPE_HALF, 1))
    return t * cos_t + partner * sin_t


def _rope_slab_bwd(d, cos_t, sin_t):
    ds = d * sin_t
    lane = lax.broadcasted_iota(jnp.int32, d.shape, 1)
    partner = jnp.where(lane < ROPE_LO + ROPE_HALF, pltpu.roll(ds, LANES - ROPE_HALF, 1), pltpu.roll(ds, ROPE_HALF, 1))
    in_rope = (lane >= ROPE_LO) & (lane < ROPE_LO + MLA_ROPE)
    return d * cos_t + jnp.where(in_rope, partner, 0.0)


def _epi_rope_heads(acc, cos_t, sin_t):
    slabs = [_rope_slab(acc[:, j * LANES:(j + 1) * LANES], cos_t, sin_t) for j in range(acc.shape[1] // LANES)]
    return (jnp.concatenate(slabs, axis=1) * MLA_SCALE,)


def _row_block(s):
    return min(512, s)


def _rms_fwd(x, g, *, name):
    s, d = x.shape
    bm = _row_block(s)

    def body(x_ref, g_ref, o_ref):
        xv = x_ref[...]
        r = lax.rsqrt(jnp.mean(xv * xv, axis=-1, keepdims=True) + NORM_EPS)
        o_ref[...] = ((xv * r) * g_ref[...]).astype(o_ref.dtype)

    return pl.pallas_call(
        body, name=name, grid=(s // bm,),
        in_specs=[pl.BlockSpec((bm, d), lambda i: (i, 0)), pl.BlockSpec((1, d), lambda i: (0, 0))],
        out_specs=pl.BlockSpec((bm, d), lambda i: (i, 0)),
        out_shape=jax.ShapeDtypeStruct((s, d), BF16),
        compiler_params=pltpu.CompilerParams(dimension_semantics=("parallel",), vmem_limit_bytes=VMEM_LIMIT),
    )(x, g.reshape(1, d))


def _rms_bwd_math(xv, gv, dy):
    r = lax.rsqrt(jnp.mean(xv * xv, axis=-1, keepdims=True) + NORM_EPS)
    xhat = xv * r
    dyg = dy * gv
    mdot = jnp.mean(dyg * xhat, axis=-1, keepdims=True)
    dx = r * (dyg - xhat * mdot)
    dg = jnp.sum(dy * xhat, axis=0, keepdims=True)
    return dx, dg


def _rms_bwd(x, g, dy, dres, *, name):
    s, d = x.shape
    bm = _row_block(s)
    has_res = dres is not None

    def body(*refs):
        x_ref, g_ref, dy_ref = refs[:3]
        dres_ref = refs[3] if has_res else None
        dx_ref, dxb_ref, dg_ref = refs[-3:]
        dx, dg = _rms_bwd_math(x_ref[...], g_ref[...], dy_ref[...].astype(F32))
        if has_res:
            dx = dx + dres_ref[...]
        dx_ref[...] = dx
        dxb_ref[...] = dx.astype(BF16)

        @pl.when(pl.program_id(0) == 0)
        def _():
            dg_ref[...] = jnp.zeros_like(dg_ref)

        dg_ref[...] += dg

    row = pl.BlockSpec((bm, d), lambda i: (i, 0))
    vec = pl.BlockSpec((1, d), lambda i: (0, 0))
    ins = [x, g.reshape(1, d), dy] + ([dres] if has_res else [])
    return pl.pallas_call(
        body, name=name, grid=(s // bm,),
        in_specs=[row, vec, row] + ([row] if has_res else []),
        out_specs=[row, row, vec],
        out_shape=[jax.ShapeDtypeStruct((s, d), F32), jax.ShapeDtypeStruct((s, d), BF16),
                   jax.ShapeDtypeStruct((1, d), F32)],
        compiler_params=pltpu.CompilerParams(dimension_semantics=("arbitrary",), vmem_limit_bytes=VMEM_LIMIT),
    )(*ins)


def _loss_bwd(x, g, target, *, name):
    s, d = x.shape
    bm = _row_block(s)

    def body(x_ref, g_ref, t_ref, loss_ref, dx_ref, dxb_ref, dg_ref):
        xv, gv = x_ref[...], g_ref[...]
        r = lax.rsqrt(jnp.mean(xv * xv, axis=-1, keepdims=True) + NORM_EPS)
        err = (xv * r) * gv - t_ref[...]
        dx, dg = _rms_bwd_math(xv, gv, err * (1.0 / d))
        dx_ref[...] = dx
        dxb_ref[...] = dx.astype(BF16)

        @pl.when(pl.program_id(0) == 0)
        def _():
            dg_ref[...] = jnp.zeros_like(dg_ref)
            loss_ref[...] = jnp.zeros_like(loss_ref)

        dg_ref[...] += dg
        loss_ref[...] += jnp.sum(jnp.mean(err * err, axis=-1, keepdims=True), axis=0, keepdims=True) * 0.5

    row = pl.BlockSpec((bm, d), lambda i: (i, 0))
    vec = pl.BlockSpec((1, d), lambda i: (0, 0))
    return pl.pallas_call(
        body, name=name, grid=(s // bm,),
        in_specs=[row, vec, row],
        out_specs=[pl.BlockSpec((8, LANES), lambda i: (0, 0)), row, row, vec],
        out_shape=[jax.ShapeDtypeStruct((8, LANES), F32), jax.ShapeDtypeStruct((s, d), F32),
                   jax.ShapeDtypeStruct((s, d), BF16), jax.ShapeDtypeStruct((1, d), F32)],
        compiler_params=pltpu.CompilerParams(dimension_semantics=("arbitrary",), vmem_limit_bytes=VMEM_LIMIT),
    )(x, g.reshape(1, d), target)


def _kv_prep(down, g, cos_t, sin_t, *, name):
    s, w = down.shape
    bm = _row_block(s)

    def body(d_ref, g_ref, c_ref, s_ref, o_ref):
        lat = d_ref[:, :MLA_KV_RANK]
        r = lax.rsqrt(jnp.mean(lat * lat, axis=-1, keepdims=True) + NORM_EPS)
        o_ref[:, :MLA_KV_RANK] = ((lat * r) * g_ref[...]).astype(BF16)
        o_ref[:, MLA_KV_RANK:] = _rope_slab(d_ref[:, MLA_KV_RANK:], c_ref[...], s_ref[...]).astype(BF16)

    row = pl.BlockSpec((bm, w), lambda i: (i, 0))
    tab = pl.BlockSpec((bm, LANES), lambda i: (i, 0))
    return pl.pallas_call(
        body, name=name, grid=(s // bm,),
        in_specs=[row, pl.BlockSpec((1, MLA_KV_RANK), lambda i: (0, 0)), tab, tab],
        out_specs=row, out_shape=jax.ShapeDtypeStruct((s, w), BF16),
        compiler_params=pltpu.CompilerParams(dimension_semantics=("parallel",), vmem_limit_bytes=VMEM_LIMIT),
    )(down, g.reshape(1, MLA_KV_RANK), cos_t, sin_t)


def _kv_prep_bwd(down, g, cos_t, sin_t, dcat, *, name):
    s, w = down.shape
    bm = _row_block(s)

    def body(d_ref, g_ref, c_ref, s_ref, dc_ref, o_ref, dg_ref):
        dlat, dg = _rms_bwd_math(d_ref[:, :MLA_KV_RANK], g_ref[...], dc_ref[:, :MLA_KV_RANK])
        o_ref[:, :MLA_KV_RANK] = dlat.astype(BF16)
        o_ref[:, MLA_KV_RANK:] = _rope_slab_bwd(dc_ref[:, MLA_KV_RANK:], c_ref[...], s_ref[...]).astype(BF16)

        @pl.when(pl.program_id(0) == 0)
        def _():
            dg_ref[...] = jnp.zeros_like(dg_ref)

        dg_ref[...] += dg

    row = pl.BlockSpec((bm, w), lambda i: (i, 0))
    tab = pl.BlockSpec((bm, LANES), lambda i: (i, 0))
    vec = pl.BlockSpec((1, MLA_KV_RANK), lambda i: (0, 0))
    return pl.pallas_call(
        body, name=name, grid=(s // bm,),
        in_specs=[row, vec, tab, tab, row],
        out_specs=[row, vec],
        out_shape=[jax.ShapeDtypeStruct((s, w), BF16), jax.ShapeDtypeStruct((1, MLA_KV_RANK), F32)],
        compiler_params=pltpu.CompilerParams(dimension_semantics=("arbitrary",), vmem_limit_bytes=VMEM_LIMIT),
    )(down, g.reshape(1, MLA_KV_RANK), cos_t, sin_t, dcat)


def _split_bf16(v):
    hi = v.astype(BF16)
    lo = (v - hi.astype(F32)).astype(BF16)
    return hi, lo


def _suffix_matrices(n):
    row = lax.broadcasted_iota(jnp.int32, (n, n), 0)
    col = lax.broadcasted_iota(jnp.int32, (n, n), 1)
    incl = (row >= col).astype(BF16)
    return (row > col).astype(BF16), jnp.concatenate([incl, incl], axis=0)


def _suffix_sum(v, matrix):
    hi, lo = _split_bf16(v)
    return _dot_nn(jnp.concatenate([hi, lo], axis=1), matrix)


def _block_positions(qi, kb, bq, bk, r0, r1):
    row = qi * bq + r0 + lax.broadcasted_iota(jnp.int32, (r1 - r0, bk), 0)
    col = kb * bk + lax.broadcasted_iota(jnp.int32, (r1 - r0, bk), 1)
    return row, col


def _att_blocks(s, key_block=ATT_K_BLOCK):
    bq, bk = min(ATT_Q_BLOCK, s), min(key_block, s)
    return bq, bk, s // bq, bq // bk


def _sweep(qi, ratio, bk, step, unroll=2, alive=None):
    bq = ratio * bk
    for d in range(ratio):
        kb, r0 = (qi + 1) * ratio - 1 - d, (ratio - 1 - d) * bk
        near = bq if alive is None else min(r0 + 2 * bk, bq)
        step(kb, True, r0, near)
        if near < bq:
            pl.when(alive(near))(functools.partial(step, kb, False, near, bq))
    unroll = unroll if ratio % unroll == 0 else 1
    trips = qi * (ratio // unroll)

    def trip(i):
        for u in range(unroll):
            step(qi * ratio - 1 - (i * unroll + u), False, 0, bq)

    if alive is None:
        lax.fori_loop(0, trips, lambda i, carry: (trip(i), carry)[1], 0)
    else:
        lax.while_loop(lambda i: jnp.logical_and(i < trips, alive(0)), lambda i: (trip(i), i + 1)[1], 0)


def _stick_left(c_ref, r0):
    return jnp.max(c_ref[r0:, :]) > SB_DEAD_LOG


def _sb_logs(q, k):
    z = _dot_nt(q, k)
    lb = jnp.minimum(z, 0.0) - jnp.log(1.0 + jnp.exp(-jnp.abs(z)))
    return lb, lb - z


def _sb_fwd(qkv, heads, *, name):
    s = qkv.shape[0]
    bq, bk, nq, ratio = _att_blocks(s)

    def body(q_ref, k_ref, v_ref, o_ref, acc_ref, c_ref):
        qi = pl.program_id(1)
        q = q_ref[...] * SB_SCALE
        m_strict, _ = _suffix_matrices(bk)
        acc_ref[...] = jnp.zeros_like(acc_ref)
        c_ref[...] = jnp.zeros_like(c_ref)

        def step(kb, masked, r0, r1):
            rows = pl.ds(pl.multiple_of(kb * bk, bk), bk)
            mine = pl.ds(r0, r1 - r0)
            k, v = k_ref[rows, :], v_ref[rows, :]
            lb, lk = _sb_logs(q[r0:r1], k)
            if masked:
                row, col = _block_positions(qi, kb, bq, bk, r0, r1)
                causal = col < row
                lk = jnp.where(causal, lk, 0.0)
            c = c_ref[mine, :]
            w = jnp.exp(lb + _dot_nn(lk.astype(BF16), m_strict) + jnp.tile(c, (1, bk // LANES)))
            if masked:
                w = jnp.where(causal, w, 0.0)
            acc_ref[mine, :] += _dot_nn(w.astype(BF16), v)
            c_ref[mine, :] = c + jnp.sum(lk, axis=-1, keepdims=True)

        _sweep(qi, ratio, bk, step, unroll=1, alive=functools.partial(_stick_left, c_ref))
        o_ref[...] = acc_ref[...].astype(o_ref.dtype)

    return pl.pallas_call(
        body, name=name, grid=(heads, nq),
        in_specs=[pl.BlockSpec((bq, LANES), lambda h, i: (i, h)),
                  pl.BlockSpec((s, LANES), lambda h, i: (0, heads + h)),
                  pl.BlockSpec((s, LANES), lambda h, i: (0, 2 * heads + h))],
        out_specs=pl.BlockSpec((bq, LANES), lambda h, i: (i, h)),
        out_shape=jax.ShapeDtypeStruct((s, heads * LANES), F32),
        scratch_shapes=[pltpu.VMEM((bq, LANES), F32), pltpu.VMEM((bq, LANES), F32)],
        compiler_params=pltpu.CompilerParams(dimension_semantics=("parallel", "arbitrary"),
                                             vmem_limit_bytes=VMEM_LIMIT),
    )(qkv, qkv, qkv)


def _sb_bwd(qkv, o, do, heads, *, name):
    s = qkv.shape[0]
    bq, bk, nq, ratio = _att_blocks(s)

    def body(q_ref, k_ref, v_ref, o_ref, do_ref, dq_ref, dk_ref, dv_ref, dq_acc, dk_acc, dv_acc, c_ref, e_ref):
        qi = pl.program_id(1)

        @pl.when(qi == 0)
        def _():
            dk_acc[...] = jnp.zeros_like(dk_acc)
            dv_acc[...] = jnp.zeros_like(dv_acc)

        q = q_ref[...] * SB_SCALE
        do = do_ref[...]
        total = jnp.sum(do.astype(F32) * o_ref[...].astype(F32), axis=-1, keepdims=True)
        m_strict, m_incl = _suffix_matrices(bk)
        dq_acc[...] = jnp.zeros_like(dq_acc)
        c_ref[...] = jnp.zeros_like(c_ref)
        e_ref[...] = jnp.broadcast_to(total, e_ref.shape)
        reps = (1, bk // LANES)

        def step(kb, masked, r0, r1):
            rows = pl.ds(pl.multiple_of(kb * bk, bk), bk)
            mine = pl.ds(r0, r1 - r0)
            k, v = k_ref[rows, :], v_ref[rows, :]
            qs, dos = q[r0:r1], do[r0:r1]
            lb, lk_all = _sb_logs(qs, k)
            lk = lk_all
            if masked:
                row, col = _block_positions(qi, kb, bq, bk, r0, r1)
                causal = col < row
                lk = jnp.where(causal, lk_all, 0.0)
            c = c_ref[mine, :]
            w = jnp.exp(lb + _dot_nn(lk.astype(BF16), m_strict) + jnp.tile(c, reps))
            if masked:
                w = jnp.where(causal, w, 0.0)
            wb = w.astype(BF16)
            g = wb.astype(F32) * _dot_nt(dos, v)
            e = e_ref[mine, :]
            g_left = jnp.tile(e, reps) - _suffix_sum(g, m_incl)
            da = g * jnp.exp(lk_all) - jnp.exp(lb) * g_left
            if masked:
                da = jnp.where(causal, da, 0.0)
            dab = da.astype(BF16)
            dq_acc[mine, :] += _dot_nn(dab, k)
            dk_acc[rows, :] += _dot_tn(dab, qs)
            dv_acc[rows, :] += _dot_tn(wb, dos)
            e_ref[mine, :] = e - jnp.sum(g, axis=-1, keepdims=True)
            c_ref[mine, :] = c + jnp.sum(lk, axis=-1, keepdims=True)

        _sweep(qi, ratio, bk, step, unroll=1, alive=functools.partial(_stick_left, c_ref))
        dq_ref[...] = (dq_acc[...] * SB_SCALE).astype(dq_ref.dtype)

        @pl.when(qi == nq - 1)
        def _():
            dk_ref[...] = dk_acc[...].astype(dk_ref.dtype)
            dv_ref[...] = dv_acc[...].astype(dv_ref.dtype)

    blk = pl.BlockSpec((bq, LANES), lambda h, i: (i, h))
    full = pl.BlockSpec((s, LANES), lambda h, i: (0, h))
    shape = jax.ShapeDtypeStruct((s, heads * LANES), BF16)
    return pl.pallas_call(
        body, name=name, grid=(heads, nq),
        in_specs=[blk,
                  pl.BlockSpec((s, LANES), lambda h, i: (0, heads + h)),
                  pl.BlockSpec((s, LANES), lambda h, i: (0, 2 * heads + h)),
                  blk, blk],
        out_specs=[blk, full, full],
        out_shape=[shape, shape, shape],
        scratch_shapes=[pltpu.VMEM((bq, LANES), F32), pltpu.VMEM((s, LANES), F32), pltpu.VMEM((s, LANES), F32),
                        pltpu.VMEM((bq, LANES), F32), pltpu.VMEM((bq, LANES), F32)],
        compiler_params=pltpu.CompilerParams(dimension_semantics=("arbitrary", "arbitrary"),
                                             vmem_limit_bytes=VMEM_LIMIT),
    )(qkv, qkv, qkv, o, do)


def _chunk_allowed(qi, kb, bq, bk, r0, r1):
    row, col = _block_positions(qi, kb, bq, bk, r0, r1)
    return (col // CHUNK) <= (row // CHUNK)


def _mla_fwd(q, kv, heads, *, name):
    s = q.shape[0]
    bq, bk, nq, ratio = _att_blocks(s, MLA_FWD_K_BLOCK)
    reps = (1, bk // LANES)

    def body(q_ref, k_ref, v_ref, o_ref, lse_ref, acc_ref, m_ref, l_ref):
        qi = pl.program_id(1)
        qv = q_ref[...]
        acc_ref[...] = jnp.zeros_like(acc_ref)
        m_ref[...] = jnp.full_like(m_ref, NEG_BIG)
        l_ref[...] = jnp.zeros_like(l_ref)

        def step(kb, masked, r0, r1):
            rows = pl.ds(pl.multiple_of(kb * bk, bk), bk)
            mine = pl.ds(r0, r1 - r0)
            k, v = k_ref[rows, :], v_ref[rows, :]
            sc = _dot_nt(qv[r0:r1], k)
            if masked:
                allowed = _chunk_allowed(qi, kb, bq, bk, r0, r1)
                sc = jnp.where(allowed, sc, NEG_BIG)
            m_old = m_ref[mine, :]
            m_new = jnp.maximum(m_old, jnp.max(sc, axis=-1, keepdims=True))
            p = jnp.exp(sc - jnp.tile(m_new, reps))
            alpha = jnp.exp(m_old - m_new)
            l_ref[mine, :] = alpha * l_ref[mine, :] + jnp.sum(p, axis=-1, keepdims=True)
            acc_ref[mine, :] = alpha * acc_ref[mine, :] + _dot_nn(p.astype(BF16), v)
            m_ref[mine, :] = m_new

        _sweep(qi, ratio, bk, step)
        o_ref[...] = (acc_ref[...] / l_ref[...]).astype(o_ref.dtype)
        lse_ref[...] = m_ref[...] + jnp.log(l_ref[...])

    blk = pl.BlockSpec((bq, LANES), lambda h, i: (i, h))
    return pl.pallas_call(
        body, name=name, grid=(heads, nq),
        in_specs=[blk,
                  pl.BlockSpec((s, LANES), lambda h, i: (0, h)),
                  pl.BlockSpec((s, LANES), lambda h, i: (0, heads + h))],
        out_specs=[blk, blk],
        out_shape=[jax.ShapeDtypeStruct((s, heads * LANES), BF16), jax.ShapeDtypeStruct((s, heads * LANES), F32)],
        scratch_shapes=[pltpu.VMEM((bq, LANES), F32), pltpu.VMEM((bq, LANES), F32), pltpu.VMEM((bq, LANES), F32)],
        compiler_params=pltpu.CompilerParams(dimension_semantics=("parallel", "arbitrary"),
                                             vmem_limit_bytes=VMEM_LIMIT),
    )(q, kv, kv)


def _mla_bwd(q, kv, o, do, lse, cos_t, sin_t, dkv_init, heads, *, name):
    s = q.shape[0]
    bq, bk, nq, ratio = _att_blocks(s)
    reps = (1, bk // LANES)
    has_init = dkv_init is not None

    def body(*refs):
        q_ref, k_ref, v_ref, o_ref, do_ref, lse_ref, c_ref, s_ref = refs[:8]
        ki_ref, vi_ref = (refs[8], refs[9]) if has_init else (None, None)
        dq_ref, dk_ref, dv_ref, dq_acc, dk_acc, dv_acc = refs[-6:]
        qi = pl.program_id(1)

        @pl.when(qi == 0)
        def _():
            if has_init:
                dk_acc[...] = ki_ref[...].astype(F32)
                dv_acc[...] = vi_ref[...].astype(F32)
            else:
                dk_acc[...] = jnp.zeros_like(dk_acc)
                dv_acc[...] = jnp.zeros_like(dv_acc)

        qv = q_ref[...]
        do = do_ref[...]
        delta = jnp.sum(do.astype(F32) * o_ref[...].astype(F32), axis=-1, keepdims=True)
        lse_wide = jnp.tile(lse_ref[...], reps)
        dq_acc[...] = jnp.zeros_like(dq_acc)

        def step(kb, masked, r0, r1):
            rows = pl.ds(pl.multiple_of(kb * bk, bk), bk)
            k, v = k_ref[rows, :], v_ref[rows, :]
            qs, dos = qv[r0:r1], do[r0:r1]
            p = jnp.exp(_dot_nt(qs, k) - lse_wide[r0:r1])
            if masked:
                p = jnp.where(_chunk_allowed(qi, kb, bq, bk, r0, r1), p, 0.0)
            ds = (p * (_dot_nt(dos, v) - delta[r0:r1])).astype(BF16)
            dq_acc[pl.ds(r0, r1 - r0), :] += _dot_nn(ds, k)
            dk_acc[rows, :] += _dot_tn(ds, qs)
            dv_acc[rows, :] += _dot_tn(p.astype(BF16), dos)

        _sweep(qi, ratio, bk, step)
        dq_ref[...] = _rope_slab_bwd(dq_acc[...] * MLA_SCALE, c_ref[...], s_ref[...]).astype(dq_ref.dtype)

        @pl.when(qi == nq - 1)
        def _():
            dk_ref[...] = dk_acc[...].astype(dk_ref.dtype)
            dv_ref[...] = dv_acc[...].astype(dv_ref.dtype)

    blk = pl.BlockSpec((bq, LANES), lambda h, i: (i, h))
    tab = pl.BlockSpec((bq, LANES), lambda h, i: (i, 0))
    k_full = pl.BlockSpec((s, LANES), lambda h, i: (0, h))
    v_full = pl.BlockSpec((s, LANES), lambda h, i: (0, heads + h))
    shape = jax.ShapeDtypeStruct((s, heads * LANES), BF16)
    ins = [q, kv, kv, o, do, lse, cos_t, sin_t] + ([dkv_init, dkv_init] if has_init else [])
    dq, dk, dv = pl.pallas_call(
        body, name=name, grid=(heads, nq),
        in_specs=[blk, k_full, v_full, blk, blk, blk, tab, tab] + ([k_full, v_full] if has_init else []),
        out_specs=[blk, k_full, k_full],
        out_shape=[shape, shape, shape],
        scratch_shapes=[pltpu.VMEM((bq, LANES), F32), pltpu.VMEM((s, LANES), F32), pltpu.VMEM((s, LANES), F32)],
        compiler_params=pltpu.CompilerParams(dimension_semantics=("arbitrary", "arbitrary"),
                                             vmem_limit_bytes=VMEM_LIMIT),
    )(*ins)
    return dq, jnp.concatenate([dk, dv], axis=1)


def _pad_last(a, width):
    return jnp.pad(a, [(0, 0)] * (a.ndim - 1) + [(0, width - a.shape[-1])])


def _pad_qkv(w, heads):
    d = w.shape[0]
    return _pad_last(w.reshape(d, 3 * heads, SB_HEAD_DIM), LANES).reshape(d, 3 * heads * LANES)


def _unpad_qkv(g, heads):
    d = g.shape[0]
    return g.reshape(d, 3 * heads, LANES)[:, :, :SB_HEAD_DIM].reshape(d, 3 * heads * SB_HEAD_DIM)


def _pad_o(w, heads):
    d = w.shape[1]
    w = w.reshape(heads, SB_HEAD_DIM, d)
    return jnp.pad(w, [(0, 0), (0, LANES - SB_HEAD_DIM), (0, 0)]).reshape(heads * LANES, d)


def _unpad_o(g, heads):
    d = g.shape[1]
    return g.reshape(heads, LANES, d)[:, :SB_HEAD_DIM, :].reshape(heads * SB_HEAD_DIM, d)


def _pad_uq(w, heads):
    r = w.shape[0]
    return _pad_last(w.reshape(r, heads, MLA_NOPE + MLA_ROPE), LANES).reshape(r, heads * LANES)


def _unpad_uq(g, heads):
    r = g.shape[0]
    return g.reshape(r, heads, LANES)[:, :, :MLA_NOPE + MLA_ROPE].reshape(r, heads * (MLA_NOPE + MLA_ROPE))


def _pad_dkv(w):
    d = w.shape[0]
    rope = jnp.zeros((d, LANES), w.dtype).at[:, ROPE_LO:ROPE_LO + MLA_ROPE].set(w[:, MLA_KV_RANK:])
    return jnp.concatenate([w[:, :MLA_KV_RANK], rope], axis=1)


def _unpad_dkv(g):
    return jnp.concatenate([g[:, :MLA_KV_RANK], g[:, MLA_KV_RANK + ROPE_LO:MLA_KV_RANK + ROPE_LO + MLA_ROPE]], axis=1)


def _pad_ukv(w, heads):
    w = w.reshape(MLA_KV_RANK, heads, 2, MLA_NOPE)
    k_part = _pad_last(w[:, :, 0, :], LANES).reshape(MLA_KV_RANK, heads * LANES)
    v_part = _pad_last(w[:, :, 1, :], LANES).reshape(MLA_KV_RANK, heads * LANES)
    lane = jnp.arange(LANES)
    place = ((lane[:, None] == lane[None, :]) & (lane[:, None] >= ROPE_LO) & (lane[:, None] < ROPE_LO + MLA_ROPE))
    place = jnp.tile(place.astype(w.dtype), (1, heads))
    top = jnp.concatenate([k_part, v_part], axis=1)
    bottom = jnp.concatenate([place, jnp.zeros_like(place)], axis=1)
    return jnp.concatenate([top, bottom], axis=0)


def _unpad_ukv(g, heads):
    g = g[:MLA_KV_RANK]
    k_part = g[:, :heads * LANES].reshape(MLA_KV_RANK, heads, LANES)[:, :, :MLA_NOPE]
    v_part = g[:, heads * LANES:].reshape(MLA_KV_RANK, heads, LANES)[:, :, :MLA_V]
    return jnp.stack([k_part, v_part], axis=2).reshape(MLA_KV_RANK, heads * (MLA_NOPE + MLA_V))


def _rope_tables(positions):
    inv_freq = ROPE_THETA ** (-jnp.arange(0, MLA_ROPE, 2, dtype=F32) / MLA_ROPE)
    ang = positions.astype(F32)[:, None] * inv_freq
    cos, sin = jnp.cos(ang), jnp.sin(ang)
    s = positions.shape[0]
    cos_t = jnp.ones((s, LANES), F32).at[:, ROPE_LO:ROPE_LO + MLA_ROPE].set(jnp.concatenate([cos, cos], axis=1))
    sin_t = jnp.zeros((s, LANES), F32).at[:, ROPE_LO:ROPE_LO + MLA_ROPE].set(jnp.concatenate([-sin, sin], axis=1))
    return cos_t, sin_t


def _local_step(x, positions, target, w, norms):
    s, d = x.shape
    heads = d // SB_HEAD_DIM
    n_a = w["sb_w_qkv"].shape[0]
    n_b = w["mla_w_dq"].shape[0]
    depth = n_a + n_b
    cos_t, sin_t = _rope_tables(positions)

    wqkv = [_pad_qkv(w["sb_w_qkv"][l], heads) for l in range(n_a)]
    wo_a = [_pad_o(w["sb_w_o"][l], heads) for l in range(n_a)]
    wdkv = _pad_dkv(w["mla_w_dkv"])
    wkv = _pad_ukv(w["mla_w_ukv"], heads)
    wdq = [w["mla_w_dq"][j] for j in range(n_b)]
    wuq = [_pad_uq(w["mla_w_uq"][j], heads) for j in range(n_b)]
    wo_b = [_pad_o(w["mla_w_o"][j], heads) for j in range(n_b)]
    w1 = [w["mlp_w1"][l] for l in range(depth)]
    w2 = [w["mlp_w2"][l] for l in range(depth)]

    saved = []
    kv_saved = None
    kv = None
    for l in range(depth):
        t = f"l{l}"
        sv = {"x_in": x}
        h = _rms_fwd(x, norms["attn_norm"][l], name=f"{t}_attn_norm")
        sv["h"] = h
        if l < n_a:
            qkv = _mm(h, wqkv[l], name=f"{t}_qkv")
            o = _sb_fwd(qkv, heads, name=f"{t}_sb_fwd")
            sv["qkv"], sv["o"] = qkv, o
            x = _mm(o, wo_a[l], name=f"{t}_attn_out", epilogue=_epi_add, extras=[(x, "tile")], out_dtypes=(F32,))
        else:
            j = l - n_a
            if j == 0:
                hk = _rms_fwd(x, norms["kv_norm"], name="kv_norm")
                down = _mm(hk, wdkv, name="kv_down", out_dtypes=(F32,))
                cat = _kv_prep(down, norms["mla_kv_lat_norm"], cos_t, sin_t, name="kv_prep")
                kv = _mm(cat, wkv, name="kv_up")
                kv_saved = {"x_in": x, "hk": hk, "down": down, "cat": cat}
            cq0 = _mm(h, wdq[j], name=f"{t}_q_down", out_dtypes=(F32,))
            cq = _rms_fwd(cq0, norms["mla_q_lat_norm"][j], name=f"{t}_q_lat_norm")
            q = _mm(cq, wuq[j], name=f"{t}_q_up", epilogue=_epi_rope_heads, extras=[(cos_t, "row"), (sin_t, "row")])
            o, lse = _mla_fwd(q, kv, heads, name=f"{t}_mla_fwd")
            sv.update(cq0=cq0, cq=cq, q=q, o=o, lse=lse)
            x = _mm(o, wo_b[j], name=f"{t}_attn_out", epilogue=_epi_add, extras=[(x, "tile")], out_dtypes=(F32,))
        sv["x_mid"] = x
        h2 = _rms_fwd(x, norms["mlp_norm"][l], name=f"{t}_mlp_norm")
        u, a = _mm(h2, w1[l], name=f"{t}_mlp_up", epilogue=_epi_relu2, out_dtypes=(BF16, BF16))
        sv.update(h2=h2, u=u, a=a)
        x = _mm(a, w2[l], name=f"{t}_mlp_down", epilogue=_epi_add, extras=[(x, "tile")], out_dtypes=(F32,))
        saved.append(sv)

    loss_slab, dx, dxb, dg_final = _loss_bwd(x, norms["final_norm"], target, name="loss")
    loss = loss_slab[0, 0]

    g_attn_norm, g_mlp_norm = [None] * depth, [None] * depth
    g_qkv, g_o_a = [None] * n_a, [None] * n_a
    g_dq, g_uq, g_o_b, g_qlat = [None] * n_b, [None] * n_b, [None] * n_b, [None] * n_b
    g_w1, g_w2 = [None] * depth, [None] * depth
    dkv = None
    g_kv_norm = g_kv_lat = g_dkv = g_ukv = None

    for l in reversed(range(depth)):
        t = f"l{l}"
        sv = saved[l]
        du = _mm(dxb, w2[l], name=f"{t}_mlp_down_dx", dims="nt", epilogue=_epi_relu2_grad, extras=[(sv["u"], "tile")])
        g_w2[l] = _mm(sv["a"], dxb, name=f"{t}_mlp_down_dw", dims="tn", out_dtypes=(F32,))
        g_w1[l] = _mm(sv["h2"], du, name=f"{t}_mlp_up_dw", dims="tn", out_dtypes=(F32,))
        dh2 = _mm(du, w1[l], name=f"{t}_mlp_up_dx", dims="nt", out_dtypes=(F32,))
        dx, dxb, g_mlp_norm[l] = _rms_bwd(sv["x_mid"], norms["mlp_norm"][l], dh2, dx, name=f"{t}_mlp_norm_bwd")
        if l < n_a:
            do = _mm(dxb, wo_a[l], name=f"{t}_attn_out_dx", dims="nt")
            g_o_a[l] = _unpad_o(_mm(sv["o"], dxb, name=f"{t}_attn_out_dw", dims="tn", out_dtypes=(F32,)), heads)
            dq, dk, dv = _sb_bwd(sv["qkv"], sv["o"], do, heads, name=f"{t}_sb_bwd")
            dqkv = jnp.concatenate([dq, dk, dv], axis=1)
            g_qkv[l] = _unpad_qkv(_mm(sv["h"], dqkv, name=f"{t}_qkv_dw", dims="tn", out_dtypes=(F32,)), heads)
            dh = _mm(dqkv, wqkv[l], name=f"{t}_qkv_dx", dims="nt", out_dtypes=(F32,))
        else:
            j = l - n_a
            do = _mm(dxb, wo_b[j], name=f"{t}_attn_out_dx", dims="nt")
            g_o_b[j] = _unpad_o(_mm(sv["o"], dxb, name=f"{t}_attn_out_dw", dims="tn", out_dtypes=(F32,)), heads)
            dq, dkv = _mla_bwd(sv["q"], kv, sv["o"], do, sv["lse"], cos_t, sin_t, dkv, heads, name=f"{t}_mla_bwd")
            g_uq[j] = _unpad_uq(_mm(sv["cq"], dq, name=f"{t}_q_up_dw", dims="tn", out_dtypes=(F32,)), heads)
            dcq = _mm(dq, wuq[j], name=f"{t}_q_up_dx", dims="nt", out_dtypes=(F32,))
            _, dcq0, g_qlat[j] = _rms_bwd(sv["cq0"], norms["mla_q_lat_norm"][j], dcq, None, name=f"{t}_q_lat_norm_bwd")
            g_dq[j] = _mm(sv["h"], dcq0, name=f"{t}_q_down_dw", dims="tn", out_dtypes=(F32,))
            dh = _mm(dcq0, wdq[j], name=f"{t}_q_down_dx", dims="nt", out_dtypes=(F32,))
        dx, dxb, g_attn_norm[l] = _rms_bwd(sv["x_in"], norms["attn_norm"][l], dh, dx, name=f"{t}_attn_norm_bwd")
        if l == n_a:
            ks = kv_saved
            dcat = _mm(dkv, wkv, name="kv_up_dx", dims="nt", out_dtypes=(F32,))
            g_ukv = _unpad_ukv(_mm(ks["cat"], dkv, name="kv_up_dw", dims="tn", out_dtypes=(F32,)), heads)
            ddown, g_kv_lat = _kv_prep_bwd(ks["down"], norms["mla_kv_lat_norm"], cos_t, sin_t, dcat, name="kv_prep_bwd")
            g_dkv = _unpad_dkv(_mm(ks["hk"], ddown, name="kv_down_dw", dims="tn", out_dtypes=(F32,)))
            dhk = _mm(ddown, wdkv, name="kv_down_dx", dims="nt", out_dtypes=(F32,))
            dx, dxb, g_kv_norm = _rms_bwd(ks["x_in"], norms["kv_norm"], dhk, dx, name="kv_norm_bwd")

    grads = {
        "attn_norm": jnp.concatenate(g_attn_norm, axis=0), "mlp_norm": jnp.concatenate(g_mlp_norm, axis=0),
        "sb_w_qkv": g_qkv, "sb_w_o": g_o_a,
        "kv_norm": g_kv_norm[0], "mla_w_dkv": g_dkv, "mla_kv_lat_norm": g_kv_lat[0], "mla_w_ukv": g_ukv,
        "mla_w_dq": g_dq, "mla_q_lat_norm": jnp.concatenate(g_qlat, axis=0),
        "mla_w_uq": g_uq, "mla_w_o": g_o_b,
        "mlp_w1": g_w1, "mlp_w2": g_w2, "final_norm": dg_final[0],
    }
    return loss, dx, grads


def _flat_rows(n_elems):
    per_block = FLAT_COLS * FLAT_ROW_BLOCK * 2
    return -(-n_elems // per_block) * FLAT_ROW_BLOCK * 2


def _pack(arrays, dtype):
    flat = jnp.concatenate([a.reshape(-1).astype(dtype) for a in arrays])
    rows = _flat_rows(flat.shape[0])
    flat = jnp.pad(flat, (0, rows * FLAT_COLS - flat.shape[0]))
    return flat.reshape(rows, FLAT_COLS)


def _unpack(flat, shapes):
    flat = flat.reshape(-1)
    out, off = [], 0
    for shp in shapes:
        n = 1
        for v in shp:
            n *= v
        out.append(flat[off:off + n].reshape(shp))
        off += n
    return out


def _pack_small(arrays):
    rows = []
    for a in arrays:
        a = a.reshape(-1, a.shape[-1]) if a.shape[-1] == FLAT_COLS else a.reshape(1, -1)
        rows.append(_pad_last(a, FLAT_COLS))
    flat = jnp.concatenate(rows, axis=0)
    return jnp.pad(flat, [(0, -flat.shape[0] % 8), (0, 0)])


def _unpack_small(flat, shapes):
    out, row = [], 0
    for shp in shapes:
        if shp[-1] == FLAT_COLS:
            n = 1
            for v in shp[:-1]:
                n *= v
            out.append(flat[row:row + n].reshape(shp))
            row += n
        else:
            n = 1
            for v in shp:
                n *= v
            out.append(flat[row, :n].reshape(shp))
            row += 1
    return out


def _other_chips(x, y):
    return [(1 - x, y), (x, 1 - y), (1 - x, 1 - y)]


def _all_gather_chips(flat, *, name):
    rows, cols = flat.shape

    def body(x_ref, out_ref, send_sems, recv_sems, pass_send_sems, pass_recv_sems):
        x, y, c = lax.axis_index("x"), lax.axis_index("y"), lax.axis_index("c")
        me = 2 * x + y
        my_rows, sib_rows = _half_rows(rows)
        chips = _other_chips(x, y)
        sends = []
        for k, (px, py) in enumerate(chips):
            cp = pltpu.make_async_remote_copy(src_ref=x_ref.at[my_rows, :], dst_ref=out_ref.at[me, my_rows, :],
                                              send_sem=send_sems.at[k], recv_sem=recv_sems.at[k],
                                              device_id=(px, py, c), device_id_type=MESH)
            cp.start()
            sends.append(cp)
        for k, (px, py) in enumerate(chips):
            landed = out_ref.at[2 * px + py, my_rows, :]
            pltpu.make_async_remote_copy(src_ref=landed, dst_ref=landed, send_sem=send_sems.at[k],
                                         recv_sem=recv_sems.at[k], device_id=(px, py, c),
                                         device_id_type=MESH).wait_recv()
            cp = pltpu.make_async_remote_copy(src_ref=landed, dst_ref=landed, send_sem=pass_send_sems.at[k],
                                              recv_sem=pass_recv_sems.at[k], device_id=_sibling(),
                                              device_id_type=MESH)
            cp.start()
            sends.append(cp)
        for k, (px, py) in enumerate(chips):
            passed = out_ref.at[2 * px + py, sib_rows, :]
            pltpu.make_async_remote_copy(src_ref=passed, dst_ref=passed, send_sem=pass_send_sems.at[k],
                                         recv_sem=pass_recv_sems.at[k], device_id=_sibling(),
                                         device_id_type=MESH).wait_recv()
        for cp in sends:
            cp.wait_send()

    out = pl.pallas_call(
        body, name=name,
        in_specs=[pl.BlockSpec(memory_space=pltpu.HBM)],
        out_specs=pl.BlockSpec(memory_space=pltpu.HBM),
        out_shape=jax.ShapeDtypeStruct((N_CHIPS, rows, cols), flat.dtype),
        scratch_shapes=[pltpu.SemaphoreType.DMA((3,)), pltpu.SemaphoreType.DMA((3,)), pltpu.SemaphoreType.DMA((3,)),
                        pltpu.SemaphoreType.DMA((3,))],
        compiler_params=pltpu.CompilerParams(has_side_effects=True),
    )(flat)
    return out


def _exchange_chips(parts, *, name):
    def body(g_ref, out_ref, send_sems, recv_sems):
        x, y, c = lax.axis_index("x"), lax.axis_index("y"), lax.axis_index("c")
        me = 2 * x + y
        sends = []
        for k, (px, py) in enumerate(_other_chips(x, y)):
            cp = pltpu.make_async_remote_copy(src_ref=g_ref.at[2 * px + py], dst_ref=out_ref.at[me],
                                              send_sem=send_sems.at[k], recv_sem=recv_sems.at[k],
                                              device_id=(px, py, c), device_id_type=MESH)
            cp.start()
            sends.append(cp)
        for k, (px, py) in enumerate(_other_chips(x, y)):
            pltpu.make_async_remote_copy(src_ref=g_ref.at[me], dst_ref=out_ref.at[2 * px + py],
                                         send_sem=send_sems.at[k], recv_sem=recv_sems.at[k],
                                         device_id=(px, py, c), device_id_type=MESH).wait_recv()
        for cp in sends:
            cp.wait_send()

    out = pl.pallas_call(
        body, name=name,
        in_specs=[pl.BlockSpec(memory_space=pltpu.HBM)],
        out_specs=pl.BlockSpec(memory_space=pltpu.HBM),
        out_shape=jax.ShapeDtypeStruct(parts.shape, parts.dtype),
        scratch_shapes=[pltpu.SemaphoreType.DMA((3,)), pltpu.SemaphoreType.DMA((3,))],
        compiler_params=pltpu.CompilerParams(has_side_effects=True),
    )(parts)
    me = _my_chip()
    return lax.dynamic_update_index_in_dim(out, lax.dynamic_index_in_dim(parts, me, 0, keepdims=False), me, 0)


def _my_chip():
    return 2 * lax.axis_index("x") + lax.axis_index("y")


def _half_rows(rows):
    c = lax.axis_index("c")
    half = rows // 2
    return pl.ds(pl.multiple_of(c * half, 8), half), pl.ds(pl.multiple_of((1 - c) * half, 8), half)


def _sibling():
    return (lax.axis_index("x"), lax.axis_index("y"), 1 - lax.axis_index("c"))


def _pair_exchange(parts, *, name):
    n, rows, cols = parts.shape

    def body(p_ref, theirs_ref, send_sem, recv_sem):
        _, sib_rows = _half_rows(rows)
        cp = pltpu.make_async_remote_copy(src_ref=p_ref.at[:, sib_rows, :], dst_ref=theirs_ref, send_sem=send_sem,
                                          recv_sem=recv_sem, device_id=_sibling(), device_id_type=MESH)
        cp.start()
        cp.wait()

    half = rows // 2
    theirs = pl.pallas_call(
        body, name=name,
        in_specs=[pl.BlockSpec(memory_space=pltpu.HBM)],
        out_specs=pl.BlockSpec(memory_space=pltpu.HBM),
        out_shape=jax.ShapeDtypeStruct((n, half, cols), parts.dtype),
        scratch_shapes=[pltpu.SemaphoreType.DMA, pltpu.SemaphoreType.DMA],
        compiler_params=pltpu.CompilerParams(has_side_effects=True),
    )(parts)
    mine = lax.dynamic_slice_in_dim(parts, lax.axis_index("c") * half, half, axis=1)
    return mine, theirs


def _pair_sum(mine, theirs, *, name):
    n, rows, cols = mine.shape

    def body(a_ref, b_ref, o_ref):
        o_ref[...] = (a_ref[...].astype(F32) + b_ref[...].astype(F32)).astype(o_ref.dtype)

    blk = pl.BlockSpec((n, FLAT_ROW_BLOCK, cols), lambda i: (0, i, 0))
    return pl.pallas_call(
        body, name=name, grid=(rows // FLAT_ROW_BLOCK,),
        in_specs=[blk, blk], out_specs=blk, out_shape=jax.ShapeDtypeStruct(mine.shape, mine.dtype),
        compiler_params=pltpu.CompilerParams(dimension_semantics=("parallel",), vmem_limit_bytes=VMEM_LIMIT),
    )(mine, theirs)


def _sum_chips(parts, *, name):
    _, rows, cols = parts.shape

    def body(p_ref, o_ref):
        o_ref[...] = ((p_ref[0].astype(F32) + p_ref[1].astype(F32)) + p_ref[2].astype(F32)) + p_ref[3].astype(F32)

    return pl.pallas_call(
        body, name=name, grid=(rows // FLAT_ROW_BLOCK,),
        in_specs=[pl.BlockSpec((N_CHIPS, FLAT_ROW_BLOCK, cols), lambda i: (0, i, 0))],
        out_specs=pl.BlockSpec((FLAT_ROW_BLOCK, cols), lambda i: (i, 0)),
        out_shape=jax.ShapeDtypeStruct((rows, cols), F32),
        compiler_params=pltpu.CompilerParams(dimension_semantics=("parallel",), vmem_limit_bytes=VMEM_LIMIT),
    )(parts)


def _join_cores(half, *, name):
    rows2, cols = half.shape

    def body(h_ref, out_ref, send_sem, recv_sem):
        my_rows, sib_rows = _half_rows(2 * rows2)
        cp = pltpu.make_async_remote_copy(src_ref=h_ref, dst_ref=out_ref.at[my_rows, :], send_sem=send_sem,
                                          recv_sem=recv_sem, device_id=_sibling(), device_id_type=MESH)
        cp.start()
        cp.wait_send()
        pltpu.make_async_remote_copy(src_ref=h_ref, dst_ref=out_ref.at[sib_rows, :], send_sem=send_sem,
                                     recv_sem=recv_sem, device_id=_sibling(), device_id_type=MESH).wait_recv()

    out = pl.pallas_call(
        body, name=name,
        in_specs=[pl.BlockSpec(memory_space=pltpu.HBM)],
        out_specs=pl.BlockSpec(memory_space=pltpu.HBM),
        out_shape=jax.ShapeDtypeStruct((2 * rows2, cols), half.dtype),
        scratch_shapes=[pltpu.SemaphoreType.DMA, pltpu.SemaphoreType.DMA],
        compiler_params=pltpu.CompilerParams(has_side_effects=True),
    )(half)
    return lax.dynamic_update_slice_in_dim(out, half, lax.axis_index("c") * rows2, axis=0)


def _all_reduce_small(v, *, name):
    rows, cols = v.shape
    flips = [(fx, fy, fc) for fx in (0, 1) for fy in (0, 1) for fc in (0, 1)][1:]

    def body(v_ref, out_ref, gath_ref, send_sems, recv_sems):
        x, y, c = lax.axis_index("x"), lax.axis_index("y"), lax.axis_index("c")
        me = 4 * x + 2 * y + c
        gath_ref[me] = v_ref[...]
        peers = [((1 - x) if fx else x, (1 - y) if fy else y, (1 - c) if fc else c) for fx, fy, fc in flips]
        sends = []
        for k, peer in enumerate(peers):
            cp = pltpu.make_async_remote_copy(src_ref=v_ref, dst_ref=gath_ref.at[me], send_sem=send_sems.at[k],
                                              recv_sem=recv_sems.at[k], device_id=peer, device_id_type=MESH)
            cp.start()
            sends.append(cp)
        for k, (px, py, pc) in enumerate(peers):
            pltpu.make_async_remote_copy(src_ref=v_ref, dst_ref=gath_ref.at[4 * px + 2 * py + pc],
                                         send_sem=send_sems.at[k], recv_sem=recv_sems.at[k],
                                         device_id=(px, py, pc), device_id_type=MESH).wait_recv()
        for cp in sends:
            cp.wait_send()
        total = gath_ref[0]
        for k in range(1, 8):
            total = total + gath_ref[k]
        out_ref[...] = total

    total, _ = pl.pallas_call(
        body, name=name,
        in_specs=[pl.BlockSpec(memory_space=pltpu.VMEM)],
        out_specs=[pl.BlockSpec(memory_space=pltpu.VMEM), pl.BlockSpec(memory_space=pltpu.VMEM)],
        out_shape=[jax.ShapeDtypeStruct((rows, cols), v.dtype), jax.ShapeDtypeStruct((8, rows, cols), v.dtype)],
        scratch_shapes=[pltpu.SemaphoreType.DMA((7,)), pltpu.SemaphoreType.DMA((7,))],
        compiler_params=pltpu.CompilerParams(has_side_effects=True),
    )(v)
    return total


def _adamw(w, g_parts, m, v, *, name):
    rows, cols = w.shape
    br = min(FLAT_ROW_BLOCK, rows)
    n_parts = len(g_parts)

    def body(*refs):
        w_ref = refs[0]
        g_refs = refs[1:1 + n_parts]
        m_ref, v_ref = refs[1 + n_parts], refs[2 + n_parts]
        g_out, d_out, m_out, v_out = refs[-4:]
        g = g_refs[0][...]
        for r in g_refs[1:]:
            g = g + r[...]
        m_new = ADAM_B1 * m_ref[...] + (1.0 - ADAM_B1) * g
        v_new = ADAM_B2 * v_ref[...] + (1.0 - ADAM_B2) * jnp.square(g)
        m_hat = m_new / (1.0 - ADAM_B1 ** ADAM_STEP)
        v_hat = v_new / (1.0 - ADAM_B2 ** ADAM_STEP)
        g_out[...] = g
        d_out[...] = -ADAM_LR * (m_hat / (jnp.sqrt(v_hat) + ADAM_EPS) + ADAM_WD * w_ref[...])
        m_out[...] = m_new
        v_out[...] = v_new

    blk = pl.BlockSpec((br, cols), lambda i: (i, 0))
    shape = jax.ShapeDtypeStruct((rows, cols), F32)
    return pl.pallas_call(
        body, name=name, grid=(rows // br,),
        in_specs=[blk] * (3 + n_parts), out_specs=[blk] * 4, out_shape=[shape] * 4,
        compiler_params=pltpu.CompilerParams(dimension_semantics=("parallel",), vmem_limit_bytes=VMEM_LIMIT),
    )(w, *g_parts, m, v)


def _assemble(gathered_shards, name):
    return jnp.concatenate(gathered_shards, axis=SHARD_AXIS[name])


def _chip_shard(full, name, j):
    if isinstance(full, list):
        axis = SHARD_AXIS[name] - 1
        layers = full
    else:
        axis = SHARD_AXIS[name]
        layers = [full]
    n = layers[0].shape[axis] // N_CHIPS
    return [lax.slice_in_dim(g, j * n, (j + 1) * n, axis=axis) for g in layers]


def kernel(x, positions, attn_norm, mlp_norm, sb_w_qkv, sb_w_o, kv_norm, mla_w_dkv, mla_kv_lat_norm, mla_w_ukv, mla_w_dq, mla_q_lat_norm, mla_w_uq, mla_w_o, mlp_w1, mlp_w2, final_norm, loss_target, m_attn_norm, m_mlp_norm, m_sb_w_qkv, m_sb_w_o, m_kv_norm, m_mla_w_dkv, m_mla_kv_lat_norm, m_mla_w_ukv, m_mla_w_dq, m_mla_q_lat_norm, m_mla_w_uq, m_mla_w_o, m_mlp_w1, m_mlp_w2, m_final_norm, v_attn_norm, v_mlp_norm, v_sb_w_qkv, v_sb_w_o, v_kv_norm, v_mla_w_dkv, v_mla_kv_lat_norm, v_mla_w_ukv, v_mla_w_dq, v_mla_q_lat_norm, v_mla_w_uq, v_mla_w_o, v_mlp_w1, v_mlp_w2, v_final_norm):
    weights = dict(attn_norm=attn_norm, mlp_norm=mlp_norm, sb_w_qkv=sb_w_qkv, sb_w_o=sb_w_o, kv_norm=kv_norm,
                   mla_w_dkv=mla_w_dkv, mla_kv_lat_norm=mla_kv_lat_norm, mla_w_ukv=mla_w_ukv, mla_w_dq=mla_w_dq,
                   mla_q_lat_norm=mla_q_lat_norm, mla_w_uq=mla_w_uq, mla_w_o=mla_w_o, mlp_w1=mlp_w1, mlp_w2=mlp_w2,
                   final_norm=final_norm)
    m_in = dict(attn_norm=m_attn_norm, mlp_norm=m_mlp_norm, sb_w_qkv=m_sb_w_qkv, sb_w_o=m_sb_w_o, kv_norm=m_kv_norm,
                mla_w_dkv=m_mla_w_dkv, mla_kv_lat_norm=m_mla_kv_lat_norm, mla_w_ukv=m_mla_w_ukv, mla_w_dq=m_mla_w_dq,
                mla_q_lat_norm=m_mla_q_lat_norm, mla_w_uq=m_mla_w_uq, mla_w_o=m_mla_w_o, mlp_w1=m_mlp_w1,
                mlp_w2=m_mlp_w2, final_norm=m_final_norm)
    v_in = dict(attn_norm=v_attn_norm, mlp_norm=v_mlp_norm, sb_w_qkv=v_sb_w_qkv, sb_w_o=v_sb_w_o, kv_norm=v_kv_norm,
                mla_w_dkv=v_mla_w_dkv, mla_kv_lat_norm=v_mla_kv_lat_norm, mla_w_ukv=v_mla_w_ukv, mla_w_dq=v_mla_w_dq,
                mla_q_lat_norm=v_mla_q_lat_norm, mla_w_uq=v_mla_w_uq, mla_w_o=v_mla_w_o, mlp_w1=v_mlp_w1,
                mlp_w2=v_mlp_w2, final_norm=v_final_norm)
    shard_shapes = [weights[n].shape for n in BIG_WEIGHTS]
    small_shapes = [weights[n].shape for n in SMALL_WEIGHTS]

    flat_w = _pack([weights[n] for n in BIG_WEIGHTS], BF16)
    gathered = _all_gather_chips(flat_w, name="weights_all_gather")
    per_chip = [_unpack(jnp.where(_my_chip() == j, flat_w, gathered[j]), shard_shapes) for j in range(N_CHIPS)]
    full_w = {n: _assemble([per_chip[j][i] for j in range(N_CHIPS)], n) for i, n in enumerate(BIG_WEIGHTS)}
    norms = {n: weights[n] for n in SMALL_WEIGHTS}

    loss, dx, grads = _local_step(x[0], positions[0], loss_target[0], full_w, norms)
    loss = lax.psum(loss, ("x", "y", "c"))

    parts = jnp.stack([_pack([piece for n in BIG_WEIGHTS for piece in _chip_shard(grads[n], n, j)], BF16)
                       for j in range(N_CHIPS)])
    mine, theirs = _pair_exchange(parts, name="grads_pair_exchange")
    chip_part = _pair_sum(mine, theirs, name="grads_pair_sum")
    received = _exchange_chips(chip_part, name="grads_exchange")
    g_half = _sum_chips(received, name="grads_sum_chips")
    g_sum = _join_cores(g_half, name="grads_join_cores")
    g_flat, d_flat, m_flat, v_flat = _adamw(
        _pack([weights[n] for n in BIG_WEIGHTS], F32), [g_sum],
        _pack([m_in[n] for n in BIG_WEIGHTS], F32), _pack([v_in[n] for n in BIG_WEIGHTS], F32), name="adamw_big")
    out_g = dict(zip(BIG_WEIGHTS, _unpack(g_flat, shard_shapes)))
    out_d = dict(zip(BIG_WEIGHTS, _unpack(d_flat, shard_shapes)))
    out_m = dict(zip(BIG_WEIGHTS, _unpack(m_flat, shard_shapes)))
    out_v = dict(zip(BIG_WEIGHTS, _unpack(v_flat, shard_shapes)))

    small_sum = _all_reduce_small(_pack_small([grads[n] for n in SMALL_WEIGHTS]), name="gains_all_reduce")
    sg, sd, sm, sv = _adamw(_pack_small([weights[n] for n in SMALL_WEIGHTS]), [small_sum],
                            _pack_small([m_in[n] for n in SMALL_WEIGHTS]),
                            _pack_small([v_in[n] for n in SMALL_WEIGHTS]), name="adamw_gains")
    out_g.update(zip(SMALL_WEIGHTS, _unpack_small(sg, small_shapes)))
    out_d.update(zip(SMALL_WEIGHTS, _unpack_small(sd, small_shapes)))
    out_m.update(zip(SMALL_WEIGHTS, _unpack_small(sm, small_shapes)))
    out_v.update(zip(SMALL_WEIGHTS, _unpack_small(sv, small_shapes)))

    return (loss, dx[None], *[out_g[n] for n in ALL_WEIGHTS], *[out_d[n] for n in ALL_WEIGHTS],
            *[out_m[n] for n in ALL_WEIGHTS], *[out_v[n] for n in ALL_WEIGHTS])
```

```python
import functools

import jax
import jax.numpy as jnp
from jax import lax
from jax.experimental import pallas as pl
from jax.experimental.pallas import tpu as pltpu

F32 = jnp.float32
BF16 = jnp.bfloat16

LANES = 128
SB_HEAD_DIM = 64
MLA_NOPE = 64
MLA_ROPE = 32
MLA_V = 64
MLA_Q_RANK = 384
MLA_KV_RANK = 256
CHUNK = 64
ROPE_THETA = 10000.0
NORM_EPS = 1e-6
SB_SCALE = SB_HEAD_DIM ** -0.5
MLA_SCALE = (MLA_NOPE + MLA_ROPE) ** -0.5
ROPE_LO = MLA_NOPE
ROPE_HALF = MLA_ROPE // 2
ATT_Q_BLOCK = 1024
ATT_K_BLOCK = 256
MLA_FWD_K_BLOCK = 512
NEG_BIG = -1e30
SB_DEAD_LOG = -110.0
VMEM_LIMIT = 56 * 1024 * 1024

ADAM_LR = 0.001
ADAM_B1 = 0.9
ADAM_B2 = 0.999
ADAM_EPS = 1e-08
ADAM_WD = 0.01
ADAM_STEP = 10

FLAT_COLS = 1024
FLAT_ROW_BLOCK = 256
N_CHIPS = 4
MESH = pl.DeviceIdType.MESH

BIG_WEIGHTS = ["sb_w_qkv", "sb_w_o", "mla_w_dkv", "mla_w_ukv", "mla_w_dq", "mla_w_uq", "mla_w_o", "mlp_w1", "mlp_w2"]
SHARD_AXIS = {"sb_w_qkv": 2, "sb_w_o": 1, "mla_w_dkv": 0, "mla_w_ukv": 1, "mla_w_dq": 1, "mla_w_uq": 2,
              "mla_w_o": 1, "mlp_w1": 2, "mlp_w2": 1}
SMALL_WEIGHTS = ["attn_norm", "mlp_norm", "kv_norm", "mla_kv_lat_norm", "mla_q_lat_norm", "final_norm"]
ALL_WEIGHTS = ["attn_norm", "mlp_norm", "sb_w_qkv", "sb_w_o", "kv_norm", "mla_w_dkv", "mla_kv_lat_norm", "mla_w_ukv",
               "mla_w_dq", "mla_q_lat_norm", "mla_w_uq", "mla_w_o", "mlp_w1", "mlp_w2", "final_norm"]


def _dot(a, b, dims):
    return lax.dot_general(a, b, (dims, ((), ())), preferred_element_type=F32)


def _dot_nn(a, b):
    return _dot(a, b, ((1,), (0,)))


def _dot_nt(a, b):
    return _dot(a, b, ((1,), (1,)))


def _dot_tn(a, b):
    return _dot(a, b, ((0,), (0,)))


def _pick_block(n, target):
    if n <= target:
        return n
    best = max(b for b in range(LANES, target + 1, LANES) if n % b == 0)
    return best


MM_ROWS = 512
MM_COLS = 1024
MM_DEPTH = 4096
MM_DEPTH_TN = 1024


def _mm(a, b, *, name, dims="nn", epilogue=None, extras=(), out_dtypes=(BF16,)):
    if dims == "nn":
        (m, k), (k2, n) = a.shape, b.shape
    elif dims == "nt":
        (m, k), (n, k2) = a.shape, b.shape
    else:
        (k, m), (k2, n) = a.shape, b.shape
    assert k == k2, (name, a.shape, b.shape)
    if dims == "tn":
        bm, bn, bk = _pick_block(m, MM_COLS), _pick_block(n, MM_COLS), _pick_block(k, MM_DEPTH_TN)
    else:
        rows = MM_ROWS if k > MM_DEPTH // 2 else 2 * MM_ROWS
        bm, bn, bk = _pick_block(m, rows), _pick_block(n, MM_COLS), _pick_block(k, MM_DEPTH)
    nk = k // bk
    if dims == "tn":
        a_spec = pl.BlockSpec((bk, bm), lambda j, i, kk: (kk, i))
    else:
        a_spec = pl.BlockSpec((bm, bk), lambda j, i, kk: (i, kk))
    if dims == "nt":
        b_spec = pl.BlockSpec((bn, bk), lambda j, i, kk: (j, kk))
    else:
        b_spec = pl.BlockSpec((bk, bn), lambda j, i, kk: (kk, j))
    extra_specs = []
    for arr, kind in extras:
        if kind == "tile":
            assert arr.shape == (m, n), (name, arr.shape)
            extra_specs.append(pl.BlockSpec((bm, bn), lambda j, i, kk: (i, j)))
        else:
            assert arr.shape == (m, LANES), (name, arr.shape)
            extra_specs.append(pl.BlockSpec((bm, LANES), lambda j, i, kk: (i, 0)))
    n_extra = len(extras)
    n_out = len(out_dtypes)
    dot = {"nn": _dot_nn, "nt": _dot_nt, "tn": _dot_tn}[dims]

    def body(*refs):
        a_ref, b_ref = refs[0], refs[1]
        extra_refs = refs[2:2 + n_extra]
        out_refs = refs[2 + n_extra:2 + n_extra + n_out]

        def finish(acc):
            outs = (acc,) if epilogue is None else epilogue(acc, *[r[...] for r in extra_refs])
            for o_ref, o in zip(out_refs, outs):
                o_ref[...] = o.astype(o_ref.dtype)

        part = dot(a_ref[...].astype(BF16), b_ref[...].astype(BF16))
        if nk == 1:
            finish(part)
            return
        acc_ref = refs[-1]
        kk = pl.program_id(2)

        @pl.when(kk == 0)
        def _():
            acc_ref[...] = part

        @pl.when(kk > 0)
        def _():
            acc_ref[...] += part

        @pl.when(kk == nk - 1)
        def _():
            finish(acc_ref[...])

    outs = pl.pallas_call(
        body, name=name, grid=(n // bn, m // bm, nk),
        in_specs=[a_spec, b_spec] + extra_specs,
        out_specs=[pl.BlockSpec((bm, bn), lambda j, i, kk: (i, j)) for _ in range(n_out)],
        out_shape=[jax.ShapeDtypeStruct((m, n), dt) for dt in out_dtypes],
        scratch_shapes=[pltpu.VMEM((bm, bn), F32)] if nk > 1 else [],
        compiler_params=pltpu.CompilerParams(dimension_semantics=("parallel", "parallel", "arbitrary"),
                                             vmem_limit_bytes=VMEM_LIMIT),
    )(a, b, *[arr for arr, _ in extras])
    return outs[0] if n_out == 1 else outs


def _epi_add(acc, res):
    return (res + acc,)


def _epi_relu2(acc):
    r = jnp.maximum(acc, 0.0)
    return acc, r * r


def _epi_relu2_grad(acc, u):
    return (acc * (2.0 * jnp.maximum(u.astype(F32), 0.0)),)


def _rope_slab(t, cos_t, sin_t):
    lane = lax.broadcasted_iota(jnp.int32, t.shape, 1)
    partner = jnp.where(lane < ROPE_LO + ROPE_HALF, pltpu.roll(t, LANES - ROPE_HALF, 1), pltpu.roll(t, ROPE_HALF, 1))
    return t * cos_t + partner * sin_t


def _rope_slab_bwd(d, cos_t, sin_t):
    ds = d * sin_t
    lane = lax.broadcasted_iota(jnp.int32, d.shape, 1)
    partner = jnp.where(lane < ROPE_LO + ROPE_HALF, pltpu.roll(ds, LANES - ROPE_HALF, 1), pltpu.roll(ds, ROPE_HALF, 1))
    in_rope = (lane >= ROPE_LO) & (lane < ROPE_LO + MLA_ROPE)
    return d * cos_t + jnp.where(in_rope, partner, 0.0)


def _epi_rope_heads(acc, cos_t, sin_t):
    slabs = [_rope_slab(acc[:, j * LANES:(j + 1) * LANES], cos_t, sin_t) for j in range(acc.shape[1] // LANES)]
    return (jnp.concatenate(slabs, axis=1) * MLA_SCALE,)


def _row_block(s):
    return min(512, s)


def _rms_fwd(x, g, *, name):
    s, d = x.shape
    bm = _row_block(s)

    def body(x_ref, g_ref, o_ref):
        xv = x_ref[...]
        r = lax.rsqrt(jnp.mean(xv * xv, axis=-1, keepdims=True) + NORM_EPS)
        o_ref[...] = ((xv * r) * g_ref[...]).astype(o_ref.dtype)

    return pl.pallas_call(
        body, name=name, grid=(s // bm,),
        in_specs=[pl.BlockSpec((bm, d), lambda i: (i, 0)), pl.BlockSpec((1, d), lambda i: (0, 0))],
        out_specs=pl.BlockSpec((bm, d), lambda i: (i, 0)),
        out_shape=jax.ShapeDtypeStruct((s, d), BF16),
        compiler_params=pltpu.CompilerParams(dimension_semantics=("parallel",), vmem_limit_bytes=VMEM_LIMIT),
    )(x, g.reshape(1, d))


def _rms_bwd_math(xv, gv, dy):
    r = lax.rsqrt(jnp.mean(xv * xv, axis=-1, keepdims=True) + NORM_EPS)
    xhat = xv * r
    dyg = dy * gv
    mdot = jnp.mean(dyg * xhat, axis=-1, keepdims=True)
    dx = r * (dyg - xhat * mdot)
    dg = jnp.sum(dy * xhat, axis=0, keepdims=True)
    return dx, dg


def _rms_bwd(x, g, dy, dres, *, name, lead_axis=False):
    s, d = x.shape
    bm = _row_block(s)
    has_res = dres is not None

    def body(*refs):
        x_ref, g_ref, dy_ref = refs[:3]
        dres_ref = refs[3] if has_res else None
        dx_ref, dxb_ref, dg_ref = refs[-3:]
        dx, dg = _rms_bwd_math(x_ref[...], g_ref[...], dy_ref[...].astype(F32))
        if has_res:
            dx = dx + dres_ref[...]
        dx_ref[...] = dx
        dxb_ref[...] = dx.astype(BF16)

        @pl.when(pl.program_id(0) == 0)
        def _():
            dg_ref[...] = jnp.zeros_like(dg_ref)

        dg_ref[...] += dg

    row = pl.BlockSpec((bm, d), lambda i: (i, 0))
    vec = pl.BlockSpec((1, d), lambda i: (0, 0))
    ins = [x, g.reshape(1, d), dy] + ([dres] if has_res else [])
    dx_spec, dx_shape = row, (s, d)
    if lead_axis:
        dx_spec, dx_shape = pl.BlockSpec((None, bm, d), lambda i: (0, i, 0)), (1, s, d)
    return pl.pallas_call(
        body, name=name, grid=(s // bm,),
        in_specs=[row, vec, row] + ([row] if has_res else []),
        out_specs=[dx_spec, row, vec],
        out_shape=[jax.ShapeDtypeStruct(dx_shape, F32), jax.ShapeDtypeStruct((s, d), BF16),
                   jax.ShapeDtypeStruct((1, d), F32)],
        compiler_params=pltpu.CompilerParams(dimension_semantics=("arbitrary",), vmem_limit_bytes=VMEM_LIMIT),
    )(*ins)


def _loss_bwd(x, g, target, *, name):
    s, d = x.shape
    bm = _row_block(s)

    def body(x_ref, g_ref, t_ref, loss_ref, dx_ref, dxb_ref, dg_ref):
        xv, gv = x_ref[...], g_ref[...]
        r = lax.rsqrt(jnp.mean(xv * xv, axis=-1, keepdims=True) + NORM_EPS)
        err = (xv * r) * gv - t_ref[...]
        dx, dg = _rms_bwd_math(xv, gv, err * (1.0 / d))
        dx_ref[...] = dx
        dxb_ref[...] = dx.astype(BF16)

        @pl.when(pl.program_id(0) == 0)
        def _():
            dg_ref[...] = jnp.zeros_like(dg_ref)
            loss_ref[...] = jnp.zeros_like(loss_ref)

        dg_ref[...] += dg
        loss_ref[...] += jnp.sum(jnp.mean(err * err, axis=-1, keepdims=True), axis=0, keepdims=True) * 0.5

    row = pl.BlockSpec((bm, d), lambda i: (i, 0))
    vec = pl.BlockSpec((1, d), lambda i: (0, 0))
    assert target.shape == (1, s, d), target.shape
    return pl.pallas_call(
        body, name=name, grid=(s // bm,),
        in_specs=[row, vec, pl.BlockSpec((None, bm, d), lambda i: (0, i, 0))],
        out_specs=[pl.BlockSpec((8, LANES), lambda i: (0, 0)), row, row, vec],
        out_shape=[jax.ShapeDtypeStruct((8, LANES), F32), jax.ShapeDtypeStruct((s, d), F32),
                   jax.ShapeDtypeStruct((s, d), BF16), jax.ShapeDtypeStruct((1, d), F32)],
        compiler_params=pltpu.CompilerParams(dimension_semantics=("arbitrary",), vmem_limit_bytes=VMEM_LIMIT),
    )(x, g.reshape(1, d), target)


def _kv_prep(down, g, cos_t, sin_t, *, name):
    s, w = down.shape
    bm = _row_block(s)

    def body(d_ref, g_ref, c_ref, s_ref, o_ref):
        lat = d_ref[:, :MLA_KV_RANK]
        r = lax.rsqrt(jnp.mean(lat * lat, axis=-1, keepdims=True) + NORM_EPS)
        o_ref[:, :MLA_KV_RANK] = ((lat * r) * g_ref[...]).astype(BF16)
        o_ref[:, MLA_KV_RANK:] = _rope_slab(d_ref[:, MLA_KV_RANK:], c_ref[...], s_ref[...]).astype(BF16)

    row = pl.BlockSpec((bm, w), lambda i: (i, 0))
    tab = pl.BlockSpec((bm, LANES), lambda i: (i, 0))
    return pl.pallas_call(
        body, name=name, grid=(s // bm,),
        in_specs=[row, pl.BlockSpec((1, MLA_KV_RANK), lambda i: (0, 0)), tab, tab],
        out_specs=row, out_shape=jax.ShapeDtypeStruct((s, w), BF16),
        compiler_params=pltpu.CompilerParams(dimension_semantics=("parallel",), vmem_limit_bytes=VMEM_LIMIT),
    )(down, g.reshape(1, MLA_KV_RANK), cos_t, sin_t)


def _kv_prep_bwd(down, g, cos_t, sin_t, dcat, *, name):
    s, w = down.shape
    bm = _row_block(s)

    def body(d_ref, g_ref, c_ref, s_ref, dc_ref, o_ref, dg_ref):
        dlat, dg = _rms_bwd_math(d_ref[:, :MLA_KV_RANK], g_ref[...], dc_ref[:, :MLA_KV_RANK])
        o_ref[:, :MLA_KV_RANK] = dlat.astype(BF16)
        o_ref[:, MLA_KV_RANK:] = _rope_slab_bwd(dc_ref[:, MLA_KV_RANK:], c_ref[...], s_ref[...]).astype(BF16)

        @pl.when(pl.program_id(0) == 0)
        def _():
            dg_ref[...] = jnp.zeros_like(dg_ref)

        dg_ref[...] += dg

    row = pl.BlockSpec((bm, w), lambda i: (i, 0))
    tab = pl.BlockSpec((bm, LANES), lambda i: (i, 0))
    vec = pl.BlockSpec((1, MLA_KV_RANK), lambda i: (0, 0))
    return pl.pallas_call(
        body, name=name, grid=(s // bm,),
        in_specs=[row, vec, tab, tab, row],
        out_specs=[row, vec],
        out_shape=[jax.ShapeDtypeStruct((s, w), BF16), jax.ShapeDtypeStruct((1, MLA_KV_RANK), F32)],
        compiler_params=pltpu.CompilerParams(dimension_semantics=("arbitrary",), vmem_limit_bytes=VMEM_LIMIT),
    )(down, g.reshape(1, MLA_KV_RANK), cos_t, sin_t, dcat)


def _split_bf16(v):
    hi = v.astype(BF16)
    lo = (v - hi.astype(F32)).astype(BF16)
    return hi, lo


def _suffix_matrices(n):
    row = lax.broadcasted_iota(jnp.int32, (n, n), 0)
    col = lax.broadcasted_iota(jnp.int32, (n, n), 1)
    incl = (row >= col).astype(BF16)
    return (row > col).astype(BF16), jnp.concatenate([incl, incl], axis=0)


def _suffix_sum(v, matrix):
    hi, lo = _split_bf16(v)
    return _dot_nn(jnp.concatenate([hi, lo], axis=1), matrix)


def _block_positions(qi, kb, bq, bk, r0, r1):
    row = qi * bq + r0 + lax.broadcasted_iota(jnp.int32, (r1 - r0, bk), 0)
    col = kb * bk + lax.broadcasted_iota(jnp.int32, (r1 - r0, bk), 1)
    return row, col


def _att_blocks(s, key_block=ATT_K_BLOCK):
    bq, bk = min(ATT_Q_BLOCK, s), min(key_block, s)
    return bq, bk, s // bq, bq // bk


def _sweep(qi, ratio, bk, step, unroll=2, alive=None):
    bq = ratio * bk
    for d in range(ratio):
        kb, r0 = (qi + 1) * ratio - 1 - d, (ratio - 1 - d) * bk
        near = bq if alive is None else min(r0 + 2 * bk, bq)
        step(kb, True, r0, near)
        if near < bq:
            pl.when(alive(near))(functools.partial(step, kb, False, near, bq))
    unroll = unroll if ratio % unroll == 0 else 1
    trips = qi * (ratio // unroll)

    def trip(i):
        for u in range(unroll):
            step(qi * ratio - 1 - (i * unroll + u), False, 0, bq)

    if alive is None:
        lax.fori_loop(0, trips, lambda i, carry: (trip(i), carry)[1], 0)
    else:
        lax.while_loop(lambda i: jnp.logical_and(i < trips, alive(0)), lambda i: (trip(i), i + 1)[1], 0)


def _stick_left(c_ref, r0):
    return jnp.max(c_ref[r0:, :]) > SB_DEAD_LOG


def _sb_logs(q, k):
    z = _dot_nt(q, k)
    lb = jnp.minimum(z, 0.0) - jnp.log(1.0 + jnp.exp(-jnp.abs(z)))
    return lb, lb - z


def _sb_fwd(qkv, heads, *, name):
    s = qkv.shape[0]
    bq, bk, nq, ratio = _att_blocks(s)

    def body(q_ref, k_ref, v_ref, o_ref, acc_ref, c_ref):
        qi = pl.program_id(1)
        q = q_ref[...] * SB_SCALE
        m_strict, _ = _suffix_matrices(bk)
        acc_ref[...] = jnp.zeros_like(acc_ref)
        c_ref[...] = jnp.zeros_like(c_ref)

        def step(kb, masked, r0, r1):
            rows = pl.ds(pl.multiple_of(kb * bk, bk), bk)
            mine = pl.ds(r0, r1 - r0)
            k, v = k_ref[rows, :], v_ref[rows, :]
            lb, lk = _sb_logs(q[r0:r1], k)
            if masked:
                row, col = _block_positions(qi, kb, bq, bk, r0, r1)
                causal = col < row
                lk = jnp.where(causal, lk, 0.0)
            c = c_ref[mine, :]
            w = jnp.exp(lb + _dot_nn(lk.astype(BF16), m_strict) + jnp.tile(c, (1, bk // LANES)))
            if masked:
                w = jnp.where(causal, w, 0.0)
            acc_ref[mine, :] += _dot_nn(w.astype(BF16), v)
            c_ref[mine, :] = c + jnp.sum(lk, axis=-1, keepdims=True)

        _sweep(qi, ratio, bk, step, unroll=1, alive=functools.partial(_stick_left, c_ref))
        o_ref[...] = acc_ref[...].astype(o_ref.dtype)

    return pl.pallas_call(
        body, name=name, grid=(heads, nq),
        in_specs=[pl.BlockSpec((bq, LANES), lambda h, i: (i, h)),
                  pl.BlockSpec((s, LANES), lambda h, i: (0, heads + h)),
                  pl.BlockSpec((s, LANES), lambda h, i: (0, 2 * heads + h))],
        out_specs=pl.BlockSpec((bq, LANES), lambda h, i: (i, h)),
        out_shape=jax.ShapeDtypeStruct((s, heads * LANES), F32),
        scratch_shapes=[pltpu.VMEM((bq, LANES), F32), pltpu.VMEM((bq, LANES), F32)],
        compiler_params=pltpu.CompilerParams(dimension_semantics=("parallel", "arbitrary"),
                                             vmem_limit_bytes=VMEM_LIMIT),
    )(qkv, qkv, qkv)


def _sb_bwd(qkv, o, do, heads, *, name):
    s = qkv.shape[0]
    bq, bk, nq, ratio = _att_blocks(s)

    def body(q_ref, k_ref, v_ref, o_ref, do_ref, dq_ref, dk_ref, dv_ref, dq_acc, dk_acc, dv_acc, c_ref, e_ref):
        qi = pl.program_id(1)

        @pl.when(qi == 0)
        def _():
            dk_acc[...] = jnp.zeros_like(dk_acc)
            dv_acc[...] = jnp.zeros_like(dv_acc)

        q = q_ref[...] * SB_SCALE
        do = do_ref[...]
        q_t, do_t = q.T, do.T
        total = jnp.sum(do.astype(F32) * o_ref[...].astype(F32), axis=-1, keepdims=True)
        m_strict, m_incl = _suffix_matrices(bk)
        dq_acc[...] = jnp.zeros_like(dq_acc)
        c_ref[...] = jnp.zeros_like(c_ref)
        e_ref[...] = jnp.broadcast_to(total, e_ref.shape)
        reps = (1, bk // LANES)

        def step(kb, masked, r0, r1):
            rows = pl.ds(pl.multiple_of(kb * bk, bk), bk)
            mine = pl.ds(r0, r1 - r0)
            k, v = k_ref[rows, :], v_ref[rows, :]
            qs, dos = q[r0:r1], do[r0:r1]
            lb, lk_all = _sb_logs(qs, k)
            lk = lk_all
            if masked:
                row, col = _block_positions(qi, kb, bq, bk, r0, r1)
                causal = col < row
                lk = jnp.where(causal, lk_all, 0.0)
            c = c_ref[mine, :]
            w = jnp.exp(lb + _dot_nn(lk.astype(BF16), m_strict) + jnp.tile(c, reps))
            if masked:
                w = jnp.where(causal, w, 0.0)
            wb = w.astype(BF16)
            g = wb.astype(F32) * _dot_nt(dos, v)
            e = e_ref[mine, :]
            g_left = jnp.tile(e, reps) - _suffix_sum(g, m_incl)
            da = g * jnp.exp(lk_all) - jnp.exp(lb) * g_left
            if masked:
                da = jnp.where(causal, da, 0.0)
            dab = da.astype(BF16)
            dq_acc[mine, :] += _dot_nn(dab, k)
            dk_acc[:, rows] += _dot_nn(q_t[:, r0:r1], dab)
            dv_acc[:, rows] += _dot_nn(do_t[:, r0:r1], wb)
            e_ref[mine, :] = e - jnp.sum(g, axis=-1, keepdims=True)
            c_ref[mine, :] = c + jnp.sum(lk, axis=-1, keepdims=True)

        _sweep(qi, ratio, bk, step, unroll=1, alive=functools.partial(_stick_left, c_ref))
        dq_ref[...] = (dq_acc[...] * SB_SCALE).astype(dq_ref.dtype)

        @pl.when(qi == nq - 1)
        def _():
            dk_ref[...] = dk_acc[...].T.astype(dk_ref.dtype)
            dv_ref[...] = dv_acc[...].T.astype(dv_ref.dtype)

    blk = pl.BlockSpec((bq, LANES), lambda h, i: (i, h))
    full = pl.BlockSpec((s, LANES), lambda h, i: (0, h))
    shape = jax.ShapeDtypeStruct((s, heads * LANES), BF16)
    return pl.pallas_call(
        body, name=name, grid=(heads, nq),
        in_specs=[blk,
                  pl.BlockSpec((s, LANES), lambda h, i: (0, heads + h)),
                  pl.BlockSpec((s, LANES), lambda h, i: (0, 2 * heads + h)),
                  blk, blk],
        out_specs=[blk, full, full],
        out_shape=[shape, shape, shape],
        scratch_shapes=[pltpu.VMEM((bq, LANES), F32), pltpu.VMEM((LANES, s), F32), pltpu.VMEM((LANES, s), F32),
                        pltpu.VMEM((bq, LANES), F32), pltpu.VMEM((bq, LANES), F32)],
        compiler_params=pltpu.CompilerParams(dimension_semantics=("arbitrary", "arbitrary"),
                                             vmem_limit_bytes=VMEM_LIMIT),
    )(qkv, qkv, qkv, o, do)


def _chunk_allowed(qi, kb, bq, bk, r0, r1):
    row, col = _block_positions(qi, kb, bq, bk, r0, r1)
    return (col // CHUNK) <= (row // CHUNK)


def _mla_fwd(q, kv, heads, *, name):
    s = q.shape[0]
    bq, bk, nq, ratio = _att_blocks(s, MLA_FWD_K_BLOCK)
    reps = (1, bk // LANES)

    def body(q_ref, k_ref, v_ref, o_ref, lse_ref, acc_ref, m_ref, l_ref):
        qi = pl.program_id(1)
        qv = q_ref[...]
        acc_ref[...] = jnp.zeros_like(acc_ref)
        m_ref[...] = jnp.full_like(m_ref, NEG_BIG)
        l_ref[...] = jnp.zeros_like(l_ref)

        def step(kb, masked, r0, r1):
            rows = pl.ds(pl.multiple_of(kb * bk, bk), bk)
            mine = pl.ds(r0, r1 - r0)
            k, v = k_ref[rows, :], v_ref[rows, :]
            sc = _dot_nt(qv[r0:r1], k)
            if masked:
                allowed = _chunk_allowed(qi, kb, bq, bk, r0, r1)
                sc = jnp.where(allowed, sc, NEG_BIG)
            m_old = m_ref[mine, :]
            m_new = jnp.maximum(m_old, jnp.max(sc, axis=-1, keepdims=True))
            p = jnp.exp(sc - jnp.tile(m_new, reps))
            alpha = jnp.exp(m_old - m_new)
            l_ref[mine, :] = alpha * l_ref[mine, :] + jnp.sum(p, axis=-1, keepdims=True)
            acc_ref[mine, :] = alpha * acc_ref[mine, :] + _dot_nn(p.astype(BF16), v)
            m_ref[mine, :] = m_new

        _sweep(qi, ratio, bk, step)
        o_ref[...] = (acc_ref[...] / l_ref[...]).astype(o_ref.dtype)
        lse_ref[...] = m_ref[...] + jnp.log(l_ref[...])

    blk = pl.BlockSpec((bq, LANES), lambda h, i: (i, h))
    return pl.pallas_call(
        body, name=name, grid=(heads, nq),
        in_specs=[blk,
                  pl.BlockSpec((s, LANES), lambda h, i: (0, h)),
                  pl.BlockSpec((s, LANES), lambda h, i: (0, heads + h))],
        out_specs=[blk, blk],
        out_shape=[jax.ShapeDtypeStruct((s, heads * LANES), BF16), jax.ShapeDtypeStruct((s, heads * LANES), F32)],
        scratch_shapes=[pltpu.VMEM((bq, LANES), F32), pltpu.VMEM((bq, LANES), F32), pltpu.VMEM((bq, LANES), F32)],
        compiler_params=pltpu.CompilerParams(dimension_semantics=("parallel", "arbitrary"),
                                             vmem_limit_bytes=VMEM_LIMIT),
    )(q, kv, kv)


def _mla_bwd(q, kv, o, do, lse, cos_t, sin_t, dkv_init, heads, *, name):
    s = q.shape[0]
    bq, bk, nq, ratio = _att_blocks(s)
    reps = (1, bk // LANES)
    has_init = dkv_init is not None

    def body(*refs):
        q_ref, k_ref, v_ref, o_ref, do_ref, lse_ref, c_ref, s_ref = refs[:8]
        ki_ref, vi_ref = (refs[8], refs[9]) if has_init else (None, None)
        dq_ref, dk_ref, dv_ref, dq_acc, dk_acc, dv_acc = refs[-6:]
        qi = pl.program_id(1)

        @pl.when(qi == 0)
        def _():
            if has_init:
                dk_acc[...] = ki_ref[...].astype(F32).T
                dv_acc[...] = vi_ref[...].astype(F32).T
            else:
                dk_acc[...] = jnp.zeros_like(dk_acc)
                dv_acc[...] = jnp.zeros_like(dv_acc)

        qv = q_ref[...]
        do = do_ref[...]
        q_t, do_t = qv.T, do.T
        delta = jnp.sum(do.astype(F32) * o_ref[...].astype(F32), axis=-1, keepdims=True)
        lse_wide = jnp.tile(lse_ref[...], reps)
        dq_acc[...] = jnp.zeros_like(dq_acc)

        def step(kb, masked, r0, r1):
            rows = pl.ds(pl.multiple_of(kb * bk, bk), bk)
            k, v = k_ref[rows, :], v_ref[rows, :]
            qs, dos = qv[r0:r1], do[r0:r1]
            p = jnp.exp(_dot_nt(qs, k) - lse_wide[r0:r1])
            if masked:
                p = jnp.where(_chunk_allowed(qi, kb, bq, bk, r0, r1), p, 0.0)
            ds = (p * (_dot_nt(dos, v) - delta[r0:r1])).astype(BF16)
            dq_acc[pl.ds(r0, r1 - r0), :] += _dot_nn(ds, k)
            dk_acc[:, rows] += _dot_nn(q_t[:, r0:r1], ds)
            dv_acc[:, rows] += _dot_nn(do_t[:, r0:r1], p.astype(BF16))

        _sweep(qi, ratio, bk, step)
        dq_ref[...] = _rope_slab_bwd(dq_acc[...] * MLA_SCALE, c_ref[...], s_ref[...]).astype(dq_ref.dtype)

        @pl.when(qi == nq - 1)
        def _():
            dk_ref[...] = dk_acc[...].T.astype(dk_ref.dtype)
            dv_ref[...] = dv_acc[...].T.astype(dv_ref.dtype)

    blk = pl.BlockSpec((bq, LANES), lambda h, i: (i, h))
    tab = pl.BlockSpec((bq, LANES), lambda h, i: (i, 0))
    k_full = pl.BlockSpec((s, LANES), lambda h, i: (0, h))
    v_full = pl.BlockSpec((s, LANES), lambda h, i: (0, heads + h))
    shape = jax.ShapeDtypeStruct((s, heads * LANES), BF16)
    ins = [q, kv, kv, o, do, lse, cos_t, sin_t] + ([dkv_init, dkv_init] if has_init else [])
    dq, dk, dv = pl.pallas_call(
        body, name=name, grid=(heads, nq),
        in_specs=[blk, k_full, v_full, blk, blk, blk, tab, tab] + ([k_full, v_full] if has_init else []),
        out_specs=[blk, k_full, k_full],
        out_shape=[shape, shape, shape],
        scratch_shapes=[pltpu.VMEM((bq, LANES), F32), pltpu.VMEM((LANES, s), F32), pltpu.VMEM((LANES, s), F32)],
        compiler_params=pltpu.CompilerParams(dimension_semantics=("arbitrary", "arbitrary"),
                                             vmem_limit_bytes=VMEM_LIMIT),
    )(*ins)
    return dq, jnp.concatenate([dk, dv], axis=1)


def _pad_last(a, width):
    return jnp.pad(a, [(0, 0)] * (a.ndim - 1) + [(0, width - a.shape[-1])])


def _pad_qkv(w, heads):
    d = w.shape[0]
    return _pad_last(w.reshape(d, 3 * heads, SB_HEAD_DIM), LANES).reshape(d, 3 * heads * LANES)


def _unpad_qkv(g, heads):
    d = g.shape[0]
    return g.reshape(d, 3 * heads, LANES)[:, :, :SB_HEAD_DIM].reshape(d, 3 * heads * SB_HEAD_DIM)


def _pad_o(w, heads):
    d = w.shape[1]
    w = w.reshape(heads, SB_HEAD_DIM, d)
    return jnp.pad(w, [(0, 0), (0, LANES - SB_HEAD_DIM), (0, 0)]).reshape(heads * LANES, d)


def _unpad_o(g, heads):
    d = g.shape[1]
    return g.reshape(heads, LANES, d)[:, :SB_HEAD_DIM, :].reshape(heads * SB_HEAD_DIM, d)


def _pad_uq(w, heads):
    r = w.shape[0]
    return _pad_last(w.reshape(r, heads, MLA_NOPE + MLA_ROPE), LANES).reshape(r, heads * LANES)


def _unpad_uq(g, heads):
    r = g.shape[0]
    return g.reshape(r, heads, LANES)[:, :, :MLA_NOPE + MLA_ROPE].reshape(r, heads * (MLA_NOPE + MLA_ROPE))


def _pad_dkv(w):
    d = w.shape[0]
    rope = jnp.zeros((d, LANES), w.dtype).at[:, ROPE_LO:ROPE_LO + MLA_ROPE].set(w[:, MLA_KV_RANK:])
    return jnp.concatenate([w[:, :MLA_KV_RANK], rope], axis=1)


def _unpad_dkv(g):
    return jnp.concatenate([g[:, :MLA_KV_RANK], g[:, MLA_KV_RANK + ROPE_LO:MLA_KV_RANK + ROPE_LO + MLA_ROPE]], axis=1)


def _pad_ukv(w, heads):
    w = w.reshape(MLA_KV_RANK, heads, 2, MLA_NOPE)
    k_part = _pad_last(w[:, :, 0, :], LANES).reshape(MLA_KV_RANK, heads * LANES)
    v_part = _pad_last(w[:, :, 1, :], LANES).reshape(MLA_KV_RANK, heads * LANES)
    lane = jnp.arange(LANES)
    place = ((lane[:, None] == lane[None, :]) & (lane[:, None] >= ROPE_LO) & (lane[:, None] < ROPE_LO + MLA_ROPE))
    place = jnp.tile(place.astype(w.dtype), (1, heads))
    top = jnp.concatenate([k_part, v_part], axis=1)
    bottom = jnp.concatenate([place, jnp.zeros_like(place)], axis=1)
    return jnp.concatenate([top, bottom], axis=0)


def _unpad_ukv(g, heads):
    g = g[:MLA_KV_RANK]
    k_part = g[:, :heads * LANES].reshape(MLA_KV_RANK, heads, LANES)[:, :, :MLA_NOPE]
    v_part = g[:, heads * LANES:].reshape(MLA_KV_RANK, heads, LANES)[:, :, :MLA_V]
    return jnp.stack([k_part, v_part], axis=2).reshape(MLA_KV_RANK, heads * (MLA_NOPE + MLA_V))


def _rope_tables(positions):
    inv_freq = ROPE_THETA ** (-jnp.arange(0, MLA_ROPE, 2, dtype=F32) / MLA_ROPE)
    ang = positions.astype(F32)[:, None] * inv_freq
    cos, sin = jnp.cos(ang), jnp.sin(ang)
    s = positions.shape[0]
    cos_t = jnp.ones((s, LANES), F32).at[:, ROPE_LO:ROPE_LO + MLA_ROPE].set(jnp.concatenate([cos, cos], axis=1))
    sin_t = jnp.zeros((s, LANES), F32).at[:, ROPE_LO:ROPE_LO + MLA_ROPE].set(jnp.concatenate([-sin, sin], axis=1))
    return cos_t, sin_t


def _local_step(x, positions, target, w, norms):
    s, d = x.shape
    heads = d // SB_HEAD_DIM
    n_a = w["sb_w_qkv"].shape[0]
    n_b = w["mla_w_dq"].shape[0]
    depth = n_a + n_b
    cos_t, sin_t = _rope_tables(positions)

    wqkv = [_pad_qkv(w["sb_w_qkv"][l], heads) for l in range(n_a)]
    wo_a = [_pad_o(w["sb_w_o"][l], heads) for l in range(n_a)]
    wdkv = _pad_dkv(w["mla_w_dkv"])
    wkv = _pad_ukv(w["mla_w_ukv"], heads)
    wdq = [w["mla_w_dq"][j] for j in range(n_b)]
    wuq = [_pad_uq(w["mla_w_uq"][j], heads) for j in range(n_b)]
    wo_b = [_pad_o(w["mla_w_o"][j], heads) for j in range(n_b)]
    w1 = [w["mlp_w1"][l] for l in range(depth)]
    w2 = [w["mlp_w2"][l] for l in range(depth)]

    saved = []
    kv_saved = None
    kv = None
    for l in range(depth):
        t = f"l{l}"
        sv = {"x_in": x}
        h = _rms_fwd(x, norms["attn_norm"][l], name=f"{t}_attn_norm")
        sv["h"] = h
        if l < n_a:
            qkv = _mm(h, wqkv[l], name=f"{t}_qkv")
            o = _sb_fwd(qkv, heads, name=f"{t}_sb_fwd")
            sv["qkv"], sv["o"] = qkv, o
            x = _mm(o, wo_a[l], name=f"{t}_attn_out", epilogue=_epi_add, extras=[(x, "tile")], out_dtypes=(F32,))
        else:
            j = l - n_a
            if j == 0:
                hk = _rms_fwd(x, norms["kv_norm"], name="kv_norm")
                down = _mm(hk, wdkv, name="kv_down", out_dtypes=(F32,))
                cat = _kv_prep(down, norms["mla_kv_lat_norm"], cos_t, sin_t, name="kv_prep")
                kv = _mm(cat, wkv, name="kv_up")
                kv_saved = {"x_in": x, "hk": hk, "down": down, "cat": cat}
            cq0 = _mm(h, wdq[j], name=f"{t}_q_down", out_dtypes=(F32,))
            cq = _rms_fwd(cq0, norms["mla_q_lat_norm"][j], name=f"{t}_q_lat_norm")
            q = _mm(cq, wuq[j], name=f"{t}_q_up", epilogue=_epi_rope_heads, extras=[(cos_t, "row"), (sin_t, "row")])
            o, lse = _mla_fwd(q, kv, heads, name=f"{t}_mla_fwd")
            sv.update(cq0=cq0, cq=cq, q=q, o=o, lse=lse)
            x = _mm(o, wo_b[j], name=f"{t}_attn_out", epilogue=_epi_add, extras=[(x, "tile")], out_dtypes=(F32,))
        sv["x_mid"] = x
        h2 = _rms_fwd(x, norms["mlp_norm"][l], name=f"{t}_mlp_norm")
        u, a = _mm(h2, w1[l], name=f"{t}_mlp_up", epilogue=_epi_relu2, out_dtypes=(BF16, BF16))
        sv.update(h2=h2, u=u, a=a)
        x = _mm(a, w2[l], name=f"{t}_mlp_down", epilogue=_epi_add, extras=[(x, "tile")], out_dtypes=(F32,))
        saved.append(sv)

    loss_slab, dx, dxb, dg_final = _loss_bwd(x, norms["final_norm"], target, name="loss")
    loss = loss_slab[0, 0]

    g_attn_norm, g_mlp_norm = [None] * depth, [None] * depth
    g_qkv, g_o_a = [None] * n_a, [None] * n_a
    g_dq, g_uq, g_o_b, g_qlat = [None] * n_b, [None] * n_b, [None] * n_b, [None] * n_b
    g_w1, g_w2 = [None] * depth, [None] * depth
    dkv = None
    g_kv_norm = g_kv_lat = g_dkv = g_ukv = None

    for l in reversed(range(depth)):
        t = f"l{l}"
        sv = saved[l]
        du = _mm(dxb, w2[l], name=f"{t}_mlp_down_dx", dims="nt", epilogue=_epi_relu2_grad, extras=[(sv["u"], "tile")])
        g_w2[l] = _mm(sv["a"], dxb, name=f"{t}_mlp_down_dw", dims="tn", out_dtypes=(F32,))
        g_w1[l] = _mm(sv["h2"], du, name=f"{t}_mlp_up_dw", dims="tn", out_dtypes=(F32,))
        dh2 = _mm(du, w1[l], name=f"{t}_mlp_up_dx", dims="nt", out_dtypes=(F32,))
        dx, dxb, g_mlp_norm[l] = _rms_bwd(sv["x_mid"], norms["mlp_norm"][l], dh2, dx, name=f"{t}_mlp_norm_bwd")
        if l < n_a:
            do = _mm(dxb, wo_a[l], name=f"{t}_attn_out_dx", dims="nt")
            g_o_a[l] = _unpad_o(_mm(sv["o"], dxb, name=f"{t}_attn_out_dw", dims="tn", out_dtypes=(F32,)), heads)
            dq, dk, dv = _sb_bwd(sv["qkv"], sv["o"], do, heads, name=f"{t}_sb_bwd")
            dqkv = jnp.concatenate([dq, dk, dv], axis=1)
            g_qkv[l] = _unpad_qkv(_mm(sv["h"], dqkv, name=f"{t}_qkv_dw", dims="tn", out_dtypes=(F32,)), heads)
            dh = _mm(dqkv, wqkv[l], name=f"{t}_qkv_dx", dims="nt", out_dtypes=(F32,))
        else:
            j = l - n_a
            do = _mm(dxb, wo_b[j], name=f"{t}_attn_out_dx", dims="nt")
            g_o_b[j] = _unpad_o(_mm(sv["o"], dxb, name=f"{t}_attn_out_dw", dims="tn", out_dtypes=(F32,)), heads)
            dq, dkv = _mla_bwd(sv["q"], kv, sv["o"], do, sv["lse"], cos_t, sin_t, dkv, heads, name=f"{t}_mla_bwd")
            g_uq[j] = _unpad_uq(_mm(sv["cq"], dq, name=f"{t}_q_up_dw", dims="tn", out_dtypes=(F32,)), heads)
            dcq = _mm(dq, wuq[j], name=f"{t}_q_up_dx", dims="nt", out_dtypes=(F32,))
            _, dcq0, g_qlat[j] = _rms_bwd(sv["cq0"], norms["mla_q_lat_norm"][j], dcq, None, name=f"{t}_q_lat_norm_bwd")
            g_dq[j] = _mm(sv["h"], dcq0, name=f"{t}_q_down_dw", dims="tn", out_dtypes=(F32,))
            dh = _mm(dcq0, wdq[j], name=f"{t}_q_down_dx", dims="nt", out_dtypes=(F32,))
        dx, dxb, g_attn_norm[l] = _rms_bwd(sv["x_in"], norms["attn_norm"][l], dh, dx, name=f"{t}_attn_norm_bwd",
                                           lead_axis=(l == 0))
        if l == n_a:
            ks = kv_saved
            dcat = _mm(dkv, wkv, name="kv_up_dx", dims="nt", out_dtypes=(F32,))
            g_ukv = _unpad_ukv(_mm(ks["cat"], dkv, name="kv_up_dw", dims="tn", out_dtypes=(F32,)), heads)
            ddown, g_kv_lat = _kv_prep_bwd(ks["down"], norms["mla_kv_lat_norm"], cos_t, sin_t, dcat, name="kv_prep_bwd")
            g_dkv = _unpad_dkv(_mm(ks["hk"], ddown, name="kv_down_dw", dims="tn", out_dtypes=(F32,)))
            dhk = _mm(ddown, wdkv, name="kv_down_dx", dims="nt", out_dtypes=(F32,))
            dx, dxb, g_kv_norm = _rms_bwd(ks["x_in"], norms["kv_norm"], dhk, dx, name="kv_norm_bwd")

    grads = {
        "attn_norm": jnp.concatenate(g_attn_norm, axis=0), "mlp_norm": jnp.concatenate(g_mlp_norm, axis=0),
        "sb_w_qkv": g_qkv, "sb_w_o": g_o_a,
        "kv_norm": g_kv_norm[0], "mla_w_dkv": g_dkv, "mla_kv_lat_norm": g_kv_lat[0], "mla_w_ukv": g_ukv,
        "mla_w_dq": g_dq, "mla_q_lat_norm": jnp.concatenate(g_qlat, axis=0),
        "mla_w_uq": g_uq, "mla_w_o": g_o_b,
        "mlp_w1": g_w1, "mlp_w2": g_w2, "final_norm": dg_final[0],
    }
    return loss, dx, grads


def _flat_rows(n_elems):
    per_block = FLAT_COLS * FLAT_ROW_BLOCK * 2
    return -(-n_elems // per_block) * FLAT_ROW_BLOCK * 2


def _pack(arrays, dtype):
    flat = jnp.concatenate([a.reshape(-1).astype(dtype) for a in arrays])
    rows = _flat_rows(flat.shape[0])
    flat = jnp.pad(flat, (0, rows * FLAT_COLS - flat.shape[0]))
    return flat.reshape(rows, FLAT_COLS)


def _unpack(flat, shapes):
    flat = flat.reshape(-1)
    out, off = [], 0
    for shp in shapes:
        n = 1
        for v in shp:
            n *= v
        out.append(flat[off:off + n].reshape(shp))
        off += n
    return out


def _pack_small(arrays):
    rows = []
    for a in arrays:
        a = a.reshape(-1, a.shape[-1]) if a.shape[-1] == FLAT_COLS else a.reshape(1, -1)
        rows.append(_pad_last(a, FLAT_COLS))
    flat = jnp.concatenate(rows, axis=0)
    return jnp.pad(flat, [(0, -flat.shape[0] % 8), (0, 0)])


def _unpack_small(flat, shapes):
    out, row = [], 0
    for shp in shapes:
        if shp[-1] == FLAT_COLS:
            n = 1
            for v in shp[:-1]:
                n *= v
            out.append(flat[row:row + n].reshape(shp))
            row += n
        else:
            n = 1
            for v in shp:
                n *= v
            out.append(flat[row, :n].reshape(shp))
            row += 1
    return out


def _other_chips(x, y):
    return [(1 - x, y), (x, 1 - y), (1 - x, 1 - y)]


def _all_gather_chips(flat, *, name):
    rows, cols = flat.shape

    def body(x_ref, out_ref, send_sems, recv_sems, pass_send_sems, pass_recv_sems):
        x, y, c = lax.axis_index("x"), lax.axis_index("y"), lax.axis_index("c")
        me = 2 * x + y
        my_rows, sib_rows = _half_rows(rows)
        chips = _other_chips(x, y)
        sends = []
        for k, (px, py) in enumerate(chips):
            cp = pltpu.make_async_remote_copy(src_ref=x_ref.at[my_rows, :], dst_ref=out_ref.at[me, my_rows, :],
                                              send_sem=send_sems.at[k], recv_sem=recv_sems.at[k],
                                              device_id=(px, py, c), device_id_type=MESH)
            cp.start()
            sends.append(cp)
        for k, (px, py) in enumerate(chips):
            landed = out_ref.at[2 * px + py, my_rows, :]
            pltpu.make_async_remote_copy(src_ref=landed, dst_ref=landed, send_sem=send_sems.at[k],
                                         recv_sem=recv_sems.at[k], device_id=(px, py, c),
                                         device_id_type=MESH).wait_recv()
            cp = pltpu.make_async_remote_copy(src_ref=landed, dst_ref=landed, send_sem=pass_send_sems.at[k],
                                              recv_sem=pass_recv_sems.at[k], device_id=_sibling(),
                                              device_id_type=MESH)
            cp.start()
            sends.append(cp)
        for k, (px, py) in enumerate(chips):
            passed = out_ref.at[2 * px + py, sib_rows, :]
            pltpu.make_async_remote_copy(src_ref=passed, dst_ref=passed, send_sem=pass_send_sems.at[k],
                                         recv_sem=pass_recv_sems.at[k], device_id=_sibling(),
                                         device_id_type=MESH).wait_recv()
        for cp in sends:
            cp.wait_send()

    out = pl.pallas_call(
        body, name=name,
        in_specs=[pl.BlockSpec(memory_space=pltpu.HBM)],
        out_specs=pl.BlockSpec(memory_space=pltpu.HBM),
        out_shape=jax.ShapeDtypeStruct((N_CHIPS, rows, cols), flat.dtype),
        scratch_shapes=[pltpu.SemaphoreType.DMA((3,)), pltpu.SemaphoreType.DMA((3,)), pltpu.SemaphoreType.DMA((3,)),
                        pltpu.SemaphoreType.DMA((3,))],
        compiler_params=pltpu.CompilerParams(has_side_effects=True),
    )(flat)
    return out


def _exchange_chips(parts, *, name):
    def body(g_ref, out_ref, send_sems, recv_sems):
        x, y, c = lax.axis_index("x"), lax.axis_index("y"), lax.axis_index("c")
        me = 2 * x + y
        sends = []
        for k, (px, py) in enumerate(_other_chips(x, y)):
            cp = pltpu.make_async_remote_copy(src_ref=g_ref.at[2 * px + py], dst_ref=out_ref.at[me],
                                              send_sem=send_sems.at[k], recv_sem=recv_sems.at[k],
                                              device_id=(px, py, c), device_id_type=MESH)
            cp.start()
            sends.append(cp)
        for k, (px, py) in enumerate(_other_chips(x, y)):
            pltpu.make_async_remote_copy(src_ref=g_ref.at[me], dst_ref=out_ref.at[2 * px + py],
                                         send_sem=send_sems.at[k], recv_sem=recv_sems.at[k],
                                         device_id=(px, py, c), device_id_type=MESH).wait_recv()
        for cp in sends:
            cp.wait_send()

    out = pl.pallas_call(
        body, name=name,
        in_specs=[pl.BlockSpec(memory_space=pltpu.HBM)],
        out_specs=pl.BlockSpec(memory_space=pltpu.HBM),
        out_shape=jax.ShapeDtypeStruct(parts.shape, parts.dtype),
        scratch_shapes=[pltpu.SemaphoreType.DMA((3,)), pltpu.SemaphoreType.DMA((3,))],
        compiler_params=pltpu.CompilerParams(has_side_effects=True),
    )(parts)
    me = _my_chip()
    return lax.dynamic_update_index_in_dim(out, lax.dynamic_index_in_dim(parts, me, 0, keepdims=False), me, 0)


def _my_chip():
    return 2 * lax.axis_index("x") + lax.axis_index("y")


def _half_rows(rows):
    c = lax.axis_index("c")
    half = rows // 2
    return pl.ds(pl.multiple_of(c * half, 8), half), pl.ds(pl.multiple_of((1 - c) * half, 8), half)


def _sibling():
    return (lax.axis_index("x"), lax.axis_index("y"), 1 - lax.axis_index("c"))


def _pair_exchange(parts, *, name):
    n, rows, cols = parts.shape

    def body(p_ref, theirs_ref, send_sem, recv_sem):
        _, sib_rows = _half_rows(rows)
        cp = pltpu.make_async_remote_copy(src_ref=p_ref.at[:, sib_rows, :], dst_ref=theirs_ref, send_sem=send_sem,
                                          recv_sem=recv_sem, device_id=_sibling(), device_id_type=MESH)
        cp.start()
        cp.wait()

    half = rows // 2
    theirs = pl.pallas_call(
        body, name=name,
        in_specs=[pl.BlockSpec(memory_space=pltpu.HBM)],
        out_specs=pl.BlockSpec(memory_space=pltpu.HBM),
        out_shape=jax.ShapeDtypeStruct((n, half, cols), parts.dtype),
        scratch_shapes=[pltpu.SemaphoreType.DMA, pltpu.SemaphoreType.DMA],
        compiler_params=pltpu.CompilerParams(has_side_effects=True),
    )(parts)
    mine = lax.dynamic_slice_in_dim(parts, lax.axis_index("c") * half, half, axis=1)
    return mine, theirs


def _pair_sum(mine, theirs, *, name):
    n, rows, cols = mine.shape

    def body(a_ref, b_ref, o_ref):
        o_ref[...] = (a_ref[...].astype(F32) + b_ref[...].astype(F32)).astype(o_ref.dtype)

    blk = pl.BlockSpec((n, FLAT_ROW_BLOCK, cols), lambda i: (0, i, 0))
    return pl.pallas_call(
        body, name=name, grid=(rows // FLAT_ROW_BLOCK,),
        in_specs=[blk, blk], out_specs=blk, out_shape=jax.ShapeDtypeStruct(mine.shape, mine.dtype),
        compiler_params=pltpu.CompilerParams(dimension_semantics=("parallel",), vmem_limit_bytes=VMEM_LIMIT),
    )(mine, theirs)


def _sum_chips(parts, *, name):
    _, rows, cols = parts.shape

    def body(p_ref, o_ref):
        o_ref[...] = ((p_ref[0].astype(F32) + p_ref[1].astype(F32)) + p_ref[2].astype(F32)) + p_ref[3].astype(F32)

    return pl.pallas_call(
        body, name=name, grid=(rows // FLAT_ROW_BLOCK,),
        in_specs=[pl.BlockSpec((N_CHIPS, FLAT_ROW_BLOCK, cols), lambda i: (0, i, 0))],
        out_specs=pl.BlockSpec((FLAT_ROW_BLOCK, cols), lambda i: (i, 0)),
        out_shape=jax.ShapeDtypeStruct((rows, cols), F32),
        compiler_params=pltpu.CompilerParams(dimension_semantics=("parallel",), vmem_limit_bytes=VMEM_LIMIT),
    )(parts)


def _join_cores(half, *, name):
    rows2, cols = half.shape

    def body(h_ref, out_ref, send_sem, recv_sem):
        my_rows, sib_rows = _half_rows(2 * rows2)
        cp = pltpu.make_async_remote_copy(src_ref=h_ref, dst_ref=out_ref.at[my_rows, :], send_sem=send_sem,
                                          recv_sem=recv_sem, device_id=_sibling(), device_id_type=MESH)
        cp.start()
        cp.wait_send()
        pltpu.make_async_remote_copy(src_ref=h_ref, dst_ref=out_ref.at[sib_rows, :], send_sem=send_sem,
                                     recv_sem=recv_sem, device_id=_sibling(), device_id_type=MESH).wait_recv()

    out = pl.pallas_call(
        body, name=name,
        in_specs=[pl.BlockSpec(memory_space=pltpu.HBM)],
        out_specs=pl.BlockSpec(memory_space=pltpu.HBM),
        out_shape=jax.ShapeDtypeStruct((2 * rows2, cols), half.dtype),
        scratch_shapes=[pltpu.SemaphoreType.DMA, pltpu.SemaphoreType.DMA],
        compiler_params=pltpu.CompilerParams(has_side_effects=True),
    )(half)
    return lax.dynamic_update_slice_in_dim(out, half, lax.axis_index("c") * rows2, axis=0)


def _all_reduce_small(v, *, name):
    rows, cols = v.shape
    flips = [(fx, fy, fc) for fx in (0, 1) for fy in (0, 1) for fc in (0, 1)][1:]

    def body(v_ref, out_ref, gath_ref, send_sems, recv_sems):
        x, y, c = lax.axis_index("x"), lax.axis_index("y"), lax.axis_index("c")
        me = 4 * x + 2 * y + c
        gath_ref[me] = v_ref[...]
        peers = [((1 - x) if fx else x, (1 - y) if fy else y, (1 - c) if fc else c) for fx, fy, fc in flips]
        sends = []
        for k, peer in enumerate(peers):
            cp = pltpu.make_async_remote_copy(src_ref=v_ref, dst_ref=gath_ref.at[me], send_sem=send_sems.at[k],
                                              recv_sem=recv_sems.at[k], device_id=peer, device_id_type=MESH)
            cp.start()
            sends.append(cp)
        for k, (px, py, pc) in enumerate(peers):
            pltpu.make_async_remote_copy(src_ref=v_ref, dst_ref=gath_ref.at[4 * px + 2 * py + pc],
                                         send_sem=send_sems.at[k], recv_sem=recv_sems.at[k],
                                         device_id=(px, py, pc), device_id_type=MESH).wait_recv()
        for cp in sends:
            cp.wait_send()
        total = gath_ref[0]
        for k in range(1, 8):
            total = total + gath_ref[k]
        out_ref[...] = total

    total, _ = pl.pallas_call(
        body, name=name,
        in_specs=[pl.BlockSpec(memory_space=pltpu.VMEM)],
        out_specs=[pl.BlockSpec(memory_space=pltpu.VMEM), pl.BlockSpec(memory_space=pltpu.VMEM)],
        out_shape=[jax.ShapeDtypeStruct((rows, cols), v.dtype), jax.ShapeDtypeStruct((8, rows, cols), v.dtype)],
        scratch_shapes=[pltpu.SemaphoreType.DMA((7,)), pltpu.SemaphoreType.DMA((7,))],
        compiler_params=pltpu.CompilerParams(has_side_effects=True),
    )(v)
    return total


def _adamw(w, g_parts, m, v, *, name):
    rows, cols = w.shape
    br = min(FLAT_ROW_BLOCK, rows)
    n_parts = len(g_parts)

    def body(*refs):
        w_ref = refs[0]
        g_refs = refs[1:1 + n_parts]
        m_ref, v_ref = refs[1 + n_parts], refs[2 + n_parts]
        g_out, d_out, m_out, v_out = refs[-4:]
        g = g_refs[0][...]
        for r in g_refs[1:]:
            g = g + r[...]
        m_new = ADAM_B1 * m_ref[...] + (1.0 - ADAM_B1) * g
        v_new = ADAM_B2 * v_ref[...] + (1.0 - ADAM_B2) * jnp.square(g)
        m_hat = m_new / (1.0 - ADAM_B1 ** ADAM_STEP)
        v_hat = v_new / (1.0 - ADAM_B2 ** ADAM_STEP)
        g_out[...] = g
        d_out[...] = -ADAM_LR * (m_hat / (jnp.sqrt(v_hat) + ADAM_EPS) + ADAM_WD * w_ref[...])
        m_out[...] = m_new
        v_out[...] = v_new

    blk = pl.BlockSpec((br, cols), lambda i: (i, 0))
    shape = jax.ShapeDtypeStruct((rows, cols), F32)
    return pl.pallas_call(
        body, name=name, grid=(rows // br,),
        in_specs=[blk] * (3 + n_parts), out_specs=[blk] * 4, out_shape=[shape] * 4,
        compiler_params=pltpu.CompilerParams(dimension_semantics=("parallel",), vmem_limit_bytes=VMEM_LIMIT),
    )(w, *g_parts, m, v)


def _assemble(gathered_shards, name):
    return jnp.concatenate(gathered_shards, axis=SHARD_AXIS[name])


def _chip_shard(full, name, j):
    if isinstance(full, list):
        axis = SHARD_AXIS[name] - 1
        layers = full
    else:
        axis = SHARD_AXIS[name]
        layers = [full]
    n = layers[0].shape[axis] // N_CHIPS
    return [lax.slice_in_dim(g, j * n, (j + 1) * n, axis=axis) for g in layers]


def kernel(x, positions, attn_norm, mlp_norm, sb_w_qkv, sb_w_o, kv_norm, mla_w_dkv, mla_kv_lat_norm, mla_w_ukv, mla_w_dq, mla_q_lat_norm, mla_w_uq, mla_w_o, mlp_w1, mlp_w2, final_norm, loss_target, m_attn_norm, m_mlp_norm, m_sb_w_qkv, m_sb_w_o, m_kv_norm, m_mla_w_dkv, m_mla_kv_lat_norm, m_mla_w_ukv, m_mla_w_dq, m_mla_q_lat_norm, m_mla_w_uq, m_mla_w_o, m_mlp_w1, m_mlp_w2, m_final_norm, v_attn_norm, v_mlp_norm, v_sb_w_qkv, v_sb_w_o, v_kv_norm, v_mla_w_dkv, v_mla_kv_lat_norm, v_mla_w_ukv, v_mla_w_dq, v_mla_q_lat_norm, v_mla_w_uq, v_mla_w_o, v_mlp_w1, v_mlp_w2, v_final_norm):
    weights = dict(attn_norm=attn_norm, mlp_norm=mlp_norm, sb_w_qkv=sb_w_qkv, sb_w_o=sb_w_o, kv_norm=kv_norm,
                   mla_w_dkv=mla_w_dkv, mla_kv_lat_norm=mla_kv_lat_norm, mla_w_ukv=mla_w_ukv, mla_w_dq=mla_w_dq,
                   mla_q_lat_norm=mla_q_lat_norm, mla_w_uq=mla_w_uq, mla_w_o=mla_w_o, mlp_w1=mlp_w1, mlp_w2=mlp_w2,
                   final_norm=final_norm)
    m_in = dict(attn_norm=m_attn_norm, mlp_norm=m_mlp_norm, sb_w_qkv=m_sb_w_qkv, sb_w_o=m_sb_w_o, kv_norm=m_kv_norm,
                mla_w_dkv=m_mla_w_dkv, mla_kv_lat_norm=m_mla_kv_lat_norm, mla_w_ukv=m_mla_w_ukv, mla_w_dq=m_mla_w_dq,
                mla_q_lat_norm=m_mla_q_lat_norm, mla_w_uq=m_mla_w_uq, mla_w_o=m_mla_w_o, mlp_w1=m_mlp_w1,
                mlp_w2=m_mlp_w2, final_norm=m_final_norm)
    v_in = dict(attn_norm=v_attn_norm, mlp_norm=v_mlp_norm, sb_w_qkv=v_sb_w_qkv, sb_w_o=v_sb_w_o, kv_norm=v_kv_norm,
                mla_w_dkv=v_mla_w_dkv, mla_kv_lat_norm=v_mla_kv_lat_norm, mla_w_ukv=v_mla_w_ukv, mla_w_dq=v_mla_w_dq,
                mla_q_lat_norm=v_mla_q_lat_norm, mla_w_uq=v_mla_w_uq, mla_w_o=v_mla_w_o, mlp_w1=v_mlp_w1,
                mlp_w2=v_mlp_w2, final_norm=v_final_norm)
    shard_shapes = [weights[n].shape for n in BIG_WEIGHTS]
    small_shapes = [weights[n].shape for n in SMALL_WEIGHTS]

    flat_w = _pack([weights[n] for n in BIG_WEIGHTS], BF16)
    gathered = _all_gather_chips(flat_w, name="weights_all_gather")
    per_chip = [_unpack(gathered[j], shard_shapes) for j in range(N_CHIPS)]
    full_w = {n: _assemble([per_chip[j][i] for j in range(N_CHIPS)], n) for i, n in enumerate(BIG_WEIGHTS)}
    for n in BIG_WEIGHTS:
        axis = SHARD_AXIS[n]
        full_w[n] = lax.dynamic_update_slice_in_dim(full_w[n], weights[n].astype(BF16),
                                                    _my_chip() * weights[n].shape[axis], axis=axis)
    norms = {n: weights[n] for n in SMALL_WEIGHTS}

    loss, dx, grads = _local_step(x[0], positions[0], loss_target, full_w, norms)
    loss = lax.psum(loss, ("x", "y", "c"))

    parts = jnp.stack([_pack([piece for n in BIG_WEIGHTS for piece in _chip_shard(grads[n], n, j)], BF16)
                       for j in range(N_CHIPS)])
    mine, theirs = _pair_exchange(parts, name="grads_pair_exchange")
    chip_part = _pair_sum(mine, theirs, name="grads_pair_sum")
    received = _exchange_chips(chip_part, name="grads_exchange")
    g_half = _sum_chips(received, name="grads_sum_chips")
    g_sum = _join_cores(g_half, name="grads_join_cores")
    g_flat, d_flat, m_flat, v_flat = _adamw(
        _pack([weights[n] for n in BIG_WEIGHTS], F32), [g_sum],
        _pack([m_in[n] for n in BIG_WEIGHTS], F32), _pack([v_in[n] for n in BIG_WEIGHTS], F32), name="adamw_big")
    out_g = dict(zip(BIG_WEIGHTS, _unpack(g_flat, shard_shapes)))
    out_d = dict(zip(BIG_WEIGHTS, _unpack(d_flat, shard_shapes)))
    out_m = dict(zip(BIG_WEIGHTS, _unpack(m_flat, shard_shapes)))
    out_v = dict(zip(BIG_WEIGHTS, _unpack(v_flat, shard_shapes)))

    small_sum = _all_reduce_small(_pack_small([grads[n] for n in SMALL_WEIGHTS]), name="gains_all_reduce")
    sg, sd, sm, sv = _adamw(_pack_small([weights[n] for n in SMALL_WEIGHTS]), [small_sum],
                            _pack_small([m_in[n] for n in SMALL_WEIGHTS]),
                            _pack_small([v_in[n] for n in SMALL_WEIGHTS]), name="adamw_gains")
    out_g.update(zip(SMALL_WEIGHTS, _unpack_small(sg, small_shapes)))
    out_d.update(zip(SMALL_WEIGHTS, _unpack_small(sd, small_shapes)))
    out_m.update(zip(SMALL_WEIGHTS, _unpack_small(sm, small_shapes)))
    out_v.update(zip(SMALL_WEIGHTS, _unpack_small(sv, small_shapes)))

    return (loss, dx, *[out_g[n] for n in ALL_WEIGHTS], *[out_d[n] for n in ALL_WEIGHTS],
            *[out_m[n] for n in ALL_WEIGHTS], *[out_v[n] for n in ALL_WEIGHTS])
```

```python
import functools

import jax
import jax.numpy as jnp
from jax import lax
from jax.experimental import pallas as pl
from jax.experimental.pallas import tpu as pltpu

F32 = jnp.float32
BF16 = jnp.bfloat16

LANES = 128
SB_HEAD_DIM = 64
MLA_NOPE = 64
MLA_ROPE = 32
MLA_V = 64
MLA_Q_RANK = 384
MLA_KV_RANK = 256
CHUNK = 64
ROPE_THETA = 10000.0
NORM_EPS = 1e-6
SB_SCALE = SB_HEAD_DIM ** -0.5
MLA_SCALE = (MLA_NOPE + MLA_ROPE) ** -0.5
ROPE_LO = MLA_NOPE
ROPE_HALF = MLA_ROPE // 2
ATT_Q_BLOCK = 1024
ATT_K_BLOCK = 256
MLA_FWD_K_BLOCK = 512
NEG_BIG = -1e30
SB_DEAD_LOG = -110.0
VMEM_LIMIT = 56 * 1024 * 1024

ADAM_LR = 0.001
ADAM_B1 = 0.9
ADAM_B2 = 0.999
ADAM_EPS = 1e-08
ADAM_WD = 0.01
ADAM_STEP = 10

FLAT_COLS = 1024
FLAT_ROW_BLOCK = 256
N_CHIPS = 4
MESH = pl.DeviceIdType.MESH

BIG_WEIGHTS = ["sb_w_qkv", "sb_w_o", "mla_w_dkv", "mla_w_ukv", "mla_w_dq", "mla_w_uq", "mla_w_o", "mlp_w1", "mlp_w2"]
SHARD_AXIS = {"sb_w_qkv": 2, "sb_w_o": 1, "mla_w_dkv": 0, "mla_w_ukv": 1, "mla_w_dq": 1, "mla_w_uq": 2,
              "mla_w_o": 1, "mlp_w1": 2, "mlp_w2": 1}
SMALL_WEIGHTS = ["attn_norm", "mlp_norm", "kv_norm", "mla_kv_lat_norm", "mla_q_lat_norm", "final_norm"]
ALL_WEIGHTS = ["attn_norm", "mlp_norm", "sb_w_qkv", "sb_w_o", "kv_norm", "mla_w_dkv", "mla_kv_lat_norm", "mla_w_ukv",
               "mla_w_dq", "mla_q_lat_norm", "mla_w_uq", "mla_w_o", "mlp_w1", "mlp_w2", "final_norm"]


def _dot(a, b, dims):
    return lax.dot_general(a, b, (dims, ((), ())), preferred_element_type=F32)


def _dot_nn(a, b):
    return _dot(a, b, ((1,), (0,)))


def _dot_nt(a, b):
    return _dot(a, b, ((1,), (1,)))


def _dot_tn(a, b):
    return _dot(a, b, ((0,), (0,)))


def _pick_block(n, target):
    if n <= target:
        return n
    best = max(b for b in range(LANES, target + 1, LANES) if n % b == 0)
    return best


MM_ROWS = 512
MM_COLS = 1024
MM_DEPTH = 4096
MM_DEPTH_TN = 1024


def _mm(a, b, *, name, dims="nn", epilogue=None, extras=(), out_dtypes=(BF16,)):
    if dims == "nn":
        (m, k), (k2, n) = a.shape, b.shape
    elif dims == "nt":
        (m, k), (n, k2) = a.shape, b.shape
    else:
        (k, m), (k2, n) = a.shape, b.shape
    assert k == k2, (name, a.shape, b.shape)
    if dims == "tn":
        bm, bn, bk = _pick_block(m, MM_COLS), _pick_block(n, MM_COLS), _pick_block(k, MM_DEPTH_TN)
    else:
        rows = MM_ROWS if k > MM_DEPTH // 2 else 2 * MM_ROWS
        bm, bn, bk = _pick_block(m, rows), _pick_block(n, MM_COLS), _pick_block(k, MM_DEPTH)
    nk = k // bk
    if dims == "tn":
        a_spec = pl.BlockSpec((bk, bm), lambda j, i, kk: (kk, i))
    else:
        a_spec = pl.BlockSpec((bm, bk), lambda j, i, kk: (i, kk))
    if dims == "nt":
        b_spec = pl.BlockSpec((bn, bk), lambda j, i, kk: (j, kk))
    else:
        b_spec = pl.BlockSpec((bk, bn), lambda j, i, kk: (kk, j))
    extra_specs = []
    for arr, kind in extras:
        if kind == "tile":
            assert arr.shape == (m, n), (name, arr.shape)
            extra_specs.append(pl.BlockSpec((bm, bn), lambda j, i, kk: (i, j)))
        else:
            assert arr.shape == (m, LANES), (name, arr.shape)
            extra_specs.append(pl.BlockSpec((bm, LANES), lambda j, i, kk: (i, 0)))
    n_extra = len(extras)
    n_out = len(out_dtypes)
    dot = {"nn": _dot_nn, "nt": _dot_nt, "tn": _dot_tn}[dims]

    def body(*refs):
        a_ref, b_ref = refs[0], refs[1]
        extra_refs = refs[2:2 + n_extra]
        out_refs = refs[2 + n_extra:2 + n_extra + n_out]

        def finish(acc):
            outs = (acc,) if epilogue is None else epilogue(acc, *[r[...] for r in extra_refs])
            for o_ref, o in zip(out_refs, outs):
                o_ref[...] = o.astype(o_ref.dtype)

        part = dot(a_ref[...].astype(BF16), b_ref[...].astype(BF16))
        if nk == 1:
            finish(part)
            return
        acc_ref = refs[-1]
        kk = pl.program_id(2)

        @pl.when(kk == 0)
        def _():
            acc_ref[...] = part

        @pl.when(kk > 0)
        def _():
            acc_ref[...] += part

        @pl.when(kk == nk - 1)
        def _():
            finish(acc_ref[...])

    outs = pl.pallas_call(
        body, name=name, grid=(n // bn, m // bm, nk),
        in_specs=[a_spec, b_spec] + extra_specs,
        out_specs=[pl.BlockSpec((bm, bn), lambda j, i, kk: (i, j)) for _ in range(n_out)],
        out_shape=[jax.ShapeDtypeStruct((m, n), dt) for dt in out_dtypes],
        scratch_shapes=[pltpu.VMEM((bm, bn), F32)] if nk > 1 else [],
        compiler_params=pltpu.CompilerParams(dimension_semantics=("parallel", "parallel", "arbitrary"),
                                             vmem_limit_bytes=VMEM_LIMIT),
    )(a, b, *[arr for arr, _ in extras])
    return outs[0] if n_out == 1 else outs


def _epi_add(acc, res):
    return (res + acc,)


def _epi_relu2(acc):
    r = jnp.maximum(acc, 0.0)
    return acc, r * r


def _epi_relu2_grad(acc, u):
    return (acc * (2.0 * jnp.maximum(u.astype(F32), 0.0)),)


def _rope_slab(t, cos_t, sin_t):
    lane = lax.broadcasted_iota(jnp.int32, t.shape, 1)
    partner = jnp.where(lane < ROPE_LO + ROPE_HALF, pltpu.roll(t, LANES - ROPE_HALF, 1), pltpu.roll(t, ROPE_HALF, 1))
    return t * cos_t + partner * sin_t


def _rope_slab_bwd(d, cos_t, sin_t):
    ds = d * sin_t
    lane = lax.broadcasted_iota(jnp.int32, d.shape, 1)
    partner = jnp.where(lane < ROPE_LO + ROPE_HALF, pltpu.roll(ds, LANES - ROPE_HALF, 1), pltpu.roll(ds, ROPE_HALF, 1))
    in_rope = (lane >= ROPE_LO) & (lane < ROPE_LO + MLA_ROPE)
    return d * cos_t + jnp.where(in_rope, partner, 0.0)


def _epi_rope_heads(acc, cos_t, sin_t):
    slabs = [_rope_slab(acc[:, j * LANES:(j + 1) * LANES], cos_t, sin_t) for j in range(acc.shape[1] // LANES)]
    return (jnp.concatenate(slabs, axis=1) * MLA_SCALE,)


def _row_block(s):
    return min(512, s)


def _rms_fwd(x, g, *, name):
    s, d = x.shape
    bm = _row_block(s)

    def body(x_ref, g_ref, o_ref):
        xv = x_ref[...]
        r = lax.rsqrt(jnp.mean(xv * xv, axis=-1, keepdims=True) + NORM_EPS)
        o_ref[...] = ((xv * r) * g_ref[...]).astype(o_ref.dtype)

    return pl.pallas_call(
        body, name=name, grid=(s // bm,),
        in_specs=[pl.BlockSpec((bm, d), lambda i: (i, 0)), pl.BlockSpec((1, d), lambda i: (0, 0))],
        out_specs=pl.BlockSpec((bm, d), lambda i: (i, 0)),
        out_shape=jax.ShapeDtypeStruct((s, d), BF16),
        compiler_params=pltpu.CompilerParams(dimension_semantics=("parallel",), vmem_limit_bytes=VMEM_LIMIT),
    )(x, g.reshape(1, d))


def _rms_bwd_math(xv, gv, dy):
    r = lax.rsqrt(jnp.mean(xv * xv, axis=-1, keepdims=True) + NORM_EPS)
    xhat = xv * r
    dyg = dy * gv
    mdot = jnp.mean(dyg * xhat, axis=-1, keepdims=True)
    dx = r * (dyg - xhat * mdot)
    dg = jnp.sum(dy * xhat, axis=0, keepdims=True)
    return dx, dg


def _rms_bwd(x, g, dy, dres, *, name, lead_axis=False):
    s, d = x.shape
    bm = _row_block(s)
    has_res = dres is not None

    def body(*refs):
        x_ref, g_ref, dy_ref = refs[:3]
        dres_ref = refs[3] if has_res else None
        dx_ref, dxb_ref, dg_ref = refs[-3:]
        dx, dg = _rms_bwd_math(x_ref[...], g_ref[...], dy_ref[...].astype(F32))
        if has_res:
            dx = dx + dres_ref[...]
        dx_ref[...] = dx
        dxb_ref[...] = dx.astype(BF16)

        @pl.when(pl.program_id(0) == 0)
        def _():
            dg_ref[...] = jnp.zeros_like(dg_ref)

        dg_ref[...] += dg

    row = pl.BlockSpec((bm, d), lambda i: (i, 0))
    vec = pl.BlockSpec((1, d), lambda i: (0, 0))
    ins = [x, g.reshape(1, d), dy] + ([dres] if has_res else [])
    dx_spec, dx_shape = row, (s, d)
    if lead_axis:
        dx_spec, dx_shape = pl.BlockSpec((None, bm, d), lambda i: (0, i, 0)), (1, s, d)
    return pl.pallas_call(
        body, name=name, grid=(s // bm,),
        in_specs=[row, vec, row] + ([row] if has_res else []),
        out_specs=[dx_spec, row, vec],
        out_shape=[jax.ShapeDtypeStruct(dx_shape, F32), jax.ShapeDtypeStruct((s, d), BF16),
                   jax.ShapeDtypeStruct((1, d), F32)],
        compiler_params=pltpu.CompilerParams(dimension_semantics=("arbitrary",), vmem_limit_bytes=VMEM_LIMIT),
    )(*ins)


def _loss_bwd(x, g, target, *, name):
    s, d = x.shape
    bm = _row_block(s)

    def body(x_ref, g_ref, t_ref, loss_ref, dx_ref, dxb_ref, dg_ref):
        xv, gv = x_ref[...], g_ref[...]
        r = lax.rsqrt(jnp.mean(xv * xv, axis=-1, keepdims=True) + NORM_EPS)
        err = (xv * r) * gv - t_ref[...]
        dx, dg = _rms_bwd_math(xv, gv, err * (1.0 / d))
        dx_ref[...] = dx
        dxb_ref[...] = dx.astype(BF16)

        @pl.when(pl.program_id(0) == 0)
        def _():
            dg_ref[...] = jnp.zeros_like(dg_ref)
            loss_ref[...] = jnp.zeros_like(loss_ref)

        dg_ref[...] += dg
        loss_ref[...] += jnp.sum(jnp.mean(err * err, axis=-1, keepdims=True), axis=0, keepdims=True) * 0.5

    row = pl.BlockSpec((bm, d), lambda i: (i, 0))
    vec = pl.BlockSpec((1, d), lambda i: (0, 0))
    assert target.shape == (1, s, d), target.shape
    return pl.pallas_call(
        body, name=name, grid=(s // bm,),
        in_specs=[row, vec, pl.BlockSpec((None, bm, d), lambda i: (0, i, 0))],
        out_specs=[pl.BlockSpec((8, LANES), lambda i: (0, 0)), row, row, vec],
        out_shape=[jax.ShapeDtypeStruct((8, LANES), F32), jax.ShapeDtypeStruct((s, d), F32),
                   jax.ShapeDtypeStruct((s, d), BF16), jax.ShapeDtypeStruct((1, d), F32)],
        compiler_params=pltpu.CompilerParams(dimension_semantics=("arbitrary",), vmem_limit_bytes=VMEM_LIMIT),
    )(x, g.reshape(1, d), target)


def _kv_prep(down, g, cos_t, sin_t, *, name):
    s, w = down.shape
    bm = _row_block(s)

    def body(d_ref, g_ref, c_ref, s_ref, o_ref):
        lat = d_ref[:, :MLA_KV_RANK]
        r = lax.rsqrt(jnp.mean(lat * lat, axis=-1, keepdims=True) + NORM_EPS)
        o_ref[:, :MLA_KV_RANK] = ((lat * r) * g_ref[...]).astype(BF16)
        o_ref[:, MLA_KV_RANK:] = _rope_slab(d_ref[:, MLA_KV_RANK:], c_ref[...], s_ref[...]).astype(BF16)

    row = pl.BlockSpec((bm, w), lambda i: (i, 0))
    tab = pl.BlockSpec((bm, LANES), lambda i: (i, 0))
    return pl.pallas_call(
        body, name=name, grid=(s // bm,),
        in_specs=[row, pl.BlockSpec((1, MLA_KV_RANK), lambda i: (0, 0)), tab, tab],
        out_specs=row, out_shape=jax.ShapeDtypeStruct((s, w), BF16),
        compiler_params=pltpu.CompilerParams(dimension_semantics=("parallel",), vmem_limit_bytes=VMEM_LIMIT),
    )(down, g.reshape(1, MLA_KV_RANK), cos_t, sin_t)


def _kv_prep_bwd(down, g, cos_t, sin_t, dcat, *, name):
    s, w = down.shape
    bm = _row_block(s)

    def body(d_ref, g_ref, c_ref, s_ref, dc_ref, o_ref, dg_ref):
        dlat, dg = _rms_bwd_math(d_ref[:, :MLA_KV_RANK], g_ref[...], dc_ref[:, :MLA_KV_RANK])
        o_ref[:, :MLA_KV_RANK] = dlat.astype(BF16)
        o_ref[:, MLA_KV_RANK:] = _rope_slab_bwd(dc_ref[:, MLA_KV_RANK:], c_ref[...], s_ref[...]).astype(BF16)

        @pl.when(pl.program_id(0) == 0)
        def _():
            dg_ref[...] = jnp.zeros_like(dg_ref)

        dg_ref[...] += dg

    row = pl.BlockSpec((bm, w), lambda i: (i, 0))
    tab = pl.BlockSpec((bm, LANES), lambda i: (i, 0))
    vec = pl.BlockSpec((1, MLA_KV_RANK), lambda i: (0, 0))
    return pl.pallas_call(
        body, name=name, grid=(s // bm,),
        in_specs=[row, vec, tab, tab, row],
        out_specs=[row, vec],
        out_shape=[jax.ShapeDtypeStruct((s, w), BF16), jax.ShapeDtypeStruct((1, MLA_KV_RANK), F32)],
        compiler_params=pltpu.CompilerParams(dimension_semantics=("arbitrary",), vmem_limit_bytes=VMEM_LIMIT),
    )(down, g.reshape(1, MLA_KV_RANK), cos_t, sin_t, dcat)


def _split_bf16(v):
    hi = v.astype(BF16)
    lo = (v - hi.astype(F32)).astype(BF16)
    return hi, lo


def _suffix_matrices(n):
    row = lax.broadcasted_iota(jnp.int32, (n, n), 0)
    col = lax.broadcasted_iota(jnp.int32, (n, n), 1)
    incl = (row >= col).astype(BF16)
    return (row > col).astype(BF16), jnp.concatenate([incl, incl], axis=0)


def _suffix_sum(v, matrix):
    hi, lo = _split_bf16(v)
    return _dot_nn(jnp.concatenate([hi, lo], axis=1), matrix)


def _block_positions(qi, kb, bq, bk, r0, r1):
    row = qi * bq + r0 + lax.broadcasted_iota(jnp.int32, (r1 - r0, bk), 0)
    col = kb * bk + lax.broadcasted_iota(jnp.int32, (r1 - r0, bk), 1)
    return row, col


def _att_blocks(s, key_block=ATT_K_BLOCK):
    bq, bk = min(ATT_Q_BLOCK, s), min(key_block, s)
    return bq, bk, s // bq, bq // bk


def _sweep(qi, ratio, bk, step, unroll=2, alive=None):
    bq = ratio * bk
    for d in range(ratio):
        kb, r0 = (qi + 1) * ratio - 1 - d, (ratio - 1 - d) * bk
        near = bq if alive is None else min(r0 + 2 * bk, bq)
        step(kb, True, r0, near)
        if near < bq:
            pl.when(alive(near))(functools.partial(step, kb, False, near, bq))
    unroll = unroll if ratio % unroll == 0 else 1
    trips = qi * (ratio // unroll)

    def trip(i):
        for u in range(unroll):
            step(qi * ratio - 1 - (i * unroll + u), False, 0, bq)

    if alive is None:
        lax.fori_loop(0, trips, lambda i, carry: (trip(i), carry)[1], 0)
    else:
        lax.while_loop(lambda i: jnp.logical_and(i < trips, alive(0)), lambda i: (trip(i), i + 1)[1], 0)


def _stick_left(c_ref, r0):
    return jnp.max(c_ref[r0:, :]) > SB_DEAD_LOG


def _sb_logs(q, k):
    z = _dot_nt(q, k)
    lb = jnp.minimum(z, 0.0) - jnp.log(1.0 + jnp.exp(-jnp.abs(z)))
    return lb, lb - z


def _sb_fwd(qkv, heads, *, name):
    s = qkv.shape[0]
    bq, bk, nq, ratio = _att_blocks(s)

    def body(q_ref, k_ref, v_ref, o_ref, acc_ref, c_ref):
        qi = pl.program_id(1)
        q = q_ref[...] * SB_SCALE
        m_strict, _ = _suffix_matrices(bk)
        acc_ref[...] = jnp.zeros_like(acc_ref)
        c_ref[...] = jnp.zeros_like(c_ref)

        def step(kb, masked, r0, r1):
            rows = pl.ds(pl.multiple_of(kb * bk, bk), bk)
            mine = pl.ds(r0, r1 - r0)
            k, v = k_ref[rows, :], v_ref[rows, :]
            lb, lk = _sb_logs(q[r0:r1], k)
            if masked:
                row, col = _block_positions(qi, kb, bq, bk, r0, r1)
                causal = col < row
                lk = jnp.where(causal, lk, 0.0)
            c = c_ref[mine, :]
            w = jnp.exp(lb + _dot_nn(lk.astype(BF16), m_strict) + jnp.tile(c, (1, bk // LANES)))
            if masked:
                w = jnp.where(causal, w, 0.0)
            acc_ref[mine, :] += _dot_nn(w.astype(BF16), v)
            c_ref[mine, :] = c + jnp.sum(lk, axis=-1, keepdims=True)

        _sweep(qi, ratio, bk, step, unroll=1, alive=functools.partial(_stick_left, c_ref))
        o_ref[...] = acc_ref[...].astype(o_ref.dtype)

    return pl.pallas_call(
        body, name=name, grid=(heads, nq),
        in_specs=[pl.BlockSpec((bq, LANES), lambda h, i: (i, h)),
                  pl.BlockSpec((s, LANES), lambda h, i: (0, heads + h)),
                  pl.BlockSpec((s, LANES), lambda h, i: (0, 2 * heads + h))],
        out_specs=pl.BlockSpec((bq, LANES), lambda h, i: (i, h)),
        out_shape=jax.ShapeDtypeStruct((s, heads * LANES), F32),
        scratch_shapes=[pltpu.VMEM((bq, LANES), F32), pltpu.VMEM((bq, LANES), F32)],
        compiler_params=pltpu.CompilerParams(dimension_semantics=("parallel", "arbitrary"),
                                             vmem_limit_bytes=VMEM_LIMIT),
    )(qkv, qkv, qkv)


def _sb_bwd(qkv, o, do, heads, *, name):
    s = qkv.shape[0]
    bq, bk, nq, ratio = _att_blocks(s)

    def body(q_ref, k_ref, v_ref, o_ref, do_ref, dq_ref, dk_ref, dv_ref, dq_acc, dk_acc, dv_acc, c_ref, e_ref):
        qi = pl.program_id(1)

        @pl.when(qi == 0)
        def _():
            dk_acc[...] = jnp.zeros_like(dk_acc)
            dv_acc[...] = jnp.zeros_like(dv_acc)

        q = q_ref[...] * SB_SCALE
        do = do_ref[...]
        q_t, do_t = q.T, do.T
        total = jnp.sum(do.astype(F32) * o_ref[...].astype(F32), axis=-1, keepdims=True)
        m_strict, m_incl = _suffix_matrices(bk)
        dq_acc[...] = jnp.zeros_like(dq_acc)
        c_ref[...] = jnp.zeros_like(c_ref)
        e_ref[...] = jnp.broadcast_to(total, e_ref.shape)
        reps = (1, bk // LANES)

        def step(kb, masked, r0, r1):
            rows = pl.ds(pl.multiple_of(kb * bk, bk), bk)
            mine = pl.ds(r0, r1 - r0)
            k, v = k_ref[rows, :], v_ref[rows, :]
            qs, dos = q[r0:r1], do[r0:r1]
            lb, lk_all = _sb_logs(qs, k)
            lk = lk_all
            if masked:
                row, col = _block_positions(qi, kb, bq, bk, r0, r1)
                causal = col < row
                lk = jnp.where(causal, lk_all, 0.0)
            c = c_ref[mine, :]
            w = jnp.exp(lb + _dot_nn(lk.astype(BF16), m_strict) + jnp.tile(c, reps))
            if masked:
                w = jnp.where(causal, w, 0.0)
            wb = w.astype(BF16)
            g = wb.astype(F32) * _dot_nt(dos, v)
            e = e_ref[mine, :]
            g_left = jnp.tile(e, reps) - _suffix_sum(g, m_incl)
            da = g * jnp.exp(lk_all) - jnp.exp(lb) * g_left
            if masked:
                da = jnp.where(causal, da, 0.0)
            dab = da.astype(BF16)
            dq_acc[mine, :] += _dot_nn(dab, k)
            dk_acc[:, rows] += _dot_nn(q_t[:, r0:r1], dab)
            dv_acc[:, rows] += _dot_nn(do_t[:, r0:r1], wb)
            e_ref[mine, :] = e - jnp.sum(g, axis=-1, keepdims=True)
            c_ref[mine, :] = c + jnp.sum(lk, axis=-1, keepdims=True)

        _sweep(qi, ratio, bk, step, unroll=1, alive=functools.partial(_stick_left, c_ref))
        dq_ref[...] = (dq_acc[...] * SB_SCALE).astype(dq_ref.dtype)

        @pl.when(qi == nq - 1)
        def _():
            dk_ref[...] = dk_acc[...].T.astype(dk_ref.dtype)
            dv_ref[...] = dv_acc[...].T.astype(dv_ref.dtype)

    blk = pl.BlockSpec((bq, LANES), lambda h, i: (i, h))
    full = pl.BlockSpec((s, LANES), lambda h, i: (0, h))
    shape = jax.ShapeDtypeStruct((s, heads * LANES), BF16)
    return pl.pallas_call(
        body, name=name, grid=(heads, nq),
        in_specs=[blk,
                  pl.BlockSpec((s, LANES), lambda h, i: (0, heads + h)),
                  pl.BlockSpec((s, LANES), lambda h, i: (0, 2 * heads + h)),
                  blk, blk],
        out_specs=[blk, full, full],
        out_shape=[shape, shape, shape],
        scratch_shapes=[pltpu.VMEM((bq, LANES), F32), pltpu.VMEM((LANES, s), F32), pltpu.VMEM((LANES, s), F32),
                        pltpu.VMEM((bq, LANES), F32), pltpu.VMEM((bq, LANES), F32)],
        compiler_params=pltpu.CompilerParams(dimension_semantics=("arbitrary", "arbitrary"),
                                             vmem_limit_bytes=VMEM_LIMIT),
    )(qkv, qkv, qkv, o, do)


def _chunk_allowed(qi, kb, bq, bk, r0, r1):
    row, col = _block_positions(qi, kb, bq, bk, r0, r1)
    return (col // CHUNK) <= (row // CHUNK)


def _mla_fwd(q, kv, heads, *, name):
    s = q.shape[0]
    bq, bk, nq, ratio = _att_blocks(s, MLA_FWD_K_BLOCK)
    reps = (1, bk // LANES)

    def body(q_ref, k_ref, v_ref, o_ref, lse_ref, acc_ref, m_ref, l_ref):
        qi = pl.program_id(1)
        qv = q_ref[...]
        acc_ref[...] = jnp.zeros_like(acc_ref)
        m_ref[...] = jnp.full_like(m_ref, NEG_BIG)
        l_ref[...] = jnp.zeros_like(l_ref)

        def step(kb, masked, r0, r1):
            rows = pl.ds(pl.multiple_of(kb * bk, bk), bk)
            mine = pl.ds(r0, r1 - r0)
            k, v = k_ref[rows, :], v_ref[rows, :]
            sc = _dot_nt(qv[r0:r1], k)
            if masked:
                allowed = _chunk_allowed(qi, kb, bq, bk, r0, r1)
                sc = jnp.where(allowed, sc, NEG_BIG)
            m_old = m_ref[mine, :]
            m_new = jnp.maximum(m_old, jnp.max(sc, axis=-1, keepdims=True))
            p = jnp.exp(sc - jnp.tile(m_new, reps))
            alpha = jnp.exp(m_old - m_new)
            l_ref[mine, :] = alpha * l_ref[mine, :] + jnp.sum(p, axis=-1, keepdims=True)
            acc_ref[mine, :] = alpha * acc_ref[mine, :] + _dot_nn(p.astype(BF16), v)
            m_ref[mine, :] = m_new

        _sweep(qi, ratio, bk, step)
        o_ref[...] = (acc_ref[...] / l_ref[...]).astype(o_ref.dtype)
        lse_ref[...] = m_ref[...] + jnp.log(l_ref[...])

    blk = pl.BlockSpec((bq, LANES), lambda h, i: (i, h))
    return pl.pallas_call(
        body, name=name, grid=(heads, nq),
        in_specs=[blk,
                  pl.BlockSpec((s, LANES), lambda h, i: (0, h)),
                  pl.BlockSpec((s, LANES), lambda h, i: (0, heads + h))],
        out_specs=[blk, blk],
        out_shape=[jax.ShapeDtypeStruct((s, heads * LANES), BF16), jax.ShapeDtypeStruct((s, heads * LANES), F32)],
        scratch_shapes=[pltpu.VMEM((bq, LANES), F32), pltpu.VMEM((bq, LANES), F32), pltpu.VMEM((bq, LANES), F32)],
        compiler_params=pltpu.CompilerParams(dimension_semantics=("parallel", "arbitrary"),
                                             vmem_limit_bytes=VMEM_LIMIT),
    )(q, kv, kv)


def _mla_bwd(q, kv, o, do, lse, cos_t, sin_t, dkv_init, heads, *, name):
    s = q.shape[0]
    bq, bk, nq, ratio = _att_blocks(s)
    reps = (1, bk // LANES)
    has_init = dkv_init is not None

    def body(*refs):
        q_ref, k_ref, v_ref, o_ref, do_ref, lse_ref, c_ref, s_ref = refs[:8]
        ki_ref, vi_ref = (refs[8], refs[9]) if has_init else (None, None)
        dq_ref, dk_ref, dv_ref, dq_acc, dk_acc, dv_acc = refs[-6:]
        qi = pl.program_id(1)

        @pl.when(qi == 0)
        def _():
            if has_init:
                dk_acc[...] = ki_ref[...].astype(F32).T
                dv_acc[...] = vi_ref[...].astype(F32).T
            else:
                dk_acc[...] = jnp.zeros_like(dk_acc)
                dv_acc[...] = jnp.zeros_like(dv_acc)

        qv = q_ref[...]
        do = do_ref[...]
        q_t, do_t = qv.T, do.T
        delta = jnp.sum(do.astype(F32) * o_ref[...].astype(F32), axis=-1, keepdims=True)
        lse_wide = jnp.tile(lse_ref[...], reps)
        dq_acc[...] = jnp.zeros_like(dq_acc)

        def step(kb, masked, r0, r1):
            rows = pl.ds(pl.multiple_of(kb * bk, bk), bk)
            k, v = k_ref[rows, :], v_ref[rows, :]
            qs, dos = qv[r0:r1], do[r0:r1]
            p = jnp.exp(_dot_nt(qs, k) - lse_wide[r0:r1])
            if masked:
                p = jnp.where(_chunk_allowed(qi, kb, bq, bk, r0, r1), p, 0.0)
            ds = (p * (_dot_nt(dos, v) - delta[r0:r1])).astype(BF16)
            dq_acc[pl.ds(r0, r1 - r0), :] += _dot_nn(ds, k)
            dk_acc[:, rows] += _dot_nn(q_t[:, r0:r1], ds)
            dv_acc[:, rows] += _dot_nn(do_t[:, r0:r1], p.astype(BF16))

        _sweep(qi, ratio, bk, step)
        dq_ref[...] = _rope_slab_bwd(dq_acc[...] * MLA_SCALE, c_ref[...], s_ref[...]).astype(dq_ref.dtype)

        @pl.when(qi == nq - 1)
        def _():
            dk_ref[...] = dk_acc[...].T.astype(dk_ref.dtype)
            dv_ref[...] = dv_acc[...].T.astype(dv_ref.dtype)

    blk = pl.BlockSpec((bq, LANES), lambda h, i: (i, h))
    tab = pl.BlockSpec((bq, LANES), lambda h, i: (i, 0))
    k_full = pl.BlockSpec((s, LANES), lambda h, i: (0, h))
    v_full = pl.BlockSpec((s, LANES), lambda h, i: (0, heads + h))
    shape = jax.ShapeDtypeStruct((s, heads * LANES), BF16)
    ins = [q, kv, kv, o, do, lse, cos_t, sin_t] + ([dkv_init, dkv_init] if has_init else [])
    dq, dk, dv = pl.pallas_call(
        body, name=name, grid=(heads, nq),
        in_specs=[blk, k_full, v_full, blk, blk, blk, tab, tab] + ([k_full, v_full] if has_init else []),
        out_specs=[blk, k_full, k_full],
        out_shape=[shape, shape, shape],
        scratch_shapes=[pltpu.VMEM((bq, LANES), F32), pltpu.VMEM((LANES, s), F32), pltpu.VMEM((LANES, s), F32)],
        compiler_params=pltpu.CompilerParams(dimension_semantics=("arbitrary", "arbitrary"),
                                             vmem_limit_bytes=VMEM_LIMIT),
    )(*ins)
    return dq, jnp.concatenate([dk, dv], axis=1)


def _pad_last(a, width):
    return jnp.pad(a, [(0, 0)] * (a.ndim - 1) + [(0, width - a.shape[-1])])


def _pad_qkv(w, heads):
    d = w.shape[0]
    return _pad_last(w.reshape(d, 3 * heads, SB_HEAD_DIM), LANES).reshape(d, 3 * heads * LANES)


def _unpad_qkv(g, heads):
    d = g.shape[0]
    return g.reshape(d, 3 * heads, LANES)[:, :, :SB_HEAD_DIM].reshape(d, 3 * heads * SB_HEAD_DIM)


def _pad_o(w, heads):
    d = w.shape[1]
    w = w.reshape(heads, SB_HEAD_DIM, d)
    return jnp.pad(w, [(0, 0), (0, LANES - SB_HEAD_DIM), (0, 0)]).reshape(heads * LANES, d)


def _unpad_o(g, heads):
    d = g.shape[1]
    return g.reshape(heads, LANES, d)[:, :SB_HEAD_DIM, :].reshape(heads * SB_HEAD_DIM, d)


def _pad_uq(w, heads):
    r = w.shape[0]
    return _pad_last(w.reshape(r, heads, MLA_NOPE + MLA_ROPE), LANES).reshape(r, heads * LANES)


def _unpad_uq(g, heads):
    r = g.shape[0]
    return g.reshape(r, heads, LANES)[:, :, :MLA_NOPE + MLA_ROPE].reshape(r, heads * (MLA_NOPE + MLA_ROPE))


def _pad_dkv(w):
    d = w.shape[0]
    rope = jnp.zeros((d, LANES), w.dtype).at[:, ROPE_LO:ROPE_LO + MLA_ROPE].set(w[:, MLA_KV_RANK:])
    return jnp.concatenate([w[:, :MLA_KV_RANK], rope], axis=1)


def _unpad_dkv(g):
    return jnp.concatenate([g[:, :MLA_KV_RANK], g[:, MLA_KV_RANK + ROPE_LO:MLA_KV_RANK + ROPE_LO + MLA_ROPE]], axis=1)


def _pad_ukv(w, heads):
    w = w.reshape(MLA_KV_RANK, heads, 2, MLA_NOPE)
    k_part = _pad_last(w[:, :, 0, :], LANES).reshape(MLA_KV_RANK, heads * LANES)
    v_part = _pad_last(w[:, :, 1, :], LANES).reshape(MLA_KV_RANK, heads * LANES)
    lane = jnp.arange(LANES)
    place = ((lane[:, None] == lane[None, :]) & (lane[:, None] >= ROPE_LO) & (lane[:, None] < ROPE_LO + MLA_ROPE))
    place = jnp.tile(place.astype(w.dtype), (1, heads))
    top = jnp.concatenate([k_part, v_part], axis=1)
    bottom = jnp.concatenate([place, jnp.zeros_like(place)], axis=1)
    return jnp.concatenate([top, bottom], axis=0)


def _unpad_ukv(g, heads):
    g = g[:MLA_KV_RANK]
    k_part = g[:, :heads * LANES].reshape(MLA_KV_RANK, heads, LANES)[:, :, :MLA_NOPE]
    v_part = g[:, heads * LANES:].reshape(MLA_KV_RANK, heads, LANES)[:, :, :MLA_V]
    return jnp.stack([k_part, v_part], axis=2).reshape(MLA_KV_RANK, heads * (MLA_NOPE + MLA_V))


def _rope_tables(positions):
    inv_freq = ROPE_THETA ** (-jnp.arange(0, MLA_ROPE, 2, dtype=F32) / MLA_ROPE)
    ang = positions.astype(F32)[:, None] * inv_freq
    cos, sin = jnp.cos(ang), jnp.sin(ang)
    s = positions.shape[0]
    cos_t = jnp.ones((s, LANES), F32).at[:, ROPE_LO:ROPE_LO + MLA_ROPE].set(jnp.concatenate([cos, cos], axis=1))
    sin_t = jnp.zeros((s, LANES), F32).at[:, ROPE_LO:ROPE_LO + MLA_ROPE].set(jnp.concatenate([-sin, sin], axis=1))
    return cos_t, sin_t


def _local_step(x, positions, target, w, norms):
    s, d = x.shape
    heads = d // SB_HEAD_DIM
    n_a = w["sb_w_qkv"].shape[0]
    n_b = w["mla_w_dq"].shape[0]
    depth = n_a + n_b
    cos_t, sin_t = _rope_tables(positions)

    wqkv = [_pad_qkv(w["sb_w_qkv"][l], heads) for l in range(n_a)]
    wo_a = [_pad_o(w["sb_w_o"][l], heads) for l in range(n_a)]
    wdkv = _pad_dkv(w["mla_w_dkv"])
    wkv = _pad_ukv(w["mla_w_ukv"], heads)
    wdq = [w["mla_w_dq"][j] for j in range(n_b)]
    wuq = [_pad_uq(w["mla_w_uq"][j], heads) for j in range(n_b)]
    wo_b = [_pad_o(w["mla_w_o"][j], heads) for j in range(n_b)]
    w1 = [w["mlp_w1"][l] for l in range(depth)]
    w2 = [w["mlp_w2"][l] for l in range(depth)]

    saved = []
    kv_saved = None
    kv = None
    for l in range(depth):
        t = f"l{l}"
        sv = {"x_in": x}
        h = _rms_fwd(x, norms["attn_norm"][l], name=f"{t}_attn_norm")
        sv["h"] = h
        if l < n_a:
            qkv = _mm(h, wqkv[l], name=f"{t}_qkv")
            o = _sb_fwd(qkv, heads, name=f"{t}_sb_fwd")
            sv["qkv"], sv["o"] = qkv, o
            x = _mm(o, wo_a[l], name=f"{t}_attn_out", epilogue=_epi_add, extras=[(x, "tile")], out_dtypes=(F32,))
        else:
            j = l - n_a
            if j == 0:
                hk = _rms_fwd(x, norms["kv_norm"], name="kv_norm")
                down = _mm(hk, wdkv, name="kv_down", out_dtypes=(F32,))
                cat = _kv_prep(down, norms["mla_kv_lat_norm"], cos_t, sin_t, name="kv_prep")
                kv = _mm(cat, wkv, name="kv_up")
                kv_saved = {"x_in": x, "hk": hk, "down": down, "cat": cat}
            cq0 = _mm(h, wdq[j], name=f"{t}_q_down", out_dtypes=(F32,))
            cq = _rms_fwd(cq0, norms["mla_q_lat_norm"][j], name=f"{t}_q_lat_norm")
            q = _mm(cq, wuq[j], name=f"{t}_q_up", epilogue=_epi_rope_heads, extras=[(cos_t, "row"), (sin_t, "row")])
            o, lse = _mla_fwd(q, kv, heads, name=f"{t}_mla_fwd")
            sv.update(cq0=cq0, cq=cq, q=q, o=o, lse=lse)
            x = _mm(o, wo_b[j], name=f"{t}_attn_out", epilogue=_epi_add, extras=[(x, "tile")], out_dtypes=(F32,))
        sv["x_mid"] = x
        h2 = _rms_fwd(x, norms["mlp_norm"][l], name=f"{t}_mlp_norm")
        u, a = _mm(h2, w1[l], name=f"{t}_mlp_up", epilogue=_epi_relu2, out_dtypes=(BF16, BF16))
        sv.update(h2=h2, u=u, a=a)
        x = _mm(a, w2[l], name=f"{t}_mlp_down", epilogue=_epi_add, extras=[(x, "tile")], out_dtypes=(F32,))
        saved.append(sv)

    loss_slab, dx, dxb, dg_final = _loss_bwd(x, norms["final_norm"], target, name="loss")
    loss = loss_slab[0, 0]

    g_attn_norm, g_mlp_norm = [None] * depth, [None] * depth
    g_qkv, g_o_a = [None] * n_a, [None] * n_a
    g_dq, g_uq, g_o_b, g_qlat = [None] * n_b, [None] * n_b, [None] * n_b, [None] * n_b
    g_w1, g_w2 = [None] * depth, [None] * depth
    dkv = None
    g_kv_norm = g_kv_lat = g_dkv = g_ukv = None

    for l in reversed(range(depth)):
        t = f"l{l}"
        sv = saved[l]
        du = _mm(dxb, w2[l], name=f"{t}_mlp_down_dx", dims="nt", epilogue=_epi_relu2_grad, extras=[(sv["u"], "tile")])
        g_w2[l] = _mm(sv["a"], dxb, name=f"{t}_mlp_down_dw", dims="tn", out_dtypes=(F32,))
        g_w1[l] = _mm(sv["h2"], du, name=f"{t}_mlp_up_dw", dims="tn", out_dtypes=(F32,))
        dh2 = _mm(du, w1[l], name=f"{t}_mlp_up_dx", dims="nt", out_dtypes=(F32,))
        dx, dxb, g_mlp_norm[l] = _rms_bwd(sv["x_mid"], norms["mlp_norm"][l], dh2, dx, name=f"{t}_mlp_norm_bwd")
        if l < n_a:
            do = _mm(dxb, wo_a[l], name=f"{t}_attn_out_dx", dims="nt")
            g_o_a[l] = _unpad_o(_mm(sv["o"], dxb, name=f"{t}_attn_out_dw", dims="tn", out_dtypes=(F32,)), heads)
            dq, dk, dv = _sb_bwd(sv["qkv"], sv["o"], do, heads, name=f"{t}_sb_bwd")
            dqkv = jnp.concatenate([dq, dk, dv], axis=1)
            g_qkv[l] = _unpad_qkv(_mm(sv["h"], dqkv, name=f"{t}_qkv_dw", dims="tn", out_dtypes=(F32,)), heads)
            dh = _mm(dqkv, wqkv[l], name=f"{t}_qkv_dx", dims="nt", out_dtypes=(F32,))
        else:
            j = l - n_a
            do = _mm(dxb, wo_b[j], name=f"{t}_attn_out_dx", dims="nt")
            g_o_b[j] = _unpad_o(_mm(sv["o"], dxb, name=f"{t}_attn_out_dw", dims="tn", out_dtypes=(F32,)), heads)
            dq, dkv = _mla_bwd(sv["q"], kv, sv["o"], do, sv["lse"], cos_t, sin_t, dkv, heads, name=f"{t}_mla_bwd")
            g_uq[j] = _unpad_uq(_mm(sv["cq"], dq, name=f"{t}_q_up_dw", dims="tn", out_dtypes=(F32,)), heads)
            dcq = _mm(dq, wuq[j], name=f"{t}_q_up_dx", dims="nt", out_dtypes=(F32,))
            _, dcq0, g_qlat[j] = _rms_bwd(sv["cq0"], norms["mla_q_lat_norm"][j], dcq, None, name=f"{t}_q_lat_norm_bwd")
            g_dq[j] = _mm(sv["h"], dcq0, name=f"{t}_q_down_dw", dims="tn", out_dtypes=(F32,))
            dh = _mm(dcq0, wdq[j], name=f"{t}_q_down_dx", dims="nt", out_dtypes=(F32,))
        dx, dxb, g_attn_norm[l] = _rms_bwd(sv["x_in"], norms["attn_norm"][l], dh, dx, name=f"{t}_attn_norm_bwd",
                                           lead_axis=(l == 0))
        if l == n_a:
            ks = kv_saved
            dcat = _mm(dkv, wkv, name="kv_up_dx", dims="nt", out_dtypes=(F32,))
            g_ukv = _unpad_ukv(_mm(ks["cat"], dkv, name="kv_up_dw", dims="tn", out_dtypes=(F32,)), heads)
            ddown, g_kv_lat = _kv_prep_bwd(ks["down"], norms["mla_kv_lat_norm"], cos_t, sin_t, dcat, name="kv_prep_bwd")
            g_dkv = _unpad_dkv(_mm(ks["hk"], ddown, name="kv_down_dw", dims="tn", out_dtypes=(F32,)))
            dhk = _mm(ddown, wdkv, name="kv_down_dx", dims="nt", out_dtypes=(F32,))
            dx, dxb, g_kv_norm = _rms_bwd(ks["x_in"], norms["kv_norm"], dhk, dx, name="kv_norm_bwd")

    grads = {
        "attn_norm": jnp.concatenate(g_attn_norm, axis=0), "mlp_norm": jnp.concatenate(g_mlp_norm, axis=0),
        "sb_w_qkv": g_qkv, "sb_w_o": g_o_a,
        "kv_norm": g_kv_norm[0], "mla_w_dkv": g_dkv, "mla_kv_lat_norm": g_kv_lat[0], "mla_w_ukv": g_ukv,
        "mla_w_dq": g_dq, "mla_q_lat_norm": jnp.concatenate(g_qlat, axis=0),
        "mla_w_uq": g_uq, "mla_w_o": g_o_b,
        "mlp_w1": g_w1, "mlp_w2": g_w2, "final_norm": dg_final[0],
    }
    return loss, dx, grads


def _flat_rows(n_elems):
    per_block = FLAT_COLS * FLAT_ROW_BLOCK * 2
    return -(-n_elems // per_block) * FLAT_ROW_BLOCK * 2


def _pack(arrays, dtype):
    flat = jnp.concatenate([a.reshape(-1).astype(dtype) for a in arrays])
    rows = _flat_rows(flat.shape[0])
    flat = jnp.pad(flat, (0, rows * FLAT_COLS - flat.shape[0]))
    return flat.reshape(rows, FLAT_COLS)


def _pack_chips(per_chip, dtype):
    n_elems = sum(a.size for a in per_chip[0])
    rows = _flat_rows(n_elems)
    tail = jnp.zeros((rows * FLAT_COLS - n_elems,), dtype)
    flat = jnp.concatenate([piece for arrays in per_chip
                            for piece in [a.reshape(-1).astype(dtype) for a in arrays] + [tail]])
    return flat.reshape(len(per_chip), rows, FLAT_COLS)


def _unpack(flat, shapes):
    flat = flat.reshape(-1)
    out, off = [], 0
    for shp in shapes:
        n = 1
        for v in shp:
            n *= v
        out.append(flat[off:off + n].reshape(shp))
        off += n
    return out


def _pack_small(arrays):
    rows = []
    for a in arrays:
        a = a.reshape(-1, a.shape[-1]) if a.shape[-1] == FLAT_COLS else a.reshape(1, -1)
        rows.append(_pad_last(a, FLAT_COLS))
    flat = jnp.concatenate(rows, axis=0)
    return jnp.pad(flat, [(0, -flat.shape[0] % 8), (0, 0)])


def _unpack_small(flat, shapes):
    out, row = [], 0
    for shp in shapes:
        if shp[-1] == FLAT_COLS:
            n = 1
            for v in shp[:-1]:
                n *= v
            out.append(flat[row:row + n].reshape(shp))
            row += n
        else:
            n = 1
            for v in shp:
                n *= v
            out.append(flat[row, :n].reshape(shp))
            row += 1
    return out


def _other_chips(x, y):
    return [(1 - x, y), (x, 1 - y), (1 - x, 1 - y)]


def _all_gather_chips(flat, *, name):
    rows, cols = flat.shape

    def body(x_ref, out_ref, send_sems, recv_sems, pass_send_sems, pass_recv_sems, own_send_sem, own_recv_sem):
        x, y, c = lax.axis_index("x"), lax.axis_index("y"), lax.axis_index("c")
        me = 2 * x + y
        my_rows, sib_rows = _half_rows(rows)
        chips = _other_chips(x, y)
        own = pltpu.make_async_remote_copy(src_ref=x_ref, dst_ref=out_ref.at[me], send_sem=own_send_sem,
                                           recv_sem=own_recv_sem, device_id=_sibling(), device_id_type=MESH)
        own.start()
        sends = [own]
        for k, (px, py) in enumerate(chips):
            cp = pltpu.make_async_remote_copy(src_ref=x_ref.at[my_rows, :], dst_ref=out_ref.at[me, my_rows, :],
                                              send_sem=send_sems.at[k], recv_sem=recv_sems.at[k],
                                              device_id=(px, py, c), device_id_type=MESH)
            cp.start()
            sends.append(cp)
        for k, (px, py) in enumerate(chips):
            landed = out_ref.at[2 * px + py, my_rows, :]
            pltpu.make_async_remote_copy(src_ref=landed, dst_ref=landed, send_sem=send_sems.at[k],
                                         recv_sem=recv_sems.at[k], device_id=(px, py, c),
                                         device_id_type=MESH).wait_recv()
            cp = pltpu.make_async_remote_copy(src_ref=landed, dst_ref=landed, send_sem=pass_send_sems.at[k],
                                              recv_sem=pass_recv_sems.at[k], device_id=_sibling(),
                                              device_id_type=MESH)
            cp.start()
            sends.append(cp)
        for k, (px, py) in enumerate(chips):
            passed = out_ref.at[2 * px + py, sib_rows, :]
            pltpu.make_async_remote_copy(src_ref=passed, dst_ref=passed, send_sem=pass_send_sems.at[k],
                                         recv_sem=pass_recv_sems.at[k], device_id=_sibling(),
                                         device_id_type=MESH).wait_recv()
        own.wait_recv()
        for cp in sends:
            cp.wait_send()

    return pl.pallas_call(
        body, name=name,
        in_specs=[pl.BlockSpec(memory_space=pltpu.HBM)],
        out_specs=pl.BlockSpec(memory_space=pltpu.HBM),
        out_shape=jax.ShapeDtypeStruct((N_CHIPS, rows, cols), flat.dtype),
        scratch_shapes=[pltpu.SemaphoreType.DMA((3,)), pltpu.SemaphoreType.DMA((3,)), pltpu.SemaphoreType.DMA((3,)),
                        pltpu.SemaphoreType.DMA((3,)), pltpu.SemaphoreType.DMA, pltpu.SemaphoreType.DMA],
        compiler_params=pltpu.CompilerParams(has_side_effects=True),
    )(flat)


def _exchange_chips(parts, *, name):
    def body(g_ref, out_ref, send_sems, recv_sems):
        x, y, c = lax.axis_index("x"), lax.axis_index("y"), lax.axis_index("c")
        me = 2 * x + y
        sends = []
        for k, (px, py) in enumerate(_other_chips(x, y)):
            cp = pltpu.make_async_remote_copy(src_ref=g_ref.at[2 * px + py], dst_ref=out_ref.at[me],
                                              send_sem=send_sems.at[k], recv_sem=recv_sems.at[k],
                                              device_id=(px, py, c), device_id_type=MESH)
            cp.start()
            sends.append(cp)
        for k, (px, py) in enumerate(_other_chips(x, y)):
            pltpu.make_async_remote_copy(src_ref=g_ref.at[me], dst_ref=out_ref.at[2 * px + py],
                                         send_sem=send_sems.at[k], recv_sem=recv_sems.at[k],
                                         device_id=(px, py, c), device_id_type=MESH).wait_recv()
        for cp in sends:
            cp.wait_send()

    out = pl.pallas_call(
        body, name=name,
        in_specs=[pl.BlockSpec(memory_space=pltpu.HBM)],
        out_specs=pl.BlockSpec(memory_space=pltpu.HBM),
        out_shape=jax.ShapeDtypeStruct(parts.shape, parts.dtype),
        scratch_shapes=[pltpu.SemaphoreType.DMA((3,)), pltpu.SemaphoreType.DMA((3,))],
        compiler_params=pltpu.CompilerParams(has_side_effects=True),
    )(parts)
    me = _my_chip()
    return lax.dynamic_update_index_in_dim(out, lax.dynamic_index_in_dim(parts, me, 0, keepdims=False), me, 0)


def _my_chip():
    return 2 * lax.axis_index("x") + lax.axis_index("y")


def _half_rows(rows):
    c = lax.axis_index("c")
    half = rows // 2
    return pl.ds(pl.multiple_of(c * half, 8), half), pl.ds(pl.multiple_of((1 - c) * half, 8), half)


def _sibling():
    return (lax.axis_index("x"), lax.axis_index("y"), 1 - lax.axis_index("c"))


def _pair_exchange(parts, *, name):
    n, rows, cols = parts.shape

    def body(p_ref, theirs_ref, send_sem, recv_sem):
        _, sib_rows = _half_rows(rows)
        cp = pltpu.make_async_remote_copy(src_ref=p_ref.at[:, sib_rows, :], dst_ref=theirs_ref, send_sem=send_sem,
                                          recv_sem=recv_sem, device_id=_sibling(), device_id_type=MESH)
        cp.start()
        cp.wait()

    half = rows // 2
    theirs = pl.pallas_call(
        body, name=name,
        in_specs=[pl.BlockSpec(memory_space=pltpu.HBM)],
        out_specs=pl.BlockSpec(memory_space=pltpu.HBM),
        out_shape=jax.ShapeDtypeStruct((n, half, cols), parts.dtype),
        scratch_shapes=[pltpu.SemaphoreType.DMA, pltpu.SemaphoreType.DMA],
        compiler_params=pltpu.CompilerParams(has_side_effects=True),
    )(parts)
    mine = lax.dynamic_slice_in_dim(parts, lax.axis_index("c") * half, half, axis=1)
    return mine, theirs


def _pair_sum(mine, theirs, *, name):
    n, rows, cols = mine.shape

    def body(a_ref, b_ref, o_ref):
        o_ref[...] = (a_ref[...].astype(F32) + b_ref[...].astype(F32)).astype(o_ref.dtype)

    blk = pl.BlockSpec((n, FLAT_ROW_BLOCK, cols), lambda i: (0, i, 0))
    return pl.pallas_call(
        body, name=name, grid=(rows // FLAT_ROW_BLOCK,),
        in_specs=[blk, blk], out_specs=blk, out_shape=jax.ShapeDtypeStruct(mine.shape, mine.dtype),
        compiler_params=pltpu.CompilerParams(dimension_semantics=("parallel",), vmem_limit_bytes=VMEM_LIMIT),
    )(mine, theirs)


def _sum_chips(parts, *, name):
    _, rows, cols = parts.shape

    def body(p_ref, o_ref):
        o_ref[...] = ((p_ref[0].astype(F32) + p_ref[1].astype(F32)) + p_ref[2].astype(F32)) + p_ref[3].astype(F32)

    return pl.pallas_call(
        body, name=name, grid=(rows // FLAT_ROW_BLOCK,),
        in_specs=[pl.BlockSpec((N_CHIPS, FLAT_ROW_BLOCK, cols), lambda i: (0, i, 0))],
        out_specs=pl.BlockSpec((FLAT_ROW_BLOCK, cols), lambda i: (i, 0)),
        out_shape=jax.ShapeDtypeStruct((rows, cols), F32),
        compiler_params=pltpu.CompilerParams(dimension_semantics=("parallel",), vmem_limit_bytes=VMEM_LIMIT),
    )(parts)


def _join_cores(half, *, name):
    rows2, cols = half.shape

    def body(h_ref, out_ref, send_sem, recv_sem):
        my_rows, sib_rows = _half_rows(2 * rows2)
        cp = pltpu.make_async_remote_copy(src_ref=h_ref, dst_ref=out_ref.at[my_rows, :], send_sem=send_sem,
                                          recv_sem=recv_sem, device_id=_sibling(), device_id_type=MESH)
        cp.start()
        cp.wait_send()
        pltpu.make_async_remote_copy(src_ref=h_ref, dst_ref=out_ref.at[sib_rows, :], send_sem=send_sem,
                                     recv_sem=recv_sem, device_id=_sibling(), device_id_type=MESH).wait_recv()

    out = pl.pallas_call(
        body, name=name,
        in_specs=[pl.BlockSpec(memory_space=pltpu.HBM)],
        out_specs=pl.BlockSpec(memory_space=pltpu.HBM),
        out_shape=jax.ShapeDtypeStruct((2 * rows2, cols), half.dtype),
        scratch_shapes=[pltpu.SemaphoreType.DMA, pltpu.SemaphoreType.DMA],
        compiler_params=pltpu.CompilerParams(has_side_effects=True),
    )(half)
    return lax.dynamic_update_slice_in_dim(out, half, lax.axis_index("c") * rows2, axis=0)


def _all_reduce_small(v, *, name):
    rows, cols = v.shape
    flips = [(fx, fy, fc) for fx in (0, 1) for fy in (0, 1) for fc in (0, 1)][1:]

    def body(v_ref, out_ref, gath_ref, send_sems, recv_sems):
        x, y, c = lax.axis_index("x"), lax.axis_index("y"), lax.axis_index("c")
        me = 4 * x + 2 * y + c
        gath_ref[me] = v_ref[...]
        peers = [((1 - x) if fx else x, (1 - y) if fy else y, (1 - c) if fc else c) for fx, fy, fc in flips]
        sends = []
        for k, peer in enumerate(peers):
            cp = pltpu.make_async_remote_copy(src_ref=v_ref, dst_ref=gath_ref.at[me], send_sem=send_sems.at[k],
                                              recv_sem=recv_sems.at[k], device_id=peer, device_id_type=MESH)
            cp.start()
            sends.append(cp)
        for k, (px, py, pc) in enumerate(peers):
            pltpu.make_async_remote_copy(src_ref=v_ref, dst_ref=gath_ref.at[4 * px + 2 * py + pc],
                                         send_sem=send_sems.at[k], recv_sem=recv_sems.at[k],
                                         device_id=(px, py, pc), device_id_type=MESH).wait_recv()
        for cp in sends:
            cp.wait_send()
        total = gath_ref[0]
        for k in range(1, 8):
            total = total + gath_ref[k]
        out_ref[...] = total

    total, _ = pl.pallas_call(
        body, name=name,
        in_specs=[pl.BlockSpec(memory_space=pltpu.VMEM)],
        out_specs=[pl.BlockSpec(memory_space=pltpu.VMEM), pl.BlockSpec(memory_space=pltpu.VMEM)],
        out_shape=[jax.ShapeDtypeStruct((rows, cols), v.dtype), jax.ShapeDtypeStruct((8, rows, cols), v.dtype)],
        scratch_shapes=[pltpu.SemaphoreType.DMA((7,)), pltpu.SemaphoreType.DMA((7,))],
        compiler_params=pltpu.CompilerParams(has_side_effects=True),
    )(v)
    return total


def _adamw(w, g, m, v, *, name):
    shape = w.shape
    cols = shape[-1]
    w2, g2, m2, v2 = (a.reshape(-1, cols) for a in (w, g, m, v))
    rows = w2.shape[0]
    br = _pick_rows(rows, FLAT_ROW_BLOCK)

    def body(w_ref, g_ref, m_ref, v_ref, d_out, m_out, v_out):
        gv = g_ref[...]
        m_new = ADAM_B1 * m_ref[...] + (1.0 - ADAM_B1) * gv
        v_new = ADAM_B2 * v_ref[...] + (1.0 - ADAM_B2) * jnp.square(gv)
        m_hat = m_new / (1.0 - ADAM_B1 ** ADAM_STEP)
        v_hat = v_new / (1.0 - ADAM_B2 ** ADAM_STEP)
        d_out[...] = -ADAM_LR * (m_hat / (jnp.sqrt(v_hat) + ADAM_EPS) + ADAM_WD * w_ref[...])
        m_out[...] = m_new
        v_out[...] = v_new

    blk = pl.BlockSpec((br, cols), lambda i: (i, 0))
    out = jax.ShapeDtypeStruct((rows, cols), F32)
    outs = pl.pallas_call(
        body, name=name, grid=(rows // br,),
        in_specs=[blk] * 4, out_specs=[blk] * 3, out_shape=[out] * 3,
        compiler_params=pltpu.CompilerParams(dimension_semantics=("parallel",), vmem_limit_bytes=VMEM_LIMIT),
    )(w2, g2, m2, v2)
    return [o.reshape(shape) for o in outs]


def _pick_rows(rows, target):
    if rows <= target:
        return rows
    return max(b for b in range(8, target + 1, 8) if rows % b == 0)


def _assemble(gathered_shards, name):
    return jnp.concatenate(gathered_shards, axis=SHARD_AXIS[name])


def _chip_shard(full, name, j):
    if isinstance(full, list):
        axis = SHARD_AXIS[name] - 1
        layers = full
    else:
        axis = SHARD_AXIS[name]
        layers = [full]
    n = layers[0].shape[axis] // N_CHIPS
    return [lax.slice_in_dim(g, j * n, (j + 1) * n, axis=axis) for g in layers]


def kernel(x, positions, attn_norm, mlp_norm, sb_w_qkv, sb_w_o, kv_norm, mla_w_dkv, mla_kv_lat_norm, mla_w_ukv, mla_w_dq, mla_q_lat_norm, mla_w_uq, mla_w_o, mlp_w1, mlp_w2, final_norm, loss_target, m_attn_norm, m_mlp_norm, m_sb_w_qkv, m_sb_w_o, m_kv_norm, m_mla_w_dkv, m_mla_kv_lat_norm, m_mla_w_ukv, m_mla_w_dq, m_mla_q_lat_norm, m_mla_w_uq, m_mla_w_o, m_mlp_w1, m_mlp_w2, m_final_norm, v_attn_norm, v_mlp_norm, v_sb_w_qkv, v_sb_w_o, v_kv_norm, v_mla_w_dkv, v_mla_kv_lat_norm, v_mla_w_ukv, v_mla_w_dq, v_mla_q_lat_norm, v_mla_w_uq, v_mla_w_o, v_mlp_w1, v_mlp_w2, v_final_norm):
    weights = dict(attn_norm=attn_norm, mlp_norm=mlp_norm, sb_w_qkv=sb_w_qkv, sb_w_o=sb_w_o, kv_norm=kv_norm,
                   mla_w_dkv=mla_w_dkv, mla_kv_lat_norm=mla_kv_lat_norm, mla_w_ukv=mla_w_ukv, mla_w_dq=mla_w_dq,
                   mla_q_lat_norm=mla_q_lat_norm, mla_w_uq=mla_w_uq, mla_w_o=mla_w_o, mlp_w1=mlp_w1, mlp_w2=mlp_w2,
                   final_norm=final_norm)
    m_in = dict(attn_norm=m_attn_norm, mlp_norm=m_mlp_norm, sb_w_qkv=m_sb_w_qkv, sb_w_o=m_sb_w_o, kv_norm=m_kv_norm,
                mla_w_dkv=m_mla_w_dkv, mla_kv_lat_norm=m_mla_kv_lat_norm, mla_w_ukv=m_mla_w_ukv, mla_w_dq=m_mla_w_dq,
                mla_q_lat_norm=m_mla_q_lat_norm, mla_w_uq=m_mla_w_uq, mla_w_o=m_mla_w_o, mlp_w1=m_mlp_w1,
                mlp_w2=m_mlp_w2, final_norm=m_final_norm)
    v_in = dict(attn_norm=v_attn_norm, mlp_norm=v_mlp_norm, sb_w_qkv=v_sb_w_qkv, sb_w_o=v_sb_w_o, kv_norm=v_kv_norm,
                mla_w_dkv=v_mla_w_dkv, mla_kv_lat_norm=v_mla_kv_lat_norm, mla_w_ukv=v_mla_w_ukv, mla_w_dq=v_mla_w_dq,
                mla_q_lat_norm=v_mla_q_lat_norm, mla_w_uq=v_mla_w_uq, mla_w_o=v_mla_w_o, mlp_w1=v_mlp_w1,
                mlp_w2=v_mlp_w2, final_norm=v_final_norm)
    shard_shapes = [weights[n].shape for n in BIG_WEIGHTS]
    small_shapes = [weights[n].shape for n in SMALL_WEIGHTS]

    flat_w = _pack([weights[n] for n in BIG_WEIGHTS], BF16)
    gathered = _all_gather_chips(flat_w, name="weights_all_gather")
    per_chip = [_unpack(gathered[j], shard_shapes) for j in range(N_CHIPS)]
    full_w = {n: _assemble([per_chip[j][i] for j in range(N_CHIPS)], n) for i, n in enumerate(BIG_WEIGHTS)}
    norms = {n: weights[n] for n in SMALL_WEIGHTS}

    loss, dx, grads = _local_step(x[0], positions[0], loss_target, full_w, norms)
    loss = lax.psum(loss, ("x", "y", "c"))

    parts = _pack_chips([[piece for n in BIG_WEIGHTS for piece in _chip_shard(grads[n], n, j)]
                         for j in range(N_CHIPS)], BF16)
    mine, theirs = _pair_exchange(parts, name="grads_pair_exchange")
    chip_part = _pair_sum(mine, theirs, name="grads_pair_sum")
    received = _exchange_chips(chip_part, name="grads_exchange")
    g_half = _sum_chips(received, name="grads_sum_chips")
    g_sum = _join_cores(g_half, name="grads_join_cores")
    out_g = dict(zip(BIG_WEIGHTS, _unpack(g_sum, shard_shapes)))
    out_d, out_m, out_v = {}, {}, {}
    for n in BIG_WEIGHTS:
        out_d[n], out_m[n], out_v[n] = _adamw(weights[n], out_g[n], m_in[n], v_in[n], name=f"adamw_{n}")

    small_sum = _all_reduce_small(_pack_small([grads[n] for n in SMALL_WEIGHTS]), name="gains_all_reduce")
    sd, sm, sv = _adamw(_pack_small([weights[n] for n in SMALL_WEIGHTS]), small_sum,
                        _pack_small([m_in[n] for n in SMALL_WEIGHTS]),
                        _pack_small([v_in[n] for n in SMALL_WEIGHTS]), name="adamw_gains")
    out_g.update(zip(SMALL_WEIGHTS, _unpack_small(small_sum, small_shapes)))
    out_d.update(zip(SMALL_WEIGHTS, _unpack_small(sd, small_shapes)))
    out_m.update(zip(SMALL_WEIGHTS, _unpack_small(sm, small_shapes)))
    out_v.update(zip(SMALL_WEIGHTS, _unpack_small(sv, small_shapes)))

    return (loss, dx, *[out_g[n] for n in ALL_WEIGHTS], *[out_d[n] for n in ALL_WEIGHTS],
            *[out_m[n] for n in ALL_WEIGHTS], *[out_v[n] for n in ALL_WEIGHTS])
```

```python
import functools

import jax
import jax.numpy as jnp
from jax import lax
from jax.experimental import pallas as pl
from jax.experimental.pallas import tpu as pltpu

F32 = jnp.float32
BF16 = jnp.bfloat16

LANES = 128
SB_HEAD_DIM = 64
MLA_NOPE = 64
MLA_ROPE = 32
MLA_V = 64
MLA_Q_RANK = 384
MLA_KV_RANK = 256
CHUNK = 64
ROPE_THETA = 10000.0
NORM_EPS = 1e-6
SB_SCALE = SB_HEAD_DIM ** -0.5
MLA_SCALE = (MLA_NOPE + MLA_ROPE) ** -0.5
ROPE_LO = MLA_NOPE
ROPE_HALF = MLA_ROPE // 2
ATT_Q_BLOCK = 1024
ATT_K_BLOCK = 256
MLA_FWD_K_BLOCK = 512
NEG_BIG = -1e30
SB_DEAD_LOG = -110.0
VMEM_LIMIT = 56 * 1024 * 1024

ADAM_LR = 0.001
ADAM_B1 = 0.9
ADAM_B2 = 0.999
ADAM_EPS = 1e-08
ADAM_WD = 0.01
ADAM_STEP = 10

FLAT_COLS = 1024
FLAT_ROW_BLOCK = 256
N_CHIPS = 4
MESH = pl.DeviceIdType.MESH

BIG_WEIGHTS = ["sb_w_qkv", "sb_w_o", "mla_w_dkv", "mla_w_ukv", "mla_w_dq", "mla_w_uq", "mla_w_o", "mlp_w1", "mlp_w2"]
SHARD_AXIS = {"sb_w_qkv": 2, "sb_w_o": 1, "mla_w_dkv": 0, "mla_w_ukv": 1, "mla_w_dq": 1, "mla_w_uq": 2,
              "mla_w_o": 1, "mlp_w1": 2, "mlp_w2": 1}
SMALL_WEIGHTS = ["attn_norm", "mlp_norm", "kv_norm", "mla_kv_lat_norm", "mla_q_lat_norm", "final_norm"]
ALL_WEIGHTS = ["attn_norm", "mlp_norm", "sb_w_qkv", "sb_w_o", "kv_norm", "mla_w_dkv", "mla_kv_lat_norm", "mla_w_ukv",
               "mla_w_dq", "mla_q_lat_norm", "mla_w_uq", "mla_w_o", "mlp_w1", "mlp_w2", "final_norm"]


def _dot(a, b, dims):
    return lax.dot_general(a, b, (dims, ((), ())), preferred_element_type=F32)


def _dot_nn(a, b):
    return _dot(a, b, ((1,), (0,)))


def _dot_nt(a, b):
    return _dot(a, b, ((1,), (1,)))


def _dot_tn(a, b):
    return _dot(a, b, ((0,), (0,)))


def _pick_block(n, target):
    if n <= target:
        return n
    best = max(b for b in range(LANES, target + 1, LANES) if n % b == 0)
    return best


MM_ROWS = 512
MM_COLS = 1024
MM_DEPTH = 4096
MM_DEPTH_TN = 1024


def _mm(a, b, *, name, dims="nn", epilogue=None, extras=(), out_dtypes=(BF16,)):
    if dims == "nn":
        (m, k), (k2, n) = a.shape, b.shape
    elif dims == "nt":
        (m, k), (n, k2) = a.shape, b.shape
    else:
        (k, m), (k2, n) = a.shape, b.shape
    assert k == k2, (name, a.shape, b.shape)
    if dims == "tn":
        bm, bn, bk = _pick_block(m, MM_COLS), _pick_block(n, MM_COLS), _pick_block(k, MM_DEPTH_TN)
    else:
        rows = MM_ROWS if k > MM_DEPTH // 2 else 2 * MM_ROWS
        bm, bn, bk = _pick_block(m, rows), _pick_block(n, MM_COLS), _pick_block(k, MM_DEPTH)
    nk = k // bk
    if dims == "tn":
        a_spec = pl.BlockSpec((bk, bm), lambda j, i, kk: (kk, i))
    else:
        a_spec = pl.BlockSpec((bm, bk), lambda j, i, kk: (i, kk))
    if dims == "nt":
        b_spec = pl.BlockSpec((bn, bk), lambda j, i, kk: (j, kk))
    else:
        b_spec = pl.BlockSpec((bk, bn), lambda j, i, kk: (kk, j))
    extra_specs = []
    for arr, kind in extras:
        if kind == "tile":
            assert arr.shape == (m, n), (name, arr.shape)
            extra_specs.append(pl.BlockSpec((bm, bn), lambda j, i, kk: (i, j)))
        else:
            assert arr.shape == (m, LANES), (name, arr.shape)
            extra_specs.append(pl.BlockSpec((bm, LANES), lambda j, i, kk: (i, 0)))
    n_extra = len(extras)
    n_out = len(out_dtypes)
    dot = {"nn": _dot_nn, "nt": _dot_nt, "tn": _dot_tn}[dims]

    def body(*refs):
        a_ref, b_ref = refs[0], refs[1]
        extra_refs = refs[2:2 + n_extra]
        out_refs = refs[2 + n_extra:2 + n_extra + n_out]

        def finish(acc):
            outs = (acc,) if epilogue is None else epilogue(acc, *[r[...] for r in extra_refs])
            for o_ref, o in zip(out_refs, outs):
                o_ref[...] = o.astype(o_ref.dtype)

        part = dot(a_ref[...].astype(BF16), b_ref[...].astype(BF16))
        if nk == 1:
            finish(part)
            return
        acc_ref = refs[-1]
        kk = pl.program_id(2)

        @pl.when(kk == 0)
        def _():
            acc_ref[...] = part

        @pl.when(kk > 0)
        def _():
            acc_ref[...] += part

        @pl.when(kk == nk - 1)
        def _():
            finish(acc_ref[...])

    outs = pl.pallas_call(
        body, name=name, grid=(n // bn, m // bm, nk),
        in_specs=[a_spec, b_spec] + extra_specs,
        out_specs=[pl.BlockSpec((bm, bn), lambda j, i, kk: (i, j)) for _ in range(n_out)],
        out_shape=[jax.ShapeDtypeStruct((m, n), dt) for dt in out_dtypes],
        scratch_shapes=[pltpu.VMEM((bm, bn), F32)] if nk > 1 else [],
        compiler_params=pltpu.CompilerParams(dimension_semantics=("parallel", "parallel", "arbitrary"),
                                             vmem_limit_bytes=VMEM_LIMIT),
    )(a, b, *[arr for arr, _ in extras])
    return outs[0] if n_out == 1 else outs


def _epi_add(acc, res):
    return (res + acc,)


def _epi_relu2(acc):
    r = jnp.maximum(acc, 0.0)
    return acc, r * r


def _epi_relu2_grad(acc, u):
    return (acc * (2.0 * jnp.maximum(u.astype(F32), 0.0)),)


def _rope_slab(t, cos_t, sin_t):
    lane = lax.broadcasted_iota(jnp.int32, t.shape, 1)
    partner = jnp.where(lane < ROPE_LO + ROPE_HALF, pltpu.roll(t, LANES - ROPE_HALF, 1), pltpu.roll(t, ROPE_HALF, 1))
    return t * cos_t + partner * sin_t


def _rope_slab_bwd(d, cos_t, sin_t):
    ds = d * sin_t
    lane = lax.broadcasted_iota(jnp.int32, d.shape, 1)
    partner = jnp.where(lane < ROPE_LO + ROPE_HALF, pltpu.roll(ds, LANES - ROPE_HALF, 1), pltpu.roll(ds, ROPE_HALF, 1))
    in_rope = (lane >= ROPE_LO) & (lane < ROPE_LO + MLA_ROPE)
    return d * cos_t + jnp.where(in_rope, partner, 0.0)


def _epi_rope_heads(acc, cos_t, sin_t):
    slabs = [_rope_slab(acc[:, j * LANES:(j + 1) * LANES], cos_t, sin_t) for j in range(acc.shape[1] // LANES)]
    return (jnp.concatenate(slabs, axis=1) * MLA_SCALE,)


def _row_block(s):
    return min(512, s)


def _rms_fwd(x, g, *, name):
    s, d = x.shape
    bm = _row_block(s)

    def body(x_ref, g_ref, o_ref):
        xv = x_ref[...]
        r = lax.rsqrt(jnp.mean(xv * xv, axis=-1, keepdims=True) + NORM_EPS)
        o_ref[...] = ((xv * r) * g_ref[...]).astype(o_ref.dtype)

    return pl.pallas_call(
        body, name=name, grid=(s // bm,),
        in_specs=[pl.BlockSpec((bm, d), lambda i: (i, 0)), pl.BlockSpec((1, d), lambda i: (0, 0))],
        out_specs=pl.BlockSpec((bm, d), lambda i: (i, 0)),
        out_shape=jax.ShapeDtypeStruct((s, d), BF16),
        compiler_params=pltpu.CompilerParams(dimension_semantics=("parallel",), vmem_limit_bytes=VMEM_LIMIT),
    )(x, g.reshape(1, d))


def _rms_bwd_math(xv, gv, dy):
    r = lax.rsqrt(jnp.mean(xv * xv, axis=-1, keepdims=True) + NORM_EPS)
    xhat = xv * r
    dyg = dy * gv
    mdot = jnp.mean(dyg * xhat, axis=-1, keepdims=True)
    dx = r * (dyg - xhat * mdot)
    dg = jnp.sum(dy * xhat, axis=0, keepdims=True)
    return dx, dg


def _rms_bwd(x, g, dy, dres, *, name, lead_axis=False):
    s, d = x.shape
    bm = _row_block(s)
    has_res = dres is not None

    def body(*refs):
        x_ref, g_ref, dy_ref = refs[:3]
        dres_ref = refs[3] if has_res else None
        dx_ref, dxb_ref, dg_ref = refs[-3:]
        dx, dg = _rms_bwd_math(x_ref[...], g_ref[...], dy_ref[...].astype(F32))
        if has_res:
            dx = dx + dres_ref[...]
        dx_ref[...] = dx
        dxb_ref[...] = dx.astype(BF16)

        @pl.when(pl.program_id(0) == 0)
        def _():
            dg_ref[...] = jnp.zeros_like(dg_ref)

        dg_ref[...] += dg

    row = pl.BlockSpec((bm, d), lambda i: (i, 0))
    vec = pl.BlockSpec((1, d), lambda i: (0, 0))
    ins = [x, g.reshape(1, d), dy] + ([dres] if has_res else [])
    dx_spec, dx_shape = row, (s, d)
    if lead_axis:
        dx_spec, dx_shape = pl.BlockSpec((None, bm, d), lambda i: (0, i, 0)), (1, s, d)
    return pl.pallas_call(
        body, name=name, grid=(s // bm,),
        in_specs=[row, vec, row] + ([row] if has_res else []),
        out_specs=[dx_spec, row, vec],
        out_shape=[jax.ShapeDtypeStruct(dx_shape, F32), jax.ShapeDtypeStruct((s, d), BF16),
                   jax.ShapeDtypeStruct((1, d), F32)],
        compiler_params=pltpu.CompilerParams(dimension_semantics=("arbitrary",), vmem_limit_bytes=VMEM_LIMIT),
    )(*ins)


def _loss_bwd(x, g, target, *, name):
    s, d = x.shape
    bm = _row_block(s)

    def body(x_ref, g_ref, t_ref, loss_ref, dx_ref, dxb_ref, dg_ref):
        xv, gv = x_ref[...], g_ref[...]
        r = lax.rsqrt(jnp.mean(xv * xv, axis=-1, keepdims=True) + NORM_EPS)
        err = (xv * r) * gv - t_ref[...]
        dx, dg = _rms_bwd_math(xv, gv, err * (1.0 / d))
        dx_ref[...] = dx
        dxb_ref[...] = dx.astype(BF16)

        @pl.when(pl.program_id(0) == 0)
        def _():
            dg_ref[...] = jnp.zeros_like(dg_ref)
            loss_ref[...] = jnp.zeros_like(loss_ref)

        dg_ref[...] += dg
        loss_ref[...] += jnp.sum(jnp.mean(err * err, axis=-1, keepdims=True), axis=0, keepdims=True) * 0.5

    row = pl.BlockSpec((bm, d), lambda i: (i, 0))
    vec = pl.BlockSpec((1, d), lambda i: (0, 0))
    assert target.shape == (1, s, d), target.shape
    return pl.pallas_call(
        body, name=name, grid=(s // bm,),
        in_specs=[row, vec, pl.BlockSpec((None, bm, d), lambda i: (0, i, 0))],
        out_specs=[pl.BlockSpec((8, LANES), lambda i: (0, 0)), row, row, vec],
        out_shape=[jax.ShapeDtypeStruct((8, LANES), F32), jax.ShapeDtypeStruct((s, d), F32),
                   jax.ShapeDtypeStruct((s, d), BF16), jax.ShapeDtypeStruct((1, d), F32)],
        compiler_params=pltpu.CompilerParams(dimension_semantics=("arbitrary",), vmem_limit_bytes=VMEM_LIMIT),
    )(x, g.reshape(1, d), target)


def _kv_prep(down, g, cos_t, sin_t, *, name):
    s, w = down.shape
    bm = _row_block(s)

    def body(d_ref, g_ref, c_ref, s_ref, o_ref):
        lat = d_ref[:, :MLA_KV_RANK]
        r = lax.rsqrt(jnp.mean(lat * lat, axis=-1, keepdims=True) + NORM_EPS)
        o_ref[:, :MLA_KV_RANK] = ((lat * r) * g_ref[...]).astype(BF16)
        o_ref[:, MLA_KV_RANK:] = _rope_slab(d_ref[:, MLA_KV_RANK:], c_ref[...], s_ref[...]).astype(BF16)

    row = pl.BlockSpec((bm, w), lambda i: (i, 0))
    tab = pl.BlockSpec((bm, LANES), lambda i: (i, 0))
    return pl.pallas_call(
        body, name=name, grid=(s // bm,),
        in_specs=[row, pl.BlockSpec((1, MLA_KV_RANK), lambda i: (0, 0)), tab, tab],
        out_specs=row, out_shape=jax.ShapeDtypeStruct((s, w), BF16),
        compiler_params=pltpu.CompilerParams(dimension_semantics=("parallel",), vmem_limit_bytes=VMEM_LIMIT),
    )(down, g.reshape(1, MLA_KV_RANK), cos_t, sin_t)


def _kv_prep_bwd(down, g, cos_t, sin_t, dcat, *, name):
    s, w = down.shape
    bm = _row_block(s)

    def body(d_ref, g_ref, c_ref, s_ref, dc_ref, o_ref, dg_ref):
        dlat, dg = _rms_bwd_math(d_ref[:, :MLA_KV_RANK], g_ref[...], dc_ref[:, :MLA_KV_RANK])
        o_ref[:, :MLA_KV_RANK] = dlat.astype(BF16)
        o_ref[:, MLA_KV_RANK:] = _rope_slab_bwd(dc_ref[:, MLA_KV_RANK:], c_ref[...], s_ref[...]).astype(BF16)

        @pl.when(pl.program_id(0) == 0)
        def _():
            dg_ref[...] = jnp.zeros_like(dg_ref)

        dg_ref[...] += dg

    row = pl.BlockSpec((bm, w), lambda i: (i, 0))
    tab = pl.BlockSpec((bm, LANES), lambda i: (i, 0))
    vec = pl.BlockSpec((1, MLA_KV_RANK), lambda i: (0, 0))
    return pl.pallas_call(
        body, name=name, grid=(s // bm,),
        in_specs=[row, vec, tab, tab, row],
        out_specs=[row, vec],
        out_shape=[jax.ShapeDtypeStruct((s, w), BF16), jax.ShapeDtypeStruct((1, MLA_KV_RANK), F32)],
        compiler_params=pltpu.CompilerParams(dimension_semantics=("arbitrary",), vmem_limit_bytes=VMEM_LIMIT),
    )(down, g.reshape(1, MLA_KV_RANK), cos_t, sin_t, dcat)


def _split_bf16(v):
    hi = v.astype(BF16)
    lo = (v - hi.astype(F32)).astype(BF16)
    return hi, lo


def _suffix_matrices(n):
    row = lax.broadcasted_iota(jnp.int32, (n, n), 0)
    col = lax.broadcasted_iota(jnp.int32, (n, n), 1)
    incl = (row >= col).astype(BF16)
    return (row > col).astype(BF16), jnp.concatenate([incl, incl], axis=0)


def _suffix_sum(v, matrix):
    hi, lo = _split_bf16(v)
    return _dot_nn(jnp.concatenate([hi, lo], axis=1), matrix)


def _block_positions(qi, kb, bq, bk, r0, r1):
    row = qi * bq + r0 + lax.broadcasted_iota(jnp.int32, (r1 - r0, bk), 0)
    col = kb * bk + lax.broadcasted_iota(jnp.int32, (r1 - r0, bk), 1)
    return row, col


def _att_blocks(s, key_block=ATT_K_BLOCK):
    bq, bk = min(ATT_Q_BLOCK, s), min(key_block, s)
    return bq, bk, s // bq, bq // bk


def _sweep(qi, ratio, bk, step, unroll=2, alive=None):
    bq = ratio * bk
    for d in range(ratio):
        kb, r0 = (qi + 1) * ratio - 1 - d, (ratio - 1 - d) * bk
        near = bq if alive is None else min(r0 + 2 * bk, bq)
        step(kb, True, r0, near)
        if near < bq:
            pl.when(alive(near))(functools.partial(step, kb, False, near, bq))
    unroll = unroll if ratio % unroll == 0 else 1
    trips = qi * (ratio // unroll)

    def trip(i):
        for u in range(unroll):
            step(qi * ratio - 1 - (i * unroll + u), False, 0, bq)

    if alive is None:
        lax.fori_loop(0, trips, lambda i, carry: (trip(i), carry)[1], 0)
    else:
        lax.while_loop(lambda i: jnp.logical_and(i < trips, alive(0)), lambda i: (trip(i), i + 1)[1], 0)


def _stick_left(c_ref, r0):
    return jnp.max(c_ref[r0:, :]) > SB_DEAD_LOG


def _sb_logs(q, k):
    z = _dot_nt(q, k)
    lb = jnp.minimum(z, 0.0) - jnp.log(1.0 + jnp.exp(-jnp.abs(z)))
    return lb, lb - z


def _sb_fwd(qkv, heads, *, name, ride=None):
    s = qkv.shape[0]
    bq, bk, nq, ratio = _att_blocks(s)
    riding = ride is not None

    def body(*refs):
        if riding:
            q_ref, k_ref, v_ref, w_ref, o_ref, gath_ref, acc_ref, c_ref = refs[:8]
            first = jnp.logical_and(pl.program_id(0) == 0, pl.program_id(1) == 0)
            last = jnp.logical_and(pl.program_id(0) == heads - 1, pl.program_id(1) == nq - 1)
            pl.when(first)(functools.partial(_all_gather_start, w_ref, gath_ref, refs[8:]))
        else:
            q_ref, k_ref, v_ref, o_ref, acc_ref, c_ref = refs
        qi = pl.program_id(1)
        q = q_ref[...] * SB_SCALE
        m_strict, _ = _suffix_matrices(bk)
        acc_ref[...] = jnp.zeros_like(acc_ref)
        c_ref[...] = jnp.zeros_like(c_ref)

        def step(kb, masked, r0, r1):
            rows = pl.ds(pl.multiple_of(kb * bk, bk), bk)
            mine = pl.ds(r0, r1 - r0)
            k, v = k_ref[rows, :], v_ref[rows, :]
            lb, lk = _sb_logs(q[r0:r1], k)
            if masked:
                row, col = _block_positions(qi, kb, bq, bk, r0, r1)
                causal = col < row
                lk = jnp.where(causal, lk, 0.0)
            c = c_ref[mine, :]
            w = jnp.exp(lb + _dot_nn(lk.astype(BF16), m_strict) + jnp.tile(c, (1, bk // LANES)))
            if masked:
                w = jnp.where(causal, w, 0.0)
            acc_ref[mine, :] += _dot_nn(w.astype(BF16), v)
            c_ref[mine, :] = c + jnp.sum(lk, axis=-1, keepdims=True)

        _sweep(qi, ratio, bk, step, unroll=1, alive=functools.partial(_stick_left, c_ref))
        o_ref[...] = acc_ref[...].astype(o_ref.dtype)
        if riding:
            pl.when(last)(functools.partial(_all_gather_finish, w_ref, gath_ref, refs[8:]))

    hbm = pl.BlockSpec(memory_space=pltpu.HBM)
    o_spec = pl.BlockSpec((bq, LANES), lambda h, i: (i, h))
    o_shape = jax.ShapeDtypeStruct((s, heads * LANES), F32)
    return pl.pallas_call(
        body, name=name, grid=(heads, nq),
        in_specs=[pl.BlockSpec((bq, LANES), lambda h, i: (i, h)),
                  pl.BlockSpec((s, LANES), lambda h, i: (0, heads + h)),
                  pl.BlockSpec((s, LANES), lambda h, i: (0, 2 * heads + h))] + ([hbm] if riding else []),
        out_specs=[o_spec, hbm] if riding else o_spec,
        out_shape=[o_shape, jax.ShapeDtypeStruct((N_CHIPS,) + ride.shape, ride.dtype)] if riding else o_shape,
        scratch_shapes=[pltpu.VMEM((bq, LANES), F32), pltpu.VMEM((bq, LANES), F32)]
        + (_all_gather_sems() if riding else []),
        compiler_params=pltpu.CompilerParams(dimension_semantics=("arbitrary", "arbitrary"),
                                             vmem_limit_bytes=VMEM_LIMIT, has_side_effects=riding),
    )(*([qkv, qkv, qkv] + ([ride] if riding else [])))


def _sb_bwd(qkv, o, do, heads, *, name):
    s = qkv.shape[0]
    bq, bk, nq, ratio = _att_blocks(s)

    def body(q_ref, k_ref, v_ref, o_ref, do_ref, dq_ref, dk_ref, dv_ref, dq_acc, dk_acc, dv_acc, c_ref, e_ref):
        qi = pl.program_id(1)

        @pl.when(qi == 0)
        def _():
            dk_acc[...] = jnp.zeros_like(dk_acc)
            dv_acc[...] = jnp.zeros_like(dv_acc)

        q = q_ref[...] * SB_SCALE
        do = do_ref[...]
        q_t, do_t = q.T, do.T
        total = jnp.sum(do.astype(F32) * o_ref[...].astype(F32), axis=-1, keepdims=True)
        m_strict, m_incl = _suffix_matrices(bk)
        dq_acc[...] = jnp.zeros_like(dq_acc)
        c_ref[...] = jnp.zeros_like(c_ref)
        e_ref[...] = jnp.broadcast_to(total, e_ref.shape)
        reps = (1, bk // LANES)

        def step(kb, masked, r0, r1):
            rows = pl.ds(pl.multiple_of(kb * bk, bk), bk)
            mine = pl.ds(r0, r1 - r0)
            k, v = k_ref[rows, :], v_ref[rows, :]
            qs, dos = q[r0:r1], do[r0:r1]
            lb, lk_all = _sb_logs(qs, k)
            lk = lk_all
            if masked:
                row, col = _block_positions(qi, kb, bq, bk, r0, r1)
                causal = col < row
                lk = jnp.where(causal, lk_all, 0.0)
            c = c_ref[mine, :]
            w = jnp.exp(lb + _dot_nn(lk.astype(BF16), m_strict) + jnp.tile(c, reps))
            if masked:
                w = jnp.where(causal, w, 0.0)
            wb = w.astype(BF16)
            g = wb.astype(F32) * _dot_nt(dos, v)
            e = e_ref[mine, :]
            g_left = jnp.tile(e, reps) - _suffix_sum(g, m_incl)
            da = g * jnp.exp(lk_all) - jnp.exp(lb) * g_left
            if masked:
                da = jnp.where(causal, da, 0.0)
            dab = da.astype(BF16)
            dq_acc[mine, :] += _dot_nn(dab, k)
            dk_acc[:, rows] += _dot_nn(q_t[:, r0:r1], dab)
            dv_acc[:, rows] += _dot_nn(do_t[:, r0:r1], wb)
            e_ref[mine, :] = e - jnp.sum(g, axis=-1, keepdims=True)
            c_ref[mine, :] = c + jnp.sum(lk, axis=-1, keepdims=True)

        _sweep(qi, ratio, bk, step, unroll=1, alive=functools.partial(_stick_left, c_ref))
        dq_ref[...] = (dq_acc[...] * SB_SCALE).astype(dq_ref.dtype)

        @pl.when(qi == nq - 1)
        def _():
            dk_ref[...] = dk_acc[...].T.astype(dk_ref.dtype)
            dv_ref[...] = dv_acc[...].T.astype(dv_ref.dtype)

    blk = pl.BlockSpec((bq, LANES), lambda h, i: (i, h))
    full = pl.BlockSpec((s, LANES), lambda h, i: (0, h))
    shape = jax.ShapeDtypeStruct((s, heads * LANES), BF16)
    return pl.pallas_call(
        body, name=name, grid=(heads, nq),
        in_specs=[blk,
                  pl.BlockSpec((s, LANES), lambda h, i: (0, heads + h)),
                  pl.BlockSpec((s, LANES), lambda h, i: (0, 2 * heads + h)),
                  blk, blk],
        out_specs=[blk, full, full],
        out_shape=[shape, shape, shape],
        scratch_shapes=[pltpu.VMEM((bq, LANES), F32), pltpu.VMEM((LANES, s), F32), pltpu.VMEM((LANES, s), F32),
                        pltpu.VMEM((bq, LANES), F32), pltpu.VMEM((bq, LANES), F32)],
        compiler_params=pltpu.CompilerParams(dimension_semantics=("arbitrary", "arbitrary"),
                                             vmem_limit_bytes=VMEM_LIMIT),
    )(qkv, qkv, qkv, o, do)


def _chunk_allowed(qi, kb, bq, bk, r0, r1):
    row, col = _block_positions(qi, kb, bq, bk, r0, r1)
    return (col // CHUNK) <= (row // CHUNK)


def _mla_fwd(q, kv, heads, *, name):
    s = q.shape[0]
    bq, bk, nq, ratio = _att_blocks(s, MLA_FWD_K_BLOCK)
    reps = (1, bk // LANES)

    def body(q_ref, k_ref, v_ref, o_ref, lse_ref, acc_ref, m_ref, l_ref):
        qi = pl.program_id(1)
        qv = q_ref[...]
        acc_ref[...] = jnp.zeros_like(acc_ref)
        m_ref[...] = jnp.full_like(m_ref, NEG_BIG)
        l_ref[...] = jnp.zeros_like(l_ref)

        def step(kb, masked, r0, r1):
            rows = pl.ds(pl.multiple_of(kb * bk, bk), bk)
            mine = pl.ds(r0, r1 - r0)
            k, v = k_ref[rows, :], v_ref[rows, :]
            sc = _dot_nt(qv[r0:r1], k)
            if masked:
                allowed = _chunk_allowed(qi, kb, bq, bk, r0, r1)
                sc = jnp.where(allowed, sc, NEG_BIG)
            m_old = m_ref[mine, :]
            m_new = jnp.maximum(m_old, jnp.max(sc, axis=-1, keepdims=True))
            p = jnp.exp(sc - jnp.tile(m_new, reps))
            alpha = jnp.exp(m_old - m_new)
            l_ref[mine, :] = alpha * l_ref[mine, :] + jnp.sum(p, axis=-1, keepdims=True)
            acc_ref[mine, :] = alpha * acc_ref[mine, :] + _dot_nn(p.astype(BF16), v)
            m_ref[mine, :] = m_new

        _sweep(qi, ratio, bk, step)
        o_ref[...] = (acc_ref[...] / l_ref[...]).astype(o_ref.dtype)
        lse_ref[...] = m_ref[...] + jnp.log(l_ref[...])

    blk = pl.BlockSpec((bq, LANES), lambda h, i: (i, h))
    return pl.pallas_call(
        body, name=name, grid=(heads, nq),
        in_specs=[blk,
                  pl.BlockSpec((s, LANES), lambda h, i: (0, h)),
                  pl.BlockSpec((s, LANES), lambda h, i: (0, heads + h))],
        out_specs=[blk, blk],
        out_shape=[jax.ShapeDtypeStruct((s, heads * LANES), BF16), jax.ShapeDtypeStruct((s, heads * LANES), F32)],
        scratch_shapes=[pltpu.VMEM((bq, LANES), F32), pltpu.VMEM((bq, LANES), F32), pltpu.VMEM((bq, LANES), F32)],
        compiler_params=pltpu.CompilerParams(dimension_semantics=("parallel", "arbitrary"),
                                             vmem_limit_bytes=VMEM_LIMIT),
    )(q, kv, kv)


def _mla_bwd(q, kv, o, do, lse, cos_t, sin_t, dkv_init, heads, *, name):
    s = q.shape[0]
    bq, bk, nq, ratio = _att_blocks(s)
    reps = (1, bk // LANES)
    has_init = dkv_init is not None

    def body(*refs):
        q_ref, k_ref, v_ref, o_ref, do_ref, lse_ref, c_ref, s_ref = refs[:8]
        ki_ref, vi_ref = (refs[8], refs[9]) if has_init else (None, None)
        dq_ref, dk_ref, dv_ref, dq_acc, dk_acc, dv_acc = refs[-6:]
        qi = pl.program_id(1)

        @pl.when(qi == 0)
        def _():
            if has_init:
                dk_acc[...] = ki_ref[...].astype(F32).T
                dv_acc[...] = vi_ref[...].astype(F32).T
            else:
                dk_acc[...] = jnp.zeros_like(dk_acc)
                dv_acc[...] = jnp.zeros_like(dv_acc)

        qv = q_ref[...]
        do = do_ref[...]
        q_t, do_t = qv.T, do.T
        delta = jnp.sum(do.astype(F32) * o_ref[...].astype(F32), axis=-1, keepdims=True)
        lse_wide = jnp.tile(lse_ref[...], reps)
        dq_acc[...] = jnp.zeros_like(dq_acc)

        def step(kb, masked, r0, r1):
            rows = pl.ds(pl.multiple_of(kb * bk, bk), bk)
            k, v = k_ref[rows, :], v_ref[rows, :]
            qs, dos = qv[r0:r1], do[r0:r1]
            p = jnp.exp(_dot_nt(qs, k) - lse_wide[r0:r1])
            if masked:
                p = jnp.where(_chunk_allowed(qi, kb, bq, bk, r0, r1), p, 0.0)
            ds = (p * (_dot_nt(dos, v) - delta[r0:r1])).astype(BF16)
            dq_acc[pl.ds(r0, r1 - r0), :] += _dot_nn(ds, k)
            dk_acc[:, rows] += _dot_nn(q_t[:, r0:r1], ds)
            dv_acc[:, rows] += _dot_nn(do_t[:, r0:r1], p.astype(BF16))

        _sweep(qi, ratio, bk, step)
        dq_ref[...] = _rope_slab_bwd(dq_acc[...] * MLA_SCALE, c_ref[...], s_ref[...]).astype(dq_ref.dtype)

        @pl.when(qi == nq - 1)
        def _():
            dk_ref[...] = dk_acc[...].T.astype(dk_ref.dtype)
            dv_ref[...] = dv_acc[...].T.astype(dv_ref.dtype)

    blk = pl.BlockSpec((bq, LANES), lambda h, i: (i, h))
    tab = pl.BlockSpec((bq, LANES), lambda h, i: (i, 0))
    k_full = pl.BlockSpec((s, LANES), lambda h, i: (0, h))
    v_full = pl.BlockSpec((s, LANES), lambda h, i: (0, heads + h))
    shape = jax.ShapeDtypeStruct((s, heads * LANES), BF16)
    ins = [q, kv, kv, o, do, lse, cos_t, sin_t] + ([dkv_init, dkv_init] if has_init else [])
    dq, dk, dv = pl.pallas_call(
        body, name=name, grid=(heads, nq),
        in_specs=[blk, k_full, v_full, blk, blk, blk, tab, tab] + ([k_full, v_full] if has_init else []),
        out_specs=[blk, k_full, k_full],
        out_shape=[shape, shape, shape],
        scratch_shapes=[pltpu.VMEM((bq, LANES), F32), pltpu.VMEM((LANES, s), F32), pltpu.VMEM((LANES, s), F32)],
        compiler_params=pltpu.CompilerParams(dimension_semantics=("arbitrary", "arbitrary"),
                                             vmem_limit_bytes=VMEM_LIMIT),
    )(*ins)
    return dq, jnp.concatenate([dk, dv], axis=1)


def _pad_last(a, width):
    return jnp.pad(a, [(0, 0)] * (a.ndim - 1) + [(0, width - a.shape[-1])])


def _pad_qkv(w, heads):
    d = w.shape[0]
    return _pad_last(w.reshape(d, 3 * heads, SB_HEAD_DIM), LANES).reshape(d, 3 * heads * LANES)


def _unpad_qkv(g, heads):
    d = g.shape[0]
    return g.reshape(d, 3 * heads, LANES)[:, :, :SB_HEAD_DIM].reshape(d, 3 * heads * SB_HEAD_DIM)


def _pad_o(w, heads):
    d = w.shape[1]
    w = w.reshape(heads, SB_HEAD_DIM, d)
    return jnp.pad(w, [(0, 0), (0, LANES - SB_HEAD_DIM), (0, 0)]).reshape(heads * LANES, d)


def _unpad_o(g, heads):
    d = g.shape[1]
    return g.reshape(heads, LANES, d)[:, :SB_HEAD_DIM, :].reshape(heads * SB_HEAD_DIM, d)


def _pad_uq(w, heads):
    r = w.shape[0]
    return _pad_last(w.reshape(r, heads, MLA_NOPE + MLA_ROPE), LANES).reshape(r, heads * LANES)


def _unpad_uq(g, heads):
    r = g.shape[0]
    return g.reshape(r, heads, LANES)[:, :, :MLA_NOPE + MLA_ROPE].reshape(r, heads * (MLA_NOPE + MLA_ROPE))


def _pad_dkv(w):
    d = w.shape[0]
    rope = jnp.zeros((d, LANES), w.dtype).at[:, ROPE_LO:ROPE_LO + MLA_ROPE].set(w[:, MLA_KV_RANK:])
    return jnp.concatenate([w[:, :MLA_KV_RANK], rope], axis=1)


def _unpad_dkv(g):
    return jnp.concatenate([g[:, :MLA_KV_RANK], g[:, MLA_KV_RANK + ROPE_LO:MLA_KV_RANK + ROPE_LO + MLA_ROPE]], axis=1)


def _pad_ukv(w, heads):
    w = w.reshape(MLA_KV_RANK, heads, 2, MLA_NOPE)
    k_part = _pad_last(w[:, :, 0, :], LANES).reshape(MLA_KV_RANK, heads * LANES)
    v_part = _pad_last(w[:, :, 1, :], LANES).reshape(MLA_KV_RANK, heads * LANES)
    lane = jnp.arange(LANES)
    place = ((lane[:, None] == lane[None, :]) & (lane[:, None] >= ROPE_LO) & (lane[:, None] < ROPE_LO + MLA_ROPE))
    place = jnp.tile(place.astype(w.dtype), (1, heads))
    top = jnp.concatenate([k_part, v_part], axis=1)
    bottom = jnp.concatenate([place, jnp.zeros_like(place)], axis=1)
    return jnp.concatenate([top, bottom], axis=0)


def _unpad_ukv(g, heads):
    g = g[:MLA_KV_RANK]
    k_part = g[:, :heads * LANES].reshape(MLA_KV_RANK, heads, LANES)[:, :, :MLA_NOPE]
    v_part = g[:, heads * LANES:].reshape(MLA_KV_RANK, heads, LANES)[:, :, :MLA_V]
    return jnp.stack([k_part, v_part], axis=2).reshape(MLA_KV_RANK, heads * (MLA_NOPE + MLA_V))


def _rope_tables(positions):
    inv_freq = ROPE_THETA ** (-jnp.arange(0, MLA_ROPE, 2, dtype=F32) / MLA_ROPE)
    ang = positions.astype(F32)[:, None] * inv_freq
    cos, sin = jnp.cos(ang), jnp.sin(ang)
    s = positions.shape[0]
    cos_t = jnp.ones((s, LANES), F32).at[:, ROPE_LO:ROPE_LO + MLA_ROPE].set(jnp.concatenate([cos, cos], axis=1))
    sin_t = jnp.zeros((s, LANES), F32).at[:, ROPE_LO:ROPE_LO + MLA_ROPE].set(jnp.concatenate([-sin, sin], axis=1))
    return cos_t, sin_t


def _local_step(x, positions, target, qkv_w0, norms, rest_weights, ride=None):
    s, d = x.shape
    heads = d // SB_HEAD_DIM
    cos_t, sin_t = _rope_tables(positions)

    h_first = _rms_fwd(x, norms["attn_norm"][0], name="l0_attn_norm")
    qkv_first = _mm(h_first, _pad_qkv(qkv_w0, heads), name="l0_qkv")
    if ride is None:
        o_first, gathered = _sb_fwd(qkv_first, heads, name="l0_sb_fwd"), None
    else:
        o_first, gathered = _sb_fwd(qkv_first, heads, name="l0_sb_fwd", ride=ride)
    w = rest_weights(gathered)
    n_a = w["sb_w_qkv"].shape[0]
    n_b = w["mla_w_dq"].shape[0]
    depth = n_a + n_b

    wqkv = [_pad_qkv(w["sb_w_qkv"][l], heads) for l in range(n_a)]
    wo_a = [_pad_o(w["sb_w_o"][l], heads) for l in range(n_a)]
    wdkv = _pad_dkv(w["mla_w_dkv"])
    wkv = _pad_ukv(w["mla_w_ukv"], heads)
    wdq = [w["mla_w_dq"][j] for j in range(n_b)]
    wuq = [_pad_uq(w["mla_w_uq"][j], heads) for j in range(n_b)]
    wo_b = [_pad_o(w["mla_w_o"][j], heads) for j in range(n_b)]
    w1 = [w["mlp_w1"][l] for l in range(depth)]
    w2 = [w["mlp_w2"][l] for l in range(depth)]

    saved = []
    kv_saved = None
    kv = None
    for l in range(depth):
        t = f"l{l}"
        sv = {"x_in": x}
        h = h_first if l == 0 else _rms_fwd(x, norms["attn_norm"][l], name=f"{t}_attn_norm")
        sv["h"] = h
        if l < n_a:
            if l == 0:
                qkv, o = qkv_first, o_first
            else:
                qkv = _mm(h, wqkv[l], name=f"{t}_qkv")
                o = _sb_fwd(qkv, heads, name=f"{t}_sb_fwd")
            sv["qkv"], sv["o"] = qkv, o
            x = _mm(o, wo_a[l], name=f"{t}_attn_out", epilogue=_epi_add, extras=[(x, "tile")], out_dtypes=(F32,))
        else:
            j = l - n_a
            if j == 0:
                hk = _rms_fwd(x, norms["kv_norm"], name="kv_norm")
                down = _mm(hk, wdkv, name="kv_down", out_dtypes=(F32,))
                cat = _kv_prep(down, norms["mla_kv_lat_norm"], cos_t, sin_t, name="kv_prep")
                kv = _mm(cat, wkv, name="kv_up")
                kv_saved = {"x_in": x, "hk": hk, "down": down, "cat": cat}
            cq0 = _mm(h, wdq[j], name=f"{t}_q_down", out_dtypes=(F32,))
            cq = _rms_fwd(cq0, norms["mla_q_lat_norm"][j], name=f"{t}_q_lat_norm")
            q = _mm(cq, wuq[j], name=f"{t}_q_up", epilogue=_epi_rope_heads, extras=[(cos_t, "row"), (sin_t, "row")])
            o, lse = _mla_fwd(q, kv, heads, name=f"{t}_mla_fwd")
            sv.update(cq0=cq0, cq=cq, q=q, o=o, lse=lse)
            x = _mm(o, wo_b[j], name=f"{t}_attn_out", epilogue=_epi_add, extras=[(x, "tile")], out_dtypes=(F32,))
        sv["x_mid"] = x
        h2 = _rms_fwd(x, norms["mlp_norm"][l], name=f"{t}_mlp_norm")
        u, a = _mm(h2, w1[l], name=f"{t}_mlp_up", epilogue=_epi_relu2, out_dtypes=(BF16, BF16))
        sv.update(h2=h2, u=u, a=a)
        x = _mm(a, w2[l], name=f"{t}_mlp_down", epilogue=_epi_add, extras=[(x, "tile")], out_dtypes=(F32,))
        saved.append(sv)

    loss_slab, dx, dxb, dg_final = _loss_bwd(x, norms["final_norm"], target, name="loss")
    loss = loss_slab[0, 0]

    g_attn_norm, g_mlp_norm = [None] * depth, [None] * depth
    g_qkv, g_o_a = [None] * n_a, [None] * n_a
    g_dq, g_uq, g_o_b, g_qlat = [None] * n_b, [None] * n_b, [None] * n_b, [None] * n_b
    g_w1, g_w2 = [None] * depth, [None] * depth
    dkv = None
    g_kv_norm = g_kv_lat = g_dkv = g_ukv = None

    for l in reversed(range(depth)):
        t = f"l{l}"
        sv = saved[l]
        du = _mm(dxb, w2[l], name=f"{t}_mlp_down_dx", dims="nt", epilogue=_epi_relu2_grad, extras=[(sv["u"], "tile")])
        g_w2[l] = _mm(sv["a"], dxb, name=f"{t}_mlp_down_dw", dims="tn", out_dtypes=(F32,))
        g_w1[l] = _mm(sv["h2"], du, name=f"{t}_mlp_up_dw", dims="tn", out_dtypes=(F32,))
        dh2 = _mm(du, w1[l], name=f"{t}_mlp_up_dx", dims="nt", out_dtypes=(F32,))
        dx, dxb, g_mlp_norm[l] = _rms_bwd(sv["x_mid"], norms["mlp_norm"][l], dh2, dx, name=f"{t}_mlp_norm_bwd")
        if l < n_a:
            do = _mm(dxb, wo_a[l], name=f"{t}_attn_out_dx", dims="nt")
            g_o_a[l] = _unpad_o(_mm(sv["o"], dxb, name=f"{t}_attn_out_dw", dims="tn", out_dtypes=(F32,)), heads)
            dq, dk, dv = _sb_bwd(sv["qkv"], sv["o"], do, heads, name=f"{t}_sb_bwd")
            dqkv = jnp.concatenate([dq, dk, dv], axis=1)
            g_qkv[l] = _unpad_qkv(_mm(sv["h"], dqkv, name=f"{t}_qkv_dw", dims="tn", out_dtypes=(F32,)), heads)
            dh = _mm(dqkv, wqkv[l], name=f"{t}_qkv_dx", dims="nt", out_dtypes=(F32,))
        else:
            j = l - n_a
            do = _mm(dxb, wo_b[j], name=f"{t}_attn_out_dx", dims="nt")
            g_o_b[j] = _unpad_o(_mm(sv["o"], dxb, name=f"{t}_attn_out_dw", dims="tn", out_dtypes=(F32,)), heads)
            dq, dkv = _mla_bwd(sv["q"], kv, sv["o"], do, sv["lse"], cos_t, sin_t, dkv, heads, name=f"{t}_mla_bwd")
            g_uq[j] = _unpad_uq(_mm(sv["cq"], dq, name=f"{t}_q_up_dw", dims="tn", out_dtypes=(F32,)), heads)
            dcq = _mm(dq, wuq[j], name=f"{t}_q_up_dx", dims="nt", out_dtypes=(F32,))
            _, dcq0, g_qlat[j] = _rms_bwd(sv["cq0"], norms["mla_q_lat_norm"][j], dcq, None, name=f"{t}_q_lat_norm_bwd")
            g_dq[j] = _mm(sv["h"], dcq0, name=f"{t}_q_down_dw", dims="tn", out_dtypes=(F32,))
            dh = _mm(dcq0, wdq[j], name=f"{t}_q_down_dx", dims="nt", out_dtypes=(F32,))
        dx, dxb, g_attn_norm[l] = _rms_bwd(sv["x_in"], norms["attn_norm"][l], dh, dx, name=f"{t}_attn_norm_bwd",
                                           lead_axis=(l == 0))
        if l == n_a:
            ks = kv_saved
            dcat = _mm(dkv, wkv, name="kv_up_dx", dims="nt", out_dtypes=(F32,))
            g_ukv = _unpad_ukv(_mm(ks["cat"], dkv, name="kv_up_dw", dims="tn", out_dtypes=(F32,)), heads)
            ddown, g_kv_lat = _kv_prep_bwd(ks["down"], norms["mla_kv_lat_norm"], cos_t, sin_t, dcat, name="kv_prep_bwd")
            g_dkv = _unpad_dkv(_mm(ks["hk"], ddown, name="kv_down_dw", dims="tn", out_dtypes=(F32,)))
            dhk = _mm(ddown, wdkv, name="kv_down_dx", dims="nt", out_dtypes=(F32,))
            dx, dxb, g_kv_norm = _rms_bwd(ks["x_in"], norms["kv_norm"], dhk, dx, name="kv_norm_bwd")

    grads = {
        "attn_norm": jnp.concatenate(g_attn_norm, axis=0), "mlp_norm": jnp.concatenate(g_mlp_norm, axis=0),
        "sb_w_qkv": g_qkv, "sb_w_o": g_o_a,
        "kv_norm": g_kv_norm[0], "mla_w_dkv": g_dkv, "mla_kv_lat_norm": g_kv_lat[0], "mla_w_ukv": g_ukv,
        "mla_w_dq": g_dq, "mla_q_lat_norm": jnp.concatenate(g_qlat, axis=0),
        "mla_w_uq": g_uq, "mla_w_o": g_o_b,
        "mlp_w1": g_w1, "mlp_w2": g_w2, "final_norm": dg_final[0],
    }
    return loss, dx, grads


def _flat_rows(n_elems):
    per_block = FLAT_COLS * FLAT_ROW_BLOCK * 2
    return -(-n_elems // per_block) * FLAT_ROW_BLOCK * 2


def _pack(arrays, dtype):
    flat = jnp.concatenate([a.reshape(-1).astype(dtype) for a in arrays])
    rows = _flat_rows(flat.shape[0])
    flat = jnp.pad(flat, (0, rows * FLAT_COLS - flat.shape[0]))
    return flat.reshape(rows, FLAT_COLS)


def _pack_chips(per_chip, dtype):
    n_elems = sum(a.size for a in per_chip[0])
    rows = _flat_rows(n_elems)
    tail = jnp.zeros((rows * FLAT_COLS - n_elems,), dtype)
    flat = jnp.concatenate([piece for arrays in per_chip
                            for piece in [a.reshape(-1).astype(dtype) for a in arrays] + [tail]])
    return flat.reshape(len(per_chip), rows, FLAT_COLS)


def _unpack(flat, shapes):
    flat = flat.reshape(-1)
    out, off = [], 0
    for shp in shapes:
        n = 1
        for v in shp:
            n *= v
        out.append(flat[off:off + n].reshape(shp))
        off += n
    return out


def _pack_small(arrays):
    rows = []
    for a in arrays:
        a = a.reshape(-1, a.shape[-1]) if a.shape[-1] == FLAT_COLS else a.reshape(1, -1)
        rows.append(_pad_last(a, FLAT_COLS))
    flat = jnp.concatenate(rows, axis=0)
    return jnp.pad(flat, [(0, -flat.shape[0] % 8), (0, 0)])


def _unpack_small(flat, shapes):
    out, row = [], 0
    for shp in shapes:
        if shp[-1] == FLAT_COLS:
            n = 1
            for v in shp[:-1]:
                n *= v
            out.append(flat[row:row + n].reshape(shp))
            row += n
        else:
            n = 1
            for v in shp:
                n *= v
            out.append(flat[row, :n].reshape(shp))
            row += 1
    return out


def _other_chips(x, y):
    return [(1 - x, y), (x, 1 - y), (1 - x, 1 - y)]


def _all_gather_chips(flat, *, name):
    rows, cols = flat.shape

    def body(x_ref, out_ref, *sems):
        _all_gather_start(x_ref, out_ref, sems)
        _all_gather_finish(x_ref, out_ref, sems)

    return pl.pallas_call(
        body, name=name,
        in_specs=[pl.BlockSpec(memory_space=pltpu.HBM)],
        out_specs=pl.BlockSpec(memory_space=pltpu.HBM),
        out_shape=jax.ShapeDtypeStruct((N_CHIPS, rows, cols), flat.dtype),
        scratch_shapes=_all_gather_sems(),
        compiler_params=pltpu.CompilerParams(has_side_effects=True),
    )(flat)


def _all_gather_sems():
    return [pltpu.SemaphoreType.DMA((3,)), pltpu.SemaphoreType.DMA((3,)), pltpu.SemaphoreType.DMA((3,)),
            pltpu.SemaphoreType.DMA((3,)), pltpu.SemaphoreType.DMA, pltpu.SemaphoreType.DMA]


def _all_gather_copies(x_ref, out_ref, sems):
    send_sems, recv_sems, pass_send_sems, pass_recv_sems, own_send_sem, own_recv_sem = sems
    x, y, c = lax.axis_index("x"), lax.axis_index("y"), lax.axis_index("c")
    me = 2 * x + y
    my_rows, sib_rows = _half_rows(x_ref.shape[0])

    def copy(src, dst, send_sem, recv_sem, to):
        return pltpu.make_async_remote_copy(src_ref=src, dst_ref=dst, send_sem=send_sem, recv_sem=recv_sem,
                                            device_id=to, device_id_type=MESH)

    own = copy(x_ref, out_ref.at[me], own_send_sem, own_recv_sem, _sibling())
    to_chips, landed, pass_on, passed = [], [], [], []
    for k, (px, py) in enumerate(_other_chips(x, y)):
        to_chips.append(copy(x_ref.at[my_rows, :], out_ref.at[me, my_rows, :], send_sems.at[k], recv_sems.at[k],
                             (px, py, c)))
        mine, theirs = out_ref.at[2 * px + py, my_rows, :], out_ref.at[2 * px + py, sib_rows, :]
        landed.append(copy(mine, mine, send_sems.at[k], recv_sems.at[k], (px, py, c)))
        pass_on.append(copy(mine, mine, pass_send_sems.at[k], pass_recv_sems.at[k], _sibling()))
        passed.append(copy(theirs, theirs, pass_send_sems.at[k], pass_recv_sems.at[k], _sibling()))
    return own, to_chips, landed, pass_on, passed


def _all_gather_start(x_ref, out_ref, sems):
    own, to_chips, _, _, _ = _all_gather_copies(x_ref, out_ref, sems)
    own.start()
    for cp in to_chips:
        cp.start()


def _all_gather_finish(x_ref, out_ref, sems):
    own, to_chips, landed, pass_on, passed = _all_gather_copies(x_ref, out_ref, sems)
    for k in range(len(landed)):
        landed[k].wait_recv()
        pass_on[k].start()
    for cp in passed:
        cp.wait_recv()
    own.wait_recv()
    for cp in [own] + to_chips + pass_on:
        cp.wait_send()


def _exchange_chips(parts, *, name):
    def body(g_ref, out_ref, send_sems, recv_sems):
        x, y, c = lax.axis_index("x"), lax.axis_index("y"), lax.axis_index("c")
        me = 2 * x + y
        sends = []
        for k, (px, py) in enumerate(_other_chips(x, y)):
            cp = pltpu.make_async_remote_copy(src_ref=g_ref.at[2 * px + py], dst_ref=out_ref.at[me],
                                              send_sem=send_sems.at[k], recv_sem=recv_sems.at[k],
                                              device_id=(px, py, c), device_id_type=MESH)
            cp.start()
            sends.append(cp)
        for k, (px, py) in enumerate(_other_chips(x, y)):
            pltpu.make_async_remote_copy(src_ref=g_ref.at[me], dst_ref=out_ref.at[2 * px + py],
                                         send_sem=send_sems.at[k], recv_sem=recv_sems.at[k],
                                         device_id=(px, py, c), device_id_type=MESH).wait_recv()
        for cp in sends:
            cp.wait_send()

    out = pl.pallas_call(
        body, name=name,
        in_specs=[pl.BlockSpec(memory_space=pltpu.HBM)],
        out_specs=pl.BlockSpec(memory_space=pltpu.HBM),
        out_shape=jax.ShapeDtypeStruct(parts.shape, parts.dtype),
        scratch_shapes=[pltpu.SemaphoreType.DMA((3,)), pltpu.SemaphoreType.DMA((3,))],
        compiler_params=pltpu.CompilerParams(has_side_effects=True),
    )(parts)
    me = _my_chip()
    return lax.dynamic_update_index_in_dim(out, lax.dynamic_index_in_dim(parts, me, 0, keepdims=False), me, 0)


def _my_chip():
    return 2 * lax.axis_index("x") + lax.axis_index("y")


def _half_rows(rows):
    c = lax.axis_index("c")
    half = rows // 2
    return pl.ds(pl.multiple_of(c * half, 8), half), pl.ds(pl.multiple_of((1 - c) * half, 8), half)


def _sibling():
    return (lax.axis_index("x"), lax.axis_index("y"), 1 - lax.axis_index("c"))


def _pair_exchange(parts, *, name):
    n, rows, cols = parts.shape

    def body(p_ref, theirs_ref, send_sem, recv_sem):
        _, sib_rows = _half_rows(rows)
        cp = pltpu.make_async_remote_copy(src_ref=p_ref.at[:, sib_rows, :], dst_ref=theirs_ref, send_sem=send_sem,
                                          recv_sem=recv_sem, device_id=_sibling(), device_id_type=MESH)
        cp.start()
        cp.wait()

    half = rows // 2
    theirs = pl.pallas_call(
        body, name=name,
        in_specs=[pl.BlockSpec(memory_space=pltpu.HBM)],
        out_specs=pl.BlockSpec(memory_space=pltpu.HBM),
        out_shape=jax.ShapeDtypeStruct((n, half, cols), parts.dtype),
        scratch_shapes=[pltpu.SemaphoreType.DMA, pltpu.SemaphoreType.DMA],
        compiler_params=pltpu.CompilerParams(has_side_effects=True),
    )(parts)
    mine = lax.dynamic_slice_in_dim(parts, lax.axis_index("c") * half, half, axis=1)
    return mine, theirs


def _pair_sum(mine, theirs, *, name):
    n, rows, cols = mine.shape

    def body(a_ref, b_ref, o_ref):
        o_ref[...] = (a_ref[...].astype(F32) + b_ref[...].astype(F32)).astype(o_ref.dtype)

    blk = pl.BlockSpec((n, FLAT_ROW_BLOCK, cols), lambda i: (0, i, 0))
    return pl.pallas_call(
        body, name=name, grid=(rows // FLAT_ROW_BLOCK,),
        in_specs=[blk, blk], out_specs=blk, out_shape=jax.ShapeDtypeStruct(mine.shape, mine.dtype),
        compiler_params=pltpu.CompilerParams(dimension_semantics=("parallel",), vmem_limit_bytes=VMEM_LIMIT),
    )(mine, theirs)


def _sum_chips(parts, *, name):
    _, rows, cols = parts.shape

    def body(p_ref, o_ref):
        o_ref[...] = ((p_ref[0].astype(F32) + p_ref[1].astype(F32)) + p_ref[2].astype(F32)) + p_ref[3].astype(F32)

    return pl.pallas_call(
        body, name=name, grid=(rows // FLAT_ROW_BLOCK,),
        in_specs=[pl.BlockSpec((N_CHIPS, FLAT_ROW_BLOCK, cols), lambda i: (0, i, 0))],
        out_specs=pl.BlockSpec((FLAT_ROW_BLOCK, cols), lambda i: (i, 0)),
        out_shape=jax.ShapeDtypeStruct((rows, cols), F32),
        compiler_params=pltpu.CompilerParams(dimension_semantics=("parallel",), vmem_limit_bytes=VMEM_LIMIT),
    )(parts)


def _join_cores(half, *, name):
    rows2, cols = half.shape

    def body(h_ref, out_ref, send_sem, recv_sem):
        my_rows, sib_rows = _half_rows(2 * rows2)
        cp = pltpu.make_async_remote_copy(src_ref=h_ref, dst_ref=out_ref.at[my_rows, :], send_sem=send_sem,
                                          recv_sem=recv_sem, device_id=_sibling(), device_id_type=MESH)
        cp.start()
        cp.wait_send()
        pltpu.make_async_remote_copy(src_ref=h_ref, dst_ref=out_ref.at[sib_rows, :], send_sem=send_sem,
                                     recv_sem=recv_sem, device_id=_sibling(), device_id_type=MESH).wait_recv()

    out = pl.pallas_call(
        body, name=name,
        in_specs=[pl.BlockSpec(memory_space=pltpu.HBM)],
        out_specs=pl.BlockSpec(memory_space=pltpu.HBM),
        out_shape=jax.ShapeDtypeStruct((2 * rows2, cols), half.dtype),
        scratch_shapes=[pltpu.SemaphoreType.DMA, pltpu.SemaphoreType.DMA],
        compiler_params=pltpu.CompilerParams(has_side_effects=True),
    )(half)
    return lax.dynamic_update_slice_in_dim(out, half, lax.axis_index("c") * rows2, axis=0)


def _all_reduce_small(v, *, name):
    rows, cols = v.shape
    flips = [(fx, fy, fc) for fx in (0, 1) for fy in (0, 1) for fc in (0, 1)][1:]

    def body(v_ref, out_ref, gath_ref, send_sems, recv_sems):
        x, y, c = lax.axis_index("x"), lax.axis_index("y"), lax.axis_index("c")
        me = 4 * x + 2 * y + c
        gath_ref[me] = v_ref[...]
        peers = [((1 - x) if fx else x, (1 - y) if fy else y, (1 - c) if fc else c) for fx, fy, fc in flips]
        sends = []
        for k, peer in enumerate(peers):
            cp = pltpu.make_async_remote_copy(src_ref=v_ref, dst_ref=gath_ref.at[me], send_sem=send_sems.at[k],
                                              recv_sem=recv_sems.at[k], device_id=peer, device_id_type=MESH)
            cp.start()
            sends.append(cp)
        for k, (px, py, pc) in enumerate(peers):
            pltpu.make_async_remote_copy(src_ref=v_ref, dst_ref=gath_ref.at[4 * px + 2 * py + pc],
                                         send_sem=send_sems.at[k], recv_sem=recv_sems.at[k],
                                         device_id=(px, py, pc), device_id_type=MESH).wait_recv()
        for cp in sends:
            cp.wait_send()
        total = gath_ref[0]
        for k in range(1, 8):
            total = total + gath_ref[k]
        out_ref[...] = total

    total, _ = pl.pallas_call(
        body, name=name,
        in_specs=[pl.BlockSpec(memory_space=pltpu.VMEM)],
        out_specs=[pl.BlockSpec(memory_space=pltpu.VMEM), pl.BlockSpec(memory_space=pltpu.VMEM)],
        out_shape=[jax.ShapeDtypeStruct((rows, cols), v.dtype), jax.ShapeDtypeStruct((8, rows, cols), v.dtype)],
        scratch_shapes=[pltpu.SemaphoreType.DMA((7,)), pltpu.SemaphoreType.DMA((7,))],
        compiler_params=pltpu.CompilerParams(has_side_effects=True),
    )(v)
    return total


def _adamw(w, g, m, v, *, name):
    shape = w.shape
    cols = shape[-1]
    w2, g2, m2, v2 = (a.reshape(-1, cols) for a in (w, g, m, v))
    rows = w2.shape[0]
    br = _pick_rows(rows, FLAT_ROW_BLOCK)

    def body(w_ref, g_ref, m_ref, v_ref, d_out, m_out, v_out):
        gv = g_ref[...]
        m_new = ADAM_B1 * m_ref[...] + (1.0 - ADAM_B1) * gv
        v_new = ADAM_B2 * v_ref[...] + (1.0 - ADAM_B2) * jnp.square(gv)
        m_hat = m_new / (1.0 - ADAM_B1 ** ADAM_STEP)
        v_hat = v_new / (1.0 - ADAM_B2 ** ADAM_STEP)
        d_out[...] = -ADAM_LR * (m_hat / (jnp.sqrt(v_hat) + ADAM_EPS) + ADAM_WD * w_ref[...])
        m_out[...] = m_new
        v_out[...] = v_new

    blk = pl.BlockSpec((br, cols), lambda i: (i, 0))
    out = jax.ShapeDtypeStruct((rows, cols), F32)
    outs = pl.pallas_call(
        body, name=name, grid=(rows // br,),
        in_specs=[blk] * 4, out_specs=[blk] * 3, out_shape=[out] * 3,
        compiler_params=pltpu.CompilerParams(dimension_semantics=("parallel",), vmem_limit_bytes=VMEM_LIMIT),
    )(w2, g2, m2, v2)
    return [o.reshape(shape) for o in outs]


def _pick_rows(rows, target):
    if rows <= target:
        return rows
    return max(b for b in range(8, target + 1, 8) if rows % b == 0)


def _assemble(gathered_shards, name, layer=False):
    return jnp.concatenate(gathered_shards, axis=SHARD_AXIS[name] - int(layer))


def _chip_shard(full, name, j):
    if isinstance(full, list):
        axis = SHARD_AXIS[name] - 1
        layers = full
    else:
        axis = SHARD_AXIS[name]
        layers = [full]
    n = layers[0].shape[axis] // N_CHIPS
    return [lax.slice_in_dim(g, j * n, (j + 1) * n, axis=axis) for g in layers]


def kernel(x, positions, attn_norm, mlp_norm, sb_w_qkv, sb_w_o, kv_norm, mla_w_dkv, mla_kv_lat_norm, mla_w_ukv, mla_w_dq, mla_q_lat_norm, mla_w_uq, mla_w_o, mlp_w1, mlp_w2, final_norm, loss_target, m_attn_norm, m_mlp_norm, m_sb_w_qkv, m_sb_w_o, m_kv_norm, m_mla_w_dkv, m_mla_kv_lat_norm, m_mla_w_ukv, m_mla_w_dq, m_mla_q_lat_norm, m_mla_w_uq, m_mla_w_o, m_mlp_w1, m_mlp_w2, m_final_norm, v_attn_norm, v_mlp_norm, v_sb_w_qkv, v_sb_w_o, v_kv_norm, v_mla_w_dkv, v_mla_kv_lat_norm, v_mla_w_ukv, v_mla_w_dq, v_mla_q_lat_norm, v_mla_w_uq, v_mla_w_o, v_mlp_w1, v_mlp_w2, v_final_norm):
    weights = dict(attn_norm=attn_norm, mlp_norm=mlp_norm, sb_w_qkv=sb_w_qkv, sb_w_o=sb_w_o, kv_norm=kv_norm,
                   mla_w_dkv=mla_w_dkv, mla_kv_lat_norm=mla_kv_lat_norm, mla_w_ukv=mla_w_ukv, mla_w_dq=mla_w_dq,
                   mla_q_lat_norm=mla_q_lat_norm, mla_w_uq=mla_w_uq, mla_w_o=mla_w_o, mlp_w1=mlp_w1, mlp_w2=mlp_w2,
                   final_norm=final_norm)
    m_in = dict(attn_norm=m_attn_norm, mlp_norm=m_mlp_norm, sb_w_qkv=m_sb_w_qkv, sb_w_o=m_sb_w_o, kv_norm=m_kv_norm,
                mla_w_dkv=m_mla_w_dkv, mla_kv_lat_norm=m_mla_kv_lat_norm, mla_w_ukv=m_mla_w_ukv, mla_w_dq=m_mla_w_dq,
                mla_q_lat_norm=m_mla_q_lat_norm, mla_w_uq=m_mla_w_uq, mla_w_o=m_mla_w_o, mlp_w1=m_mlp_w1,
                mlp_w2=m_mlp_w2, final_norm=m_final_norm)
    v_in = dict(attn_norm=v_attn_norm, mlp_norm=v_mlp_norm, sb_w_qkv=v_sb_w_qkv, sb_w_o=v_sb_w_o, kv_norm=v_kv_norm,
                mla_w_dkv=v_mla_w_dkv, mla_kv_lat_norm=v_mla_kv_lat_norm, mla_w_ukv=v_mla_w_ukv, mla_w_dq=v_mla_w_dq,
                mla_q_lat_norm=v_mla_q_lat_norm, mla_w_uq=v_mla_w_uq, mla_w_o=v_mla_w_o, mlp_w1=v_mlp_w1,
                mlp_w2=v_mlp_w2, final_norm=v_final_norm)
    shard_shapes = [weights[n].shape for n in BIG_WEIGHTS]
    small_shapes = [weights[n].shape for n in SMALL_WEIGHTS]

    first_name = BIG_WEIGHTS[0]
    qkv_first, qkv_later = weights[first_name][0], weights[first_name][1:]
    gathered_first = _all_gather_chips(_pack([qkv_first], BF16), name="first_weight_all_gather")
    qkv_w0 = _assemble([_unpack(gathered_first[j], [qkv_first.shape])[0] for j in range(N_CHIPS)], first_name, layer=True)
    ride = _pack([qkv_later] + [weights[n] for n in BIG_WEIGHTS[1:]], BF16)
    ride_shapes = [qkv_later.shape] + shard_shapes[1:]

    def rest_weights(gathered):
        per_chip = [_unpack(gathered[j], ride_shapes) for j in range(N_CHIPS)]
        full = {n: _assemble([per_chip[j][i] for j in range(N_CHIPS)], n) for i, n in enumerate(BIG_WEIGHTS)}
        full[first_name] = jnp.concatenate([qkv_w0[None], full[first_name]], axis=0)
        return full

    norms = {n: weights[n] for n in SMALL_WEIGHTS}

    loss, dx, grads = _local_step(x[0], positions[0], loss_target, qkv_w0, norms, rest_weights, ride=ride)
    loss = lax.psum(loss, ("x", "y", "c"))

    parts = _pack_chips([[piece for n in BIG_WEIGHTS for piece in _chip_shard(grads[n], n, j)]
                         for j in range(N_CHIPS)], BF16)
    mine, theirs = _pair_exchange(parts, name="grads_pair_exchange")
    chip_part = _pair_sum(mine, theirs, name="grads_pair_sum")
    received = _exchange_chips(chip_part, name="grads_exchange")
    g_half = _sum_chips(received, name="grads_sum_chips")
    g_sum = _join_cores(g_half, name="grads_join_cores")
    out_g = dict(zip(BIG_WEIGHTS, _unpack(g_sum, shard_shapes)))
    out_d, out_m, out_v = {}, {}, {}
    for n in BIG_WEIGHTS:
        out_d[n], out_m[n], out_v[n] = _adamw(weights[n], out_g[n], m_in[n], v_in[n], name=f"adamw_{n}")

    small_sum = _all_reduce_small(_pack_small([grads[n] for n in SMALL_WEIGHTS]), name="gains_all_reduce")
    sd, sm, sv = _adamw(_pack_small([weights[n] for n in SMALL_WEIGHTS]), small_sum,
                        _pack_small([m_in[n] for n in SMALL_WEIGHTS]),
                        _pack_small([v_in[n] for n in SMALL_WEIGHTS]), name="adamw_gains")
    out_g.update(zip(SMALL_WEIGHTS, _unpack_small(small_sum, small_shapes)))
    out_d.update(zip(SMALL_WEIGHTS, _unpack_small(sd, small_shapes)))
    out_m.update(zip(SMALL_WEIGHTS, _unpack_small(sm, small_shapes)))
    out_v.update(zip(SMALL_WEIGHTS, _unpack_small(sv, small_shapes)))

    return (loss, dx, *[out_g[n] for n in ALL_WEIGHTS], *[out_d[n] for n in ALL_WEIGHTS],
            *[out_m[n] for n in ALL_WEIGHTS], *[out_v[n] for n in ALL_WEIGHTS])
```

```python
import functools

import jax
import jax.numpy as jnp
from jax import lax
from jax.experimental import pallas as pl
from jax.experimental.pallas import tpu as pltpu

F32 = jnp.float32
BF16 = jnp.bfloat16

LANES = 128
SB_HEAD_DIM = 64
MLA_NOPE = 64
MLA_ROPE = 32
MLA_V = 64
MLA_Q_RANK = 384
MLA_KV_RANK = 256
CHUNK = 64
ROPE_THETA = 10000.0
NORM_EPS = 1e-6
SB_SCALE = SB_HEAD_DIM ** -0.5
MLA_SCALE = (MLA_NOPE + MLA_ROPE) ** -0.5
ROPE_LO = MLA_NOPE
ROPE_HALF = MLA_ROPE // 2
ATT_Q_BLOCK = 1024
ATT_K_BLOCK = 256
MLA_FWD_K_BLOCK = 512
NEG_BIG = -1e30
SB_DEAD_LOG = -110.0
VMEM_LIMIT = 56 * 1024 * 1024

ADAM_LR = 0.001
ADAM_B1 = 0.9
ADAM_B2 = 0.999
ADAM_EPS = 1e-08
ADAM_WD = 0.01
ADAM_STEP = 10

FLAT_COLS = 1024
FLAT_ROW_BLOCK = 256
N_CHIPS = 4
MESH = pl.DeviceIdType.MESH

BIG_WEIGHTS = ["sb_w_qkv", "sb_w_o", "mla_w_dkv", "mla_w_ukv", "mla_w_dq", "mla_w_uq", "mla_w_o", "mlp_w1", "mlp_w2"]
SHARD_AXIS = {"sb_w_qkv": 2, "sb_w_o": 1, "mla_w_dkv": 0, "mla_w_ukv": 1, "mla_w_dq": 1, "mla_w_uq": 2,
              "mla_w_o": 1, "mlp_w1": 2, "mlp_w2": 1}
SMALL_WEIGHTS = ["attn_norm", "mlp_norm", "kv_norm", "mla_kv_lat_norm", "mla_q_lat_norm", "final_norm"]
ALL_WEIGHTS = ["attn_norm", "mlp_norm", "sb_w_qkv", "sb_w_o", "kv_norm", "mla_w_dkv", "mla_kv_lat_norm", "mla_w_ukv",
               "mla_w_dq", "mla_q_lat_norm", "mla_w_uq", "mla_w_o", "mlp_w1", "mlp_w2", "final_norm"]


def _dot(a, b, dims):
    return lax.dot_general(a, b, (dims, ((), ())), preferred_element_type=F32)


def _dot_nn(a, b):
    return _dot(a, b, ((1,), (0,)))


def _dot_nt(a, b):
    return _dot(a, b, ((1,), (1,)))


def _dot_tn(a, b):
    return _dot(a, b, ((0,), (0,)))


def _pick_block(n, target):
    if n <= target:
        return n
    best = max(b for b in range(LANES, target + 1, LANES) if n % b == 0)
    return best


MM_ROWS = 512
MM_COLS = 1024
MM_DEPTH = 4096
MM_DEPTH_TN = 1024


def _mm(a, b, *, name, dims="nn", epilogue=None, extras=(), out_dtypes=(BF16,)):
    if dims == "nn":
        (m, k), (k2, n) = a.shape, b.shape
    elif dims == "nt":
        (m, k), (n, k2) = a.shape, b.shape
    else:
        (k, m), (k2, n) = a.shape, b.shape
    assert k == k2, (name, a.shape, b.shape)
    if dims == "tn":
        bm, bn, bk = _pick_block(m, MM_COLS), _pick_block(n, MM_COLS), _pick_block(k, MM_DEPTH_TN)
    else:
        rows = MM_ROWS if k > MM_DEPTH // 2 else 2 * MM_ROWS
        bm, bn, bk = _pick_block(m, rows), _pick_block(n, MM_COLS), _pick_block(k, MM_DEPTH)
    nk = k // bk
    if dims == "tn":
        a_spec = pl.BlockSpec((bk, bm), lambda j, i, kk: (kk, i))
    else:
        a_spec = pl.BlockSpec((bm, bk), lambda j, i, kk: (i, kk))
    if dims == "nt":
        b_spec = pl.BlockSpec((bn, bk), lambda j, i, kk: (j, kk))
    else:
        b_spec = pl.BlockSpec((bk, bn), lambda j, i, kk: (kk, j))
    extra_specs = []
    for arr, kind in extras:
        if kind == "tile":
            assert arr.shape == (m, n), (name, arr.shape)
            extra_specs.append(pl.BlockSpec((bm, bn), lambda j, i, kk: (i, j)))
        else:
            assert arr.shape == (m, LANES), (name, arr.shape)
            extra_specs.append(pl.BlockSpec((bm, LANES), lambda j, i, kk: (i, 0)))
    n_extra = len(extras)
    n_out = len(out_dtypes)
    dot = {"nn": _dot_nn, "nt": _dot_nt, "tn": _dot_tn}[dims]

    def body(*refs):
        a_ref, b_ref = refs[0], refs[1]
        extra_refs = refs[2:2 + n_extra]
        out_refs = refs[2 + n_extra:2 + n_extra + n_out]

        def finish(acc):
            outs = (acc,) if epilogue is None else epilogue(acc, *[r[...] for r in extra_refs])
            for o_ref, o in zip(out_refs, outs):
                o_ref[...] = o.astype(o_ref.dtype)

        part = dot(a_ref[...].astype(BF16), b_ref[...].astype(BF16))
        if nk == 1:
            finish(part)
            return
        acc_ref = refs[-1]
        kk = pl.program_id(2)

        @pl.when(kk == 0)
        def _():
            acc_ref[...] = part

        @pl.when(kk > 0)
        def _():
            acc_ref[...] += part

        @pl.when(kk == nk - 1)
        def _():
            finish(acc_ref[...])

    outs = pl.pallas_call(
        body, name=name, grid=(n // bn, m // bm, nk),
        in_specs=[a_spec, b_spec] + extra_specs,
        out_specs=[pl.BlockSpec((bm, bn), lambda j, i, kk: (i, j)) for _ in range(n_out)],
        out_shape=[jax.ShapeDtypeStruct((m, n), dt) for dt in out_dtypes],
        scratch_shapes=[pltpu.VMEM((bm, bn), F32)] if nk > 1 else [],
        compiler_params=pltpu.CompilerParams(dimension_semantics=("parallel", "parallel", "arbitrary"),
                                             vmem_limit_bytes=VMEM_LIMIT),
    )(a, b, *[arr for arr, _ in extras])
    return outs[0] if n_out == 1 else outs


def _epi_add(acc, res):
    return (res + acc,)


def _epi_relu2(acc):
    r = jnp.maximum(acc, 0.0)
    return acc, r * r


def _epi_relu2_grad(acc, u):
    return (acc * (2.0 * jnp.maximum(u.astype(F32), 0.0)),)


def _rope_slab(t, cos_t, sin_t):
    lane = lax.broadcasted_iota(jnp.int32, t.shape, 1)
    partner = jnp.where(lane < ROPE_LO + ROPE_HALF, pltpu.roll(t, LANES - ROPE_HALF, 1), pltpu.roll(t, ROPE_HALF, 1))
    return t * cos_t + partner * sin_t


def _rope_slab_bwd(d, cos_t, sin_t):
    ds = d * sin_t
    lane = lax.broadcasted_iota(jnp.int32, d.shape, 1)
    partner = jnp.where(lane < ROPE_LO + ROPE_HALF, pltpu.roll(ds, LANES - ROPE_HALF, 1), pltpu.roll(ds, ROPE_HALF, 1))
    in_rope = (lane >= ROPE_LO) & (lane < ROPE_LO + MLA_ROPE)
    return d * cos_t + jnp.where(in_rope, partner, 0.0)


def _epi_rope_heads(acc, cos_t, sin_t):
    slabs = [_rope_slab(acc[:, j * LANES:(j + 1) * LANES], cos_t, sin_t) for j in range(acc.shape[1] // LANES)]
    return (jnp.concatenate(slabs, axis=1) * MLA_SCALE,)


def _row_block(s):
    return min(512, s)


def _rms_fwd(x, g, *, name):
    s, d = x.shape
    bm = _row_block(s)

    def body(x_ref, g_ref, o_ref):
        xv = x_ref[...]
        r = lax.rsqrt(jnp.mean(xv * xv, axis=-1, keepdims=True) + NORM_EPS)
        o_ref[...] = ((xv * r) * g_ref[...]).astype(o_ref.dtype)

    return pl.pallas_call(
        body, name=name, grid=(s // bm,),
        in_specs=[pl.BlockSpec((bm, d), lambda i: (i, 0)), pl.BlockSpec((1, d), lambda i: (0, 0))],
        out_specs=pl.BlockSpec((bm, d), lambda i: (i, 0)),
        out_shape=jax.ShapeDtypeStruct((s, d), BF16),
        compiler_params=pltpu.CompilerParams(dimension_semantics=("parallel",), vmem_limit_bytes=VMEM_LIMIT),
    )(x, g.reshape(1, d))


def _rms_bwd_math(xv, gv, dy):
    r = lax.rsqrt(jnp.mean(xv * xv, axis=-1, keepdims=True) + NORM_EPS)
    xhat = xv * r
    dyg = dy * gv
    mdot = jnp.mean(dyg * xhat, axis=-1, keepdims=True)
    dx = r * (dyg - xhat * mdot)
    dg = jnp.sum(dy * xhat, axis=0, keepdims=True)
    return dx, dg


def _rms_bwd(x, g, dy, dres, *, name, lead_axis=False):
    s, d = x.shape
    bm = _row_block(s)
    has_res = dres is not None

    def body(*refs):
        x_ref, g_ref, dy_ref = refs[:3]
        dres_ref = refs[3] if has_res else None
        dx_ref, dxb_ref, dg_ref = refs[-3:]
        dx, dg = _rms_bwd_math(x_ref[...], g_ref[...], dy_ref[...].astype(F32))
        if has_res:
            dx = dx + dres_ref[...]
        dx_ref[...] = dx
        dxb_ref[...] = dx.astype(BF16)

        @pl.when(pl.program_id(0) == 0)
        def _():
            dg_ref[...] = jnp.zeros_like(dg_ref)

        dg_ref[...] += dg

    row = pl.BlockSpec((bm, d), lambda i: (i, 0))
    vec = pl.BlockSpec((1, d), lambda i: (0, 0))
    ins = [x, g.reshape(1, d), dy] + ([dres] if has_res else [])
    dx_spec, dx_shape = row, (s, d)
    if lead_axis:
        dx_spec, dx_shape = pl.BlockSpec((None, bm, d), lambda i: (0, i, 0)), (1, s, d)
    return pl.pallas_call(
        body, name=name, grid=(s // bm,),
        in_specs=[row, vec, row] + ([row] if has_res else []),
        out_specs=[dx_spec, row, vec],
        out_shape=[jax.ShapeDtypeStruct(dx_shape, F32), jax.ShapeDtypeStruct((s, d), BF16),
                   jax.ShapeDtypeStruct((1, d), F32)],
        compiler_params=pltpu.CompilerParams(dimension_semantics=("arbitrary",), vmem_limit_bytes=VMEM_LIMIT),
    )(*ins)


def _loss_bwd(x, g, target, *, name):
    s, d = x.shape
    bm = _row_block(s)

    def body(x_ref, g_ref, t_ref, loss_ref, dx_ref, dxb_ref, dg_ref):
        xv, gv = x_ref[...], g_ref[...]
        r = lax.rsqrt(jnp.mean(xv * xv, axis=-1, keepdims=True) + NORM_EPS)
        err = (xv * r) * gv - t_ref[...]
        dx, dg = _rms_bwd_math(xv, gv, err * (1.0 / d))
        dx_ref[...] = dx
        dxb_ref[...] = dx.astype(BF16)

        @pl.when(pl.program_id(0) == 0)
        def _():
            dg_ref[...] = jnp.zeros_like(dg_ref)
            loss_ref[...] = jnp.zeros_like(loss_ref)

        dg_ref[...] += dg
        loss_ref[...] += jnp.sum(jnp.mean(err * err, axis=-1, keepdims=True), axis=0, keepdims=True) * 0.5

    row = pl.BlockSpec((bm, d), lambda i: (i, 0))
    vec = pl.BlockSpec((1, d), lambda i: (0, 0))
    assert target.shape == (1, s, d), target.shape
    return pl.pallas_call(
        body, name=name, grid=(s // bm,),
        in_specs=[row, vec, pl.BlockSpec((None, bm, d), lambda i: (0, i, 0))],
        out_specs=[pl.BlockSpec((8, LANES), lambda i: (0, 0)), row, row, vec],
        out_shape=[jax.ShapeDtypeStruct((8, LANES), F32), jax.ShapeDtypeStruct((s, d), F32),
                   jax.ShapeDtypeStruct((s, d), BF16), jax.ShapeDtypeStruct((1, d), F32)],
        compiler_params=pltpu.CompilerParams(dimension_semantics=("arbitrary",), vmem_limit_bytes=VMEM_LIMIT),
    )(x, g.reshape(1, d), target)


def _kv_prep(down, g, cos_t, sin_t, *, name):
    s, w = down.shape
    bm = _row_block(s)

    def body(d_ref, g_ref, c_ref, s_ref, o_ref):
        lat = d_ref[:, :MLA_KV_RANK]
        r = lax.rsqrt(jnp.mean(lat * lat, axis=-1, keepdims=True) + NORM_EPS)
        o_ref[:, :MLA_KV_RANK] = ((lat * r) * g_ref[...]).astype(BF16)
        o_ref[:, MLA_KV_RANK:] = _rope_slab(d_ref[:, MLA_KV_RANK:], c_ref[...], s_ref[...]).astype(BF16)

    row = pl.BlockSpec((bm, w), lambda i: (i, 0))
    tab = pl.BlockSpec((bm, LANES), lambda i: (i, 0))
    return pl.pallas_call(
        body, name=name, grid=(s // bm,),
        in_specs=[row, pl.BlockSpec((1, MLA_KV_RANK), lambda i: (0, 0)), tab, tab],
        out_specs=row, out_shape=jax.ShapeDtypeStruct((s, w), BF16),
        compiler_params=pltpu.CompilerParams(dimension_semantics=("parallel",), vmem_limit_bytes=VMEM_LIMIT),
    )(down, g.reshape(1, MLA_KV_RANK), cos_t, sin_t)


def _kv_prep_bwd(down, g, cos_t, sin_t, dcat, *, name):
    s, w = down.shape
    bm = _row_block(s)

    def body(d_ref, g_ref, c_ref, s_ref, dc_ref, o_ref, dg_ref):
        dlat, dg = _rms_bwd_math(d_ref[:, :MLA_KV_RANK], g_ref[...], dc_ref[:, :MLA_KV_RANK])
        o_ref[:, :MLA_KV_RANK] = dlat.astype(BF16)
        o_ref[:, MLA_KV_RANK:] = _rope_slab_bwd(dc_ref[:, MLA_KV_RANK:], c_ref[...], s_ref[...]).astype(BF16)

        @pl.when(pl.program_id(0) == 0)
        def _():
            dg_ref[...] = jnp.zeros_like(dg_ref)

        dg_ref[...] += dg

    row = pl.BlockSpec((bm, w), lambda i: (i, 0))
    tab = pl.BlockSpec((bm, LANES), lambda i: (i, 0))
    vec = pl.BlockSpec((1, MLA_KV_RANK), lambda i: (0, 0))
    return pl.pallas_call(
        body, name=name, grid=(s // bm,),
        in_specs=[row, vec, tab, tab, row],
        out_specs=[row, vec],
        out_shape=[jax.ShapeDtypeStruct((s, w), BF16), jax.ShapeDtypeStruct((1, MLA_KV_RANK), F32)],
        compiler_params=pltpu.CompilerParams(dimension_semantics=("arbitrary",), vmem_limit_bytes=VMEM_LIMIT),
    )(down, g.reshape(1, MLA_KV_RANK), cos_t, sin_t, dcat)


def _split_bf16(v):
    hi = v.astype(BF16)
    lo = (v - hi.astype(F32)).astype(BF16)
    return hi, lo


def _suffix_matrices(n):
    row = lax.broadcasted_iota(jnp.int32, (n, n), 0)
    col = lax.broadcasted_iota(jnp.int32, (n, n), 1)
    incl = (row >= col).astype(BF16)
    return (row > col).astype(BF16), jnp.concatenate([incl, incl], axis=0)


def _suffix_sum(v, matrix):
    hi, lo = _split_bf16(v)
    return _dot_nn(jnp.concatenate([hi, lo], axis=1), matrix)


def _block_positions(qi, kb, bq, bk, r0, r1):
    row = qi * bq + r0 + lax.broadcasted_iota(jnp.int32, (r1 - r0, bk), 0)
    col = kb * bk + lax.broadcasted_iota(jnp.int32, (r1 - r0, bk), 1)
    return row, col


def _att_blocks(s, key_block=ATT_K_BLOCK):
    bq, bk = min(ATT_Q_BLOCK, s), min(key_block, s)
    return bq, bk, s // bq, bq // bk


def _sweep(qi, ratio, bk, step, unroll=2, alive=None):
    bq = ratio * bk
    for d in range(ratio):
        kb, r0 = (qi + 1) * ratio - 1 - d, (ratio - 1 - d) * bk
        near = bq if alive is None else min(r0 + 2 * bk, bq)
        step(kb, True, r0, near)
        if near < bq:
            pl.when(alive(near))(functools.partial(step, kb, False, near, bq))
    unroll = unroll if ratio % unroll == 0 else 1
    trips = qi * (ratio // unroll)

    def trip(i):
        for u in range(unroll):
            step(qi * ratio - 1 - (i * unroll + u), False, 0, bq)

    if alive is None:
        lax.fori_loop(0, trips, lambda i, carry: (trip(i), carry)[1], 0)
    else:
        lax.while_loop(lambda i: jnp.logical_and(i < trips, alive(0)), lambda i: (trip(i), i + 1)[1], 0)


def _stick_left(c_ref, r0):
    return jnp.max(c_ref[r0:, :]) > SB_DEAD_LOG


def _sb_logs(q, k):
    z = _dot_nt(q, k)
    lb = jnp.minimum(z, 0.0) - jnp.log(1.0 + jnp.exp(-jnp.abs(z)))
    return lb, lb - z


def _sb_fwd(qkv, heads, *, name, ride=None):
    s = qkv.shape[0]
    bq, bk, nq, ratio = _att_blocks(s)
    riding = ride is not None

    def body(*refs):
        if riding:
            q_ref, k_ref, v_ref, w_ref, o_ref, gath_ref, acc_ref, c_ref = refs[:8]
            first = jnp.logical_and(pl.program_id(0) == 0, pl.program_id(1) == 0)
            last = jnp.logical_and(pl.program_id(0) == heads - 1, pl.program_id(1) == nq - 1)
            pl.when(first)(functools.partial(_all_gather_start, w_ref, gath_ref, refs[8:]))
        else:
            q_ref, k_ref, v_ref, o_ref, acc_ref, c_ref = refs
        qi = pl.program_id(1)
        q = q_ref[...] * SB_SCALE
        m_strict, _ = _suffix_matrices(bk)
        acc_ref[...] = jnp.zeros_like(acc_ref)
        c_ref[...] = jnp.zeros_like(c_ref)

        def step(kb, masked, r0, r1):
            rows = pl.ds(pl.multiple_of(kb * bk, bk), bk)
            mine = pl.ds(r0, r1 - r0)
            k, v = k_ref[rows, :], v_ref[rows, :]
            lb, lk = _sb_logs(q[r0:r1], k)
            if masked:
                row, col = _block_positions(qi, kb, bq, bk, r0, r1)
                causal = col < row
                lk = jnp.where(causal, lk, 0.0)
            c = c_ref[mine, :]
            w = jnp.exp(lb + _dot_nn(lk.astype(BF16), m_strict) + jnp.tile(c, (1, bk // LANES)))
            if masked:
                w = jnp.where(causal, w, 0.0)
            acc_ref[mine, :] += _dot_nn(w.astype(BF16), v)
            c_ref[mine, :] = c + jnp.sum(lk, axis=-1, keepdims=True)

        _sweep(qi, ratio, bk, step, unroll=1, alive=functools.partial(_stick_left, c_ref))
        o_ref[...] = acc_ref[...].astype(o_ref.dtype)
        if riding:
            pl.when(last)(functools.partial(_all_gather_finish, w_ref, gath_ref, refs[8:]))

    hbm = pl.BlockSpec(memory_space=pltpu.HBM)
    o_spec = pl.BlockSpec((bq, LANES), lambda h, i: (i, h))
    o_shape = jax.ShapeDtypeStruct((s, heads * LANES), F32)
    return pl.pallas_call(
        body, name=name, grid=(heads, nq),
        in_specs=[pl.BlockSpec((bq, LANES), lambda h, i: (i, h)),
                  pl.BlockSpec((s, LANES), lambda h, i: (0, heads + h)),
                  pl.BlockSpec((s, LANES), lambda h, i: (0, 2 * heads + h))] + ([hbm] if riding else []),
        out_specs=[o_spec, hbm] if riding else o_spec,
        out_shape=[o_shape, jax.ShapeDtypeStruct((N_CHIPS,) + ride.shape, ride.dtype)] if riding else o_shape,
        scratch_shapes=[pltpu.VMEM((bq, LANES), F32), pltpu.VMEM((bq, LANES), F32)]
        + (_all_gather_sems() if riding else []),
        compiler_params=pltpu.CompilerParams(dimension_semantics=("arbitrary", "arbitrary"),
                                             vmem_limit_bytes=VMEM_LIMIT, has_side_effects=riding),
    )(*([qkv, qkv, qkv] + ([ride] if riding else [])))


def _sb_bwd(qkv, o, do, heads, *, name):
    s = qkv.shape[0]
    bq, bk, nq, ratio = _att_blocks(s)

    def body(q_ref, k_ref, v_ref, o_ref, do_ref, dq_ref, dk_ref, dv_ref, dq_acc, dk_acc, dv_acc, c_ref, e_ref):
        qi = pl.program_id(1)

        @pl.when(qi == 0)
        def _():
            dk_acc[...] = jnp.zeros_like(dk_acc)
            dv_acc[...] = jnp.zeros_like(dv_acc)

        q = q_ref[...] * SB_SCALE
        do = do_ref[...]
        q_t, do_t = q.T, do.T
        total = jnp.sum(do.astype(F32) * o_ref[...].astype(F32), axis=-1, keepdims=True)
        m_strict, m_incl = _suffix_matrices(bk)
        dq_acc[...] = jnp.zeros_like(dq_acc)
        c_ref[...] = jnp.zeros_like(c_ref)
        e_ref[...] = jnp.broadcast_to(total, e_ref.shape)
        reps = (1, bk // LANES)

        def step(kb, masked, r0, r1):
            rows = pl.ds(pl.multiple_of(kb * bk, bk), bk)
            mine = pl.ds(r0, r1 - r0)
            k, v = k_ref[rows, :], v_ref[rows, :]
            qs, dos = q[r0:r1], do[r0:r1]
            lb, lk_all = _sb_logs(qs, k)
            lk = lk_all
            if masked:
                row, col = _block_positions(qi, kb, bq, bk, r0, r1)
                causal = col < row
                lk = jnp.where(causal, lk_all, 0.0)
            c = c_ref[mine, :]
            w = jnp.exp(lb + _dot_nn(lk.astype(BF16), m_strict) + jnp.tile(c, reps))
            if masked:
                w = jnp.where(causal, w, 0.0)
            wb = w.astype(BF16)
            g = wb.astype(F32) * _dot_nt(dos, v)
            e = e_ref[mine, :]
            g_left = jnp.tile(e, reps) - _suffix_sum(g, m_incl)
            da = g * jnp.exp(lk_all) - jnp.exp(lb) * g_left
            if masked:
                da = jnp.where(causal, da, 0.0)
            dab = da.astype(BF16)
            dq_acc[mine, :] += _dot_nn(dab, k)
            dk_acc[:, rows] += _dot_nn(q_t[:, r0:r1], dab)
            dv_acc[:, rows] += _dot_nn(do_t[:, r0:r1], wb)
            e_ref[mine, :] = e - jnp.sum(g, axis=-1, keepdims=True)
            c_ref[mine, :] = c + jnp.sum(lk, axis=-1, keepdims=True)

        _sweep(qi, ratio, bk, step, unroll=1, alive=functools.partial(_stick_left, c_ref))
        dq_ref[...] = (dq_acc[...] * SB_SCALE).astype(dq_ref.dtype)

        @pl.when(qi == nq - 1)
        def _():
            dk_ref[...] = dk_acc[...].T.astype(dk_ref.dtype)
            dv_ref[...] = dv_acc[...].T.astype(dv_ref.dtype)

    blk = pl.BlockSpec((bq, LANES), lambda h, i: (i, h))
    full = pl.BlockSpec((s, LANES), lambda h, i: (0, h))
    shape = jax.ShapeDtypeStruct((s, heads * LANES), BF16)
    return pl.pallas_call(
        body, name=name, grid=(heads, nq),
        in_specs=[blk,
                  pl.BlockSpec((s, LANES), lambda h, i: (0, heads + h)),
                  pl.BlockSpec((s, LANES), lambda h, i: (0, 2 * heads + h)),
                  blk, blk],
        out_specs=[blk, full, full],
        out_shape=[shape, shape, shape],
        scratch_shapes=[pltpu.VMEM((bq, LANES), F32), pltpu.VMEM((LANES, s), F32), pltpu.VMEM((LANES, s), F32),
                        pltpu.VMEM((bq, LANES), F32), pltpu.VMEM((bq, LANES), F32)],
        compiler_params=pltpu.CompilerParams(dimension_semantics=("arbitrary", "arbitrary"),
                                             vmem_limit_bytes=VMEM_LIMIT),
    )(qkv, qkv, qkv, o, do)


def _chunk_allowed(qi, kb, bq, bk, r0, r1):
    row, col = _block_positions(qi, kb, bq, bk, r0, r1)
    return (col // CHUNK) <= (row // CHUNK)


def _mla_fwd(q, kv, heads, *, name):
    s = q.shape[0]
    bq, bk, nq, ratio = _att_blocks(s, MLA_FWD_K_BLOCK)
    reps = (1, bk // LANES)

    def body(q_ref, k_ref, v_ref, o_ref, lse_ref, acc_ref, m_ref, l_ref):
        qi = pl.program_id(1)
        qv = q_ref[...]
        acc_ref[...] = jnp.zeros_like(acc_ref)
        m_ref[...] = jnp.full_like(m_ref, NEG_BIG)
        l_ref[...] = jnp.zeros_like(l_ref)

        def step(kb, masked, r0, r1):
            rows = pl.ds(pl.multiple_of(kb * bk, bk), bk)
            mine = pl.ds(r0, r1 - r0)
            k, v = k_ref[rows, :], v_ref[rows, :]
            sc = _dot_nt(qv[r0:r1], k)
            if masked:
                allowed = _chunk_allowed(qi, kb, bq, bk, r0, r1)
                sc = jnp.where(allowed, sc, NEG_BIG)
            m_old = m_ref[mine, :]
            m_new = jnp.maximum(m_old, jnp.max(sc, axis=-1, keepdims=True))
            p = jnp.exp(sc - jnp.tile(m_new, reps))
            alpha = jnp.exp(m_old - m_new)
            l_ref[mine, :] = alpha * l_ref[mine, :] + jnp.sum(p, axis=-1, keepdims=True)
            acc_ref[mine, :] = alpha * acc_ref[mine, :] + _dot_nn(p.astype(BF16), v)
            m_ref[mine, :] = m_new

        _sweep(qi, ratio, bk, step)
        o_ref[...] = (acc_ref[...] / l_ref[...]).astype(o_ref.dtype)
        lse_ref[...] = m_ref[...] + jnp.log(l_ref[...])

    blk = pl.BlockSpec((bq, LANES), lambda h, i: (i, h))
    return pl.pallas_call(
        body, name=name, grid=(heads, nq),
        in_specs=[blk,
                  pl.BlockSpec((s, LANES), lambda h, i: (0, h)),
                  pl.BlockSpec((s, LANES), lambda h, i: (0, heads + h))],
        out_specs=[blk, blk],
        out_shape=[jax.ShapeDtypeStruct((s, heads * LANES), BF16), jax.ShapeDtypeStruct((s, heads * LANES), F32)],
        scratch_shapes=[pltpu.VMEM((bq, LANES), F32), pltpu.VMEM((bq, LANES), F32), pltpu.VMEM((bq, LANES), F32)],
        compiler_params=pltpu.CompilerParams(dimension_semantics=("parallel", "arbitrary"),
                                             vmem_limit_bytes=VMEM_LIMIT),
    )(q, kv, kv)


def _mla_bwd(q, kv, o, do, lse, cos_t, sin_t, dkv_init, heads, *, name):
    s = q.shape[0]
    bq, bk, nq, ratio = _att_blocks(s)
    reps = (1, bk // LANES)
    has_init = dkv_init is not None

    def body(*refs):
        q_ref, k_ref, v_ref, o_ref, do_ref, lse_ref, c_ref, s_ref = refs[:8]
        ki_ref, vi_ref = (refs[8], refs[9]) if has_init else (None, None)
        dq_ref, dk_ref, dv_ref, dq_acc, dk_acc, dv_acc = refs[-6:]
        qi = pl.program_id(1)

        @pl.when(qi == 0)
        def _():
            if has_init:
                dk_acc[...] = ki_ref[...].astype(F32).T
                dv_acc[...] = vi_ref[...].astype(F32).T
            else:
                dk_acc[...] = jnp.zeros_like(dk_acc)
                dv_acc[...] = jnp.zeros_like(dv_acc)

        qv = q_ref[...]
        do = do_ref[...]
        q_t, do_t = qv.T, do.T
        delta = jnp.sum(do.astype(F32) * o_ref[...].astype(F32), axis=-1, keepdims=True)
        lse_wide = jnp.tile(lse_ref[...], reps)
        dq_acc[...] = jnp.zeros_like(dq_acc)

        def step(kb, masked, r0, r1):
            rows = pl.ds(pl.multiple_of(kb * bk, bk), bk)
            k, v = k_ref[rows, :], v_ref[rows, :]
            qs, dos = qv[r0:r1], do[r0:r1]
            p = jnp.exp(_dot_nt(qs, k) - lse_wide[r0:r1])
            if masked:
                p = jnp.where(_chunk_allowed(qi, kb, bq, bk, r0, r1), p, 0.0)
            ds = (p * (_dot_nt(dos, v) - delta[r0:r1])).astype(BF16)
            dq_acc[pl.ds(r0, r1 - r0), :] += _dot_nn(ds, k)
            dk_acc[:, rows] += _dot_nn(q_t[:, r0:r1], ds)
            dv_acc[:, rows] += _dot_nn(do_t[:, r0:r1], p.astype(BF16))

        _sweep(qi, ratio, bk, step)
        dq_ref[...] = _rope_slab_bwd(dq_acc[...] * MLA_SCALE, c_ref[...], s_ref[...]).astype(dq_ref.dtype)

        @pl.when(qi == nq - 1)
        def _():
            dk_ref[...] = dk_acc[...].T.astype(dk_ref.dtype)
            dv_ref[...] = dv_acc[...].T.astype(dv_ref.dtype)

    blk = pl.BlockSpec((bq, LANES), lambda h, i: (i, h))
    tab = pl.BlockSpec((bq, LANES), lambda h, i: (i, 0))
    k_full = pl.BlockSpec((s, LANES), lambda h, i: (0, h))
    v_full = pl.BlockSpec((s, LANES), lambda h, i: (0, heads + h))
    shape = jax.ShapeDtypeStruct((s, heads * LANES), BF16)
    ins = [q, kv, kv, o, do, lse, cos_t, sin_t] + ([dkv_init, dkv_init] if has_init else [])
    dq, dk, dv = pl.pallas_call(
        body, name=name, grid=(heads, nq),
        in_specs=[blk, k_full, v_full, blk, blk, blk, tab, tab] + ([k_full, v_full] if has_init else []),
        out_specs=[blk, k_full, k_full],
        out_shape=[shape, shape, shape],
        scratch_shapes=[pltpu.VMEM((bq, LANES), F32), pltpu.VMEM((LANES, s), F32), pltpu.VMEM((LANES, s), F32)],
        compiler_params=pltpu.CompilerParams(dimension_semantics=("arbitrary", "arbitrary"),
                                             vmem_limit_bytes=VMEM_LIMIT),
    )(*ins)
    return dq, jnp.concatenate([dk, dv], axis=1)


def _pad_last(a, width):
    return jnp.pad(a, [(0, 0)] * (a.ndim - 1) + [(0, width - a.shape[-1])])


def _pad_qkv(w, heads):
    d = w.shape[0]
    return _pad_last(w.reshape(d, 3 * heads, SB_HEAD_DIM), LANES).reshape(d, 3 * heads * LANES)


def _unpad_qkv(g, heads):
    d = g.shape[0]
    return g.reshape(d, 3 * heads, LANES)[:, :, :SB_HEAD_DIM].reshape(d, 3 * heads * SB_HEAD_DIM)


def _pad_o(w, heads):
    d = w.shape[1]
    w = w.reshape(heads, SB_HEAD_DIM, d)
    return jnp.pad(w, [(0, 0), (0, LANES - SB_HEAD_DIM), (0, 0)]).reshape(heads * LANES, d)


def _unpad_o(g, heads):
    d = g.shape[1]
    return g.reshape(heads, LANES, d)[:, :SB_HEAD_DIM, :].reshape(heads * SB_HEAD_DIM, d)


def _pad_uq(w, heads):
    r = w.shape[0]
    return _pad_last(w.reshape(r, heads, MLA_NOPE + MLA_ROPE), LANES).reshape(r, heads * LANES)


def _unpad_uq(g, heads):
    r = g.shape[0]
    return g.reshape(r, heads, LANES)[:, :, :MLA_NOPE + MLA_ROPE].reshape(r, heads * (MLA_NOPE + MLA_ROPE))


def _pad_dkv(w):
    d = w.shape[0]
    rope = jnp.zeros((d, LANES), w.dtype).at[:, ROPE_LO:ROPE_LO + MLA_ROPE].set(w[:, MLA_KV_RANK:])
    return jnp.concatenate([w[:, :MLA_KV_RANK], rope], axis=1)


def _unpad_dkv(g):
    return jnp.concatenate([g[:, :MLA_KV_RANK], g[:, MLA_KV_RANK + ROPE_LO:MLA_KV_RANK + ROPE_LO + MLA_ROPE]], axis=1)


def _pad_ukv(w, heads):
    w = w.reshape(MLA_KV_RANK, heads, 2, MLA_NOPE)
    k_part = _pad_last(w[:, :, 0, :], LANES).reshape(MLA_KV_RANK, heads * LANES)
    v_part = _pad_last(w[:, :, 1, :], LANES).reshape(MLA_KV_RANK, heads * LANES)
    lane = jnp.arange(LANES)
    place = ((lane[:, None] == lane[None, :]) & (lane[:, None] >= ROPE_LO) & (lane[:, None] < ROPE_LO + MLA_ROPE))
    place = jnp.tile(place.astype(w.dtype), (1, heads))
    top = jnp.concatenate([k_part, v_part], axis=1)
    bottom = jnp.concatenate([place, jnp.zeros_like(place)], axis=1)
    return jnp.concatenate([top, bottom], axis=0)


def _unpad_ukv(g, heads):
    g = g[:MLA_KV_RANK]
    k_part = g[:, :heads * LANES].reshape(MLA_KV_RANK, heads, LANES)[:, :, :MLA_NOPE]
    v_part = g[:, heads * LANES:].reshape(MLA_KV_RANK, heads, LANES)[:, :, :MLA_V]
    return jnp.stack([k_part, v_part], axis=2).reshape(MLA_KV_RANK, heads * (MLA_NOPE + MLA_V))


def _rope_tables(positions):
    inv_freq = ROPE_THETA ** (-jnp.arange(0, MLA_ROPE, 2, dtype=F32) / MLA_ROPE)
    ang = positions.astype(F32)[:, None] * inv_freq
    cos, sin = jnp.cos(ang), jnp.sin(ang)
    s = positions.shape[0]
    cos_t = jnp.ones((s, LANES), F32).at[:, ROPE_LO:ROPE_LO + MLA_ROPE].set(jnp.concatenate([cos, cos], axis=1))
    sin_t = jnp.zeros((s, LANES), F32).at[:, ROPE_LO:ROPE_LO + MLA_ROPE].set(jnp.concatenate([-sin, sin], axis=1))
    return cos_t, sin_t


def _local_step(x, positions, target, qkv_w0, norms, rest_weights, ride=None):
    s, d = x.shape
    heads = d // SB_HEAD_DIM
    cos_t, sin_t = _rope_tables(positions)

    h_first = _rms_fwd(x, norms["attn_norm"][0], name="l0_attn_norm")
    qkv_first = _mm(h_first, _pad_qkv(qkv_w0, heads), name="l0_qkv")
    if ride is None:
        o_first, gathered = _sb_fwd(qkv_first, heads, name="l0_sb_fwd"), None
    else:
        o_first, gathered = _sb_fwd(qkv_first, heads, name="l0_sb_fwd", ride=ride)
    w = rest_weights(gathered)
    n_a = w["sb_w_qkv"].shape[0]
    n_b = w["mla_w_dq"].shape[0]
    depth = n_a + n_b

    wqkv = [_pad_qkv(w["sb_w_qkv"][l], heads) for l in range(n_a)]
    wo_a = [_pad_o(w["sb_w_o"][l], heads) for l in range(n_a)]
    wdkv = _pad_dkv(w["mla_w_dkv"])
    wkv = _pad_ukv(w["mla_w_ukv"], heads)
    wdq = [w["mla_w_dq"][j] for j in range(n_b)]
    wuq = [_pad_uq(w["mla_w_uq"][j], heads) for j in range(n_b)]
    wo_b = [_pad_o(w["mla_w_o"][j], heads) for j in range(n_b)]
    w1 = [w["mlp_w1"][l] for l in range(depth)]
    w2 = [w["mlp_w2"][l] for l in range(depth)]

    saved = []
    kv_saved = None
    kv = None
    for l in range(depth):
        t = f"l{l}"
        sv = {"x_in": x}
        h = h_first if l == 0 else _rms_fwd(x, norms["attn_norm"][l], name=f"{t}_attn_norm")
        sv["h"] = h
        if l < n_a:
            if l == 0:
                qkv, o = qkv_first, o_first
            else:
                qkv = _mm(h, wqkv[l], name=f"{t}_qkv")
                o = _sb_fwd(qkv, heads, name=f"{t}_sb_fwd")
            sv["qkv"], sv["o"] = qkv, o
            x = _mm(o, wo_a[l], name=f"{t}_attn_out", epilogue=_epi_add, extras=[(x, "tile")], out_dtypes=(F32,))
        else:
            j = l - n_a
            if j == 0:
                hk = _rms_fwd(x, norms["kv_norm"], name="kv_norm")
                down = _mm(hk, wdkv, name="kv_down", out_dtypes=(F32,))
                cat = _kv_prep(down, norms["mla_kv_lat_norm"], cos_t, sin_t, name="kv_prep")
                kv = _mm(cat, wkv, name="kv_up")
                kv_saved = {"x_in": x, "hk": hk, "down": down, "cat": cat}
            cq0 = _mm(h, wdq[j], name=f"{t}_q_down", out_dtypes=(F32,))
            cq = _rms_fwd(cq0, norms["mla_q_lat_norm"][j], name=f"{t}_q_lat_norm")
            q = _mm(cq, wuq[j], name=f"{t}_q_up", epilogue=_epi_rope_heads, extras=[(cos_t, "row"), (sin_t, "row")])
            o, lse = _mla_fwd(q, kv, heads, name=f"{t}_mla_fwd")
            sv.update(cq0=cq0, cq=cq, q=q, o=o, lse=lse)
            x = _mm(o, wo_b[j], name=f"{t}_attn_out", epilogue=_epi_add, extras=[(x, "tile")], out_dtypes=(F32,))
        sv["x_mid"] = x
        h2 = _rms_fwd(x, norms["mlp_norm"][l], name=f"{t}_mlp_norm")
        u, a = _mm(h2, w1[l], name=f"{t}_mlp_up", epilogue=_epi_relu2, out_dtypes=(BF16, BF16))
        sv.update(h2=h2, u=u, a=a)
        x = _mm(a, w2[l], name=f"{t}_mlp_down", epilogue=_epi_add, extras=[(x, "tile")], out_dtypes=(F32,))
        saved.append(sv)

    loss_slab, dx, dxb, dg_final = _loss_bwd(x, norms["final_norm"], target, name="loss")
    loss = loss_slab[0, 0]

    g_attn_norm, g_mlp_norm = [None] * depth, [None] * depth
    g_qkv, g_o_a = [None] * n_a, [None] * n_a
    g_dq, g_uq, g_o_b, g_qlat = [None] * n_b, [None] * n_b, [None] * n_b, [None] * n_b
    g_w1, g_w2 = [None] * depth, [None] * depth
    dkv = None
    g_kv_norm = g_kv_lat = g_dkv = g_ukv = None

    for l in reversed(range(depth)):
        t = f"l{l}"
        sv = saved[l]
        du = _mm(dxb, w2[l], name=f"{t}_mlp_down_dx", dims="nt", epilogue=_epi_relu2_grad, extras=[(sv["u"], "tile")])
        g_w2[l] = _mm(sv["a"], dxb, name=f"{t}_mlp_down_dw", dims="tn", out_dtypes=(F32,))
        g_w1[l] = _mm(sv["h2"], du, name=f"{t}_mlp_up_dw", dims="tn", out_dtypes=(F32,))
        dh2 = _mm(du, w1[l], name=f"{t}_mlp_up_dx", dims="nt", out_dtypes=(F32,))
        dx, dxb, g_mlp_norm[l] = _rms_bwd(sv["x_mid"], norms["mlp_norm"][l], dh2, dx, name=f"{t}_mlp_norm_bwd")
        if l < n_a:
            do = _mm(dxb, wo_a[l], name=f"{t}_attn_out_dx", dims="nt")
            g_o_a[l] = _unpad_o(_mm(sv["o"], dxb, name=f"{t}_attn_out_dw", dims="tn", out_dtypes=(F32,)), heads)
            dq, dk, dv = _sb_bwd(sv["qkv"], sv["o"], do, heads, name=f"{t}_sb_bwd")
            dqkv = jnp.concatenate([dq, dk, dv], axis=1)
            g_qkv[l] = _unpad_qkv(_mm(sv["h"], dqkv, name=f"{t}_qkv_dw", dims="tn", out_dtypes=(F32,)), heads)
            dh = _mm(dqkv, wqkv[l], name=f"{t}_qkv_dx", dims="nt", out_dtypes=(F32,))
        else:
            j = l - n_a
            do = _mm(dxb, wo_b[j], name=f"{t}_attn_out_dx", dims="nt")
            g_o_b[j] = _unpad_o(_mm(sv["o"], dxb, name=f"{t}_attn_out_dw", dims="tn", out_dtypes=(F32,)), heads)
            dq, dkv = _mla_bwd(sv["q"], kv, sv["o"], do, sv["lse"], cos_t, sin_t, dkv, heads, name=f"{t}_mla_bwd")
            g_uq[j] = _unpad_uq(_mm(sv["cq"], dq, name=f"{t}_q_up_dw", dims="tn", out_dtypes=(F32,)), heads)
            dcq = _mm(dq, wuq[j], name=f"{t}_q_up_dx", dims="nt", out_dtypes=(F32,))
            _, dcq0, g_qlat[j] = _rms_bwd(sv["cq0"], norms["mla_q_lat_norm"][j], dcq, None, name=f"{t}_q_lat_norm_bwd")
            g_dq[j] = _mm(sv["h"], dcq0, name=f"{t}_q_down_dw", dims="tn", out_dtypes=(F32,))
            dh = _mm(dcq0, wdq[j], name=f"{t}_q_down_dx", dims="nt", out_dtypes=(F32,))
        dx, dxb, g_attn_norm[l] = _rms_bwd(sv["x_in"], norms["attn_norm"][l], dh, dx, name=f"{t}_attn_norm_bwd",
                                           lead_axis=(l == 0))
        if l == n_a:
            ks = kv_saved
            dcat = _mm(dkv, wkv, name="kv_up_dx", dims="nt", out_dtypes=(F32,))
            g_ukv = _unpad_ukv(_mm(ks["cat"], dkv, name="kv_up_dw", dims="tn", out_dtypes=(F32,)), heads)
            ddown, g_kv_lat = _kv_prep_bwd(ks["down"], norms["mla_kv_lat_norm"], cos_t, sin_t, dcat, name="kv_prep_bwd")
            g_dkv = _unpad_dkv(_mm(ks["hk"], ddown, name="kv_down_dw", dims="tn", out_dtypes=(F32,)))
            dhk = _mm(ddown, wdkv, name="kv_down_dx", dims="nt", out_dtypes=(F32,))
            dx, dxb, g_kv_norm = _rms_bwd(ks["x_in"], norms["kv_norm"], dhk, dx, name="kv_norm_bwd")

    grads = {
        "attn_norm": jnp.concatenate(g_attn_norm, axis=0), "mlp_norm": jnp.concatenate(g_mlp_norm, axis=0),
        "sb_w_qkv": g_qkv, "sb_w_o": g_o_a,
        "kv_norm": g_kv_norm[0], "mla_w_dkv": g_dkv, "mla_kv_lat_norm": g_kv_lat[0], "mla_w_ukv": g_ukv,
        "mla_w_dq": g_dq, "mla_q_lat_norm": jnp.concatenate(g_qlat, axis=0),
        "mla_w_uq": g_uq, "mla_w_o": g_o_b,
        "mlp_w1": g_w1, "mlp_w2": g_w2, "final_norm": dg_final[0],
    }
    return loss, dx, grads


def _flat_rows(n_elems):
    per_block = FLAT_COLS * FLAT_ROW_BLOCK * 2
    return -(-n_elems // per_block) * FLAT_ROW_BLOCK * 2


def _row_blocks(arrays, dtype):
    for a in arrays:
        assert a.size % FLAT_COLS == 0, a.shape
    blocks = [a.astype(dtype).reshape(-1, FLAT_COLS) for a in arrays]
    used = sum(b.shape[0] for b in blocks)
    rows = _flat_rows(used * FLAT_COLS)
    return blocks + [jnp.zeros((rows - used, FLAT_COLS), dtype)], rows


def _pack(arrays, dtype):
    blocks, _ = _row_blocks(arrays, dtype)
    return jnp.concatenate(blocks, axis=0)


def _pack_chips(per_chip, dtype):
    blocks, rows = [], 0
    for arrays in per_chip:
        chip_blocks, rows = _row_blocks(arrays, dtype)
        blocks += chip_blocks
    return jnp.concatenate(blocks, axis=0).reshape(len(per_chip), rows, FLAT_COLS)


def _unpack(flat, shapes):
    out, row = [], 0
    for shp in shapes:
        n = 1
        for v in shp:
            n *= v
        out.append(flat[row:row + n // FLAT_COLS].reshape(shp))
        row += n // FLAT_COLS
    return out


def _pack_small(arrays):
    rows = []
    for a in arrays:
        a = a.reshape(-1, a.shape[-1]) if a.shape[-1] == FLAT_COLS else a.reshape(1, -1)
        rows.append(_pad_last(a, FLAT_COLS))
    flat = jnp.concatenate(rows, axis=0)
    return jnp.pad(flat, [(0, -flat.shape[0] % 8), (0, 0)])


def _unpack_small(flat, shapes):
    out, row = [], 0
    for shp in shapes:
        if shp[-1] == FLAT_COLS:
            n = 1
            for v in shp[:-1]:
                n *= v
            out.append(flat[row:row + n].reshape(shp))
            row += n
        else:
            n = 1
            for v in shp:
                n *= v
            out.append(flat[row, :n].reshape(shp))
            row += 1
    return out


def _other_chips(x, y):
    return [(1 - x, y), (x, 1 - y), (1 - x, 1 - y)]


def _all_gather_chips(flat, *, name):
    rows, cols = flat.shape

    def body(x_ref, out_ref, *sems):
        _all_gather_start(x_ref, out_ref, sems)
        _all_gather_finish(x_ref, out_ref, sems)

    return pl.pallas_call(
        body, name=name,
        in_specs=[pl.BlockSpec(memory_space=pltpu.HBM)],
        out_specs=pl.BlockSpec(memory_space=pltpu.HBM),
        out_shape=jax.ShapeDtypeStruct((N_CHIPS, rows, cols), flat.dtype),
        scratch_shapes=_all_gather_sems(),
        compiler_params=pltpu.CompilerParams(has_side_effects=True),
    )(flat)


def _all_gather_sems():
    return [pltpu.SemaphoreType.DMA((3,)), pltpu.SemaphoreType.DMA((3,)), pltpu.SemaphoreType.DMA((3,)),
            pltpu.SemaphoreType.DMA((3,)), pltpu.SemaphoreType.DMA, pltpu.SemaphoreType.DMA]


def _all_gather_copies(x_ref, out_ref, sems):
    send_sems, recv_sems, pass_send_sems, pass_recv_sems, own_send_sem, own_recv_sem = sems
    x, y, c = lax.axis_index("x"), lax.axis_index("y"), lax.axis_index("c")
    me = 2 * x + y
    my_rows, sib_rows = _half_rows(x_ref.shape[0])

    def copy(src, dst, send_sem, recv_sem, to):
        return pltpu.make_async_remote_copy(src_ref=src, dst_ref=dst, send_sem=send_sem, recv_sem=recv_sem,
                                            device_id=to, device_id_type=MESH)

    own = copy(x_ref, out_ref.at[me], own_send_sem, own_recv_sem, _sibling())
    to_chips, landed, pass_on, passed = [], [], [], []
    for k, (px, py) in enumerate(_other_chips(x, y)):
        to_chips.append(copy(x_ref.at[my_rows, :], out_ref.at[me, my_rows, :], send_sems.at[k], recv_sems.at[k],
                             (px, py, c)))
        mine, theirs = out_ref.at[2 * px + py, my_rows, :], out_ref.at[2 * px + py, sib_rows, :]
        landed.append(copy(mine, mine, send_sems.at[k], recv_sems.at[k], (px, py, c)))
        pass_on.append(copy(mine, mine, pass_send_sems.at[k], pass_recv_sems.at[k], _sibling()))
        passed.append(copy(theirs, theirs, pass_send_sems.at[k], pass_recv_sems.at[k], _sibling()))
    return own, to_chips, landed, pass_on, passed


def _all_gather_start(x_ref, out_ref, sems):
    own, to_chips, _, _, _ = _all_gather_copies(x_ref, out_ref, sems)
    own.start()
    for cp in to_chips:
        cp.start()


def _all_gather_finish(x_ref, out_ref, sems):
    own, to_chips, landed, pass_on, passed = _all_gather_copies(x_ref, out_ref, sems)
    for k in range(len(landed)):
        landed[k].wait_recv()
        pass_on[k].start()
    for cp in passed:
        cp.wait_recv()
    own.wait_recv()
    for cp in [own] + to_chips + pass_on:
        cp.wait_send()


def _exchange_chips(parts, *, name):
    def body(g_ref, out_ref, send_sems, recv_sems):
        x, y, c = lax.axis_index("x"), lax.axis_index("y"), lax.axis_index("c")
        me = 2 * x + y
        sends = []
        for k, (px, py) in enumerate(_other_chips(x, y)):
            cp = pltpu.make_async_remote_copy(src_ref=g_ref.at[2 * px + py], dst_ref=out_ref.at[me],
                                              send_sem=send_sems.at[k], recv_sem=recv_sems.at[k],
                                              device_id=(px, py, c), device_id_type=MESH)
            cp.start()
            sends.append(cp)
        for k, (px, py) in enumerate(_other_chips(x, y)):
            pltpu.make_async_remote_copy(src_ref=g_ref.at[me], dst_ref=out_ref.at[2 * px + py],
                                         send_sem=send_sems.at[k], recv_sem=recv_sems.at[k],
                                         device_id=(px, py, c), device_id_type=MESH).wait_recv()
        for cp in sends:
            cp.wait_send()

    out = pl.pallas_call(
        body, name=name,
        in_specs=[pl.BlockSpec(memory_space=pltpu.HBM)],
        out_specs=pl.BlockSpec(memory_space=pltpu.HBM),
        out_shape=jax.ShapeDtypeStruct(parts.shape, parts.dtype),
        scratch_shapes=[pltpu.SemaphoreType.DMA((3,)), pltpu.SemaphoreType.DMA((3,))],
        compiler_params=pltpu.CompilerParams(has_side_effects=True),
    )(parts)
    return out


def _my_chip():
    return 2 * lax.axis_index("x") + lax.axis_index("y")


def _half_rows(rows):
    c = lax.axis_index("c")
    half = rows // 2
    return pl.ds(pl.multiple_of(c * half, 8), half), pl.ds(pl.multiple_of((1 - c) * half, 8), half)


def _sibling():
    return (lax.axis_index("x"), lax.axis_index("y"), 1 - lax.axis_index("c"))


def _pair_exchange(parts, *, name):
    n, rows, cols = parts.shape

    def body(p_ref, theirs_ref, send_sem, recv_sem):
        _, sib_rows = _half_rows(rows)
        cp = pltpu.make_async_remote_copy(src_ref=p_ref.at[:, sib_rows, :], dst_ref=theirs_ref, send_sem=send_sem,
                                          recv_sem=recv_sem, device_id=_sibling(), device_id_type=MESH)
        cp.start()
        cp.wait()

    half = rows // 2
    theirs = pl.pallas_call(
        body, name=name,
        in_specs=[pl.BlockSpec(memory_space=pltpu.HBM)],
        out_specs=pl.BlockSpec(memory_space=pltpu.HBM),
        out_shape=jax.ShapeDtypeStruct((n, half, cols), parts.dtype),
        scratch_shapes=[pltpu.SemaphoreType.DMA, pltpu.SemaphoreType.DMA],
        compiler_params=pltpu.CompilerParams(has_side_effects=True),
    )(parts)
    mine = lax.dynamic_slice_in_dim(parts, lax.axis_index("c") * half, half, axis=1)
    return mine, theirs


def _pair_sum(mine, theirs, *, name):
    n, rows, cols = mine.shape

    def body(a_ref, b_ref, o_ref):
        o_ref[...] = (a_ref[...].astype(F32) + b_ref[...].astype(F32)).astype(o_ref.dtype)

    blk = pl.BlockSpec((n, FLAT_ROW_BLOCK, cols), lambda i: (0, i, 0))
    return pl.pallas_call(
        body, name=name, grid=(rows // FLAT_ROW_BLOCK,),
        in_specs=[blk, blk], out_specs=blk, out_shape=jax.ShapeDtypeStruct(mine.shape, mine.dtype),
        compiler_params=pltpu.CompilerParams(dimension_semantics=("parallel",), vmem_limit_bytes=VMEM_LIMIT),
    )(mine, theirs)


def _sum_chips(received, own, *, name):
    _, rows, cols = received.shape

    def body(p_ref, own_ref, o_ref):
        me = 2 * lax.axis_index("x") + lax.axis_index("y")
        slot = [jnp.where(me == j, own_ref[j], p_ref[j]).astype(F32) for j in range(N_CHIPS)]
        o_ref[...] = ((slot[0] + slot[1]) + slot[2]) + slot[3]

    blk = pl.BlockSpec((N_CHIPS, FLAT_ROW_BLOCK, cols), lambda i: (0, i, 0))
    return pl.pallas_call(
        body, name=name, grid=(rows // FLAT_ROW_BLOCK,),
        in_specs=[blk, blk],
        out_specs=pl.BlockSpec((FLAT_ROW_BLOCK, cols), lambda i: (i, 0)),
        out_shape=jax.ShapeDtypeStruct((rows, cols), F32),
        compiler_params=pltpu.CompilerParams(dimension_semantics=("parallel",), vmem_limit_bytes=VMEM_LIMIT),
    )(received, own)


def _join_cores(half, *, name):
    rows2, cols = half.shape

    def body(h_ref, out_ref, send_sem, recv_sem):
        my_rows, sib_rows = _half_rows(2 * rows2)
        cp = pltpu.make_async_remote_copy(src_ref=h_ref, dst_ref=out_ref.at[my_rows, :], send_sem=send_sem,
                                          recv_sem=recv_sem, device_id=_sibling(), device_id_type=MESH)
        cp.start()
        cp.wait_send()
        pltpu.make_async_remote_copy(src_ref=h_ref, dst_ref=out_ref.at[sib_rows, :], send_sem=send_sem,
                                     recv_sem=recv_sem, device_id=_sibling(), device_id_type=MESH).wait_recv()

    out = pl.pallas_call(
        body, name=name,
        in_specs=[pl.BlockSpec(memory_space=pltpu.HBM)],
        out_specs=pl.BlockSpec(memory_space=pltpu.HBM),
        out_shape=jax.ShapeDtypeStruct((2 * rows2, cols), half.dtype),
        scratch_shapes=[pltpu.SemaphoreType.DMA, pltpu.SemaphoreType.DMA],
        compiler_params=pltpu.CompilerParams(has_side_effects=True),
    )(half)
    return lax.dynamic_update_slice_in_dim(out, half, lax.axis_index("c") * rows2, axis=0)


def _all_reduce_small(v, *, name):
    rows, cols = v.shape
    flips = [(fx, fy, fc) for fx in (0, 1) for fy in (0, 1) for fc in (0, 1)][1:]

    def body(v_ref, out_ref, gath_ref, send_sems, recv_sems):
        x, y, c = lax.axis_index("x"), lax.axis_index("y"), lax.axis_index("c")
        me = 4 * x + 2 * y + c
        gath_ref[me] = v_ref[...]
        peers = [((1 - x) if fx else x, (1 - y) if fy else y, (1 - c) if fc else c) for fx, fy, fc in flips]
        sends = []
        for k, peer in enumerate(peers):
            cp = pltpu.make_async_remote_copy(src_ref=v_ref, dst_ref=gath_ref.at[me], send_sem=send_sems.at[k],
                                              recv_sem=recv_sems.at[k], device_id=peer, device_id_type=MESH)
            cp.start()
            sends.append(cp)
        for k, (px, py, pc) in enumerate(peers):
            pltpu.make_async_remote_copy(src_ref=v_ref, dst_ref=gath_ref.at[4 * px + 2 * py + pc],
                                         send_sem=send_sems.at[k], recv_sem=recv_sems.at[k],
                                         device_id=(px, py, pc), device_id_type=MESH).wait_recv()
        for cp in sends:
            cp.wait_send()
        total = gath_ref[0]
        for k in range(1, 8):
            total = total + gath_ref[k]
        out_ref[...] = total

    total, _ = pl.pallas_call(
        body, name=name,
        in_specs=[pl.BlockSpec(memory_space=pltpu.VMEM)],
        out_specs=[pl.BlockSpec(memory_space=pltpu.VMEM), pl.BlockSpec(memory_space=pltpu.VMEM)],
        out_shape=[jax.ShapeDtypeStruct((rows, cols), v.dtype), jax.ShapeDtypeStruct((8, rows, cols), v.dtype)],
        scratch_shapes=[pltpu.SemaphoreType.DMA((7,)), pltpu.SemaphoreType.DMA((7,))],
        compiler_params=pltpu.CompilerParams(has_side_effects=True),
    )(v)
    return total


def _adamw(w, g, m, v, *, name):
    shape = w.shape
    cols = shape[-1]
    w2, g2, m2, v2 = (a.reshape(-1, cols) for a in (w, g, m, v))
    rows = w2.shape[0]
    br = _pick_rows(rows, FLAT_ROW_BLOCK)

    def body(w_ref, g_ref, m_ref, v_ref, d_out, m_out, v_out):
        gv = g_ref[...]
        m_new = ADAM_B1 * m_ref[...] + (1.0 - ADAM_B1) * gv
        v_new = ADAM_B2 * v_ref[...] + (1.0 - ADAM_B2) * jnp.square(gv)
        m_hat = m_new / (1.0 - ADAM_B1 ** ADAM_STEP)
        v_hat = v_new / (1.0 - ADAM_B2 ** ADAM_STEP)
        d_out[...] = -ADAM_LR * (m_hat / (jnp.sqrt(v_hat) + ADAM_EPS) + ADAM_WD * w_ref[...])
        m_out[...] = m_new
        v_out[...] = v_new

    blk = pl.BlockSpec((br, cols), lambda i: (i, 0))
    out = jax.ShapeDtypeStruct((rows, cols), F32)
    outs = pl.pallas_call(
        body, name=name, grid=(rows // br,),
        in_specs=[blk] * 4, out_specs=[blk] * 3, out_shape=[out] * 3,
        compiler_params=pltpu.CompilerParams(dimension_semantics=("parallel",), vmem_limit_bytes=VMEM_LIMIT),
    )(w2, g2, m2, v2)
    return [o.reshape(shape) for o in outs]


def _pick_rows(rows, target):
    if rows <= target:
        return rows
    return max(b for b in range(8, target + 1, 8) if rows % b == 0)


def _assemble(gathered_shards, name, layer=False):
    return jnp.concatenate(gathered_shards, axis=SHARD_AXIS[name] - int(layer))


def _chip_shard(full, name, j):
    if isinstance(full, list):
        axis = SHARD_AXIS[name] - 1
        layers = full
    else:
        axis = SHARD_AXIS[name]
        layers = [full]
    n = layers[0].shape[axis] // N_CHIPS
    return [lax.slice_in_dim(g, j * n, (j + 1) * n, axis=axis) for g in layers]


def kernel(x, positions, attn_norm, mlp_norm, sb_w_qkv, sb_w_o, kv_norm, mla_w_dkv, mla_kv_lat_norm, mla_w_ukv, mla_w_dq, mla_q_lat_norm, mla_w_uq, mla_w_o, mlp_w1, mlp_w2, final_norm, loss_target, m_attn_norm, m_mlp_norm, m_sb_w_qkv, m_sb_w_o, m_kv_norm, m_mla_w_dkv, m_mla_kv_lat_norm, m_mla_w_ukv, m_mla_w_dq, m_mla_q_lat_norm, m_mla_w_uq, m_mla_w_o, m_mlp_w1, m_mlp_w2, m_final_norm, v_attn_norm, v_mlp_norm, v_sb_w_qkv, v_sb_w_o, v_kv_norm, v_mla_w_dkv, v_mla_kv_lat_norm, v_mla_w_ukv, v_mla_w_dq, v_mla_q_lat_norm, v_mla_w_uq, v_mla_w_o, v_mlp_w1, v_mlp_w2, v_final_norm):
    weights = dict(attn_norm=attn_norm, mlp_norm=mlp_norm, sb_w_qkv=sb_w_qkv, sb_w_o=sb_w_o, kv_norm=kv_norm,
                   mla_w_dkv=mla_w_dkv, mla_kv_lat_norm=mla_kv_lat_norm, mla_w_ukv=mla_w_ukv, mla_w_dq=mla_w_dq,
                   mla_q_lat_norm=mla_q_lat_norm, mla_w_uq=mla_w_uq, mla_w_o=mla_w_o, mlp_w1=mlp_w1, mlp_w2=mlp_w2,
                   final_norm=final_norm)
    m_in = dict(attn_norm=m_attn_norm, mlp_norm=m_mlp_norm, sb_w_qkv=m_sb_w_qkv, sb_w_o=m_sb_w_o, kv_norm=m_kv_norm,
                mla_w_dkv=m_mla_w_dkv, mla_kv_lat_norm=m_mla_kv_lat_norm, mla_w_ukv=m_mla_w_ukv, mla_w_dq=m_mla_w_dq,
                mla_q_lat_norm=m_mla_q_lat_norm, mla_w_uq=m_mla_w_uq, mla_w_o=m_mla_w_o, mlp_w1=m_mlp_w1,
                mlp_w2=m_mlp_w2, final_norm=m_final_norm)
    v_in = dict(attn_norm=v_attn_norm, mlp_norm=v_mlp_norm, sb_w_qkv=v_sb_w_qkv, sb_w_o=v_sb_w_o, kv_norm=v_kv_norm,
                mla_w_dkv=v_mla_w_dkv, mla_kv_lat_norm=v_mla_kv_lat_norm, mla_w_ukv=v_mla_w_ukv, mla_w_dq=v_mla_w_dq,
                mla_q_lat_norm=v_mla_q_lat_norm, mla_w_uq=v_mla_w_uq, mla_w_o=v_mla_w_o, mlp_w1=v_mlp_w1,
                mlp_w2=v_mlp_w2, final_norm=v_final_norm)
    shard_shapes = [weights[n].shape for n in BIG_WEIGHTS]
    small_shapes = [weights[n].shape for n in SMALL_WEIGHTS]

    first_name = BIG_WEIGHTS[0]
    qkv_first, qkv_later = weights[first_name][0], weights[first_name][1:]
    gathered_first = _all_gather_chips(_pack([qkv_first], BF16), name="first_weight_all_gather")
    qkv_w0 = _assemble([_unpack(gathered_first[j], [qkv_first.shape])[0] for j in range(N_CHIPS)], first_name, layer=True)
    ride = _pack([qkv_later] + [weights[n] for n in BIG_WEIGHTS[1:]], BF16)
    ride_shapes = [qkv_later.shape] + shard_shapes[1:]

    def rest_weights(gathered):
        per_chip = [_unpack(gathered[j], ride_shapes) for j in range(N_CHIPS)]
        full = {n: _assemble([per_chip[j][i] for j in range(N_CHIPS)], n) for i, n in enumerate(BIG_WEIGHTS)}
        full[first_name] = jnp.concatenate([qkv_w0[None], full[first_name]], axis=0)
        return full

    norms = {n: weights[n] for n in SMALL_WEIGHTS}

    loss, dx, grads = _local_step(x[0], positions[0], loss_target, qkv_w0, norms, rest_weights, ride=ride)
    loss = lax.psum(loss, ("x", "y", "c"))

    parts = _pack_chips([[piece for n in BIG_WEIGHTS for piece in _chip_shard(grads[n], n, j)]
                         for j in range(N_CHIPS)], BF16)
    mine, theirs = _pair_exchange(parts, name="grads_pair_exchange")
    chip_part = _pair_sum(mine, theirs, name="grads_pair_sum")
    received = _exchange_chips(chip_part, name="grads_exchange")
    g_half = _sum_chips(received, chip_part, name="grads_sum_chips")
    g_sum = _join_cores(g_half, name="grads_join_cores")
    out_g = dict(zip(BIG_WEIGHTS, _unpack(g_sum, shard_shapes)))
    out_d, out_m, out_v = {}, {}, {}
    for n in BIG_WEIGHTS:
        out_d[n], out_m[n], out_v[n] = _adamw(weights[n], out_g[n], m_in[n], v_in[n], name=f"adamw_{n}")

    small_sum = _all_reduce_small(_pack_small([grads[n] for n in SMALL_WEIGHTS]), name="gains_all_reduce")
    sd, sm, sv = _adamw(_pack_small([weights[n] for n in SMALL_WEIGHTS]), small_sum,
                        _pack_small([m_in[n] for n in SMALL_WEIGHTS]),
                        _pack_small([v_in[n] for n in SMALL_WEIGHTS]), name="adamw_gains")
    out_g.update(zip(SMALL_WEIGHTS, _unpack_small(small_sum, small_shapes)))
    out_d.update(zip(SMALL_WEIGHTS, _unpack_small(sd, small_shapes)))
    out_m.update(zip(SMALL_WEIGHTS, _unpack_small(sm, small_shapes)))
    out_v.update(zip(SMALL_WEIGHTS, _unpack_small(sv, small_shapes)))

    return (loss, dx, *[out_g[n] for n in ALL_WEIGHTS], *[out_d[n] for n in ALL_WEIGHTS],
            *[out_m[n] for n in ALL_WEIGHTS], *[out_v[n] for n in ALL_WEIGHTS])
```

```python
import functools

import jax
import jax.numpy as jnp
from jax import lax
from jax.experimental import pallas as pl
from jax.experimental.pallas import tpu as pltpu

F32 = jnp.float32
BF16 = jnp.bfloat16

LANES = 128
SB_HEAD_DIM = 64
MLA_NOPE = 64
MLA_ROPE = 32
MLA_V = 64
MLA_Q_RANK = 384
MLA_KV_RANK = 256
CHUNK = 64
ROPE_THETA = 10000.0
NORM_EPS = 1e-6
SB_SCALE = SB_HEAD_DIM ** -0.5
MLA_SCALE = (MLA_NOPE + MLA_ROPE) ** -0.5
ROPE_LO = MLA_NOPE
ROPE_HALF = MLA_ROPE // 2
ATT_Q_BLOCK = 1024
ATT_K_BLOCK = 256
MLA_FWD_K_BLOCK = 512
NEG_BIG = -1e30
SB_DEAD_LOG = -110.0
VMEM_LIMIT = 56 * 1024 * 1024

ADAM_LR = 0.001
ADAM_B1 = 0.9
ADAM_B2 = 0.999
ADAM_EPS = 1e-08
ADAM_WD = 0.01
ADAM_STEP = 10

FLAT_COLS = 1024
FLAT_ROW_BLOCK = 256
N_CHIPS = 4
MESH = pl.DeviceIdType.MESH

BIG_WEIGHTS = ["sb_w_qkv", "sb_w_o", "mla_w_dkv", "mla_w_ukv", "mla_w_dq", "mla_w_uq", "mla_w_o", "mlp_w1", "mlp_w2"]
SHARD_AXIS = {"sb_w_qkv": 2, "sb_w_o": 1, "mla_w_dkv": 0, "mla_w_ukv": 1, "mla_w_dq": 1, "mla_w_uq": 2,
              "mla_w_o": 1, "mlp_w1": 2, "mlp_w2": 1}
SMALL_WEIGHTS = ["attn_norm", "mlp_norm", "kv_norm", "mla_kv_lat_norm", "mla_q_lat_norm", "final_norm"]
ALL_WEIGHTS = ["attn_norm", "mlp_norm", "sb_w_qkv", "sb_w_o", "kv_norm", "mla_w_dkv", "mla_kv_lat_norm", "mla_w_ukv",
               "mla_w_dq", "mla_q_lat_norm", "mla_w_uq", "mla_w_o", "mlp_w1", "mlp_w2", "final_norm"]


def _dot(a, b, dims):
    return lax.dot_general(a, b, (dims, ((), ())), preferred_element_type=F32)


def _dot_nn(a, b):
    return _dot(a, b, ((1,), (0,)))


def _dot_nt(a, b):
    return _dot(a, b, ((1,), (1,)))


def _dot_tn(a, b):
    return _dot(a, b, ((0,), (0,)))


def _pick_block(n, target):
    if n <= target:
        return n
    best = max(b for b in range(LANES, target + 1, LANES) if n % b == 0)
    return best


MM_ROWS = 512
MM_COLS = 1024
MM_DEPTH = 4096
MM_DEPTH_TN = 1024


def _mm(a, b, *, name, dims="nn", epilogue=None, extras=(), out_dtypes=(BF16,)):
    if dims == "nn":
        (m, k), (k2, n) = a.shape, b.shape
    elif dims == "nt":
        (m, k), (n, k2) = a.shape, b.shape
    else:
        (k, m), (k2, n) = a.shape, b.shape
    assert k == k2, (name, a.shape, b.shape)
    if dims == "tn":
        bm, bn, bk = _pick_block(m, MM_COLS), _pick_block(n, MM_COLS), _pick_block(k, MM_DEPTH_TN)
    else:
        rows = MM_ROWS if k > MM_DEPTH // 2 else 2 * MM_ROWS
        bm, bn, bk = _pick_block(m, rows), _pick_block(n, MM_COLS), _pick_block(k, MM_DEPTH)
    nk = k // bk
    if dims == "tn":
        a_spec = pl.BlockSpec((bk, bm), lambda j, i, kk: (kk, i))
    else:
        a_spec = pl.BlockSpec((bm, bk), lambda j, i, kk: (i, kk))
    if dims == "nt":
        b_spec = pl.BlockSpec((bn, bk), lambda j, i, kk: (j, kk))
    else:
        b_spec = pl.BlockSpec((bk, bn), lambda j, i, kk: (kk, j))
    extra_specs = []
    for arr, kind in extras:
        if kind == "tile":
            assert arr.shape == (m, n), (name, arr.shape)
            extra_specs.append(pl.BlockSpec((bm, bn), lambda j, i, kk: (i, j)))
        else:
            assert arr.shape == (m, LANES), (name, arr.shape)
            extra_specs.append(pl.BlockSpec((bm, LANES), lambda j, i, kk: (i, 0)))
    n_extra = len(extras)
    n_out = len(out_dtypes)
    dot = {"nn": _dot_nn, "nt": _dot_nt, "tn": _dot_tn}[dims]

    def body(*refs):
        a_ref, b_ref = refs[0], refs[1]
        extra_refs = refs[2:2 + n_extra]
        out_refs = refs[2 + n_extra:2 + n_extra + n_out]

        def finish(acc):
            outs = (acc,) if epilogue is None else epilogue(acc, *[r[...] for r in extra_refs])
            for o_ref, o in zip(out_refs, outs):
                o_ref[...] = o.astype(o_ref.dtype)

        part = dot(a_ref[...].astype(BF16), b_ref[...].astype(BF16))
        if nk == 1:
            finish(part)
            return
        acc_ref = refs[-1]
        kk = pl.program_id(2)

        @pl.when(kk == 0)
        def _():
            acc_ref[...] = part

        @pl.when(kk > 0)
        def _():
            acc_ref[...] += part

        @pl.when(kk == nk - 1)
        def _():
            finish(acc_ref[...])

    outs = pl.pallas_call(
        body, name=name, grid=(n // bn, m // bm, nk),
        in_specs=[a_spec, b_spec] + extra_specs,
        out_specs=[pl.BlockSpec((bm, bn), lambda j, i, kk: (i, j)) for _ in range(n_out)],
        out_shape=[jax.ShapeDtypeStruct((m, n), dt) for dt in out_dtypes],
        scratch_shapes=[pltpu.VMEM((bm, bn), F32)] if nk > 1 else [],
        compiler_params=pltpu.CompilerParams(dimension_semantics=("parallel", "parallel", "arbitrary"),
                                             vmem_limit_bytes=VMEM_LIMIT),
    )(a, b, *[arr for arr, _ in extras])
    return outs[0] if n_out == 1 else outs


def _epi_add(acc, res):
    return (res + acc,)


def _epi_relu2(acc):
    r = jnp.maximum(acc, 0.0)
    return acc, r * r


def _epi_relu2_grad(acc, u):
    return (acc * (2.0 * jnp.maximum(u.astype(F32), 0.0)),)


def _rope_slab(t, cos_t, sin_t):
    lane = lax.broadcasted_iota(jnp.int32, t.shape, 1)
    partner = jnp.where(lane < ROPE_LO + ROPE_HALF, pltpu.roll(t, LANES - ROPE_HALF, 1), pltpu.roll(t, ROPE_HALF, 1))
    return t * cos_t + partner * sin_t


def _rope_slab_bwd(d, cos_t, sin_t):
    ds = d * sin_t
    lane = lax.broadcasted_iota(jnp.int32, d.shape, 1)
    partner = jnp.where(lane < ROPE_LO + ROPE_HALF, pltpu.roll(ds, LANES - ROPE_HALF, 1), pltpu.roll(ds, ROPE_HALF, 1))
    in_rope = (lane >= ROPE_LO) & (lane < ROPE_LO + MLA_ROPE)
    return d * cos_t + jnp.where(in_rope, partner, 0.0)


def _epi_rope_heads(acc, cos_t, sin_t):
    slabs = [_rope_slab(acc[:, j * LANES:(j + 1) * LANES], cos_t, sin_t) for j in range(acc.shape[1] // LANES)]
    return (jnp.concatenate(slabs, axis=1) * MLA_SCALE,)


def _row_block(s):
    return min(512, s)


def _rms_fwd(x, g, *, name):
    s, d = x.shape
    bm = _row_block(s)

    def body(x_ref, g_ref, o_ref):
        xv = x_ref[...]
        r = lax.rsqrt(jnp.mean(xv * xv, axis=-1, keepdims=True) + NORM_EPS)
        o_ref[...] = ((xv * r) * g_ref[...]).astype(o_ref.dtype)

    return pl.pallas_call(
        body, name=name, grid=(s // bm,),
        in_specs=[pl.BlockSpec((bm, d), lambda i: (i, 0)), pl.BlockSpec((1, d), lambda i: (0, 0))],
        out_specs=pl.BlockSpec((bm, d), lambda i: (i, 0)),
        out_shape=jax.ShapeDtypeStruct((s, d), BF16),
        compiler_params=pltpu.CompilerParams(dimension_semantics=("parallel",), vmem_limit_bytes=VMEM_LIMIT),
    )(x, g.reshape(1, d))


def _rms_bwd_math(xv, gv, dy):
    r = lax.rsqrt(jnp.mean(xv * xv, axis=-1, keepdims=True) + NORM_EPS)
    xhat = xv * r
    dyg = dy * gv
    mdot = jnp.mean(dyg * xhat, axis=-1, keepdims=True)
    dx = r * (dyg - xhat * mdot)
    dg = jnp.sum(dy * xhat, axis=0, keepdims=True)
    return dx, dg


def _rms_bwd(x, g, dy, dres, *, name, lead_axis=False):
    s, d = x.shape
    bm = _row_block(s)
    has_res = dres is not None

    def body(*refs):
        x_ref, g_ref, dy_ref = refs[:3]
        dres_ref = refs[3] if has_res else None
        dx_ref, dxb_ref, dg_ref = refs[-3:]
        dx, dg = _rms_bwd_math(x_ref[...], g_ref[...], dy_ref[...].astype(F32))
        if has_res:
            dx = dx + dres_ref[...]
        dx_ref[...] = dx
        dxb_ref[...] = dx.astype(BF16)

        @pl.when(pl.program_id(0) == 0)
        def _():
            dg_ref[...] = jnp.zeros_like(dg_ref)

        dg_ref[...] += dg

    row = pl.BlockSpec((bm, d), lambda i: (i, 0))
    vec = pl.BlockSpec((1, d), lambda i: (0, 0))
    ins = [x, g.reshape(1, d), dy] + ([dres] if has_res else [])
    dx_spec, dx_shape = row, (s, d)
    if lead_axis:
        dx_spec, dx_shape = pl.BlockSpec((None, bm, d), lambda i: (0, i, 0)), (1, s, d)
    return pl.pallas_call(
        body, name=name, grid=(s // bm,),
        in_specs=[row, vec, row] + ([row] if has_res else []),
        out_specs=[dx_spec, row, vec],
        out_shape=[jax.ShapeDtypeStruct(dx_shape, F32), jax.ShapeDtypeStruct((s, d), BF16),
                   jax.ShapeDtypeStruct((1, d), F32)],
        compiler_params=pltpu.CompilerParams(dimension_semantics=("arbitrary",), vmem_limit_bytes=VMEM_LIMIT),
    )(*ins)


def _loss_bwd(x, g, target, *, name):
    s, d = x.shape
    bm = _row_block(s)

    def body(x_ref, g_ref, t_ref, loss_ref, dx_ref, dxb_ref, dg_ref):
        xv, gv = x_ref[...], g_ref[...]
        r = lax.rsqrt(jnp.mean(xv * xv, axis=-1, keepdims=True) + NORM_EPS)
        err = (xv * r) * gv - t_ref[...]
        dx, dg = _rms_bwd_math(xv, gv, err * (1.0 / d))
        dx_ref[...] = dx
        dxb_ref[...] = dx.astype(BF16)

        @pl.when(pl.program_id(0) == 0)
        def _():
            dg_ref[...] = jnp.zeros_like(dg_ref)
            loss_ref[...] = jnp.zeros_like(loss_ref)

        dg_ref[...] += dg
        loss_ref[...] += jnp.sum(jnp.mean(err * err, axis=-1, keepdims=True), axis=0, keepdims=True) * 0.5

    row = pl.BlockSpec((bm, d), lambda i: (i, 0))
    vec = pl.BlockSpec((1, d), lambda i: (0, 0))
    assert target.shape == (1, s, d), target.shape
    return pl.pallas_call(
        body, name=name, grid=(s // bm,),
        in_specs=[row, vec, pl.BlockSpec((None, bm, d), lambda i: (0, i, 0))],
        out_specs=[pl.BlockSpec((8, LANES), lambda i: (0, 0)), row, row, vec],
        out_shape=[jax.ShapeDtypeStruct((8, LANES), F32), jax.ShapeDtypeStruct((s, d), F32),
                   jax.ShapeDtypeStruct((s, d), BF16), jax.ShapeDtypeStruct((1, d), F32)],
        compiler_params=pltpu.CompilerParams(dimension_semantics=("arbitrary",), vmem_limit_bytes=VMEM_LIMIT),
    )(x, g.reshape(1, d), target)


def _kv_prep(down, g, cos_t, sin_t, *, name):
    s, w = down.shape
    bm = _row_block(s)

    def body(d_ref, g_ref, c_ref, s_ref, o_ref):
        lat = d_ref[:, :MLA_KV_RANK]
        r = lax.rsqrt(jnp.mean(lat * lat, axis=-1, keepdims=True) + NORM_EPS)
        o_ref[:, :MLA_KV_RANK] = ((lat * r) * g_ref[...]).astype(BF16)
        o_ref[:, MLA_KV_RANK:] = _rope_slab(d_ref[:, MLA_KV_RANK:], c_ref[...], s_ref[...]).astype(BF16)

    row = pl.BlockSpec((bm, w), lambda i: (i, 0))
    tab = pl.BlockSpec((bm, LANES), lambda i: (i, 0))
    return pl.pallas_call(
        body, name=name, grid=(s // bm,),
        in_specs=[row, pl.BlockSpec((1, MLA_KV_RANK), lambda i: (0, 0)), tab, tab],
        out_specs=row, out_shape=jax.ShapeDtypeStruct((s, w), BF16),
        compiler_params=pltpu.CompilerParams(dimension_semantics=("parallel",), vmem_limit_bytes=VMEM_LIMIT),
    )(down, g.reshape(1, MLA_KV_RANK), cos_t, sin_t)


def _kv_prep_bwd(down, g, cos_t, sin_t, dcat, *, name):
    s, w = down.shape
    bm = _row_block(s)

    def body(d_ref, g_ref, c_ref, s_ref, dc_ref, o_ref, dg_ref):
        dlat, dg = _rms_bwd_math(d_ref[:, :MLA_KV_RANK], g_ref[...], dc_ref[:, :MLA_KV_RANK])
        o_ref[:, :MLA_KV_RANK] = dlat.astype(BF16)
        o_ref[:, MLA_KV_RANK:] = _rope_slab_bwd(dc_ref[:, MLA_KV_RANK:], c_ref[...], s_ref[...]).astype(BF16)

        @pl.when(pl.program_id(0) == 0)
        def _():
            dg_ref[...] = jnp.zeros_like(dg_ref)

        dg_ref[...] += dg

    row = pl.BlockSpec((bm, w), lambda i: (i, 0))
    tab = pl.BlockSpec((bm, LANES), lambda i: (i, 0))
    vec = pl.BlockSpec((1, MLA_KV_RANK), lambda i: (0, 0))
    return pl.pallas_call(
        body, name=name, grid=(s // bm,),
        in_specs=[row, vec, tab, tab, row],
        out_specs=[row, vec],
        out_shape=[jax.ShapeDtypeStruct((s, w), BF16), jax.ShapeDtypeStruct((1, MLA_KV_RANK), F32)],
        compiler_params=pltpu.CompilerParams(dimension_semantics=("arbitrary",), vmem_limit_bytes=VMEM_LIMIT),
    )(down, g.reshape(1, MLA_KV_RANK), cos_t, sin_t, dcat)


def _split_bf16(v):
    hi = v.astype(BF16)
    lo = (v - hi.astype(F32)).astype(BF16)
    return hi, lo


def _suffix_matrices(n):
    row = lax.broadcasted_iota(jnp.int32, (n, n), 0)
    col = lax.broadcasted_iota(jnp.int32, (n, n), 1)
    incl = (row >= col).astype(BF16)
    return (row > col).astype(BF16), jnp.concatenate([incl, incl], axis=0)


def _suffix_sum(v, matrix):
    hi, lo = _split_bf16(v)
    return _dot_nn(jnp.concatenate([hi, lo], axis=1), matrix)


def _block_positions(qi, kb, bq, bk, r0, r1):
    row = qi * bq + r0 + lax.broadcasted_iota(jnp.int32, (r1 - r0, bk), 0)
    col = kb * bk + lax.broadcasted_iota(jnp.int32, (r1 - r0, bk), 1)
    return row, col


def _att_blocks(s, key_block=ATT_K_BLOCK):
    bq, bk = min(ATT_Q_BLOCK, s), min(key_block, s)
    return bq, bk, s // bq, bq // bk


def _sweep(qi, ratio, bk, step, unroll=2, alive=None):
    bq = ratio * bk
    for d in range(ratio):
        kb, r0 = (qi + 1) * ratio - 1 - d, (ratio - 1 - d) * bk
        near = bq if alive is None else min(r0 + 2 * bk, bq)
        step(kb, True, r0, near)
        if near < bq:
            pl.when(alive(near))(functools.partial(step, kb, False, near, bq))
    unroll = unroll if ratio % unroll == 0 else 1
    trips = qi * (ratio // unroll)

    def trip(i):
        for u in range(unroll):
            step(qi * ratio - 1 - (i * unroll + u), False, 0, bq)

    if alive is None:
        lax.fori_loop(0, trips, lambda i, carry: (trip(i), carry)[1], 0)
    else:
        lax.while_loop(lambda i: jnp.logical_and(i < trips, alive(0)), lambda i: (trip(i), i + 1)[1], 0)


def _stick_left(c_ref, r0):
    return jnp.max(c_ref[r0:, :]) > SB_DEAD_LOG


def _sb_logs(q, k):
    z = _dot_nt(q, k)
    lb = jnp.minimum(z, 0.0) - jnp.log(1.0 + jnp.exp(-jnp.abs(z)))
    return lb, lb - z


def _sb_fwd(qkv, heads, *, name, ride=None):
    s = qkv.shape[0]
    bq, bk, nq, ratio = _att_blocks(s)
    riding = ride is not None

    def body(*refs):
        if riding:
            q_ref, k_ref, v_ref, w_ref, o_ref, gath_ref, acc_ref, c_ref = refs[:8]
            first = jnp.logical_and(pl.program_id(0) == 0, pl.program_id(1) == 0)
            last = jnp.logical_and(pl.program_id(0) == heads - 1, pl.program_id(1) == nq - 1)
            pl.when(first)(functools.partial(_all_gather_start, w_ref, gath_ref, refs[8:]))
        else:
            q_ref, k_ref, v_ref, o_ref, acc_ref, c_ref = refs
        qi = pl.program_id(1)
        q = q_ref[...] * SB_SCALE
        m_strict, _ = _suffix_matrices(bk)
        acc_ref[...] = jnp.zeros_like(acc_ref)
        c_ref[...] = jnp.zeros_like(c_ref)

        def step(kb, masked, r0, r1):
            rows = pl.ds(pl.multiple_of(kb * bk, bk), bk)
            mine = pl.ds(r0, r1 - r0)
            k, v = k_ref[rows, :], v_ref[rows, :]
            lb, lk = _sb_logs(q[r0:r1], k)
            if masked:
                row, col = _block_positions(qi, kb, bq, bk, r0, r1)
                causal = col < row
                lk = jnp.where(causal, lk, 0.0)
            c = c_ref[mine, :]
            w = jnp.exp(lb + _dot_nn(lk.astype(BF16), m_strict) + jnp.tile(c, (1, bk // LANES)))
            if masked:
                w = jnp.where(causal, w, 0.0)
            acc_ref[mine, :] += _dot_nn(w.astype(BF16), v)
            c_ref[mine, :] = c + jnp.sum(lk, axis=-1, keepdims=True)

        _sweep(qi, ratio, bk, step, unroll=1, alive=functools.partial(_stick_left, c_ref))
        o_ref[...] = acc_ref[...].astype(o_ref.dtype)
        if riding:
            pl.when(last)(functools.partial(_all_gather_finish, w_ref, gath_ref, refs[8:]))

    hbm = pl.BlockSpec(memory_space=pltpu.HBM)
    o_spec = pl.BlockSpec((bq, LANES), lambda h, i: (i, h))
    o_shape = jax.ShapeDtypeStruct((s, heads * LANES), F32)
    return pl.pallas_call(
        body, name=name, grid=(heads, nq),
        in_specs=[pl.BlockSpec((bq, LANES), lambda h, i: (i, h)),
                  pl.BlockSpec((s, LANES), lambda h, i: (0, heads + h)),
                  pl.BlockSpec((s, LANES), lambda h, i: (0, 2 * heads + h))] + ([hbm] if riding else []),
        out_specs=[o_spec, hbm] if riding else o_spec,
        out_shape=[o_shape, jax.ShapeDtypeStruct((N_CHIPS,) + ride.shape, ride.dtype)] if riding else o_shape,
        scratch_shapes=[pltpu.VMEM((bq, LANES), F32), pltpu.VMEM((bq, LANES), F32)]
        + (_all_gather_sems() if riding else []),
        compiler_params=pltpu.CompilerParams(dimension_semantics=("arbitrary", "arbitrary"),
                                             vmem_limit_bytes=VMEM_LIMIT, has_side_effects=riding),
    )(*([qkv, qkv, qkv] + ([ride] if riding else [])))


def _sb_bwd(qkv, o, do, heads, *, name, ride=None):
    s = qkv.shape[0]
    bq, bk, nq, ratio = _att_blocks(s)
    riding = ride is not None

    def body(*refs):
        if riding:
            (q_ref, k_ref, v_ref, o_ref, do_ref, g_ref, dq_ref, dk_ref, dv_ref, got_ref,
             dq_acc, dk_acc, dv_acc, c_ref, e_ref) = refs[:15]
            first = jnp.logical_and(pl.program_id(0) == 0, pl.program_id(1) == 0)
            last = jnp.logical_and(pl.program_id(0) == heads - 1, pl.program_id(1) == nq - 1)
            pl.when(first)(functools.partial(_exchange_start, g_ref, got_ref, refs[15:]))
        else:
            q_ref, k_ref, v_ref, o_ref, do_ref, dq_ref, dk_ref, dv_ref, dq_acc, dk_acc, dv_acc, c_ref, e_ref = refs
        qi = pl.program_id(1)

        @pl.when(qi == 0)
        def _():
            dk_acc[...] = jnp.zeros_like(dk_acc)
            dv_acc[...] = jnp.zeros_like(dv_acc)

        q = q_ref[...] * SB_SCALE
        do = do_ref[...]
        q_t, do_t = q.T, do.T
        total = jnp.sum(do.astype(F32) * o_ref[...].astype(F32), axis=-1, keepdims=True)
        m_strict, m_incl = _suffix_matrices(bk)
        dq_acc[...] = jnp.zeros_like(dq_acc)
        c_ref[...] = jnp.zeros_like(c_ref)
        e_ref[...] = jnp.broadcast_to(total, e_ref.shape)
        reps = (1, bk // LANES)

        def step(kb, masked, r0, r1):
            rows = pl.ds(pl.multiple_of(kb * bk, bk), bk)
            mine = pl.ds(r0, r1 - r0)
            k, v = k_ref[rows, :], v_ref[rows, :]
            qs, dos = q[r0:r1], do[r0:r1]
            lb, lk_all = _sb_logs(qs, k)
            lk = lk_all
            if masked:
                row, col = _block_positions(qi, kb, bq, bk, r0, r1)
                causal = col < row
                lk = jnp.where(causal, lk_all, 0.0)
            c = c_ref[mine, :]
            w = jnp.exp(lb + _dot_nn(lk.astype(BF16), m_strict) + jnp.tile(c, reps))
            if masked:
                w = jnp.where(causal, w, 0.0)
            wb = w.astype(BF16)
            g = wb.astype(F32) * _dot_nt(dos, v)
            e = e_ref[mine, :]
            g_left = jnp.tile(e, reps) - _suffix_sum(g, m_incl)
            da = g * jnp.exp(lk_all) - jnp.exp(lb) * g_left
            if masked:
                da = jnp.where(causal, da, 0.0)
            dab = da.astype(BF16)
            dq_acc[mine, :] += _dot_nn(dab, k)
            dk_acc[:, rows] += _dot_nn(q_t[:, r0:r1], dab)
            dv_acc[:, rows] += _dot_nn(do_t[:, r0:r1], wb)
            e_ref[mine, :] = e - jnp.sum(g, axis=-1, keepdims=True)
            c_ref[mine, :] = c + jnp.sum(lk, axis=-1, keepdims=True)

        _sweep(qi, ratio, bk, step, unroll=1, alive=functools.partial(_stick_left, c_ref))
        dq_ref[...] = (dq_acc[...] * SB_SCALE).astype(dq_ref.dtype)

        @pl.when(qi == nq - 1)
        def _():
            dk_ref[...] = dk_acc[...].T.astype(dk_ref.dtype)
            dv_ref[...] = dv_acc[...].T.astype(dv_ref.dtype)

        if riding:
            pl.when(last)(functools.partial(_exchange_finish, g_ref, got_ref, refs[15:]))

    blk = pl.BlockSpec((bq, LANES), lambda h, i: (i, h))
    full = pl.BlockSpec((s, LANES), lambda h, i: (0, h))
    hbm = pl.BlockSpec(memory_space=pltpu.HBM)
    shape = jax.ShapeDtypeStruct((s, heads * LANES), BF16)
    return pl.pallas_call(
        body, name=name, grid=(heads, nq),
        in_specs=[blk,
                  pl.BlockSpec((s, LANES), lambda h, i: (0, heads + h)),
                  pl.BlockSpec((s, LANES), lambda h, i: (0, 2 * heads + h)),
                  blk, blk] + ([hbm] if riding else []),
        out_specs=[blk, full, full] + ([hbm] if riding else []),
        out_shape=[shape, shape, shape] + ([jax.ShapeDtypeStruct(ride.shape, ride.dtype)] if riding else []),
        scratch_shapes=[pltpu.VMEM((bq, LANES), F32), pltpu.VMEM((LANES, s), F32), pltpu.VMEM((LANES, s), F32),
                        pltpu.VMEM((bq, LANES), F32), pltpu.VMEM((bq, LANES), F32)]
        + (_exchange_sems() if riding else []),
        compiler_params=pltpu.CompilerParams(dimension_semantics=("arbitrary", "arbitrary"),
                                             vmem_limit_bytes=VMEM_LIMIT, has_side_effects=riding),
    )(*([qkv, qkv, qkv, o, do] + ([ride] if riding else [])))


def _chunk_allowed(qi, kb, bq, bk, r0, r1):
    row, col = _block_positions(qi, kb, bq, bk, r0, r1)
    return (col // CHUNK) <= (row // CHUNK)


def _mla_fwd(q, kv, heads, *, name):
    s = q.shape[0]
    bq, bk, nq, ratio = _att_blocks(s, MLA_FWD_K_BLOCK)
    reps = (1, bk // LANES)

    def body(q_ref, k_ref, v_ref, o_ref, lse_ref, acc_ref, m_ref, l_ref):
        qi = pl.program_id(1)
        qv = q_ref[...]
        acc_ref[...] = jnp.zeros_like(acc_ref)
        m_ref[...] = jnp.full_like(m_ref, NEG_BIG)
        l_ref[...] = jnp.zeros_like(l_ref)

        def step(kb, masked, r0, r1):
            rows = pl.ds(pl.multiple_of(kb * bk, bk), bk)
            mine = pl.ds(r0, r1 - r0)
            k, v = k_ref[rows, :], v_ref[rows, :]
            sc = _dot_nt(qv[r0:r1], k)
            if masked:
                allowed = _chunk_allowed(qi, kb, bq, bk, r0, r1)
                sc = jnp.where(allowed, sc, NEG_BIG)
            m_old = m_ref[mine, :]
            m_new = jnp.maximum(m_old, jnp.max(sc, axis=-1, keepdims=True))
            p = jnp.exp(sc - jnp.tile(m_new, reps))
            alpha = jnp.exp(m_old - m_new)
            l_ref[mine, :] = alpha * l_ref[mine, :] + jnp.sum(p, axis=-1, keepdims=True)
            acc_ref[mine, :] = alpha * acc_ref[mine, :] + _dot_nn(p.astype(BF16), v)
            m_ref[mine, :] = m_new

        _sweep(qi, ratio, bk, step)
        o_ref[...] = (acc_ref[...] / l_ref[...]).astype(o_ref.dtype)
        lse_ref[...] = m_ref[...] + jnp.log(l_ref[...])

    blk = pl.BlockSpec((bq, LANES), lambda h, i: (i, h))
    return pl.pallas_call(
        body, name=name, grid=(heads, nq),
        in_specs=[blk,
                  pl.BlockSpec((s, LANES), lambda h, i: (0, h)),
                  pl.BlockSpec((s, LANES), lambda h, i: (0, heads + h))],
        out_specs=[blk, blk],
        out_shape=[jax.ShapeDtypeStruct((s, heads * LANES), BF16), jax.ShapeDtypeStruct((s, heads * LANES), F32)],
        scratch_shapes=[pltpu.VMEM((bq, LANES), F32), pltpu.VMEM((bq, LANES), F32), pltpu.VMEM((bq, LANES), F32)],
        compiler_params=pltpu.CompilerParams(dimension_semantics=("parallel", "arbitrary"),
                                             vmem_limit_bytes=VMEM_LIMIT),
    )(q, kv, kv)


def _mla_bwd(q, kv, o, do, lse, cos_t, sin_t, dkv_init, heads, *, name):
    s = q.shape[0]
    bq, bk, nq, ratio = _att_blocks(s)
    reps = (1, bk // LANES)
    has_init = dkv_init is not None

    def body(*refs):
        q_ref, k_ref, v_ref, o_ref, do_ref, lse_ref, c_ref, s_ref = refs[:8]
        ki_ref, vi_ref = (refs[8], refs[9]) if has_init else (None, None)
        dq_ref, dk_ref, dv_ref, dq_acc, dk_acc, dv_acc = refs[-6:]
        qi = pl.program_id(1)

        @pl.when(qi == 0)
        def _():
            if has_init:
                dk_acc[...] = ki_ref[...].astype(F32).T
                dv_acc[...] = vi_ref[...].astype(F32).T
            else:
                dk_acc[...] = jnp.zeros_like(dk_acc)
                dv_acc[...] = jnp.zeros_like(dv_acc)

        qv = q_ref[...]
        do = do_ref[...]
        q_t, do_t = qv.T, do.T
        delta = jnp.sum(do.astype(F32) * o_ref[...].astype(F32), axis=-1, keepdims=True)
        lse_wide = jnp.tile(lse_ref[...], reps)
        dq_acc[...] = jnp.zeros_like(dq_acc)

        def step(kb, masked, r0, r1):
            rows = pl.ds(pl.multiple_of(kb * bk, bk), bk)
            k, v = k_ref[rows, :], v_ref[rows, :]
            qs, dos = qv[r0:r1], do[r0:r1]
            p = jnp.exp(_dot_nt(qs, k) - lse_wide[r0:r1])
            if masked:
                p = jnp.where(_chunk_allowed(qi, kb, bq, bk, r0, r1), p, 0.0)
            ds = (p * (_dot_nt(dos, v) - delta[r0:r1])).astype(BF16)
            dq_acc[pl.ds(r0, r1 - r0), :] += _dot_nn(ds, k)
            dk_acc[:, rows] += _dot_nn(q_t[:, r0:r1], ds)
            dv_acc[:, rows] += _dot_nn(do_t[:, r0:r1], p.astype(BF16))

        _sweep(qi, ratio, bk, step)
        dq_ref[...] = _rope_slab_bwd(dq_acc[...] * MLA_SCALE, c_ref[...], s_ref[...]).astype(dq_ref.dtype)

        @pl.when(qi == nq - 1)
        def _():
            dk_ref[...] = dk_acc[...].T.astype(dk_ref.dtype)
            dv_ref[...] = dv_acc[...].T.astype(dv_ref.dtype)

    blk = pl.BlockSpec((bq, LANES), lambda h, i: (i, h))
    tab = pl.BlockSpec((bq, LANES), lambda h, i: (i, 0))
    k_full = pl.BlockSpec((s, LANES), lambda h, i: (0, h))
    v_full = pl.BlockSpec((s, LANES), lambda h, i: (0, heads + h))
    shape = jax.ShapeDtypeStruct((s, heads * LANES), BF16)
    ins = [q, kv, kv, o, do, lse, cos_t, sin_t] + ([dkv_init, dkv_init] if has_init else [])
    dq, dk, dv = pl.pallas_call(
        body, name=name, grid=(heads, nq),
        in_specs=[blk, k_full, v_full, blk, blk, blk, tab, tab] + ([k_full, v_full] if has_init else []),
        out_specs=[blk, k_full, k_full],
        out_shape=[shape, shape, shape],
        scratch_shapes=[pltpu.VMEM((bq, LANES), F32), pltpu.VMEM((LANES, s), F32), pltpu.VMEM((LANES, s), F32)],
        compiler_params=pltpu.CompilerParams(dimension_semantics=("arbitrary", "arbitrary"),
                                             vmem_limit_bytes=VMEM_LIMIT),
    )(*ins)
    return dq, jnp.concatenate([dk, dv], axis=1)


def _pad_last(a, width):
    return jnp.pad(a, [(0, 0)] * (a.ndim - 1) + [(0, width - a.shape[-1])])


def _pad_qkv(w, heads):
    d = w.shape[0]
    return _pad_last(w.reshape(d, 3 * heads, SB_HEAD_DIM), LANES).reshape(d, 3 * heads * LANES)


def _unpad_qkv(g, heads):
    d = g.shape[0]
    return g.reshape(d, 3 * heads, LANES)[:, :, :SB_HEAD_DIM].reshape(d, 3 * heads * SB_HEAD_DIM)


def _pad_o(w, heads):
    d = w.shape[1]
    w = w.reshape(heads, SB_HEAD_DIM, d)
    return jnp.pad(w, [(0, 0), (0, LANES - SB_HEAD_DIM), (0, 0)]).reshape(heads * LANES, d)


def _unpad_o(g, heads):
    d = g.shape[1]
    return g.reshape(heads, LANES, d)[:, :SB_HEAD_DIM, :].reshape(heads * SB_HEAD_DIM, d)


def _pad_uq(w, heads):
    r = w.shape[0]
    return _pad_last(w.reshape(r, heads, MLA_NOPE + MLA_ROPE), LANES).reshape(r, heads * LANES)


def _unpad_uq(g, heads):
    r = g.shape[0]
    return g.reshape(r, heads, LANES)[:, :, :MLA_NOPE + MLA_ROPE].reshape(r, heads * (MLA_NOPE + MLA_ROPE))


def _pad_dkv(w):
    d = w.shape[0]
    rope = jnp.zeros((d, LANES), w.dtype).at[:, ROPE_LO:ROPE_LO + MLA_ROPE].set(w[:, MLA_KV_RANK:])
    return jnp.concatenate([w[:, :MLA_KV_RANK], rope], axis=1)


def _unpad_dkv(g):
    return jnp.concatenate([g[:, :MLA_KV_RANK], g[:, MLA_KV_RANK + ROPE_LO:MLA_KV_RANK + ROPE_LO + MLA_ROPE]], axis=1)


def _pad_ukv(w, heads):
    w = w.reshape(MLA_KV_RANK, heads, 2, MLA_NOPE)
    k_part = _pad_last(w[:, :, 0, :], LANES).reshape(MLA_KV_RANK, heads * LANES)
    v_part = _pad_last(w[:, :, 1, :], LANES).reshape(MLA_KV_RANK, heads * LANES)
    lane = jnp.arange(LANES)
    place = ((lane[:, None] == lane[None, :]) & (lane[:, None] >= ROPE_LO) & (lane[:, None] < ROPE_LO + MLA_ROPE))
    place = jnp.tile(place.astype(w.dtype), (1, heads))
    top = jnp.concatenate([k_part, v_part], axis=1)
    bottom = jnp.concatenate([place, jnp.zeros_like(place)], axis=1)
    return jnp.concatenate([top, bottom], axis=0)


def _unpad_ukv(g, heads):
    g = g[:MLA_KV_RANK]
    k_part = g[:, :heads * LANES].reshape(MLA_KV_RANK, heads, LANES)[:, :, :MLA_NOPE]
    v_part = g[:, heads * LANES:].reshape(MLA_KV_RANK, heads, LANES)[:, :, :MLA_V]
    return jnp.stack([k_part, v_part], axis=2).reshape(MLA_KV_RANK, heads * (MLA_NOPE + MLA_V))


def _rope_tables(positions):
    inv_freq = ROPE_THETA ** (-jnp.arange(0, MLA_ROPE, 2, dtype=F32) / MLA_ROPE)
    ang = positions.astype(F32)[:, None] * inv_freq
    cos, sin = jnp.cos(ang), jnp.sin(ang)
    s = positions.shape[0]
    cos_t = jnp.ones((s, LANES), F32).at[:, ROPE_LO:ROPE_LO + MLA_ROPE].set(jnp.concatenate([cos, cos], axis=1))
    sin_t = jnp.zeros((s, LANES), F32).at[:, ROPE_LO:ROPE_LO + MLA_ROPE].set(jnp.concatenate([-sin, sin], axis=1))
    return cos_t, sin_t


def _local_step(x, positions, target, qkv_w0, norms, rest_weights, ride=None, early_reduce=None):
    s, d = x.shape
    heads = d // SB_HEAD_DIM
    cos_t, sin_t = _rope_tables(positions)

    h_first = _rms_fwd(x, norms["attn_norm"][0], name="l0_attn_norm")
    qkv_first = _mm(h_first, _pad_qkv(qkv_w0, heads), name="l0_qkv")
    if ride is None:
        o_first, gathered = _sb_fwd(qkv_first, heads, name="l0_sb_fwd"), None
    else:
        o_first, gathered = _sb_fwd(qkv_first, heads, name="l0_sb_fwd", ride=ride)
    w = rest_weights(gathered)
    n_a = w["sb_w_qkv"].shape[0]
    n_b = w["mla_w_dq"].shape[0]
    depth = n_a + n_b

    wqkv = [_pad_qkv(w["sb_w_qkv"][l], heads) for l in range(n_a)]
    wo_a = [_pad_o(w["sb_w_o"][l], heads) for l in range(n_a)]
    wdkv = _pad_dkv(w["mla_w_dkv"])
    wkv = _pad_ukv(w["mla_w_ukv"], heads)
    wdq = [w["mla_w_dq"][j] for j in range(n_b)]
    wuq = [_pad_uq(w["mla_w_uq"][j], heads) for j in range(n_b)]
    wo_b = [_pad_o(w["mla_w_o"][j], heads) for j in range(n_b)]
    w1 = [w["mlp_w1"][l] for l in range(depth)]
    w2 = [w["mlp_w2"][l] for l in range(depth)]

    saved = []
    kv_saved = None
    kv = None
    for l in range(depth):
        t = f"l{l}"
        sv = {"x_in": x}
        h = h_first if l == 0 else _rms_fwd(x, norms["attn_norm"][l], name=f"{t}_attn_norm")
        sv["h"] = h
        if l < n_a:
            if l == 0:
                qkv, o = qkv_first, o_first
            else:
                qkv = _mm(h, wqkv[l], name=f"{t}_qkv")
                o = _sb_fwd(qkv, heads, name=f"{t}_sb_fwd")
            sv["qkv"], sv["o"] = qkv, o
            x = _mm(o, wo_a[l], name=f"{t}_attn_out", epilogue=_epi_add, extras=[(x, "tile")], out_dtypes=(F32,))
        else:
            j = l - n_a
            if j == 0:
                hk = _rms_fwd(x, norms["kv_norm"], name="kv_norm")
                down = _mm(hk, wdkv, name="kv_down", out_dtypes=(F32,))
                cat = _kv_prep(down, norms["mla_kv_lat_norm"], cos_t, sin_t, name="kv_prep")
                kv = _mm(cat, wkv, name="kv_up")
                kv_saved = {"x_in": x, "hk": hk, "down": down, "cat": cat}
            cq0 = _mm(h, wdq[j], name=f"{t}_q_down", out_dtypes=(F32,))
            cq = _rms_fwd(cq0, norms["mla_q_lat_norm"][j], name=f"{t}_q_lat_norm")
            q = _mm(cq, wuq[j], name=f"{t}_q_up", epilogue=_epi_rope_heads, extras=[(cos_t, "row"), (sin_t, "row")])
            o, lse = _mla_fwd(q, kv, heads, name=f"{t}_mla_fwd")
            sv.update(cq0=cq0, cq=cq, q=q, o=o, lse=lse)
            x = _mm(o, wo_b[j], name=f"{t}_attn_out", epilogue=_epi_add, extras=[(x, "tile")], out_dtypes=(F32,))
        sv["x_mid"] = x
        h2 = _rms_fwd(x, norms["mlp_norm"][l], name=f"{t}_mlp_norm")
        u, a = _mm(h2, w1[l], name=f"{t}_mlp_up", epilogue=_epi_relu2, out_dtypes=(BF16, BF16))
        sv.update(h2=h2, u=u, a=a)
        x = _mm(a, w2[l], name=f"{t}_mlp_down", epilogue=_epi_add, extras=[(x, "tile")], out_dtypes=(F32,))
        saved.append(sv)

    loss_slab, dx, dxb, dg_final = _loss_bwd(x, norms["final_norm"], target, name="loss")
    loss = loss_slab[0, 0]

    g_attn_norm, g_mlp_norm = [None] * depth, [None] * depth
    g_qkv, g_o_a = [None] * n_a, [None] * n_a
    g_dq, g_uq, g_o_b, g_qlat = [None] * n_b, [None] * n_b, [None] * n_b, [None] * n_b
    g_w1, g_w2 = [None] * depth, [None] * depth
    dkv = None
    g_kv_norm = g_kv_lat = g_dkv = g_ukv = None
    early_parts = early_got = None

    for l in reversed(range(depth)):
        t = f"l{l}"
        sv = saved[l]
        du = _mm(dxb, w2[l], name=f"{t}_mlp_down_dx", dims="nt", epilogue=_epi_relu2_grad, extras=[(sv["u"], "tile")])
        g_w2[l] = _mm(sv["a"], dxb, name=f"{t}_mlp_down_dw", dims="tn", out_dtypes=(F32,))
        g_w1[l] = _mm(sv["h2"], du, name=f"{t}_mlp_up_dw", dims="tn", out_dtypes=(F32,))
        dh2 = _mm(du, w1[l], name=f"{t}_mlp_up_dx", dims="nt", out_dtypes=(F32,))
        dx, dxb, g_mlp_norm[l] = _rms_bwd(sv["x_mid"], norms["mlp_norm"][l], dh2, dx, name=f"{t}_mlp_norm_bwd")
        if l < n_a:
            do = _mm(dxb, wo_a[l], name=f"{t}_attn_out_dx", dims="nt")
            g_o_a[l] = _unpad_o(_mm(sv["o"], dxb, name=f"{t}_attn_out_dw", dims="tn", out_dtypes=(F32,)), heads)
            if l == 0 and early_reduce is not None:
                early_parts = early_reduce({
                    "sb_w_qkv": g_qkv[1:], "sb_w_o": g_o_a, "mla_w_dkv": g_dkv, "mla_w_ukv": g_ukv, "mla_w_dq": g_dq,
                    "mla_w_uq": g_uq, "mla_w_o": g_o_b, "mlp_w1": g_w1, "mlp_w2": g_w2})
                dq, dk, dv, early_got = _sb_bwd(sv["qkv"], sv["o"], do, heads, name=f"{t}_sb_bwd", ride=early_parts)
            else:
                dq, dk, dv = _sb_bwd(sv["qkv"], sv["o"], do, heads, name=f"{t}_sb_bwd")
            dqkv = jnp.concatenate([dq, dk, dv], axis=1)
            g_qkv[l] = _unpad_qkv(_mm(sv["h"], dqkv, name=f"{t}_qkv_dw", dims="tn", out_dtypes=(F32,)), heads)
            dh = _mm(dqkv, wqkv[l], name=f"{t}_qkv_dx", dims="nt", out_dtypes=(F32,))
        else:
            j = l - n_a
            do = _mm(dxb, wo_b[j], name=f"{t}_attn_out_dx", dims="nt")
            g_o_b[j] = _unpad_o(_mm(sv["o"], dxb, name=f"{t}_attn_out_dw", dims="tn", out_dtypes=(F32,)), heads)
            dq, dkv = _mla_bwd(sv["q"], kv, sv["o"], do, sv["lse"], cos_t, sin_t, dkv, heads, name=f"{t}_mla_bwd")
            g_uq[j] = _unpad_uq(_mm(sv["cq"], dq, name=f"{t}_q_up_dw", dims="tn", out_dtypes=(F32,)), heads)
            dcq = _mm(dq, wuq[j], name=f"{t}_q_up_dx", dims="nt", out_dtypes=(F32,))
            _, dcq0, g_qlat[j] = _rms_bwd(sv["cq0"], norms["mla_q_lat_norm"][j], dcq, None, name=f"{t}_q_lat_norm_bwd")
            g_dq[j] = _mm(sv["h"], dcq0, name=f"{t}_q_down_dw", dims="tn", out_dtypes=(F32,))
            dh = _mm(dcq0, wdq[j], name=f"{t}_q_down_dx", dims="nt", out_dtypes=(F32,))
        dx, dxb, g_attn_norm[l] = _rms_bwd(sv["x_in"], norms["attn_norm"][l], dh, dx, name=f"{t}_attn_norm_bwd",
                                           lead_axis=(l == 0))
        if l == n_a:
            ks = kv_saved
            dcat = _mm(dkv, wkv, name="kv_up_dx", dims="nt", out_dtypes=(F32,))
            g_ukv = _unpad_ukv(_mm(ks["cat"], dkv, name="kv_up_dw", dims="tn", out_dtypes=(F32,)), heads)
            ddown, g_kv_lat = _kv_prep_bwd(ks["down"], norms["mla_kv_lat_norm"], cos_t, sin_t, dcat, name="kv_prep_bwd")
            g_dkv = _unpad_dkv(_mm(ks["hk"], ddown, name="kv_down_dw", dims="tn", out_dtypes=(F32,)))
            dhk = _mm(ddown, wdkv, name="kv_down_dx", dims="nt", out_dtypes=(F32,))
            dx, dxb, g_kv_norm = _rms_bwd(ks["x_in"], norms["kv_norm"], dhk, dx, name="kv_norm_bwd")

    grads = {
        "attn_norm": jnp.concatenate(g_attn_norm, axis=0), "mlp_norm": jnp.concatenate(g_mlp_norm, axis=0),
        "sb_w_qkv": g_qkv, "sb_w_o": g_o_a,
        "kv_norm": g_kv_norm[0], "mla_w_dkv": g_dkv, "mla_kv_lat_norm": g_kv_lat[0], "mla_w_ukv": g_ukv,
        "mla_w_dq": g_dq, "mla_q_lat_norm": jnp.concatenate(g_qlat, axis=0),
        "mla_w_uq": g_uq, "mla_w_o": g_o_b,
        "mlp_w1": g_w1, "mlp_w2": g_w2, "final_norm": dg_final[0],
    }
    return loss, dx, grads, early_parts, early_got


def _flat_rows(n_elems):
    per_block = FLAT_COLS * FLAT_ROW_BLOCK * 2
    return -(-n_elems // per_block) * FLAT_ROW_BLOCK * 2


def _row_blocks(arrays, dtype):
    for a in arrays:
        assert a.size % FLAT_COLS == 0, a.shape
    blocks = [a.astype(dtype).reshape(-1, FLAT_COLS) for a in arrays]
    used = sum(b.shape[0] for b in blocks)
    rows = _flat_rows(used * FLAT_COLS)
    return blocks + [jnp.zeros((rows - used, FLAT_COLS), dtype)], rows


def _pack(arrays, dtype):
    blocks, _ = _row_blocks(arrays, dtype)
    return jnp.concatenate(blocks, axis=0)


def _pack_chips(per_chip, dtype):
    blocks, rows = [], 0
    for arrays in per_chip:
        chip_blocks, rows = _row_blocks(arrays, dtype)
        blocks += chip_blocks
    return jnp.concatenate(blocks, axis=0).reshape(len(per_chip), rows, FLAT_COLS)


def _unpack(flat, shapes):
    out, row = [], 0
    for shp in shapes:
        n = 1
        for v in shp:
            n *= v
        out.append(flat[row:row + n // FLAT_COLS].reshape(shp))
        row += n // FLAT_COLS
    return out


def _pack_small(arrays):
    rows = []
    for a in arrays:
        a = a.reshape(-1, a.shape[-1]) if a.shape[-1] == FLAT_COLS else a.reshape(1, -1)
        rows.append(_pad_last(a, FLAT_COLS))
    flat = jnp.concatenate(rows, axis=0)
    return jnp.pad(flat, [(0, -flat.shape[0] % 8), (0, 0)])


def _unpack_small(flat, shapes):
    out, row = [], 0
    for shp in shapes:
        if shp[-1] == FLAT_COLS:
            n = 1
            for v in shp[:-1]:
                n *= v
            out.append(flat[row:row + n].reshape(shp))
            row += n
        else:
            n = 1
            for v in shp:
                n *= v
            out.append(flat[row, :n].reshape(shp))
            row += 1
    return out


def _other_chips(x, y):
    return [(1 - x, y), (x, 1 - y), (1 - x, 1 - y)]


def _all_gather_chips(flat, *, name):
    rows, cols = flat.shape

    def body(x_ref, out_ref, *sems):
        _all_gather_start(x_ref, out_ref, sems)
        _all_gather_finish(x_ref, out_ref, sems)

    return pl.pallas_call(
        body, name=name,
        in_specs=[pl.BlockSpec(memory_space=pltpu.HBM)],
        out_specs=pl.BlockSpec(memory_space=pltpu.HBM),
        out_shape=jax.ShapeDtypeStruct((N_CHIPS, rows, cols), flat.dtype),
        scratch_shapes=_all_gather_sems(),
        compiler_params=pltpu.CompilerParams(has_side_effects=True),
    )(flat)


def _all_gather_sems():
    return [pltpu.SemaphoreType.DMA((3,)), pltpu.SemaphoreType.DMA((3,)), pltpu.SemaphoreType.DMA((3,)),
            pltpu.SemaphoreType.DMA((3,)), pltpu.SemaphoreType.DMA, pltpu.SemaphoreType.DMA]


def _all_gather_copies(x_ref, out_ref, sems, finishing):
    send_sems, recv_sems, pass_send_sems, pass_recv_sems, own_send_sem, own_recv_sem = sems
    x, y, c = lax.axis_index("x"), lax.axis_index("y"), lax.axis_index("c")
    me = 2 * x + y
    my_rows, sib_rows = _half_rows(x_ref.shape[0])

    def copy(src, dst, send_sem, recv_sem, to):
        return pltpu.make_async_remote_copy(src_ref=src, dst_ref=dst, send_sem=send_sem, recv_sem=recv_sem,
                                            device_id=to, device_id_type=MESH)

    own = copy(x_ref, out_ref.at[me], own_send_sem, own_recv_sem, _sibling())
    to_chips, landed, pass_on, passed = [], [], [], []
    for k, (px, py) in enumerate(_other_chips(x, y)):
        to_chips.append(copy(x_ref.at[my_rows, :], out_ref.at[me, my_rows, :], send_sems.at[k], recv_sems.at[k],
                             (px, py, c)))
        if finishing:
            mine, theirs = out_ref.at[2 * px + py, my_rows, :], out_ref.at[2 * px + py, sib_rows, :]
            landed.append(copy(mine, mine, send_sems.at[k], recv_sems.at[k], (px, py, c)))
            pass_on.append(copy(mine, mine, pass_send_sems.at[k], pass_recv_sems.at[k], _sibling()))
            passed.append(copy(theirs, theirs, pass_send_sems.at[k], pass_recv_sems.at[k], _sibling()))
    return own, to_chips, landed, pass_on, passed


def _all_gather_start(x_ref, out_ref, sems):
    own, to_chips, _, _, _ = _all_gather_copies(x_ref, out_ref, sems, finishing=False)
    own.start()
    for cp in to_chips:
        cp.start()


def _all_gather_finish(x_ref, out_ref, sems):
    own, to_chips, landed, pass_on, passed = _all_gather_copies(x_ref, out_ref, sems, finishing=True)
    for k in range(len(landed)):
        landed[k].wait_recv()
        pass_on[k].start()
    for cp in passed:
        cp.wait_recv()
    own.wait_recv()
    for cp in [own] + to_chips + pass_on:
        cp.wait_send()


def _exchange_chips(parts, *, name):
    def body(g_ref, out_ref, *sems):
        _exchange_start(g_ref, out_ref, sems)
        _exchange_finish(g_ref, out_ref, sems)

    return pl.pallas_call(
        body, name=name,
        in_specs=[pl.BlockSpec(memory_space=pltpu.HBM)],
        out_specs=pl.BlockSpec(memory_space=pltpu.HBM),
        out_shape=jax.ShapeDtypeStruct(parts.shape, parts.dtype),
        scratch_shapes=_exchange_sems(),
        compiler_params=pltpu.CompilerParams(has_side_effects=True),
    )(parts)


def _exchange_sems():
    return [pltpu.SemaphoreType.DMA((3,)), pltpu.SemaphoreType.DMA((3,))]


def _exchange_copies(g_ref, out_ref, sems, receiving):
    send_sems, recv_sems = sems
    x, y, c = lax.axis_index("x"), lax.axis_index("y"), lax.axis_index("c")
    me = 2 * x + y
    copies = []
    for k, (px, py) in enumerate(_other_chips(x, y)):
        src, dst = (g_ref.at[me], out_ref.at[2 * px + py]) if receiving else (g_ref.at[2 * px + py], out_ref.at[me])
        copies.append(pltpu.make_async_remote_copy(src_ref=src, dst_ref=dst, send_sem=send_sems.at[k],
                                                   recv_sem=recv_sems.at[k], device_id=(px, py, c),
                                                   device_id_type=MESH))
    return copies


def _exchange_start(g_ref, out_ref, sems):
    for cp in _exchange_copies(g_ref, out_ref, sems, receiving=False):
        cp.start()


def _exchange_finish(g_ref, out_ref, sems):
    for cp in _exchange_copies(g_ref, out_ref, sems, receiving=True):
        cp.wait_recv()
    for cp in _exchange_copies(g_ref, out_ref, sems, receiving=False):
        cp.wait_send()


def _my_chip():
    return 2 * lax.axis_index("x") + lax.axis_index("y")


def _half_rows(rows):
    c = lax.axis_index("c")
    half = rows // 2
    return pl.ds(pl.multiple_of(c * half, 8), half), pl.ds(pl.multiple_of((1 - c) * half, 8), half)


def _sibling():
    return (lax.axis_index("x"), lax.axis_index("y"), 1 - lax.axis_index("c"))


def _pair_exchange(parts, *, name):
    n, rows, cols = parts.shape

    def body(p_ref, theirs_ref, send_sem, recv_sem):
        _, sib_rows = _half_rows(rows)
        cp = pltpu.make_async_remote_copy(src_ref=p_ref.at[:, sib_rows, :], dst_ref=theirs_ref, send_sem=send_sem,
                                          recv_sem=recv_sem, device_id=_sibling(), device_id_type=MESH)
        cp.start()
        cp.wait()

    half = rows // 2
    theirs = pl.pallas_call(
        body, name=name,
        in_specs=[pl.BlockSpec(memory_space=pltpu.HBM)],
        out_specs=pl.BlockSpec(memory_space=pltpu.HBM),
        out_shape=jax.ShapeDtypeStruct((n, half, cols), parts.dtype),
        scratch_shapes=[pltpu.SemaphoreType.DMA, pltpu.SemaphoreType.DMA],
        compiler_params=pltpu.CompilerParams(has_side_effects=True),
    )(parts)
    mine = lax.dynamic_slice_in_dim(parts, lax.axis_index("c") * half, half, axis=1)
    return mine, theirs


def _pair_sum(mine, theirs, *, name):
    n, rows, cols = mine.shape

    def body(a_ref, b_ref, o_ref):
        o_ref[...] = (a_ref[...].astype(F32) + b_ref[...].astype(F32)).astype(o_ref.dtype)

    blk = pl.BlockSpec((n, FLAT_ROW_BLOCK, cols), lambda i: (0, i, 0))
    return pl.pallas_call(
        body, name=name, grid=(rows // FLAT_ROW_BLOCK,),
        in_specs=[blk, blk], out_specs=blk, out_shape=jax.ShapeDtypeStruct(mine.shape, mine.dtype),
        compiler_params=pltpu.CompilerParams(dimension_semantics=("parallel",), vmem_limit_bytes=VMEM_LIMIT),
    )(mine, theirs)


def _sum_chips(received, own, *, name):
    _, rows, cols = received.shape

    def body(p_ref, own_ref, o_ref):
        me = 2 * lax.axis_index("x") + lax.axis_index("y")
        slot = [jnp.where(me == j, own_ref[j], p_ref[j]).astype(F32) for j in range(N_CHIPS)]
        o_ref[...] = ((slot[0] + slot[1]) + slot[2]) + slot[3]

    blk = pl.BlockSpec((N_CHIPS, FLAT_ROW_BLOCK, cols), lambda i: (0, i, 0))
    return pl.pallas_call(
        body, name=name, grid=(rows // FLAT_ROW_BLOCK,),
        in_specs=[blk, blk],
        out_specs=pl.BlockSpec((FLAT_ROW_BLOCK, cols), lambda i: (i, 0)),
        out_shape=jax.ShapeDtypeStruct((rows, cols), F32),
        compiler_params=pltpu.CompilerParams(dimension_semantics=("parallel",), vmem_limit_bytes=VMEM_LIMIT),
    )(received, own)


def _join_cores(half, *, name):
    rows2, cols = half.shape

    def body(h_ref, out_ref, send_sem, recv_sem):
        my_rows, sib_rows = _half_rows(2 * rows2)
        cp = pltpu.make_async_remote_copy(src_ref=h_ref, dst_ref=out_ref.at[my_rows, :], send_sem=send_sem,
                                          recv_sem=recv_sem, device_id=_sibling(), device_id_type=MESH)
        cp.start()
        cp.wait_send()
        pltpu.make_async_remote_copy(src_ref=h_ref, dst_ref=out_ref.at[sib_rows, :], send_sem=send_sem,
                                     recv_sem=recv_sem, device_id=_sibling(), device_id_type=MESH).wait_recv()

    out = pl.pallas_call(
        body, name=name,
        in_specs=[pl.BlockSpec(memory_space=pltpu.HBM)],
        out_specs=pl.BlockSpec(memory_space=pltpu.HBM),
        out_shape=jax.ShapeDtypeStruct((2 * rows2, cols), half.dtype),
        scratch_shapes=[pltpu.SemaphoreType.DMA, pltpu.SemaphoreType.DMA],
        compiler_params=pltpu.CompilerParams(has_side_effects=True),
    )(half)
    return lax.dynamic_update_slice_in_dim(out, half, lax.axis_index("c") * rows2, axis=0)


def _all_reduce_small(v, *, name):
    rows, cols = v.shape
    flips = [(fx, fy, fc) for fx in (0, 1) for fy in (0, 1) for fc in (0, 1)][1:]

    def body(v_ref, out_ref, gath_ref, send_sems, recv_sems):
        x, y, c = lax.axis_index("x"), lax.axis_index("y"), lax.axis_index("c")
        me = 4 * x + 2 * y + c
        gath_ref[me] = v_ref[...]
        peers = [((1 - x) if fx else x, (1 - y) if fy else y, (1 - c) if fc else c) for fx, fy, fc in flips]
        sends = []
        for k, peer in enumerate(peers):
            cp = pltpu.make_async_remote_copy(src_ref=v_ref, dst_ref=gath_ref.at[me], send_sem=send_sems.at[k],
                                              recv_sem=recv_sems.at[k], device_id=peer, device_id_type=MESH)
            cp.start()
            sends.append(cp)
        for k, (px, py, pc) in enumerate(peers):
            pltpu.make_async_remote_copy(src_ref=v_ref, dst_ref=gath_ref.at[4 * px + 2 * py + pc],
                                         send_sem=send_sems.at[k], recv_sem=recv_sems.at[k],
                                         device_id=(px, py, pc), device_id_type=MESH).wait_recv()
        for cp in sends:
            cp.wait_send()
        total = gath_ref[0]
        for k in range(1, 8):
            total = total + gath_ref[k]
        out_ref[...] = total

    total, _ = pl.pallas_call(
        body, name=name,
        in_specs=[pl.BlockSpec(memory_space=pltpu.VMEM)],
        out_specs=[pl.BlockSpec(memory_space=pltpu.VMEM), pl.BlockSpec(memory_space=pltpu.VMEM)],
        out_shape=[jax.ShapeDtypeStruct((rows, cols), v.dtype), jax.ShapeDtypeStruct((8, rows, cols), v.dtype)],
        scratch_shapes=[pltpu.SemaphoreType.DMA((7,)), pltpu.SemaphoreType.DMA((7,))],
        compiler_params=pltpu.CompilerParams(has_side_effects=True),
    )(v)
    return total


def _adamw(w, g, m, v, *, name):
    shape = w.shape
    cols = shape[-1]
    w2, g2, m2, v2 = (a.reshape(-1, cols) for a in (w, g, m, v))
    rows = w2.shape[0]
    br = _pick_rows(rows, FLAT_ROW_BLOCK)

    def body(w_ref, g_ref, m_ref, v_ref, d_out, m_out, v_out):
        gv = g_ref[...]
        m_new = ADAM_B1 * m_ref[...] + (1.0 - ADAM_B1) * gv
        v_new = ADAM_B2 * v_ref[...] + (1.0 - ADAM_B2) * jnp.square(gv)
        m_hat = m_new / (1.0 - ADAM_B1 ** ADAM_STEP)
        v_hat = v_new / (1.0 - ADAM_B2 ** ADAM_STEP)
        d_out[...] = -ADAM_LR * (m_hat / (jnp.sqrt(v_hat) + ADAM_EPS) + ADAM_WD * w_ref[...])
        m_out[...] = m_new
        v_out[...] = v_new

    blk = pl.BlockSpec((br, cols), lambda i: (i, 0))
    out = jax.ShapeDtypeStruct((rows, cols), F32)
    outs = pl.pallas_call(
        body, name=name, grid=(rows // br,),
        in_specs=[blk] * 4, out_specs=[blk] * 3, out_shape=[out] * 3,
        compiler_params=pltpu.CompilerParams(dimension_semantics=("parallel",), vmem_limit_bytes=VMEM_LIMIT),
    )(w2, g2, m2, v2)
    return [o.reshape(shape) for o in outs]


def _pick_rows(rows, target):
    if rows <= target:
        return rows
    return max(b for b in range(8, target + 1, 8) if rows % b == 0)


def _assemble(gathered_shards, name, layer=False):
    return jnp.concatenate(gathered_shards, axis=SHARD_AXIS[name] - int(layer))


def _chip_shard(full, name, j):
    if isinstance(full, list):
        axis = SHARD_AXIS[name] - 1
        layers = full
    else:
        axis = SHARD_AXIS[name]
        layers = [full]
    n = layers[0].shape[axis] // N_CHIPS
    return [lax.slice_in_dim(g, j * n, (j + 1) * n, axis=axis) for g in layers]


def kernel(x, positions, attn_norm, mlp_norm, sb_w_qkv, sb_w_o, kv_norm, mla_w_dkv, mla_kv_lat_norm, mla_w_ukv, mla_w_dq, mla_q_lat_norm, mla_w_uq, mla_w_o, mlp_w1, mlp_w2, final_norm, loss_target, m_attn_norm, m_mlp_norm, m_sb_w_qkv, m_sb_w_o, m_kv_norm, m_mla_w_dkv, m_mla_kv_lat_norm, m_mla_w_ukv, m_mla_w_dq, m_mla_q_lat_norm, m_mla_w_uq, m_mla_w_o, m_mlp_w1, m_mlp_w2, m_final_norm, v_attn_norm, v_mlp_norm, v_sb_w_qkv, v_sb_w_o, v_kv_norm, v_mla_w_dkv, v_mla_kv_lat_norm, v_mla_w_ukv, v_mla_w_dq, v_mla_q_lat_norm, v_mla_w_uq, v_mla_w_o, v_mlp_w1, v_mlp_w2, v_final_norm):
    weights = dict(attn_norm=attn_norm, mlp_norm=mlp_norm, sb_w_qkv=sb_w_qkv, sb_w_o=sb_w_o, kv_norm=kv_norm,
                   mla_w_dkv=mla_w_dkv, mla_kv_lat_norm=mla_kv_lat_norm, mla_w_ukv=mla_w_ukv, mla_w_dq=mla_w_dq,
                   mla_q_lat_norm=mla_q_lat_norm, mla_w_uq=mla_w_uq, mla_w_o=mla_w_o, mlp_w1=mlp_w1, mlp_w2=mlp_w2,
                   final_norm=final_norm)
    m_in = dict(attn_norm=m_attn_norm, mlp_norm=m_mlp_norm, sb_w_qkv=m_sb_w_qkv, sb_w_o=m_sb_w_o, kv_norm=m_kv_norm,
                mla_w_dkv=m_mla_w_dkv, mla_kv_lat_norm=m_mla_kv_lat_norm, mla_w_ukv=m_mla_w_ukv, mla_w_dq=m_mla_w_dq,
                mla_q_lat_norm=m_mla_q_lat_norm, mla_w_uq=m_mla_w_uq, mla_w_o=m_mla_w_o, mlp_w1=m_mlp_w1,
                mlp_w2=m_mlp_w2, final_norm=m_final_norm)
    v_in = dict(attn_norm=v_attn_norm, mlp_norm=v_mlp_norm, sb_w_qkv=v_sb_w_qkv, sb_w_o=v_sb_w_o, kv_norm=v_kv_norm,
                mla_w_dkv=v_mla_w_dkv, mla_kv_lat_norm=v_mla_kv_lat_norm, mla_w_ukv=v_mla_w_ukv, mla_w_dq=v_mla_w_dq,
                mla_q_lat_norm=v_mla_q_lat_norm, mla_w_uq=v_mla_w_uq, mla_w_o=v_mla_w_o, mlp_w1=v_mlp_w1,
                mlp_w2=v_mlp_w2, final_norm=v_final_norm)
    shard_shapes = [weights[n].shape for n in BIG_WEIGHTS]
    small_shapes = [weights[n].shape for n in SMALL_WEIGHTS]

    first_name = BIG_WEIGHTS[0]
    qkv_first, qkv_later = weights[first_name][0], weights[first_name][1:]
    gathered_first = _all_gather_chips(_pack([qkv_first], BF16), name="first_weight_all_gather")
    qkv_w0 = _assemble([_unpack(gathered_first[j], [qkv_first.shape])[0] for j in range(N_CHIPS)], first_name, layer=True)
    ride = _pack([qkv_later] + [weights[n] for n in BIG_WEIGHTS[1:]], BF16)
    ride_shapes = [qkv_later.shape] + shard_shapes[1:]

    def rest_weights(gathered):
        per_chip = [_unpack(gathered[j], ride_shapes) for j in range(N_CHIPS)]
        full = {n: _assemble([per_chip[j][i] for j in range(N_CHIPS)], n) for i, n in enumerate(BIG_WEIGHTS)}
        full[first_name] = jnp.concatenate([qkv_w0[None], full[first_name]], axis=0)
        return full

    norms = {n: weights[n] for n in SMALL_WEIGHTS}

    def chip_parts(g, tag):
        parts = _pack_chips([[piece for n in BIG_WEIGHTS if n in g for piece in _chip_shard(g[n], n, j)]
                             for j in range(N_CHIPS)], BF16)
        mine, theirs = _pair_exchange(parts, name=f"grads_pair_exchange_{tag}")
        return _pair_sum(mine, theirs, name=f"grads_pair_sum_{tag}")

    def finish(received, chip_part, tag):
        g_half = _sum_chips(received, chip_part, name=f"grads_sum_chips_{tag}")
        return _join_cores(g_half, name=f"grads_join_cores_{tag}")

    loss, dx, grads, early_parts, early_got = _local_step(
        x[0], positions[0], loss_target, qkv_w0, norms, rest_weights, ride=ride,
        early_reduce=functools.partial(chip_parts, tag="early"))
    loss = lax.psum(loss, ("x", "y", "c"))
    early_sum = finish(early_got, early_parts, "early")
    last_parts = chip_parts({first_name: grads[first_name][:1]}, "last")
    last_sum = finish(_exchange_chips(last_parts, name="grads_exchange_last"), last_parts, "last")

    out_g = dict(zip(BIG_WEIGHTS, _unpack(early_sum, ride_shapes)))
    out_g[first_name] = jnp.concatenate([_unpack(last_sum, [qkv_first.shape])[0][None], out_g[first_name]], axis=0)
    out_d, out_m, out_v = {}, {}, {}
    for n in BIG_WEIGHTS:
        out_d[n], out_m[n], out_v[n] = _adamw(weights[n], out_g[n], m_in[n], v_in[n], name=f"adamw_{n}")

    small_sum = _all_reduce_small(_pack_small([grads[n] for n in SMALL_WEIGHTS]), name="gains_all_reduce")
    sd, sm, sv = _adamw(_pack_small([weights[n] for n in SMALL_WEIGHTS]), small_sum,
                        _pack_small([m_in[n] for n in SMALL_WEIGHTS]),
                        _pack_small([v_in[n] for n in SMALL_WEIGHTS]), name="adamw_gains")
    out_g.update(zip(SMALL_WEIGHTS, _unpack_small(small_sum, small_shapes)))
    out_d.update(zip(SMALL_WEIGHTS, _unpack_small(sd, small_shapes)))
    out_m.update(zip(SMALL_WEIGHTS, _unpack_small(sm, small_shapes)))
    out_v.update(zip(SMALL_WEIGHTS, _unpack_small(sv, small_shapes)))

    return (loss, dx, *[out_g[n] for n in ALL_WEIGHTS], *[out_d[n] for n in ALL_WEIGHTS],
            *[out_m[n] for n in ALL_WEIGHTS], *[out_v[n] for n in ALL_WEIGHTS])
```

```python
import functools

import jax
import jax.numpy as jnp
from jax import lax
from jax.experimental import pallas as pl
from jax.experimental.pallas import tpu as pltpu

F32 = jnp.float32
BF16 = jnp.bfloat16

LANES = 128
SB_HEAD_DIM = 64
MLA_NOPE = 64
MLA_ROPE = 32
MLA_V = 64
MLA_Q_RANK = 384
MLA_KV_RANK = 256
CHUNK = 64
ROPE_THETA = 10000.0
NORM_EPS = 1e-6
SB_SCALE = SB_HEAD_DIM ** -0.5
MLA_SCALE = (MLA_NOPE + MLA_ROPE) ** -0.5
ROPE_LO = MLA_NOPE
ROPE_HALF = MLA_ROPE // 2
ATT_Q_BLOCK = 1024
ATT_K_BLOCK = 256
MLA_FWD_K_BLOCK = 512
NEG_BIG = -1e30
SB_DEAD_LOG = -110.0
VMEM_LIMIT = 56 * 1024 * 1024

ADAM_LR = 0.001
ADAM_B1 = 0.9
ADAM_B2 = 0.999
ADAM_EPS = 1e-08
ADAM_WD = 0.01
ADAM_STEP = 10

FLAT_COLS = 1024
FLAT_ROW_BLOCK = 256
N_CHIPS = 4
MESH = pl.DeviceIdType.MESH

BIG_WEIGHTS = ["sb_w_qkv", "sb_w_o", "mla_w_dkv", "mla_w_ukv", "mla_w_dq", "mla_w_uq", "mla_w_o", "mlp_w1", "mlp_w2"]
SHARD_AXIS = {"sb_w_qkv": 2, "sb_w_o": 1, "mla_w_dkv": 0, "mla_w_ukv": 1, "mla_w_dq": 1, "mla_w_uq": 2,
              "mla_w_o": 1, "mlp_w1": 2, "mlp_w2": 1}
SMALL_WEIGHTS = ["attn_norm", "mlp_norm", "kv_norm", "mla_kv_lat_norm", "mla_q_lat_norm", "final_norm"]
ALL_WEIGHTS = ["attn_norm", "mlp_norm", "sb_w_qkv", "sb_w_o", "kv_norm", "mla_w_dkv", "mla_kv_lat_norm", "mla_w_ukv",
               "mla_w_dq", "mla_q_lat_norm", "mla_w_uq", "mla_w_o", "mlp_w1", "mlp_w2", "final_norm"]


def _dot(a, b, dims):
    return lax.dot_general(a, b, (dims, ((), ())), preferred_element_type=F32)


def _dot_nn(a, b):
    return _dot(a, b, ((1,), (0,)))


def _dot_nt(a, b):
    return _dot(a, b, ((1,), (1,)))


def _dot_tn(a, b):
    return _dot(a, b, ((0,), (0,)))


def _pick_block(n, target):
    if n <= target:
        return n
    best = max(b for b in range(LANES, target + 1, LANES) if n % b == 0)
    return best


MM_ROWS = 512
MM_COLS = 1024
MM_DEPTH = 4096
MM_DEPTH_TN = 1024


def _mm(a, b, *, name, dims="nn", epilogue=None, extras=(), out_dtypes=(BF16,)):
    if dims == "nn":
        (m, k), (k2, n) = a.shape, b.shape
    elif dims == "nt":
        (m, k), (n, k2) = a.shape, b.shape
    else:
        (k, m), (k2, n) = a.shape, b.shape
    assert k == k2, (name, a.shape, b.shape)
    if dims == "tn":
        bm, bn, bk = _pick_block(m, MM_COLS), _pick_block(n, MM_COLS), _pick_block(k, MM_DEPTH_TN)
    else:
        rows = MM_ROWS if k > MM_DEPTH // 2 else 2 * MM_ROWS
        bm, bn, bk = _pick_block(m, rows), _pick_block(n, MM_COLS), _pick_block(k, MM_DEPTH)
    nk = k // bk
    if dims == "tn":
        a_spec = pl.BlockSpec((bk, bm), lambda j, i, kk: (kk, i))
    else:
        a_spec = pl.BlockSpec((bm, bk), lambda j, i, kk: (i, kk))
    if dims == "nt":
        b_spec = pl.BlockSpec((bn, bk), lambda j, i, kk: (j, kk))
    else:
        b_spec = pl.BlockSpec((bk, bn), lambda j, i, kk: (kk, j))
    extra_specs = []
    for arr, kind in extras:
        if kind == "tile":
            assert arr.shape == (m, n), (name, arr.shape)
            extra_specs.append(pl.BlockSpec((bm, bn), lambda j, i, kk: (i, j)))
        else:
            assert arr.shape == (m, LANES), (name, arr.shape)
            extra_specs.append(pl.BlockSpec((bm, LANES), lambda j, i, kk: (i, 0)))
    n_extra = len(extras)
    n_out = len(out_dtypes)
    dot = {"nn": _dot_nn, "nt": _dot_nt, "tn": _dot_tn}[dims]

    def body(*refs):
        a_ref, b_ref = refs[0], refs[1]
        extra_refs = refs[2:2 + n_extra]
        out_refs = refs[2 + n_extra:2 + n_extra + n_out]

        def finish(acc):
            outs = (acc,) if epilogue is None else epilogue(acc, *[r[...] for r in extra_refs])
            for o_ref, o in zip(out_refs, outs):
                o_ref[...] = o.astype(o_ref.dtype)

        part = dot(a_ref[...].astype(BF16), b_ref[...].astype(BF16))
        if nk == 1:
            finish(part)
            return
        acc_ref = refs[-1]
        kk = pl.program_id(2)

        @pl.when(kk == 0)
        def _():
            acc_ref[...] = part

        @pl.when(kk > 0)
        def _():
            acc_ref[...] += part

        @pl.when(kk == nk - 1)
        def _():
            finish(acc_ref[...])

    outs = pl.pallas_call(
        body, name=name, grid=(n // bn, m // bm, nk),
        in_specs=[a_spec, b_spec] + extra_specs,
        out_specs=[pl.BlockSpec((bm, bn), lambda j, i, kk: (i, j)) for _ in range(n_out)],
        out_shape=[jax.ShapeDtypeStruct((m, n), dt) for dt in out_dtypes],
        scratch_shapes=[pltpu.VMEM((bm, bn), F32)] if nk > 1 else [],
        compiler_params=pltpu.CompilerParams(dimension_semantics=("parallel", "parallel", "arbitrary"),
                                             vmem_limit_bytes=VMEM_LIMIT),
    )(a, b, *[arr for arr, _ in extras])
    return outs[0] if n_out == 1 else outs


def _epi_add(acc, res):
    return (res + acc,)


def _epi_relu2(acc):
    r = jnp.maximum(acc, 0.0)
    return acc, r * r


def _epi_relu2_grad(acc, u):
    return (acc * (2.0 * jnp.maximum(u.astype(F32), 0.0)),)


def _rope_slab(t, cos_t, sin_t):
    lane = lax.broadcasted_iota(jnp.int32, t.shape, 1)
    partner = jnp.where(lane < ROPE_LO + ROPE_HALF, pltpu.roll(t, LANES - ROPE_HALF, 1), pltpu.roll(t, ROPE_HALF, 1))
    return t * cos_t + partner * sin_t


def _rope_slab_bwd(d, cos_t, sin_t):
    ds = d * sin_t
    lane = lax.broadcasted_iota(jnp.int32, d.shape, 1)
    partner = jnp.where(lane < ROPE_LO + ROPE_HALF, pltpu.roll(ds, LANES - ROPE_HALF, 1), pltpu.roll(ds, ROPE_HALF, 1))
    in_rope = (lane >= ROPE_LO) & (lane < ROPE_LO + MLA_ROPE)
    return d * cos_t + jnp.where(in_rope, partner, 0.0)


def _epi_rope_heads(acc, cos_t, sin_t):
    slabs = [_rope_slab(acc[:, j * LANES:(j + 1) * LANES], cos_t, sin_t) for j in range(acc.shape[1] // LANES)]
    return (jnp.concatenate(slabs, axis=1) * MLA_SCALE,)


def _row_block(s):
    return min(512, s)


def _rms_fwd(x, g, *, name):
    s, d = x.shape
    bm = _row_block(s)

    def body(x_ref, g_ref, o_ref):
        xv = x_ref[...]
        r = lax.rsqrt(jnp.mean(xv * xv, axis=-1, keepdims=True) + NORM_EPS)
        o_ref[...] = ((xv * r) * g_ref[...]).astype(o_ref.dtype)

    return pl.pallas_call(
        body, name=name, grid=(s // bm,),
        in_specs=[pl.BlockSpec((bm, d), lambda i: (i, 0)), pl.BlockSpec((1, d), lambda i: (0, 0))],
        out_specs=pl.BlockSpec((bm, d), lambda i: (i, 0)),
        out_shape=jax.ShapeDtypeStruct((s, d), BF16),
        compiler_params=pltpu.CompilerParams(dimension_semantics=("parallel",), vmem_limit_bytes=VMEM_LIMIT),
    )(x, g.reshape(1, d))


def _rms_bwd_math(xv, gv, dy):
    r = lax.rsqrt(jnp.mean(xv * xv, axis=-1, keepdims=True) + NORM_EPS)
    xhat = xv * r
    dyg = dy * gv
    mdot = jnp.mean(dyg * xhat, axis=-1, keepdims=True)
    dx = r * (dyg - xhat * mdot)
    dg = jnp.sum(dy * xhat, axis=0, keepdims=True)
    return dx, dg


def _rms_bwd(x, g, dy, dres, *, name, lead_axis=False):
    s, d = x.shape
    bm = _row_block(s)
    has_res = dres is not None

    def body(*refs):
        x_ref, g_ref, dy_ref = refs[:3]
        dres_ref = refs[3] if has_res else None
        dx_ref, dxb_ref, dg_ref = refs[-3:]
        dx, dg = _rms_bwd_math(x_ref[...], g_ref[...], dy_ref[...].astype(F32))
        if has_res:
            dx = dx + dres_ref[...]
        dx_ref[...] = dx
        dxb_ref[...] = dx.astype(BF16)

        @pl.when(pl.program_id(0) == 0)
        def _():
            dg_ref[...] = jnp.zeros_like(dg_ref)

        dg_ref[...] += dg

    row = pl.BlockSpec((bm, d), lambda i: (i, 0))
    vec = pl.BlockSpec((1, d), lambda i: (0, 0))
    ins = [x, g.reshape(1, d), dy] + ([dres] if has_res else [])
    dx_spec, dx_shape = row, (s, d)
    if lead_axis:
        dx_spec, dx_shape = pl.BlockSpec((None, bm, d), lambda i: (0, i, 0)), (1, s, d)
    return pl.pallas_call(
        body, name=name, grid=(s // bm,),
        in_specs=[row, vec, row] + ([row] if has_res else []),
        out_specs=[dx_spec, row, vec],
        out_shape=[jax.ShapeDtypeStruct(dx_shape, F32), jax.ShapeDtypeStruct((s, d), BF16),
                   jax.ShapeDtypeStruct((1, d), F32)],
        compiler_params=pltpu.CompilerParams(dimension_semantics=("arbitrary",), vmem_limit_bytes=VMEM_LIMIT),
    )(*ins)


def _loss_bwd(x, g, target, *, name):
    s, d = x.shape
    bm = _row_block(s)

    def body(x_ref, g_ref, t_ref, loss_ref, dx_ref, dxb_ref, dg_ref):
        xv, gv = x_ref[...], g_ref[...]
        r = lax.rsqrt(jnp.mean(xv * xv, axis=-1, keepdims=True) + NORM_EPS)
        err = (xv * r) * gv - t_ref[...]
        dx, dg = _rms_bwd_math(xv, gv, err * (1.0 / d))
        dx_ref[...] = dx
        dxb_ref[...] = dx.astype(BF16)

        @pl.when(pl.program_id(0) == 0)
        def _():
            dg_ref[...] = jnp.zeros_like(dg_ref)
            loss_ref[...] = jnp.zeros_like(loss_ref)

        dg_ref[...] += dg
        loss_ref[...] += jnp.sum(jnp.mean(err * err, axis=-1, keepdims=True), axis=0, keepdims=True) * 0.5

    row = pl.BlockSpec((bm, d), lambda i: (i, 0))
    vec = pl.BlockSpec((1, d), lambda i: (0, 0))
    assert target.shape == (1, s, d), target.shape
    return pl.pallas_call(
        body, name=name, grid=(s // bm,),
        in_specs=[row, vec, pl.BlockSpec((None, bm, d), lambda i: (0, i, 0))],
        out_specs=[pl.BlockSpec((8, LANES), lambda i: (0, 0)), row, row, vec],
        out_shape=[jax.ShapeDtypeStruct((8, LANES), F32), jax.ShapeDtypeStruct((s, d), F32),
                   jax.ShapeDtypeStruct((s, d), BF16), jax.ShapeDtypeStruct((1, d), F32)],
        compiler_params=pltpu.CompilerParams(dimension_semantics=("arbitrary",), vmem_limit_bytes=VMEM_LIMIT),
    )(x, g.reshape(1, d), target)


def _kv_prep(down, g, cos_t, sin_t, *, name):
    s, w = down.shape
    bm = _row_block(s)

    def body(d_ref, g_ref, c_ref, s_ref, o_ref):
        lat = d_ref[:, :MLA_KV_RANK]
        r = lax.rsqrt(jnp.mean(lat * lat, axis=-1, keepdims=True) + NORM_EPS)
        o_ref[:, :MLA_KV_RANK] = ((lat * r) * g_ref[...]).astype(BF16)
        o_ref[:, MLA_KV_RANK:] = _rope_slab(d_ref[:, MLA_KV_RANK:], c_ref[...], s_ref[...]).astype(BF16)

    row = pl.BlockSpec((bm, w), lambda i: (i, 0))
    tab = pl.BlockSpec((bm, LANES), lambda i: (i, 0))
    return pl.pallas_call(
        body, name=name, grid=(s // bm,),
        in_specs=[row, pl.BlockSpec((1, MLA_KV_RANK), lambda i: (0, 0)), tab, tab],
        out_specs=row, out_shape=jax.ShapeDtypeStruct((s, w), BF16),
        compiler_params=pltpu.CompilerParams(dimension_semantics=("parallel",), vmem_limit_bytes=VMEM_LIMIT),
    )(down, g.reshape(1, MLA_KV_RANK), cos_t, sin_t)


def _kv_prep_bwd(down, g, cos_t, sin_t, dcat, *, name):
    s, w = down.shape
    bm = _row_block(s)

    def body(d_ref, g_ref, c_ref, s_ref, dc_ref, o_ref, dg_ref):
        dlat, dg = _rms_bwd_math(d_ref[:, :MLA_KV_RANK], g_ref[...], dc_ref[:, :MLA_KV_RANK])
        o_ref[:, :MLA_KV_RANK] = dlat.astype(BF16)
        o_ref[:, MLA_KV_RANK:] = _rope_slab_bwd(dc_ref[:, MLA_KV_RANK:], c_ref[...], s_ref[...]).astype(BF16)

        @pl.when(pl.program_id(0) == 0)
        def _():
            dg_ref[...] = jnp.zeros_like(dg_ref)

        dg_ref[...] += dg

    row = pl.BlockSpec((bm, w), lambda i: (i, 0))
    tab = pl.BlockSpec((bm, LANES), lambda i: (i, 0))
    vec = pl.BlockSpec((1, MLA_KV_RANK), lambda i: (0, 0))
    return pl.pallas_call(
        body, name=name, grid=(s // bm,),
        in_specs=[row, vec, tab, tab, row],
        out_specs=[row, vec],
        out_shape=[jax.ShapeDtypeStruct((s, w), BF16), jax.ShapeDtypeStruct((1, MLA_KV_RANK), F32)],
        compiler_params=pltpu.CompilerParams(dimension_semantics=("arbitrary",), vmem_limit_bytes=VMEM_LIMIT),
    )(down, g.reshape(1, MLA_KV_RANK), cos_t, sin_t, dcat)


def _split_bf16(v):
    hi = v.astype(BF16)
    lo = (v - hi.astype(F32)).astype(BF16)
    return hi, lo


def _suffix_matrices(n):
    row = lax.broadcasted_iota(jnp.int32, (n, n), 0)
    col = lax.broadcasted_iota(jnp.int32, (n, n), 1)
    incl = (row >= col).astype(BF16)
    return (row > col).astype(BF16), jnp.concatenate([incl, incl], axis=0)


def _suffix_sum(v, matrix):
    hi, lo = _split_bf16(v)
    return _dot_nn(jnp.concatenate([hi, lo], axis=1), matrix)


def _block_positions(qi, kb, bq, bk, r0, r1):
    row = qi * bq + r0 + lax.broadcasted_iota(jnp.int32, (r1 - r0, bk), 0)
    col = kb * bk + lax.broadcasted_iota(jnp.int32, (r1 - r0, bk), 1)
    return row, col


def _att_blocks(s, key_block=ATT_K_BLOCK):
    bq, bk = min(ATT_Q_BLOCK, s), min(key_block, s)
    return bq, bk, s // bq, bq // bk


def _sweep(qi, ratio, bk, step, unroll=2, alive=None):
    bq = ratio * bk
    for d in range(ratio):
        kb, r0 = (qi + 1) * ratio - 1 - d, (ratio - 1 - d) * bk
        near = bq if alive is None else min(r0 + 2 * bk, bq)
        step(kb, True, r0, near)
        if near < bq:
            pl.when(alive(near))(functools.partial(step, kb, False, near, bq))
    unroll = unroll if ratio % unroll == 0 else 1
    trips = qi * (ratio // unroll)

    def trip(i):
        for u in range(unroll):
            kb = qi * ratio - 1 - (i * unroll + u)
            if alive is None:
                step(kb, False, 0, bq)
            else:
                step(kb, False, 0, bk)
                pl.when(alive(bk))(functools.partial(step, kb, False, bk, bq))

    if alive is None:
        lax.fori_loop(0, trips, lambda i, carry: (trip(i), carry)[1], 0)
    else:
        lax.while_loop(lambda i: jnp.logical_and(i < trips, alive(0)), lambda i: (trip(i), i + 1)[1], 0)


def _stick_left(c_ref, r0):
    return jnp.max(c_ref[r0:, :]) > SB_DEAD_LOG


def _sb_logs(q, k):
    z = _dot_nt(q, k)
    lb = jnp.minimum(z, 0.0) - jnp.log(1.0 + jnp.exp(-jnp.abs(z)))
    return lb, lb - z


def _sb_fwd(qkv, heads, *, name, ride=None):
    s = qkv.shape[0]
    bq, bk, nq, ratio = _att_blocks(s)
    riding = ride is not None

    def body(*refs):
        if riding:
            q_ref, k_ref, v_ref, w_ref, o_ref, gath_ref, acc_ref, c_ref = refs[:8]
            first = jnp.logical_and(pl.program_id(0) == 0, pl.program_id(1) == 0)
            last = jnp.logical_and(pl.program_id(0) == heads - 1, pl.program_id(1) == nq - 1)
            pl.when(first)(functools.partial(_all_gather_start, w_ref, gath_ref, refs[8:]))
        else:
            q_ref, k_ref, v_ref, o_ref, acc_ref, c_ref = refs
        qi = pl.program_id(1)
        q = q_ref[...] * SB_SCALE
        m_strict, _ = _suffix_matrices(bk)
        acc_ref[...] = jnp.zeros_like(acc_ref)
        c_ref[...] = jnp.zeros_like(c_ref)

        def step(kb, masked, r0, r1):
            rows = pl.ds(pl.multiple_of(kb * bk, bk), bk)
            mine = pl.ds(r0, r1 - r0)
            k, v = k_ref[rows, :], v_ref[rows, :]
            lb, lk = _sb_logs(q[r0:r1], k)
            if masked:
                row, col = _block_positions(qi, kb, bq, bk, r0, r1)
                causal = col < row
                lk = jnp.where(causal, lk, 0.0)
            c = c_ref[mine, :]
            w = jnp.exp(lb + _dot_nn(lk.astype(BF16), m_strict) + jnp.tile(c, (1, bk // LANES)))
            if masked:
                w = jnp.where(causal, w, 0.0)
            acc_ref[mine, :] += _dot_nn(w.astype(BF16), v)
            c_ref[mine, :] = c + jnp.sum(lk, axis=-1, keepdims=True)

        _sweep(qi, ratio, bk, step, unroll=1, alive=functools.partial(_stick_left, c_ref))
        o_ref[...] = acc_ref[...].astype(o_ref.dtype)
        if riding:
            pl.when(last)(functools.partial(_all_gather_finish, w_ref, gath_ref, refs[8:]))

    hbm = pl.BlockSpec(memory_space=pltpu.HBM)
    o_spec = pl.BlockSpec((bq, LANES), lambda h, i: (i, h))
    o_shape = jax.ShapeDtypeStruct((s, heads * LANES), F32)
    return pl.pallas_call(
        body, name=name, grid=(heads, nq),
        in_specs=[pl.BlockSpec((bq, LANES), lambda h, i: (i, h)),
                  pl.BlockSpec((s, LANES), lambda h, i: (0, heads + h)),
                  pl.BlockSpec((s, LANES), lambda h, i: (0, 2 * heads + h))] + ([hbm] if riding else []),
        out_specs=[o_spec, hbm] if riding else o_spec,
        out_shape=[o_shape, jax.ShapeDtypeStruct((N_CHIPS,) + ride.shape, ride.dtype)] if riding else o_shape,
        scratch_shapes=[pltpu.VMEM((bq, LANES), F32), pltpu.VMEM((bq, LANES), F32)]
        + (_all_gather_sems() if riding else []),
        compiler_params=pltpu.CompilerParams(dimension_semantics=("arbitrary", "arbitrary"),
                                             vmem_limit_bytes=VMEM_LIMIT, has_side_effects=riding),
    )(*([qkv, qkv, qkv] + ([ride] if riding else [])))


def _sb_bwd(qkv, o, do, heads, *, name, ride=None):
    s = qkv.shape[0]
    bq, bk, nq, ratio = _att_blocks(s)
    riding = ride is not None

    def body(*refs):
        if riding:
            (q_ref, k_ref, v_ref, o_ref, do_ref, g_ref, dq_ref, dk_ref, dv_ref, got_ref,
             dq_acc, dk_acc, dv_acc, c_ref, e_ref) = refs[:15]
            first = jnp.logical_and(pl.program_id(0) == 0, pl.program_id(1) == 0)
            last = jnp.logical_and(pl.program_id(0) == heads - 1, pl.program_id(1) == nq - 1)
            pl.when(first)(functools.partial(_exchange_start, g_ref, got_ref, refs[15:]))
        else:
            q_ref, k_ref, v_ref, o_ref, do_ref, dq_ref, dk_ref, dv_ref, dq_acc, dk_acc, dv_acc, c_ref, e_ref = refs
        qi = pl.program_id(1)

        @pl.when(qi == 0)
        def _():
            dk_acc[...] = jnp.zeros_like(dk_acc)
            dv_acc[...] = jnp.zeros_like(dv_acc)

        q = q_ref[...] * SB_SCALE
        do = do_ref[...]
        q_t, do_t = q.T, do.T
        total = jnp.sum(do.astype(F32) * o_ref[...].astype(F32), axis=-1, keepdims=True)
        m_strict, m_incl = _suffix_matrices(bk)
        dq_acc[...] = jnp.zeros_like(dq_acc)
        c_ref[...] = jnp.zeros_like(c_ref)
        e_ref[...] = jnp.broadcast_to(total, e_ref.shape)
        reps = (1, bk // LANES)

        def step(kb, masked, r0, r1):
            rows = pl.ds(pl.multiple_of(kb * bk, bk), bk)
            mine = pl.ds(r0, r1 - r0)
            k, v = k_ref[rows, :], v_ref[rows, :]
            qs, dos = q[r0:r1], do[r0:r1]
            lb, lk_all = _sb_logs(qs, k)
            lk = lk_all
            if masked:
                row, col = _block_positions(qi, kb, bq, bk, r0, r1)
                causal = col < row
                lk = jnp.where(causal, lk_all, 0.0)
            c = c_ref[mine, :]
            w = jnp.exp(lb + _dot_nn(lk.astype(BF16), m_strict) + jnp.tile(c, reps))
            if masked:
                w = jnp.where(causal, w, 0.0)
            wb = w.astype(BF16)
            g = wb.astype(F32) * _dot_nt(dos, v)
            e = e_ref[mine, :]
            g_left = jnp.tile(e, reps) - _suffix_sum(g, m_incl)
            da = g * jnp.exp(lk_all) - jnp.exp(lb) * g_left
            if masked:
                da = jnp.where(causal, da, 0.0)
            dab = da.astype(BF16)
            dq_acc[mine, :] += _dot_nn(dab, k)
            dk_acc[:, rows] += _dot_nn(q_t[:, r0:r1], dab)
            dv_acc[:, rows] += _dot_nn(do_t[:, r0:r1], wb)
            e_ref[mine, :] = e - jnp.sum(g, axis=-1, keepdims=True)
            c_ref[mine, :] = c + jnp.sum(lk, axis=-1, keepdims=True)

        _sweep(qi, ratio, bk, step, unroll=1, alive=functools.partial(_stick_left, c_ref))
        dq_ref[...] = (dq_acc[...] * SB_SCALE).astype(dq_ref.dtype)

        @pl.when(qi == nq - 1)
        def _():
            dk_ref[...] = dk_acc[...].T.astype(dk_ref.dtype)
            dv_ref[...] = dv_acc[...].T.astype(dv_ref.dtype)

        if riding:
            pl.when(last)(functools.partial(_exchange_finish, g_ref, got_ref, refs[15:]))

    blk = pl.BlockSpec((bq, LANES), lambda h, i: (i, h))
    full = pl.BlockSpec((s, LANES), lambda h, i: (0, h))
    hbm = pl.BlockSpec(memory_space=pltpu.HBM)
    shape = jax.ShapeDtypeStruct((s, heads * LANES), BF16)
    return pl.pallas_call(
        body, name=name, grid=(heads, nq),
        in_specs=[blk,
                  pl.BlockSpec((s, LANES), lambda h, i: (0, heads + h)),
                  pl.BlockSpec((s, LANES), lambda h, i: (0, 2 * heads + h)),
                  blk, blk] + ([hbm] if riding else []),
        out_specs=[blk, full, full] + ([hbm] if riding else []),
        out_shape=[shape, shape, shape] + ([jax.ShapeDtypeStruct(ride.shape, ride.dtype)] if riding else []),
        scratch_shapes=[pltpu.VMEM((bq, LANES), F32), pltpu.VMEM((LANES, s), F32), pltpu.VMEM((LANES, s), F32),
                        pltpu.VMEM((bq, LANES), F32), pltpu.VMEM((bq, LANES), F32)]
        + (_exchange_sems() if riding else []),
        compiler_params=pltpu.CompilerParams(dimension_semantics=("arbitrary", "arbitrary"),
                                             vmem_limit_bytes=VMEM_LIMIT, has_side_effects=riding),
    )(*([qkv, qkv, qkv, o, do] + ([ride] if riding else [])))


def _chunk_allowed(qi, kb, bq, bk, r0, r1):
    row, col = _block_positions(qi, kb, bq, bk, r0, r1)
    return (col // CHUNK) <= (row // CHUNK)


def _mla_fwd(q, kv, heads, *, name):
    s = q.shape[0]
    bq, bk, nq, ratio = _att_blocks(s, MLA_FWD_K_BLOCK)
    reps = (1, bk // LANES)

    def body(q_ref, k_ref, v_ref, o_ref, lse_ref, acc_ref, m_ref, l_ref):
        qi = pl.program_id(1)
        qv = q_ref[...]
        acc_ref[...] = jnp.zeros_like(acc_ref)
        m_ref[...] = jnp.full_like(m_ref, NEG_BIG)
        l_ref[...] = jnp.zeros_like(l_ref)

        def step(kb, masked, r0, r1):
            rows = pl.ds(pl.multiple_of(kb * bk, bk), bk)
            mine = pl.ds(r0, r1 - r0)
            k, v = k_ref[rows, :], v_ref[rows, :]
            sc = _dot_nt(qv[r0:r1], k)
            if masked:
                allowed = _chunk_allowed(qi, kb, bq, bk, r0, r1)
                sc = jnp.where(allowed, sc, NEG_BIG)
            m_old = m_ref[mine, :]
            m_new = jnp.maximum(m_old, jnp.max(sc, axis=-1, keepdims=True))
            p = jnp.exp(sc - jnp.tile(m_new, reps))
            alpha = jnp.exp(m_old - m_new)
            l_ref[mine, :] = alpha * l_ref[mine, :] + jnp.sum(p, axis=-1, keepdims=True)
            acc_ref[mine, :] = alpha * acc_ref[mine, :] + _dot_nn(p.astype(BF16), v)
            m_ref[mine, :] = m_new

        _sweep(qi, ratio, bk, step)
        o_ref[...] = (acc_ref[...] / l_ref[...]).astype(o_ref.dtype)
        lse_ref[...] = m_ref[...] + jnp.log(l_ref[...])

    blk = pl.BlockSpec((bq, LANES), lambda h, i: (i, h))
    return pl.pallas_call(
        body, name=name, grid=(heads, nq),
        in_specs=[blk,
                  pl.BlockSpec((s, LANES), lambda h, i: (0, h)),
                  pl.BlockSpec((s, LANES), lambda h, i: (0, heads + h))],
        out_specs=[blk, blk],
        out_shape=[jax.ShapeDtypeStruct((s, heads * LANES), BF16), jax.ShapeDtypeStruct((s, heads * LANES), F32)],
        scratch_shapes=[pltpu.VMEM((bq, LANES), F32), pltpu.VMEM((bq, LANES), F32), pltpu.VMEM((bq, LANES), F32)],
        compiler_params=pltpu.CompilerParams(dimension_semantics=("parallel", "arbitrary"),
                                             vmem_limit_bytes=VMEM_LIMIT),
    )(q, kv, kv)


def _mla_bwd(q, kv, o, do, lse, cos_t, sin_t, dkv_init, heads, *, name):
    s = q.shape[0]
    bq, bk, nq, ratio = _att_blocks(s)
    reps = (1, bk // LANES)
    has_init = dkv_init is not None

    def body(*refs):
        q_ref, k_ref, v_ref, o_ref, do_ref, lse_ref, c_ref, s_ref = refs[:8]
        ki_ref, vi_ref = (refs[8], refs[9]) if has_init else (None, None)
        dq_ref, dk_ref, dv_ref, dq_acc, dk_acc, dv_acc = refs[-6:]
        qi = pl.program_id(1)

        @pl.when(qi == 0)
        def _():
            if has_init:
                dk_acc[...] = ki_ref[...].astype(F32).T
                dv_acc[...] = vi_ref[...].astype(F32).T
            else:
                dk_acc[...] = jnp.zeros_like(dk_acc)
                dv_acc[...] = jnp.zeros_like(dv_acc)

        qv = q_ref[...]
        do = do_ref[...]
        q_t, do_t = qv.T, do.T
        delta = jnp.sum(do.astype(F32) * o_ref[...].astype(F32), axis=-1, keepdims=True)
        lse_wide = jnp.tile(lse_ref[...], reps)
        dq_acc[...] = jnp.zeros_like(dq_acc)

        def step(kb, masked, r0, r1):
            rows = pl.ds(pl.multiple_of(kb * bk, bk), bk)
            k, v = k_ref[rows, :], v_ref[rows, :]
            qs, dos = qv[r0:r1], do[r0:r1]
            p = jnp.exp(_dot_nt(qs, k) - lse_wide[r0:r1])
            if masked:
                p = jnp.where(_chunk_allowed(qi, kb, bq, bk, r0, r1), p, 0.0)
            ds = (p * (_dot_nt(dos, v) - delta[r0:r1])).astype(BF16)
            dq_acc[pl.ds(r0, r1 - r0), :] += _dot_nn(ds, k)
            dk_acc[:, rows] += _dot_nn(q_t[:, r0:r1], ds)
            dv_acc[:, rows] += _dot_nn(do_t[:, r0:r1], p.astype(BF16))

        _sweep(qi, ratio, bk, step)
        dq_ref[...] = _rope_slab_bwd(dq_acc[...] * MLA_SCALE, c_ref[...], s_ref[...]).astype(dq_ref.dtype)

        @pl.when(qi == nq - 1)
        def _():
            dk_ref[...] = dk_acc[...].T.astype(dk_ref.dtype)
            dv_ref[...] = dv_acc[...].T.astype(dv_ref.dtype)

    blk = pl.BlockSpec((bq, LANES), lambda h, i: (i, h))
    tab = pl.BlockSpec((bq, LANES), lambda h, i: (i, 0))
    k_full = pl.BlockSpec((s, LANES), lambda h, i: (0, h))
    v_full = pl.BlockSpec((s, LANES), lambda h, i: (0, heads + h))
    shape = jax.ShapeDtypeStruct((s, heads * LANES), BF16)
    ins = [q, kv, kv, o, do, lse, cos_t, sin_t] + ([dkv_init, dkv_init] if has_init else [])
    dq, dk, dv = pl.pallas_call(
        body, name=name, grid=(heads, nq),
        in_specs=[blk, k_full, v_full, blk, blk, blk, tab, tab] + ([k_full, v_full] if has_init else []),
        out_specs=[blk, k_full, k_full],
        out_shape=[shape, shape, shape],
        scratch_shapes=[pltpu.VMEM((bq, LANES), F32), pltpu.VMEM((LANES, s), F32), pltpu.VMEM((LANES, s), F32)],
        compiler_params=pltpu.CompilerParams(dimension_semantics=("arbitrary", "arbitrary"),
                                             vmem_limit_bytes=VMEM_LIMIT),
    )(*ins)
    return dq, jnp.concatenate([dk, dv], axis=1)


def _pad_last(a, width):
    return jnp.pad(a, [(0, 0)] * (a.ndim - 1) + [(0, width - a.shape[-1])])


def _pad_qkv(w, heads):
    d = w.shape[0]
    return _pad_last(w.reshape(d, 3 * heads, SB_HEAD_DIM), LANES).reshape(d, 3 * heads * LANES)


def _unpad_qkv(g, heads):
    d = g.shape[0]
    return g.reshape(d, 3 * heads, LANES)[:, :, :SB_HEAD_DIM].reshape(d, 3 * heads * SB_HEAD_DIM)


def _pad_o(w, heads):
    d = w.shape[1]
    w = w.reshape(heads, SB_HEAD_DIM, d)
    return jnp.pad(w, [(0, 0), (0, LANES - SB_HEAD_DIM), (0, 0)]).reshape(heads * LANES, d)


def _unpad_o(g, heads):
    d = g.shape[1]
    return g.reshape(heads, LANES, d)[:, :SB_HEAD_DIM, :].reshape(heads * SB_HEAD_DIM, d)


def _pad_uq(w, heads):
    r = w.shape[0]
    return _pad_last(w.reshape(r, heads, MLA_NOPE + MLA_ROPE), LANES).reshape(r, heads * LANES)


def _unpad_uq(g, heads):
    r = g.shape[0]
    return g.reshape(r, heads, LANES)[:, :, :MLA_NOPE + MLA_ROPE].reshape(r, heads * (MLA_NOPE + MLA_ROPE))


def _pad_dkv(w):
    d = w.shape[0]
    rope = jnp.zeros((d, LANES), w.dtype).at[:, ROPE_LO:ROPE_LO + MLA_ROPE].set(w[:, MLA_KV_RANK:])
    return jnp.concatenate([w[:, :MLA_KV_RANK], rope], axis=1)


def _unpad_dkv(g):
    return jnp.concatenate([g[:, :MLA_KV_RANK], g[:, MLA_KV_RANK + ROPE_LO:MLA_KV_RANK + ROPE_LO + MLA_ROPE]], axis=1)


def _pad_ukv(w, heads):
    w = w.reshape(MLA_KV_RANK, heads, 2, MLA_NOPE)
    k_part = _pad_last(w[:, :, 0, :], LANES).reshape(MLA_KV_RANK, heads * LANES)
    v_part = _pad_last(w[:, :, 1, :], LANES).reshape(MLA_KV_RANK, heads * LANES)
    lane = jnp.arange(LANES)
    place = ((lane[:, None] == lane[None, :]) & (lane[:, None] >= ROPE_LO) & (lane[:, None] < ROPE_LO + MLA_ROPE))
    place = jnp.tile(place.astype(w.dtype), (1, heads))
    top = jnp.concatenate([k_part, v_part], axis=1)
    bottom = jnp.concatenate([place, jnp.zeros_like(place)], axis=1)
    return jnp.concatenate([top, bottom], axis=0)


def _unpad_ukv(g, heads):
    g = g[:MLA_KV_RANK]
    k_part = g[:, :heads * LANES].reshape(MLA_KV_RANK, heads, LANES)[:, :, :MLA_NOPE]
    v_part = g[:, heads * LANES:].reshape(MLA_KV_RANK, heads, LANES)[:, :, :MLA_V]
    return jnp.stack([k_part, v_part], axis=2).reshape(MLA_KV_RANK, heads * (MLA_NOPE + MLA_V))


def _rope_tables(positions):
    inv_freq = ROPE_THETA ** (-jnp.arange(0, MLA_ROPE, 2, dtype=F32) / MLA_ROPE)
    ang = positions.astype(F32)[:, None] * inv_freq
    cos, sin = jnp.cos(ang), jnp.sin(ang)
    s = positions.shape[0]
    cos_t = jnp.ones((s, LANES), F32).at[:, ROPE_LO:ROPE_LO + MLA_ROPE].set(jnp.concatenate([cos, cos], axis=1))
    sin_t = jnp.zeros((s, LANES), F32).at[:, ROPE_LO:ROPE_LO + MLA_ROPE].set(jnp.concatenate([-sin, sin], axis=1))
    return cos_t, sin_t


def _local_step(x, positions, target, qkv_w0, norms, rest_weights, ride=None, early_reduce=None):
    s, d = x.shape
    heads = d // SB_HEAD_DIM
    cos_t, sin_t = _rope_tables(positions)

    h_first = _rms_fwd(x, norms["attn_norm"][0], name="l0_attn_norm")
    qkv_first = _mm(h_first, _pad_qkv(qkv_w0, heads), name="l0_qkv")
    if ride is None:
        o_first, gathered = _sb_fwd(qkv_first, heads, name="l0_sb_fwd"), None
    else:
        o_first, gathered = _sb_fwd(qkv_first, heads, name="l0_sb_fwd", ride=ride)
    w = rest_weights(gathered)
    n_a = w["sb_w_qkv"].shape[0]
    n_b = w["mla_w_dq"].shape[0]
    depth = n_a + n_b

    wqkv = [_pad_qkv(w["sb_w_qkv"][l], heads) for l in range(n_a)]
    wo_a = [_pad_o(w["sb_w_o"][l], heads) for l in range(n_a)]
    wdkv = _pad_dkv(w["mla_w_dkv"])
    wkv = _pad_ukv(w["mla_w_ukv"], heads)
    wdq = [w["mla_w_dq"][j] for j in range(n_b)]
    wuq = [_pad_uq(w["mla_w_uq"][j], heads) for j in range(n_b)]
    wo_b = [_pad_o(w["mla_w_o"][j], heads) for j in range(n_b)]
    w1 = [w["mlp_w1"][l] for l in range(depth)]
    w2 = [w["mlp_w2"][l] for l in range(depth)]

    saved = []
    kv_saved = None
    kv = None
    for l in range(depth):
        t = f"l{l}"
        sv = {"x_in": x}
        h = h_first if l == 0 else _rms_fwd(x, norms["attn_norm"][l], name=f"{t}_attn_norm")
        sv["h"] = h
        if l < n_a:
            if l == 0:
                qkv, o = qkv_first, o_first
            else:
                qkv = _mm(h, wqkv[l], name=f"{t}_qkv")
                o = _sb_fwd(qkv, heads, name=f"{t}_sb_fwd")
            sv["qkv"], sv["o"] = qkv, o
            x = _mm(o, wo_a[l], name=f"{t}_attn_out", epilogue=_epi_add, extras=[(x, "tile")], out_dtypes=(F32,))
        else:
            j = l - n_a
            if j == 0:
                hk = _rms_fwd(x, norms["kv_norm"], name="kv_norm")
                down = _mm(hk, wdkv, name="kv_down", out_dtypes=(F32,))
                cat = _kv_prep(down, norms["mla_kv_lat_norm"], cos_t, sin_t, name="kv_prep")
                kv = _mm(cat, wkv, name="kv_up")
                kv_saved = {"x_in": x, "hk": hk, "down": down, "cat": cat}
            cq0 = _mm(h, wdq[j], name=f"{t}_q_down", out_dtypes=(F32,))
            cq = _rms_fwd(cq0, norms["mla_q_lat_norm"][j], name=f"{t}_q_lat_norm")
            q = _mm(cq, wuq[j], name=f"{t}_q_up", epilogue=_epi_rope_heads, extras=[(cos_t, "row"), (sin_t, "row")])
            o, lse = _mla_fwd(q, kv, heads, name=f"{t}_mla_fwd")
            sv.update(cq0=cq0, cq=cq, q=q, o=o, lse=lse)
            x = _mm(o, wo_b[j], name=f"{t}_attn_out", epilogue=_epi_add, extras=[(x, "tile")], out_dtypes=(F32,))
        sv["x_mid"] = x
        h2 = _rms_fwd(x, norms["mlp_norm"][l], name=f"{t}_mlp_norm")
        u, a = _mm(h2, w1[l], name=f"{t}_mlp_up", epilogue=_epi_relu2, out_dtypes=(BF16, BF16))
        sv.update(h2=h2, u=u, a=a)
        x = _mm(a, w2[l], name=f"{t}_mlp_down", epilogue=_epi_add, extras=[(x, "tile")], out_dtypes=(F32,))
        saved.append(sv)

    loss_slab, dx, dxb, dg_final = _loss_bwd(x, norms["final_norm"], target, name="loss")
    loss = loss_slab[0, 0]

    g_attn_norm, g_mlp_norm = [None] * depth, [None] * depth
    g_qkv, g_o_a = [None] * n_a, [None] * n_a
    g_dq, g_uq, g_o_b, g_qlat = [None] * n_b, [None] * n_b, [None] * n_b, [None] * n_b
    g_w1, g_w2 = [None] * depth, [None] * depth
    dkv = None
    g_kv_norm = g_kv_lat = g_dkv = g_ukv = None
    early_parts = early_got = None

    for l in reversed(range(depth)):
        t = f"l{l}"
        sv = saved[l]
        du = _mm(dxb, w2[l], name=f"{t}_mlp_down_dx", dims="nt", epilogue=_epi_relu2_grad, extras=[(sv["u"], "tile")])
        g_w2[l] = _mm(sv["a"], dxb, name=f"{t}_mlp_down_dw", dims="tn", out_dtypes=(F32,))
        g_w1[l] = _mm(sv["h2"], du, name=f"{t}_mlp_up_dw", dims="tn", out_dtypes=(F32,))
        dh2 = _mm(du, w1[l], name=f"{t}_mlp_up_dx", dims="nt", out_dtypes=(F32,))
        dx, dxb, g_mlp_norm[l] = _rms_bwd(sv["x_mid"], norms["mlp_norm"][l], dh2, dx, name=f"{t}_mlp_norm_bwd")
        if l < n_a:
            do = _mm(dxb, wo_a[l], name=f"{t}_attn_out_dx", dims="nt")
            g_o_a[l] = _unpad_o(_mm(sv["o"], dxb, name=f"{t}_attn_out_dw", dims="tn", out_dtypes=(F32,)), heads)
            if l == 0 and early_reduce is not None:
                early_parts = early_reduce({
                    "sb_w_qkv": g_qkv[1:], "sb_w_o": g_o_a, "mla_w_dkv": g_dkv, "mla_w_ukv": g_ukv, "mla_w_dq": g_dq,
                    "mla_w_uq": g_uq, "mla_w_o": g_o_b, "mlp_w1": g_w1, "mlp_w2": g_w2})
                dq, dk, dv, early_got = _sb_bwd(sv["qkv"], sv["o"], do, heads, name=f"{t}_sb_bwd", ride=early_parts)
            else:
                dq, dk, dv = _sb_bwd(sv["qkv"], sv["o"], do, heads, name=f"{t}_sb_bwd")
            dqkv = jnp.concatenate([dq, dk, dv], axis=1)
            g_qkv[l] = _unpad_qkv(_mm(sv["h"], dqkv, name=f"{t}_qkv_dw", dims="tn", out_dtypes=(F32,)), heads)
            dh = _mm(dqkv, wqkv[l], name=f"{t}_qkv_dx", dims="nt", out_dtypes=(F32,))
        else:
            j = l - n_a
            do = _mm(dxb, wo_b[j], name=f"{t}_attn_out_dx", dims="nt")
            g_o_b[j] = _unpad_o(_mm(sv["o"], dxb, name=f"{t}_attn_out_dw", dims="tn", out_dtypes=(F32,)), heads)
            dq, dkv = _mla_bwd(sv["q"], kv, sv["o"], do, sv["lse"], cos_t, sin_t, dkv, heads, name=f"{t}_mla_bwd")
            g_uq[j] = _unpad_uq(_mm(sv["cq"], dq, name=f"{t}_q_up_dw", dims="tn", out_dtypes=(F32,)), heads)
            dcq = _mm(dq, wuq[j], name=f"{t}_q_up_dx", dims="nt", out_dtypes=(F32,))
            _, dcq0, g_qlat[j] = _rms_bwd(sv["cq0"], norms["mla_q_lat_norm"][j], dcq, None, name=f"{t}_q_lat_norm_bwd")
            g_dq[j] = _mm(sv["h"], dcq0, name=f"{t}_q_down_dw", dims="tn", out_dtypes=(F32,))
            dh = _mm(dcq0, wdq[j], name=f"{t}_q_down_dx", dims="nt", out_dtypes=(F32,))
        dx, dxb, g_attn_norm[l] = _rms_bwd(sv["x_in"], norms["attn_norm"][l], dh, dx, name=f"{t}_attn_norm_bwd",
                                           lead_axis=(l == 0))
        if l == n_a:
            ks = kv_saved
            dcat = _mm(dkv, wkv, name="kv_up_dx", dims="nt", out_dtypes=(F32,))
            g_ukv = _unpad_ukv(_mm(ks["cat"], dkv, name="kv_up_dw", dims="tn", out_dtypes=(F32,)), heads)
            ddown, g_kv_lat = _kv_prep_bwd(ks["down"], norms["mla_kv_lat_norm"], cos_t, sin_t, dcat, name="kv_prep_bwd")
            g_dkv = _unpad_dkv(_mm(ks["hk"], ddown, name="kv_down_dw", dims="tn", out_dtypes=(F32,)))
            dhk = _mm(ddown, wdkv, name="kv_down_dx", dims="nt", out_dtypes=(F32,))
            dx, dxb, g_kv_norm = _rms_bwd(ks["x_in"], norms["kv_norm"], dhk, dx, name="kv_norm_bwd")

    grads = {
        "attn_norm": jnp.concatenate(g_attn_norm, axis=0), "mlp_norm": jnp.concatenate(g_mlp_norm, axis=0),
        "sb_w_qkv": g_qkv, "sb_w_o": g_o_a,
        "kv_norm": g_kv_norm[0], "mla_w_dkv": g_dkv, "mla_kv_lat_norm": g_kv_lat[0], "mla_w_ukv": g_ukv,
        "mla_w_dq": g_dq, "mla_q_lat_norm": jnp.concatenate(g_qlat, axis=0),
        "mla_w_uq": g_uq, "mla_w_o": g_o_b,
        "mlp_w1": g_w1, "mlp_w2": g_w2, "final_norm": dg_final[0],
    }
    return loss, dx, grads, early_parts, early_got


def _flat_rows(n_elems):
    per_block = FLAT_COLS * FLAT_ROW_BLOCK * 2
    return -(-n_elems // per_block) * FLAT_ROW_BLOCK * 2


def _row_blocks(arrays, dtype):
    for a in arrays:
        assert a.size % FLAT_COLS == 0, a.shape
    blocks = [a.astype(dtype).reshape(-1, FLAT_COLS) for a in arrays]
    used = sum(b.shape[0] for b in blocks)
    rows = _flat_rows(used * FLAT_COLS)
    return blocks + [jnp.zeros((rows - used, FLAT_COLS), dtype)], rows


def _pack(arrays, dtype):
    blocks, _ = _row_blocks(arrays, dtype)
    return jnp.concatenate(blocks, axis=0)


def _pack_chips(per_chip, dtype):
    blocks, rows = [], 0
    for arrays in per_chip:
        chip_blocks, rows = _row_blocks(arrays, dtype)
        blocks += chip_blocks
    return jnp.concatenate(blocks, axis=0).reshape(len(per_chip), rows, FLAT_COLS)


def _unpack(flat, shapes):
    out, row = [], 0
    for shp in shapes:
        n = 1
        for v in shp:
            n *= v
        out.append(flat[row:row + n // FLAT_COLS].reshape(shp))
        row += n // FLAT_COLS
    return out


def _pack_small(arrays):
    rows = []
    for a in arrays:
        a = a.reshape(-1, a.shape[-1]) if a.shape[-1] == FLAT_COLS else a.reshape(1, -1)
        rows.append(_pad_last(a, FLAT_COLS))
    flat = jnp.concatenate(rows, axis=0)
    return jnp.pad(flat, [(0, -flat.shape[0] % 8), (0, 0)])


def _unpack_small(flat, shapes):
    out, row = [], 0
    for shp in shapes:
        if shp[-1] == FLAT_COLS:
            n = 1
            for v in shp[:-1]:
                n *= v
            out.append(flat[row:row + n].reshape(shp))
            row += n
        else:
            n = 1
            for v in shp:
                n *= v
            out.append(flat[row, :n].reshape(shp))
            row += 1
    return out


def _other_chips(x, y):
    return [(1 - x, y), (x, 1 - y), (1 - x, 1 - y)]


def _all_gather_chips(flat, *, name):
    rows, cols = flat.shape

    def body(x_ref, out_ref, *sems):
        _all_gather_start(x_ref, out_ref, sems)
        _all_gather_finish(x_ref, out_ref, sems)

    return pl.pallas_call(
        body, name=name,
        in_specs=[pl.BlockSpec(memory_space=pltpu.HBM)],
        out_specs=pl.BlockSpec(memory_space=pltpu.HBM),
        out_shape=jax.ShapeDtypeStruct((N_CHIPS, rows, cols), flat.dtype),
        scratch_shapes=_all_gather_sems(),
        compiler_params=pltpu.CompilerParams(has_side_effects=True),
    )(flat)


def _all_gather_sems():
    return [pltpu.SemaphoreType.DMA((3,)), pltpu.SemaphoreType.DMA((3,)), pltpu.SemaphoreType.DMA((3,)),
            pltpu.SemaphoreType.DMA((3,)), pltpu.SemaphoreType.DMA, pltpu.SemaphoreType.DMA]


def _all_gather_copies(x_ref, out_ref, sems, finishing):
    send_sems, recv_sems, pass_send_sems, pass_recv_sems, own_send_sem, own_recv_sem = sems
    x, y, c = lax.axis_index("x"), lax.axis_index("y"), lax.axis_index("c")
    me = 2 * x + y
    my_rows, sib_rows = _half_rows(x_ref.shape[0])

    def copy(src, dst, send_sem, recv_sem, to):
        return pltpu.make_async_remote_copy(src_ref=src, dst_ref=dst, send_sem=send_sem, recv_sem=recv_sem,
                                            device_id=to, device_id_type=MESH)

    own = copy(x_ref, out_ref.at[me], own_send_sem, own_recv_sem, _sibling())
    to_chips, landed, pass_on, passed = [], [], [], []
    for k, (px, py) in enumerate(_other_chips(x, y)):
        to_chips.append(copy(x_ref.at[my_rows, :], out_ref.at[me, my_rows, :], send_sems.at[k], recv_sems.at[k],
                             (px, py, c)))
        if finishing:
            mine, theirs = out_ref.at[2 * px + py, my_rows, :], out_ref.at[2 * px + py, sib_rows, :]
            landed.append(copy(mine, mine, send_sems.at[k], recv_sems.at[k], (px, py, c)))
            pass_on.append(copy(mine, mine, pass_send_sems.at[k], pass_recv_sems.at[k], _sibling()))
            passed.append(copy(theirs, theirs, pass_send_sems.at[k], pass_recv_sems.at[k], _sibling()))
    return own, to_chips, landed, pass_on, passed


def _all_gather_start(x_ref, out_ref, sems):
    own, to_chips, _, _, _ = _all_gather_copies(x_ref, out_ref, sems, finishing=False)
    own.start()
    for cp in to_chips:
        cp.start()


def _all_gather_finish(x_ref, out_ref, sems):
    own, to_chips, landed, pass_on, passed = _all_gather_copies(x_ref, out_ref, sems, finishing=True)
    for k in range(len(landed)):
        landed[k].wait_recv()
        pass_on[k].start()
    for cp in passed:
        cp.wait_recv()
    own.wait_recv()
    for cp in [own] + to_chips + pass_on:
        cp.wait_send()


def _exchange_chips(parts, *, name):
    def body(g_ref, out_ref, *sems):
        _exchange_start(g_ref, out_ref, sems)
        _exchange_finish(g_ref, out_ref, sems)

    return pl.pallas_call(
        body, name=name,
        in_specs=[pl.BlockSpec(memory_space=pltpu.HBM)],
        out_specs=pl.BlockSpec(memory_space=pltpu.HBM),
        out_shape=jax.ShapeDtypeStruct(parts.shape, parts.dtype),
        scratch_shapes=_exchange_sems(),
        compiler_params=pltpu.CompilerParams(has_side_effects=True),
    )(parts)


def _exchange_sems():
    return [pltpu.SemaphoreType.DMA((3,)), pltpu.SemaphoreType.DMA((3,))]


def _exchange_copies(g_ref, out_ref, sems, receiving):
    send_sems, recv_sems = sems
    x, y, c = lax.axis_index("x"), lax.axis_index("y"), lax.axis_index("c")
    me = 2 * x + y
    copies = []
    for k, (px, py) in enumerate(_other_chips(x, y)):
        src, dst = (g_ref.at[me], out_ref.at[2 * px + py]) if receiving else (g_ref.at[2 * px + py], out_ref.at[me])
        copies.append(pltpu.make_async_remote_copy(src_ref=src, dst_ref=dst, send_sem=send_sems.at[k],
                                                   recv_sem=recv_sems.at[k], device_id=(px, py, c),
                                                   device_id_type=MESH))
    return copies


def _exchange_start(g_ref, out_ref, sems):
    for cp in _exchange_copies(g_ref, out_ref, sems, receiving=False):
        cp.start()


def _exchange_finish(g_ref, out_ref, sems):
    for cp in _exchange_copies(g_ref, out_ref, sems, receiving=True):
        cp.wait_recv()
    for cp in _exchange_copies(g_ref, out_ref, sems, receiving=False):
        cp.wait_send()


def _my_chip():
    return 2 * lax.axis_index("x") + lax.axis_index("y")


def _half_rows(rows):
    c = lax.axis_index("c")
    half = rows // 2
    return pl.ds(pl.multiple_of(c * half, 8), half), pl.ds(pl.multiple_of((1 - c) * half, 8), half)


def _sibling():
    return (lax.axis_index("x"), lax.axis_index("y"), 1 - lax.axis_index("c"))


def _pair_exchange(parts, *, name):
    n, rows, cols = parts.shape

    def body(p_ref, theirs_ref, send_sem, recv_sem):
        _, sib_rows = _half_rows(rows)
        cp = pltpu.make_async_remote_copy(src_ref=p_ref.at[:, sib_rows, :], dst_ref=theirs_ref, send_sem=send_sem,
                                          recv_sem=recv_sem, device_id=_sibling(), device_id_type=MESH)
        cp.start()
        cp.wait()

    half = rows // 2
    theirs = pl.pallas_call(
        body, name=name,
        in_specs=[pl.BlockSpec(memory_space=pltpu.HBM)],
        out_specs=pl.BlockSpec(memory_space=pltpu.HBM),
        out_shape=jax.ShapeDtypeStruct((n, half, cols), parts.dtype),
        scratch_shapes=[pltpu.SemaphoreType.DMA, pltpu.SemaphoreType.DMA],
        compiler_params=pltpu.CompilerParams(has_side_effects=True),
    )(parts)
    mine = lax.dynamic_slice_in_dim(parts, lax.axis_index("c") * half, half, axis=1)
    return mine, theirs


def _pair_sum(mine, theirs, *, name):
    n, rows, cols = mine.shape

    def body(a_ref, b_ref, o_ref):
        o_ref[...] = (a_ref[...].astype(F32) + b_ref[...].astype(F32)).astype(o_ref.dtype)

    blk = pl.BlockSpec((n, FLAT_ROW_BLOCK, cols), lambda i: (0, i, 0))
    return pl.pallas_call(
        body, name=name, grid=(rows // FLAT_ROW_BLOCK,),
        in_specs=[blk, blk], out_specs=blk, out_shape=jax.ShapeDtypeStruct(mine.shape, mine.dtype),
        compiler_params=pltpu.CompilerParams(dimension_semantics=("parallel",), vmem_limit_bytes=VMEM_LIMIT),
    )(mine, theirs)


def _sum_chips(received, own, *, name):
    _, rows, cols = received.shape

    def body(p_ref, own_ref, o_ref):
        me = 2 * lax.axis_index("x") + lax.axis_index("y")
        slot = [jnp.where(me == j, own_ref[j], p_ref[j]).astype(F32) for j in range(N_CHIPS)]
        o_ref[...] = ((slot[0] + slot[1]) + slot[2]) + slot[3]

    blk = pl.BlockSpec((N_CHIPS, FLAT_ROW_BLOCK, cols), lambda i: (0, i, 0))
    return pl.pallas_call(
        body, name=name, grid=(rows // FLAT_ROW_BLOCK,),
        in_specs=[blk, blk],
        out_specs=pl.BlockSpec((FLAT_ROW_BLOCK, cols), lambda i: (i, 0)),
        out_shape=jax.ShapeDtypeStruct((rows, cols), F32),
        compiler_params=pltpu.CompilerParams(dimension_semantics=("parallel",), vmem_limit_bytes=VMEM_LIMIT),
    )(received, own)


def _join_cores(half, *, name):
    rows2, cols = half.shape

    def body(h_ref, out_ref, send_sem, recv_sem):
        my_rows, sib_rows = _half_rows(2 * rows2)
        cp = pltpu.make_async_remote_copy(src_ref=h_ref, dst_ref=out_ref.at[my_rows, :], send_sem=send_sem,
                                          recv_sem=recv_sem, device_id=_sibling(), device_id_type=MESH)
        cp.start()
        cp.wait_send()
        pltpu.make_async_remote_copy(src_ref=h_ref, dst_ref=out_ref.at[sib_rows, :], send_sem=send_sem,
                                     recv_sem=recv_sem, device_id=_sibling(), device_id_type=MESH).wait_recv()

    out = pl.pallas_call(
        body, name=name,
        in_specs=[pl.BlockSpec(memory_space=pltpu.HBM)],
        out_specs=pl.BlockSpec(memory_space=pltpu.HBM),
        out_shape=jax.ShapeDtypeStruct((2 * rows2, cols), half.dtype),
        scratch_shapes=[pltpu.SemaphoreType.DMA, pltpu.SemaphoreType.DMA],
        compiler_params=pltpu.CompilerParams(has_side_effects=True),
    )(half)
    return lax.dynamic_update_slice_in_dim(out, half, lax.axis_index("c") * rows2, axis=0)


def _all_reduce_small(v, *, name):
    rows, cols = v.shape
    flips = [(fx, fy, fc) for fx in (0, 1) for fy in (0, 1) for fc in (0, 1)][1:]

    def body(v_ref, out_ref, gath_ref, send_sems, recv_sems):
        x, y, c = lax.axis_index("x"), lax.axis_index("y"), lax.axis_index("c")
        me = 4 * x + 2 * y + c
        gath_ref[me] = v_ref[...]
        peers = [((1 - x) if fx else x, (1 - y) if fy else y, (1 - c) if fc else c) for fx, fy, fc in flips]
        sends = []
        for k, peer in enumerate(peers):
            cp = pltpu.make_async_remote_copy(src_ref=v_ref, dst_ref=gath_ref.at[me], send_sem=send_sems.at[k],
                                              recv_sem=recv_sems.at[k], device_id=peer, device_id_type=MESH)
            cp.start()
            sends.append(cp)
        for k, (px, py, pc) in enumerate(peers):
            pltpu.make_async_remote_copy(src_ref=v_ref, dst_ref=gath_ref.at[4 * px + 2 * py + pc],
                                         send_sem=send_sems.at[k], recv_sem=recv_sems.at[k],
                                         device_id=(px, py, pc), device_id_type=MESH).wait_recv()
        for cp in sends:
            cp.wait_send()
        total = gath_ref[0]
        for k in range(1, 8):
            total = total + gath_ref[k]
        out_ref[...] = total

    total, _ = pl.pallas_call(
        body, name=name,
        in_specs=[pl.BlockSpec(memory_space=pltpu.VMEM)],
        out_specs=[pl.BlockSpec(memory_space=pltpu.VMEM), pl.BlockSpec(memory_space=pltpu.VMEM)],
        out_shape=[jax.ShapeDtypeStruct((rows, cols), v.dtype), jax.ShapeDtypeStruct((8, rows, cols), v.dtype)],
        scratch_shapes=[pltpu.SemaphoreType.DMA((7,)), pltpu.SemaphoreType.DMA((7,))],
        compiler_params=pltpu.CompilerParams(has_side_effects=True),
    )(v)
    return total


def _adamw(w, g, m, v, *, name):
    shape = w.shape
    cols = shape[-1]
    w2, g2, m2, v2 = (a.reshape(-1, cols) for a in (w, g, m, v))
    rows = w2.shape[0]
    br = _pick_rows(rows, FLAT_ROW_BLOCK)

    def body(w_ref, g_ref, m_ref, v_ref, d_out, m_out, v_out):
        gv = g_ref[...]
        m_new = ADAM_B1 * m_ref[...] + (1.0 - ADAM_B1) * gv
        v_new = ADAM_B2 * v_ref[...] + (1.0 - ADAM_B2) * jnp.square(gv)
        m_hat = m_new / (1.0 - ADAM_B1 ** ADAM_STEP)
        v_hat = v_new / (1.0 - ADAM_B2 ** ADAM_STEP)
        d_out[...] = -ADAM_LR * (m_hat / (jnp.sqrt(v_hat) + ADAM_EPS) + ADAM_WD * w_ref[...])
        m_out[...] = m_new
        v_out[...] = v_new

    blk = pl.BlockSpec((br, cols), lambda i: (i, 0))
    out = jax.ShapeDtypeStruct((rows, cols), F32)
    outs = pl.pallas_call(
        body, name=name, grid=(rows // br,),
        in_specs=[blk] * 4, out_specs=[blk] * 3, out_shape=[out] * 3,
        compiler_params=pltpu.CompilerParams(dimension_semantics=("parallel",), vmem_limit_bytes=VMEM_LIMIT),
    )(w2, g2, m2, v2)
    return [o.reshape(shape) for o in outs]


def _pick_rows(rows, target):
    if rows <= target:
        return rows
    return max(b for b in range(8, target + 1, 8) if rows % b == 0)


def _assemble(gathered_shards, name, layer=False):
    return jnp.concatenate(gathered_shards, axis=SHARD_AXIS[name] - int(layer))


def _chip_shard(full, name, j):
    if isinstance(full, list):
        axis = SHARD_AXIS[name] - 1
        layers = full
    else:
        axis = SHARD_AXIS[name]
        layers = [full]
    n = layers[0].shape[axis] // N_CHIPS
    return [lax.slice_in_dim(g, j * n, (j + 1) * n, axis=axis) for g in layers]


def kernel(x, positions, attn_norm, mlp_norm, sb_w_qkv, sb_w_o, kv_norm, mla_w_dkv, mla_kv_lat_norm, mla_w_ukv, mla_w_dq, mla_q_lat_norm, mla_w_uq, mla_w_o, mlp_w1, mlp_w2, final_norm, loss_target, m_attn_norm, m_mlp_norm, m_sb_w_qkv, m_sb_w_o, m_kv_norm, m_mla_w_dkv, m_mla_kv_lat_norm, m_mla_w_ukv, m_mla_w_dq, m_mla_q_lat_norm, m_mla_w_uq, m_mla_w_o, m_mlp_w1, m_mlp_w2, m_final_norm, v_attn_norm, v_mlp_norm, v_sb_w_qkv, v_sb_w_o, v_kv_norm, v_mla_w_dkv, v_mla_kv_lat_norm, v_mla_w_ukv, v_mla_w_dq, v_mla_q_lat_norm, v_mla_w_uq, v_mla_w_o, v_mlp_w1, v_mlp_w2, v_final_norm):
    weights = dict(attn_norm=attn_norm, mlp_norm=mlp_norm, sb_w_qkv=sb_w_qkv, sb_w_o=sb_w_o, kv_norm=kv_norm,
                   mla_w_dkv=mla_w_dkv, mla_kv_lat_norm=mla_kv_lat_norm, mla_w_ukv=mla_w_ukv, mla_w_dq=mla_w_dq,
                   mla_q_lat_norm=mla_q_lat_norm, mla_w_uq=mla_w_uq, mla_w_o=mla_w_o, mlp_w1=mlp_w1, mlp_w2=mlp_w2,
                   final_norm=final_norm)
    m_in = dict(attn_norm=m_attn_norm, mlp_norm=m_mlp_norm, sb_w_qkv=m_sb_w_qkv, sb_w_o=m_sb_w_o, kv_norm=m_kv_norm,
                mla_w_dkv=m_mla_w_dkv, mla_kv_lat_norm=m_mla_kv_lat_norm, mla_w_ukv=m_mla_w_ukv, mla_w_dq=m_mla_w_dq,
                mla_q_lat_norm=m_mla_q_lat_norm, mla_w_uq=m_mla_w_uq, mla_w_o=m_mla_w_o, mlp_w1=m_mlp_w1,
                mlp_w2=m_mlp_w2, final_norm=m_final_norm)
    v_in = dict(attn_norm=v_attn_norm, mlp_norm=v_mlp_norm, sb_w_qkv=v_sb_w_qkv, sb_w_o=v_sb_w_o, kv_norm=v_kv_norm,
                mla_w_dkv=v_mla_w_dkv, mla_kv_lat_norm=v_mla_kv_lat_norm, mla_w_ukv=v_mla_w_ukv, mla_w_dq=v_mla_w_dq,
                mla_q_lat_norm=v_mla_q_lat_norm, mla_w_uq=v_mla_w_uq, mla_w_o=v_mla_w_o, mlp_w1=v_mlp_w1,
                mlp_w2=v_mlp_w2, final_norm=v_final_norm)
    shard_shapes = [weights[n].shape for n in BIG_WEIGHTS]
    small_shapes = [weights[n].shape for n in SMALL_WEIGHTS]

    first_name = BIG_WEIGHTS[0]
    qkv_first, qkv_later = weights[first_name][0], weights[first_name][1:]
    gathered_first = _all_gather_chips(_pack([qkv_first], BF16), name="first_weight_all_gather")
    qkv_w0 = _assemble([_unpack(gathered_first[j], [qkv_first.shape])[0] for j in range(N_CHIPS)], first_name, layer=True)
    ride = _pack([qkv_later] + [weights[n] for n in BIG_WEIGHTS[1:]], BF16)
    ride_shapes = [qkv_later.shape] + shard_shapes[1:]

    def rest_weights(gathered):
        per_chip = [_unpack(gathered[j], ride_shapes) for j in range(N_CHIPS)]
        full = {n: _assemble([per_chip[j][i] for j in range(N_CHIPS)], n) for i, n in enumerate(BIG_WEIGHTS)}
        full[first_name] = jnp.concatenate([qkv_w0[None], full[first_name]], axis=0)
        return full

    norms = {n: weights[n] for n in SMALL_WEIGHTS}

    def chip_parts(g, tag):
        parts = _pack_chips([[piece for n in BIG_WEIGHTS if n in g for piece in _chip_shard(g[n], n, j)]
                             for j in range(N_CHIPS)], BF16)
        mine, theirs = _pair_exchange(parts, name=f"grads_pair_exchange_{tag}")
        return _pair_sum(mine, theirs, name=f"grads_pair_sum_{tag}")

    def finish(received, chip_part, tag):
        g_half = _sum_chips(received, chip_part, name=f"grads_sum_chips_{tag}")
        return _join_cores(g_half, name=f"grads_join_cores_{tag}")

    loss, dx, grads, early_parts, early_got = _local_step(
        x[0], positions[0], loss_target, qkv_w0, norms, rest_weights, ride=ride,
        early_reduce=functools.partial(chip_parts, tag="early"))
    loss = lax.psum(loss, ("x", "y", "c"))
    early_sum = finish(early_got, early_parts, "early")
    last_parts = chip_parts({first_name: grads[first_name][:1]}, "last")
    last_sum = finish(_exchange_chips(last_parts, name="grads_exchange_last"), last_parts, "last")

    out_g = dict(zip(BIG_WEIGHTS, _unpack(early_sum, ride_shapes)))
    out_g[first_name] = jnp.concatenate([_unpack(last_sum, [qkv_first.shape])[0][None], out_g[first_name]], axis=0)
    out_d, out_m, out_v = {}, {}, {}
    for n in BIG_WEIGHTS:
        out_d[n], out_m[n], out_v[n] = _adamw(weights[n], out_g[n], m_in[n], v_in[n], name=f"adamw_{n}")

    small_sum = _all_reduce_small(_pack_small([grads[n] for n in SMALL_WEIGHTS]), name="gains_all_reduce")
    sd, sm, sv = _adamw(_pack_small([weights[n] for n in SMALL_WEIGHTS]), small_sum,
                        _pack_small([m_in[n] for n in SMALL_WEIGHTS]),
                        _pack_small([v_in[n] for n in SMALL_WEIGHTS]), name="adamw_gains")
    out_g.update(zip(SMALL_WEIGHTS, _unpack_small(small_sum, small_shapes)))
    out_d.update(zip(SMALL_WEIGHTS, _unpack_small(sd, small_shapes)))
    out_m.update(zip(SMALL_WEIGHTS, _unpack_small(sm, small_shapes)))
    out_v.update(zip(SMALL_WEIGHTS, _unpack_small(sv, small_shapes)))

    return (loss, dx, *[out_g[n] for n in ALL_WEIGHTS], *[out_d[n] for n in ALL_WEIGHTS],
            *[out_m[n] for n in ALL_WEIGHTS], *[out_v[n] for n in ALL_WEIGHTS])
```

```python
import functools

import jax
import jax.numpy as jnp
from jax import lax
from jax.experimental import pallas as pl
from jax.experimental.pallas import tpu as pltpu

F32 = jnp.float32
BF16 = jnp.bfloat16

LANES = 128
SB_HEAD_DIM = 64
MLA_NOPE = 64
MLA_ROPE = 32
MLA_V = 64
MLA_Q_RANK = 384
MLA_KV_RANK = 256
CHUNK = 64
ROPE_THETA = 10000.0
NORM_EPS = 1e-6
SB_SCALE = SB_HEAD_DIM ** -0.5
MLA_SCALE = (MLA_NOPE + MLA_ROPE) ** -0.5
ROPE_LO = MLA_NOPE
ROPE_HALF = MLA_ROPE // 2
ATT_Q_BLOCK = 1024
ATT_K_BLOCK = 256
MLA_FWD_K_BLOCK = 512
NEG_BIG = -1e30
SB_DEAD_LOG = -110.0
VMEM_LIMIT = 56 * 1024 * 1024

ADAM_LR = 0.001
ADAM_B1 = 0.9
ADAM_B2 = 0.999
ADAM_EPS = 1e-08
ADAM_WD = 0.01
ADAM_STEP = 10

FLAT_COLS = 1024
FLAT_ROW_BLOCK = 256
N_CHIPS = 4
MESH = pl.DeviceIdType.MESH

BIG_WEIGHTS = ["sb_w_qkv", "sb_w_o", "mla_w_dkv", "mla_w_ukv", "mla_w_dq", "mla_w_uq", "mla_w_o", "mlp_w1", "mlp_w2"]
SHARD_AXIS = {"sb_w_qkv": 2, "sb_w_o": 1, "mla_w_dkv": 0, "mla_w_ukv": 1, "mla_w_dq": 1, "mla_w_uq": 2,
              "mla_w_o": 1, "mlp_w1": 2, "mlp_w2": 1}
SMALL_WEIGHTS = ["attn_norm", "mlp_norm", "kv_norm", "mla_kv_lat_norm", "mla_q_lat_norm", "final_norm"]
ALL_WEIGHTS = ["attn_norm", "mlp_norm", "sb_w_qkv", "sb_w_o", "kv_norm", "mla_w_dkv", "mla_kv_lat_norm", "mla_w_ukv",
               "mla_w_dq", "mla_q_lat_norm", "mla_w_uq", "mla_w_o", "mlp_w1", "mlp_w2", "final_norm"]


def _dot(a, b, dims):
    return lax.dot_general(a, b, (dims, ((), ())), preferred_element_type=F32)


def _dot_nn(a, b):
    return _dot(a, b, ((1,), (0,)))


def _dot_nt(a, b):
    return _dot(a, b, ((1,), (1,)))


def _dot_tn(a, b):
    return _dot(a, b, ((0,), (0,)))


def _pick_block(n, target):
    if n <= target:
        return n
    best = max(b for b in range(LANES, target + 1, LANES) if n % b == 0)
    return best


MM_ROWS = 512
MM_COLS = 1024
MM_DEPTH = 4096
MM_DEPTH_TN = 1024


def _mm(a, b, *, name, dims="nn", epilogue=None, extras=(), out_dtypes=(BF16,), column_sum=False):
    if dims == "nn":
        (m, k), (k2, n) = a.shape, b.shape
    elif dims == "nt":
        (m, k), (n, k2) = a.shape, b.shape
    else:
        (k, m), (k2, n) = a.shape, b.shape
    assert k == k2, (name, a.shape, b.shape)
    if dims == "tn":
        bm, bn, bk = _pick_block(m, MM_COLS), _pick_block(n, MM_COLS), _pick_block(k, MM_DEPTH_TN)
    else:
        rows = MM_ROWS if k > MM_DEPTH // 2 else 2 * MM_ROWS
        bm, bn, bk = _pick_block(m, rows), _pick_block(n, MM_COLS), _pick_block(k, MM_DEPTH)
    nk = k // bk
    if dims == "tn":
        a_spec = pl.BlockSpec((bk, bm), lambda j, i, kk: (kk, i))
    else:
        a_spec = pl.BlockSpec((bm, bk), lambda j, i, kk: (i, kk))
    if dims == "nt":
        b_spec = pl.BlockSpec((bn, bk), lambda j, i, kk: (j, kk))
    else:
        b_spec = pl.BlockSpec((bk, bn), lambda j, i, kk: (kk, j))
    extra_specs = []
    for arr, kind in extras:
        if kind == "tile":
            assert arr.shape == (m, n), (name, arr.shape)
            extra_specs.append(pl.BlockSpec((bm, bn), lambda j, i, kk: (i, j)))
        elif kind == "vec":
            assert arr.shape == (1, n), (name, arr.shape)
            extra_specs.append(pl.BlockSpec((1, bn), lambda j, i, kk: (0, j)))
        else:
            assert arr.shape == (m, LANES), (name, arr.shape)
            extra_specs.append(pl.BlockSpec((bm, LANES), lambda j, i, kk: (i, 0)))
    n_extra = len(extras)
    n_out = len(out_dtypes)
    n_sum = int(column_sum)
    dot = {"nn": _dot_nn, "nt": _dot_nt, "tn": _dot_tn}[dims]

    def body(*refs):
        a_ref, b_ref = refs[0], refs[1]
        extra_refs = refs[2:2 + n_extra]
        out_refs = refs[2 + n_extra:2 + n_extra + n_out]

        def finish(acc):
            outs = (acc,) if epilogue is None else epilogue(acc, *[r[...] for r in extra_refs])
            for o_ref, o in zip(out_refs, outs):
                o_ref[...] = o.astype(o_ref.dtype)
            if column_sum:
                sum_ref = refs[2 + n_extra + n_out]
                first_rows = pl.program_id(1) == 0

                @pl.when(first_rows)
                def _():
                    sum_ref[...] = outs[n_out]

                @pl.when(jnp.logical_not(first_rows))
                def _():
                    sum_ref[...] += outs[n_out]

        part = dot(a_ref[...].astype(BF16), b_ref[...].astype(BF16))
        if nk == 1:
            finish(part)
            return
        acc_ref = refs[-1]
        kk = pl.program_id(2)

        @pl.when(kk == 0)
        def _():
            acc_ref[...] = part

        @pl.when(kk > 0)
        def _():
            acc_ref[...] += part

        @pl.when(kk == nk - 1)
        def _():
            finish(acc_ref[...])

    outs = pl.pallas_call(
        body, name=name, grid=(n // bn, m // bm, nk),
        in_specs=[a_spec, b_spec] + extra_specs,
        out_specs=[pl.BlockSpec((bm, bn), lambda j, i, kk: (i, j)) for _ in range(n_out)]
        + [pl.BlockSpec((1, bn), lambda j, i, kk: (0, j))] * n_sum,
        out_shape=[jax.ShapeDtypeStruct((m, n), dt) for dt in out_dtypes] + [jax.ShapeDtypeStruct((1, n), F32)] * n_sum,
        scratch_shapes=[pltpu.VMEM((bm, bn), F32)] if nk > 1 else [],
        compiler_params=pltpu.CompilerParams(
            dimension_semantics=("parallel", "arbitrary" if column_sum else "parallel", "arbitrary"),
            vmem_limit_bytes=VMEM_LIMIT),
    )(a, b, *[arr for arr, _ in extras])
    return outs[0] if n_out + n_sum == 1 else outs


def _epi_add(acc, res):
    return (res + acc,)


def _epi_relu2(acc):
    r = jnp.maximum(acc, 0.0)
    return acc, r * r


def _epi_relu2_grad(acc, u):
    return (acc * (2.0 * jnp.maximum(u.astype(F32), 0.0)),)


def _rope_slab(t, cos_t, sin_t):
    lane = lax.broadcasted_iota(jnp.int32, t.shape, 1)
    partner = jnp.where(lane < ROPE_LO + ROPE_HALF, pltpu.roll(t, LANES - ROPE_HALF, 1), pltpu.roll(t, ROPE_HALF, 1))
    return t * cos_t + partner * sin_t


def _rope_slab_bwd(d, cos_t, sin_t):
    ds = d * sin_t
    lane = lax.broadcasted_iota(jnp.int32, d.shape, 1)
    partner = jnp.where(lane < ROPE_LO + ROPE_HALF, pltpu.roll(ds, LANES - ROPE_HALF, 1), pltpu.roll(ds, ROPE_HALF, 1))
    in_rope = (lane >= ROPE_LO) & (lane < ROPE_LO + MLA_ROPE)
    return d * cos_t + jnp.where(in_rope, partner, 0.0)


def _epi_rope_heads(acc, cos_t, sin_t):
    slabs = [_rope_slab(acc[:, j * LANES:(j + 1) * LANES], cos_t, sin_t) for j in range(acc.shape[1] // LANES)]
    return (jnp.concatenate(slabs, axis=1) * MLA_SCALE,)


def _row_block(s):
    return min(512, s)


def _rms_fwd(x, g, *, name):
    s, d = x.shape
    bm = _row_block(s)

    def body(x_ref, g_ref, o_ref):
        xv = x_ref[...]
        r = lax.rsqrt(jnp.mean(xv * xv, axis=-1, keepdims=True) + NORM_EPS)
        o_ref[...] = ((xv * r) * g_ref[...]).astype(o_ref.dtype)

    return pl.pallas_call(
        body, name=name, grid=(s // bm,),
        in_specs=[pl.BlockSpec((bm, d), lambda i: (i, 0)), pl.BlockSpec((1, d), lambda i: (0, 0))],
        out_specs=pl.BlockSpec((bm, d), lambda i: (i, 0)),
        out_shape=jax.ShapeDtypeStruct((s, d), BF16),
        compiler_params=pltpu.CompilerParams(dimension_semantics=("parallel",), vmem_limit_bytes=VMEM_LIMIT),
    )(x, g.reshape(1, d))


def _rms_bwd_math(xv, gv, dy):
    r = lax.rsqrt(jnp.mean(xv * xv, axis=-1, keepdims=True) + NORM_EPS)
    xhat = xv * r
    dyg = dy * gv
    mdot = jnp.mean(dyg * xhat, axis=-1, keepdims=True)
    dx = r * (dyg - xhat * mdot)
    dg = jnp.sum(dy * xhat, axis=0, keepdims=True)
    return dx, dg


def _rms_bwd(x, g, dy, dres, *, name, lead_axis=False):
    s, d = x.shape
    bm = _row_block(s)
    has_res = dres is not None

    def body(*refs):
        x_ref, g_ref, dy_ref = refs[:3]
        dres_ref = refs[3] if has_res else None
        dx_ref, dxb_ref, dg_ref = refs[-3:]
        dx, dg = _rms_bwd_math(x_ref[...], g_ref[...], dy_ref[...].astype(F32))
        if has_res:
            dx = dx + dres_ref[...]
        dx_ref[...] = dx
        dxb_ref[...] = dx.astype(BF16)

        @pl.when(pl.program_id(0) == 0)
        def _():
            dg_ref[...] = jnp.zeros_like(dg_ref)

        dg_ref[...] += dg

    row = pl.BlockSpec((bm, d), lambda i: (i, 0))
    vec = pl.BlockSpec((1, d), lambda i: (0, 0))
    ins = [x, g.reshape(1, d), dy] + ([dres] if has_res else [])
    dx_spec, dx_shape = row, (s, d)
    if lead_axis:
        dx_spec, dx_shape = pl.BlockSpec((None, bm, d), lambda i: (0, i, 0)), (1, s, d)
    return pl.pallas_call(
        body, name=name, grid=(s // bm,),
        in_specs=[row, vec, row] + ([row] if has_res else []),
        out_specs=[dx_spec, row, vec],
        out_shape=[jax.ShapeDtypeStruct(dx_shape, F32), jax.ShapeDtypeStruct((s, d), BF16),
                   jax.ShapeDtypeStruct((1, d), F32)],
        compiler_params=pltpu.CompilerParams(dimension_semantics=("arbitrary",), vmem_limit_bytes=VMEM_LIMIT),
    )(*ins)


def _epi_rms_bwd(acc, x, g, dres=None):
    dx, dg = _rms_bwd_math(x, g, acc)
    if dres is not None:
        dx = dx + dres
    return dx, dx, dg


def _mm_rms_bwd(dy_src, w, x, g, dres, *, name):
    d = x.shape[1]
    assert w.shape[0] == d and d <= MM_COLS, (name, w.shape, x.shape)
    extras = [(x, "tile"), (g.reshape(1, d), "vec")] + ([(dres, "tile")] if dres is not None else [])
    return _mm(dy_src, w, name=name, dims="nt", epilogue=_epi_rms_bwd, extras=extras, out_dtypes=(F32, BF16),
               column_sum=True)


def _loss_bwd(x, g, target, *, name):
    s, d = x.shape
    bm = _row_block(s)

    def body(x_ref, g_ref, t_ref, loss_ref, dx_ref, dxb_ref, dg_ref):
        xv, gv = x_ref[...], g_ref[...]
        r = lax.rsqrt(jnp.mean(xv * xv, axis=-1, keepdims=True) + NORM_EPS)
        err = (xv * r) * gv - t_ref[...]
        dx, dg = _rms_bwd_math(xv, gv, err * (1.0 / d))
        dx_ref[...] = dx
        dxb_ref[...] = dx.astype(BF16)

        @pl.when(pl.program_id(0) == 0)
        def _():
            dg_ref[...] = jnp.zeros_like(dg_ref)
            loss_ref[...] = jnp.zeros_like(loss_ref)

        dg_ref[...] += dg
        loss_ref[...] += jnp.sum(jnp.mean(err * err, axis=-1, keepdims=True), axis=0, keepdims=True) * 0.5

    row = pl.BlockSpec((bm, d), lambda i: (i, 0))
    vec = pl.BlockSpec((1, d), lambda i: (0, 0))
    assert target.shape == (1, s, d), target.shape
    return pl.pallas_call(
        body, name=name, grid=(s // bm,),
        in_specs=[row, vec, pl.BlockSpec((None, bm, d), lambda i: (0, i, 0))],
        out_specs=[pl.BlockSpec((8, LANES), lambda i: (0, 0)), row, row, vec],
        out_shape=[jax.ShapeDtypeStruct((8, LANES), F32), jax.ShapeDtypeStruct((s, d), F32),
                   jax.ShapeDtypeStruct((s, d), BF16), jax.ShapeDtypeStruct((1, d), F32)],
        compiler_params=pltpu.CompilerParams(dimension_semantics=("arbitrary",), vmem_limit_bytes=VMEM_LIMIT),
    )(x, g.reshape(1, d), target)


def _kv_prep(down, g, cos_t, sin_t, *, name):
    s, w = down.shape
    bm = _row_block(s)

    def body(d_ref, g_ref, c_ref, s_ref, o_ref):
        lat = d_ref[:, :MLA_KV_RANK]
        r = lax.rsqrt(jnp.mean(lat * lat, axis=-1, keepdims=True) + NORM_EPS)
        o_ref[:, :MLA_KV_RANK] = ((lat * r) * g_ref[...]).astype(BF16)
        o_ref[:, MLA_KV_RANK:] = _rope_slab(d_ref[:, MLA_KV_RANK:], c_ref[...], s_ref[...]).astype(BF16)

    row = pl.BlockSpec((bm, w), lambda i: (i, 0))
    tab = pl.BlockSpec((bm, LANES), lambda i: (i, 0))
    return pl.pallas_call(
        body, name=name, grid=(s // bm,),
        in_specs=[row, pl.BlockSpec((1, MLA_KV_RANK), lambda i: (0, 0)), tab, tab],
        out_specs=row, out_shape=jax.ShapeDtypeStruct((s, w), BF16),
        compiler_params=pltpu.CompilerParams(dimension_semantics=("parallel",), vmem_limit_bytes=VMEM_LIMIT),
    )(down, g.reshape(1, MLA_KV_RANK), cos_t, sin_t)


def _kv_prep_bwd(down, g, cos_t, sin_t, dcat, *, name):
    s, w = down.shape
    bm = _row_block(s)

    def body(d_ref, g_ref, c_ref, s_ref, dc_ref, o_ref, dg_ref):
        dlat, dg = _rms_bwd_math(d_ref[:, :MLA_KV_RANK], g_ref[...], dc_ref[:, :MLA_KV_RANK])
        o_ref[:, :MLA_KV_RANK] = dlat.astype(BF16)
        o_ref[:, MLA_KV_RANK:] = _rope_slab_bwd(dc_ref[:, MLA_KV_RANK:], c_ref[...], s_ref[...]).astype(BF16)

        @pl.when(pl.program_id(0) == 0)
        def _():
            dg_ref[...] = jnp.zeros_like(dg_ref)

        dg_ref[...] += dg

    row = pl.BlockSpec((bm, w), lambda i: (i, 0))
    tab = pl.BlockSpec((bm, LANES), lambda i: (i, 0))
    vec = pl.BlockSpec((1, MLA_KV_RANK), lambda i: (0, 0))
    return pl.pallas_call(
        body, name=name, grid=(s // bm,),
        in_specs=[row, vec, tab, tab, row],
        out_specs=[row, vec],
        out_shape=[jax.ShapeDtypeStruct((s, w), BF16), jax.ShapeDtypeStruct((1, MLA_KV_RANK), F32)],
        compiler_params=pltpu.CompilerParams(dimension_semantics=("arbitrary",), vmem_limit_bytes=VMEM_LIMIT),
    )(down, g.reshape(1, MLA_KV_RANK), cos_t, sin_t, dcat)


def _split_bf16(v):
    hi = v.astype(BF16)
    lo = (v - hi.astype(F32)).astype(BF16)
    return hi, lo


def _suffix_matrices(n):
    row = lax.broadcasted_iota(jnp.int32, (n, n), 0)
    col = lax.broadcasted_iota(jnp.int32, (n, n), 1)
    incl = (row >= col).astype(BF16)
    return (row > col).astype(BF16), jnp.concatenate([incl, incl], axis=0)


def _suffix_sum(v, matrix):
    hi, lo = _split_bf16(v)
    return _dot_nn(jnp.concatenate([hi, lo], axis=1), matrix)


def _block_positions(qi, kb, bq, bk, r0, r1):
    row = qi * bq + r0 + lax.broadcasted_iota(jnp.int32, (r1 - r0, bk), 0)
    col = kb * bk + lax.broadcasted_iota(jnp.int32, (r1 - r0, bk), 1)
    return row, col


def _att_blocks(s, key_block=ATT_K_BLOCK):
    bq, bk = min(ATT_Q_BLOCK, s), min(key_block, s)
    return bq, bk, s // bq, bq // bk


def _sweep(qi, ratio, bk, step, unroll=2, alive=None):
    bq = ratio * bk
    for d in range(ratio):
        kb, r0 = (qi + 1) * ratio - 1 - d, (ratio - 1 - d) * bk
        near = bq if alive is None else min(r0 + 2 * bk, bq)
        step(kb, True, r0, near)
        if near < bq:
            pl.when(alive(near))(functools.partial(step, kb, False, near, bq))
    unroll = unroll if ratio % unroll == 0 else 1
    trips = qi * (ratio // unroll)

    def trip(i):
        for u in range(unroll):
            kb = qi * ratio - 1 - (i * unroll + u)
            if alive is None:
                step(kb, False, 0, bq)
            else:
                step(kb, False, 0, bk)
                pl.when(alive(bk))(functools.partial(step, kb, False, bk, bq))

    if alive is None:
        lax.fori_loop(0, trips, lambda i, carry: (trip(i), carry)[1], 0)
    else:
        lax.while_loop(lambda i: jnp.logical_and(i < trips, alive(0)), lambda i: (trip(i), i + 1)[1], 0)


def _stick_left(c_ref, r0):
    return jnp.max(c_ref[r0:, :]) > SB_DEAD_LOG


def _sb_logs(q, k):
    z = _dot_nt(q, k)
    lb = jnp.minimum(z, 0.0) - jnp.log(1.0 + jnp.exp(-jnp.abs(z)))
    return lb, lb - z


def _sb_fwd(qkv, heads, *, name, ride=None):
    s = qkv.shape[0]
    bq, bk, nq, ratio = _att_blocks(s)
    riding = ride is not None

    def body(*refs):
        if riding:
            q_ref, k_ref, v_ref, w_ref, o_ref, gath_ref, acc_ref, c_ref = refs[:8]
            first = jnp.logical_and(pl.program_id(0) == 0, pl.program_id(1) == 0)
            last = jnp.logical_and(pl.program_id(0) == heads - 1, pl.program_id(1) == nq - 1)
            pl.when(first)(functools.partial(_all_gather_start, w_ref, gath_ref, refs[8:]))
        else:
            q_ref, k_ref, v_ref, o_ref, acc_ref, c_ref = refs
        qi = pl.program_id(1)
        q = q_ref[...] * SB_SCALE
        m_strict, _ = _suffix_matrices(bk)
        acc_ref[...] = jnp.zeros_like(acc_ref)
        c_ref[...] = jnp.zeros_like(c_ref)

        def step(kb, masked, r0, r1):
            rows = pl.ds(pl.multiple_of(kb * bk, bk), bk)
            mine = pl.ds(r0, r1 - r0)
            k, v = k_ref[rows, :], v_ref[rows, :]
            lb, lk = _sb_logs(q[r0:r1], k)
            if masked:
                row, col = _block_positions(qi, kb, bq, bk, r0, r1)
                causal = col < row
                lk = jnp.where(causal, lk, 0.0)
            c = c_ref[mine, :]
            w = jnp.exp(lb + _dot_nn(lk.astype(BF16), m_strict) + jnp.tile(c, (1, bk // LANES)))
            if masked:
                w = jnp.where(causal, w, 0.0)
            acc_ref[mine, :] += _dot_nn(w.astype(BF16), v)
            c_ref[mine, :] = c + jnp.sum(lk, axis=-1, keepdims=True)

        _sweep(qi, ratio, bk, step, unroll=1, alive=functools.partial(_stick_left, c_ref))
        o_ref[...] = acc_ref[...].astype(o_ref.dtype)
        if riding:
            pl.when(last)(functools.partial(_all_gather_finish, w_ref, gath_ref, refs[8:]))

    hbm = pl.BlockSpec(memory_space=pltpu.HBM)
    o_spec = pl.BlockSpec((bq, LANES), lambda h, i: (i, h))
    o_shape = jax.ShapeDtypeStruct((s, heads * LANES), F32)
    return pl.pallas_call(
        body, name=name, grid=(heads, nq),
        in_specs=[pl.BlockSpec((bq, LANES), lambda h, i: (i, h)),
                  pl.BlockSpec((s, LANES), lambda h, i: (0, heads + h)),
                  pl.BlockSpec((s, LANES), lambda h, i: (0, 2 * heads + h))] + ([hbm] if riding else []),
        out_specs=[o_spec, hbm] if riding else o_spec,
        out_shape=[o_shape, jax.ShapeDtypeStruct((N_CHIPS,) + ride.shape, ride.dtype)] if riding else o_shape,
        scratch_shapes=[pltpu.VMEM((bq, LANES), F32), pltpu.VMEM((bq, LANES), F32)]
        + (_all_gather_sems() if riding else []),
        compiler_params=pltpu.CompilerParams(dimension_semantics=("arbitrary", "arbitrary"),
                                             vmem_limit_bytes=VMEM_LIMIT, has_side_effects=riding),
    )(*([qkv, qkv, qkv] + ([ride] if riding else [])))


def _sb_bwd(qkv, o, do, heads, *, name, ride=None):
    s = qkv.shape[0]
    bq, bk, nq, ratio = _att_blocks(s)
    riding = ride is not None

    def body(*refs):
        if riding:
            (q_ref, k_ref, v_ref, o_ref, do_ref, g_ref, dq_ref, dk_ref, dv_ref, got_ref,
             dq_acc, dk_acc, dv_acc, c_ref, e_ref) = refs[:15]
            first = jnp.logical_and(pl.program_id(0) == 0, pl.program_id(1) == 0)
            last = jnp.logical_and(pl.program_id(0) == heads - 1, pl.program_id(1) == nq - 1)
            pl.when(first)(functools.partial(_exchange_start, g_ref, got_ref, refs[15:]))
        else:
            q_ref, k_ref, v_ref, o_ref, do_ref, dq_ref, dk_ref, dv_ref, dq_acc, dk_acc, dv_acc, c_ref, e_ref = refs
        qi = pl.program_id(1)

        @pl.when(qi == 0)
        def _():
            dk_acc[...] = jnp.zeros_like(dk_acc)
            dv_acc[...] = jnp.zeros_like(dv_acc)

        q = q_ref[...] * SB_SCALE
        do = do_ref[...]
        q_t, do_t = q.T, do.T
        total = jnp.sum(do.astype(F32) * o_ref[...].astype(F32), axis=-1, keepdims=True)
        m_strict, m_incl = _suffix_matrices(bk)
        dq_acc[...] = jnp.zeros_like(dq_acc)
        c_ref[...] = jnp.zeros_like(c_ref)
        e_ref[...] = jnp.broadcast_to(total, e_ref.shape)
        reps = (1, bk // LANES)

        def step(kb, masked, r0, r1):
            rows = pl.ds(pl.multiple_of(kb * bk, bk), bk)
            mine = pl.ds(r0, r1 - r0)
            k, v = k_ref[rows, :], v_ref[rows, :]
            qs, dos = q[r0:r1], do[r0:r1]
            lb, lk_all = _sb_logs(qs, k)
            lk = lk_all
            if masked:
                row, col = _block_positions(qi, kb, bq, bk, r0, r1)
                causal = col < row
                lk = jnp.where(causal, lk_all, 0.0)
            c = c_ref[mine, :]
            w = jnp.exp(lb + _dot_nn(lk.astype(BF16), m_strict) + jnp.tile(c, reps))
            if masked:
                w = jnp.where(causal, w, 0.0)
            wb = w.astype(BF16)
            g = wb.astype(F32) * _dot_nt(dos, v)
            e = e_ref[mine, :]
            g_left = jnp.tile(e, reps) - _suffix_sum(g, m_incl)
            da = g * jnp.exp(lk_all) - jnp.exp(lb) * g_left
            if masked:
                da = jnp.where(causal, da, 0.0)
            dab = da.astype(BF16)
            dq_acc[mine, :] += _dot_nn(dab, k)
            dk_acc[:SB_HEAD_DIM, rows] += _dot_nn(q_t[:SB_HEAD_DIM, r0:r1], dab)
            dv_acc[:SB_HEAD_DIM, rows] += _dot_nn(do_t[:SB_HEAD_DIM, r0:r1], wb)
            e_ref[mine, :] = e - jnp.sum(g, axis=-1, keepdims=True)
            c_ref[mine, :] = c + jnp.sum(lk, axis=-1, keepdims=True)

        _sweep(qi, ratio, bk, step, unroll=1, alive=functools.partial(_stick_left, c_ref))
        dq_ref[...] = (dq_acc[...] * SB_SCALE).astype(dq_ref.dtype)

        @pl.when(qi == nq - 1)
        def _():
            dk_ref[...] = dk_acc[...].T.astype(dk_ref.dtype)
            dv_ref[...] = dv_acc[...].T.astype(dv_ref.dtype)

        if riding:
            pl.when(last)(functools.partial(_exchange_finish, g_ref, got_ref, refs[15:]))

    blk = pl.BlockSpec((bq, LANES), lambda h, i: (i, h))
    full = pl.BlockSpec((s, LANES), lambda h, i: (0, h))
    hbm = pl.BlockSpec(memory_space=pltpu.HBM)
    shape = jax.ShapeDtypeStruct((s, heads * LANES), BF16)
    return pl.pallas_call(
        body, name=name, grid=(heads, nq),
        in_specs=[blk,
                  pl.BlockSpec((s, LANES), lambda h, i: (0, heads + h)),
                  pl.BlockSpec((s, LANES), lambda h, i: (0, 2 * heads + h)),
                  blk, blk] + ([hbm] if riding else []),
        out_specs=[blk, full, full] + ([hbm] if riding else []),
        out_shape=[shape, shape, shape] + ([jax.ShapeDtypeStruct(ride.shape, ride.dtype)] if riding else []),
        scratch_shapes=[pltpu.VMEM((bq, LANES), F32), pltpu.VMEM((LANES, s), F32), pltpu.VMEM((LANES, s), F32),
                        pltpu.VMEM((bq, LANES), F32), pltpu.VMEM((bq, LANES), F32)]
        + (_exchange_sems() if riding else []),
        compiler_params=pltpu.CompilerParams(dimension_semantics=("arbitrary", "arbitrary"),
                                             vmem_limit_bytes=VMEM_LIMIT, has_side_effects=riding),
    )(*([qkv, qkv, qkv, o, do] + ([ride] if riding else [])))


def _chunk_allowed(qi, kb, bq, bk, r0, r1):
    row, col = _block_positions(qi, kb, bq, bk, r0, r1)
    return (col // CHUNK) <= (row // CHUNK)


def _mla_fwd(q, kv, heads, *, name):
    s = q.shape[0]
    bq, bk, nq, ratio = _att_blocks(s, MLA_FWD_K_BLOCK)
    reps = (1, bk // LANES)

    def body(q_ref, k_ref, v_ref, o_ref, lse_ref, acc_ref, m_ref, l_ref):
        qi = pl.program_id(1)
        qv = q_ref[...]
        acc_ref[...] = jnp.zeros_like(acc_ref)
        m_ref[...] = jnp.full_like(m_ref, NEG_BIG)
        l_ref[...] = jnp.zeros_like(l_ref)

        def step(kb, masked, r0, r1):
            rows = pl.ds(pl.multiple_of(kb * bk, bk), bk)
            mine = pl.ds(r0, r1 - r0)
            k, v = k_ref[rows, :], v_ref[rows, :]
            sc = _dot_nt(qv[r0:r1], k)
            if masked:
                allowed = _chunk_allowed(qi, kb, bq, bk, r0, r1)
                sc = jnp.where(allowed, sc, NEG_BIG)
            m_old = m_ref[mine, :]
            m_new = jnp.maximum(m_old, jnp.max(sc, axis=-1, keepdims=True))
            p = jnp.exp(sc - jnp.tile(m_new, reps))
            alpha = jnp.exp(m_old - m_new)
            l_ref[mine, :] = alpha * l_ref[mine, :] + jnp.sum(p, axis=-1, keepdims=True)
            acc_ref[mine, :] = alpha * acc_ref[mine, :] + _dot_nn(p.astype(BF16), v)
            m_ref[mine, :] = m_new

        _sweep(qi, ratio, bk, step)
        o_ref[...] = (acc_ref[...] / l_ref[...]).astype(o_ref.dtype)
        lse_ref[...] = m_ref[...] + jnp.log(l_ref[...])

    blk = pl.BlockSpec((bq, LANES), lambda h, i: (i, h))
    return pl.pallas_call(
        body, name=name, grid=(heads, nq),
        in_specs=[blk,
                  pl.BlockSpec((s, LANES), lambda h, i: (0, h)),
                  pl.BlockSpec((s, LANES), lambda h, i: (0, heads + h))],
        out_specs=[blk, blk],
        out_shape=[jax.ShapeDtypeStruct((s, heads * LANES), BF16), jax.ShapeDtypeStruct((s, heads * LANES), F32)],
        scratch_shapes=[pltpu.VMEM((bq, LANES), F32), pltpu.VMEM((bq, LANES), F32), pltpu.VMEM((bq, LANES), F32)],
        compiler_params=pltpu.CompilerParams(dimension_semantics=("parallel", "arbitrary"),
                                             vmem_limit_bytes=VMEM_LIMIT),
    )(q, kv, kv)


def _mla_bwd(q, kv, o, do, lse, cos_t, sin_t, dkv_init, heads, *, name):
    s = q.shape[0]
    bq, bk, nq, ratio = _att_blocks(s)
    reps = (1, bk // LANES)
    has_init = dkv_init is not None

    def body(*refs):
        q_ref, k_ref, v_ref, o_ref, do_ref, lse_ref, c_ref, s_ref = refs[:8]
        ki_ref, vi_ref = (refs[8], refs[9]) if has_init else (None, None)
        dq_ref, dk_ref, dv_ref, dq_acc, dk_acc, dv_acc = refs[-6:]
        qi = pl.program_id(1)

        @pl.when(qi == 0)
        def _():
            if has_init:
                dk_acc[...] = ki_ref[...].astype(F32).T
                dv_acc[...] = vi_ref[...].astype(F32).T
            else:
                dk_acc[...] = jnp.zeros_like(dk_acc)
                dv_acc[...] = jnp.zeros_like(dv_acc)

        qv = q_ref[...]
        do = do_ref[...]
        q_t, do_t = qv.T, do.T
        delta = jnp.sum(do.astype(F32) * o_ref[...].astype(F32), axis=-1, keepdims=True)
        lse_wide = jnp.tile(lse_ref[...], reps)
        dq_acc[...] = jnp.zeros_like(dq_acc)

        def step(kb, masked, r0, r1):
            rows = pl.ds(pl.multiple_of(kb * bk, bk), bk)
            k, v = k_ref[rows, :], v_ref[rows, :]
            qs, dos = qv[r0:r1], do[r0:r1]
            p = jnp.exp(_dot_nt(qs, k) - lse_wide[r0:r1])
            if masked:
                p = jnp.where(_chunk_allowed(qi, kb, bq, bk, r0, r1), p, 0.0)
            ds = (p * (_dot_nt(dos, v) - delta[r0:r1])).astype(BF16)
            dq_acc[pl.ds(r0, r1 - r0), :] += _dot_nn(ds, k)
            dk_acc[:MLA_NOPE + MLA_ROPE, rows] += _dot_nn(q_t[:MLA_NOPE + MLA_ROPE, r0:r1], ds)
            dv_acc[:MLA_V, rows] += _dot_nn(do_t[:MLA_V, r0:r1], p.astype(BF16))

        _sweep(qi, ratio, bk, step)
        dq_ref[...] = _rope_slab_bwd(dq_acc[...] * MLA_SCALE, c_ref[...], s_ref[...]).astype(dq_ref.dtype)

        @pl.when(qi == nq - 1)
        def _():
            dk_ref[...] = dk_acc[...].T.astype(dk_ref.dtype)
            dv_ref[...] = dv_acc[...].T.astype(dv_ref.dtype)

    blk = pl.BlockSpec((bq, LANES), lambda h, i: (i, h))
    tab = pl.BlockSpec((bq, LANES), lambda h, i: (i, 0))
    k_full = pl.BlockSpec((s, LANES), lambda h, i: (0, h))
    v_full = pl.BlockSpec((s, LANES), lambda h, i: (0, heads + h))
    shape = jax.ShapeDtypeStruct((s, heads * LANES), BF16)
    ins = [q, kv, kv, o, do, lse, cos_t, sin_t] + ([dkv_init, dkv_init] if has_init else [])
    dq, dk, dv = pl.pallas_call(
        body, name=name, grid=(heads, nq),
        in_specs=[blk, k_full, v_full, blk, blk, blk, tab, tab] + ([k_full, v_full] if has_init else []),
        out_specs=[blk, k_full, k_full],
        out_shape=[shape, shape, shape],
        scratch_shapes=[pltpu.VMEM((bq, LANES), F32), pltpu.VMEM((LANES, s), F32), pltpu.VMEM((LANES, s), F32)],
        compiler_params=pltpu.CompilerParams(dimension_semantics=("arbitrary", "arbitrary"),
                                             vmem_limit_bytes=VMEM_LIMIT),
    )(*ins)
    return dq, jnp.concatenate([dk, dv], axis=1)


def _pad_last(a, width):
    return jnp.pad(a, [(0, 0)] * (a.ndim - 1) + [(0, width - a.shape[-1])])


def _pad_qkv(w, heads):
    d = w.shape[0]
    return _pad_last(w.reshape(d, 3 * heads, SB_HEAD_DIM), LANES).reshape(d, 3 * heads * LANES)


def _unpad_qkv(g, heads):
    d = g.shape[0]
    return g.reshape(d, 3 * heads, LANES)[:, :, :SB_HEAD_DIM].reshape(d, 3 * heads * SB_HEAD_DIM)


def _pad_o(w, heads):
    d = w.shape[1]
    w = w.reshape(heads, SB_HEAD_DIM, d)
    return jnp.pad(w, [(0, 0), (0, LANES - SB_HEAD_DIM), (0, 0)]).reshape(heads * LANES, d)


def _unpad_o(g, heads):
    d = g.shape[1]
    return g.reshape(heads, LANES, d)[:, :SB_HEAD_DIM, :].reshape(heads * SB_HEAD_DIM, d)


def _pad_uq(w, heads):
    r = w.shape[0]
    return _pad_last(w.reshape(r, heads, MLA_NOPE + MLA_ROPE), LANES).reshape(r, heads * LANES)


def _unpad_uq(g, heads):
    r = g.shape[0]
    return g.reshape(r, heads, LANES)[:, :, :MLA_NOPE + MLA_ROPE].reshape(r, heads * (MLA_NOPE + MLA_ROPE))


def _pad_dkv(w):
    d = w.shape[0]
    rope = jnp.zeros((d, LANES), w.dtype).at[:, ROPE_LO:ROPE_LO + MLA_ROPE].set(w[:, MLA_KV_RANK:])
    return jnp.concatenate([w[:, :MLA_KV_RANK], rope], axis=1)


def _unpad_dkv(g):
    return jnp.concatenate([g[:, :MLA_KV_RANK], g[:, MLA_KV_RANK + ROPE_LO:MLA_KV_RANK + ROPE_LO + MLA_ROPE]], axis=1)


def _pad_ukv(w, heads):
    w = w.reshape(MLA_KV_RANK, heads, 2, MLA_NOPE)
    k_part = _pad_last(w[:, :, 0, :], LANES).reshape(MLA_KV_RANK, heads * LANES)
    v_part = _pad_last(w[:, :, 1, :], LANES).reshape(MLA_KV_RANK, heads * LANES)
    lane = jnp.arange(LANES)
    place = ((lane[:, None] == lane[None, :]) & (lane[:, None] >= ROPE_LO) & (lane[:, None] < ROPE_LO + MLA_ROPE))
    place = jnp.tile(place.astype(w.dtype), (1, heads))
    top = jnp.concatenate([k_part, v_part], axis=1)
    bottom = jnp.concatenate([place, jnp.zeros_like(place)], axis=1)
    return jnp.concatenate([top, bottom], axis=0)


def _unpad_ukv(g, heads):
    g = g[:MLA_KV_RANK]
    k_part = g[:, :heads * LANES].reshape(MLA_KV_RANK, heads, LANES)[:, :, :MLA_NOPE]
    v_part = g[:, heads * LANES:].reshape(MLA_KV_RANK, heads, LANES)[:, :, :MLA_V]
    return jnp.stack([k_part, v_part], axis=2).reshape(MLA_KV_RANK, heads * (MLA_NOPE + MLA_V))


def _rope_tables(positions):
    inv_freq = ROPE_THETA ** (-jnp.arange(0, MLA_ROPE, 2, dtype=F32) / MLA_ROPE)
    ang = positions.astype(F32)[:, None] * inv_freq
    cos, sin = jnp.cos(ang), jnp.sin(ang)
    s = positions.shape[0]
    cos_t = jnp.ones((s, LANES), F32).at[:, ROPE_LO:ROPE_LO + MLA_ROPE].set(jnp.concatenate([cos, cos], axis=1))
    sin_t = jnp.zeros((s, LANES), F32).at[:, ROPE_LO:ROPE_LO + MLA_ROPE].set(jnp.concatenate([-sin, sin], axis=1))
    return cos_t, sin_t


def _local_step(x, positions, target, qkv_w0, norms, rest_weights, ride=None, early_reduce=None):
    s, d = x.shape
    heads = d // SB_HEAD_DIM
    cos_t, sin_t = _rope_tables(positions)

    h_first = _rms_fwd(x, norms["attn_norm"][0], name="l0_attn_norm")
    qkv_first = _mm(h_first, _pad_qkv(qkv_w0, heads), name="l0_qkv")
    if ride is None:
        o_first, gathered = _sb_fwd(qkv_first, heads, name="l0_sb_fwd"), None
    else:
        o_first, gathered = _sb_fwd(qkv_first, heads, name="l0_sb_fwd", ride=ride)
    w = rest_weights(gathered)
    n_a = w["sb_w_qkv"].shape[0]
    n_b = w["mla_w_dq"].shape[0]
    depth = n_a + n_b

    wqkv = [_pad_qkv(w["sb_w_qkv"][l], heads) for l in range(n_a)]
    wo_a = [_pad_o(w["sb_w_o"][l], heads) for l in range(n_a)]
    wdkv = _pad_dkv(w["mla_w_dkv"])
    wkv = _pad_ukv(w["mla_w_ukv"], heads)
    wdq = [w["mla_w_dq"][j] for j in range(n_b)]
    wuq = [_pad_uq(w["mla_w_uq"][j], heads) for j in range(n_b)]
    wo_b = [_pad_o(w["mla_w_o"][j], heads) for j in range(n_b)]
    w1 = [w["mlp_w1"][l] for l in range(depth)]
    w2 = [w["mlp_w2"][l] for l in range(depth)]

    saved = []
    kv_saved = None
    kv = None
    for l in range(depth):
        t = f"l{l}"
        sv = {"x_in": x}
        h = h_first if l == 0 else _rms_fwd(x, norms["attn_norm"][l], name=f"{t}_attn_norm")
        sv["h"] = h
        if l < n_a:
            if l == 0:
                qkv, o = qkv_first, o_first
            else:
                qkv = _mm(h, wqkv[l], name=f"{t}_qkv")
                o = _sb_fwd(qkv, heads, name=f"{t}_sb_fwd")
            sv["qkv"], sv["o"] = qkv, o
            x = _mm(o, wo_a[l], name=f"{t}_attn_out", epilogue=_epi_add, extras=[(x, "tile")], out_dtypes=(F32,))
        else:
            j = l - n_a
            if j == 0:
                hk = _rms_fwd(x, norms["kv_norm"], name="kv_norm")
                down = _mm(hk, wdkv, name="kv_down", out_dtypes=(F32,))
                cat = _kv_prep(down, norms["mla_kv_lat_norm"], cos_t, sin_t, name="kv_prep")
                kv = _mm(cat, wkv, name="kv_up")
                kv_saved = {"x_in": x, "hk": hk, "down": down, "cat": cat}
            cq0 = _mm(h, wdq[j], name=f"{t}_q_down", out_dtypes=(F32,))
            cq = _rms_fwd(cq0, norms["mla_q_lat_norm"][j], name=f"{t}_q_lat_norm")
            q = _mm(cq, wuq[j], name=f"{t}_q_up", epilogue=_epi_rope_heads, extras=[(cos_t, "row"), (sin_t, "row")])
            o, lse = _mla_fwd(q, kv, heads, name=f"{t}_mla_fwd")
            sv.update(cq0=cq0, cq=cq, q=q, o=o, lse=lse)
            x = _mm(o, wo_b[j], name=f"{t}_attn_out", epilogue=_epi_add, extras=[(x, "tile")], out_dtypes=(F32,))
        sv["x_mid"] = x
        h2 = _rms_fwd(x, norms["mlp_norm"][l], name=f"{t}_mlp_norm")
        u, a = _mm(h2, w1[l], name=f"{t}_mlp_up", epilogue=_epi_relu2, out_dtypes=(BF16, BF16))
        sv.update(h2=h2, u=u, a=a)
        x = _mm(a, w2[l], name=f"{t}_mlp_down", epilogue=_epi_add, extras=[(x, "tile")], out_dtypes=(F32,))
        saved.append(sv)

    loss_slab, dx, dxb, dg_final = _loss_bwd(x, norms["final_norm"], target, name="loss")
    loss = loss_slab[0, 0]

    g_attn_norm, g_mlp_norm = [None] * depth, [None] * depth
    g_qkv, g_o_a = [None] * n_a, [None] * n_a
    g_dq, g_uq, g_o_b, g_qlat = [None] * n_b, [None] * n_b, [None] * n_b, [None] * n_b
    g_w1, g_w2 = [None] * depth, [None] * depth
    dkv = None
    g_kv_norm = g_kv_lat = g_dkv = g_ukv = None
    early_parts = early_got = None

    for l in reversed(range(depth)):
        t = f"l{l}"
        sv = saved[l]
        du = _mm(dxb, w2[l], name=f"{t}_mlp_down_dx", dims="nt", epilogue=_epi_relu2_grad, extras=[(sv["u"], "tile")])
        g_w2[l] = _mm(sv["a"], dxb, name=f"{t}_mlp_down_dw", dims="tn", out_dtypes=(F32,))
        g_w1[l] = _mm(sv["h2"], du, name=f"{t}_mlp_up_dw", dims="tn", out_dtypes=(F32,))
        dx, dxb, g_mlp_norm[l] = _mm_rms_bwd(du, w1[l], sv["x_mid"], norms["mlp_norm"][l], dx, name=f"{t}_mlp_up_dx")
        if l < n_a:
            do = _mm(dxb, wo_a[l], name=f"{t}_attn_out_dx", dims="nt")
            g_o_a[l] = _unpad_o(_mm(sv["o"], dxb, name=f"{t}_attn_out_dw", dims="tn", out_dtypes=(F32,)), heads)
            if l == 0 and early_reduce is not None:
                early_parts = early_reduce({
                    "sb_w_qkv": g_qkv[1:], "sb_w_o": g_o_a, "mla_w_dkv": g_dkv, "mla_w_ukv": g_ukv, "mla_w_dq": g_dq,
                    "mla_w_uq": g_uq, "mla_w_o": g_o_b, "mlp_w1": g_w1, "mlp_w2": g_w2})
                dq, dk, dv, early_got = _sb_bwd(sv["qkv"], sv["o"], do, heads, name=f"{t}_sb_bwd", ride=early_parts)
            else:
                dq, dk, dv = _sb_bwd(sv["qkv"], sv["o"], do, heads, name=f"{t}_sb_bwd")
            dqkv = jnp.concatenate([dq, dk, dv], axis=1)
            g_qkv[l] = _unpad_qkv(_mm(sv["h"], dqkv, name=f"{t}_qkv_dw", dims="tn", out_dtypes=(F32,)), heads)
            dh_src, dh_w = dqkv, wqkv[l]
        else:
            j = l - n_a
            do = _mm(dxb, wo_b[j], name=f"{t}_attn_out_dx", dims="nt")
            g_o_b[j] = _unpad_o(_mm(sv["o"], dxb, name=f"{t}_attn_out_dw", dims="tn", out_dtypes=(F32,)), heads)
            dq, dkv = _mla_bwd(sv["q"], kv, sv["o"], do, sv["lse"], cos_t, sin_t, dkv, heads, name=f"{t}_mla_bwd")
            g_uq[j] = _unpad_uq(_mm(sv["cq"], dq, name=f"{t}_q_up_dw", dims="tn", out_dtypes=(F32,)), heads)
            _, dcq0, g_qlat[j] = _mm_rms_bwd(dq, wuq[j], sv["cq0"], norms["mla_q_lat_norm"][j], None,
                                             name=f"{t}_q_up_dx")
            g_dq[j] = _mm(sv["h"], dcq0, name=f"{t}_q_down_dw", dims="tn", out_dtypes=(F32,))
            dh_src, dh_w = dcq0, wdq[j]
        if l == 0:
            dh = _mm(dh_src, dh_w, name=f"{t}_attn_in_dx", dims="nt", out_dtypes=(F32,))
            dx, dxb, g_attn_norm[l] = _rms_bwd(sv["x_in"], norms["attn_norm"][l], dh, dx, name=f"{t}_attn_norm_bwd",
                                               lead_axis=True)
        else:
            dx, dxb, g_attn_norm[l] = _mm_rms_bwd(dh_src, dh_w, sv["x_in"], norms["attn_norm"][l], dx,
                                                  name=f"{t}_attn_in_dx")
        if l == n_a:
            ks = kv_saved
            dcat = _mm(dkv, wkv, name="kv_up_dx", dims="nt", out_dtypes=(F32,))
            g_ukv = _unpad_ukv(_mm(ks["cat"], dkv, name="kv_up_dw", dims="tn", out_dtypes=(F32,)), heads)
            ddown, g_kv_lat = _kv_prep_bwd(ks["down"], norms["mla_kv_lat_norm"], cos_t, sin_t, dcat, name="kv_prep_bwd")
            g_dkv = _unpad_dkv(_mm(ks["hk"], ddown, name="kv_down_dw", dims="tn", out_dtypes=(F32,)))
            dx, dxb, g_kv_norm = _mm_rms_bwd(ddown, wdkv, ks["x_in"], norms["kv_norm"], dx, name="kv_down_dx")

    grads = {
        "attn_norm": jnp.concatenate(g_attn_norm, axis=0), "mlp_norm": jnp.concatenate(g_mlp_norm, axis=0),
        "sb_w_qkv": g_qkv, "sb_w_o": g_o_a,
        "kv_norm": g_kv_norm[0], "mla_w_dkv": g_dkv, "mla_kv_lat_norm": g_kv_lat[0], "mla_w_ukv": g_ukv,
        "mla_w_dq": g_dq, "mla_q_lat_norm": jnp.concatenate(g_qlat, axis=0),
        "mla_w_uq": g_uq, "mla_w_o": g_o_b,
        "mlp_w1": g_w1, "mlp_w2": g_w2, "final_norm": dg_final[0],
    }
    return loss, dx, grads, early_parts, early_got


def _flat_rows(n_elems):
    per_block = FLAT_COLS * FLAT_ROW_BLOCK * 2
    return -(-n_elems // per_block) * FLAT_ROW_BLOCK * 2


def _row_blocks(arrays, dtype):
    for a in arrays:
        assert a.size % FLAT_COLS == 0, a.shape
    blocks = [a.astype(dtype).reshape(-1, FLAT_COLS) for a in arrays]
    used = sum(b.shape[0] for b in blocks)
    rows = _flat_rows(used * FLAT_COLS)
    return blocks + [jnp.zeros((rows - used, FLAT_COLS), dtype)], rows


def _pack(arrays, dtype):
    blocks, _ = _row_blocks(arrays, dtype)
    return jnp.concatenate(blocks, axis=0)


def _pack_chips(per_chip, dtype):
    blocks, rows = [], 0
    for arrays in per_chip:
        chip_blocks, rows = _row_blocks(arrays, dtype)
        blocks += chip_blocks
    return jnp.concatenate(blocks, axis=0).reshape(len(per_chip), rows, FLAT_COLS)


def _unpack(flat, shapes):
    out, row = [], 0
    for shp in shapes:
        n = 1
        for v in shp:
            n *= v
        out.append(flat[row:row + n // FLAT_COLS].reshape(shp))
        row += n // FLAT_COLS
    return out


def _pack_small(arrays):
    rows = []
    for a in arrays:
        a = a.reshape(-1, a.shape[-1]) if a.shape[-1] == FLAT_COLS else a.reshape(1, -1)
        rows.append(_pad_last(a, FLAT_COLS))
    flat = jnp.concatenate(rows, axis=0)
    return jnp.pad(flat, [(0, -flat.shape[0] % 8), (0, 0)])


def _unpack_small(flat, shapes):
    out, row = [], 0
    for shp in shapes:
        if shp[-1] == FLAT_COLS:
            n = 1
            for v in shp[:-1]:
                n *= v
            out.append(flat[row:row + n].reshape(shp))
            row += n
        else:
            n = 1
            for v in shp:
                n *= v
            out.append(flat[row, :n].reshape(shp))
            row += 1
    return out


def _other_chips(x, y):
    return [(1 - x, y), (x, 1 - y), (1 - x, 1 - y)]


def _all_gather_chips(flat, *, name):
    rows, cols = flat.shape

    def body(x_ref, out_ref, *sems):
        _all_gather_start(x_ref, out_ref, sems)
        _all_gather_finish(x_ref, out_ref, sems)

    return pl.pallas_call(
        body, name=name,
        in_specs=[pl.BlockSpec(memory_space=pltpu.HBM)],
        out_specs=pl.BlockSpec(memory_space=pltpu.HBM),
        out_shape=jax.ShapeDtypeStruct((N_CHIPS, rows, cols), flat.dtype),
        scratch_shapes=_all_gather_sems(),
        compiler_params=pltpu.CompilerParams(has_side_effects=True),
    )(flat)


def _all_gather_sems():
    return [pltpu.SemaphoreType.DMA((3,)), pltpu.SemaphoreType.DMA((3,)), pltpu.SemaphoreType.DMA((3,)),
            pltpu.SemaphoreType.DMA((3,)), pltpu.SemaphoreType.DMA, pltpu.SemaphoreType.DMA]


def _all_gather_copies(x_ref, out_ref, sems, finishing):
    send_sems, recv_sems, pass_send_sems, pass_recv_sems, own_send_sem, own_recv_sem = sems
    x, y, c = lax.axis_index("x"), lax.axis_index("y"), lax.axis_index("c")
    me = 2 * x + y
    my_rows, sib_rows = _half_rows(x_ref.shape[0])

    def copy(src, dst, send_sem, recv_sem, to):
        return pltpu.make_async_remote_copy(src_ref=src, dst_ref=dst, send_sem=send_sem, recv_sem=recv_sem,
                                            device_id=to, device_id_type=MESH)

    own = copy(x_ref, out_ref.at[me], own_send_sem, own_recv_sem, _sibling())
    to_chips, landed, pass_on, passed = [], [], [], []
    for k, (px, py) in enumerate(_other_chips(x, y)):
        to_chips.append(copy(x_ref.at[my_rows, :], out_ref.at[me, my_rows, :], send_sems.at[k], recv_sems.at[k],
                             (px, py, c)))
        if finishing:
            mine, theirs = out_ref.at[2 * px + py, my_rows, :], out_ref.at[2 * px + py, sib_rows, :]
            landed.append(copy(mine, mine, send_sems.at[k], recv_sems.at[k], (px, py, c)))
            pass_on.append(copy(mine, mine, pass_send_sems.at[k], pass_recv_sems.at[k], _sibling()))
            passed.append(copy(theirs, theirs, pass_send_sems.at[k], pass_recv_sems.at[k], _sibling()))
    return own, to_chips, landed, pass_on, passed


def _all_gather_start(x_ref, out_ref, sems):
    own, to_chips, _, _, _ = _all_gather_copies(x_ref, out_ref, sems, finishing=False)
    own.start()
    for cp in to_chips:
        cp.start()


def _all_gather_finish(x_ref, out_ref, sems):
    own, to_chips, landed, pass_on, passed = _all_gather_copies(x_ref, out_ref, sems, finishing=True)
    for k in range(len(landed)):
        landed[k].wait_recv()
        pass_on[k].start()
    for cp in passed:
        cp.wait_recv()
    own.wait_recv()
    for cp in [own] + to_chips + pass_on:
        cp.wait_send()


def _exchange_chips(parts, *, name):
    def body(g_ref, out_ref, *sems):
        _exchange_start(g_ref, out_ref, sems)
        _exchange_finish(g_ref, out_ref, sems)

    return pl.pallas_call(
        body, name=name,
        in_specs=[pl.BlockSpec(memory_space=pltpu.HBM)],
        out_specs=pl.BlockSpec(memory_space=pltpu.HBM),
        out_shape=jax.ShapeDtypeStruct(parts.shape, parts.dtype),
        scratch_shapes=_exchange_sems(),
        compiler_params=pltpu.CompilerParams(has_side_effects=True),
    )(parts)


def _exchange_sems():
    return [pltpu.SemaphoreType.DMA((3,)), pltpu.SemaphoreType.DMA((3,))]


def _exchange_copies(g_ref, out_ref, sems, receiving):
    send_sems, recv_sems = sems
    x, y, c = lax.axis_index("x"), lax.axis_index("y"), lax.axis_index("c")
    me = 2 * x + y
    copies = []
    for k, (px, py) in enumerate(_other_chips(x, y)):
        src, dst = (g_ref.at[me], out_ref.at[2 * px + py]) if receiving else (g_ref.at[2 * px + py], out_ref.at[me])
        copies.append(pltpu.make_async_remote_copy(src_ref=src, dst_ref=dst, send_sem=send_sems.at[k],
                                                   recv_sem=recv_sems.at[k], device_id=(px, py, c),
                                                   device_id_type=MESH))
    return copies


def _exchange_start(g_ref, out_ref, sems):
    for cp in _exchange_copies(g_ref, out_ref, sems, receiving=False):
        cp.start()


def _exchange_finish(g_ref, out_ref, sems):
    for cp in _exchange_copies(g_ref, out_ref, sems, receiving=True):
        cp.wait_recv()
    for cp in _exchange_copies(g_ref, out_ref, sems, receiving=False):
        cp.wait_send()


def _my_chip():
    return 2 * lax.axis_index("x") + lax.axis_index("y")


def _half_rows(rows):
    c = lax.axis_index("c")
    half = rows // 2
    return pl.ds(pl.multiple_of(c * half, 8), half), pl.ds(pl.multiple_of((1 - c) * half, 8), half)


def _sibling():
    return (lax.axis_index("x"), lax.axis_index("y"), 1 - lax.axis_index("c"))


def _pair_exchange(parts, *, name):
    n, rows, cols = parts.shape

    def body(p_ref, theirs_ref, send_sem, recv_sem):
        _, sib_rows = _half_rows(rows)
        cp = pltpu.make_async_remote_copy(src_ref=p_ref.at[:, sib_rows, :], dst_ref=theirs_ref, send_sem=send_sem,
                                          recv_sem=recv_sem, device_id=_sibling(), device_id_type=MESH)
        cp.start()
        cp.wait()

    half = rows // 2
    theirs = pl.pallas_call(
        body, name=name,
        in_specs=[pl.BlockSpec(memory_space=pltpu.HBM)],
        out_specs=pl.BlockSpec(memory_space=pltpu.HBM),
        out_shape=jax.ShapeDtypeStruct((n, half, cols), parts.dtype),
        scratch_shapes=[pltpu.SemaphoreType.DMA, pltpu.SemaphoreType.DMA],
        compiler_params=pltpu.CompilerParams(has_side_effects=True),
    )(parts)
    mine = lax.dynamic_slice_in_dim(parts, lax.axis_index("c") * half, half, axis=1)
    return mine, theirs


def _pair_sum(mine, theirs, *, name):
    n, rows, cols = mine.shape

    def body(a_ref, b_ref, o_ref):
        o_ref[...] = (a_ref[...].astype(F32) + b_ref[...].astype(F32)).astype(o_ref.dtype)

    blk = pl.BlockSpec((n, FLAT_ROW_BLOCK, cols), lambda i: (0, i, 0))
    return pl.pallas_call(
        body, name=name, grid=(rows // FLAT_ROW_BLOCK,),
        in_specs=[blk, blk], out_specs=blk, out_shape=jax.ShapeDtypeStruct(mine.shape, mine.dtype),
        compiler_params=pltpu.CompilerParams(dimension_semantics=("parallel",), vmem_limit_bytes=VMEM_LIMIT),
    )(mine, theirs)


def _sum_chips(received, own, *, name):
    _, rows, cols = received.shape

    def body(p_ref, own_ref, o_ref):
        me = 2 * lax.axis_index("x") + lax.axis_index("y")
        slot = [jnp.where(me == j, own_ref[j], p_ref[j]).astype(F32) for j in range(N_CHIPS)]
        o_ref[...] = ((slot[0] + slot[1]) + slot[2]) + slot[3]

    blk = pl.BlockSpec((N_CHIPS, FLAT_ROW_BLOCK, cols), lambda i: (0, i, 0))
    return pl.pallas_call(
        body, name=name, grid=(rows // FLAT_ROW_BLOCK,),
        in_specs=[blk, blk],
        out_specs=pl.BlockSpec((FLAT_ROW_BLOCK, cols), lambda i: (i, 0)),
        out_shape=jax.ShapeDtypeStruct((rows, cols), F32),
        compiler_params=pltpu.CompilerParams(dimension_semantics=("parallel",), vmem_limit_bytes=VMEM_LIMIT),
    )(received, own)


def _join_cores(half, *, name):
    rows2, cols = half.shape

    def body(h_ref, out_ref, send_sem, recv_sem):
        my_rows, sib_rows = _half_rows(2 * rows2)
        cp = pltpu.make_async_remote_copy(src_ref=h_ref, dst_ref=out_ref.at[my_rows, :], send_sem=send_sem,
                                          recv_sem=recv_sem, device_id=_sibling(), device_id_type=MESH)
        cp.start()
        cp.wait_send()
        pltpu.make_async_remote_copy(src_ref=h_ref, dst_ref=out_ref.at[sib_rows, :], send_sem=send_sem,
                                     recv_sem=recv_sem, device_id=_sibling(), device_id_type=MESH).wait_recv()

    out = pl.pallas_call(
        body, name=name,
        in_specs=[pl.BlockSpec(memory_space=pltpu.HBM)],
        out_specs=pl.BlockSpec(memory_space=pltpu.HBM),
        out_shape=jax.ShapeDtypeStruct((2 * rows2, cols), half.dtype),
        scratch_shapes=[pltpu.SemaphoreType.DMA, pltpu.SemaphoreType.DMA],
        compiler_params=pltpu.CompilerParams(has_side_effects=True),
    )(half)
    return lax.dynamic_update_slice_in_dim(out, half, lax.axis_index("c") * rows2, axis=0)


def _all_reduce_small(v, *, name):
    rows, cols = v.shape
    flips = [(fx, fy, fc) for fx in (0, 1) for fy in (0, 1) for fc in (0, 1)][1:]

    def body(v_ref, out_ref, gath_ref, send_sems, recv_sems):
        x, y, c = lax.axis_index("x"), lax.axis_index("y"), lax.axis_index("c")
        me = 4 * x + 2 * y + c
        gath_ref[me] = v_ref[...]
        peers = [((1 - x) if fx else x, (1 - y) if fy else y, (1 - c) if fc else c) for fx, fy, fc in flips]
        sends = []
        for k, peer in enumerate(peers):
            cp = pltpu.make_async_remote_copy(src_ref=v_ref, dst_ref=gath_ref.at[me], send_sem=send_sems.at[k],
                                              recv_sem=recv_sems.at[k], device_id=peer, device_id_type=MESH)
            cp.start()
            sends.append(cp)
        for k, (px, py, pc) in enumerate(peers):
            pltpu.make_async_remote_copy(src_ref=v_ref, dst_ref=gath_ref.at[4 * px + 2 * py + pc],
                                         send_sem=send_sems.at[k], recv_sem=recv_sems.at[k],
                                         device_id=(px, py, pc), device_id_type=MESH).wait_recv()
        for cp in sends:
            cp.wait_send()
        total = gath_ref[0]
        for k in range(1, 8):
            total = total + gath_ref[k]
        out_ref[...] = total

    total, _ = pl.pallas_call(
        body, name=name,
        in_specs=[pl.BlockSpec(memory_space=pltpu.VMEM)],
        out_specs=[pl.BlockSpec(memory_space=pltpu.VMEM), pl.BlockSpec(memory_space=pltpu.VMEM)],
        out_shape=[jax.ShapeDtypeStruct((rows, cols), v.dtype), jax.ShapeDtypeStruct((8, rows, cols), v.dtype)],
        scratch_shapes=[pltpu.SemaphoreType.DMA((7,)), pltpu.SemaphoreType.DMA((7,))],
        compiler_params=pltpu.CompilerParams(has_side_effects=True),
    )(v)
    return total


def _adamw(w, g, m, v, *, name):
    shape = w.shape
    cols = shape[-1]
    w2, g2, m2, v2 = (a.reshape(-1, cols) for a in (w, g, m, v))
    rows = w2.shape[0]
    br = _pick_rows(rows, FLAT_ROW_BLOCK)

    def body(w_ref, g_ref, m_ref, v_ref, d_out, m_out, v_out):
        gv = g_ref[...]
        m_new = ADAM_B1 * m_ref[...] + (1.0 - ADAM_B1) * gv
        v_new = ADAM_B2 * v_ref[...] + (1.0 - ADAM_B2) * jnp.square(gv)
        m_hat = m_new / (1.0 - ADAM_B1 ** ADAM_STEP)
        v_hat = v_new / (1.0 - ADAM_B2 ** ADAM_STEP)
        d_out[...] = -ADAM_LR * (m_hat / (jnp.sqrt(v_hat) + ADAM_EPS) + ADAM_WD * w_ref[...])
        m_out[...] = m_new
        v_out[...] = v_new

    blk = pl.BlockSpec((br, cols), lambda i: (i, 0))
    out = jax.ShapeDtypeStruct((rows, cols), F32)
    outs = pl.pallas_call(
        body, name=name, grid=(rows // br,),
        in_specs=[blk] * 4, out_specs=[blk] * 3, out_shape=[out] * 3,
        compiler_params=pltpu.CompilerParams(dimension_semantics=("parallel",), vmem_limit_bytes=VMEM_LIMIT),
    )(w2, g2, m2, v2)
    return [o.reshape(shape) for o in outs]


def _pick_rows(rows, target):
    if rows <= target:
        return rows
    return max(b for b in range(8, target + 1, 8) if rows % b == 0)


def _assemble(gathered_shards, name, layer=False):
    return jnp.concatenate(gathered_shards, axis=SHARD_AXIS[name] - int(layer))


def _chip_shard(full, name, j):
    if isinstance(full, list):
        axis = SHARD_AXIS[name] - 1
        layers = full
    else:
        axis = SHARD_AXIS[name]
        layers = [full]
    n = layers[0].shape[axis] // N_CHIPS
    return [lax.slice_in_dim(g, j * n, (j + 1) * n, axis=axis) for g in layers]


def kernel(x, positions, attn_norm, mlp_norm, sb_w_qkv, sb_w_o, kv_norm, mla_w_dkv, mla_kv_lat_norm, mla_w_ukv, mla_w_dq, mla_q_lat_norm, mla_w_uq, mla_w_o, mlp_w1, mlp_w2, final_norm, loss_target, m_attn_norm, m_mlp_norm, m_sb_w_qkv, m_sb_w_o, m_kv_norm, m_mla_w_dkv, m_mla_kv_lat_norm, m_mla_w_ukv, m_mla_w_dq, m_mla_q_lat_norm, m_mla_w_uq, m_mla_w_o, m_mlp_w1, m_mlp_w2, m_final_norm, v_attn_norm, v_mlp_norm, v_sb_w_qkv, v_sb_w_o, v_kv_norm, v_mla_w_dkv, v_mla_kv_lat_norm, v_mla_w_ukv, v_mla_w_dq, v_mla_q_lat_norm, v_mla_w_uq, v_mla_w_o, v_mlp_w1, v_mlp_w2, v_final_norm):
    weights = dict(attn_norm=attn_norm, mlp_norm=mlp_norm, sb_w_qkv=sb_w_qkv, sb_w_o=sb_w_o, kv_norm=kv_norm,
                   mla_w_dkv=mla_w_dkv, mla_kv_lat_norm=mla_kv_lat_norm, mla_w_ukv=mla_w_ukv, mla_w_dq=mla_w_dq,
                   mla_q_lat_norm=mla_q_lat_norm, mla_w_uq=mla_w_uq, mla_w_o=mla_w_o, mlp_w1=mlp_w1, mlp_w2=mlp_w2,
                   final_norm=final_norm)
    m_in = dict(attn_norm=m_attn_norm, mlp_norm=m_mlp_norm, sb_w_qkv=m_sb_w_qkv, sb_w_o=m_sb_w_o, kv_norm=m_kv_norm,
                mla_w_dkv=m_mla_w_dkv, mla_kv_lat_norm=m_mla_kv_lat_norm, mla_w_ukv=m_mla_w_ukv, mla_w_dq=m_mla_w_dq,
                mla_q_lat_norm=m_mla_q_lat_norm, mla_w_uq=m_mla_w_uq, mla_w_o=m_mla_w_o, mlp_w1=m_mlp_w1,
                mlp_w2=m_mlp_w2, final_norm=m_final_norm)
    v_in = dict(attn_norm=v_attn_norm, mlp_norm=v_mlp_norm, sb_w_qkv=v_sb_w_qkv, sb_w_o=v_sb_w_o, kv_norm=v_kv_norm,
                mla_w_dkv=v_mla_w_dkv, mla_kv_lat_norm=v_mla_kv_lat_norm, mla_w_ukv=v_mla_w_ukv, mla_w_dq=v_mla_w_dq,
                mla_q_lat_norm=v_mla_q_lat_norm, mla_w_uq=v_mla_w_uq, mla_w_o=v_mla_w_o, mlp_w1=v_mlp_w1,
                mlp_w2=v_mlp_w2, final_norm=v_final_norm)
    shard_shapes = [weights[n].shape for n in BIG_WEIGHTS]
    small_shapes = [weights[n].shape for n in SMALL_WEIGHTS]

    first_name = BIG_WEIGHTS[0]
    qkv_first, qkv_later = weights[first_name][0], weights[first_name][1:]
    gathered_first = _all_gather_chips(_pack([qkv_first], BF16), name="first_weight_all_gather")
    qkv_w0 = _assemble([_unpack(gathered_first[j], [qkv_first.shape])[0] for j in range(N_CHIPS)], first_name, layer=True)
    ride = _pack([qkv_later] + [weights[n] for n in BIG_WEIGHTS[1:]], BF16)
    ride_shapes = [qkv_later.shape] + shard_shapes[1:]

    def rest_weights(gathered):
        per_chip = [_unpack(gathered[j], ride_shapes) for j in range(N_CHIPS)]
        full = {n: _assemble([per_chip[j][i] for j in range(N_CHIPS)], n) for i, n in enumerate(BIG_WEIGHTS)}
        full[first_name] = jnp.concatenate([qkv_w0[None], full[first_name]], axis=0)
        return full

    norms = {n: weights[n] for n in SMALL_WEIGHTS}

    def chip_parts(g, tag):
        parts = _pack_chips([[piece for n in BIG_WEIGHTS if n in g for piece in _chip_shard(g[n], n, j)]
                             for j in range(N_CHIPS)], BF16)
        mine, theirs = _pair_exchange(parts, name=f"grads_pair_exchange_{tag}")
        return _pair_sum(mine, theirs, name=f"grads_pair_sum_{tag}")

    def finish(received, chip_part, tag):
        g_half = _sum_chips(received, chip_part, name=f"grads_sum_chips_{tag}")
        return _join_cores(g_half, name=f"grads_join_cores_{tag}")

    loss, dx, grads, early_parts, early_got = _local_step(
        x[0], positions[0], loss_target, qkv_w0, norms, rest_weights, ride=ride,
        early_reduce=functools.partial(chip_parts, tag="early"))
    loss = lax.psum(loss, ("x", "y", "c"))
    early_sum = finish(early_got, early_parts, "early")
    last_parts = chip_parts({first_name: grads[first_name][:1]}, "last")
    last_sum = finish(_exchange_chips(last_parts, name="grads_exchange_last"), last_parts, "last")

    out_g = dict(zip(BIG_WEIGHTS, _unpack(early_sum, ride_shapes)))
    out_g[first_name] = jnp.concatenate([_unpack(last_sum, [qkv_first.shape])[0][None], out_g[first_name]], axis=0)
    out_d, out_m, out_v = {}, {}, {}
    for n in BIG_WEIGHTS:
        out_d[n], out_m[n], out_v[n] = _adamw(weights[n], out_g[n], m_in[n], v_in[n], name=f"adamw_{n}")

    small_sum = _all_reduce_small(_pack_small([grads[n] for n in SMALL_WEIGHTS]), name="gains_all_reduce")
    sd, sm, sv = _adamw(_pack_small([weights[n] for n in SMALL_WEIGHTS]), small_sum,
                        _pack_small([m_in[n] for n in SMALL_WEIGHTS]),
                        _pack_small([v_in[n] for n in SMALL_WEIGHTS]), name="adamw_gains")
    out_g.update(zip(SMALL_WEIGHTS, _unpack_small(small_sum, small_shapes)))
    out_d.update(zip(SMALL_WEIGHTS, _unpack_small(sd, small_shapes)))
    out_m.update(zip(SMALL_WEIGHTS, _unpack_small(sm, small_shapes)))
    out_v.update(zip(SMALL_WEIGHTS, _unpack_small(sv, small_shapes)))

    return (loss, dx, *[out_g[n] for n in ALL_WEIGHTS], *[out_d[n] for n in ALL_WEIGHTS],
            *[out_m[n] for n in ALL_WEIGHTS], *[out_v[n] for n in ALL_WEIGHTS])
```

```python
import functools

import jax
import jax.numpy as jnp
from jax import lax
from jax.experimental import pallas as pl
from jax.experimental.pallas import tpu as pltpu

F32 = jnp.float32
BF16 = jnp.bfloat16

LANES = 128
SB_HEAD_DIM = 64
MLA_NOPE = 64
MLA_ROPE = 32
MLA_V = 64
MLA_Q_RANK = 384
MLA_KV_RANK = 256
CHUNK = 64
ROPE_THETA = 10000.0
NORM_EPS = 1e-6
SB_SCALE = SB_HEAD_DIM ** -0.5
MLA_SCALE = (MLA_NOPE + MLA_ROPE) ** -0.5
ROPE_LO = MLA_NOPE
ROPE_HALF = MLA_ROPE // 2
ATT_Q_BLOCK = 1024
ATT_K_BLOCK = 256
MLA_FWD_K_BLOCK = 512
NEG_BIG = -1e30
SB_DEAD_LOG = -110.0
VMEM_LIMIT = 56 * 1024 * 1024

ADAM_LR = 0.001
ADAM_B1 = 0.9
ADAM_B2 = 0.999
ADAM_EPS = 1e-08
ADAM_WD = 0.01
ADAM_STEP = 10

FLAT_COLS = 1024
FLAT_ROW_BLOCK = 256
N_CHIPS = 4
MESH = pl.DeviceIdType.MESH

BIG_WEIGHTS = ["sb_w_qkv", "sb_w_o", "mla_w_dkv", "mla_w_ukv", "mla_w_dq", "mla_w_uq", "mla_w_o", "mlp_w1", "mlp_w2"]
SHARD_AXIS = {"sb_w_qkv": 2, "sb_w_o": 1, "mla_w_dkv": 0, "mla_w_ukv": 1, "mla_w_dq": 1, "mla_w_uq": 2,
              "mla_w_o": 1, "mlp_w1": 2, "mlp_w2": 1}
SMALL_WEIGHTS = ["attn_norm", "mlp_norm", "kv_norm", "mla_kv_lat_norm", "mla_q_lat_norm", "final_norm"]
ALL_WEIGHTS = ["attn_norm", "mlp_norm", "sb_w_qkv", "sb_w_o", "kv_norm", "mla_w_dkv", "mla_kv_lat_norm", "mla_w_ukv",
               "mla_w_dq", "mla_q_lat_norm", "mla_w_uq", "mla_w_o", "mlp_w1", "mlp_w2", "final_norm"]


def _dot(a, b, dims):
    return lax.dot_general(a, b, (dims, ((), ())), preferred_element_type=F32)


def _dot_nn(a, b):
    return _dot(a, b, ((1,), (0,)))


def _dot_nt(a, b):
    return _dot(a, b, ((1,), (1,)))


def _dot_tn(a, b):
    return _dot(a, b, ((0,), (0,)))


def _pick_block(n, target):
    if n <= target:
        return n
    best = max(b for b in range(LANES, target + 1, LANES) if n % b == 0)
    return best


MM_ROWS = 512
MM_COLS = 1024
MM_DEPTH = 4096
MM_DEPTH_TN = 1024


def _mm(a, b, *, name, dims="nn", epilogue=None, extras=(), out_dtypes=(BF16,), column_sum=False):
    if dims == "nn":
        (m, k), (k2, n) = a.shape, b.shape
    elif dims == "nt":
        (m, k), (n, k2) = a.shape, b.shape
    else:
        (k, m), (k2, n) = a.shape, b.shape
    assert k == k2, (name, a.shape, b.shape)
    if dims == "tn":
        bm, bn, bk = _pick_block(m, MM_COLS), _pick_block(n, MM_COLS), _pick_block(k, MM_DEPTH_TN)
    else:
        rows = MM_ROWS if k > MM_DEPTH // 2 else 2 * MM_ROWS
        cols = 2 * MM_COLS if (k <= MM_COLS and n >= 4 * MM_COLS) else MM_COLS
        bm, bn, bk = _pick_block(m, rows), _pick_block(n, cols), _pick_block(k, MM_DEPTH)
    nk = k // bk
    if dims == "tn":
        a_spec = pl.BlockSpec((bk, bm), lambda j, i, kk: (kk, i))
    else:
        a_spec = pl.BlockSpec((bm, bk), lambda j, i, kk: (i, kk))
    if dims == "nt":
        b_spec = pl.BlockSpec((bn, bk), lambda j, i, kk: (j, kk))
    else:
        b_spec = pl.BlockSpec((bk, bn), lambda j, i, kk: (kk, j))
    extra_specs = []
    for arr, kind in extras:
        if kind == "tile":
            assert arr.shape == (m, n), (name, arr.shape)
            extra_specs.append(pl.BlockSpec((bm, bn), lambda j, i, kk: (i, j)))
        elif kind == "vec":
            assert arr.shape == (1, n), (name, arr.shape)
            extra_specs.append(pl.BlockSpec((1, bn), lambda j, i, kk: (0, j)))
        else:
            assert arr.shape == (m, LANES), (name, arr.shape)
            extra_specs.append(pl.BlockSpec((bm, LANES), lambda j, i, kk: (i, 0)))
    n_extra = len(extras)
    n_out = len(out_dtypes)
    n_sum = int(column_sum)
    dot = {"nn": _dot_nn, "nt": _dot_nt, "tn": _dot_tn}[dims]

    def body(*refs):
        a_ref, b_ref = refs[0], refs[1]
        extra_refs = refs[2:2 + n_extra]
        out_refs = refs[2 + n_extra:2 + n_extra + n_out]

        def finish(acc):
            outs = (acc,) if epilogue is None else epilogue(acc, *[r[...] for r in extra_refs])
            for o_ref, o in zip(out_refs, outs):
                o_ref[...] = o.astype(o_ref.dtype)
            if column_sum:
                sum_ref = refs[2 + n_extra + n_out]
                first_rows = pl.program_id(1) == 0

                @pl.when(first_rows)
                def _():
                    sum_ref[...] = outs[n_out]

                @pl.when(jnp.logical_not(first_rows))
                def _():
                    sum_ref[...] += outs[n_out]

        part = dot(a_ref[...].astype(BF16), b_ref[...].astype(BF16))
        if nk == 1:
            finish(part)
            return
        acc_ref = refs[-1]
        kk = pl.program_id(2)

        @pl.when(kk == 0)
        def _():
            acc_ref[...] = part

        @pl.when(kk > 0)
        def _():
            acc_ref[...] += part

        @pl.when(kk == nk - 1)
        def _():
            finish(acc_ref[...])

    outs = pl.pallas_call(
        body, name=name, grid=(n // bn, m // bm, nk),
        in_specs=[a_spec, b_spec] + extra_specs,
        out_specs=[pl.BlockSpec((bm, bn), lambda j, i, kk: (i, j)) for _ in range(n_out)]
        + [pl.BlockSpec((1, bn), lambda j, i, kk: (0, j))] * n_sum,
        out_shape=[jax.ShapeDtypeStruct((m, n), dt) for dt in out_dtypes] + [jax.ShapeDtypeStruct((1, n), F32)] * n_sum,
        scratch_shapes=[pltpu.VMEM((bm, bn), F32)] if nk > 1 else [],
        compiler_params=pltpu.CompilerParams(
            dimension_semantics=("parallel", "arbitrary" if column_sum else "parallel", "arbitrary"),
            vmem_limit_bytes=VMEM_LIMIT),
    )(a, b, *[arr for arr, _ in extras])
    return outs[0] if n_out + n_sum == 1 else outs


def _epi_add(acc, res):
    return (res + acc,)


def _epi_relu2(acc):
    r = jnp.maximum(acc, 0.0)
    return acc, r * r


def _epi_relu2_grad(acc, u):
    return (acc * (2.0 * jnp.maximum(u.astype(F32), 0.0)),)


def _rope_slab(t, cos_t, sin_t):
    lane = lax.broadcasted_iota(jnp.int32, t.shape, 1)
    partner = jnp.where(lane < ROPE_LO + ROPE_HALF, pltpu.roll(t, LANES - ROPE_HALF, 1), pltpu.roll(t, ROPE_HALF, 1))
    return t * cos_t + partner * sin_t


def _rope_slab_bwd(d, cos_t, sin_t):
    ds = d * sin_t
    lane = lax.broadcasted_iota(jnp.int32, d.shape, 1)
    partner = jnp.where(lane < ROPE_LO + ROPE_HALF, pltpu.roll(ds, LANES - ROPE_HALF, 1), pltpu.roll(ds, ROPE_HALF, 1))
    in_rope = (lane >= ROPE_LO) & (lane < ROPE_LO + MLA_ROPE)
    return d * cos_t + jnp.where(in_rope, partner, 0.0)


def _epi_rope_heads(acc, cos_t, sin_t):
    slabs = [_rope_slab(acc[:, j * LANES:(j + 1) * LANES], cos_t, sin_t) for j in range(acc.shape[1] // LANES)]
    return (jnp.concatenate(slabs, axis=1) * MLA_SCALE,)


def _row_block(s):
    return min(512, s)


def _rms_fwd(x, g, *, name):
    s, d = x.shape
    bm = _row_block(s)

    def body(x_ref, g_ref, o_ref):
        xv = x_ref[...]
        r = lax.rsqrt(jnp.mean(xv * xv, axis=-1, keepdims=True) + NORM_EPS)
        o_ref[...] = ((xv * r) * g_ref[...]).astype(o_ref.dtype)

    return pl.pallas_call(
        body, name=name, grid=(s // bm,),
        in_specs=[pl.BlockSpec((bm, d), lambda i: (i, 0)), pl.BlockSpec((1, d), lambda i: (0, 0))],
        out_specs=pl.BlockSpec((bm, d), lambda i: (i, 0)),
        out_shape=jax.ShapeDtypeStruct((s, d), BF16),
        compiler_params=pltpu.CompilerParams(dimension_semantics=("parallel",), vmem_limit_bytes=VMEM_LIMIT),
    )(x, g.reshape(1, d))


def _rms_bwd_math(xv, gv, dy):
    r = lax.rsqrt(jnp.mean(xv * xv, axis=-1, keepdims=True) + NORM_EPS)
    xhat = xv * r
    dyg = dy * gv
    mdot = jnp.mean(dyg * xhat, axis=-1, keepdims=True)
    dx = r * (dyg - xhat * mdot)
    dg = jnp.sum(dy * xhat, axis=0, keepdims=True)
    return dx, dg


def _rms_bwd(x, g, dy, dres, *, name, lead_axis=False):
    s, d = x.shape
    bm = _row_block(s)
    has_res = dres is not None

    def body(*refs):
        x_ref, g_ref, dy_ref = refs[:3]
        dres_ref = refs[3] if has_res else None
        dx_ref, dxb_ref, dg_ref = refs[-3:]
        dx, dg = _rms_bwd_math(x_ref[...], g_ref[...], dy_ref[...].astype(F32))
        if has_res:
            dx = dx + dres_ref[...]
        dx_ref[...] = dx
        dxb_ref[...] = dx.astype(BF16)

        @pl.when(pl.program_id(0) == 0)
        def _():
            dg_ref[...] = jnp.zeros_like(dg_ref)

        dg_ref[...] += dg

    row = pl.BlockSpec((bm, d), lambda i: (i, 0))
    vec = pl.BlockSpec((1, d), lambda i: (0, 0))
    ins = [x, g.reshape(1, d), dy] + ([dres] if has_res else [])
    dx_spec, dx_shape = row, (s, d)
    if lead_axis:
        dx_spec, dx_shape = pl.BlockSpec((None, bm, d), lambda i: (0, i, 0)), (1, s, d)
    return pl.pallas_call(
        body, name=name, grid=(s // bm,),
        in_specs=[row, vec, row] + ([row] if has_res else []),
        out_specs=[dx_spec, row, vec],
        out_shape=[jax.ShapeDtypeStruct(dx_shape, F32), jax.ShapeDtypeStruct((s, d), BF16),
                   jax.ShapeDtypeStruct((1, d), F32)],
        compiler_params=pltpu.CompilerParams(dimension_semantics=("arbitrary",), vmem_limit_bytes=VMEM_LIMIT),
    )(*ins)


def _epi_rms_bwd(acc, x, g, dres=None):
    dx, dg = _rms_bwd_math(x, g, acc)
    if dres is not None:
        dx = dx + dres
    return dx, dx, dg


def _mm_rms_bwd(dy_src, w, x, g, dres, *, name):
    d = x.shape[1]
    assert w.shape[0] == d and d <= MM_COLS, (name, w.shape, x.shape)
    extras = [(x, "tile"), (g.reshape(1, d), "vec")] + ([(dres, "tile")] if dres is not None else [])
    return _mm(dy_src, w, name=name, dims="nt", epilogue=_epi_rms_bwd, extras=extras, out_dtypes=(F32, BF16),
               column_sum=True)


def _loss_bwd(x, g, target, *, name):
    s, d = x.shape
    bm = _row_block(s)

    def body(x_ref, g_ref, t_ref, loss_ref, dx_ref, dxb_ref, dg_ref):
        xv, gv = x_ref[...], g_ref[...]
        r = lax.rsqrt(jnp.mean(xv * xv, axis=-1, keepdims=True) + NORM_EPS)
        err = (xv * r) * gv - t_ref[...]
        dx, dg = _rms_bwd_math(xv, gv, err * (1.0 / d))
        dx_ref[...] = dx
        dxb_ref[...] = dx.astype(BF16)

        @pl.when(pl.program_id(0) == 0)
        def _():
            dg_ref[...] = jnp.zeros_like(dg_ref)
            loss_ref[...] = jnp.zeros_like(loss_ref)

        dg_ref[...] += dg
        loss_ref[...] += jnp.sum(jnp.mean(err * err, axis=-1, keepdims=True), axis=0, keepdims=True) * 0.5

    row = pl.BlockSpec((bm, d), lambda i: (i, 0))
    vec = pl.BlockSpec((1, d), lambda i: (0, 0))
    assert target.shape == (1, s, d), target.shape
    return pl.pallas_call(
        body, name=name, grid=(s // bm,),
        in_specs=[row, vec, pl.BlockSpec((None, bm, d), lambda i: (0, i, 0))],
        out_specs=[pl.BlockSpec((8, LANES), lambda i: (0, 0)), row, row, vec],
        out_shape=[jax.ShapeDtypeStruct((8, LANES), F32), jax.ShapeDtypeStruct((s, d), F32),
                   jax.ShapeDtypeStruct((s, d), BF16), jax.ShapeDtypeStruct((1, d), F32)],
        compiler_params=pltpu.CompilerParams(dimension_semantics=("arbitrary",), vmem_limit_bytes=VMEM_LIMIT),
    )(x, g.reshape(1, d), target)


def _kv_prep(down, g, cos_t, sin_t, *, name):
    s, w = down.shape
    bm = _row_block(s)

    def body(d_ref, g_ref, c_ref, s_ref, o_ref):
        lat = d_ref[:, :MLA_KV_RANK]
        r = lax.rsqrt(jnp.mean(lat * lat, axis=-1, keepdims=True) + NORM_EPS)
        o_ref[:, :MLA_KV_RANK] = ((lat * r) * g_ref[...]).astype(BF16)
        o_ref[:, MLA_KV_RANK:] = _rope_slab(d_ref[:, MLA_KV_RANK:], c_ref[...], s_ref[...]).astype(BF16)

    row = pl.BlockSpec((bm, w), lambda i: (i, 0))
    tab = pl.BlockSpec((bm, LANES), lambda i: (i, 0))
    return pl.pallas_call(
        body, name=name, grid=(s // bm,),
        in_specs=[row, pl.BlockSpec((1, MLA_KV_RANK), lambda i: (0, 0)), tab, tab],
        out_specs=row, out_shape=jax.ShapeDtypeStruct((s, w), BF16),
        compiler_params=pltpu.CompilerParams(dimension_semantics=("parallel",), vmem_limit_bytes=VMEM_LIMIT),
    )(down, g.reshape(1, MLA_KV_RANK), cos_t, sin_t)


def _kv_prep_bwd(down, g, cos_t, sin_t, dcat, *, name):
    s, w = down.shape
    bm = _row_block(s)

    def body(d_ref, g_ref, c_ref, s_ref, dc_ref, o_ref, dg_ref):
        dlat, dg = _rms_bwd_math(d_ref[:, :MLA_KV_RANK], g_ref[...], dc_ref[:, :MLA_KV_RANK])
        o_ref[:, :MLA_KV_RANK] = dlat.astype(BF16)
        o_ref[:, MLA_KV_RANK:] = _rope_slab_bwd(dc_ref[:, MLA_KV_RANK:], c_ref[...], s_ref[...]).astype(BF16)

        @pl.when(pl.program_id(0) == 0)
        def _():
            dg_ref[...] = jnp.zeros_like(dg_ref)

        dg_ref[...] += dg

    row = pl.BlockSpec((bm, w), lambda i: (i, 0))
    tab = pl.BlockSpec((bm, LANES), lambda i: (i, 0))
    vec = pl.BlockSpec((1, MLA_KV_RANK), lambda i: (0, 0))
    return pl.pallas_call(
        body, name=name, grid=(s // bm,),
        in_specs=[row, vec, tab, tab, row],
        out_specs=[row, vec],
        out_shape=[jax.ShapeDtypeStruct((s, w), BF16), jax.ShapeDtypeStruct((1, MLA_KV_RANK), F32)],
        compiler_params=pltpu.CompilerParams(dimension_semantics=("arbitrary",), vmem_limit_bytes=VMEM_LIMIT),
    )(down, g.reshape(1, MLA_KV_RANK), cos_t, sin_t, dcat)


def _split_bf16(v):
    hi = v.astype(BF16)
    lo = (v - hi.astype(F32)).astype(BF16)
    return hi, lo


def _suffix_matrices(n):
    row = lax.broadcasted_iota(jnp.int32, (n, n), 0)
    col = lax.broadcasted_iota(jnp.int32, (n, n), 1)
    incl = (row >= col).astype(BF16)
    return (row > col).astype(BF16), jnp.concatenate([incl, incl], axis=0)


def _suffix_sum(v, matrix):
    hi, lo = _split_bf16(v)
    return _dot_nn(jnp.concatenate([hi, lo], axis=1), matrix)


def _block_positions(qi, kb, bq, bk, r0, r1):
    row = qi * bq + r0 + lax.broadcasted_iota(jnp.int32, (r1 - r0, bk), 0)
    col = kb * bk + lax.broadcasted_iota(jnp.int32, (r1 - r0, bk), 1)
    return row, col


def _att_blocks(s, key_block=ATT_K_BLOCK):
    bq, bk = min(ATT_Q_BLOCK, s), min(key_block, s)
    return bq, bk, s // bq, bq // bk


def _sweep(qi, ratio, bk, step, unroll=2, alive=None):
    bq = ratio * bk
    for d in range(ratio):
        kb, r0 = (qi + 1) * ratio - 1 - d, (ratio - 1 - d) * bk
        near = bq if alive is None else min(r0 + 2 * bk, bq)
        step(kb, True, r0, near)
        if near < bq:
            pl.when(alive(near))(functools.partial(step, kb, False, near, bq))
    unroll = unroll if ratio % unroll == 0 else 1
    trips = qi * (ratio // unroll)

    def trip(i):
        for u in range(unroll):
            kb = qi * ratio - 1 - (i * unroll + u)
            if alive is None or ratio == 1:
                step(kb, False, 0, bq)
            else:
                step(kb, False, 0, bk)
                pl.when(alive(bk))(functools.partial(step, kb, False, bk, bq))

    if alive is None:
        lax.fori_loop(0, trips, lambda i, carry: (trip(i), carry)[1], 0)
    else:
        lax.while_loop(lambda i: jnp.logical_and(i < trips, alive(0)), lambda i: (trip(i), i + 1)[1], 0)


def _stick_left(c_ref, r0):
    return jnp.max(c_ref[r0:, :]) > SB_DEAD_LOG


def _sb_logs(q, k):
    z = _dot_nt(q, k)
    lb = jnp.minimum(z, 0.0) - jnp.log(1.0 + jnp.exp(-jnp.abs(z)))
    return lb, lb - z


def _sb_fwd(qkv, heads, *, name, ride=None):
    s = qkv.shape[0]
    bq, bk, nq, ratio = _att_blocks(s)
    riding = ride is not None

    def body(*refs):
        if riding:
            q_ref, k_ref, v_ref, w_ref, o_ref, gath_ref, acc_ref, c_ref = refs[:8]
            first = jnp.logical_and(pl.program_id(0) == 0, pl.program_id(1) == 0)
            last = jnp.logical_and(pl.program_id(0) == heads - 1, pl.program_id(1) == nq - 1)
            pl.when(first)(functools.partial(_all_gather_start, w_ref, gath_ref, refs[8:]))
        else:
            q_ref, k_ref, v_ref, o_ref, acc_ref, c_ref = refs
        qi = pl.program_id(1)
        q = q_ref[...] * SB_SCALE
        m_strict, _ = _suffix_matrices(bk)
        acc_ref[...] = jnp.zeros_like(acc_ref)
        c_ref[...] = jnp.zeros_like(c_ref)

        def step(kb, masked, r0, r1):
            rows = pl.ds(pl.multiple_of(kb * bk, bk), bk)
            mine = pl.ds(r0, r1 - r0)
            k, v = k_ref[rows, :], v_ref[rows, :]
            lb, lk = _sb_logs(q[r0:r1], k)
            if masked:
                row, col = _block_positions(qi, kb, bq, bk, r0, r1)
                causal = col < row
                lk = jnp.where(causal, lk, 0.0)
            c = c_ref[mine, :]
            w = jnp.exp(lb + _dot_nn(lk.astype(BF16), m_strict) + jnp.tile(c, (1, bk // LANES)))
            if masked:
                w = jnp.where(causal, w, 0.0)
            acc_ref[mine, :] += _dot_nn(w.astype(BF16), v)
            c_ref[mine, :] = c + jnp.sum(lk, axis=-1, keepdims=True)

        _sweep(qi, ratio, bk, step, unroll=1, alive=functools.partial(_stick_left, c_ref))
        o_ref[...] = acc_ref[...].astype(o_ref.dtype)
        if riding:
            pl.when(last)(functools.partial(_all_gather_finish, w_ref, gath_ref, refs[8:]))

    hbm = pl.BlockSpec(memory_space=pltpu.HBM)
    o_spec = pl.BlockSpec((bq, LANES), lambda h, i: (i, h))
    o_shape = jax.ShapeDtypeStruct((s, heads * LANES), F32)
    return pl.pallas_call(
        body, name=name, grid=(heads, nq),
        in_specs=[pl.BlockSpec((bq, LANES), lambda h, i: (i, h)),
                  pl.BlockSpec((s, LANES), lambda h, i: (0, heads + h)),
                  pl.BlockSpec((s, LANES), lambda h, i: (0, 2 * heads + h))] + ([hbm] if riding else []),
        out_specs=[o_spec, hbm] if riding else o_spec,
        out_shape=[o_shape, jax.ShapeDtypeStruct((N_CHIPS,) + ride.shape, ride.dtype)] if riding else o_shape,
        scratch_shapes=[pltpu.VMEM((bq, LANES), F32), pltpu.VMEM((bq, LANES), F32)]
        + (_all_gather_sems() if riding else []),
        compiler_params=pltpu.CompilerParams(dimension_semantics=("arbitrary", "arbitrary"),
                                             vmem_limit_bytes=VMEM_LIMIT, has_side_effects=riding),
    )(*([qkv, qkv, qkv] + ([ride] if riding else [])))


def _sb_bwd(qkv, o, do, heads, *, name, ride=None):
    s = qkv.shape[0]
    bq, bk, nq, ratio = _att_blocks(s)
    riding = ride is not None

    def body(*refs):
        if riding:
            (q_ref, k_ref, v_ref, o_ref, do_ref, g_ref, dq_ref, dk_ref, dv_ref, got_ref,
             dq_acc, dk_acc, dv_acc, c_ref, e_ref) = refs[:15]
            first = jnp.logical_and(pl.program_id(0) == 0, pl.program_id(1) == 0)
            last = jnp.logical_and(pl.program_id(0) == heads - 1, pl.program_id(1) == nq - 1)
            pl.when(first)(functools.partial(_exchange_start, g_ref, got_ref, refs[15:]))
        else:
            q_ref, k_ref, v_ref, o_ref, do_ref, dq_ref, dk_ref, dv_ref, dq_acc, dk_acc, dv_acc, c_ref, e_ref = refs
        qi = pl.program_id(1)

        @pl.when(qi == 0)
        def _():
            dk_acc[...] = jnp.zeros_like(dk_acc)
            dv_acc[...] = jnp.zeros_like(dv_acc)

        q = q_ref[...] * SB_SCALE
        do = do_ref[...]
        q_t, do_t = q.T, do.T
        total = jnp.sum(do.astype(F32) * o_ref[...].astype(F32), axis=-1, keepdims=True)
        m_strict, m_incl = _suffix_matrices(bk)
        dq_acc[...] = jnp.zeros_like(dq_acc)
        c_ref[...] = jnp.zeros_like(c_ref)
        e_ref[...] = jnp.broadcast_to(total, e_ref.shape)
        reps = (1, bk // LANES)

        def step(kb, masked, r0, r1):
            rows = pl.ds(pl.multiple_of(kb * bk, bk), bk)
            mine = pl.ds(r0, r1 - r0)
            k, v = k_ref[rows, :], v_ref[rows, :]
            qs, dos = q[r0:r1], do[r0:r1]
            lb, lk_all = _sb_logs(qs, k)
            lk = lk_all
            if masked:
                row, col = _block_positions(qi, kb, bq, bk, r0, r1)
                causal = col < row
                lk = jnp.where(causal, lk_all, 0.0)
            c = c_ref[mine, :]
            w = jnp.exp(lb + _dot_nn(lk.astype(BF16), m_strict) + jnp.tile(c, reps))
            if masked:
                w = jnp.where(causal, w, 0.0)
            wb = w.astype(BF16)
            g = wb.astype(F32) * _dot_nt(dos, v)
            e = e_ref[mine, :]
            g_left = jnp.tile(e, reps) - _suffix_sum(g, m_incl)
            da = g * jnp.exp(lk_all) - jnp.exp(lb) * g_left
            if masked:
                da = jnp.where(causal, da, 0.0)
            dab = da.astype(BF16)
            dq_acc[mine, :] += _dot_nn(dab, k)
            dk_acc[:, rows] += _dot_nn(q_t[:, r0:r1], dab)
            dv_acc[:, rows] += _dot_nn(do_t[:, r0:r1], wb)
            e_ref[mine, :] = e - jnp.sum(g, axis=-1, keepdims=True)
            c_ref[mine, :] = c + jnp.sum(lk, axis=-1, keepdims=True)

        _sweep(qi, ratio, bk, step, unroll=1, alive=functools.partial(_stick_left, c_ref))
        dq_ref[...] = (dq_acc[...] * SB_SCALE).astype(dq_ref.dtype)

        @pl.when(qi == nq - 1)
        def _():
            dk_ref[...] = dk_acc[...].T.astype(dk_ref.dtype)
            dv_ref[...] = dv_acc[...].T.astype(dv_ref.dtype)

        if riding:
            pl.when(last)(functools.partial(_exchange_finish, g_ref, got_ref, refs[15:]))

    blk = pl.BlockSpec((bq, LANES), lambda h, i: (i, h))
    full = pl.BlockSpec((s, LANES), lambda h, i: (0, h))
    hbm = pl.BlockSpec(memory_space=pltpu.HBM)
    shape = jax.ShapeDtypeStruct((s, heads * LANES), BF16)
    return pl.pallas_call(
        body, name=name, grid=(heads, nq),
        in_specs=[blk,
                  pl.BlockSpec((s, LANES), lambda h, i: (0, heads + h)),
                  pl.BlockSpec((s, LANES), lambda h, i: (0, 2 * heads + h)),
                  blk, blk] + ([hbm] if riding else []),
        out_specs=[blk, full, full] + ([hbm] if riding else []),
        out_shape=[shape, shape, shape] + ([jax.ShapeDtypeStruct(ride.shape, ride.dtype)] if riding else []),
        scratch_shapes=[pltpu.VMEM((bq, LANES), F32), pltpu.VMEM((LANES, s), F32), pltpu.VMEM((LANES, s), F32),
                        pltpu.VMEM((bq, LANES), F32), pltpu.VMEM((bq, LANES), F32)]
        + (_exchange_sems() if riding else []),
        compiler_params=pltpu.CompilerParams(dimension_semantics=("arbitrary", "arbitrary"),
                                             vmem_limit_bytes=VMEM_LIMIT, has_side_effects=riding),
    )(*([qkv, qkv, qkv, o, do] + ([ride] if riding else [])))


def _chunk_allowed(qi, kb, bq, bk, r0, r1):
    row, col = _block_positions(qi, kb, bq, bk, r0, r1)
    return (col // CHUNK) <= (row // CHUNK)


def _mla_fwd(q, kv, heads, *, name):
    s = q.shape[0]
    bq, bk, nq, ratio = _att_blocks(s, MLA_FWD_K_BLOCK)
    reps = (1, bk // LANES)

    def body(q_ref, k_ref, v_ref, o_ref, lse_ref, acc_ref, m_ref, l_ref):
        qi = pl.program_id(1)
        qv = q_ref[...]
        acc_ref[...] = jnp.zeros_like(acc_ref)
        m_ref[...] = jnp.full_like(m_ref, NEG_BIG)
        l_ref[...] = jnp.zeros_like(l_ref)

        def step(kb, masked, r0, r1):
            rows = pl.ds(pl.multiple_of(kb * bk, bk), bk)
            mine = pl.ds(r0, r1 - r0)
            k, v = k_ref[rows, :], v_ref[rows, :]
            sc = _dot_nt(qv[r0:r1], k)
            if masked:
                allowed = _chunk_allowed(qi, kb, bq, bk, r0, r1)
                sc = jnp.where(allowed, sc, NEG_BIG)
            m_old = m_ref[mine, :]
            m_new = jnp.maximum(m_old, jnp.max(sc, axis=-1, keepdims=True))
            p = jnp.exp(sc - jnp.tile(m_new, reps))
            alpha = jnp.exp(m_old - m_new)
            l_ref[mine, :] = alpha * l_ref[mine, :] + jnp.sum(p, axis=-1, keepdims=True)
            acc_ref[mine, :] = alpha * acc_ref[mine, :] + _dot_nn(p.astype(BF16), v)
            m_ref[mine, :] = m_new

        _sweep(qi, ratio, bk, step)
        o_ref[...] = (acc_ref[...] / l_ref[...]).astype(o_ref.dtype)
        lse_ref[...] = m_ref[...] + jnp.log(l_ref[...])

    blk = pl.BlockSpec((bq, LANES), lambda h, i: (i, h))
    return pl.pallas_call(
        body, name=name, grid=(heads, nq),
        in_specs=[blk,
                  pl.BlockSpec((s, LANES), lambda h, i: (0, h)),
                  pl.BlockSpec((s, LANES), lambda h, i: (0, heads + h))],
        out_specs=[blk, blk],
        out_shape=[jax.ShapeDtypeStruct((s, heads * LANES), BF16), jax.ShapeDtypeStruct((s, heads * LANES), F32)],
        scratch_shapes=[pltpu.VMEM((bq, LANES), F32), pltpu.VMEM((bq, LANES), F32), pltpu.VMEM((bq, LANES), F32)],
        compiler_params=pltpu.CompilerParams(dimension_semantics=("parallel", "arbitrary"),
                                             vmem_limit_bytes=VMEM_LIMIT),
    )(q, kv, kv)


def _mla_bwd(q, kv, o, do, lse, cos_t, sin_t, dkv_init, heads, *, name):
    s = q.shape[0]
    bq, bk, nq, ratio = _att_blocks(s)
    reps = (1, bk // LANES)
    has_init = dkv_init is not None

    def body(*refs):
        q_ref, k_ref, v_ref, o_ref, do_ref, lse_ref, c_ref, s_ref = refs[:8]
        ki_ref, vi_ref = (refs[8], refs[9]) if has_init else (None, None)
        dq_ref, dk_ref, dv_ref, dq_acc, dk_acc, dv_acc = refs[-6:]
        qi = pl.program_id(1)

        @pl.when(qi == 0)
        def _():
            if has_init:
                dk_acc[...] = ki_ref[...].astype(F32).T
                dv_acc[...] = vi_ref[...].astype(F32).T
            else:
                dk_acc[...] = jnp.zeros_like(dk_acc)
                dv_acc[...] = jnp.zeros_like(dv_acc)

        qv = q_ref[...]
        do = do_ref[...]
        q_t, do_t = qv.T, do.T
        delta = jnp.sum(do.astype(F32) * o_ref[...].astype(F32), axis=-1, keepdims=True)
        lse_wide = jnp.tile(lse_ref[...], reps)
        dq_acc[...] = jnp.zeros_like(dq_acc)

        def step(kb, masked, r0, r1):
            rows = pl.ds(pl.multiple_of(kb * bk, bk), bk)
            k, v = k_ref[rows, :], v_ref[rows, :]
            qs, dos = qv[r0:r1], do[r0:r1]
            p = jnp.exp(_dot_nt(qs, k) - lse_wide[r0:r1])
            if masked:
                p = jnp.where(_chunk_allowed(qi, kb, bq, bk, r0, r1), p, 0.0)
            ds = (p * (_dot_nt(dos, v) - delta[r0:r1])).astype(BF16)
            dq_acc[pl.ds(r0, r1 - r0), :] += _dot_nn(ds, k)
            dk_acc[:, rows] += _dot_nn(q_t[:, r0:r1], ds)
            dv_acc[:, rows] += _dot_nn(do_t[:, r0:r1], p.astype(BF16))

        _sweep(qi, ratio, bk, step)
        dq_ref[...] = _rope_slab_bwd(dq_acc[...] * MLA_SCALE, c_ref[...], s_ref[...]).astype(dq_ref.dtype)

        @pl.when(qi == nq - 1)
        def _():
            dk_ref[...] = dk_acc[...].T.astype(dk_ref.dtype)
            dv_ref[...] = dv_acc[...].T.astype(dv_ref.dtype)

    blk = pl.BlockSpec((bq, LANES), lambda h, i: (i, h))
    tab = pl.BlockSpec((bq, LANES), lambda h, i: (i, 0))
    k_full = pl.BlockSpec((s, LANES), lambda h, i: (0, h))
    v_full = pl.BlockSpec((s, LANES), lambda h, i: (0, heads + h))
    shape = jax.ShapeDtypeStruct((s, heads * LANES), BF16)
    ins = [q, kv, kv, o, do, lse, cos_t, sin_t] + ([dkv_init, dkv_init] if has_init else [])
    dq, dk, dv = pl.pallas_call(
        body, name=name, grid=(heads, nq),
        in_specs=[blk, k_full, v_full, blk, blk, blk, tab, tab] + ([k_full, v_full] if has_init else []),
        out_specs=[blk, k_full, k_full],
        out_shape=[shape, shape, shape],
        scratch_shapes=[pltpu.VMEM((bq, LANES), F32), pltpu.VMEM((LANES, s), F32), pltpu.VMEM((LANES, s), F32)],
        compiler_params=pltpu.CompilerParams(dimension_semantics=("arbitrary", "arbitrary"),
                                             vmem_limit_bytes=VMEM_LIMIT),
    )(*ins)
    return dq, jnp.concatenate([dk, dv], axis=1)


def _pad_last(a, width):
    return jnp.pad(a, [(0, 0)] * (a.ndim - 1) + [(0, width - a.shape[-1])])


def _pad_qkv(w, heads):
    d = w.shape[0]
    return _pad_last(w.reshape(d, 3 * heads, SB_HEAD_DIM), LANES).reshape(d, 3 * heads * LANES)


def _unpad_qkv(g, heads):
    d = g.shape[0]
    return g.reshape(d, 3 * heads, LANES)[:, :, :SB_HEAD_DIM].reshape(d, 3 * heads * SB_HEAD_DIM)


def _pad_o(w, heads):
    d = w.shape[1]
    w = w.reshape(heads, SB_HEAD_DIM, d)
    return jnp.pad(w, [(0, 0), (0, LANES - SB_HEAD_DIM), (0, 0)]).reshape(heads * LANES, d)


def _unpad_o(g, heads):
    d = g.shape[1]
    return g.reshape(heads, LANES, d)[:, :SB_HEAD_DIM, :].reshape(heads * SB_HEAD_DIM, d)


def _pad_uq(w, heads):
    r = w.shape[0]
    return _pad_last(w.reshape(r, heads, MLA_NOPE + MLA_ROPE), LANES).reshape(r, heads * LANES)


def _unpad_uq(g, heads):
    r = g.shape[0]
    return g.reshape(r, heads, LANES)[:, :, :MLA_NOPE + MLA_ROPE].reshape(r, heads * (MLA_NOPE + MLA_ROPE))


def _pad_dkv(w):
    d = w.shape[0]
    rope = jnp.zeros((d, LANES), w.dtype).at[:, ROPE_LO:ROPE_LO + MLA_ROPE].set(w[:, MLA_KV_RANK:])
    return jnp.concatenate([w[:, :MLA_KV_RANK], rope], axis=1)


def _unpad_dkv(g):
    return jnp.concatenate([g[:, :MLA_KV_RANK], g[:, MLA_KV_RANK + ROPE_LO:MLA_KV_RANK + ROPE_LO + MLA_ROPE]], axis=1)


def _pad_ukv(w, heads):
    w = w.reshape(MLA_KV_RANK, heads, 2, MLA_NOPE)
    k_part = _pad_last(w[:, :, 0, :], LANES).reshape(MLA_KV_RANK, heads * LANES)
    v_part = _pad_last(w[:, :, 1, :], LANES).reshape(MLA_KV_RANK, heads * LANES)
    lane = jnp.arange(LANES)
    place = ((lane[:, None] == lane[None, :]) & (lane[:, None] >= ROPE_LO) & (lane[:, None] < ROPE_LO + MLA_ROPE))
    place = jnp.tile(place.astype(w.dtype), (1, heads))
    top = jnp.concatenate([k_part, v_part], axis=1)
    bottom = jnp.concatenate([place, jnp.zeros_like(place)], axis=1)
    return jnp.concatenate([top, bottom], axis=0)


def _unpad_ukv(g, heads):
    g = g[:MLA_KV_RANK]
    k_part = g[:, :heads * LANES].reshape(MLA_KV_RANK, heads, LANES)[:, :, :MLA_NOPE]
    v_part = g[:, heads * LANES:].reshape(MLA_KV_RANK, heads, LANES)[:, :, :MLA_V]
    return jnp.stack([k_part, v_part], axis=2).reshape(MLA_KV_RANK, heads * (MLA_NOPE + MLA_V))


def _rope_tables(positions):
    inv_freq = ROPE_THETA ** (-jnp.arange(0, MLA_ROPE, 2, dtype=F32) / MLA_ROPE)
    ang = positions.astype(F32)[:, None] * inv_freq
    cos, sin = jnp.cos(ang), jnp.sin(ang)
    s = positions.shape[0]
    cos_t = jnp.ones((s, LANES), F32).at[:, ROPE_LO:ROPE_LO + MLA_ROPE].set(jnp.concatenate([cos, cos], axis=1))
    sin_t = jnp.zeros((s, LANES), F32).at[:, ROPE_LO:ROPE_LO + MLA_ROPE].set(jnp.concatenate([-sin, sin], axis=1))
    return cos_t, sin_t


def _local_step(x, positions, target, qkv_w0, norms, rest_weights, ride=None, early_reduce=None):
    s, d = x.shape
    heads = d // SB_HEAD_DIM
    cos_t, sin_t = _rope_tables(positions)

    h_first = _rms_fwd(x, norms["attn_norm"][0], name="l0_attn_norm")
    qkv_first = _mm(h_first, _pad_qkv(qkv_w0, heads), name="l0_qkv")
    if ride is None:
        o_first, gathered = _sb_fwd(qkv_first, heads, name="l0_sb_fwd"), None
    else:
        o_first, gathered = _sb_fwd(qkv_first, heads, name="l0_sb_fwd", ride=ride)
    w = rest_weights(gathered)
    n_a = w["sb_w_qkv"].shape[0]
    n_b = w["mla_w_dq"].shape[0]
    depth = n_a + n_b

    wqkv = [_pad_qkv(w["sb_w_qkv"][l], heads) for l in range(n_a)]
    wo_a = [_pad_o(w["sb_w_o"][l], heads) for l in range(n_a)]
    wdkv = _pad_dkv(w["mla_w_dkv"])
    wkv = _pad_ukv(w["mla_w_ukv"], heads)
    wdq = [w["mla_w_dq"][j] for j in range(n_b)]
    wuq = [_pad_uq(w["mla_w_uq"][j], heads) for j in range(n_b)]
    wo_b = [_pad_o(w["mla_w_o"][j], heads) for j in range(n_b)]
    w1 = [w["mlp_w1"][l] for l in range(depth)]
    w2 = [w["mlp_w2"][l] for l in range(depth)]

    saved = []
    kv_saved = None
    kv = None
    for l in range(depth):
        t = f"l{l}"
        sv = {"x_in": x}
        h = h_first if l == 0 else _rms_fwd(x, norms["attn_norm"][l], name=f"{t}_attn_norm")
        sv["h"] = h
        if l < n_a:
            if l == 0:
                qkv, o = qkv_first, o_first
            else:
                qkv = _mm(h, wqkv[l], name=f"{t}_qkv")
                o = _sb_fwd(qkv, heads, name=f"{t}_sb_fwd")
            sv["qkv"], sv["o"] = qkv, o
            x = _mm(o, wo_a[l], name=f"{t}_attn_out", epilogue=_epi_add, extras=[(x, "tile")], out_dtypes=(F32,))
        else:
            j = l - n_a
            if j == 0:
                hk = _rms_fwd(x, norms["kv_norm"], name="kv_norm")
                down = _mm(hk, wdkv, name="kv_down", out_dtypes=(F32,))
                cat = _kv_prep(down, norms["mla_kv_lat_norm"], cos_t, sin_t, name="kv_prep")
                kv = _mm(cat, wkv, name="kv_up")
                kv_saved = {"x_in": x, "hk": hk, "down": down, "cat": cat}
            cq0 = _mm(h, wdq[j], name=f"{t}_q_down", out_dtypes=(F32,))
            cq = _rms_fwd(cq0, norms["mla_q_lat_norm"][j], name=f"{t}_q_lat_norm")
            q = _mm(cq, wuq[j], name=f"{t}_q_up", epilogue=_epi_rope_heads, extras=[(cos_t, "row"), (sin_t, "row")])
            o, lse = _mla_fwd(q, kv, heads, name=f"{t}_mla_fwd")
            sv.update(cq0=cq0, cq=cq, q=q, o=o, lse=lse)
            x = _mm(o, wo_b[j], name=f"{t}_attn_out", epilogue=_epi_add, extras=[(x, "tile")], out_dtypes=(F32,))
        sv["x_mid"] = x
        h2 = _rms_fwd(x, norms["mlp_norm"][l], name=f"{t}_mlp_norm")
        u, a = _mm(h2, w1[l], name=f"{t}_mlp_up", epilogue=_epi_relu2, out_dtypes=(BF16, BF16))
        sv.update(h2=h2, u=u, a=a)
        x = _mm(a, w2[l], name=f"{t}_mlp_down", epilogue=_epi_add, extras=[(x, "tile")], out_dtypes=(F32,))
        saved.append(sv)

    loss_slab, dx, dxb, dg_final = _loss_bwd(x, norms["final_norm"], target, name="loss")
    loss = loss_slab[0, 0]

    g_attn_norm, g_mlp_norm = [None] * depth, [None] * depth
    g_qkv, g_o_a = [None] * n_a, [None] * n_a
    g_dq, g_uq, g_o_b, g_qlat = [None] * n_b, [None] * n_b, [None] * n_b, [None] * n_b
    g_w1, g_w2 = [None] * depth, [None] * depth
    dkv = None
    g_kv_norm = g_kv_lat = g_dkv = g_ukv = None
    early_parts = early_got = None

    for l in reversed(range(depth)):
        t = f"l{l}"
        sv = saved[l]
        du = _mm(dxb, w2[l], name=f"{t}_mlp_down_dx", dims="nt", epilogue=_epi_relu2_grad, extras=[(sv["u"], "tile")])
        g_w2[l] = _mm(sv["a"], dxb, name=f"{t}_mlp_down_dw", dims="tn", out_dtypes=(F32,))
        g_w1[l] = _mm(sv["h2"], du, name=f"{t}_mlp_up_dw", dims="tn", out_dtypes=(F32,))
        dx, dxb, g_mlp_norm[l] = _mm_rms_bwd(du, w1[l], sv["x_mid"], norms["mlp_norm"][l], dx, name=f"{t}_mlp_up_dx")
        if l < n_a:
            do = _mm(dxb, wo_a[l], name=f"{t}_attn_out_dx", dims="nt")
            g_o_a[l] = _unpad_o(_mm(sv["o"], dxb, name=f"{t}_attn_out_dw", dims="tn", out_dtypes=(F32,)), heads)
            if l == 0 and early_reduce is not None:
                early_parts = early_reduce({
                    "sb_w_qkv": g_qkv[1:], "sb_w_o": g_o_a, "mla_w_dkv": g_dkv, "mla_w_ukv": g_ukv, "mla_w_dq": g_dq,
                    "mla_w_uq": g_uq, "mla_w_o": g_o_b, "mlp_w1": g_w1, "mlp_w2": g_w2})
                dq, dk, dv, early_got = _sb_bwd(sv["qkv"], sv["o"], do, heads, name=f"{t}_sb_bwd", ride=early_parts)
            else:
                dq, dk, dv = _sb_bwd(sv["qkv"], sv["o"], do, heads, name=f"{t}_sb_bwd")
            dqkv = jnp.concatenate([dq, dk, dv], axis=1)
            g_qkv[l] = _unpad_qkv(_mm(sv["h"], dqkv, name=f"{t}_qkv_dw", dims="tn", out_dtypes=(F32,)), heads)
            dh_src, dh_w = dqkv, wqkv[l]
        else:
            j = l - n_a
            do = _mm(dxb, wo_b[j], name=f"{t}_attn_out_dx", dims="nt")
            g_o_b[j] = _unpad_o(_mm(sv["o"], dxb, name=f"{t}_attn_out_dw", dims="tn", out_dtypes=(F32,)), heads)
            dq, dkv = _mla_bwd(sv["q"], kv, sv["o"], do, sv["lse"], cos_t, sin_t, dkv, heads, name=f"{t}_mla_bwd")
            g_uq[j] = _unpad_uq(_mm(sv["cq"], dq, name=f"{t}_q_up_dw", dims="tn", out_dtypes=(F32,)), heads)
            _, dcq0, g_qlat[j] = _mm_rms_bwd(dq, wuq[j], sv["cq0"], norms["mla_q_lat_norm"][j], None,
                                             name=f"{t}_q_up_dx")
            g_dq[j] = _mm(sv["h"], dcq0, name=f"{t}_q_down_dw", dims="tn", out_dtypes=(F32,))
            dh_src, dh_w = dcq0, wdq[j]
        if l == 0:
            dh = _mm(dh_src, dh_w, name=f"{t}_attn_in_dx", dims="nt", out_dtypes=(F32,))
            dx, dxb, g_attn_norm[l] = _rms_bwd(sv["x_in"], norms["attn_norm"][l], dh, dx, name=f"{t}_attn_norm_bwd",
                                               lead_axis=True)
        else:
            dx, dxb, g_attn_norm[l] = _mm_rms_bwd(dh_src, dh_w, sv["x_in"], norms["attn_norm"][l], dx,
                                                  name=f"{t}_attn_in_dx")
        if l == n_a:
            ks = kv_saved
            dcat = _mm(dkv, wkv, name="kv_up_dx", dims="nt", out_dtypes=(F32,))
            g_ukv = _unpad_ukv(_mm(ks["cat"], dkv, name="kv_up_dw", dims="tn", out_dtypes=(F32,)), heads)
            ddown, g_kv_lat = _kv_prep_bwd(ks["down"], norms["mla_kv_lat_norm"], cos_t, sin_t, dcat, name="kv_prep_bwd")
            g_dkv = _unpad_dkv(_mm(ks["hk"], ddown, name="kv_down_dw", dims="tn", out_dtypes=(F32,)))
            dx, dxb, g_kv_norm = _mm_rms_bwd(ddown, wdkv, ks["x_in"], norms["kv_norm"], dx, name="kv_down_dx")

    grads = {
        "attn_norm": jnp.concatenate(g_attn_norm, axis=0), "mlp_norm": jnp.concatenate(g_mlp_norm, axis=0),
        "sb_w_qkv": g_qkv, "sb_w_o": g_o_a,
        "kv_norm": g_kv_norm[0], "mla_w_dkv": g_dkv, "mla_kv_lat_norm": g_kv_lat[0], "mla_w_ukv": g_ukv,
        "mla_w_dq": g_dq, "mla_q_lat_norm": jnp.concatenate(g_qlat, axis=0),
        "mla_w_uq": g_uq, "mla_w_o": g_o_b,
        "mlp_w1": g_w1, "mlp_w2": g_w2, "final_norm": dg_final[0],
    }
    return loss, dx, grads, early_parts, early_got


def _flat_rows(n_elems):
    per_block = FLAT_COLS * FLAT_ROW_BLOCK * 2
    return -(-n_elems // per_block) * FLAT_ROW_BLOCK * 2


def _row_blocks(arrays, dtype):
    for a in arrays:
        assert a.size % FLAT_COLS == 0, a.shape
    blocks = [a.astype(dtype).reshape(-1, FLAT_COLS) for a in arrays]
    used = sum(b.shape[0] for b in blocks)
    rows = _flat_rows(used * FLAT_COLS)
    return blocks + [jnp.zeros((rows - used, FLAT_COLS), dtype)], rows


def _pack(arrays, dtype):
    blocks, _ = _row_blocks(arrays, dtype)
    return jnp.concatenate(blocks, axis=0)


def _pack_chips(per_chip, dtype):
    blocks, rows = [], 0
    for arrays in per_chip:
        chip_blocks, rows = _row_blocks(arrays, dtype)
        blocks += chip_blocks
    return jnp.concatenate(blocks, axis=0).reshape(len(per_chip), rows, FLAT_COLS)


def _unpack(flat, shapes):
    out, row = [], 0
    for shp in shapes:
        n = 1
        for v in shp:
            n *= v
        out.append(flat[row:row + n // FLAT_COLS].reshape(shp))
        row += n // FLAT_COLS
    return out


def _pack_small(arrays):
    rows = []
    for a in arrays:
        a = a.reshape(-1, a.shape[-1]) if a.shape[-1] == FLAT_COLS else a.reshape(1, -1)
        rows.append(_pad_last(a, FLAT_COLS))
    flat = jnp.concatenate(rows, axis=0)
    return jnp.pad(flat, [(0, -flat.shape[0] % 8), (0, 0)])


def _unpack_small(flat, shapes):
    out, row = [], 0
    for shp in shapes:
        if shp[-1] == FLAT_COLS:
            n = 1
            for v in shp[:-1]:
                n *= v
            out.append(flat[row:row + n].reshape(shp))
            row += n
        else:
            n = 1
            for v in shp:
                n *= v
            out.append(flat[row, :n].reshape(shp))
            row += 1
    return out


def _other_chips(x, y):
    return [(1 - x, y), (x, 1 - y), (1 - x, 1 - y)]


def _all_gather_chips(flat, *, name):
    rows, cols = flat.shape

    def body(x_ref, out_ref, *sems):
        _all_gather_start(x_ref, out_ref, sems)
        _all_gather_finish(x_ref, out_ref, sems)

    return pl.pallas_call(
        body, name=name,
        in_specs=[pl.BlockSpec(memory_space=pltpu.HBM)],
        out_specs=pl.BlockSpec(memory_space=pltpu.HBM),
        out_shape=jax.ShapeDtypeStruct((N_CHIPS, rows, cols), flat.dtype),
        scratch_shapes=_all_gather_sems(),
        compiler_params=pltpu.CompilerParams(has_side_effects=True),
    )(flat)


def _all_gather_sems():
    return [pltpu.SemaphoreType.DMA((3,)), pltpu.SemaphoreType.DMA((3,)), pltpu.SemaphoreType.DMA((3,)),
            pltpu.SemaphoreType.DMA((3,)), pltpu.SemaphoreType.DMA, pltpu.SemaphoreType.DMA]


def _all_gather_copies(x_ref, out_ref, sems, finishing):
    send_sems, recv_sems, pass_send_sems, pass_recv_sems, own_send_sem, own_recv_sem = sems
    x, y, c = lax.axis_index("x"), lax.axis_index("y"), lax.axis_index("c")
    me = 2 * x + y
    my_rows, sib_rows = _half_rows(x_ref.shape[0])

    def copy(src, dst, send_sem, recv_sem, to):
        return pltpu.make_async_remote_copy(src_ref=src, dst_ref=dst, send_sem=send_sem, recv_sem=recv_sem,
                                            device_id=to, device_id_type=MESH)

    own = copy(x_ref, out_ref.at[me], own_send_sem, own_recv_sem, _sibling())
    to_chips, landed, pass_on, passed = [], [], [], []
    for k, (px, py) in enumerate(_other_chips(x, y)):
        to_chips.append(copy(x_ref.at[my_rows, :], out_ref.at[me, my_rows, :], send_sems.at[k], recv_sems.at[k],
                             (px, py, c)))
        if finishing:
            mine, theirs = out_ref.at[2 * px + py, my_rows, :], out_ref.at[2 * px + py, sib_rows, :]
            landed.append(copy(mine, mine, send_sems.at[k], recv_sems.at[k], (px, py, c)))
            pass_on.append(copy(mine, mine, pass_send_sems.at[k], pass_recv_sems.at[k], _sibling()))
            passed.append(copy(theirs, theirs, pass_send_sems.at[k], pass_recv_sems.at[k], _sibling()))
    return own, to_chips, landed, pass_on, passed


def _all_gather_start(x_ref, out_ref, sems):
    own, to_chips, _, _, _ = _all_gather_copies(x_ref, out_ref, sems, finishing=False)
    own.start()
    for cp in to_chips:
        cp.start()


def _all_gather_finish(x_ref, out_ref, sems):
    own, to_chips, landed, pass_on, passed = _all_gather_copies(x_ref, out_ref, sems, finishing=True)
    for k in range(len(landed)):
        landed[k].wait_recv()
        pass_on[k].start()
    for cp in passed:
        cp.wait_recv()
    own.wait_recv()
    for cp in [own] + to_chips + pass_on:
        cp.wait_send()


def _exchange_chips(parts, *, name):
    def body(g_ref, out_ref, *sems):
        _exchange_start(g_ref, out_ref, sems)
        _exchange_finish(g_ref, out_ref, sems)

    return pl.pallas_call(
        body, name=name,
        in_specs=[pl.BlockSpec(memory_space=pltpu.HBM)],
        out_specs=pl.BlockSpec(memory_space=pltpu.HBM),
        out_shape=jax.ShapeDtypeStruct(parts.shape, parts.dtype),
        scratch_shapes=_exchange_sems(),
        compiler_params=pltpu.CompilerParams(has_side_effects=True),
    )(parts)


def _exchange_sems():
    return [pltpu.SemaphoreType.DMA((3,)), pltpu.SemaphoreType.DMA((3,))]


def _exchange_copies(g_ref, out_ref, sems, receiving):
    send_sems, recv_sems = sems
    x, y, c = lax.axis_index("x"), lax.axis_index("y"), lax.axis_index("c")
    me = 2 * x + y
    copies = []
    for k, (px, py) in enumerate(_other_chips(x, y)):
        src, dst = (g_ref.at[me], out_ref.at[2 * px + py]) if receiving else (g_ref.at[2 * px + py], out_ref.at[me])
        copies.append(pltpu.make_async_remote_copy(src_ref=src, dst_ref=dst, send_sem=send_sems.at[k],
                                                   recv_sem=recv_sems.at[k], device_id=(px, py, c),
                                                   device_id_type=MESH))
    return copies


def _exchange_start(g_ref, out_ref, sems):
    for cp in _exchange_copies(g_ref, out_ref, sems, receiving=False):
        cp.start()


def _exchange_finish(g_ref, out_ref, sems):
    for cp in _exchange_copies(g_ref, out_ref, sems, receiving=True):
        cp.wait_recv()
    for cp in _exchange_copies(g_ref, out_ref, sems, receiving=False):
        cp.wait_send()


def _my_chip():
    return 2 * lax.axis_index("x") + lax.axis_index("y")


def _half_rows(rows):
    c = lax.axis_index("c")
    half = rows // 2
    return pl.ds(pl.multiple_of(c * half, 8), half), pl.ds(pl.multiple_of((1 - c) * half, 8), half)


def _sibling():
    return (lax.axis_index("x"), lax.axis_index("y"), 1 - lax.axis_index("c"))


def _pair_exchange(parts, *, name):
    n, rows, cols = parts.shape

    def body(p_ref, theirs_ref, send_sem, recv_sem):
        _, sib_rows = _half_rows(rows)
        cp = pltpu.make_async_remote_copy(src_ref=p_ref.at[:, sib_rows, :], dst_ref=theirs_ref, send_sem=send_sem,
                                          recv_sem=recv_sem, device_id=_sibling(), device_id_type=MESH)
        cp.start()
        cp.wait()

    half = rows // 2
    theirs = pl.pallas_call(
        body, name=name,
        in_specs=[pl.BlockSpec(memory_space=pltpu.HBM)],
        out_specs=pl.BlockSpec(memory_space=pltpu.HBM),
        out_shape=jax.ShapeDtypeStruct((n, half, cols), parts.dtype),
        scratch_shapes=[pltpu.SemaphoreType.DMA, pltpu.SemaphoreType.DMA],
        compiler_params=pltpu.CompilerParams(has_side_effects=True),
    )(parts)
    mine = lax.dynamic_slice_in_dim(parts, lax.axis_index("c") * half, half, axis=1)
    return mine, theirs


def _pair_sum(mine, theirs, *, name):
    n, rows, cols = mine.shape

    def body(a_ref, b_ref, o_ref):
        o_ref[...] = (a_ref[...].astype(F32) + b_ref[...].astype(F32)).astype(o_ref.dtype)

    blk = pl.BlockSpec((n, FLAT_ROW_BLOCK, cols), lambda i: (0, i, 0))
    return pl.pallas_call(
        body, name=name, grid=(rows // FLAT_ROW_BLOCK,),
        in_specs=[blk, blk], out_specs=blk, out_shape=jax.ShapeDtypeStruct(mine.shape, mine.dtype),
        compiler_params=pltpu.CompilerParams(dimension_semantics=("parallel",), vmem_limit_bytes=VMEM_LIMIT),
    )(mine, theirs)


def _sum_chips(received, own, *, name):
    _, rows, cols = received.shape

    def body(p_ref, own_ref, o_ref):
        me = 2 * lax.axis_index("x") + lax.axis_index("y")
        slot = [jnp.where(me == j, own_ref[j], p_ref[j]).astype(F32) for j in range(N_CHIPS)]
        o_ref[...] = ((slot[0] + slot[1]) + slot[2]) + slot[3]

    blk = pl.BlockSpec((N_CHIPS, FLAT_ROW_BLOCK, cols), lambda i: (0, i, 0))
    return pl.pallas_call(
        body, name=name, grid=(rows // FLAT_ROW_BLOCK,),
        in_specs=[blk, blk],
        out_specs=pl.BlockSpec((FLAT_ROW_BLOCK, cols), lambda i: (i, 0)),
        out_shape=jax.ShapeDtypeStruct((rows, cols), F32),
        compiler_params=pltpu.CompilerParams(dimension_semantics=("parallel",), vmem_limit_bytes=VMEM_LIMIT),
    )(received, own)


def _join_cores(half, *, name):
    rows2, cols = half.shape

    def body(h_ref, out_ref, send_sem, recv_sem):
        my_rows, sib_rows = _half_rows(2 * rows2)
        cp = pltpu.make_async_remote_copy(src_ref=h_ref, dst_ref=out_ref.at[my_rows, :], send_sem=send_sem,
                                          recv_sem=recv_sem, device_id=_sibling(), device_id_type=MESH)
        cp.start()
        cp.wait_send()
        pltpu.make_async_remote_copy(src_ref=h_ref, dst_ref=out_ref.at[sib_rows, :], send_sem=send_sem,
                                     recv_sem=recv_sem, device_id=_sibling(), device_id_type=MESH).wait_recv()

    out = pl.pallas_call(
        body, name=name,
        in_specs=[pl.BlockSpec(memory_space=pltpu.HBM)],
        out_specs=pl.BlockSpec(memory_space=pltpu.HBM),
        out_shape=jax.ShapeDtypeStruct((2 * rows2, cols), half.dtype),
        scratch_shapes=[pltpu.SemaphoreType.DMA, pltpu.SemaphoreType.DMA],
        compiler_params=pltpu.CompilerParams(has_side_effects=True),
    )(half)
    return lax.dynamic_update_slice_in_dim(out, half, lax.axis_index("c") * rows2, axis=0)


def _all_reduce_small(v, *, name):
    rows, cols = v.shape
    flips = [(fx, fy, fc) for fx in (0, 1) for fy in (0, 1) for fc in (0, 1)][1:]

    def body(v_ref, out_ref, gath_ref, send_sems, recv_sems):
        x, y, c = lax.axis_index("x"), lax.axis_index("y"), lax.axis_index("c")
        me = 4 * x + 2 * y + c
        gath_ref[me] = v_ref[...]
        peers = [((1 - x) if fx else x, (1 - y) if fy else y, (1 - c) if fc else c) for fx, fy, fc in flips]
        sends = []
        for k, peer in enumerate(peers):
            cp = pltpu.make_async_remote_copy(src_ref=v_ref, dst_ref=gath_ref.at[me], send_sem=send_sems.at[k],
                                              recv_sem=recv_sems.at[k], device_id=peer, device_id_type=MESH)
            cp.start()
            sends.append(cp)
        for k, (px, py, pc) in enumerate(peers):
            pltpu.make_async_remote_copy(src_ref=v_ref, dst_ref=gath_ref.at[4 * px + 2 * py + pc],
                                         send_sem=send_sems.at[k], recv_sem=recv_sems.at[k],
                                         device_id=(px, py, pc), device_id_type=MESH).wait_recv()
        for cp in sends:
            cp.wait_send()
        total = gath_ref[0]
        for k in range(1, 8):
            total = total + gath_ref[k]
        out_ref[...] = total

    total, _ = pl.pallas_call(
        body, name=name,
        in_specs=[pl.BlockSpec(memory_space=pltpu.VMEM)],
        out_specs=[pl.BlockSpec(memory_space=pltpu.VMEM), pl.BlockSpec(memory_space=pltpu.VMEM)],
        out_shape=[jax.ShapeDtypeStruct((rows, cols), v.dtype), jax.ShapeDtypeStruct((8, rows, cols), v.dtype)],
        scratch_shapes=[pltpu.SemaphoreType.DMA((7,)), pltpu.SemaphoreType.DMA((7,))],
        compiler_params=pltpu.CompilerParams(has_side_effects=True),
    )(v)
    return total


def _adamw(w, g, m, v, *, name):
    shape = w.shape
    cols = shape[-1]
    w2, g2, m2, v2 = (a.reshape(-1, cols) for a in (w, g, m, v))
    rows = w2.shape[0]
    br = _pick_rows(rows, FLAT_ROW_BLOCK)

    def body(w_ref, g_ref, m_ref, v_ref, d_out, m_out, v_out):
        gv = g_ref[...]
        m_new = ADAM_B1 * m_ref[...] + (1.0 - ADAM_B1) * gv
        v_new = ADAM_B2 * v_ref[...] + (1.0 - ADAM_B2) * jnp.square(gv)
        m_hat = m_new / (1.0 - ADAM_B1 ** ADAM_STEP)
        v_hat = v_new / (1.0 - ADAM_B2 ** ADAM_STEP)
        d_out[...] = -ADAM_LR * (m_hat / (jnp.sqrt(v_hat) + ADAM_EPS) + ADAM_WD * w_ref[...])
        m_out[...] = m_new
        v_out[...] = v_new

    blk = pl.BlockSpec((br, cols), lambda i: (i, 0))
    out = jax.ShapeDtypeStruct((rows, cols), F32)
    outs = pl.pallas_call(
        body, name=name, grid=(rows // br,),
        in_specs=[blk] * 4, out_specs=[blk] * 3, out_shape=[out] * 3,
        compiler_params=pltpu.CompilerParams(dimension_semantics=("parallel",), vmem_limit_bytes=VMEM_LIMIT),
    )(w2, g2, m2, v2)
    return [o.reshape(shape) for o in outs]


def _pick_rows(rows, target):
    if rows <= target:
        return rows
    return max(b for b in range(8, target + 1, 8) if rows % b == 0)


def _assemble(gathered_shards, name, layer=False):
    return jnp.concatenate(gathered_shards, axis=SHARD_AXIS[name] - int(layer))


def _chip_shard(full, name, j):
    if isinstance(full, list):
        axis = SHARD_AXIS[name] - 1
        layers = full
    else:
        axis = SHARD_AXIS[name]
        layers = [full]
    n = layers[0].shape[axis] // N_CHIPS
    return [lax.slice_in_dim(g, j * n, (j + 1) * n, axis=axis) for g in layers]


def kernel(x, positions, attn_norm, mlp_norm, sb_w_qkv, sb_w_o, kv_norm, mla_w_dkv, mla_kv_lat_norm, mla_w_ukv, mla_w_dq, mla_q_lat_norm, mla_w_uq, mla_w_o, mlp_w1, mlp_w2, final_norm, loss_target, m_attn_norm, m_mlp_norm, m_sb_w_qkv, m_sb_w_o, m_kv_norm, m_mla_w_dkv, m_mla_kv_lat_norm, m_mla_w_ukv, m_mla_w_dq, m_mla_q_lat_norm, m_mla_w_uq, m_mla_w_o, m_mlp_w1, m_mlp_w2, m_final_norm, v_attn_norm, v_mlp_norm, v_sb_w_qkv, v_sb_w_o, v_kv_norm, v_mla_w_dkv, v_mla_kv_lat_norm, v_mla_w_ukv, v_mla_w_dq, v_mla_q_lat_norm, v_mla_w_uq, v_mla_w_o, v_mlp_w1, v_mlp_w2, v_final_norm):
    weights = dict(attn_norm=attn_norm, mlp_norm=mlp_norm, sb_w_qkv=sb_w_qkv, sb_w_o=sb_w_o, kv_norm=kv_norm,
                   mla_w_dkv=mla_w_dkv, mla_kv_lat_norm=mla_kv_lat_norm, mla_w_ukv=mla_w_ukv, mla_w_dq=mla_w_dq,
                   mla_q_lat_norm=mla_q_lat_norm, mla_w_uq=mla_w_uq, mla_w_o=mla_w_o, mlp_w1=mlp_w1, mlp_w2=mlp_w2,
                   final_norm=final_norm)
    m_in = dict(attn_norm=m_attn_norm, mlp_norm=m_mlp_norm, sb_w_qkv=m_sb_w_qkv, sb_w_o=m_sb_w_o, kv_norm=m_kv_norm,
                mla_w_dkv=m_mla_w_dkv, mla_kv_lat_norm=m_mla_kv_lat_norm, mla_w_ukv=m_mla_w_ukv, mla_w_dq=m_mla_w_dq,
                mla_q_lat_norm=m_mla_q_lat_norm, mla_w_uq=m_mla_w_uq, mla_w_o=m_mla_w_o, mlp_w1=m_mlp_w1,
                mlp_w2=m_mlp_w2, final_norm=m_final_norm)
    v_in = dict(attn_norm=v_attn_norm, mlp_norm=v_mlp_norm, sb_w_qkv=v_sb_w_qkv, sb_w_o=v_sb_w_o, kv_norm=v_kv_norm,
                mla_w_dkv=v_mla_w_dkv, mla_kv_lat_norm=v_mla_kv_lat_norm, mla_w_ukv=v_mla_w_ukv, mla_w_dq=v_mla_w_dq,
                mla_q_lat_norm=v_mla_q_lat_norm, mla_w_uq=v_mla_w_uq, mla_w_o=v_mla_w_o, mlp_w1=v_mlp_w1,
                mlp_w2=v_mlp_w2, final_norm=v_final_norm)
    shard_shapes = [weights[n].shape for n in BIG_WEIGHTS]
    small_shapes = [weights[n].shape for n in SMALL_WEIGHTS]

    first_name = BIG_WEIGHTS[0]
    qkv_first, qkv_later = weights[first_name][0], weights[first_name][1:]
    gathered_first = _all_gather_chips(_pack([qkv_first], BF16), name="first_weight_all_gather")
    qkv_w0 = _assemble([_unpack(gathered_first[j], [qkv_first.shape])[0] for j in range(N_CHIPS)], first_name, layer=True)
    ride = _pack([qkv_later] + [weights[n] for n in BIG_WEIGHTS[1:]], BF16)
    ride_shapes = [qkv_later.shape] + shard_shapes[1:]

    def rest_weights(gathered):
        per_chip = [_unpack(gathered[j], ride_shapes) for j in range(N_CHIPS)]
        full = {n: _assemble([per_chip[j][i] for j in range(N_CHIPS)], n) for i, n in enumerate(BIG_WEIGHTS)}
        full[first_name] = jnp.concatenate([qkv_w0[None], full[first_name]], axis=0)
        return full

    norms = {n: weights[n] for n in SMALL_WEIGHTS}

    def chip_parts(g, tag):
        parts = _pack_chips([[piece for n in BIG_WEIGHTS if n in g for piece in _chip_shard(g[n], n, j)]
                             for j in range(N_CHIPS)], BF16)
        mine, theirs = _pair_exchange(parts, name=f"grads_pair_exchange_{tag}")
        return _pair_sum(mine, theirs, name=f"grads_pair_sum_{tag}")

    def finish(received, chip_part, tag):
        g_half = _sum_chips(received, chip_part, name=f"grads_sum_chips_{tag}")
        return _join_cores(g_half, name=f"grads_join_cores_{tag}")

    loss, dx, grads, early_parts, early_got = _local_step(
        x[0], positions[0], loss_target, qkv_w0, norms, rest_weights, ride=ride,
        early_reduce=functools.partial(chip_parts, tag="early"))
    loss = lax.psum(loss, ("x", "y", "c"))
    early_sum = finish(early_got, early_parts, "early")
    last_parts = chip_parts({first_name: grads[first_name][:1]}, "last")
    last_sum = finish(_exchange_chips(last_parts, name="grads_exchange_last"), last_parts, "last")

    out_g = dict(zip(BIG_WEIGHTS, _unpack(early_sum, ride_shapes)))
    out_g[first_name] = jnp.concatenate([_unpack(last_sum, [qkv_first.shape])[0][None], out_g[first_name]], axis=0)
    out_d, out_m, out_v = {}, {}, {}
    for n in BIG_WEIGHTS:
        out_d[n], out_m[n], out_v[n] = _adamw(weights[n], out_g[n], m_in[n], v_in[n], name=f"adamw_{n}")

    small_sum = _all_reduce_small(_pack_small([grads[n] for n in SMALL_WEIGHTS]), name="gains_all_reduce")
    sd, sm, sv = _adamw(_pack_small([weights[n] for n in SMALL_WEIGHTS]), small_sum,
                        _pack_small([m_in[n] for n in SMALL_WEIGHTS]),
                        _pack_small([v_in[n] for n in SMALL_WEIGHTS]), name="adamw_gains")
    out_g.update(zip(SMALL_WEIGHTS, _unpack_small(small_sum, small_shapes)))
    out_d.update(zip(SMALL_WEIGHTS, _unpack_small(sd, small_shapes)))
    out_m.update(zip(SMALL_WEIGHTS, _unpack_small(sm, small_shapes)))
    out_v.update(zip(SMALL_WEIGHTS, _unpack_small(sv, small_shapes)))

    return (loss, dx, *[out_g[n] for n in ALL_WEIGHTS], *[out_d[n] for n in ALL_WEIGHTS],
            *[out_m[n] for n in ALL_WEIGHTS], *[out_v[n] for n in ALL_WEIGHTS])
```

```python
import functools

import jax
import jax.numpy as jnp
from jax import lax
from jax.experimental import pallas as pl
from jax.experimental.pallas import tpu as pltpu

F32 = jnp.float32
BF16 = jnp.bfloat16

LANES = 128
SB_HEAD_DIM = 64
MLA_NOPE = 64
MLA_ROPE = 32
MLA_V = 64
MLA_Q_RANK = 384
MLA_KV_RANK = 256
CHUNK = 64
ROPE_THETA = 10000.0
NORM_EPS = 1e-6
SB_SCALE = SB_HEAD_DIM ** -0.5
MLA_SCALE = (MLA_NOPE + MLA_ROPE) ** -0.5
ROPE_LO = MLA_NOPE
ROPE_HALF = MLA_ROPE // 2
ATT_Q_BLOCK = 1024
ATT_K_BLOCK = 256
MLA_FWD_K_BLOCK = 512
NEG_BIG = -1e30
SB_DEAD_LOG = -110.0
VMEM_LIMIT = 56 * 1024 * 1024

ADAM_LR = 0.001
ADAM_B1 = 0.9
ADAM_B2 = 0.999
ADAM_EPS = 1e-08
ADAM_WD = 0.01
ADAM_STEP = 10

FLAT_COLS = 1024
FLAT_ROW_BLOCK = 256
N_CHIPS = 4
MESH = pl.DeviceIdType.MESH

BIG_WEIGHTS = ["sb_w_qkv", "sb_w_o", "mla_w_dkv", "mla_w_ukv", "mla_w_dq", "mla_w_uq", "mla_w_o", "mlp_w1", "mlp_w2"]
SHARD_AXIS = {"sb_w_qkv": 2, "sb_w_o": 1, "mla_w_dkv": 0, "mla_w_ukv": 1, "mla_w_dq": 1, "mla_w_uq": 2,
              "mla_w_o": 1, "mlp_w1": 2, "mlp_w2": 1}
SMALL_WEIGHTS = ["attn_norm", "mlp_norm", "kv_norm", "mla_kv_lat_norm", "mla_q_lat_norm", "final_norm"]
ALL_WEIGHTS = ["attn_norm", "mlp_norm", "sb_w_qkv", "sb_w_o", "kv_norm", "mla_w_dkv", "mla_kv_lat_norm", "mla_w_ukv",
               "mla_w_dq", "mla_q_lat_norm", "mla_w_uq", "mla_w_o", "mlp_w1", "mlp_w2", "final_norm"]


def _dot(a, b, dims):
    return lax.dot_general(a, b, (dims, ((), ())), preferred_element_type=F32)


def _dot_nn(a, b):
    return _dot(a, b, ((1,), (0,)))


def _dot_nt(a, b):
    return _dot(a, b, ((1,), (1,)))


def _dot_tn(a, b):
    return _dot(a, b, ((0,), (0,)))


def _pick_block(n, target):
    if n <= target:
        return n
    best = max(b for b in range(LANES, target + 1, LANES) if n % b == 0)
    return best


MM_ROWS = 512
MM_COLS = 1024
MM_DEPTH = 4096
MM_DEPTH_TN = 2048


def _mm(a, b, *, name, dims="nn", epilogue=None, extras=(), out_dtypes=(BF16,), column_sum=False):
    if dims == "nn":
        (m, k), (k2, n) = a.shape, b.shape
    elif dims == "nt":
        (m, k), (n, k2) = a.shape, b.shape
    else:
        (k, m), (k2, n) = a.shape, b.shape
    assert k == k2, (name, a.shape, b.shape)
    if dims == "tn":
        bm, bn, bk = _pick_block(m, MM_COLS), _pick_block(n, MM_COLS), _pick_block(k, MM_DEPTH_TN)
    else:
        rows = MM_ROWS if k > MM_DEPTH // 2 else 2 * MM_ROWS
        cols = 2 * MM_COLS if (k <= MM_COLS and n >= 4 * MM_COLS) else MM_COLS
        bm, bn, bk = _pick_block(m, rows), _pick_block(n, cols), _pick_block(k, MM_DEPTH)
    nk = k // bk
    if dims == "tn":
        a_spec = pl.BlockSpec((bk, bm), lambda j, i, kk: (kk, i))
    else:
        a_spec = pl.BlockSpec((bm, bk), lambda j, i, kk: (i, kk))
    if dims == "nt":
        b_spec = pl.BlockSpec((bn, bk), lambda j, i, kk: (j, kk))
    else:
        b_spec = pl.BlockSpec((bk, bn), lambda j, i, kk: (kk, j))
    extra_specs = []
    for arr, kind in extras:
        if kind == "tile":
            assert arr.shape == (m, n), (name, arr.shape)
            extra_specs.append(pl.BlockSpec((bm, bn), lambda j, i, kk: (i, j)))
        elif kind == "vec":
            assert arr.shape == (1, n), (name, arr.shape)
            extra_specs.append(pl.BlockSpec((1, bn), lambda j, i, kk: (0, j)))
        else:
            assert arr.shape == (m, LANES), (name, arr.shape)
            extra_specs.append(pl.BlockSpec((bm, LANES), lambda j, i, kk: (i, 0)))
    n_extra = len(extras)
    n_out = len(out_dtypes)
    n_sum = int(column_sum)
    dot = {"nn": _dot_nn, "nt": _dot_nt, "tn": _dot_tn}[dims]

    def body(*refs):
        a_ref, b_ref = refs[0], refs[1]
        extra_refs = refs[2:2 + n_extra]
        out_refs = refs[2 + n_extra:2 + n_extra + n_out]

        def finish(acc):
            outs = (acc,) if epilogue is None else epilogue(acc, *[r[...] for r in extra_refs])
            for o_ref, o in zip(out_refs, outs):
                o_ref[...] = o.astype(o_ref.dtype)
            if column_sum:
                sum_ref = refs[2 + n_extra + n_out]
                first_rows = pl.program_id(1) == 0

                @pl.when(first_rows)
                def _():
                    sum_ref[...] = outs[n_out]

                @pl.when(jnp.logical_not(first_rows))
                def _():
                    sum_ref[...] += outs[n_out]

        part = dot(a_ref[...].astype(BF16), b_ref[...].astype(BF16))
        if nk == 1:
            finish(part)
            return
        acc_ref = refs[-1]
        kk = pl.program_id(2)

        @pl.when(kk == 0)
        def _():
            acc_ref[...] = part

        @pl.when(kk > 0)
        def _():
            acc_ref[...] += part

        @pl.when(kk == nk - 1)
        def _():
            finish(acc_ref[...])

    outs = pl.pallas_call(
        body, name=name, grid=(n // bn, m // bm, nk),
        in_specs=[a_spec, b_spec] + extra_specs,
        out_specs=[pl.BlockSpec((bm, bn), lambda j, i, kk: (i, j)) for _ in range(n_out)]
        + [pl.BlockSpec((1, bn), lambda j, i, kk: (0, j))] * n_sum,
        out_shape=[jax.ShapeDtypeStruct((m, n), dt) for dt in out_dtypes] + [jax.ShapeDtypeStruct((1, n), F32)] * n_sum,
        scratch_shapes=[pltpu.VMEM((bm, bn), F32)] if nk > 1 else [],
        compiler_params=pltpu.CompilerParams(
            dimension_semantics=("parallel", "arbitrary" if column_sum else "parallel", "arbitrary"),
            vmem_limit_bytes=VMEM_LIMIT),
    )(a, b, *[arr for arr, _ in extras])
    return outs[0] if n_out + n_sum == 1 else outs


def _epi_add(acc, res):
    return (res + acc,)


def _epi_relu2(acc):
    r = jnp.maximum(acc, 0.0)
    return acc, r * r


def _epi_relu2_grad(acc, u):
    return (acc * (2.0 * jnp.maximum(u.astype(F32), 0.0)),)


def _rope_slab(t, cos_t, sin_t):
    lane = lax.broadcasted_iota(jnp.int32, t.shape, 1)
    partner = jnp.where(lane < ROPE_LO + ROPE_HALF, pltpu.roll(t, LANES - ROPE_HALF, 1), pltpu.roll(t, ROPE_HALF, 1))
    return t * cos_t + partner * sin_t


def _rope_slab_bwd(d, cos_t, sin_t):
    ds = d * sin_t
    lane = lax.broadcasted_iota(jnp.int32, d.shape, 1)
    partner = jnp.where(lane < ROPE_LO + ROPE_HALF, pltpu.roll(ds, LANES - ROPE_HALF, 1), pltpu.roll(ds, ROPE_HALF, 1))
    in_rope = (lane >= ROPE_LO) & (lane < ROPE_LO + MLA_ROPE)
    return d * cos_t + jnp.where(in_rope, partner, 0.0)


def _epi_rope_heads(acc, cos_t, sin_t):
    slabs = [_rope_slab(acc[:, j * LANES:(j + 1) * LANES], cos_t, sin_t) for j in range(acc.shape[1] // LANES)]
    return (jnp.concatenate(slabs, axis=1) * MLA_SCALE,)


def _row_block(s):
    return min(512, s)


def _rms_fwd(x, g, *, name):
    s, d = x.shape
    bm = _row_block(s)

    def body(x_ref, g_ref, o_ref):
        xv = x_ref[...]
        r = lax.rsqrt(jnp.mean(xv * xv, axis=-1, keepdims=True) + NORM_EPS)
        o_ref[...] = ((xv * r) * g_ref[...]).astype(o_ref.dtype)

    return pl.pallas_call(
        body, name=name, grid=(s // bm,),
        in_specs=[pl.BlockSpec((bm, d), lambda i: (i, 0)), pl.BlockSpec((1, d), lambda i: (0, 0))],
        out_specs=pl.BlockSpec((bm, d), lambda i: (i, 0)),
        out_shape=jax.ShapeDtypeStruct((s, d), BF16),
        compiler_params=pltpu.CompilerParams(dimension_semantics=("parallel",), vmem_limit_bytes=VMEM_LIMIT),
    )(x, g.reshape(1, d))


def _rms_bwd_math(xv, gv, dy):
    r = lax.rsqrt(jnp.mean(xv * xv, axis=-1, keepdims=True) + NORM_EPS)
    xhat = xv * r
    dyg = dy * gv
    mdot = jnp.mean(dyg * xhat, axis=-1, keepdims=True)
    dx = r * (dyg - xhat * mdot)
    dg = jnp.sum(dy * xhat, axis=0, keepdims=True)
    return dx, dg


def _rms_bwd(x, g, dy, dres, *, name, lead_axis=False):
    s, d = x.shape
    bm = _row_block(s)
    has_res = dres is not None

    def body(*refs):
        x_ref, g_ref, dy_ref = refs[:3]
        dres_ref = refs[3] if has_res else None
        dx_ref, dxb_ref, dg_ref = refs[-3:]
        dx, dg = _rms_bwd_math(x_ref[...], g_ref[...], dy_ref[...].astype(F32))
        if has_res:
            dx = dx + dres_ref[...]
        dx_ref[...] = dx
        dxb_ref[...] = dx.astype(BF16)

        @pl.when(pl.program_id(0) == 0)
        def _():
            dg_ref[...] = jnp.zeros_like(dg_ref)

        dg_ref[...] += dg

    row = pl.BlockSpec((bm, d), lambda i: (i, 0))
    vec = pl.BlockSpec((1, d), lambda i: (0, 0))
    ins = [x, g.reshape(1, d), dy] + ([dres] if has_res else [])
    dx_spec, dx_shape = row, (s, d)
    if lead_axis:
        dx_spec, dx_shape = pl.BlockSpec((None, bm, d), lambda i: (0, i, 0)), (1, s, d)
    return pl.pallas_call(
        body, name=name, grid=(s // bm,),
        in_specs=[row, vec, row] + ([row] if has_res else []),
        out_specs=[dx_spec, row, vec],
        out_shape=[jax.ShapeDtypeStruct(dx_shape, F32), jax.ShapeDtypeStruct((s, d), BF16),
                   jax.ShapeDtypeStruct((1, d), F32)],
        compiler_params=pltpu.CompilerParams(dimension_semantics=("arbitrary",), vmem_limit_bytes=VMEM_LIMIT),
    )(*ins)


def _epi_rms_bwd(acc, x, g, dres=None):
    dx, dg = _rms_bwd_math(x, g, acc)
    if dres is not None:
        dx = dx + dres
    return dx, dx, dg


def _mm_rms_bwd(dy_src, w, x, g, dres, *, name):
    d = x.shape[1]
    assert w.shape[0] == d and d <= MM_COLS, (name, w.shape, x.shape)
    extras = [(x, "tile"), (g.reshape(1, d), "vec")] + ([(dres, "tile")] if dres is not None else [])
    return _mm(dy_src, w, name=name, dims="nt", epilogue=_epi_rms_bwd, extras=extras, out_dtypes=(F32, BF16),
               column_sum=True)


def _loss_bwd(x, g, target, *, name):
    s, d = x.shape
    bm = _row_block(s)

    def body(x_ref, g_ref, t_ref, loss_ref, dx_ref, dxb_ref, dg_ref):
        xv, gv = x_ref[...], g_ref[...]
        r = lax.rsqrt(jnp.mean(xv * xv, axis=-1, keepdims=True) + NORM_EPS)
        err = (xv * r) * gv - t_ref[...]
        dx, dg = _rms_bwd_math(xv, gv, err * (1.0 / d))
        dx_ref[...] = dx
        dxb_ref[...] = dx.astype(BF16)

        @pl.when(pl.program_id(0) == 0)
        def _():
            dg_ref[...] = jnp.zeros_like(dg_ref)
            loss_ref[...] = jnp.zeros_like(loss_ref)

        dg_ref[...] += dg
        loss_ref[...] += jnp.sum(jnp.mean(err * err, axis=-1, keepdims=True), axis=0, keepdims=True) * 0.5

    row = pl.BlockSpec((bm, d), lambda i: (i, 0))
    vec = pl.BlockSpec((1, d), lambda i: (0, 0))
    assert target.shape == (1, s, d), target.shape
    return pl.pallas_call(
        body, name=name, grid=(s // bm,),
        in_specs=[row, vec, pl.BlockSpec((None, bm, d), lambda i: (0, i, 0))],
        out_specs=[pl.BlockSpec((8, LANES), lambda i: (0, 0)), row, row, vec],
        out_shape=[jax.ShapeDtypeStruct((8, LANES), F32), jax.ShapeDtypeStruct((s, d), F32),
                   jax.ShapeDtypeStruct((s, d), BF16), jax.ShapeDtypeStruct((1, d), F32)],
        compiler_params=pltpu.CompilerParams(dimension_semantics=("arbitrary",), vmem_limit_bytes=VMEM_LIMIT),
    )(x, g.reshape(1, d), target)


def _kv_prep(down, g, cos_t, sin_t, *, name):
    s, w = down.shape
    bm = _row_block(s)

    def body(d_ref, g_ref, c_ref, s_ref, o_ref):
        lat = d_ref[:, :MLA_KV_RANK]
        r = lax.rsqrt(jnp.mean(lat * lat, axis=-1, keepdims=True) + NORM_EPS)
        o_ref[:, :MLA_KV_RANK] = ((lat * r) * g_ref[...]).astype(BF16)
        o_ref[:, MLA_KV_RANK:] = _rope_slab(d_ref[:, MLA_KV_RANK:], c_ref[...], s_ref[...]).astype(BF16)

    row = pl.BlockSpec((bm, w), lambda i: (i, 0))
    tab = pl.BlockSpec((bm, LANES), lambda i: (i, 0))
    return pl.pallas_call(
        body, name=name, grid=(s // bm,),
        in_specs=[row, pl.BlockSpec((1, MLA_KV_RANK), lambda i: (0, 0)), tab, tab],
        out_specs=row, out_shape=jax.ShapeDtypeStruct((s, w), BF16),
        compiler_params=pltpu.CompilerParams(dimension_semantics=("parallel",), vmem_limit_bytes=VMEM_LIMIT),
    )(down, g.reshape(1, MLA_KV_RANK), cos_t, sin_t)


def _kv_prep_bwd(down, g, cos_t, sin_t, dcat, *, name):
    s, w = down.shape
    bm = _row_block(s)

    def body(d_ref, g_ref, c_ref, s_ref, dc_ref, o_ref, dg_ref):
        dlat, dg = _rms_bwd_math(d_ref[:, :MLA_KV_RANK], g_ref[...], dc_ref[:, :MLA_KV_RANK])
        o_ref[:, :MLA_KV_RANK] = dlat.astype(BF16)
        o_ref[:, MLA_KV_RANK:] = _rope_slab_bwd(dc_ref[:, MLA_KV_RANK:], c_ref[...], s_ref[...]).astype(BF16)

        @pl.when(pl.program_id(0) == 0)
        def _():
            dg_ref[...] = jnp.zeros_like(dg_ref)

        dg_ref[...] += dg

    row = pl.BlockSpec((bm, w), lambda i: (i, 0))
    tab = pl.BlockSpec((bm, LANES), lambda i: (i, 0))
    vec = pl.BlockSpec((1, MLA_KV_RANK), lambda i: (0, 0))
    return pl.pallas_call(
        body, name=name, grid=(s // bm,),
        in_specs=[row, vec, tab, tab, row],
        out_specs=[row, vec],
        out_shape=[jax.ShapeDtypeStruct((s, w), BF16), jax.ShapeDtypeStruct((1, MLA_KV_RANK), F32)],
        compiler_params=pltpu.CompilerParams(dimension_semantics=("arbitrary",), vmem_limit_bytes=VMEM_LIMIT),
    )(down, g.reshape(1, MLA_KV_RANK), cos_t, sin_t, dcat)


def _split_bf16(v):
    hi = v.astype(BF16)
    lo = (v - hi.astype(F32)).astype(BF16)
    return hi, lo


def _suffix_matrices(n):
    row = lax.broadcasted_iota(jnp.int32, (n, n), 0)
    col = lax.broadcasted_iota(jnp.int32, (n, n), 1)
    incl = (row >= col).astype(BF16)
    return (row > col).astype(BF16), jnp.concatenate([incl, incl], axis=0)


def _suffix_sum(v, matrix):
    hi, lo = _split_bf16(v)
    return _dot_nn(jnp.concatenate([hi, lo], axis=1), matrix)


def _block_positions(qi, kb, bq, bk, r0, r1):
    row = qi * bq + r0 + lax.broadcasted_iota(jnp.int32, (r1 - r0, bk), 0)
    col = kb * bk + lax.broadcasted_iota(jnp.int32, (r1 - r0, bk), 1)
    return row, col


def _att_blocks(s, key_block=ATT_K_BLOCK):
    bq, bk = min(ATT_Q_BLOCK, s), min(key_block, s)
    return bq, bk, s // bq, bq // bk


def _sweep(qi, ratio, bk, step, unroll=2, alive=None):
    bq = ratio * bk
    for d in range(ratio):
        kb, r0 = (qi + 1) * ratio - 1 - d, (ratio - 1 - d) * bk
        near = bq if alive is None else min(r0 + 2 * bk, bq)
        step(kb, True, r0, near)
        if near < bq:
            pl.when(alive(near))(functools.partial(step, kb, False, near, bq))
    unroll = unroll if ratio % unroll == 0 else 1
    trips = qi * (ratio // unroll)

    def trip(i):
        for u in range(unroll):
            kb = qi * ratio - 1 - (i * unroll + u)
            if alive is None or ratio == 1:
                step(kb, False, 0, bq)
            else:
                step(kb, False, 0, bk)
                pl.when(alive(bk))(functools.partial(step, kb, False, bk, bq))

    if alive is None:
        lax.fori_loop(0, trips, lambda i, carry: (trip(i), carry)[1], 0)
    else:
        lax.while_loop(lambda i: jnp.logical_and(i < trips, alive(0)), lambda i: (trip(i), i + 1)[1], 0)


def _stick_left(c_ref, r0):
    return jnp.max(c_ref[r0:, :]) > SB_DEAD_LOG


def _sb_logs(q, k):
    z = _dot_nt(q, k)
    lb = jnp.minimum(z, 0.0) - jnp.log(1.0 + jnp.exp(-jnp.abs(z)))
    return lb, lb - z


def _sb_fwd(qkv, heads, *, name, ride=None):
    s = qkv.shape[0]
    bq, bk, nq, ratio = _att_blocks(s)
    riding = ride is not None

    def body(*refs):
        if riding:
            q_ref, k_ref, v_ref, w_ref, o_ref, gath_ref, acc_ref, c_ref = refs[:8]
            first = jnp.logical_and(pl.program_id(0) == 0, pl.program_id(1) == 0)
            last = jnp.logical_and(pl.program_id(0) == heads - 1, pl.program_id(1) == nq - 1)
            pl.when(first)(functools.partial(_all_gather_start, w_ref, gath_ref, refs[8:]))
        else:
            q_ref, k_ref, v_ref, o_ref, acc_ref, c_ref = refs
        qi = pl.program_id(1)
        q = q_ref[...] * SB_SCALE
        m_strict, _ = _suffix_matrices(bk)
        acc_ref[...] = jnp.zeros_like(acc_ref)
        c_ref[...] = jnp.zeros_like(c_ref)

        def step(kb, masked, r0, r1):
            rows = pl.ds(pl.multiple_of(kb * bk, bk), bk)
            mine = pl.ds(r0, r1 - r0)
            k, v = k_ref[rows, :], v_ref[rows, :]
            lb, lk = _sb_logs(q[r0:r1], k)
            if masked:
                row, col = _block_positions(qi, kb, bq, bk, r0, r1)
                causal = col < row
                lk = jnp.where(causal, lk, 0.0)
            c = c_ref[mine, :]
            w = jnp.exp(lb + _dot_nn(lk.astype(BF16), m_strict) + jnp.tile(c, (1, bk // LANES)))
            if masked:
                w = jnp.where(causal, w, 0.0)
            acc_ref[mine, :] += _dot_nn(w.astype(BF16), v)
            c_ref[mine, :] = c + jnp.sum(lk, axis=-1, keepdims=True)

        _sweep(qi, ratio, bk, step, unroll=1, alive=functools.partial(_stick_left, c_ref))
        o_ref[...] = acc_ref[...].astype(o_ref.dtype)
        if riding:
            pl.when(last)(functools.partial(_all_gather_finish, w_ref, gath_ref, refs[8:]))

    hbm = pl.BlockSpec(memory_space=pltpu.HBM)
    o_spec = pl.BlockSpec((bq, LANES), lambda h, i: (i, h))
    o_shape = jax.ShapeDtypeStruct((s, heads * LANES), F32)
    return pl.pallas_call(
        body, name=name, grid=(heads, nq),
        in_specs=[pl.BlockSpec((bq, LANES), lambda h, i: (i, h)),
                  pl.BlockSpec((s, LANES), lambda h, i: (0, heads + h)),
                  pl.BlockSpec((s, LANES), lambda h, i: (0, 2 * heads + h))] + ([hbm] if riding else []),
        out_specs=[o_spec, hbm] if riding else o_spec,
        out_shape=[o_shape, jax.ShapeDtypeStruct((N_CHIPS,) + ride.shape, ride.dtype)] if riding else o_shape,
        scratch_shapes=[pltpu.VMEM((bq, LANES), F32), pltpu.VMEM((bq, LANES), F32)]
        + (_all_gather_sems() if riding else []),
        compiler_params=pltpu.CompilerParams(dimension_semantics=("arbitrary", "arbitrary"),
                                             vmem_limit_bytes=VMEM_LIMIT, has_side_effects=riding),
    )(*([qkv, qkv, qkv] + ([ride] if riding else [])))


def _sb_bwd(qkv, o, do, heads, *, name, ride=None):
    s = qkv.shape[0]
    bq, bk, nq, ratio = _att_blocks(s)
    riding = ride is not None

    def body(*refs):
        if riding:
            (q_ref, k_ref, v_ref, o_ref, do_ref, g_ref, dq_ref, dk_ref, dv_ref, got_ref,
             dq_acc, dk_acc, dv_acc, c_ref, e_ref) = refs[:15]
            first = jnp.logical_and(pl.program_id(0) == 0, pl.program_id(1) == 0)
            last = jnp.logical_and(pl.program_id(0) == heads - 1, pl.program_id(1) == nq - 1)
            pl.when(first)(functools.partial(_exchange_start, g_ref, got_ref, refs[15:]))
        else:
            q_ref, k_ref, v_ref, o_ref, do_ref, dq_ref, dk_ref, dv_ref, dq_acc, dk_acc, dv_acc, c_ref, e_ref = refs
        qi = pl.program_id(1)

        @pl.when(qi == 0)
        def _():
            dk_acc[...] = jnp.zeros_like(dk_acc)
            dv_acc[...] = jnp.zeros_like(dv_acc)

        q = q_ref[...] * SB_SCALE
        do = do_ref[...]
        q_t, do_t = q.T, do.T
        total = jnp.sum(do.astype(F32) * o_ref[...].astype(F32), axis=-1, keepdims=True)
        m_strict, m_incl = _suffix_matrices(bk)
        dq_acc[...] = jnp.zeros_like(dq_acc)
        c_ref[...] = jnp.zeros_like(c_ref)
        e_ref[...] = jnp.broadcast_to(total, e_ref.shape)
        reps = (1, bk // LANES)

        def step(kb, masked, r0, r1):
            rows = pl.ds(pl.multiple_of(kb * bk, bk), bk)
            mine = pl.ds(r0, r1 - r0)
            k, v = k_ref[rows, :], v_ref[rows, :]
            qs, dos = q[r0:r1], do[r0:r1]
            lb, lk_all = _sb_logs(qs, k)
            lk = lk_all
            if masked:
                row, col = _block_positions(qi, kb, bq, bk, r0, r1)
                causal = col < row
                lk = jnp.where(causal, lk_all, 0.0)
            c = c_ref[mine, :]
            w = jnp.exp(lb + _dot_nn(lk.astype(BF16), m_strict) + jnp.tile(c, reps))
            if masked:
                w = jnp.where(causal, w, 0.0)
            wb = w.astype(BF16)
            g = wb.astype(F32) * _dot_nt(dos, v)
            e = e_ref[mine, :]
            g_left = jnp.tile(e, reps) - _suffix_sum(g, m_incl)
            da = g * jnp.exp(lk_all) - jnp.exp(lb) * g_left
            if masked:
                da = jnp.where(causal, da, 0.0)
            dab = da.astype(BF16)
            dq_acc[mine, :] += _dot_nn(dab, k)
            dk_acc[:, rows] += _dot_nn(q_t[:, r0:r1], dab)
            dv_acc[:, rows] += _dot_nn(do_t[:, r0:r1], wb)
            e_ref[mine, :] = e - jnp.sum(g, axis=-1, keepdims=True)
            c_ref[mine, :] = c + jnp.sum(lk, axis=-1, keepdims=True)

        _sweep(qi, ratio, bk, step, unroll=1, alive=functools.partial(_stick_left, c_ref))
        dq_ref[...] = (dq_acc[...] * SB_SCALE).astype(dq_ref.dtype)

        @pl.when(qi == nq - 1)
        def _():
            dk_ref[...] = dk_acc[...].T.astype(dk_ref.dtype)
            dv_ref[...] = dv_acc[...].T.astype(dv_ref.dtype)

        if riding:
            pl.when(last)(functools.partial(_exchange_finish, g_ref, got_ref, refs[15:]))

    blk = pl.BlockSpec((bq, LANES), lambda h, i: (i, h))
    full = pl.BlockSpec((s, LANES), lambda h, i: (0, h))
    hbm = pl.BlockSpec(memory_space=pltpu.HBM)
    shape = jax.ShapeDtypeStruct((s, heads * LANES), BF16)
    return pl.pallas_call(
        body, name=name, grid=(heads, nq),
        in_specs=[blk,
                  pl.BlockSpec((s, LANES), lambda h, i: (0, heads + h)),
                  pl.BlockSpec((s, LANES), lambda h, i: (0, 2 * heads + h)),
                  blk, blk] + ([hbm] if riding else []),
        out_specs=[blk, full, full] + ([hbm] if riding else []),
        out_shape=[shape, shape, shape] + ([jax.ShapeDtypeStruct(ride.shape, ride.dtype)] if riding else []),
        scratch_shapes=[pltpu.VMEM((bq, LANES), F32), pltpu.VMEM((LANES, s), F32), pltpu.VMEM((LANES, s), F32),
                        pltpu.VMEM((bq, LANES), F32), pltpu.VMEM((bq, LANES), F32)]
        + (_exchange_sems() if riding else []),
        compiler_params=pltpu.CompilerParams(dimension_semantics=("arbitrary", "arbitrary"),
                                             vmem_limit_bytes=VMEM_LIMIT, has_side_effects=riding),
    )(*([qkv, qkv, qkv, o, do] + ([ride] if riding else [])))


def _chunk_allowed(qi, kb, bq, bk, r0, r1):
    row, col = _block_positions(qi, kb, bq, bk, r0, r1)
    return (col // CHUNK) <= (row // CHUNK)


def _mla_fwd(q, kv, heads, *, name):
    s = q.shape[0]
    bq, bk, nq, ratio = _att_blocks(s, MLA_FWD_K_BLOCK)
    reps = (1, bk // LANES)

    def body(q_ref, k_ref, v_ref, o_ref, lse_ref, acc_ref, m_ref, l_ref):
        qi = pl.program_id(1)
        qv = q_ref[...]
        acc_ref[...] = jnp.zeros_like(acc_ref)
        m_ref[...] = jnp.full_like(m_ref, NEG_BIG)
        l_ref[...] = jnp.zeros_like(l_ref)

        def step(kb, masked, r0, r1):
            rows = pl.ds(pl.multiple_of(kb * bk, bk), bk)
            mine = pl.ds(r0, r1 - r0)
            k, v = k_ref[rows, :], v_ref[rows, :]
            sc = _dot_nt(qv[r0:r1], k)
            if masked:
                allowed = _chunk_allowed(qi, kb, bq, bk, r0, r1)
                sc = jnp.where(allowed, sc, NEG_BIG)
            m_old = m_ref[mine, :]
            m_new = jnp.maximum(m_old, jnp.max(sc, axis=-1, keepdims=True))
            p = jnp.exp(sc - jnp.tile(m_new, reps))
            alpha = jnp.exp(m_old - m_new)
            l_ref[mine, :] = alpha * l_ref[mine, :] + jnp.sum(p, axis=-1, keepdims=True)
            acc_ref[mine, :] = alpha * acc_ref[mine, :] + _dot_nn(p.astype(BF16), v)
            m_ref[mine, :] = m_new

        _sweep(qi, ratio, bk, step)
        o_ref[...] = (acc_ref[...] / l_ref[...]).astype(o_ref.dtype)
        lse_ref[...] = m_ref[...] + jnp.log(l_ref[...])

    blk = pl.BlockSpec((bq, LANES), lambda h, i: (i, h))
    return pl.pallas_call(
        body, name=name, grid=(heads, nq),
        in_specs=[blk,
                  pl.BlockSpec((s, LANES), lambda h, i: (0, h)),
                  pl.BlockSpec((s, LANES), lambda h, i: (0, heads + h))],
        out_specs=[blk, blk],
        out_shape=[jax.ShapeDtypeStruct((s, heads * LANES), BF16), jax.ShapeDtypeStruct((s, heads * LANES), F32)],
        scratch_shapes=[pltpu.VMEM((bq, LANES), F32), pltpu.VMEM((bq, LANES), F32), pltpu.VMEM((bq, LANES), F32)],
        compiler_params=pltpu.CompilerParams(dimension_semantics=("parallel", "arbitrary"),
                                             vmem_limit_bytes=VMEM_LIMIT),
    )(q, kv, kv)


def _mla_bwd(q, kv, o, do, lse, cos_t, sin_t, dkv_init, heads, *, name):
    s = q.shape[0]
    bq, bk, nq, ratio = _att_blocks(s)
    reps = (1, bk // LANES)
    has_init = dkv_init is not None

    def body(*refs):
        q_ref, k_ref, v_ref, o_ref, do_ref, lse_ref, c_ref, s_ref = refs[:8]
        ki_ref, vi_ref = (refs[8], refs[9]) if has_init else (None, None)
        dq_ref, dk_ref, dv_ref, dq_acc, dk_acc, dv_acc = refs[-6:]
        qi = pl.program_id(1)

        @pl.when(qi == 0)
        def _():
            if has_init:
                dk_acc[...] = ki_ref[...].astype(F32).T
                dv_acc[...] = vi_ref[...].astype(F32).T
            else:
                dk_acc[...] = jnp.zeros_like(dk_acc)
                dv_acc[...] = jnp.zeros_like(dv_acc)

        qv = q_ref[...]
        do = do_ref[...]
        q_t, do_t = qv.T, do.T
        delta = jnp.sum(do.astype(F32) * o_ref[...].astype(F32), axis=-1, keepdims=True)
        lse_wide = jnp.tile(lse_ref[...], reps)
        dq_acc[...] = jnp.zeros_like(dq_acc)

        def step(kb, masked, r0, r1):
            rows = pl.ds(pl.multiple_of(kb * bk, bk), bk)
            k, v = k_ref[rows, :], v_ref[rows, :]
            qs, dos = qv[r0:r1], do[r0:r1]
            p = jnp.exp(_dot_nt(qs, k) - lse_wide[r0:r1])
            if masked:
                p = jnp.where(_chunk_allowed(qi, kb, bq, bk, r0, r1), p, 0.0)
            ds = (p * (_dot_nt(dos, v) - delta[r0:r1])).astype(BF16)
            dq_acc[pl.ds(r0, r1 - r0), :] += _dot_nn(ds, k)
            dk_acc[:, rows] += _dot_nn(q_t[:, r0:r1], ds)
            dv_acc[:, rows] += _dot_nn(do_t[:, r0:r1], p.astype(BF16))

        _sweep(qi, ratio, bk, step)
        dq_ref[...] = _rope_slab_bwd(dq_acc[...] * MLA_SCALE, c_ref[...], s_ref[...]).astype(dq_ref.dtype)

        @pl.when(qi == nq - 1)
        def _():
            dk_ref[...] = dk_acc[...].T.astype(dk_ref.dtype)
            dv_ref[...] = dv_acc[...].T.astype(dv_ref.dtype)

    blk = pl.BlockSpec((bq, LANES), lambda h, i: (i, h))
    tab = pl.BlockSpec((bq, LANES), lambda h, i: (i, 0))
    k_full = pl.BlockSpec((s, LANES), lambda h, i: (0, h))
    v_full = pl.BlockSpec((s, LANES), lambda h, i: (0, heads + h))
    shape = jax.ShapeDtypeStruct((s, heads * LANES), BF16)
    ins = [q, kv, kv, o, do, lse, cos_t, sin_t] + ([dkv_init, dkv_init] if has_init else [])
    dq, dk, dv = pl.pallas_call(
        body, name=name, grid=(heads, nq),
        in_specs=[blk, k_full, v_full, blk, blk, blk, tab, tab] + ([k_full, v_full] if has_init else []),
        out_specs=[blk, k_full, k_full],
        out_shape=[shape, shape, shape],
        scratch_shapes=[pltpu.VMEM((bq, LANES), F32), pltpu.VMEM((LANES, s), F32), pltpu.VMEM((LANES, s), F32)],
        compiler_params=pltpu.CompilerParams(dimension_semantics=("arbitrary", "arbitrary"),
                                             vmem_limit_bytes=VMEM_LIMIT),
    )(*ins)
    return dq, jnp.concatenate([dk, dv], axis=1)


def _pad_last(a, width):
    return jnp.pad(a, [(0, 0)] * (a.ndim - 1) + [(0, width - a.shape[-1])])


def _pad_qkv(w, heads):
    d = w.shape[0]
    return _pad_last(w.reshape(d, 3 * heads, SB_HEAD_DIM), LANES).reshape(d, 3 * heads * LANES)


def _unpad_qkv(g, heads):
    d = g.shape[0]
    return g.reshape(d, 3 * heads, LANES)[:, :, :SB_HEAD_DIM].reshape(d, 3 * heads * SB_HEAD_DIM)


def _pad_o(w, heads):
    d = w.shape[1]
    w = w.reshape(heads, SB_HEAD_DIM, d)
    return jnp.pad(w, [(0, 0), (0, LANES - SB_HEAD_DIM), (0, 0)]).reshape(heads * LANES, d)


def _unpad_o(g, heads):
    d = g.shape[1]
    return g.reshape(heads, LANES, d)[:, :SB_HEAD_DIM, :].reshape(heads * SB_HEAD_DIM, d)


def _pad_uq(w, heads):
    r = w.shape[0]
    return _pad_last(w.reshape(r, heads, MLA_NOPE + MLA_ROPE), LANES).reshape(r, heads * LANES)


def _unpad_uq(g, heads):
    r = g.shape[0]
    return g.reshape(r, heads, LANES)[:, :, :MLA_NOPE + MLA_ROPE].reshape(r, heads * (MLA_NOPE + MLA_ROPE))


def _pad_dkv(w):
    d = w.shape[0]
    rope = jnp.zeros((d, LANES), w.dtype).at[:, ROPE_LO:ROPE_LO + MLA_ROPE].set(w[:, MLA_KV_RANK:])
    return jnp.concatenate([w[:, :MLA_KV_RANK], rope], axis=1)


def _unpad_dkv(g):
    return jnp.concatenate([g[:, :MLA_KV_RANK], g[:, MLA_KV_RANK + ROPE_LO:MLA_KV_RANK + ROPE_LO + MLA_ROPE]], axis=1)


def _pad_ukv(w, heads):
    w = w.reshape(MLA_KV_RANK, heads, 2, MLA_NOPE)
    k_part = _pad_last(w[:, :, 0, :], LANES).reshape(MLA_KV_RANK, heads * LANES)
    v_part = _pad_last(w[:, :, 1, :], LANES).reshape(MLA_KV_RANK, heads * LANES)
    lane = jnp.arange(LANES)
    place = ((lane[:, None] == lane[None, :]) & (lane[:, None] >= ROPE_LO) & (lane[:, None] < ROPE_LO + MLA_ROPE))
    place = jnp.tile(place.astype(w.dtype), (1, heads))
    top = jnp.concatenate([k_part, v_part], axis=1)
    bottom = jnp.concatenate([place, jnp.zeros_like(place)], axis=1)
    return jnp.concatenate([top, bottom], axis=0)


def _unpad_ukv(g, heads):
    g = g[:MLA_KV_RANK]
    k_part = g[:, :heads * LANES].reshape(MLA_KV_RANK, heads, LANES)[:, :, :MLA_NOPE]
    v_part = g[:, heads * LANES:].reshape(MLA_KV_RANK, heads, LANES)[:, :, :MLA_V]
    return jnp.stack([k_part, v_part], axis=2).reshape(MLA_KV_RANK, heads * (MLA_NOPE + MLA_V))


def _rope_tables(positions):
    inv_freq = ROPE_THETA ** (-jnp.arange(0, MLA_ROPE, 2, dtype=F32) / MLA_ROPE)
    ang = positions.astype(F32)[:, None] * inv_freq
    cos, sin = jnp.cos(ang), jnp.sin(ang)
    s = positions.shape[0]
    cos_t = jnp.ones((s, LANES), F32).at[:, ROPE_LO:ROPE_LO + MLA_ROPE].set(jnp.concatenate([cos, cos], axis=1))
    sin_t = jnp.zeros((s, LANES), F32).at[:, ROPE_LO:ROPE_LO + MLA_ROPE].set(jnp.concatenate([-sin, sin], axis=1))
    return cos_t, sin_t


def _local_step(x, positions, target, qkv_w0, norms, rest_weights, ride=None, early_reduce=None):
    s, d = x.shape
    heads = d // SB_HEAD_DIM
    cos_t, sin_t = _rope_tables(positions)

    h_first = _rms_fwd(x, norms["attn_norm"][0], name="l0_attn_norm")
    qkv_first = _mm(h_first, _pad_qkv(qkv_w0, heads), name="l0_qkv")
    if ride is None:
        o_first, gathered = _sb_fwd(qkv_first, heads, name="l0_sb_fwd"), None
    else:
        o_first, gathered = _sb_fwd(qkv_first, heads, name="l0_sb_fwd", ride=ride)
    w = rest_weights(gathered)
    n_a = w["sb_w_qkv"].shape[0]
    n_b = w["mla_w_dq"].shape[0]
    depth = n_a + n_b

    wqkv = [_pad_qkv(w["sb_w_qkv"][l], heads) for l in range(n_a)]
    wo_a = [_pad_o(w["sb_w_o"][l], heads) for l in range(n_a)]
    wdkv = _pad_dkv(w["mla_w_dkv"])
    wkv = _pad_ukv(w["mla_w_ukv"], heads)
    wdq = [w["mla_w_dq"][j] for j in range(n_b)]
    wuq = [_pad_uq(w["mla_w_uq"][j], heads) for j in range(n_b)]
    wo_b = [_pad_o(w["mla_w_o"][j], heads) for j in range(n_b)]
    w1 = [w["mlp_w1"][l] for l in range(depth)]
    w2 = [w["mlp_w2"][l] for l in range(depth)]

    saved = []
    kv_saved = None
    kv = None
    for l in range(depth):
        t = f"l{l}"
        sv = {"x_in": x}
        h = h_first if l == 0 else _rms_fwd(x, norms["attn_norm"][l], name=f"{t}_attn_norm")
        sv["h"] = h
        if l < n_a:
            if l == 0:
                qkv, o = qkv_first, o_first
            else:
                qkv = _mm(h, wqkv[l], name=f"{t}_qkv")
                o = _sb_fwd(qkv, heads, name=f"{t}_sb_fwd")
            sv["qkv"], sv["o"] = qkv, o
            x = _mm(o, wo_a[l], name=f"{t}_attn_out", epilogue=_epi_add, extras=[(x, "tile")], out_dtypes=(F32,))
        else:
            j = l - n_a
            if j == 0:
                hk = _rms_fwd(x, norms["kv_norm"], name="kv_norm")
                down = _mm(hk, wdkv, name="kv_down", out_dtypes=(F32,))
                cat = _kv_prep(down, norms["mla_kv_lat_norm"], cos_t, sin_t, name="kv_prep")
                kv = _mm(cat, wkv, name="kv_up")
                kv_saved = {"x_in": x, "hk": hk, "down": down, "cat": cat}
            cq0 = _mm(h, wdq[j], name=f"{t}_q_down", out_dtypes=(F32,))
            cq = _rms_fwd(cq0, norms["mla_q_lat_norm"][j], name=f"{t}_q_lat_norm")
            q = _mm(cq, wuq[j], name=f"{t}_q_up", epilogue=_epi_rope_heads, extras=[(cos_t, "row"), (sin_t, "row")])
            o, lse = _mla_fwd(q, kv, heads, name=f"{t}_mla_fwd")
            sv.update(cq0=cq0, cq=cq, q=q, o=o, lse=lse)
            x = _mm(o, wo_b[j], name=f"{t}_attn_out", epilogue=_epi_add, extras=[(x, "tile")], out_dtypes=(F32,))
        sv["x_mid"] = x
        h2 = _rms_fwd(x, norms["mlp_norm"][l], name=f"{t}_mlp_norm")
        u, a = _mm(h2, w1[l], name=f"{t}_mlp_up", epilogue=_epi_relu2, out_dtypes=(BF16, BF16))
        sv.update(h2=h2, u=u, a=a)
        x = _mm(a, w2[l], name=f"{t}_mlp_down", epilogue=_epi_add, extras=[(x, "tile")], out_dtypes=(F32,))
        saved.append(sv)

    loss_slab, dx, dxb, dg_final = _loss_bwd(x, norms["final_norm"], target, name="loss")
    loss = loss_slab[0, 0]

    g_attn_norm, g_mlp_norm = [None] * depth, [None] * depth
    g_qkv, g_o_a = [None] * n_a, [None] * n_a
    g_dq, g_uq, g_o_b, g_qlat = [None] * n_b, [None] * n_b, [None] * n_b, [None] * n_b
    g_w1, g_w2 = [None] * depth, [None] * depth
    dkv = None
    g_kv_norm = g_kv_lat = g_dkv = g_ukv = None
    early_parts = early_got = None

    for l in reversed(range(depth)):
        t = f"l{l}"
        sv = saved[l]
        du = _mm(dxb, w2[l], name=f"{t}_mlp_down_dx", dims="nt", epilogue=_epi_relu2_grad, extras=[(sv["u"], "tile")])
        g_w2[l] = _mm(sv["a"], dxb, name=f"{t}_mlp_down_dw", dims="tn", out_dtypes=(F32,))
        g_w1[l] = _mm(sv["h2"], du, name=f"{t}_mlp_up_dw", dims="tn", out_dtypes=(F32,))
        dx, dxb, g_mlp_norm[l] = _mm_rms_bwd(du, w1[l], sv["x_mid"], norms["mlp_norm"][l], dx, name=f"{t}_mlp_up_dx")
        if l < n_a:
            do = _mm(dxb, wo_a[l], name=f"{t}_attn_out_dx", dims="nt")
            g_o_a[l] = _unpad_o(_mm(sv["o"], dxb, name=f"{t}_attn_out_dw", dims="tn", out_dtypes=(F32,)), heads)
            if l == 0 and early_reduce is not None:
                early_parts = early_reduce({
                    "sb_w_qkv": g_qkv[1:], "sb_w_o": g_o_a, "mla_w_dkv": g_dkv, "mla_w_ukv": g_ukv, "mla_w_dq": g_dq,
                    "mla_w_uq": g_uq, "mla_w_o": g_o_b, "mlp_w1": g_w1, "mlp_w2": g_w2})
                dq, dk, dv, early_got = _sb_bwd(sv["qkv"], sv["o"], do, heads, name=f"{t}_sb_bwd", ride=early_parts)
            else:
                dq, dk, dv = _sb_bwd(sv["qkv"], sv["o"], do, heads, name=f"{t}_sb_bwd")
            dqkv = jnp.concatenate([dq, dk, dv], axis=1)
            g_qkv[l] = _unpad_qkv(_mm(sv["h"], dqkv, name=f"{t}_qkv_dw", dims="tn", out_dtypes=(F32,)), heads)
            dh_src, dh_w = dqkv, wqkv[l]
        else:
            j = l - n_a
            do = _mm(dxb, wo_b[j], name=f"{t}_attn_out_dx", dims="nt")
            g_o_b[j] = _unpad_o(_mm(sv["o"], dxb, name=f"{t}_attn_out_dw", dims="tn", out_dtypes=(F32,)), heads)
            dq, dkv = _mla_bwd(sv["q"], kv, sv["o"], do, sv["lse"], cos_t, sin_t, dkv, heads, name=f"{t}_mla_bwd")
            g_uq[j] = _unpad_uq(_mm(sv["cq"], dq, name=f"{t}_q_up_dw", dims="tn", out_dtypes=(F32,)), heads)
            _, dcq0, g_qlat[j] = _mm_rms_bwd(dq, wuq[j], sv["cq0"], norms["mla_q_lat_norm"][j], None,
                                             name=f"{t}_q_up_dx")
            g_dq[j] = _mm(sv["h"], dcq0, name=f"{t}_q_down_dw", dims="tn", out_dtypes=(F32,))
            dh_src, dh_w = dcq0, wdq[j]
        if l == 0:
            dh = _mm(dh_src, dh_w, name=f"{t}_attn_in_dx", dims="nt", out_dtypes=(F32,))
            dx, dxb, g_attn_norm[l] = _rms_bwd(sv["x_in"], norms["attn_norm"][l], dh, dx, name=f"{t}_attn_norm_bwd",
                                               lead_axis=True)
        else:
            dx, dxb, g_attn_norm[l] = _mm_rms_bwd(dh_src, dh_w, sv["x_in"], norms["attn_norm"][l], dx,
                                                  name=f"{t}_attn_in_dx")
        if l == n_a:
            ks = kv_saved
            dcat = _mm(dkv, wkv, name="kv_up_dx", dims="nt", out_dtypes=(F32,))
            g_ukv = _unpad_ukv(_mm(ks["cat"], dkv, name="kv_up_dw", dims="tn", out_dtypes=(F32,)), heads)
            ddown, g_kv_lat = _kv_prep_bwd(ks["down"], norms["mla_kv_lat_norm"], cos_t, sin_t, dcat, name="kv_prep_bwd")
            g_dkv = _unpad_dkv(_mm(ks["hk"], ddown, name="kv_down_dw", dims="tn", out_dtypes=(F32,)))
            dx, dxb, g_kv_norm = _mm_rms_bwd(ddown, wdkv, ks["x_in"], norms["kv_norm"], dx, name="kv_down_dx")

    grads = {
        "attn_norm": jnp.concatenate(g_attn_norm, axis=0), "mlp_norm": jnp.concatenate(g_mlp_norm, axis=0),
        "sb_w_qkv": g_qkv, "sb_w_o": g_o_a,
        "kv_norm": g_kv_norm[0], "mla_w_dkv": g_dkv, "mla_kv_lat_norm": g_kv_lat[0], "mla_w_ukv": g_ukv,
        "mla_w_dq": g_dq, "mla_q_lat_norm": jnp.concatenate(g_qlat, axis=0),
        "mla_w_uq": g_uq, "mla_w_o": g_o_b,
        "mlp_w1": g_w1, "mlp_w2": g_w2, "final_norm": dg_final[0],
    }
    return loss, dx, grads, early_parts, early_got


def _flat_rows(n_elems):
    per_block = FLAT_COLS * FLAT_ROW_BLOCK * 2
    return -(-n_elems // per_block) * FLAT_ROW_BLOCK * 2


def _row_blocks(arrays, dtype):
    for a in arrays:
        assert a.size % FLAT_COLS == 0, a.shape
    blocks = [a.astype(dtype).reshape(-1, FLAT_COLS) for a in arrays]
    used = sum(b.shape[0] for b in blocks)
    rows = _flat_rows(used * FLAT_COLS)
    return blocks + [jnp.zeros((rows - used, FLAT_COLS), dtype)], rows


def _pack(arrays, dtype):
    blocks, _ = _row_blocks(arrays, dtype)
    return jnp.concatenate(blocks, axis=0)


def _pack_chips(per_chip, dtype):
    blocks, rows = [], 0
    for arrays in per_chip:
        chip_blocks, rows = _row_blocks(arrays, dtype)
        blocks += chip_blocks
    return jnp.concatenate(blocks, axis=0).reshape(len(per_chip), rows, FLAT_COLS)


def _unpack(flat, shapes):
    out, row = [], 0
    for shp in shapes:
        n = 1
        for v in shp:
            n *= v
        out.append(flat[row:row + n // FLAT_COLS].reshape(shp))
        row += n // FLAT_COLS
    return out


def _pack_small(arrays):
    rows = []
    for a in arrays:
        a = a.reshape(-1, a.shape[-1]) if a.shape[-1] == FLAT_COLS else a.reshape(1, -1)
        rows.append(_pad_last(a, FLAT_COLS))
    flat = jnp.concatenate(rows, axis=0)
    return jnp.pad(flat, [(0, -flat.shape[0] % 8), (0, 0)])


def _unpack_small(flat, shapes):
    out, row = [], 0
    for shp in shapes:
        if shp[-1] == FLAT_COLS:
            n = 1
            for v in shp[:-1]:
                n *= v
            out.append(flat[row:row + n].reshape(shp))
            row += n
        else:
            n = 1
            for v in shp:
                n *= v
            out.append(flat[row, :n].reshape(shp))
            row += 1
    return out


def _other_chips(x, y):
    return [(1 - x, y), (x, 1 - y), (1 - x, 1 - y)]


def _all_gather_chips(flat, *, name):
    rows, cols = flat.shape

    def body(x_ref, out_ref, *sems):
        _all_gather_start(x_ref, out_ref, sems)
        _all_gather_finish(x_ref, out_ref, sems)

    return pl.pallas_call(
        body, name=name,
        in_specs=[pl.BlockSpec(memory_space=pltpu.HBM)],
        out_specs=pl.BlockSpec(memory_space=pltpu.HBM),
        out_shape=jax.ShapeDtypeStruct((N_CHIPS, rows, cols), flat.dtype),
        scratch_shapes=_all_gather_sems(),
        compiler_params=pltpu.CompilerParams(has_side_effects=True),
    )(flat)


def _all_gather_sems():
    return [pltpu.SemaphoreType.DMA((3,)), pltpu.SemaphoreType.DMA((3,)), pltpu.SemaphoreType.DMA((3,)),
            pltpu.SemaphoreType.DMA((3,)), pltpu.SemaphoreType.DMA, pltpu.SemaphoreType.DMA]


def _all_gather_copies(x_ref, out_ref, sems, finishing):
    send_sems, recv_sems, pass_send_sems, pass_recv_sems, own_send_sem, own_recv_sem = sems
    x, y, c = lax.axis_index("x"), lax.axis_index("y"), lax.axis_index("c")
    me = 2 * x + y
    my_rows, sib_rows = _half_rows(x_ref.shape[0])

    def copy(src, dst, send_sem, recv_sem, to):
        return pltpu.make_async_remote_copy(src_ref=src, dst_ref=dst, send_sem=send_sem, recv_sem=recv_sem,
                                            device_id=to, device_id_type=MESH)

    own = copy(x_ref, out_ref.at[me], own_send_sem, own_recv_sem, _sibling())
    to_chips, landed, pass_on, passed = [], [], [], []
    for k, (px, py) in enumerate(_other_chips(x, y)):
        to_chips.append(copy(x_ref.at[my_rows, :], out_ref.at[me, my_rows, :], send_sems.at[k], recv_sems.at[k],
                             (px, py, c)))
        if finishing:
            mine, theirs = out_ref.at[2 * px + py, my_rows, :], out_ref.at[2 * px + py, sib_rows, :]
            landed.append(copy(mine, mine, send_sems.at[k], recv_sems.at[k], (px, py, c)))
            pass_on.append(copy(mine, mine, pass_send_sems.at[k], pass_recv_sems.at[k], _sibling()))
            passed.append(copy(theirs, theirs, pass_send_sems.at[k], pass_recv_sems.at[k], _sibling()))
    return own, to_chips, landed, pass_on, passed


def _all_gather_start(x_ref, out_ref, sems):
    own, to_chips, _, _, _ = _all_gather_copies(x_ref, out_ref, sems, finishing=False)
    own.start()
    for cp in to_chips:
        cp.start()


def _all_gather_finish(x_ref, out_ref, sems):
    own, to_chips, landed, pass_on, passed = _all_gather_copies(x_ref, out_ref, sems, finishing=True)
    for k in range(len(landed)):
        landed[k].wait_recv()
        pass_on[k].start()
    for cp in passed:
        cp.wait_recv()
    own.wait_recv()
    for cp in [own] + to_chips + pass_on:
        cp.wait_send()


def _exchange_chips(parts, *, name):
    def body(g_ref, out_ref, *sems):
        _exchange_start(g_ref, out_ref, sems)
        _exchange_finish(g_ref, out_ref, sems)

    return pl.pallas_call(
        body, name=name,
        in_specs=[pl.BlockSpec(memory_space=pltpu.HBM)],
        out_specs=pl.BlockSpec(memory_space=pltpu.HBM),
        out_shape=jax.ShapeDtypeStruct(parts.shape, parts.dtype),
        scratch_shapes=_exchange_sems(),
        compiler_params=pltpu.CompilerParams(has_side_effects=True),
    )(parts)


def _exchange_sems():
    return [pltpu.SemaphoreType.DMA((3,)), pltpu.SemaphoreType.DMA((3,))]


def _exchange_copies(g_ref, out_ref, sems, receiving):
    send_sems, recv_sems = sems
    x, y, c = lax.axis_index("x"), lax.axis_index("y"), lax.axis_index("c")
    me = 2 * x + y
    copies = []
    for k, (px, py) in enumerate(_other_chips(x, y)):
        src, dst = (g_ref.at[me], out_ref.at[2 * px + py]) if receiving else (g_ref.at[2 * px + py], out_ref.at[me])
        copies.append(pltpu.make_async_remote_copy(src_ref=src, dst_ref=dst, send_sem=send_sems.at[k],
                                                   recv_sem=recv_sems.at[k], device_id=(px, py, c),
                                                   device_id_type=MESH))
    return copies


def _exchange_start(g_ref, out_ref, sems):
    for cp in _exchange_copies(g_ref, out_ref, sems, receiving=False):
        cp.start()


def _exchange_finish(g_ref, out_ref, sems):
    for cp in _exchange_copies(g_ref, out_ref, sems, receiving=True):
        cp.wait_recv()
    for cp in _exchange_copies(g_ref, out_ref, sems, receiving=False):
        cp.wait_send()


def _my_chip():
    return 2 * lax.axis_index("x") + lax.axis_index("y")


def _half_rows(rows):
    c = lax.axis_index("c")
    half = rows // 2
    return pl.ds(pl.multiple_of(c * half, 8), half), pl.ds(pl.multiple_of((1 - c) * half, 8), half)


def _sibling():
    return (lax.axis_index("x"), lax.axis_index("y"), 1 - lax.axis_index("c"))


def _pair_exchange(parts, *, name):
    n, rows, cols = parts.shape

    def body(p_ref, theirs_ref, send_sem, recv_sem):
        _, sib_rows = _half_rows(rows)
        cp = pltpu.make_async_remote_copy(src_ref=p_ref.at[:, sib_rows, :], dst_ref=theirs_ref, send_sem=send_sem,
                                          recv_sem=recv_sem, device_id=_sibling(), device_id_type=MESH)
        cp.start()
        cp.wait()

    half = rows // 2
    theirs = pl.pallas_call(
        body, name=name,
        in_specs=[pl.BlockSpec(memory_space=pltpu.HBM)],
        out_specs=pl.BlockSpec(memory_space=pltpu.HBM),
        out_shape=jax.ShapeDtypeStruct((n, half, cols), parts.dtype),
        scratch_shapes=[pltpu.SemaphoreType.DMA, pltpu.SemaphoreType.DMA],
        compiler_params=pltpu.CompilerParams(has_side_effects=True),
    )(parts)
    mine = lax.dynamic_slice_in_dim(parts, lax.axis_index("c") * half, half, axis=1)
    return mine, theirs


def _pair_sum(mine, theirs, *, name):
    n, rows, cols = mine.shape

    def body(a_ref, b_ref, o_ref):
        o_ref[...] = (a_ref[...].astype(F32) + b_ref[...].astype(F32)).astype(o_ref.dtype)

    blk = pl.BlockSpec((n, FLAT_ROW_BLOCK, cols), lambda i: (0, i, 0))
    return pl.pallas_call(
        body, name=name, grid=(rows // FLAT_ROW_BLOCK,),
        in_specs=[blk, blk], out_specs=blk, out_shape=jax.ShapeDtypeStruct(mine.shape, mine.dtype),
        compiler_params=pltpu.CompilerParams(dimension_semantics=("parallel",), vmem_limit_bytes=VMEM_LIMIT),
    )(mine, theirs)


def _sum_chips(received, own, *, name):
    _, rows, cols = received.shape

    def body(p_ref, own_ref, o_ref):
        me = 2 * lax.axis_index("x") + lax.axis_index("y")
        slot = [jnp.where(me == j, own_ref[j], p_ref[j]).astype(F32) for j in range(N_CHIPS)]
        o_ref[...] = ((slot[0] + slot[1]) + slot[2]) + slot[3]

    blk = pl.BlockSpec((N_CHIPS, FLAT_ROW_BLOCK, cols), lambda i: (0, i, 0))
    return pl.pallas_call(
        body, name=name, grid=(rows // FLAT_ROW_BLOCK,),
        in_specs=[blk, blk],
        out_specs=pl.BlockSpec((FLAT_ROW_BLOCK, cols), lambda i: (i, 0)),
        out_shape=jax.ShapeDtypeStruct((rows, cols), F32),
        compiler_params=pltpu.CompilerParams(dimension_semantics=("parallel",), vmem_limit_bytes=VMEM_LIMIT),
    )(received, own)


def _join_cores(half, *, name):
    rows2, cols = half.shape

    def body(h_ref, out_ref, send_sem, recv_sem):
        my_rows, sib_rows = _half_rows(2 * rows2)
        cp = pltpu.make_async_remote_copy(src_ref=h_ref, dst_ref=out_ref.at[my_rows, :], send_sem=send_sem,
                                          recv_sem=recv_sem, device_id=_sibling(), device_id_type=MESH)
        cp.start()
        cp.wait_send()
        pltpu.make_async_remote_copy(src_ref=h_ref, dst_ref=out_ref.at[sib_rows, :], send_sem=send_sem,
                                     recv_sem=recv_sem, device_id=_sibling(), device_id_type=MESH).wait_recv()

    out = pl.pallas_call(
        body, name=name,
        in_specs=[pl.BlockSpec(memory_space=pltpu.HBM)],
        out_specs=pl.BlockSpec(memory_space=pltpu.HBM),
        out_shape=jax.ShapeDtypeStruct((2 * rows2, cols), half.dtype),
        scratch_shapes=[pltpu.SemaphoreType.DMA, pltpu.SemaphoreType.DMA],
        compiler_params=pltpu.CompilerParams(has_side_effects=True),
    )(half)
    return lax.dynamic_update_slice_in_dim(out, half, lax.axis_index("c") * rows2, axis=0)


def _all_reduce_small(v, *, name):
    rows, cols = v.shape
    flips = [(fx, fy, fc) for fx in (0, 1) for fy in (0, 1) for fc in (0, 1)][1:]

    def body(v_ref, out_ref, gath_ref, send_sems, recv_sems):
        x, y, c = lax.axis_index("x"), lax.axis_index("y"), lax.axis_index("c")
        me = 4 * x + 2 * y + c
        gath_ref[me] = v_ref[...]
        peers = [((1 - x) if fx else x, (1 - y) if fy else y, (1 - c) if fc else c) for fx, fy, fc in flips]
        sends = []
        for k, peer in enumerate(peers):
            cp = pltpu.make_async_remote_copy(src_ref=v_ref, dst_ref=gath_ref.at[me], send_sem=send_sems.at[k],
                                              recv_sem=recv_sems.at[k], device_id=peer, device_id_type=MESH)
            cp.start()
            sends.append(cp)
        for k, (px, py, pc) in enumerate(peers):
            pltpu.make_async_remote_copy(src_ref=v_ref, dst_ref=gath_ref.at[4 * px + 2 * py + pc],
                                         send_sem=send_sems.at[k], recv_sem=recv_sems.at[k],
                                         device_id=(px, py, pc), device_id_type=MESH).wait_recv()
        for cp in sends:
            cp.wait_send()
        total = gath_ref[0]
        for k in range(1, 8):
            total = total + gath_ref[k]
        out_ref[...] = total

    total, _ = pl.pallas_call(
        body, name=name,
        in_specs=[pl.BlockSpec(memory_space=pltpu.VMEM)],
        out_specs=[pl.BlockSpec(memory_space=pltpu.VMEM), pl.BlockSpec(memory_space=pltpu.VMEM)],
        out_shape=[jax.ShapeDtypeStruct((rows, cols), v.dtype), jax.ShapeDtypeStruct((8, rows, cols), v.dtype)],
        scratch_shapes=[pltpu.SemaphoreType.DMA((7,)), pltpu.SemaphoreType.DMA((7,))],
        compiler_params=pltpu.CompilerParams(has_side_effects=True),
    )(v)
    return total


def _adamw(w, g, m, v, *, name):
    shape = w.shape
    cols = shape[-1]
    w2, g2, m2, v2 = (a.reshape(-1, cols) for a in (w, g, m, v))
    rows = w2.shape[0]
    br = _pick_rows(rows, FLAT_ROW_BLOCK)

    def body(w_ref, g_ref, m_ref, v_ref, d_out, m_out, v_out):
        gv = g_ref[...]
        m_new = ADAM_B1 * m_ref[...] + (1.0 - ADAM_B1) * gv
        v_new = ADAM_B2 * v_ref[...] + (1.0 - ADAM_B2) * jnp.square(gv)
        m_hat = m_new / (1.0 - ADAM_B1 ** ADAM_STEP)
        v_hat = v_new / (1.0 - ADAM_B2 ** ADAM_STEP)
        d_out[...] = -ADAM_LR * (m_hat / (jnp.sqrt(v_hat) + ADAM_EPS) + ADAM_WD * w_ref[...])
        m_out[...] = m_new
        v_out[...] = v_new

    blk = pl.BlockSpec((br, cols), lambda i: (i, 0))
    out = jax.ShapeDtypeStruct((rows, cols), F32)
    outs = pl.pallas_call(
        body, name=name, grid=(rows // br,),
        in_specs=[blk] * 4, out_specs=[blk] * 3, out_shape=[out] * 3,
        compiler_params=pltpu.CompilerParams(dimension_semantics=("parallel",), vmem_limit_bytes=VMEM_LIMIT),
    )(w2, g2, m2, v2)
    return [o.reshape(shape) for o in outs]


def _pick_rows(rows, target):
    if rows <= target:
        return rows
    return max(b for b in range(8, target + 1, 8) if rows % b == 0)


def _assemble(gathered_shards, name, layer=False):
    return jnp.concatenate(gathered_shards, axis=SHARD_AXIS[name] - int(layer))


def _chip_shard(full, name, j):
    if isinstance(full, list):
        axis = SHARD_AXIS[name] - 1
        layers = full
    else:
        axis = SHARD_AXIS[name]
        layers = [full]
    n = layers[0].shape[axis] // N_CHIPS
    return [lax.slice_in_dim(g, j * n, (j + 1) * n, axis=axis) for g in layers]


def kernel(x, positions, attn_norm, mlp_norm, sb_w_qkv, sb_w_o, kv_norm, mla_w_dkv, mla_kv_lat_norm, mla_w_ukv, mla_w_dq, mla_q_lat_norm, mla_w_uq, mla_w_o, mlp_w1, mlp_w2, final_norm, loss_target, m_attn_norm, m_mlp_norm, m_sb_w_qkv, m_sb_w_o, m_kv_norm, m_mla_w_dkv, m_mla_kv_lat_norm, m_mla_w_ukv, m_mla_w_dq, m_mla_q_lat_norm, m_mla_w_uq, m_mla_w_o, m_mlp_w1, m_mlp_w2, m_final_norm, v_attn_norm, v_mlp_norm, v_sb_w_qkv, v_sb_w_o, v_kv_norm, v_mla_w_dkv, v_mla_kv_lat_norm, v_mla_w_ukv, v_mla_w_dq, v_mla_q_lat_norm, v_mla_w_uq, v_mla_w_o, v_mlp_w1, v_mlp_w2, v_final_norm):
    weights = dict(attn_norm=attn_norm, mlp_norm=mlp_norm, sb_w_qkv=sb_w_qkv, sb_w_o=sb_w_o, kv_norm=kv_norm,
                   mla_w_dkv=mla_w_dkv, mla_kv_lat_norm=mla_kv_lat_norm, mla_w_ukv=mla_w_ukv, mla_w_dq=mla_w_dq,
                   mla_q_lat_norm=mla_q_lat_norm, mla_w_uq=mla_w_uq, mla_w_o=mla_w_o, mlp_w1=mlp_w1, mlp_w2=mlp_w2,
                   final_norm=final_norm)
    m_in = dict(attn_norm=m_attn_norm, mlp_norm=m_mlp_norm, sb_w_qkv=m_sb_w_qkv, sb_w_o=m_sb_w_o, kv_norm=m_kv_norm,
                mla_w_dkv=m_mla_w_dkv, mla_kv_lat_norm=m_mla_kv_lat_norm, mla_w_ukv=m_mla_w_ukv, mla_w_dq=m_mla_w_dq,
                mla_q_lat_norm=m_mla_q_lat_norm, mla_w_uq=m_mla_w_uq, mla_w_o=m_mla_w_o, mlp_w1=m_mlp_w1,
                mlp_w2=m_mlp_w2, final_norm=m_final_norm)
    v_in = dict(attn_norm=v_attn_norm, mlp_norm=v_mlp_norm, sb_w_qkv=v_sb_w_qkv, sb_w_o=v_sb_w_o, kv_norm=v_kv_norm,
                mla_w_dkv=v_mla_w_dkv, mla_kv_lat_norm=v_mla_kv_lat_norm, mla_w_ukv=v_mla_w_ukv, mla_w_dq=v_mla_w_dq,
                mla_q_lat_norm=v_mla_q_lat_norm, mla_w_uq=v_mla_w_uq, mla_w_o=v_mla_w_o, mlp_w1=v_mlp_w1,
                mlp_w2=v_mlp_w2, final_norm=v_final_norm)
    shard_shapes = [weights[n].shape for n in BIG_WEIGHTS]
    small_shapes = [weights[n].shape for n in SMALL_WEIGHTS]

    first_name = BIG_WEIGHTS[0]
    qkv_first, qkv_later = weights[first_name][0], weights[first_name][1:]
    gathered_first = _all_gather_chips(_pack([qkv_first], BF16), name="first_weight_all_gather")
    qkv_w0 = _assemble([_unpack(gathered_first[j], [qkv_first.shape])[0] for j in range(N_CHIPS)], first_name, layer=True)
    ride = _pack([qkv_later] + [weights[n] for n in BIG_WEIGHTS[1:]], BF16)
    ride_shapes = [qkv_later.shape] + shard_shapes[1:]

    def rest_weights(gathered):
        per_chip = [_unpack(gathered[j], ride_shapes) for j in range(N_CHIPS)]
        full = {n: _assemble([per_chip[j][i] for j in range(N_CHIPS)], n) for i, n in enumerate(BIG_WEIGHTS)}
        full[first_name] = jnp.concatenate([qkv_w0[None], full[first_name]], axis=0)
        return full

    norms = {n: weights[n] for n in SMALL_WEIGHTS}

    def chip_parts(g, tag):
        parts = _pack_chips([[piece for n in BIG_WEIGHTS if n in g for piece in _chip_shard(g[n], n, j)]
                             for j in range(N_CHIPS)], BF16)
        mine, theirs = _pair_exchange(parts, name=f"grads_pair_exchange_{tag}")
        return _pair_sum(mine, theirs, name=f"grads_pair_sum_{tag}")

    def finish(received, chip_part, tag):
        g_half = _sum_chips(received, chip_part, name=f"grads_sum_chips_{tag}")
        return _join_cores(g_half, name=f"grads_join_cores_{tag}")

    loss, dx, grads, early_parts, early_got = _local_step(
        x[0], positions[0], loss_target, qkv_w0, norms, rest_weights, ride=ride,
        early_reduce=functools.partial(chip_parts, tag="early"))
    loss = lax.psum(loss, ("x", "y", "c"))
    early_sum = finish(early_got, early_parts, "early")
    last_parts = chip_parts({first_name: grads[first_name][:1]}, "last")
    last_sum = finish(_exchange_chips(last_parts, name="grads_exchange_last"), last_parts, "last")

    out_g = dict(zip(BIG_WEIGHTS, _unpack(early_sum, ride_shapes)))
    out_g[first_name] = jnp.concatenate([_unpack(last_sum, [qkv_first.shape])[0][None], out_g[first_name]], axis=0)
    out_d, out_m, out_v = {}, {}, {}
    for n in BIG_WEIGHTS:
        out_d[n], out_m[n], out_v[n] = _adamw(weights[n], out_g[n], m_in[n], v_in[n], name=f"adamw_{n}")

    small_sum = _all_reduce_small(_pack_small([grads[n] for n in SMALL_WEIGHTS]), name="gains_all_reduce")
    sd, sm, sv = _adamw(_pack_small([weights[n] for n in SMALL_WEIGHTS]), small_sum,
                        _pack_small([m_in[n] for n in SMALL_WEIGHTS]),
                        _pack_small([v_in[n] for n in SMALL_WEIGHTS]), name="adamw_gains")
    out_g.update(zip(SMALL_WEIGHTS, _unpack_small(small_sum, small_shapes)))
    out_d.update(zip(SMALL_WEIGHTS, _unpack_small(sd, small_shapes)))
    out_m.update(zip(SMALL_WEIGHTS, _unpack_small(sm, small_shapes)))
    out_v.update(zip(SMALL_WEIGHTS, _unpack_small(sv, small_shapes)))

    return (loss, dx, *[out_g[n] for n in ALL_WEIGHTS], *[out_d[n] for n in ALL_WEIGHTS],
            *[out_m[n] for n in ALL_WEIGHTS], *[out_v[n] for n in ALL_WEIGHTS])
```

```python
import functools

import jax
import jax.numpy as jnp
from jax import lax
from jax.experimental import pallas as pl
from jax.experimental.pallas import tpu as pltpu

F32 = jnp.float32
BF16 = jnp.bfloat16

LANES = 128
SB_HEAD_DIM = 64
MLA_NOPE = 64
MLA_ROPE = 32
MLA_V = 64
MLA_Q_RANK = 384
MLA_KV_RANK = 256
CHUNK = 64
ROPE_THETA = 10000.0
NORM_EPS = 1e-6
SB_SCALE = SB_HEAD_DIM ** -0.5
MLA_SCALE = (MLA_NOPE + MLA_ROPE) ** -0.5
ROPE_LO = MLA_NOPE
ROPE_HALF = MLA_ROPE // 2
ATT_Q_BLOCK = 1024
ATT_K_BLOCK = 256
MLA_FWD_K_BLOCK = 512
NEG_BIG = -1e30
SB_DEAD_LOG = -110.0
VMEM_LIMIT = 56 * 1024 * 1024

ADAM_LR = 0.001
ADAM_B1 = 0.9
ADAM_B2 = 0.999
ADAM_EPS = 1e-08
ADAM_WD = 0.01
ADAM_STEP = 10

FLAT_COLS = 1024
FLAT_ROW_BLOCK = 256
N_CHIPS = 4
MESH = pl.DeviceIdType.MESH

BIG_WEIGHTS = ["sb_w_qkv", "sb_w_o", "mla_w_dkv", "mla_w_ukv", "mla_w_dq", "mla_w_uq", "mla_w_o", "mlp_w1", "mlp_w2"]
SHARD_AXIS = {"sb_w_qkv": 2, "sb_w_o": 1, "mla_w_dkv": 0, "mla_w_ukv": 1, "mla_w_dq": 1, "mla_w_uq": 2,
              "mla_w_o": 1, "mlp_w1": 2, "mlp_w2": 1}
SMALL_WEIGHTS = ["attn_norm", "mlp_norm", "kv_norm", "mla_kv_lat_norm", "mla_q_lat_norm", "final_norm"]
ALL_WEIGHTS = ["attn_norm", "mlp_norm", "sb_w_qkv", "sb_w_o", "kv_norm", "mla_w_dkv", "mla_kv_lat_norm", "mla_w_ukv",
               "mla_w_dq", "mla_q_lat_norm", "mla_w_uq", "mla_w_o", "mlp_w1", "mlp_w2", "final_norm"]


def _dot(a, b, dims):
    return lax.dot_general(a, b, (dims, ((), ())), preferred_element_type=F32)


def _dot_nn(a, b):
    return _dot(a, b, ((1,), (0,)))


def _dot_nt(a, b):
    return _dot(a, b, ((1,), (1,)))


def _dot_tn(a, b):
    return _dot(a, b, ((0,), (0,)))


def _pick_block(n, target):
    if n <= target:
        return n
    best = max(b for b in range(LANES, target + 1, LANES) if n % b == 0)
    return best


MM_ROWS = 512
MM_COLS = 1024
MM_DEPTH = 4096
MM_DEPTH_TN = 2048


def _mm(a, b, *, name, dims="nn", epilogue=None, extras=(), out_dtypes=(BF16,), column_sum=False):
    if dims == "nn":
        (m, k), (k2, n) = a.shape, b.shape
    elif dims == "nt":
        (m, k), (n, k2) = a.shape, b.shape
    else:
        (k, m), (k2, n) = a.shape, b.shape
    assert k == k2, (name, a.shape, b.shape)
    if dims == "tn":
        bm, bn, bk = _pick_block(m, MM_COLS), _pick_block(n, MM_COLS), _pick_block(k, MM_DEPTH_TN)
    else:
        rows = MM_ROWS if k > MM_DEPTH // 2 else 2 * MM_ROWS
        cols = 2 * MM_COLS if (k <= MM_COLS and n >= 4 * MM_COLS) else MM_COLS
        bm, bn, bk = _pick_block(m, rows), _pick_block(n, cols), _pick_block(k, MM_DEPTH)
    nk = k // bk
    if dims == "tn":
        a_spec = pl.BlockSpec((bk, bm), lambda j, i, kk: (kk, i))
    else:
        a_spec = pl.BlockSpec((bm, bk), lambda j, i, kk: (i, kk))
    if dims == "nt":
        b_spec = pl.BlockSpec((bn, bk), lambda j, i, kk: (j, kk))
    else:
        b_spec = pl.BlockSpec((bk, bn), lambda j, i, kk: (kk, j))
    extra_specs = []
    for arr, kind in extras:
        if kind == "tile":
            assert arr.shape == (m, n), (name, arr.shape)
            extra_specs.append(pl.BlockSpec((bm, bn), lambda j, i, kk: (i, j)))
        elif kind == "vec":
            assert arr.shape == (1, n), (name, arr.shape)
            extra_specs.append(pl.BlockSpec((1, bn), lambda j, i, kk: (0, j)))
        else:
            assert arr.shape == (m, LANES), (name, arr.shape)
            extra_specs.append(pl.BlockSpec((bm, LANES), lambda j, i, kk: (i, 0)))
    n_extra = len(extras)
    n_out = len(out_dtypes)
    n_sum = int(column_sum)
    dot = {"nn": _dot_nn, "nt": _dot_nt, "tn": _dot_tn}[dims]

    def body(*refs):
        a_ref, b_ref = refs[0], refs[1]
        extra_refs = refs[2:2 + n_extra]
        out_refs = refs[2 + n_extra:2 + n_extra + n_out]

        def finish(acc):
            outs = (acc,) if epilogue is None else epilogue(acc, *[r[...] for r in extra_refs])
            for o_ref, o in zip(out_refs, outs):
                o_ref[...] = o.astype(o_ref.dtype)
            if column_sum:
                sum_ref = refs[2 + n_extra + n_out]
                first_rows = pl.program_id(1) == 0

                @pl.when(first_rows)
                def _():
                    sum_ref[...] = outs[n_out]

                @pl.when(jnp.logical_not(first_rows))
                def _():
                    sum_ref[...] += outs[n_out]

        part = dot(a_ref[...].astype(BF16), b_ref[...].astype(BF16))
        if nk == 1:
            finish(part)
            return
        acc_ref = refs[-1]
        kk = pl.program_id(2)

        @pl.when(kk == 0)
        def _():
            acc_ref[...] = part

        @pl.when(kk > 0)
        def _():
            acc_ref[...] += part

        @pl.when(kk == nk - 1)
        def _():
            finish(acc_ref[...])

    outs = pl.pallas_call(
        body, name=name, grid=(n // bn, m // bm, nk),
        in_specs=[a_spec, b_spec] + extra_specs,
        out_specs=[pl.BlockSpec((bm, bn), lambda j, i, kk: (i, j)) for _ in range(n_out)]
        + [pl.BlockSpec((1, bn), lambda j, i, kk: (0, j))] * n_sum,
        out_shape=[jax.ShapeDtypeStruct((m, n), dt) for dt in out_dtypes] + [jax.ShapeDtypeStruct((1, n), F32)] * n_sum,
        scratch_shapes=[pltpu.VMEM((bm, bn), F32)] if nk > 1 else [],
        compiler_params=pltpu.CompilerParams(
            dimension_semantics=("parallel", "arbitrary" if column_sum else "parallel", "arbitrary"),
            vmem_limit_bytes=VMEM_LIMIT),
    )(a, b, *[arr for arr, _ in extras])
    return outs[0] if n_out + n_sum == 1 else outs


def _epi_add(acc, res):
    return (res + acc,)


def _epi_relu2(acc):
    r = jnp.maximum(acc, 0.0)
    return acc, r * r


def _epi_relu2_grad(acc, u):
    return (acc * (2.0 * jnp.maximum(u.astype(F32), 0.0)),)


def _rope_slab(t, cos_t, sin_t):
    lane = lax.broadcasted_iota(jnp.int32, t.shape, 1)
    partner = jnp.where(lane < ROPE_LO + ROPE_HALF, pltpu.roll(t, LANES - ROPE_HALF, 1), pltpu.roll(t, ROPE_HALF, 1))
    return t * cos_t + partner * sin_t


def _rope_slab_bwd(d, cos_t, sin_t):
    ds = d * sin_t
    lane = lax.broadcasted_iota(jnp.int32, d.shape, 1)
    partner = jnp.where(lane < ROPE_LO + ROPE_HALF, pltpu.roll(ds, LANES - ROPE_HALF, 1), pltpu.roll(ds, ROPE_HALF, 1))
    in_rope = (lane >= ROPE_LO) & (lane < ROPE_LO + MLA_ROPE)
    return d * cos_t + jnp.where(in_rope, partner, 0.0)


def _epi_rope_heads(acc, cos_t, sin_t):
    slabs = [_rope_slab(acc[:, j * LANES:(j + 1) * LANES], cos_t, sin_t) for j in range(acc.shape[1] // LANES)]
    return (jnp.concatenate(slabs, axis=1) * MLA_SCALE,)


def _row_block(s):
    return min(512, s)


def _rms_fwd(x, g, *, name):
    s, d = x.shape
    bm = _row_block(s)

    def body(x_ref, g_ref, o_ref):
        xv = x_ref[...]
        r = lax.rsqrt(jnp.mean(xv * xv, axis=-1, keepdims=True) + NORM_EPS)
        o_ref[...] = ((xv * r) * g_ref[...]).astype(o_ref.dtype)

    return pl.pallas_call(
        body, name=name, grid=(s // bm,),
        in_specs=[pl.BlockSpec((bm, d), lambda i: (i, 0)), pl.BlockSpec((1, d), lambda i: (0, 0))],
        out_specs=pl.BlockSpec((bm, d), lambda i: (i, 0)),
        out_shape=jax.ShapeDtypeStruct((s, d), BF16),
        compiler_params=pltpu.CompilerParams(dimension_semantics=("parallel",), vmem_limit_bytes=VMEM_LIMIT),
    )(x, g.reshape(1, d))


def _rms_bwd_math(xv, gv, dy):
    r = lax.rsqrt(jnp.mean(xv * xv, axis=-1, keepdims=True) + NORM_EPS)
    xhat = xv * r
    dyg = dy * gv
    mdot = jnp.mean(dyg * xhat, axis=-1, keepdims=True)
    dx = r * (dyg - xhat * mdot)
    dg = jnp.sum(dy * xhat, axis=0, keepdims=True)
    return dx, dg


def _rms_bwd(x, g, dy, dres, *, name, lead_axis=False):
    s, d = x.shape
    bm = _row_block(s)
    has_res = dres is not None

    def body(*refs):
        x_ref, g_ref, dy_ref = refs[:3]
        dres_ref = refs[3] if has_res else None
        dx_ref, dxb_ref, dg_ref = refs[-3:]
        dx, dg = _rms_bwd_math(x_ref[...], g_ref[...], dy_ref[...].astype(F32))
        if has_res:
            dx = dx + dres_ref[...]
        dx_ref[...] = dx
        dxb_ref[...] = dx.astype(BF16)

        @pl.when(pl.program_id(0) == 0)
        def _():
            dg_ref[...] = jnp.zeros_like(dg_ref)

        dg_ref[...] += dg

    row = pl.BlockSpec((bm, d), lambda i: (i, 0))
    vec = pl.BlockSpec((1, d), lambda i: (0, 0))
    ins = [x, g.reshape(1, d), dy] + ([dres] if has_res else [])
    dx_spec, dx_shape = row, (s, d)
    if lead_axis:
        dx_spec, dx_shape = pl.BlockSpec((None, bm, d), lambda i: (0, i, 0)), (1, s, d)
    return pl.pallas_call(
        body, name=name, grid=(s // bm,),
        in_specs=[row, vec, row] + ([row] if has_res else []),
        out_specs=[dx_spec, row, vec],
        out_shape=[jax.ShapeDtypeStruct(dx_shape, F32), jax.ShapeDtypeStruct((s, d), BF16),
                   jax.ShapeDtypeStruct((1, d), F32)],
        compiler_params=pltpu.CompilerParams(dimension_semantics=("arbitrary",), vmem_limit_bytes=VMEM_LIMIT),
    )(*ins)


def _epi_rms_bwd(acc, x, g, dres=None):
    dx, dg = _rms_bwd_math(x, g, acc)
    if dres is not None:
        dx = dx + dres
    return dx, dx, dg


def _mm_rms_bwd(dy_src, w, x, g, dres, *, name):
    d = x.shape[1]
    assert w.shape[0] == d and d <= MM_COLS, (name, w.shape, x.shape)
    extras = [(x, "tile"), (g.reshape(1, d), "vec")] + ([(dres, "tile")] if dres is not None else [])
    return _mm(dy_src, w, name=name, dims="nt", epilogue=_epi_rms_bwd, extras=extras, out_dtypes=(F32, BF16),
               column_sum=True)


def _loss_bwd(x, g, target, *, name):
    s, d = x.shape
    bm = _row_block(s)

    def body(x_ref, g_ref, t_ref, loss_ref, dx_ref, dxb_ref, dg_ref):
        xv, gv = x_ref[...], g_ref[...]
        r = lax.rsqrt(jnp.mean(xv * xv, axis=-1, keepdims=True) + NORM_EPS)
        err = (xv * r) * gv - t_ref[...]
        dx, dg = _rms_bwd_math(xv, gv, err * (1.0 / d))
        dx_ref[...] = dx
        dxb_ref[...] = dx.astype(BF16)

        @pl.when(pl.program_id(0) == 0)
        def _():
            dg_ref[...] = jnp.zeros_like(dg_ref)
            loss_ref[...] = jnp.zeros_like(loss_ref)

        dg_ref[...] += dg
        loss_ref[...] += jnp.sum(jnp.mean(err * err, axis=-1, keepdims=True), axis=0, keepdims=True) * 0.5

    row = pl.BlockSpec((bm, d), lambda i: (i, 0))
    vec = pl.BlockSpec((1, d), lambda i: (0, 0))
    assert target.shape == (1, s, d), target.shape
    return pl.pallas_call(
        body, name=name, grid=(s // bm,),
        in_specs=[row, vec, pl.BlockSpec((None, bm, d), lambda i: (0, i, 0))],
        out_specs=[pl.BlockSpec((8, LANES), lambda i: (0, 0)), row, row, vec],
        out_shape=[jax.ShapeDtypeStruct((8, LANES), F32), jax.ShapeDtypeStruct((s, d), F32),
                   jax.ShapeDtypeStruct((s, d), BF16), jax.ShapeDtypeStruct((1, d), F32)],
        compiler_params=pltpu.CompilerParams(dimension_semantics=("arbitrary",), vmem_limit_bytes=VMEM_LIMIT),
    )(x, g.reshape(1, d), target)


def _kv_prep(down, g, cos_t, sin_t, *, name):
    s, w = down.shape
    bm = _row_block(s)

    def body(d_ref, g_ref, c_ref, s_ref, o_ref):
        lat = d_ref[:, :MLA_KV_RANK]
        r = lax.rsqrt(jnp.mean(lat * lat, axis=-1, keepdims=True) + NORM_EPS)
        o_ref[:, :MLA_KV_RANK] = ((lat * r) * g_ref[...]).astype(BF16)
        o_ref[:, MLA_KV_RANK:] = _rope_slab(d_ref[:, MLA_KV_RANK:], c_ref[...], s_ref[...]).astype(BF16)

    row = pl.BlockSpec((bm, w), lambda i: (i, 0))
    tab = pl.BlockSpec((bm, LANES), lambda i: (i, 0))
    return pl.pallas_call(
        body, name=name, grid=(s // bm,),
        in_specs=[row, pl.BlockSpec((1, MLA_KV_RANK), lambda i: (0, 0)), tab, tab],
        out_specs=row, out_shape=jax.ShapeDtypeStruct((s, w), BF16),
        compiler_params=pltpu.CompilerParams(dimension_semantics=("parallel",), vmem_limit_bytes=VMEM_LIMIT),
    )(down, g.reshape(1, MLA_KV_RANK), cos_t, sin_t)


def _kv_prep_bwd(down, g, cos_t, sin_t, dcat, *, name):
    s, w = down.shape
    bm = _row_block(s)

    def body(d_ref, g_ref, c_ref, s_ref, dc_ref, o_ref, dg_ref):
        dlat, dg = _rms_bwd_math(d_ref[:, :MLA_KV_RANK], g_ref[...], dc_ref[:, :MLA_KV_RANK])
        o_ref[:, :MLA_KV_RANK] = dlat.astype(BF16)
        o_ref[:, MLA_KV_RANK:] = _rope_slab_bwd(dc_ref[:, MLA_KV_RANK:], c_ref[...], s_ref[...]).astype(BF16)

        @pl.when(pl.program_id(0) == 0)
        def _():
            dg_ref[...] = jnp.zeros_like(dg_ref)

        dg_ref[...] += dg

    row = pl.BlockSpec((bm, w), lambda i: (i, 0))
    tab = pl.BlockSpec((bm, LANES), lambda i: (i, 0))
    vec = pl.BlockSpec((1, MLA_KV_RANK), lambda i: (0, 0))
    return pl.pallas_call(
        body, name=name, grid=(s // bm,),
        in_specs=[row, vec, tab, tab, row],
        out_specs=[row, vec],
        out_shape=[jax.ShapeDtypeStruct((s, w), BF16), jax.ShapeDtypeStruct((1, MLA_KV_RANK), F32)],
        compiler_params=pltpu.CompilerParams(dimension_semantics=("arbitrary",), vmem_limit_bytes=VMEM_LIMIT),
    )(down, g.reshape(1, MLA_KV_RANK), cos_t, sin_t, dcat)


def _split_bf16(v):
    hi = v.astype(BF16)
    lo = (v - hi.astype(F32)).astype(BF16)
    return hi, lo


def _suffix_matrices(n):
    row = lax.broadcasted_iota(jnp.int32, (n, n), 0)
    col = lax.broadcasted_iota(jnp.int32, (n, n), 1)
    incl = (row >= col).astype(BF16)
    return (row > col).astype(BF16), jnp.concatenate([incl, incl], axis=0)


def _suffix_sum(v, matrix):
    hi, lo = _split_bf16(v)
    return _dot_nn(jnp.concatenate([hi, lo], axis=1), matrix)


def _block_positions(qi, kb, bq, bk, r0, r1):
    row = qi * bq + r0 + lax.broadcasted_iota(jnp.int32, (r1 - r0, bk), 0)
    col = kb * bk + lax.broadcasted_iota(jnp.int32, (r1 - r0, bk), 1)
    return row, col


def _att_blocks(s, key_block=ATT_K_BLOCK):
    bq, bk = min(ATT_Q_BLOCK, s), min(key_block, s)
    return bq, bk, s // bq, bq // bk


def _sweep(qi, ratio, bk, step, unroll=2, alive=None):
    bq = ratio * bk
    for d in range(ratio):
        kb, r0 = (qi + 1) * ratio - 1 - d, (ratio - 1 - d) * bk
        near = bq if alive is None else min(r0 + 2 * bk, bq)
        step(kb, True, r0, near)
        if near < bq:
            pl.when(alive(near))(functools.partial(step, kb, False, near, bq))
    unroll = unroll if ratio % unroll == 0 else 1
    trips = qi * (ratio // unroll)

    def trip(i):
        for u in range(unroll):
            kb = qi * ratio - 1 - (i * unroll + u)
            if alive is None or ratio == 1:
                step(kb, False, 0, bq)
            else:
                step(kb, False, 0, bk)
                pl.when(alive(bk))(functools.partial(step, kb, False, bk, bq))

    if alive is None:
        lax.fori_loop(0, trips, lambda i, carry: (trip(i), carry)[1], 0)
    else:
        lax.while_loop(lambda i: jnp.logical_and(i < trips, alive(0)), lambda i: (trip(i), i + 1)[1], 0)


def _stick_left(c_ref, r0):
    return jnp.max(c_ref[r0:, :]) > SB_DEAD_LOG


def _sb_logs(q, k):
    z = _dot_nt(q, k)
    lb = jnp.minimum(z, 0.0) - jnp.log(1.0 + jnp.exp(-jnp.abs(z)))
    return lb, lb - z


def _sb_fwd(qkv, heads, *, name, ride=None):
    s = qkv.shape[0]
    bq, bk, nq, ratio = _att_blocks(s)
    riding = ride is not None

    def body(*refs):
        if riding:
            q_ref, k_ref, v_ref, w_ref, o_ref, gath_ref, acc_ref, c_ref = refs[:8]
            first = jnp.logical_and(pl.program_id(0) == 0, pl.program_id(1) == 0)
            last = jnp.logical_and(pl.program_id(0) == heads - 1, pl.program_id(1) == nq - 1)
            pl.when(first)(functools.partial(_all_gather_start, w_ref, gath_ref, refs[8:]))
        else:
            q_ref, k_ref, v_ref, o_ref, acc_ref, c_ref = refs
        qi = pl.program_id(1)
        q = q_ref[...] * SB_SCALE
        m_strict, _ = _suffix_matrices(bk)
        acc_ref[...] = jnp.zeros_like(acc_ref)
        c_ref[...] = jnp.zeros_like(c_ref)

        def step(kb, masked, r0, r1):
            rows = pl.ds(pl.multiple_of(kb * bk, bk), bk)
            mine = pl.ds(r0, r1 - r0)
            k, v = k_ref[rows, :], v_ref[rows, :]
            lb, lk = _sb_logs(q[r0:r1], k)
            if masked:
                row, col = _block_positions(qi, kb, bq, bk, r0, r1)
                causal = col < row
                lk = jnp.where(causal, lk, 0.0)
            c = c_ref[mine, :]
            w = jnp.exp(lb + _dot_nn(lk.astype(BF16), m_strict) + jnp.tile(c, (1, bk // LANES)))
            if masked:
                w = jnp.where(causal, w, 0.0)
            acc_ref[mine, :] += _dot_nn(w.astype(BF16), v)
            c_ref[mine, :] = c + jnp.sum(lk, axis=-1, keepdims=True)

        _sweep(qi, ratio, bk, step, unroll=1, alive=functools.partial(_stick_left, c_ref))
        o_ref[...] = acc_ref[...].astype(o_ref.dtype)
        if riding:
            pl.when(last)(functools.partial(_all_gather_finish, w_ref, gath_ref, refs[8:]))

    hbm = pl.BlockSpec(memory_space=pltpu.HBM)
    o_spec = pl.BlockSpec((bq, LANES), lambda h, i: (i, h))
    o_shape = jax.ShapeDtypeStruct((s, heads * LANES), F32)
    return pl.pallas_call(
        body, name=name, grid=(heads, nq),
        in_specs=[pl.BlockSpec((bq, LANES), lambda h, i: (i, h)),
                  pl.BlockSpec((s, LANES), lambda h, i: (0, heads + h)),
                  pl.BlockSpec((s, LANES), lambda h, i: (0, 2 * heads + h))] + ([hbm] if riding else []),
        out_specs=[o_spec, hbm] if riding else o_spec,
        out_shape=[o_shape, jax.ShapeDtypeStruct((N_CHIPS,) + ride.shape, ride.dtype)] if riding else o_shape,
        scratch_shapes=[pltpu.VMEM((bq, LANES), F32), pltpu.VMEM((bq, LANES), F32)]
        + (_all_gather_sems() if riding else []),
        compiler_params=pltpu.CompilerParams(dimension_semantics=("arbitrary", "arbitrary"),
                                             vmem_limit_bytes=VMEM_LIMIT, has_side_effects=riding),
    )(*([qkv, qkv, qkv] + ([ride] if riding else [])))


def _sb_bwd(qkv, o, do, heads, *, name, ride=None):
    s = qkv.shape[0]
    bq, bk, nq, ratio = _att_blocks(s)
    riding = ride is not None

    def body(*refs):
        if riding:
            (q_ref, k_ref, v_ref, o_ref, do_ref, g_ref, dq_ref, dk_ref, dv_ref, got_ref,
             dq_acc, dk_acc, dv_acc, c_ref, e_ref) = refs[:15]
            first = jnp.logical_and(pl.program_id(0) == 0, pl.program_id(1) == 0)
            last = jnp.logical_and(pl.program_id(0) == heads - 1, pl.program_id(1) == nq - 1)
            pl.when(first)(functools.partial(_exchange_start, g_ref, got_ref, refs[15:]))
        else:
            q_ref, k_ref, v_ref, o_ref, do_ref, dq_ref, dk_ref, dv_ref, dq_acc, dk_acc, dv_acc, c_ref, e_ref = refs
        qi = pl.program_id(1)

        @pl.when(qi == 0)
        def _():
            dk_acc[...] = jnp.zeros_like(dk_acc)
            dv_acc[...] = jnp.zeros_like(dv_acc)

        q = q_ref[...] * SB_SCALE
        do = do_ref[...]
        q_t, do_t = q.T, do.T
        total = jnp.sum(do.astype(F32) * o_ref[...].astype(F32), axis=-1, keepdims=True)
        m_strict, m_incl = _suffix_matrices(bk)
        dq_acc[...] = jnp.zeros_like(dq_acc)
        c_ref[...] = jnp.zeros_like(c_ref)
        e_ref[...] = jnp.broadcast_to(total, e_ref.shape)
        reps = (1, bk // LANES)

        def step(kb, masked, r0, r1):
            rows = pl.ds(pl.multiple_of(kb * bk, bk), bk)
            mine = pl.ds(r0, r1 - r0)
            k, v = k_ref[rows, :], v_ref[rows, :]
            qs, dos = q[r0:r1], do[r0:r1]
            lb, lk_all = _sb_logs(qs, k)
            lk = lk_all
            if masked:
                row, col = _block_positions(qi, kb, bq, bk, r0, r1)
                causal = col < row
                lk = jnp.where(causal, lk_all, 0.0)
            c = c_ref[mine, :]
            w = jnp.exp(lb + _dot_nn(lk.astype(BF16), m_strict) + jnp.tile(c, reps))
            if masked:
                w = jnp.where(causal, w, 0.0)
            wb = w.astype(BF16)
            g = wb.astype(F32) * _dot_nt(dos, v)
            e = e_ref[mine, :]
            g_left = jnp.tile(e, reps) - _suffix_sum(g, m_incl)
            da = g * jnp.exp(lk_all) - jnp.exp(lb) * g_left
            if masked:
                da = jnp.where(causal, da, 0.0)
            dab = da.astype(BF16)
            dq_acc[mine, :] += _dot_nn(dab, k)
            dk_acc[:, rows] += _dot_nn(q_t[:, r0:r1], dab)
            dv_acc[:, rows] += _dot_nn(do_t[:, r0:r1], wb)
            e_ref[mine, :] = e - jnp.sum(g, axis=-1, keepdims=True)
            c_ref[mine, :] = c + jnp.sum(lk, axis=-1, keepdims=True)

        _sweep(qi, ratio, bk, step, unroll=1, alive=functools.partial(_stick_left, c_ref))
        dq_ref[...] = (dq_acc[...] * SB_SCALE).astype(dq_ref.dtype)

        @pl.when(qi == nq - 1)
        def _():
            dk_ref[...] = dk_acc[...].T.astype(dk_ref.dtype)
            dv_ref[...] = dv_acc[...].T.astype(dv_ref.dtype)

        if riding:
            pl.when(last)(functools.partial(_exchange_finish, g_ref, got_ref, refs[15:]))

    blk = pl.BlockSpec((bq, LANES), lambda h, i: (i, h))
    full = pl.BlockSpec((s, LANES), lambda h, i: (0, h))
    hbm = pl.BlockSpec(memory_space=pltpu.HBM)
    shape = jax.ShapeDtypeStruct((s, heads * LANES), BF16)
    return pl.pallas_call(
        body, name=name, grid=(heads, nq),
        in_specs=[blk,
                  pl.BlockSpec((s, LANES), lambda h, i: (0, heads + h)),
                  pl.BlockSpec((s, LANES), lambda h, i: (0, 2 * heads + h)),
                  blk, blk] + ([hbm] if riding else []),
        out_specs=[blk, full, full] + ([hbm] if riding else []),
        out_shape=[shape, shape, shape] + ([jax.ShapeDtypeStruct(ride.shape, ride.dtype)] if riding else []),
        scratch_shapes=[pltpu.VMEM((bq, LANES), F32), pltpu.VMEM((LANES, s), F32), pltpu.VMEM((LANES, s), F32),
                        pltpu.VMEM((bq, LANES), F32), pltpu.VMEM((bq, LANES), F32)]
        + (_exchange_sems() if riding else []),
        compiler_params=pltpu.CompilerParams(dimension_semantics=("arbitrary", "arbitrary"),
                                             vmem_limit_bytes=VMEM_LIMIT, has_side_effects=riding),
    )(*([qkv, qkv, qkv, o, do] + ([ride] if riding else [])))


def _chunk_allowed(qi, kb, bq, bk, r0, r1):
    row, col = _block_positions(qi, kb, bq, bk, r0, r1)
    return (col // CHUNK) <= (row // CHUNK)


def _mla_fwd(q, kv, heads, *, name):
    s = q.shape[0]
    bq, bk, nq, ratio = _att_blocks(s, MLA_FWD_K_BLOCK)
    reps = (1, bk // LANES)

    def body(q_ref, k_ref, v_ref, o_ref, lse_ref, acc_ref, m_ref, l_ref):
        qi = pl.program_id(1)
        qv = q_ref[...]
        acc_ref[...] = jnp.zeros_like(acc_ref)
        m_ref[...] = jnp.full_like(m_ref, NEG_BIG)
        l_ref[...] = jnp.zeros_like(l_ref)

        def step(kb, masked, r0, r1):
            rows = pl.ds(pl.multiple_of(kb * bk, bk), bk)
            mine = pl.ds(r0, r1 - r0)
            k, v = k_ref[rows, :], v_ref[rows, :]
            sc = _dot_nt(qv[r0:r1], k)
            if masked:
                allowed = _chunk_allowed(qi, kb, bq, bk, r0, r1)
                sc = jnp.where(allowed, sc, NEG_BIG)
            m_old = m_ref[mine, :]
            m_new = jnp.maximum(m_old, jnp.max(sc, axis=-1, keepdims=True))
            p = jnp.exp(sc - jnp.tile(m_new, reps))
            alpha = jnp.exp(m_old - m_new)
            l_ref[mine, :] = alpha * l_ref[mine, :] + jnp.sum(p, axis=-1, keepdims=True)
            acc_ref[mine, :] = alpha * acc_ref[mine, :] + _dot_nn(p.astype(BF16), v)
            m_ref[mine, :] = m_new

        _sweep(qi, ratio, bk, step)
        o_ref[...] = (acc_ref[...] / l_ref[...]).astype(o_ref.dtype)
        lse_ref[...] = m_ref[...] + jnp.log(l_ref[...])

    blk = pl.BlockSpec((bq, LANES), lambda h, i: (i, h))
    return pl.pallas_call(
        body, name=name, grid=(heads, nq),
        in_specs=[blk,
                  pl.BlockSpec((s, LANES), lambda h, i: (0, h)),
                  pl.BlockSpec((s, LANES), lambda h, i: (0, heads + h))],
        out_specs=[blk, blk],
        out_shape=[jax.ShapeDtypeStruct((s, heads * LANES), BF16), jax.ShapeDtypeStruct((s, heads * LANES), F32)],
        scratch_shapes=[pltpu.VMEM((bq, LANES), F32), pltpu.VMEM((bq, LANES), F32), pltpu.VMEM((bq, LANES), F32)],
        compiler_params=pltpu.CompilerParams(dimension_semantics=("parallel", "arbitrary"),
                                             vmem_limit_bytes=VMEM_LIMIT),
    )(q, kv, kv)


def _mla_bwd(q, kv, o, do, lse, cos_t, sin_t, dkv_init, heads, *, name):
    s = q.shape[0]
    bq, bk, nq, ratio = _att_blocks(s)
    reps = (1, bk // LANES)
    has_init = dkv_init is not None

    def body(*refs):
        q_ref, k_ref, v_ref, o_ref, do_ref, lse_ref, c_ref, s_ref = refs[:8]
        ki_ref, vi_ref = (refs[8], refs[9]) if has_init else (None, None)
        dq_ref, dk_ref, dv_ref, dq_acc, dk_acc, dv_acc = refs[-6:]
        qi = pl.program_id(1)

        @pl.when(qi == 0)
        def _():
            if has_init:
                dk_acc[...] = ki_ref[...].astype(F32).T
                dv_acc[...] = vi_ref[...].astype(F32).T
            else:
                dk_acc[...] = jnp.zeros_like(dk_acc)
                dv_acc[...] = jnp.zeros_like(dv_acc)

        qv = q_ref[...]
        do = do_ref[...]
        q_t, do_t = qv.T, do.T
        delta = jnp.sum(do.astype(F32) * o_ref[...].astype(F32), axis=-1, keepdims=True)
        lse_wide = jnp.tile(lse_ref[...], reps)
        dq_acc[...] = jnp.zeros_like(dq_acc)

        def step(kb, masked, r0, r1):
            rows = pl.ds(pl.multiple_of(kb * bk, bk), bk)
            k, v = k_ref[rows, :], v_ref[rows, :]
            qs, dos = qv[r0:r1], do[r0:r1]
            p = jnp.exp(_dot_nt(qs, k) - lse_wide[r0:r1])
            if masked:
                p = jnp.where(_chunk_allowed(qi, kb, bq, bk, r0, r1), p, 0.0)
            ds = (p * (_dot_nt(dos, v) - delta[r0:r1])).astype(BF16)
            dq_acc[pl.ds(r0, r1 - r0), :] += _dot_nn(ds, k)
            dk_acc[:, rows] += _dot_nn(q_t[:, r0:r1], ds)
            dv_acc[:, rows] += _dot_nn(do_t[:, r0:r1], p.astype(BF16))

        _sweep(qi, ratio, bk, step, unroll=4)
        dq_ref[...] = _rope_slab_bwd(dq_acc[...] * MLA_SCALE, c_ref[...], s_ref[...]).astype(dq_ref.dtype)

        @pl.when(qi == nq - 1)
        def _():
            dk_ref[...] = dk_acc[...].T.astype(dk_ref.dtype)
            dv_ref[...] = dv_acc[...].T.astype(dv_ref.dtype)

    blk = pl.BlockSpec((bq, LANES), lambda h, i: (i, h))
    tab = pl.BlockSpec((bq, LANES), lambda h, i: (i, 0))
    k_full = pl.BlockSpec((s, LANES), lambda h, i: (0, h))
    v_full = pl.BlockSpec((s, LANES), lambda h, i: (0, heads + h))
    shape = jax.ShapeDtypeStruct((s, heads * LANES), BF16)
    ins = [q, kv, kv, o, do, lse, cos_t, sin_t] + ([dkv_init, dkv_init] if has_init else [])
    dq, dk, dv = pl.pallas_call(
        body, name=name, grid=(heads, nq),
        in_specs=[blk, k_full, v_full, blk, blk, blk, tab, tab] + ([k_full, v_full] if has_init else []),
        out_specs=[blk, k_full, k_full],
        out_shape=[shape, shape, shape],
        scratch_shapes=[pltpu.VMEM((bq, LANES), F32), pltpu.VMEM((LANES, s), F32), pltpu.VMEM((LANES, s), F32)],
        compiler_params=pltpu.CompilerParams(dimension_semantics=("arbitrary", "arbitrary"),
                                             vmem_limit_bytes=VMEM_LIMIT),
    )(*ins)
    return dq, jnp.concatenate([dk, dv], axis=1)


def _pad_last(a, width):
    return jnp.pad(a, [(0, 0)] * (a.ndim - 1) + [(0, width - a.shape[-1])])


def _pad_qkv(w, heads):
    d = w.shape[0]
    return _pad_last(w.reshape(d, 3 * heads, SB_HEAD_DIM), LANES).reshape(d, 3 * heads * LANES)


def _unpad_qkv(g, heads):
    d = g.shape[0]
    return g.reshape(d, 3 * heads, LANES)[:, :, :SB_HEAD_DIM].reshape(d, 3 * heads * SB_HEAD_DIM)


def _pad_o(w, heads):
    d = w.shape[1]
    w = w.reshape(heads, SB_HEAD_DIM, d)
    return jnp.pad(w, [(0, 0), (0, LANES - SB_HEAD_DIM), (0, 0)]).reshape(heads * LANES, d)


def _unpad_o(g, heads):
    d = g.shape[1]
    return g.reshape(heads, LANES, d)[:, :SB_HEAD_DIM, :].reshape(heads * SB_HEAD_DIM, d)


def _pad_uq(w, heads):
    r = w.shape[0]
    return _pad_last(w.reshape(r, heads, MLA_NOPE + MLA_ROPE), LANES).reshape(r, heads * LANES)


def _unpad_uq(g, heads):
    r = g.shape[0]
    return g.reshape(r, heads, LANES)[:, :, :MLA_NOPE + MLA_ROPE].reshape(r, heads * (MLA_NOPE + MLA_ROPE))


def _pad_dkv(w):
    d = w.shape[0]
    rope = jnp.zeros((d, LANES), w.dtype).at[:, ROPE_LO:ROPE_LO + MLA_ROPE].set(w[:, MLA_KV_RANK:])
    return jnp.concatenate([w[:, :MLA_KV_RANK], rope], axis=1)


def _unpad_dkv(g):
    return jnp.concatenate([g[:, :MLA_KV_RANK], g[:, MLA_KV_RANK + ROPE_LO:MLA_KV_RANK + ROPE_LO + MLA_ROPE]], axis=1)


def _pad_ukv(w, heads):
    w = w.reshape(MLA_KV_RANK, heads, 2, MLA_NOPE)
    k_part = _pad_last(w[:, :, 0, :], LANES).reshape(MLA_KV_RANK, heads * LANES)
    v_part = _pad_last(w[:, :, 1, :], LANES).reshape(MLA_KV_RANK, heads * LANES)
    lane = jnp.arange(LANES)
    place = ((lane[:, None] == lane[None, :]) & (lane[:, None] >= ROPE_LO) & (lane[:, None] < ROPE_LO + MLA_ROPE))
    place = jnp.tile(place.astype(w.dtype), (1, heads))
    top = jnp.concatenate([k_part, v_part], axis=1)
    bottom = jnp.concatenate([place, jnp.zeros_like(place)], axis=1)
    return jnp.concatenate([top, bottom], axis=0)


def _unpad_ukv(g, heads):
    g = g[:MLA_KV_RANK]
    k_part = g[:, :heads * LANES].reshape(MLA_KV_RANK, heads, LANES)[:, :, :MLA_NOPE]
    v_part = g[:, heads * LANES:].reshape(MLA_KV_RANK, heads, LANES)[:, :, :MLA_V]
    return jnp.stack([k_part, v_part], axis=2).reshape(MLA_KV_RANK, heads * (MLA_NOPE + MLA_V))


def _rope_tables(positions):
    inv_freq = ROPE_THETA ** (-jnp.arange(0, MLA_ROPE, 2, dtype=F32) / MLA_ROPE)
    ang = positions.astype(F32)[:, None] * inv_freq
    cos, sin = jnp.cos(ang), jnp.sin(ang)
    s = positions.shape[0]
    cos_t = jnp.ones((s, LANES), F32).at[:, ROPE_LO:ROPE_LO + MLA_ROPE].set(jnp.concatenate([cos, cos], axis=1))
    sin_t = jnp.zeros((s, LANES), F32).at[:, ROPE_LO:ROPE_LO + MLA_ROPE].set(jnp.concatenate([-sin, sin], axis=1))
    return cos_t, sin_t


def _local_step(x, positions, target, qkv_w0, norms, rest_weights, ride=None, early_reduce=None):
    s, d = x.shape
    heads = d // SB_HEAD_DIM
    cos_t, sin_t = _rope_tables(positions)

    h_first = _rms_fwd(x, norms["attn_norm"][0], name="l0_attn_norm")
    qkv_first = _mm(h_first, _pad_qkv(qkv_w0, heads), name="l0_qkv")
    if ride is None:
        o_first, gathered = _sb_fwd(qkv_first, heads, name="l0_sb_fwd"), None
    else:
        o_first, gathered = _sb_fwd(qkv_first, heads, name="l0_sb_fwd", ride=ride)
    w = rest_weights(gathered)
    n_a = w["sb_w_qkv"].shape[0]
    n_b = w["mla_w_dq"].shape[0]
    depth = n_a + n_b

    wqkv = [_pad_qkv(w["sb_w_qkv"][l], heads) for l in range(n_a)]
    wo_a = [_pad_o(w["sb_w_o"][l], heads) for l in range(n_a)]
    wdkv = _pad_dkv(w["mla_w_dkv"])
    wkv = _pad_ukv(w["mla_w_ukv"], heads)
    wdq = [w["mla_w_dq"][j] for j in range(n_b)]
    wuq = [_pad_uq(w["mla_w_uq"][j], heads) for j in range(n_b)]
    wo_b = [_pad_o(w["mla_w_o"][j], heads) for j in range(n_b)]
    w1 = [w["mlp_w1"][l] for l in range(depth)]
    w2 = [w["mlp_w2"][l] for l in range(depth)]

    saved = []
    kv_saved = None
    kv = None
    for l in range(depth):
        t = f"l{l}"
        sv = {"x_in": x}
        h = h_first if l == 0 else _rms_fwd(x, norms["attn_norm"][l], name=f"{t}_attn_norm")
        sv["h"] = h
        if l < n_a:
            if l == 0:
                qkv, o = qkv_first, o_first
            else:
                qkv = _mm(h, wqkv[l], name=f"{t}_qkv")
                o = _sb_fwd(qkv, heads, name=f"{t}_sb_fwd")
            sv["qkv"], sv["o"] = qkv, o
            x = _mm(o, wo_a[l], name=f"{t}_attn_out", epilogue=_epi_add, extras=[(x, "tile")], out_dtypes=(F32,))
        else:
            j = l - n_a
            if j == 0:
                hk = _rms_fwd(x, norms["kv_norm"], name="kv_norm")
                down = _mm(hk, wdkv, name="kv_down", out_dtypes=(F32,))
                cat = _kv_prep(down, norms["mla_kv_lat_norm"], cos_t, sin_t, name="kv_prep")
                kv = _mm(cat, wkv, name="kv_up")
                kv_saved = {"x_in": x, "hk": hk, "down": down, "cat": cat}
            cq0 = _mm(h, wdq[j], name=f"{t}_q_down", out_dtypes=(F32,))
            cq = _rms_fwd(cq0, norms["mla_q_lat_norm"][j], name=f"{t}_q_lat_norm")
            q = _mm(cq, wuq[j], name=f"{t}_q_up", epilogue=_epi_rope_heads, extras=[(cos_t, "row"), (sin_t, "row")])
            o, lse = _mla_fwd(q, kv, heads, name=f"{t}_mla_fwd")
            sv.update(cq0=cq0, cq=cq, q=q, o=o, lse=lse)
            x = _mm(o, wo_b[j], name=f"{t}_attn_out", epilogue=_epi_add, extras=[(x, "tile")], out_dtypes=(F32,))
        sv["x_mid"] = x
        h2 = _rms_fwd(x, norms["mlp_norm"][l], name=f"{t}_mlp_norm")
        u, a = _mm(h2, w1[l], name=f"{t}_mlp_up", epilogue=_epi_relu2, out_dtypes=(BF16, BF16))
        sv.update(h2=h2, u=u, a=a)
        x = _mm(a, w2[l], name=f"{t}_mlp_down", epilogue=_epi_add, extras=[(x, "tile")], out_dtypes=(F32,))
        saved.append(sv)

    loss_slab, dx, dxb, dg_final = _loss_bwd(x, norms["final_norm"], target, name="loss")
    loss = loss_slab[0, 0]

    g_attn_norm, g_mlp_norm = [None] * depth, [None] * depth
    g_qkv, g_o_a = [None] * n_a, [None] * n_a
    g_dq, g_uq, g_o_b, g_qlat = [None] * n_b, [None] * n_b, [None] * n_b, [None] * n_b
    g_w1, g_w2 = [None] * depth, [None] * depth
    dkv = None
    g_kv_norm = g_kv_lat = g_dkv = g_ukv = None
    early_parts = early_got = None

    for l in reversed(range(depth)):
        t = f"l{l}"
        sv = saved[l]
        du = _mm(dxb, w2[l], name=f"{t}_mlp_down_dx", dims="nt", epilogue=_epi_relu2_grad, extras=[(sv["u"], "tile")])
        g_w2[l] = _mm(sv["a"], dxb, name=f"{t}_mlp_down_dw", dims="tn", out_dtypes=(F32,))
        g_w1[l] = _mm(sv["h2"], du, name=f"{t}_mlp_up_dw", dims="tn", out_dtypes=(F32,))
        dx, dxb, g_mlp_norm[l] = _mm_rms_bwd(du, w1[l], sv["x_mid"], norms["mlp_norm"][l], dx, name=f"{t}_mlp_up_dx")
        if l < n_a:
            do = _mm(dxb, wo_a[l], name=f"{t}_attn_out_dx", dims="nt")
            g_o_a[l] = _unpad_o(_mm(sv["o"], dxb, name=f"{t}_attn_out_dw", dims="tn", out_dtypes=(F32,)), heads)
            if l == 0 and early_reduce is not None:
                early_parts = early_reduce({
                    "sb_w_qkv": g_qkv[1:], "sb_w_o": g_o_a, "mla_w_dkv": g_dkv, "mla_w_ukv": g_ukv, "mla_w_dq": g_dq,
                    "mla_w_uq": g_uq, "mla_w_o": g_o_b, "mlp_w1": g_w1, "mlp_w2": g_w2})
                dq, dk, dv, early_got = _sb_bwd(sv["qkv"], sv["o"], do, heads, name=f"{t}_sb_bwd", ride=early_parts)
            else:
                dq, dk, dv = _sb_bwd(sv["qkv"], sv["o"], do, heads, name=f"{t}_sb_bwd")
            dqkv = jnp.concatenate([dq, dk, dv], axis=1)
            g_qkv[l] = _unpad_qkv(_mm(sv["h"], dqkv, name=f"{t}_qkv_dw", dims="tn", out_dtypes=(F32,)), heads)
            dh_src, dh_w = dqkv, wqkv[l]
        else:
            j = l - n_a
            do = _mm(dxb, wo_b[j], name=f"{t}_attn_out_dx", dims="nt")
            g_o_b[j] = _unpad_o(_mm(sv["o"], dxb, name=f"{t}_attn_out_dw", dims="tn", out_dtypes=(F32,)), heads)
            dq, dkv = _mla_bwd(sv["q"], kv, sv["o"], do, sv["lse"], cos_t, sin_t, dkv, heads, name=f"{t}_mla_bwd")
            g_uq[j] = _unpad_uq(_mm(sv["cq"], dq, name=f"{t}_q_up_dw", dims="tn", out_dtypes=(F32,)), heads)
            _, dcq0, g_qlat[j] = _mm_rms_bwd(dq, wuq[j], sv["cq0"], norms["mla_q_lat_norm"][j], None,
                                             name=f"{t}_q_up_dx")
            g_dq[j] = _mm(sv["h"], dcq0, name=f"{t}_q_down_dw", dims="tn", out_dtypes=(F32,))
            dh_src, dh_w = dcq0, wdq[j]
        if l == 0:
            dh = _mm(dh_src, dh_w, name=f"{t}_attn_in_dx", dims="nt", out_dtypes=(F32,))
            dx, dxb, g_attn_norm[l] = _rms_bwd(sv["x_in"], norms["attn_norm"][l], dh, dx, name=f"{t}_attn_norm_bwd",
                                               lead_axis=True)
        else:
            dx, dxb, g_attn_norm[l] = _mm_rms_bwd(dh_src, dh_w, sv["x_in"], norms["attn_norm"][l], dx,
                                                  name=f"{t}_attn_in_dx")
        if l == n_a:
            ks = kv_saved
            dcat = _mm(dkv, wkv, name="kv_up_dx", dims="nt", out_dtypes=(F32,))
            g_ukv = _unpad_ukv(_mm(ks["cat"], dkv, name="kv_up_dw", dims="tn", out_dtypes=(F32,)), heads)
            ddown, g_kv_lat = _kv_prep_bwd(ks["down"], norms["mla_kv_lat_norm"], cos_t, sin_t, dcat, name="kv_prep_bwd")
            g_dkv = _unpad_dkv(_mm(ks["hk"], ddown, name="kv_down_dw", dims="tn", out_dtypes=(F32,)))
            dx, dxb, g_kv_norm = _mm_rms_bwd(ddown, wdkv, ks["x_in"], norms["kv_norm"], dx, name="kv_down_dx")

    grads = {
        "attn_norm": jnp.concatenate(g_attn_norm, axis=0), "mlp_norm": jnp.concatenate(g_mlp_norm, axis=0),
        "sb_w_qkv": g_qkv, "sb_w_o": g_o_a,
        "kv_norm": g_kv_norm[0], "mla_w_dkv": g_dkv, "mla_kv_lat_norm": g_kv_lat[0], "mla_w_ukv": g_ukv,
        "mla_w_dq": g_dq, "mla_q_lat_norm": jnp.concatenate(g_qlat, axis=0),
        "mla_w_uq": g_uq, "mla_w_o": g_o_b,
        "mlp_w1": g_w1, "mlp_w2": g_w2, "final_norm": dg_final[0],
    }
    return loss, dx, grads, early_parts, early_got


def _flat_rows(n_elems):
    per_block = FLAT_COLS * FLAT_ROW_BLOCK * 2
    return -(-n_elems // per_block) * FLAT_ROW_BLOCK * 2


def _row_blocks(arrays, dtype):
    for a in arrays:
        assert a.size % FLAT_COLS == 0, a.shape
    blocks = [a.astype(dtype).reshape(-1, FLAT_COLS) for a in arrays]
    used = sum(b.shape[0] for b in blocks)
    rows = _flat_rows(used * FLAT_COLS)
    return blocks + [jnp.zeros((rows - used, FLAT_COLS), dtype)], rows


def _pack(arrays, dtype):
    blocks, _ = _row_blocks(arrays, dtype)
    return jnp.concatenate(blocks, axis=0)


def _pack_chips(per_chip, dtype):
    blocks, rows = [], 0
    for arrays in per_chip:
        chip_blocks, rows = _row_blocks(arrays, dtype)
        blocks += chip_blocks
    return jnp.concatenate(blocks, axis=0).reshape(len(per_chip), rows, FLAT_COLS)


def _unpack(flat, shapes):
    out, row = [], 0
    for shp in shapes:
        n = 1
        for v in shp:
            n *= v
        out.append(flat[row:row + n // FLAT_COLS].reshape(shp))
        row += n // FLAT_COLS
    return out


def _pack_small(arrays):
    rows = []
    for a in arrays:
        a = a.reshape(-1, a.shape[-1]) if a.shape[-1] == FLAT_COLS else a.reshape(1, -1)
        rows.append(_pad_last(a, FLAT_COLS))
    flat = jnp.concatenate(rows, axis=0)
    return jnp.pad(flat, [(0, -flat.shape[0] % 8), (0, 0)])


def _unpack_small(flat, shapes):
    out, row = [], 0
    for shp in shapes:
        if shp[-1] == FLAT_COLS:
            n = 1
            for v in shp[:-1]:
                n *= v
            out.append(flat[row:row + n].reshape(shp))
            row += n
        else:
            n = 1
            for v in shp:
                n *= v
            out.append(flat[row, :n].reshape(shp))
            row += 1
    return out


def _other_chips(x, y):
    return [(1 - x, y), (x, 1 - y), (1 - x, 1 - y)]


def _all_gather_chips(flat, *, name):
    rows, cols = flat.shape

    def body(x_ref, out_ref, *sems):
        _all_gather_start(x_ref, out_ref, sems)
        _all_gather_finish(x_ref, out_ref, sems)

    return pl.pallas_call(
        body, name=name,
        in_specs=[pl.BlockSpec(memory_space=pltpu.HBM)],
        out_specs=pl.BlockSpec(memory_space=pltpu.HBM),
        out_shape=jax.ShapeDtypeStruct((N_CHIPS, rows, cols), flat.dtype),
        scratch_shapes=_all_gather_sems(),
        compiler_params=pltpu.CompilerParams(has_side_effects=True),
    )(flat)


def _all_gather_sems():
    return [pltpu.SemaphoreType.DMA((3,)), pltpu.SemaphoreType.DMA((3,)), pltpu.SemaphoreType.DMA((3,)),
            pltpu.SemaphoreType.DMA((3,)), pltpu.SemaphoreType.DMA, pltpu.SemaphoreType.DMA]


def _all_gather_copies(x_ref, out_ref, sems, finishing):
    send_sems, recv_sems, pass_send_sems, pass_recv_sems, own_send_sem, own_recv_sem = sems
    x, y, c = lax.axis_index("x"), lax.axis_index("y"), lax.axis_index("c")
    me = 2 * x + y
    my_rows, sib_rows = _half_rows(x_ref.shape[0])

    def copy(src, dst, send_sem, recv_sem, to):
        return pltpu.make_async_remote_copy(src_ref=src, dst_ref=dst, send_sem=send_sem, recv_sem=recv_sem,
                                            device_id=to, device_id_type=MESH)

    own = copy(x_ref, out_ref.at[me], own_send_sem, own_recv_sem, _sibling())
    to_chips, landed, pass_on, passed = [], [], [], []
    for k, (px, py) in enumerate(_other_chips(x, y)):
        to_chips.append(copy(x_ref.at[my_rows, :], out_ref.at[me, my_rows, :], send_sems.at[k], recv_sems.at[k],
                             (px, py, c)))
        if finishing:
            mine, theirs = out_ref.at[2 * px + py, my_rows, :], out_ref.at[2 * px + py, sib_rows, :]
            landed.append(copy(mine, mine, send_sems.at[k], recv_sems.at[k], (px, py, c)))
            pass_on.append(copy(mine, mine, pass_send_sems.at[k], pass_recv_sems.at[k], _sibling()))
            passed.append(copy(theirs, theirs, pass_send_sems.at[k], pass_recv_sems.at[k], _sibling()))
    return own, to_chips, landed, pass_on, passed


def _all_gather_start(x_ref, out_ref, sems):
    own, to_chips, _, _, _ = _all_gather_copies(x_ref, out_ref, sems, finishing=False)
    own.start()
    for cp in to_chips:
        cp.start()


def _all_gather_finish(x_ref, out_ref, sems):
    own, to_chips, landed, pass_on, passed = _all_gather_copies(x_ref, out_ref, sems, finishing=True)
    for k in range(len(landed)):
        landed[k].wait_recv()
        pass_on[k].start()
    for cp in passed:
        cp.wait_recv()
    own.wait_recv()
    for cp in [own] + to_chips + pass_on:
        cp.wait_send()


def _exchange_chips(parts, *, name):
    def body(g_ref, out_ref, *sems):
        _exchange_start(g_ref, out_ref, sems)
        _exchange_finish(g_ref, out_ref, sems)

    return pl.pallas_call(
        body, name=name,
        in_specs=[pl.BlockSpec(memory_space=pltpu.HBM)],
        out_specs=pl.BlockSpec(memory_space=pltpu.HBM),
        out_shape=jax.ShapeDtypeStruct(parts.shape, parts.dtype),
        scratch_shapes=_exchange_sems(),
        compiler_params=pltpu.CompilerParams(has_side_effects=True),
    )(parts)


def _exchange_sems():
    return [pltpu.SemaphoreType.DMA((3,)), pltpu.SemaphoreType.DMA((3,))]


def _exchange_copies(g_ref, out_ref, sems, receiving):
    send_sems, recv_sems = sems
    x, y, c = lax.axis_index("x"), lax.axis_index("y"), lax.axis_index("c")
    me = 2 * x + y
    copies = []
    for k, (px, py) in enumerate(_other_chips(x, y)):
        src, dst = (g_ref.at[me], out_ref.at[2 * px + py]) if receiving else (g_ref.at[2 * px + py], out_ref.at[me])
        copies.append(pltpu.make_async_remote_copy(src_ref=src, dst_ref=dst, send_sem=send_sems.at[k],
                                                   recv_sem=recv_sems.at[k], device_id=(px, py, c),
                                                   device_id_type=MESH))
    return copies


def _exchange_start(g_ref, out_ref, sems):
    for cp in _exchange_copies(g_ref, out_ref, sems, receiving=False):
        cp.start()


def _exchange_finish(g_ref, out_ref, sems):
    for cp in _exchange_copies(g_ref, out_ref, sems, receiving=True):
        cp.wait_recv()
    for cp in _exchange_copies(g_ref, out_ref, sems, receiving=False):
        cp.wait_send()


def _my_chip():
    return 2 * lax.axis_index("x") + lax.axis_index("y")


def _half_rows(rows):
    c = lax.axis_index("c")
    half = rows // 2
    return pl.ds(pl.multiple_of(c * half, 8), half), pl.ds(pl.multiple_of((1 - c) * half, 8), half)


def _sibling():
    return (lax.axis_index("x"), lax.axis_index("y"), 1 - lax.axis_index("c"))


def _pair_exchange(parts, *, name):
    n, rows, cols = parts.shape

    def body(p_ref, theirs_ref, send_sem, recv_sem):
        _, sib_rows = _half_rows(rows)
        cp = pltpu.make_async_remote_copy(src_ref=p_ref.at[:, sib_rows, :], dst_ref=theirs_ref, send_sem=send_sem,
                                          recv_sem=recv_sem, device_id=_sibling(), device_id_type=MESH)
        cp.start()
        cp.wait()

    half = rows // 2
    theirs = pl.pallas_call(
        body, name=name,
        in_specs=[pl.BlockSpec(memory_space=pltpu.HBM)],
        out_specs=pl.BlockSpec(memory_space=pltpu.HBM),
        out_shape=jax.ShapeDtypeStruct((n, half, cols), parts.dtype),
        scratch_shapes=[pltpu.SemaphoreType.DMA, pltpu.SemaphoreType.DMA],
        compiler_params=pltpu.CompilerParams(has_side_effects=True),
    )(parts)
    mine = lax.dynamic_slice_in_dim(parts, lax.axis_index("c") * half, half, axis=1)
    return mine, theirs


def _pair_sum(mine, theirs, *, name):
    n, rows, cols = mine.shape

    def body(a_ref, b_ref, o_ref):
        o_ref[...] = (a_ref[...].astype(F32) + b_ref[...].astype(F32)).astype(o_ref.dtype)

    blk = pl.BlockSpec((n, FLAT_ROW_BLOCK, cols), lambda i: (0, i, 0))
    return pl.pallas_call(
        body, name=name, grid=(rows // FLAT_ROW_BLOCK,),
        in_specs=[blk, blk], out_specs=blk, out_shape=jax.ShapeDtypeStruct(mine.shape, mine.dtype),
        compiler_params=pltpu.CompilerParams(dimension_semantics=("parallel",), vmem_limit_bytes=VMEM_LIMIT),
    )(mine, theirs)


def _sum_chips(received, own, *, name):
    _, rows, cols = received.shape

    def body(p_ref, own_ref, o_ref):
        me = 2 * lax.axis_index("x") + lax.axis_index("y")
        slot = [jnp.where(me == j, own_ref[j], p_ref[j]).astype(F32) for j in range(N_CHIPS)]
        o_ref[...] = ((slot[0] + slot[1]) + slot[2]) + slot[3]

    blk = pl.BlockSpec((N_CHIPS, FLAT_ROW_BLOCK, cols), lambda i: (0, i, 0))
    return pl.pallas_call(
        body, name=name, grid=(rows // FLAT_ROW_BLOCK,),
        in_specs=[blk, blk],
        out_specs=pl.BlockSpec((FLAT_ROW_BLOCK, cols), lambda i: (i, 0)),
        out_shape=jax.ShapeDtypeStruct((rows, cols), F32),
        compiler_params=pltpu.CompilerParams(dimension_semantics=("parallel",), vmem_limit_bytes=VMEM_LIMIT),
    )(received, own)


def _join_cores(half, *, name):
    rows2, cols = half.shape

    def body(h_ref, out_ref, send_sem, recv_sem):
        my_rows, sib_rows = _half_rows(2 * rows2)
        cp = pltpu.make_async_remote_copy(src_ref=h_ref, dst_ref=out_ref.at[my_rows, :], send_sem=send_sem,
                                          recv_sem=recv_sem, device_id=_sibling(), device_id_type=MESH)
        cp.start()
        cp.wait_send()
        pltpu.make_async_remote_copy(src_ref=h_ref, dst_ref=out_ref.at[sib_rows, :], send_sem=send_sem,
                                     recv_sem=recv_sem, device_id=_sibling(), device_id_type=MESH).wait_recv()

    out = pl.pallas_call(
        body, name=name,
        in_specs=[pl.BlockSpec(memory_space=pltpu.HBM)],
        out_specs=pl.BlockSpec(memory_space=pltpu.HBM),
        out_shape=jax.ShapeDtypeStruct((2 * rows2, cols), half.dtype),
        scratch_shapes=[pltpu.SemaphoreType.DMA, pltpu.SemaphoreType.DMA],
        compiler_params=pltpu.CompilerParams(has_side_effects=True),
    )(half)
    return lax.dynamic_update_slice_in_dim(out, half, lax.axis_index("c") * rows2, axis=0)


def _all_reduce_small(v, *, name):
    rows, cols = v.shape
    flips = [(fx, fy, fc) for fx in (0, 1) for fy in (0, 1) for fc in (0, 1)][1:]

    def body(v_ref, out_ref, gath_ref, send_sems, recv_sems):
        x, y, c = lax.axis_index("x"), lax.axis_index("y"), lax.axis_index("c")
        me = 4 * x + 2 * y + c
        gath_ref[me] = v_ref[...]
        peers = [((1 - x) if fx else x, (1 - y) if fy else y, (1 - c) if fc else c) for fx, fy, fc in flips]
        sends = []
        for k, peer in enumerate(peers):
            cp = pltpu.make_async_remote_copy(src_ref=v_ref, dst_ref=gath_ref.at[me], send_sem=send_sems.at[k],
                                              recv_sem=recv_sems.at[k], device_id=peer, device_id_type=MESH)
            cp.start()
            sends.append(cp)
        for k, (px, py, pc) in enumerate(peers):
            pltpu.make_async_remote_copy(src_ref=v_ref, dst_ref=gath_ref.at[4 * px + 2 * py + pc],
                                         send_sem=send_sems.at[k], recv_sem=recv_sems.at[k],
                                         device_id=(px, py, pc), device_id_type=MESH).wait_recv()
        for cp in sends:
            cp.wait_send()
        total = gath_ref[0]
        for k in range(1, 8):
            total = total + gath_ref[k]
        out_ref[...] = total

    total, _ = pl.pallas_call(
        body, name=name,
        in_specs=[pl.BlockSpec(memory_space=pltpu.VMEM)],
        out_specs=[pl.BlockSpec(memory_space=pltpu.VMEM), pl.BlockSpec(memory_space=pltpu.VMEM)],
        out_shape=[jax.ShapeDtypeStruct((rows, cols), v.dtype), jax.ShapeDtypeStruct((8, rows, cols), v.dtype)],
        scratch_shapes=[pltpu.SemaphoreType.DMA((7,)), pltpu.SemaphoreType.DMA((7,))],
        compiler_params=pltpu.CompilerParams(has_side_effects=True),
    )(v)
    return total


def _adamw(w, g, m, v, *, name):
    shape = w.shape
    cols = shape[-1]
    w2, g2, m2, v2 = (a.reshape(-1, cols) for a in (w, g, m, v))
    rows = w2.shape[0]
    br = _pick_rows(rows, FLAT_ROW_BLOCK)

    def body(w_ref, g_ref, m_ref, v_ref, d_out, m_out, v_out):
        gv = g_ref[...]
        m_new = ADAM_B1 * m_ref[...] + (1.0 - ADAM_B1) * gv
        v_new = ADAM_B2 * v_ref[...] + (1.0 - ADAM_B2) * jnp.square(gv)
        m_hat = m_new / (1.0 - ADAM_B1 ** ADAM_STEP)
        v_hat = v_new / (1.0 - ADAM_B2 ** ADAM_STEP)
        d_out[...] = -ADAM_LR * (m_hat / (jnp.sqrt(v_hat) + ADAM_EPS) + ADAM_WD * w_ref[...])
        m_out[...] = m_new
        v_out[...] = v_new

    blk = pl.BlockSpec((br, cols), lambda i: (i, 0))
    out = jax.ShapeDtypeStruct((rows, cols), F32)
    outs = pl.pallas_call(
        body, name=name, grid=(rows // br,),
        in_specs=[blk] * 4, out_specs=[blk] * 3, out_shape=[out] * 3,
        compiler_params=pltpu.CompilerParams(dimension_semantics=("parallel",), vmem_limit_bytes=VMEM_LIMIT),
    )(w2, g2, m2, v2)
    return [o.reshape(shape) for o in outs]


def _pick_rows(rows, target):
    if rows <= target:
        return rows
    return max(b for b in range(8, target + 1, 8) if rows % b == 0)


def _assemble(gathered_shards, name, layer=False):
    return jnp.concatenate(gathered_shards, axis=SHARD_AXIS[name] - int(layer))


def _chip_shard(full, name, j):
    if isinstance(full, list):
        axis = SHARD_AXIS[name] - 1
        layers = full
    else:
        axis = SHARD_AXIS[name]
        layers = [full]
    n = layers[0].shape[axis] // N_CHIPS
    return [lax.slice_in_dim(g, j * n, (j + 1) * n, axis=axis) for g in layers]


def kernel(x, positions, attn_norm, mlp_norm, sb_w_qkv, sb_w_o, kv_norm, mla_w_dkv, mla_kv_lat_norm, mla_w_ukv, mla_w_dq, mla_q_lat_norm, mla_w_uq, mla_w_o, mlp_w1, mlp_w2, final_norm, loss_target, m_attn_norm, m_mlp_norm, m_sb_w_qkv, m_sb_w_o, m_kv_norm, m_mla_w_dkv, m_mla_kv_lat_norm, m_mla_w_ukv, m_mla_w_dq, m_mla_q_lat_norm, m_mla_w_uq, m_mla_w_o, m_mlp_w1, m_mlp_w2, m_final_norm, v_attn_norm, v_mlp_norm, v_sb_w_qkv, v_sb_w_o, v_kv_norm, v_mla_w_dkv, v_mla_kv_lat_norm, v_mla_w_ukv, v_mla_w_dq, v_mla_q_lat_norm, v_mla_w_uq, v_mla_w_o, v_mlp_w1, v_mlp_w2, v_final_norm):
    weights = dict(attn_norm=attn_norm, mlp_norm=mlp_norm, sb_w_qkv=sb_w_qkv, sb_w_o=sb_w_o, kv_norm=kv_norm,
                   mla_w_dkv=mla_w_dkv, mla_kv_lat_norm=mla_kv_lat_norm, mla_w_ukv=mla_w_ukv, mla_w_dq=mla_w_dq,
                   mla_q_lat_norm=mla_q_lat_norm, mla_w_uq=mla_w_uq, mla_w_o=mla_w_o, mlp_w1=mlp_w1, mlp_w2=mlp_w2,
                   final_norm=final_norm)
    m_in = dict(attn_norm=m_attn_norm, mlp_norm=m_mlp_norm, sb_w_qkv=m_sb_w_qkv, sb_w_o=m_sb_w_o, kv_norm=m_kv_norm,
                mla_w_dkv=m_mla_w_dkv, mla_kv_lat_norm=m_mla_kv_lat_norm, mla_w_ukv=m_mla_w_ukv, mla_w_dq=m_mla_w_dq,
                mla_q_lat_norm=m_mla_q_lat_norm, mla_w_uq=m_mla_w_uq, mla_w_o=m_mla_w_o, mlp_w1=m_mlp_w1,
                mlp_w2=m_mlp_w2, final_norm=m_final_norm)
    v_in = dict(attn_norm=v_attn_norm, mlp_norm=v_mlp_norm, sb_w_qkv=v_sb_w_qkv, sb_w_o=v_sb_w_o, kv_norm=v_kv_norm,
                mla_w_dkv=v_mla_w_dkv, mla_kv_lat_norm=v_mla_kv_lat_norm, mla_w_ukv=v_mla_w_ukv, mla_w_dq=v_mla_w_dq,
                mla_q_lat_norm=v_mla_q_lat_norm, mla_w_uq=v_mla_w_uq, mla_w_o=v_mla_w_o, mlp_w1=v_mlp_w1,
                mlp_w2=v_mlp_w2, final_norm=v_final_norm)
    shard_shapes = [weights[n].shape for n in BIG_WEIGHTS]
    small_shapes = [weights[n].shape for n in SMALL_WEIGHTS]

    first_name = BIG_WEIGHTS[0]
    qkv_first, qkv_later = weights[first_name][0], weights[first_name][1:]
    gathered_first = _all_gather_chips(_pack([qkv_first], BF16), name="first_weight_all_gather")
    qkv_w0 = _assemble([_unpack(gathered_first[j], [qkv_first.shape])[0] for j in range(N_CHIPS)], first_name, layer=True)
    ride = _pack([qkv_later] + [weights[n] for n in BIG_WEIGHTS[1:]], BF16)
    ride_shapes = [qkv_later.shape] + shard_shapes[1:]

    def rest_weights(gathered):
        per_chip = [_unpack(gathered[j], ride_shapes) for j in range(N_CHIPS)]
        full = {n: _assemble([per_chip[j][i] for j in range(N_CHIPS)], n) for i, n in enumerate(BIG_WEIGHTS)}
        full[first_name] = jnp.concatenate([qkv_w0[None], full[first_name]], axis=0)
        return full

    norms = {n: weights[n] for n in SMALL_WEIGHTS}

    def chip_parts(g, tag):
        parts = _pack_chips([[piece for n in BIG_WEIGHTS if n in g for piece in _chip_shard(g[n], n, j)]
                             for j in range(N_CHIPS)], BF16)
        mine, theirs = _pair_exchange(parts, name=f"grads_pair_exchange_{tag}")
        return _pair_sum(mine, theirs, name=f"grads_pair_sum_{tag}")

    def finish(received, chip_part, tag):
        g_half = _sum_chips(received, chip_part, name=f"grads_sum_chips_{tag}")
        return _join_cores(g_half, name=f"grads_join_cores_{tag}")

    loss, dx, grads, early_parts, early_got = _local_step(
        x[0], positions[0], loss_target, qkv_w0, norms, rest_weights, ride=ride,
        early_reduce=functools.partial(chip_parts, tag="early"))
    loss = lax.psum(loss, ("x", "y", "c"))
    early_sum = finish(early_got, early_parts, "early")
    last_parts = chip_parts({first_name: grads[first_name][:1]}, "last")
    last_sum = finish(_exchange_chips(last_parts, name="grads_exchange_last"), last_parts, "last")

    out_g = dict(zip(BIG_WEIGHTS, _unpack(early_sum, ride_shapes)))
    out_g[first_name] = jnp.concatenate([_unpack(last_sum, [qkv_first.shape])[0][None], out_g[first_name]], axis=0)
    out_d, out_m, out_v = {}, {}, {}
    for n in BIG_WEIGHTS:
        out_d[n], out_m[n], out_v[n] = _adamw(weights[n], out_g[n], m_in[n], v_in[n], name=f"adamw_{n}")

    small_sum = _all_reduce_small(_pack_small([grads[n] for n in SMALL_WEIGHTS]), name="gains_all_reduce")
    sd, sm, sv = _adamw(_pack_small([weights[n] for n in SMALL_WEIGHTS]), small_sum,
                        _pack_small([m_in[n] for n in SMALL_WEIGHTS]),
                        _pack_small([v_in[n] for n in SMALL_WEIGHTS]), name="adamw_gains")
    out_g.update(zip(SMALL_WEIGHTS, _unpack_small(small_sum, small_shapes)))
    out_d.update(zip(SMALL_WEIGHTS, _unpack_small(sd, small_shapes)))
    out_m.update(zip(SMALL_WEIGHTS, _unpack_small(sm, small_shapes)))
    out_v.update(zip(SMALL_WEIGHTS, _unpack_small(sv, small_shapes)))

    return (loss, dx, *[out_g[n] for n in ALL_WEIGHTS], *[out_d[n] for n in ALL_WEIGHTS],
            *[out_m[n] for n in ALL_WEIGHTS], *[out_v[n] for n in ALL_WEIGHTS])
```

```python
import functools

import jax
import jax.numpy as jnp
from jax import lax
from jax.experimental import pallas as pl
from jax.experimental.pallas import tpu as pltpu

F32 = jnp.float32
BF16 = jnp.bfloat16

LANES = 128
SB_HEAD_DIM = 64
MLA_NOPE = 64
MLA_ROPE = 32
MLA_V = 64
MLA_Q_RANK = 384
MLA_KV_RANK = 256
CHUNK = 64
ROPE_THETA = 10000.0
NORM_EPS = 1e-6
SB_SCALE = SB_HEAD_DIM ** -0.5
MLA_SCALE = (MLA_NOPE + MLA_ROPE) ** -0.5
ROPE_LO = MLA_NOPE
ROPE_HALF = MLA_ROPE // 2
ATT_Q_BLOCK = 1024
ATT_K_BLOCK = 256
MLA_FWD_K_BLOCK = 512
NEG_BIG = -1e30
SB_DEAD_LOG = -110.0
VMEM_LIMIT = 56 * 1024 * 1024

ADAM_LR = 0.001
ADAM_B1 = 0.9
ADAM_B2 = 0.999
ADAM_EPS = 1e-08
ADAM_WD = 0.01
ADAM_STEP = 10

FLAT_COLS = 1024
FLAT_ROW_BLOCK = 256
N_CHIPS = 4
MESH = pl.DeviceIdType.MESH

BIG_WEIGHTS = ["sb_w_qkv", "sb_w_o", "mla_w_dkv", "mla_w_ukv", "mla_w_dq", "mla_w_uq", "mla_w_o", "mlp_w1", "mlp_w2"]
SHARD_AXIS = {"sb_w_qkv": 2, "sb_w_o": 1, "mla_w_dkv": 0, "mla_w_ukv": 1, "mla_w_dq": 1, "mla_w_uq": 2,
              "mla_w_o": 1, "mlp_w1": 2, "mlp_w2": 1}
SMALL_WEIGHTS = ["attn_norm", "mlp_norm", "kv_norm", "mla_kv_lat_norm", "mla_q_lat_norm", "final_norm"]
ALL_WEIGHTS = ["attn_norm", "mlp_norm", "sb_w_qkv", "sb_w_o", "kv_norm", "mla_w_dkv", "mla_kv_lat_norm", "mla_w_ukv",
               "mla_w_dq", "mla_q_lat_norm", "mla_w_uq", "mla_w_o", "mlp_w1", "mlp_w2", "final_norm"]


def _dot(a, b, dims):
    return lax.dot_general(a, b, (dims, ((), ())), preferred_element_type=F32)


def _dot_nn(a, b):
    return _dot(a, b, ((1,), (0,)))


def _dot_nt(a, b):
    return _dot(a, b, ((1,), (1,)))


def _dot_tn(a, b):
    return _dot(a, b, ((0,), (0,)))


def _pick_block(n, target):
    if n <= target:
        return n
    best = max(b for b in range(LANES, target + 1, LANES) if n % b == 0)
    return best


MM_ROWS = 512
MM_COLS = 1024
MM_DEPTH = 4096
MM_DEPTH_TN = 2048


def _mm(a, b, *, name, dims="nn", epilogue=None, extras=(), out_dtypes=(BF16,), column_sum=False):
    if dims == "nn":
        (m, k), (k2, n) = a.shape, b.shape
    elif dims == "nt":
        (m, k), (n, k2) = a.shape, b.shape
    else:
        (k, m), (k2, n) = a.shape, b.shape
    assert k == k2, (name, a.shape, b.shape)
    if dims == "tn":
        bm, bn, bk = _pick_block(m, MM_COLS), _pick_block(n, MM_COLS), _pick_block(k, MM_DEPTH_TN)
    else:
        rows = MM_ROWS if k > MM_DEPTH // 2 else 2 * MM_ROWS
        cols = 2 * MM_COLS if (k <= MM_COLS and n >= 4 * MM_COLS) else MM_COLS
        bm, bn, bk = _pick_block(m, rows), _pick_block(n, cols), _pick_block(k, MM_DEPTH)
    nk = k // bk
    if dims == "tn":
        a_spec = pl.BlockSpec((bk, bm), lambda j, i, kk: (kk, i))
    else:
        a_spec = pl.BlockSpec((bm, bk), lambda j, i, kk: (i, kk))
    if dims == "nt":
        b_spec = pl.BlockSpec((bn, bk), lambda j, i, kk: (j, kk))
    else:
        b_spec = pl.BlockSpec((bk, bn), lambda j, i, kk: (kk, j))
    extra_specs = []
    for arr, kind in extras:
        if kind == "tile":
            assert arr.shape == (m, n), (name, arr.shape)
            extra_specs.append(pl.BlockSpec((bm, bn), lambda j, i, kk: (i, j)))
        elif kind == "vec":
            assert arr.shape == (1, n), (name, arr.shape)
            extra_specs.append(pl.BlockSpec((1, bn), lambda j, i, kk: (0, j)))
        else:
            assert arr.shape == (m, LANES), (name, arr.shape)
            extra_specs.append(pl.BlockSpec((bm, LANES), lambda j, i, kk: (i, 0)))
    n_extra = len(extras)
    n_out = len(out_dtypes)
    n_sum = int(column_sum)
    dot = {"nn": _dot_nn, "nt": _dot_nt, "tn": _dot_tn}[dims]

    def body(*refs):
        a_ref, b_ref = refs[0], refs[1]
        extra_refs = refs[2:2 + n_extra]
        out_refs = refs[2 + n_extra:2 + n_extra + n_out]

        def finish(acc):
            outs = (acc,) if epilogue is None else epilogue(acc, *[r[...] for r in extra_refs])
            for o_ref, o in zip(out_refs, outs):
                o_ref[...] = o.astype(o_ref.dtype)
            if column_sum:
                sum_ref = refs[2 + n_extra + n_out]
                first_rows = pl.program_id(1) == 0

                @pl.when(first_rows)
                def _():
                    sum_ref[...] = outs[n_out]

                @pl.when(jnp.logical_not(first_rows))
                def _():
                    sum_ref[...] += outs[n_out]

        part = dot(a_ref[...].astype(BF16), b_ref[...].astype(BF16))
        if nk == 1:
            finish(part)
            return
        acc_ref = refs[-1]
        kk = pl.program_id(2)

        @pl.when(kk == 0)
        def _():
            acc_ref[...] = part

        @pl.when(kk > 0)
        def _():
            acc_ref[...] += part

        @pl.when(kk == nk - 1)
        def _():
            finish(acc_ref[...])

    outs = pl.pallas_call(
        body, name=name, grid=(n // bn, m // bm, nk),
        in_specs=[a_spec, b_spec] + extra_specs,
        out_specs=[pl.BlockSpec((bm, bn), lambda j, i, kk: (i, j)) for _ in range(n_out)]
        + [pl.BlockSpec((1, bn), lambda j, i, kk: (0, j))] * n_sum,
        out_shape=[jax.ShapeDtypeStruct((m, n), dt) for dt in out_dtypes] + [jax.ShapeDtypeStruct((1, n), F32)] * n_sum,
        scratch_shapes=[pltpu.VMEM((bm, bn), F32)] if nk > 1 else [],
        compiler_params=pltpu.CompilerParams(
            dimension_semantics=("parallel", "arbitrary" if column_sum else "parallel", "arbitrary"),
            vmem_limit_bytes=VMEM_LIMIT),
    )(a, b, *[arr for arr, _ in extras])
    return outs[0] if n_out + n_sum == 1 else outs


def _epi_add(acc, res):
    return (res + acc,)


def _epi_relu2(acc):
    r = jnp.maximum(acc, 0.0)
    return acc, r * r


def _epi_relu2_grad(acc, u):
    return (acc * (2.0 * jnp.maximum(u.astype(F32), 0.0)),)


def _rope_slab(t, cos_t, sin_t):
    lane = lax.broadcasted_iota(jnp.int32, t.shape, 1)
    partner = jnp.where(lane < ROPE_LO + ROPE_HALF, pltpu.roll(t, LANES - ROPE_HALF, 1), pltpu.roll(t, ROPE_HALF, 1))
    return t * cos_t + partner * sin_t


def _rope_slab_bwd(d, cos_t, sin_t):
    ds = d * sin_t
    lane = lax.broadcasted_iota(jnp.int32, d.shape, 1)
    partner = jnp.where(lane < ROPE_LO + ROPE_HALF, pltpu.roll(ds, LANES - ROPE_HALF, 1), pltpu.roll(ds, ROPE_HALF, 1))
    in_rope = (lane >= ROPE_LO) & (lane < ROPE_LO + MLA_ROPE)
    return d * cos_t + jnp.where(in_rope, partner, 0.0)


def _epi_rope_heads(acc, cos_t, sin_t):
    slabs = [_rope_slab(acc[:, j * LANES:(j + 1) * LANES], cos_t, sin_t) for j in range(acc.shape[1] // LANES)]
    return (jnp.concatenate(slabs, axis=1) * MLA_SCALE,)


def _row_block(s):
    return min(512, s)


def _rms_fwd(x, g, *, name):
    s, d = x.shape
    bm = _row_block(s)

    def body(x_ref, g_ref, o_ref):
        xv = x_ref[...]
        r = lax.rsqrt(jnp.mean(xv * xv, axis=-1, keepdims=True) + NORM_EPS)
        o_ref[...] = ((xv * r) * g_ref[...]).astype(o_ref.dtype)

    return pl.pallas_call(
        body, name=name, grid=(s // bm,),
        in_specs=[pl.BlockSpec((bm, d), lambda i: (i, 0)), pl.BlockSpec((1, d), lambda i: (0, 0))],
        out_specs=pl.BlockSpec((bm, d), lambda i: (i, 0)),
        out_shape=jax.ShapeDtypeStruct((s, d), BF16),
        compiler_params=pltpu.CompilerParams(dimension_semantics=("parallel",), vmem_limit_bytes=VMEM_LIMIT),
    )(x, g.reshape(1, d))


def _rms_bwd_math(xv, gv, dy):
    r = lax.rsqrt(jnp.mean(xv * xv, axis=-1, keepdims=True) + NORM_EPS)
    xhat = xv * r
    dyg = dy * gv
    mdot = jnp.mean(dyg * xhat, axis=-1, keepdims=True)
    dx = r * (dyg - xhat * mdot)
    dg = jnp.sum(dy * xhat, axis=0, keepdims=True)
    return dx, dg


def _rms_bwd(x, g, dy, dres, *, name, lead_axis=False):
    s, d = x.shape
    bm = _row_block(s)
    has_res = dres is not None

    def body(*refs):
        x_ref, g_ref, dy_ref = refs[:3]
        dres_ref = refs[3] if has_res else None
        dx_ref, dxb_ref, dg_ref = refs[-3:]
        dx, dg = _rms_bwd_math(x_ref[...], g_ref[...], dy_ref[...].astype(F32))
        if has_res:
            dx = dx + dres_ref[...]
        dx_ref[...] = dx
        dxb_ref[...] = dx.astype(BF16)

        @pl.when(pl.program_id(0) == 0)
        def _():
            dg_ref[...] = jnp.zeros_like(dg_ref)

        dg_ref[...] += dg

    row = pl.BlockSpec((bm, d), lambda i: (i, 0))
    vec = pl.BlockSpec((1, d), lambda i: (0, 0))
    ins = [x, g.reshape(1, d), dy] + ([dres] if has_res else [])
    dx_spec, dx_shape = row, (s, d)
    if lead_axis:
        dx_spec, dx_shape = pl.BlockSpec((None, bm, d), lambda i: (0, i, 0)), (1, s, d)
    return pl.pallas_call(
        body, name=name, grid=(s // bm,),
        in_specs=[row, vec, row] + ([row] if has_res else []),
        out_specs=[dx_spec, row, vec],
        out_shape=[jax.ShapeDtypeStruct(dx_shape, F32), jax.ShapeDtypeStruct((s, d), BF16),
                   jax.ShapeDtypeStruct((1, d), F32)],
        compiler_params=pltpu.CompilerParams(dimension_semantics=("arbitrary",), vmem_limit_bytes=VMEM_LIMIT),
    )(*ins)


def _epi_rms_bwd(acc, x, g, dres=None):
    dx, dg = _rms_bwd_math(x, g, acc)
    if dres is not None:
        dx = dx + dres
    return dx, dx, dg


def _mm_rms_bwd(dy_src, w, x, g, dres, *, name):
    d = x.shape[1]
    assert w.shape[0] == d and d <= MM_COLS, (name, w.shape, x.shape)
    extras = [(x, "tile"), (g.reshape(1, d), "vec")] + ([(dres, "tile")] if dres is not None else [])
    return _mm(dy_src, w, name=name, dims="nt", epilogue=_epi_rms_bwd, extras=extras, out_dtypes=(F32, BF16),
               column_sum=True)


def _loss_bwd(x, g, target, *, name):
    s, d = x.shape
    bm = _row_block(s)

    def body(x_ref, g_ref, t_ref, loss_ref, dx_ref, dxb_ref, dg_ref):
        xv, gv = x_ref[...], g_ref[...]
        r = lax.rsqrt(jnp.mean(xv * xv, axis=-1, keepdims=True) + NORM_EPS)
        err = (xv * r) * gv - t_ref[...]
        dx, dg = _rms_bwd_math(xv, gv, err * (1.0 / d))
        dx_ref[...] = dx
        dxb_ref[...] = dx.astype(BF16)

        @pl.when(pl.program_id(0) == 0)
        def _():
            dg_ref[...] = jnp.zeros_like(dg_ref)
            loss_ref[...] = jnp.zeros_like(loss_ref)

        dg_ref[...] += dg
        loss_ref[...] += jnp.sum(jnp.mean(err * err, axis=-1, keepdims=True), axis=0, keepdims=True) * 0.5

    row = pl.BlockSpec((bm, d), lambda i: (i, 0))
    vec = pl.BlockSpec((1, d), lambda i: (0, 0))
    assert target.shape == (1, s, d), target.shape
    return pl.pallas_call(
        body, name=name, grid=(s // bm,),
        in_specs=[row, vec, pl.BlockSpec((None, bm, d), lambda i: (0, i, 0))],
        out_specs=[pl.BlockSpec((8, LANES), lambda i: (0, 0)), row, row, vec],
        out_shape=[jax.ShapeDtypeStruct((8, LANES), F32), jax.ShapeDtypeStruct((s, d), F32),
                   jax.ShapeDtypeStruct((s, d), BF16), jax.ShapeDtypeStruct((1, d), F32)],
        compiler_params=pltpu.CompilerParams(dimension_semantics=("arbitrary",), vmem_limit_bytes=VMEM_LIMIT),
    )(x, g.reshape(1, d), target)


def _kv_prep(down, g, cos_t, sin_t, *, name):
    s, w = down.shape
    bm = _row_block(s)

    def body(d_ref, g_ref, c_ref, s_ref, o_ref):
        lat = d_ref[:, :MLA_KV_RANK]
        r = lax.rsqrt(jnp.mean(lat * lat, axis=-1, keepdims=True) + NORM_EPS)
        o_ref[:, :MLA_KV_RANK] = ((lat * r) * g_ref[...]).astype(BF16)
        o_ref[:, MLA_KV_RANK:] = _rope_slab(d_ref[:, MLA_KV_RANK:], c_ref[...], s_ref[...]).astype(BF16)

    row = pl.BlockSpec((bm, w), lambda i: (i, 0))
    tab = pl.BlockSpec((bm, LANES), lambda i: (i, 0))
    return pl.pallas_call(
        body, name=name, grid=(s // bm,),
        in_specs=[row, pl.BlockSpec((1, MLA_KV_RANK), lambda i: (0, 0)), tab, tab],
        out_specs=row, out_shape=jax.ShapeDtypeStruct((s, w), BF16),
        compiler_params=pltpu.CompilerParams(dimension_semantics=("parallel",), vmem_limit_bytes=VMEM_LIMIT),
    )(down, g.reshape(1, MLA_KV_RANK), cos_t, sin_t)


def _kv_prep_bwd(down, g, cos_t, sin_t, dcat, *, name):
    s, w = down.shape
    bm = _row_block(s)

    def body(d_ref, g_ref, c_ref, s_ref, dc_ref, o_ref, dg_ref):
        dlat, dg = _rms_bwd_math(d_ref[:, :MLA_KV_RANK], g_ref[...], dc_ref[:, :MLA_KV_RANK])
        o_ref[:, :MLA_KV_RANK] = dlat.astype(BF16)
        o_ref[:, MLA_KV_RANK:] = _rope_slab_bwd(dc_ref[:, MLA_KV_RANK:], c_ref[...], s_ref[...]).astype(BF16)

        @pl.when(pl.program_id(0) == 0)
        def _():
            dg_ref[...] = jnp.zeros_like(dg_ref)

        dg_ref[...] += dg

    row = pl.BlockSpec((bm, w), lambda i: (i, 0))
    tab = pl.BlockSpec((bm, LANES), lambda i: (i, 0))
    vec = pl.BlockSpec((1, MLA_KV_RANK), lambda i: (0, 0))
    return pl.pallas_call(
        body, name=name, grid=(s // bm,),
        in_specs=[row, vec, tab, tab, row],
        out_specs=[row, vec],
        out_shape=[jax.ShapeDtypeStruct((s, w), BF16), jax.ShapeDtypeStruct((1, MLA_KV_RANK), F32)],
        compiler_params=pltpu.CompilerParams(dimension_semantics=("arbitrary",), vmem_limit_bytes=VMEM_LIMIT),
    )(down, g.reshape(1, MLA_KV_RANK), cos_t, sin_t, dcat)


def _split_bf16(v):
    hi = v.astype(BF16)
    lo = (v - hi.astype(F32)).astype(BF16)
    return hi, lo


def _suffix_matrices(n):
    row = lax.broadcasted_iota(jnp.int32, (n, n), 0)
    col = lax.broadcasted_iota(jnp.int32, (n, n), 1)
    incl = (row >= col).astype(BF16)
    return (row > col).astype(BF16), jnp.concatenate([incl, incl], axis=0)


def _suffix_sum(v, matrix):
    hi, lo = _split_bf16(v)
    return _dot_nn(jnp.concatenate([hi, lo], axis=1), matrix)


def _block_positions(qi, kb, bq, bk, r0, r1):
    row = qi * bq + r0 + lax.broadcasted_iota(jnp.int32, (r1 - r0, bk), 0)
    col = kb * bk + lax.broadcasted_iota(jnp.int32, (r1 - r0, bk), 1)
    return row, col


def _att_blocks(s, key_block=ATT_K_BLOCK):
    bq, bk = min(ATT_Q_BLOCK, s), min(key_block, s)
    return bq, bk, s // bq, bq // bk


def _sweep(qi, ratio, bk, step, unroll=2, alive=None):
    bq = ratio * bk
    for d in range(ratio):
        kb, r0 = (qi + 1) * ratio - 1 - d, (ratio - 1 - d) * bk
        near = bq if alive is None else min(r0 + 2 * bk, bq)
        step(kb, True, r0, near)
        if near < bq:
            pl.when(alive(near))(functools.partial(step, kb, False, near, bq))
    unroll = unroll if ratio % unroll == 0 else 1
    trips = qi * (ratio // unroll)

    def trip(i):
        for u in range(unroll):
            kb = qi * ratio - 1 - (i * unroll + u)
            if alive is None or ratio == 1:
                step(kb, False, 0, bq)
            else:
                step(kb, False, 0, bk)
                pl.when(alive(bk))(functools.partial(step, kb, False, bk, bq))

    if alive is None:
        lax.fori_loop(0, trips, lambda i, carry: (trip(i), carry)[1], 0)
    else:
        lax.while_loop(lambda i: jnp.logical_and(i < trips, alive(0)), lambda i: (trip(i), i + 1)[1], 0)


def _stick_left(c_ref, r0):
    return jnp.max(c_ref[r0:, :]) > SB_DEAD_LOG


def _sb_logs(q, k):
    z = _dot_nt(q, k)
    lb = jnp.minimum(z, 0.0) - jnp.log(1.0 + jnp.exp(-jnp.abs(z)))
    return lb, lb - z


def _sb_fwd(qkv, heads, *, name, ride=None):
    s = qkv.shape[0]
    bq, bk, nq, ratio = _att_blocks(s)
    riding = ride is not None

    def body(*refs):
        if riding:
            q_ref, k_ref, v_ref, w_ref, o_ref, gath_ref, acc_ref, c_ref = refs[:8]
            first = jnp.logical_and(pl.program_id(0) == 0, pl.program_id(1) == 0)
            last = jnp.logical_and(pl.program_id(0) == heads - 1, pl.program_id(1) == nq - 1)
            pl.when(first)(functools.partial(_all_gather_start, w_ref, gath_ref, refs[8:]))
        else:
            q_ref, k_ref, v_ref, o_ref, acc_ref, c_ref = refs
        qi = pl.program_id(1)
        q = q_ref[...] * SB_SCALE
        m_strict, _ = _suffix_matrices(bk)
        acc_ref[...] = jnp.zeros_like(acc_ref)
        c_ref[...] = jnp.zeros_like(c_ref)

        def step(kb, masked, r0, r1):
            rows = pl.ds(pl.multiple_of(kb * bk, bk), bk)
            mine = pl.ds(r0, r1 - r0)
            k, v = k_ref[rows, :], v_ref[rows, :]
            lb, lk = _sb_logs(q[r0:r1], k)
            if masked:
                row, col = _block_positions(qi, kb, bq, bk, r0, r1)
                causal = col < row
                lk = jnp.where(causal, lk, 0.0)
            c = c_ref[mine, :]
            w = jnp.exp(lb + _dot_nn(lk.astype(BF16), m_strict) + jnp.tile(c, (1, bk // LANES)))
            if masked:
                w = jnp.where(causal, w, 0.0)
            acc_ref[mine, :] += _dot_nn(w.astype(BF16), v)
            c_ref[mine, :] = c + jnp.sum(lk, axis=-1, keepdims=True)

        _sweep(qi, ratio, bk, step, unroll=1, alive=functools.partial(_stick_left, c_ref))
        o_ref[...] = acc_ref[...].astype(o_ref.dtype)
        if riding:
            pl.when(last)(functools.partial(_all_gather_finish, w_ref, gath_ref, refs[8:]))

    hbm = pl.BlockSpec(memory_space=pltpu.HBM)
    o_spec = pl.BlockSpec((bq, LANES), lambda h, i: (i, h))
    o_shape = jax.ShapeDtypeStruct((s, heads * LANES), F32)
    return pl.pallas_call(
        body, name=name, grid=(heads, nq),
        in_specs=[pl.BlockSpec((bq, LANES), lambda h, i: (i, h)),
                  pl.BlockSpec((s, LANES), lambda h, i: (0, heads + h)),
                  pl.BlockSpec((s, LANES), lambda h, i: (0, 2 * heads + h))] + ([hbm] if riding else []),
        out_specs=[o_spec, hbm] if riding else o_spec,
        out_shape=[o_shape, jax.ShapeDtypeStruct((N_CHIPS,) + ride.shape, ride.dtype)] if riding else o_shape,
        scratch_shapes=[pltpu.VMEM((bq, LANES), F32), pltpu.VMEM((bq, LANES), F32)]
        + (_all_gather_sems() if riding else []),
        compiler_params=pltpu.CompilerParams(dimension_semantics=("arbitrary", "arbitrary"),
                                             vmem_limit_bytes=VMEM_LIMIT, has_side_effects=riding),
    )(*([qkv, qkv, qkv] + ([ride] if riding else [])))


def _sb_bwd(qkv, o, do, heads, *, name, ride=None):
    s = qkv.shape[0]
    bq, bk, nq, ratio = _att_blocks(s)
    riding = ride is not None

    def body(*refs):
        if riding:
            (q_ref, k_ref, v_ref, o_ref, do_ref, g_ref, dq_ref, dk_ref, dv_ref, got_ref,
             dq_acc, dk_acc, dv_acc, c_ref, e_ref) = refs[:15]
            first = jnp.logical_and(pl.program_id(0) == 0, pl.program_id(1) == 0)
            last = jnp.logical_and(pl.program_id(0) == heads - 1, pl.program_id(1) == nq - 1)
            pl.when(first)(functools.partial(_exchange_start, g_ref, got_ref, refs[15:]))
        else:
            q_ref, k_ref, v_ref, o_ref, do_ref, dq_ref, dk_ref, dv_ref, dq_acc, dk_acc, dv_acc, c_ref, e_ref = refs
        qi = pl.program_id(1)

        @pl.when(qi == 0)
        def _():
            dk_acc[...] = jnp.zeros_like(dk_acc)
            dv_acc[...] = jnp.zeros_like(dv_acc)

        q = q_ref[...] * SB_SCALE
        do = do_ref[...]
        q_t, do_t = q.T, do.T
        total = jnp.sum(do.astype(F32) * o_ref[...].astype(F32), axis=-1, keepdims=True)
        m_strict, m_incl = _suffix_matrices(bk)
        dq_acc[...] = jnp.zeros_like(dq_acc)
        c_ref[...] = jnp.zeros_like(c_ref)
        e_ref[...] = jnp.broadcast_to(total, e_ref.shape)
        reps = (1, bk // LANES)

        def step(kb, masked, r0, r1):
            rows = pl.ds(pl.multiple_of(kb * bk, bk), bk)
            mine = pl.ds(r0, r1 - r0)
            k, v = k_ref[rows, :], v_ref[rows, :]
            qs, dos = q[r0:r1], do[r0:r1]
            lb, lk_all = _sb_logs(qs, k)
            lk = lk_all
            if masked:
                row, col = _block_positions(qi, kb, bq, bk, r0, r1)
                causal = col < row
                lk = jnp.where(causal, lk_all, 0.0)
            c = c_ref[mine, :]
            w = jnp.exp(lb + _dot_nn(lk.astype(BF16), m_strict) + jnp.tile(c, reps))
            if masked:
                w = jnp.where(causal, w, 0.0)
            wb = w.astype(BF16)
            g = wb.astype(F32) * _dot_nt(dos, v)
            e = e_ref[mine, :]
            g_left = jnp.tile(e, reps) - _suffix_sum(g, m_incl)
            da = g * jnp.exp(lk_all) - jnp.exp(lb) * g_left
            if masked:
                da = jnp.where(causal, da, 0.0)
            dab = da.astype(BF16)
            dq_acc[mine, :] += _dot_nn(dab, k)
            dk_acc[:, rows] += _dot_nn(q_t[:, r0:r1], dab)
            dv_acc[:, rows] += _dot_nn(do_t[:, r0:r1], wb)
            e_ref[mine, :] = e - jnp.sum(g, axis=-1, keepdims=True)
            c_ref[mine, :] = c + jnp.sum(lk, axis=-1, keepdims=True)

        _sweep(qi, ratio, bk, step, unroll=1, alive=functools.partial(_stick_left, c_ref))
        dq_ref[...] = (dq_acc[...] * SB_SCALE).astype(dq_ref.dtype)

        @pl.when(qi == nq - 1)
        def _():
            dk_ref[...] = dk_acc[...].T.astype(dk_ref.dtype)
            dv_ref[...] = dv_acc[...].T.astype(dv_ref.dtype)

        if riding:
            pl.when(last)(functools.partial(_exchange_finish, g_ref, got_ref, refs[15:]))

    blk = pl.BlockSpec((bq, LANES), lambda h, i: (i, h))
    full = pl.BlockSpec((s, LANES), lambda h, i: (0, h))
    hbm = pl.BlockSpec(memory_space=pltpu.HBM)
    shape = jax.ShapeDtypeStruct((s, heads * LANES), BF16)
    return pl.pallas_call(
        body, name=name, grid=(heads, nq),
        in_specs=[blk,
                  pl.BlockSpec((s, LANES), lambda h, i: (0, heads + h)),
                  pl.BlockSpec((s, LANES), lambda h, i: (0, 2 * heads + h)),
                  blk, blk] + ([hbm] if riding else []),
        out_specs=[blk, full, full] + ([hbm] if riding else []),
        out_shape=[shape, shape, shape] + ([jax.ShapeDtypeStruct(ride.shape, ride.dtype)] if riding else []),
        scratch_shapes=[pltpu.VMEM((bq, LANES), F32), pltpu.VMEM((LANES, s), F32), pltpu.VMEM((LANES, s), F32),
                        pltpu.VMEM((bq, LANES), F32), pltpu.VMEM((bq, LANES), F32)]
        + (_exchange_sems() if riding else []),
        compiler_params=pltpu.CompilerParams(dimension_semantics=("arbitrary", "arbitrary"),
                                             vmem_limit_bytes=VMEM_LIMIT, has_side_effects=riding),
    )(*([qkv, qkv, qkv, o, do] + ([ride] if riding else [])))


def _chunk_allowed(qi, kb, bq, bk, r0, r1):
    row, col = _block_positions(qi, kb, bq, bk, r0, r1)
    return (col // CHUNK) <= (row // CHUNK)


def _mla_fwd(q, kv, heads, *, name):
    s = q.shape[0]
    bq, bk, nq, ratio = _att_blocks(s, MLA_FWD_K_BLOCK)
    reps = (1, bk // LANES)

    def body(q_ref, k_ref, v_ref, o_ref, lse_ref, acc_ref, m_ref, l_ref):
        qi = pl.program_id(1)
        qv = q_ref[...]
        acc_ref[...] = jnp.zeros_like(acc_ref)
        m_ref[...] = jnp.full_like(m_ref, NEG_BIG)
        l_ref[...] = jnp.zeros_like(l_ref)

        def step(kb, masked, r0, r1):
            rows = pl.ds(pl.multiple_of(kb * bk, bk), bk)
            mine = pl.ds(r0, r1 - r0)
            k, v = k_ref[rows, :], v_ref[rows, :]
            sc = _dot_nt(qv[r0:r1], k)
            if masked:
                allowed = _chunk_allowed(qi, kb, bq, bk, r0, r1)
                sc = jnp.where(allowed, sc, NEG_BIG)
            m_old = m_ref[mine, :]
            m_new = jnp.maximum(m_old, jnp.max(sc, axis=-1, keepdims=True))
            p = jnp.exp(sc - jnp.tile(m_new, reps))
            alpha = jnp.exp(m_old - m_new)
            l_ref[mine, :] = alpha * l_ref[mine, :] + jnp.sum(p, axis=-1, keepdims=True)
            acc_ref[mine, :] = alpha * acc_ref[mine, :] + _dot_nn(p.astype(BF16), v)
            m_ref[mine, :] = m_new

        _sweep(qi, ratio, bk, step)
        o_ref[...] = (acc_ref[...] / l_ref[...]).astype(o_ref.dtype)
        lse_ref[...] = m_ref[...] + jnp.log(l_ref[...])

    blk = pl.BlockSpec((bq, LANES), lambda h, i: (i, h))
    return pl.pallas_call(
        body, name=name, grid=(heads, nq),
        in_specs=[blk,
                  pl.BlockSpec((s, LANES), lambda h, i: (0, h)),
                  pl.BlockSpec((s, LANES), lambda h, i: (0, heads + h))],
        out_specs=[blk, blk],
        out_shape=[jax.ShapeDtypeStruct((s, heads * LANES), BF16), jax.ShapeDtypeStruct((s, heads * LANES), F32)],
        scratch_shapes=[pltpu.VMEM((bq, LANES), F32), pltpu.VMEM((bq, LANES), F32), pltpu.VMEM((bq, LANES), F32)],
        compiler_params=pltpu.CompilerParams(dimension_semantics=("parallel", "arbitrary"),
                                             vmem_limit_bytes=VMEM_LIMIT),
    )(q, kv, kv)


def _mla_bwd(q, kv, o, do, lse, cos_t, sin_t, dkv_init, heads, *, name):
    s = q.shape[0]
    bq, bk, nq, ratio = _att_blocks(s)
    reps = (1, bk // LANES)
    has_init = dkv_init is not None

    def body(*refs):
        q_ref, k_ref, v_ref, o_ref, do_ref, lse_ref, c_ref, s_ref = refs[:8]
        ki_ref, vi_ref = (refs[8], refs[9]) if has_init else (None, None)
        dq_ref, dk_ref, dv_ref, dq_acc, dk_acc, dv_acc = refs[-6:]
        qi = pl.program_id(1)

        @pl.when(qi == 0)
        def _():
            if has_init:
                dk_acc[...] = ki_ref[...].astype(F32).T
                dv_acc[...] = vi_ref[...].astype(F32).T
            else:
                dk_acc[...] = jnp.zeros_like(dk_acc)
                dv_acc[...] = jnp.zeros_like(dv_acc)

        qv = q_ref[...]
        do = do_ref[...]
        q_t, do_t = qv.T, do.T
        delta = jnp.sum(do.astype(F32) * o_ref[...].astype(F32), axis=-1, keepdims=True)
        lse_wide = jnp.tile(lse_ref[...], reps)
        dq_acc[...] = jnp.zeros_like(dq_acc)

        def step(kb, masked, r0, r1):
            rows = pl.ds(pl.multiple_of(kb * bk, bk), bk)
            k, v = k_ref[rows, :], v_ref[rows, :]
            qs, dos = qv[r0:r1], do[r0:r1]
            p = jnp.exp(_dot_nt(qs, k) - lse_wide[r0:r1])
            if masked:
                p = jnp.where(_chunk_allowed(qi, kb, bq, bk, r0, r1), p, 0.0)
            ds = (p * (_dot_nt(dos, v) - delta[r0:r1])).astype(BF16)
            dq_acc[pl.ds(r0, r1 - r0), :] += _dot_nn(ds, k)
            dk_acc[:, rows] += _dot_nn(q_t[:, r0:r1], ds)
            dv_acc[:, rows] += _dot_nn(do_t[:, r0:r1], p.astype(BF16))

        _sweep(qi, ratio, bk, step, unroll=4)
        dq_ref[...] = _rope_slab_bwd(dq_acc[...] * MLA_SCALE, c_ref[...], s_ref[...]).astype(dq_ref.dtype)

        @pl.when(qi == nq - 1)
        def _():
            dk_ref[...] = dk_acc[...].T.astype(dk_ref.dtype)
            dv_ref[...] = dv_acc[...].T.astype(dv_ref.dtype)

    blk = pl.BlockSpec((bq, LANES), lambda h, i: (i, h))
    tab = pl.BlockSpec((bq, LANES), lambda h, i: (i, 0))
    k_full = pl.BlockSpec((s, LANES), lambda h, i: (0, h))
    v_full = pl.BlockSpec((s, LANES), lambda h, i: (0, heads + h))
    shape = jax.ShapeDtypeStruct((s, heads * LANES), BF16)
    ins = [q, kv, kv, o, do, lse, cos_t, sin_t] + ([dkv_init, dkv_init] if has_init else [])
    dq, dk, dv = pl.pallas_call(
        body, name=name, grid=(heads, nq),
        in_specs=[blk, k_full, v_full, blk, blk, blk, tab, tab] + ([k_full, v_full] if has_init else []),
        out_specs=[blk, k_full, k_full],
        out_shape=[shape, shape, shape],
        scratch_shapes=[pltpu.VMEM((bq, LANES), F32), pltpu.VMEM((LANES, s), F32), pltpu.VMEM((LANES, s), F32)],
        compiler_params=pltpu.CompilerParams(dimension_semantics=("arbitrary", "arbitrary"),
                                             vmem_limit_bytes=VMEM_LIMIT),
    )(*ins)
    return dq, jnp.concatenate([dk, dv], axis=1)


def _pad_last(a, width):
    return jnp.pad(a, [(0, 0)] * (a.ndim - 1) + [(0, width - a.shape[-1])])


def _pad_qkv(w, heads):
    d = w.shape[0]
    return _pad_last(w.reshape(d, 3 * heads, SB_HEAD_DIM), LANES).reshape(d, 3 * heads * LANES)


def _unpad_qkv(g, heads):
    d = g.shape[0]
    return g.reshape(d, 3 * heads, LANES)[:, :, :SB_HEAD_DIM].reshape(d, 3 * heads * SB_HEAD_DIM)


def _pad_o(w, heads):
    d = w.shape[1]
    w = w.reshape(heads, SB_HEAD_DIM, d)
    return jnp.pad(w, [(0, 0), (0, LANES - SB_HEAD_DIM), (0, 0)]).reshape(heads * LANES, d)


def _unpad_o(g, heads):
    d = g.shape[1]
    return g.reshape(heads, LANES, d)[:, :SB_HEAD_DIM, :].reshape(heads * SB_HEAD_DIM, d)


def _pad_uq(w, heads):
    r = w.shape[0]
    return _pad_last(w.reshape(r, heads, MLA_NOPE + MLA_ROPE), LANES).reshape(r, heads * LANES)


def _unpad_uq(g, heads):
    r = g.shape[0]
    return g.reshape(r, heads, LANES)[:, :, :MLA_NOPE + MLA_ROPE].reshape(r, heads * (MLA_NOPE + MLA_ROPE))


def _pad_dkv(w):
    d = w.shape[0]
    rope = jnp.zeros((d, LANES), w.dtype).at[:, ROPE_LO:ROPE_LO + MLA_ROPE].set(w[:, MLA_KV_RANK:])
    return jnp.concatenate([w[:, :MLA_KV_RANK], rope], axis=1)


def _unpad_dkv(g):
    return jnp.concatenate([g[:, :MLA_KV_RANK], g[:, MLA_KV_RANK + ROPE_LO:MLA_KV_RANK + ROPE_LO + MLA_ROPE]], axis=1)


def _pad_ukv(w, heads):
    w = w.reshape(MLA_KV_RANK, heads, 2, MLA_NOPE)
    k_part = _pad_last(w[:, :, 0, :], LANES).reshape(MLA_KV_RANK, heads * LANES)
    v_part = _pad_last(w[:, :, 1, :], LANES).reshape(MLA_KV_RANK, heads * LANES)
    lane = jnp.arange(LANES)
    place = ((lane[:, None] == lane[None, :]) & (lane[:, None] >= ROPE_LO) & (lane[:, None] < ROPE_LO + MLA_ROPE))
    place = jnp.tile(place.astype(w.dtype), (1, heads))
    top = jnp.concatenate([k_part, v_part], axis=1)
    bottom = jnp.concatenate([place, jnp.zeros_like(place)], axis=1)
    return jnp.concatenate([top, bottom], axis=0)


def _unpad_ukv(g, heads):
    g = g[:MLA_KV_RANK]
    k_part = g[:, :heads * LANES].reshape(MLA_KV_RANK, heads, LANES)[:, :, :MLA_NOPE]
    v_part = g[:, heads * LANES:].reshape(MLA_KV_RANK, heads, LANES)[:, :, :MLA_V]
    return jnp.stack([k_part, v_part], axis=2).reshape(MLA_KV_RANK, heads * (MLA_NOPE + MLA_V))


def _rope_tables(positions):
    inv_freq = ROPE_THETA ** (-jnp.arange(0, MLA_ROPE, 2, dtype=F32) / MLA_ROPE)
    ang = positions.astype(F32)[:, None] * inv_freq
    cos, sin = jnp.cos(ang), jnp.sin(ang)
    s = positions.shape[0]
    cos_t = jnp.ones((s, LANES), F32).at[:, ROPE_LO:ROPE_LO + MLA_ROPE].set(jnp.concatenate([cos, cos], axis=1))
    sin_t = jnp.zeros((s, LANES), F32).at[:, ROPE_LO:ROPE_LO + MLA_ROPE].set(jnp.concatenate([-sin, sin], axis=1))
    return cos_t, sin_t


def _local_step(x, positions, target, qkv_w0, norms, rest_weights, ride=None, early_reduce=None):
    s, d = x.shape
    heads = d // SB_HEAD_DIM
    cos_t, sin_t = _rope_tables(positions)

    h_first = _rms_fwd(x, norms["attn_norm"][0], name="l0_attn_norm")
    qkv_first = _mm(h_first, _pad_qkv(qkv_w0, heads), name="l0_qkv")
    if ride is None:
        o_first, gathered = _sb_fwd(qkv_first, heads, name="l0_sb_fwd"), None
    else:
        o_first, gathered = _sb_fwd(qkv_first, heads, name="l0_sb_fwd", ride=ride)
    w = rest_weights(gathered)
    later = w.pop("later", None)
    depth = norms["attn_norm"].shape[0]
    n_a = w["sb_w_qkv"].shape[0]
    n_b = depth - n_a
    assert later is None or n_a >= 2

    wqkv = [_pad_qkv(w["sb_w_qkv"][l], heads) for l in range(n_a)]
    wo_a = [_pad_o(w["sb_w_o"][l], heads) for l in range(n_a)]
    w1 = [w["mlp_w1"][l] for l in range(w["mlp_w1"].shape[0])]
    w2 = [w["mlp_w2"][l] for l in range(w["mlp_w2"].shape[0])]
    mla = {}

    def take_mla_weights(src):
        mla.update(wdkv=_pad_dkv(src["mla_w_dkv"]), wkv=_pad_ukv(src["mla_w_ukv"], heads),
                   wdq=[src["mla_w_dq"][j] for j in range(n_b)],
                   wuq=[_pad_uq(src["mla_w_uq"][j], heads) for j in range(n_b)],
                   wo_b=[_pad_o(src["mla_w_o"][j], heads) for j in range(n_b)])

    if later is None:
        take_mla_weights(w)

    saved = []
    kv_saved = None
    kv = None
    for l in range(depth):
        t = f"l{l}"
        sv = {"x_in": x}
        h = h_first if l == 0 else _rms_fwd(x, norms["attn_norm"][l], name=f"{t}_attn_norm")
        sv["h"] = h
        if l < n_a:
            if l == 0:
                qkv, o = qkv_first, o_first
            else:
                qkv = _mm(h, wqkv[l], name=f"{t}_qkv")
                if l == 1 and later is not None:
                    o, gathered_later = _sb_fwd(qkv, heads, name=f"{t}_sb_fwd", ride=later[0])
                    w_later = later[1](gathered_later)
                    take_mla_weights(w_later)
                    w1 += [w_later["mlp_w1"][i] for i in range(w_later["mlp_w1"].shape[0])]
                    w2 += [w_later["mlp_w2"][i] for i in range(w_later["mlp_w2"].shape[0])]
                else:
                    o = _sb_fwd(qkv, heads, name=f"{t}_sb_fwd")
            sv["qkv"], sv["o"] = qkv, o
            x = _mm(o, wo_a[l], name=f"{t}_attn_out", epilogue=_epi_add, extras=[(x, "tile")], out_dtypes=(F32,))
        else:
            j = l - n_a
            wdkv, wkv, wdq, wuq, wo_b = mla["wdkv"], mla["wkv"], mla["wdq"], mla["wuq"], mla["wo_b"]
            if j == 0:
                hk = _rms_fwd(x, norms["kv_norm"], name="kv_norm")
                down = _mm(hk, wdkv, name="kv_down", out_dtypes=(F32,))
                cat = _kv_prep(down, norms["mla_kv_lat_norm"], cos_t, sin_t, name="kv_prep")
                kv = _mm(cat, wkv, name="kv_up")
                kv_saved = {"x_in": x, "hk": hk, "down": down, "cat": cat}
            cq0 = _mm(h, wdq[j], name=f"{t}_q_down", out_dtypes=(F32,))
            cq = _rms_fwd(cq0, norms["mla_q_lat_norm"][j], name=f"{t}_q_lat_norm")
            q = _mm(cq, wuq[j], name=f"{t}_q_up", epilogue=_epi_rope_heads, extras=[(cos_t, "row"), (sin_t, "row")])
            o, lse = _mla_fwd(q, kv, heads, name=f"{t}_mla_fwd")
            sv.update(cq0=cq0, cq=cq, q=q, o=o, lse=lse)
            x = _mm(o, wo_b[j], name=f"{t}_attn_out", epilogue=_epi_add, extras=[(x, "tile")], out_dtypes=(F32,))
        sv["x_mid"] = x
        h2 = _rms_fwd(x, norms["mlp_norm"][l], name=f"{t}_mlp_norm")
        u, a = _mm(h2, w1[l], name=f"{t}_mlp_up", epilogue=_epi_relu2, out_dtypes=(BF16, BF16))
        sv.update(h2=h2, u=u, a=a)
        x = _mm(a, w2[l], name=f"{t}_mlp_down", epilogue=_epi_add, extras=[(x, "tile")], out_dtypes=(F32,))
        saved.append(sv)

    loss_slab, dx, dxb, dg_final = _loss_bwd(x, norms["final_norm"], target, name="loss")
    loss = loss_slab[0, 0]

    g_attn_norm, g_mlp_norm = [None] * depth, [None] * depth
    g_qkv, g_o_a = [None] * n_a, [None] * n_a
    g_dq, g_uq, g_o_b, g_qlat = [None] * n_b, [None] * n_b, [None] * n_b, [None] * n_b
    g_w1, g_w2 = [None] * depth, [None] * depth
    dkv = None
    g_kv_norm = g_kv_lat = g_dkv = g_ukv = None
    early_parts = early_got = None

    for l in reversed(range(depth)):
        t = f"l{l}"
        sv = saved[l]
        du = _mm(dxb, w2[l], name=f"{t}_mlp_down_dx", dims="nt", epilogue=_epi_relu2_grad, extras=[(sv["u"], "tile")])
        g_w2[l] = _mm(sv["a"], dxb, name=f"{t}_mlp_down_dw", dims="tn", out_dtypes=(F32,))
        g_w1[l] = _mm(sv["h2"], du, name=f"{t}_mlp_up_dw", dims="tn", out_dtypes=(F32,))
        dx, dxb, g_mlp_norm[l] = _mm_rms_bwd(du, w1[l], sv["x_mid"], norms["mlp_norm"][l], dx, name=f"{t}_mlp_up_dx")
        if l < n_a:
            do = _mm(dxb, wo_a[l], name=f"{t}_attn_out_dx", dims="nt")
            g_o_a[l] = _unpad_o(_mm(sv["o"], dxb, name=f"{t}_attn_out_dw", dims="tn", out_dtypes=(F32,)), heads)
            if l == 0 and early_reduce is not None:
                early_parts = early_reduce({
                    "sb_w_qkv": g_qkv[1:], "sb_w_o": g_o_a, "mla_w_dkv": g_dkv, "mla_w_ukv": g_ukv, "mla_w_dq": g_dq,
                    "mla_w_uq": g_uq, "mla_w_o": g_o_b, "mlp_w1": g_w1, "mlp_w2": g_w2})
                dq, dk, dv, early_got = _sb_bwd(sv["qkv"], sv["o"], do, heads, name=f"{t}_sb_bwd", ride=early_parts)
            else:
                dq, dk, dv = _sb_bwd(sv["qkv"], sv["o"], do, heads, name=f"{t}_sb_bwd")
            dqkv = jnp.concatenate([dq, dk, dv], axis=1)
            g_qkv[l] = _unpad_qkv(_mm(sv["h"], dqkv, name=f"{t}_qkv_dw", dims="tn", out_dtypes=(F32,)), heads)
            dh_src, dh_w = dqkv, wqkv[l]
        else:
            j = l - n_a
            do = _mm(dxb, wo_b[j], name=f"{t}_attn_out_dx", dims="nt")
            g_o_b[j] = _unpad_o(_mm(sv["o"], dxb, name=f"{t}_attn_out_dw", dims="tn", out_dtypes=(F32,)), heads)
            dq, dkv = _mla_bwd(sv["q"], kv, sv["o"], do, sv["lse"], cos_t, sin_t, dkv, heads, name=f"{t}_mla_bwd")
            g_uq[j] = _unpad_uq(_mm(sv["cq"], dq, name=f"{t}_q_up_dw", dims="tn", out_dtypes=(F32,)), heads)
            _, dcq0, g_qlat[j] = _mm_rms_bwd(dq, wuq[j], sv["cq0"], norms["mla_q_lat_norm"][j], None,
                                             name=f"{t}_q_up_dx")
            g_dq[j] = _mm(sv["h"], dcq0, name=f"{t}_q_down_dw", dims="tn", out_dtypes=(F32,))
            dh_src, dh_w = dcq0, wdq[j]
        if l == 0:
            dh = _mm(dh_src, dh_w, name=f"{t}_attn_in_dx", dims="nt", out_dtypes=(F32,))
            dx, dxb, g_attn_norm[l] = _rms_bwd(sv["x_in"], norms["attn_norm"][l], dh, dx, name=f"{t}_attn_norm_bwd",
                                               lead_axis=True)
        else:
            dx, dxb, g_attn_norm[l] = _mm_rms_bwd(dh_src, dh_w, sv["x_in"], norms["attn_norm"][l], dx,
                                                  name=f"{t}_attn_in_dx")
        if l == n_a:
            ks = kv_saved
            dcat = _mm(dkv, wkv, name="kv_up_dx", dims="nt", out_dtypes=(F32,))
            g_ukv = _unpad_ukv(_mm(ks["cat"], dkv, name="kv_up_dw", dims="tn", out_dtypes=(F32,)), heads)
            ddown, g_kv_lat = _kv_prep_bwd(ks["down"], norms["mla_kv_lat_norm"], cos_t, sin_t, dcat, name="kv_prep_bwd")
            g_dkv = _unpad_dkv(_mm(ks["hk"], ddown, name="kv_down_dw", dims="tn", out_dtypes=(F32,)))
            dx, dxb, g_kv_norm = _mm_rms_bwd(ddown, wdkv, ks["x_in"], norms["kv_norm"], dx, name="kv_down_dx")

    grads = {
        "attn_norm": jnp.concatenate(g_attn_norm, axis=0), "mlp_norm": jnp.concatenate(g_mlp_norm, axis=0),
        "sb_w_qkv": g_qkv, "sb_w_o": g_o_a,
        "kv_norm": g_kv_norm[0], "mla_w_dkv": g_dkv, "mla_kv_lat_norm": g_kv_lat[0], "mla_w_ukv": g_ukv,
        "mla_w_dq": g_dq, "mla_q_lat_norm": jnp.concatenate(g_qlat, axis=0),
        "mla_w_uq": g_uq, "mla_w_o": g_o_b,
        "mlp_w1": g_w1, "mlp_w2": g_w2, "final_norm": dg_final[0],
    }
    return loss, dx, grads, early_parts, early_got


def _flat_rows(n_elems):
    per_block = FLAT_COLS * FLAT_ROW_BLOCK * 2
    return -(-n_elems // per_block) * FLAT_ROW_BLOCK * 2


def _row_blocks(arrays, dtype):
    for a in arrays:
        assert a.size % FLAT_COLS == 0, a.shape
    blocks = [a.astype(dtype).reshape(-1, FLAT_COLS) for a in arrays]
    used = sum(b.shape[0] for b in blocks)
    rows = _flat_rows(used * FLAT_COLS)
    return blocks + [jnp.zeros((rows - used, FLAT_COLS), dtype)], rows


def _pack(arrays, dtype):
    blocks, _ = _row_blocks(arrays, dtype)
    return jnp.concatenate(blocks, axis=0)


def _pack_chips(per_chip, dtype):
    blocks, rows = [], 0
    for arrays in per_chip:
        chip_blocks, rows = _row_blocks(arrays, dtype)
        blocks += chip_blocks
    return jnp.concatenate(blocks, axis=0).reshape(len(per_chip), rows, FLAT_COLS)


def _unpack(flat, shapes):
    out, row = [], 0
    for shp in shapes:
        n = 1
        for v in shp:
            n *= v
        out.append(flat[row:row + n // FLAT_COLS].reshape(shp))
        row += n // FLAT_COLS
    return out


def _pack_small(arrays):
    rows = []
    for a in arrays:
        a = a.reshape(-1, a.shape[-1]) if a.shape[-1] == FLAT_COLS else a.reshape(1, -1)
        rows.append(_pad_last(a, FLAT_COLS))
    flat = jnp.concatenate(rows, axis=0)
    return jnp.pad(flat, [(0, -flat.shape[0] % 8), (0, 0)])


def _unpack_small(flat, shapes):
    out, row = [], 0
    for shp in shapes:
        if shp[-1] == FLAT_COLS:
            n = 1
            for v in shp[:-1]:
                n *= v
            out.append(flat[row:row + n].reshape(shp))
            row += n
        else:
            n = 1
            for v in shp:
                n *= v
            out.append(flat[row, :n].reshape(shp))
            row += 1
    return out


def _other_chips(x, y):
    return [(1 - x, y), (x, 1 - y), (1 - x, 1 - y)]


def _all_gather_chips(flat, *, name):
    rows, cols = flat.shape

    def body(x_ref, out_ref, *sems):
        _all_gather_start(x_ref, out_ref, sems)
        _all_gather_finish(x_ref, out_ref, sems)

    return pl.pallas_call(
        body, name=name,
        in_specs=[pl.BlockSpec(memory_space=pltpu.HBM)],
        out_specs=pl.BlockSpec(memory_space=pltpu.HBM),
        out_shape=jax.ShapeDtypeStruct((N_CHIPS, rows, cols), flat.dtype),
        scratch_shapes=_all_gather_sems(),
        compiler_params=pltpu.CompilerParams(has_side_effects=True),
    )(flat)


def _all_gather_sems():
    return [pltpu.SemaphoreType.DMA((3,)), pltpu.SemaphoreType.DMA((3,)), pltpu.SemaphoreType.DMA((3,)),
            pltpu.SemaphoreType.DMA((3,)), pltpu.SemaphoreType.DMA, pltpu.SemaphoreType.DMA]


def _all_gather_copies(x_ref, out_ref, sems, finishing):
    send_sems, recv_sems, pass_send_sems, pass_recv_sems, own_send_sem, own_recv_sem = sems
    x, y, c = lax.axis_index("x"), lax.axis_index("y"), lax.axis_index("c")
    me = 2 * x + y
    my_rows, sib_rows = _half_rows(x_ref.shape[0])

    def copy(src, dst, send_sem, recv_sem, to):
        return pltpu.make_async_remote_copy(src_ref=src, dst_ref=dst, send_sem=send_sem, recv_sem=recv_sem,
                                            device_id=to, device_id_type=MESH)

    own = copy(x_ref, out_ref.at[me], own_send_sem, own_recv_sem, _sibling())
    to_chips, landed, pass_on, passed = [], [], [], []
    for k, (px, py) in enumerate(_other_chips(x, y)):
        to_chips.append(copy(x_ref.at[my_rows, :], out_ref.at[me, my_rows, :], send_sems.at[k], recv_sems.at[k],
                             (px, py, c)))
        if finishing:
            mine, theirs = out_ref.at[2 * px + py, my_rows, :], out_ref.at[2 * px + py, sib_rows, :]
            landed.append(copy(mine, mine, send_sems.at[k], recv_sems.at[k], (px, py, c)))
            pass_on.append(copy(mine, mine, pass_send_sems.at[k], pass_recv_sems.at[k], _sibling()))
            passed.append(copy(theirs, theirs, pass_send_sems.at[k], pass_recv_sems.at[k], _sibling()))
    return own, to_chips, landed, pass_on, passed


def _all_gather_start(x_ref, out_ref, sems):
    own, to_chips, _, _, _ = _all_gather_copies(x_ref, out_ref, sems, finishing=False)
    own.start()
    for cp in to_chips:
        cp.start()


def _all_gather_finish(x_ref, out_ref, sems):
    own, to_chips, landed, pass_on, passed = _all_gather_copies(x_ref, out_ref, sems, finishing=True)
    for k in range(len(landed)):
        landed[k].wait_recv()
        pass_on[k].start()
    for cp in passed:
        cp.wait_recv()
    own.wait_recv()
    for cp in [own] + to_chips + pass_on:
        cp.wait_send()


def _exchange_chips(parts, *, name):
    def body(g_ref, out_ref, *sems):
        _exchange_start(g_ref, out_ref, sems)
        _exchange_finish(g_ref, out_ref, sems)

    return pl.pallas_call(
        body, name=name,
        in_specs=[pl.BlockSpec(memory_space=pltpu.HBM)],
        out_specs=pl.BlockSpec(memory_space=pltpu.HBM),
        out_shape=jax.ShapeDtypeStruct(parts.shape, parts.dtype),
        scratch_shapes=_exchange_sems(),
        compiler_params=pltpu.CompilerParams(has_side_effects=True),
    )(parts)


def _exchange_sems():
    return [pltpu.SemaphoreType.DMA((3,)), pltpu.SemaphoreType.DMA((3,))]


def _exchange_copies(g_ref, out_ref, sems, receiving):
    send_sems, recv_sems = sems
    x, y, c = lax.axis_index("x"), lax.axis_index("y"), lax.axis_index("c")
    me = 2 * x + y
    copies = []
    for k, (px, py) in enumerate(_other_chips(x, y)):
        src, dst = (g_ref.at[me], out_ref.at[2 * px + py]) if receiving else (g_ref.at[2 * px + py], out_ref.at[me])
        copies.append(pltpu.make_async_remote_copy(src_ref=src, dst_ref=dst, send_sem=send_sems.at[k],
                                                   recv_sem=recv_sems.at[k], device_id=(px, py, c),
                                                   device_id_type=MESH))
    return copies


def _exchange_start(g_ref, out_ref, sems):
    for cp in _exchange_copies(g_ref, out_ref, sems, receiving=False):
        cp.start()


def _exchange_finish(g_ref, out_ref, sems):
    for cp in _exchange_copies(g_ref, out_ref, sems, receiving=True):
        cp.wait_recv()
    for cp in _exchange_copies(g_ref, out_ref, sems, receiving=False):
        cp.wait_send()


def _my_chip():
    return 2 * lax.axis_index("x") + lax.axis_index("y")


def _half_rows(rows):
    c = lax.axis_index("c")
    half = rows // 2
    return pl.ds(pl.multiple_of(c * half, 8), half), pl.ds(pl.multiple_of((1 - c) * half, 8), half)


def _sibling():
    return (lax.axis_index("x"), lax.axis_index("y"), 1 - lax.axis_index("c"))


def _pair_exchange(parts, *, name):
    n, rows, cols = parts.shape

    def body(p_ref, theirs_ref, send_sem, recv_sem):
        _, sib_rows = _half_rows(rows)
        cp = pltpu.make_async_remote_copy(src_ref=p_ref.at[:, sib_rows, :], dst_ref=theirs_ref, send_sem=send_sem,
                                          recv_sem=recv_sem, device_id=_sibling(), device_id_type=MESH)
        cp.start()
        cp.wait()

    half = rows // 2
    theirs = pl.pallas_call(
        body, name=name,
        in_specs=[pl.BlockSpec(memory_space=pltpu.HBM)],
        out_specs=pl.BlockSpec(memory_space=pltpu.HBM),
        out_shape=jax.ShapeDtypeStruct((n, half, cols), parts.dtype),
        scratch_shapes=[pltpu.SemaphoreType.DMA, pltpu.SemaphoreType.DMA],
        compiler_params=pltpu.CompilerParams(has_side_effects=True),
    )(parts)
    mine = lax.dynamic_slice_in_dim(parts, lax.axis_index("c") * half, half, axis=1)
    return mine, theirs


def _pair_sum(mine, theirs, *, name):
    n, rows, cols = mine.shape

    def body(a_ref, b_ref, o_ref):
        o_ref[...] = (a_ref[...].astype(F32) + b_ref[...].astype(F32)).astype(o_ref.dtype)

    blk = pl.BlockSpec((n, FLAT_ROW_BLOCK, cols), lambda i: (0, i, 0))
    return pl.pallas_call(
        body, name=name, grid=(rows // FLAT_ROW_BLOCK,),
        in_specs=[blk, blk], out_specs=blk, out_shape=jax.ShapeDtypeStruct(mine.shape, mine.dtype),
        compiler_params=pltpu.CompilerParams(dimension_semantics=("parallel",), vmem_limit_bytes=VMEM_LIMIT),
    )(mine, theirs)


def _sum_chips(received, own, *, name):
    _, rows, cols = received.shape

    def body(p_ref, own_ref, o_ref):
        me = 2 * lax.axis_index("x") + lax.axis_index("y")
        slot = [jnp.where(me == j, own_ref[j], p_ref[j]).astype(F32) for j in range(N_CHIPS)]
        o_ref[...] = ((slot[0] + slot[1]) + slot[2]) + slot[3]

    blk = pl.BlockSpec((N_CHIPS, FLAT_ROW_BLOCK, cols), lambda i: (0, i, 0))
    return pl.pallas_call(
        body, name=name, grid=(rows // FLAT_ROW_BLOCK,),
        in_specs=[blk, blk],
        out_specs=pl.BlockSpec((FLAT_ROW_BLOCK, cols), lambda i: (i, 0)),
        out_shape=jax.ShapeDtypeStruct((rows, cols), F32),
        compiler_params=pltpu.CompilerParams(dimension_semantics=("parallel",), vmem_limit_bytes=VMEM_LIMIT),
    )(received, own)


def _join_cores(half, *, name):
    rows2, cols = half.shape

    def body(h_ref, out_ref, send_sem, recv_sem):
        my_rows, sib_rows = _half_rows(2 * rows2)
        cp = pltpu.make_async_remote_copy(src_ref=h_ref, dst_ref=out_ref.at[my_rows, :], send_sem=send_sem,
                                          recv_sem=recv_sem, device_id=_sibling(), device_id_type=MESH)
        cp.start()
        cp.wait_send()
        pltpu.make_async_remote_copy(src_ref=h_ref, dst_ref=out_ref.at[sib_rows, :], send_sem=send_sem,
                                     recv_sem=recv_sem, device_id=_sibling(), device_id_type=MESH).wait_recv()

    out = pl.pallas_call(
        body, name=name,
        in_specs=[pl.BlockSpec(memory_space=pltpu.HBM)],
        out_specs=pl.BlockSpec(memory_space=pltpu.HBM),
        out_shape=jax.ShapeDtypeStruct((2 * rows2, cols), half.dtype),
        scratch_shapes=[pltpu.SemaphoreType.DMA, pltpu.SemaphoreType.DMA],
        compiler_params=pltpu.CompilerParams(has_side_effects=True),
    )(half)
    return lax.dynamic_update_slice_in_dim(out, half, lax.axis_index("c") * rows2, axis=0)


def _all_reduce_small(v, *, name):
    rows, cols = v.shape
    flips = [(fx, fy, fc) for fx in (0, 1) for fy in (0, 1) for fc in (0, 1)][1:]

    def body(v_ref, out_ref, gath_ref, send_sems, recv_sems):
        x, y, c = lax.axis_index("x"), lax.axis_index("y"), lax.axis_index("c")
        me = 4 * x + 2 * y + c
        gath_ref[me] = v_ref[...]
        peers = [((1 - x) if fx else x, (1 - y) if fy else y, (1 - c) if fc else c) for fx, fy, fc in flips]
        sends = []
        for k, peer in enumerate(peers):
            cp = pltpu.make_async_remote_copy(src_ref=v_ref, dst_ref=gath_ref.at[me], send_sem=send_sems.at[k],
                                              recv_sem=recv_sems.at[k], device_id=peer, device_id_type=MESH)
            cp.start()
            sends.append(cp)
        for k, (px, py, pc) in enumerate(peers):
            pltpu.make_async_remote_copy(src_ref=v_ref, dst_ref=gath_ref.at[4 * px + 2 * py + pc],
                                         send_sem=send_sems.at[k], recv_sem=recv_sems.at[k],
                                         device_id=(px, py, pc), device_id_type=MESH).wait_recv()
        for cp in sends:
            cp.wait_send()
        total = gath_ref[0]
        for k in range(1, 8):
            total = total + gath_ref[k]
        out_ref[...] = total

    total, _ = pl.pallas_call(
        body, name=name,
        in_specs=[pl.BlockSpec(memory_space=pltpu.VMEM)],
        out_specs=[pl.BlockSpec(memory_space=pltpu.VMEM), pl.BlockSpec(memory_space=pltpu.VMEM)],
        out_shape=[jax.ShapeDtypeStruct((rows, cols), v.dtype), jax.ShapeDtypeStruct((8, rows, cols), v.dtype)],
        scratch_shapes=[pltpu.SemaphoreType.DMA((7,)), pltpu.SemaphoreType.DMA((7,))],
        compiler_params=pltpu.CompilerParams(has_side_effects=True),
    )(v)
    return total


def _adamw(w, g, m, v, *, name):
    shape = w.shape
    cols = shape[-1]
    w2, g2, m2, v2 = (a.reshape(-1, cols) for a in (w, g, m, v))
    rows = w2.shape[0]
    br = _pick_rows(rows, FLAT_ROW_BLOCK)

    def body(w_ref, g_ref, m_ref, v_ref, d_out, m_out, v_out):
        gv = g_ref[...]
        m_new = ADAM_B1 * m_ref[...] + (1.0 - ADAM_B1) * gv
        v_new = ADAM_B2 * v_ref[...] + (1.0 - ADAM_B2) * jnp.square(gv)
        m_hat = m_new / (1.0 - ADAM_B1 ** ADAM_STEP)
        v_hat = v_new / (1.0 - ADAM_B2 ** ADAM_STEP)
        d_out[...] = -ADAM_LR * (m_hat / (jnp.sqrt(v_hat) + ADAM_EPS) + ADAM_WD * w_ref[...])
        m_out[...] = m_new
        v_out[...] = v_new

    blk = pl.BlockSpec((br, cols), lambda i: (i, 0))
    out = jax.ShapeDtypeStruct((rows, cols), F32)
    outs = pl.pallas_call(
        body, name=name, grid=(rows // br,),
        in_specs=[blk] * 4, out_specs=[blk] * 3, out_shape=[out] * 3,
        compiler_params=pltpu.CompilerParams(dimension_semantics=("parallel",), vmem_limit_bytes=VMEM_LIMIT),
    )(w2, g2, m2, v2)
    return [o.reshape(shape) for o in outs]


def _pick_rows(rows, target):
    if rows <= target:
        return rows
    return max(b for b in range(8, target + 1, 8) if rows % b == 0)


def _assemble(gathered_shards, name, layer=False):
    return jnp.concatenate(gathered_shards, axis=SHARD_AXIS[name] - int(layer))


def _chip_shard(full, name, j):
    if isinstance(full, list):
        axis = SHARD_AXIS[name] - 1
        layers = full
    else:
        axis = SHARD_AXIS[name]
        layers = [full]
    n = layers[0].shape[axis] // N_CHIPS
    return [lax.slice_in_dim(g, j * n, (j + 1) * n, axis=axis) for g in layers]


def kernel(x, positions, attn_norm, mlp_norm, sb_w_qkv, sb_w_o, kv_norm, mla_w_dkv, mla_kv_lat_norm, mla_w_ukv, mla_w_dq, mla_q_lat_norm, mla_w_uq, mla_w_o, mlp_w1, mlp_w2, final_norm, loss_target, m_attn_norm, m_mlp_norm, m_sb_w_qkv, m_sb_w_o, m_kv_norm, m_mla_w_dkv, m_mla_kv_lat_norm, m_mla_w_ukv, m_mla_w_dq, m_mla_q_lat_norm, m_mla_w_uq, m_mla_w_o, m_mlp_w1, m_mlp_w2, m_final_norm, v_attn_norm, v_mlp_norm, v_sb_w_qkv, v_sb_w_o, v_kv_norm, v_mla_w_dkv, v_mla_kv_lat_norm, v_mla_w_ukv, v_mla_w_dq, v_mla_q_lat_norm, v_mla_w_uq, v_mla_w_o, v_mlp_w1, v_mlp_w2, v_final_norm):
    weights = dict(attn_norm=attn_norm, mlp_norm=mlp_norm, sb_w_qkv=sb_w_qkv, sb_w_o=sb_w_o, kv_norm=kv_norm,
                   mla_w_dkv=mla_w_dkv, mla_kv_lat_norm=mla_kv_lat_norm, mla_w_ukv=mla_w_ukv, mla_w_dq=mla_w_dq,
                   mla_q_lat_norm=mla_q_lat_norm, mla_w_uq=mla_w_uq, mla_w_o=mla_w_o, mlp_w1=mlp_w1, mlp_w2=mlp_w2,
                   final_norm=final_norm)
    m_in = dict(attn_norm=m_attn_norm, mlp_norm=m_mlp_norm, sb_w_qkv=m_sb_w_qkv, sb_w_o=m_sb_w_o, kv_norm=m_kv_norm,
                mla_w_dkv=m_mla_w_dkv, mla_kv_lat_norm=m_mla_kv_lat_norm, mla_w_ukv=m_mla_w_ukv, mla_w_dq=m_mla_w_dq,
                mla_q_lat_norm=m_mla_q_lat_norm, mla_w_uq=m_mla_w_uq, mla_w_o=m_mla_w_o, mlp_w1=m_mlp_w1,
                mlp_w2=m_mlp_w2, final_norm=m_final_norm)
    v_in = dict(attn_norm=v_attn_norm, mlp_norm=v_mlp_norm, sb_w_qkv=v_sb_w_qkv, sb_w_o=v_sb_w_o, kv_norm=v_kv_norm,
                mla_w_dkv=v_mla_w_dkv, mla_kv_lat_norm=v_mla_kv_lat_norm, mla_w_ukv=v_mla_w_ukv, mla_w_dq=v_mla_w_dq,
                mla_q_lat_norm=v_mla_q_lat_norm, mla_w_uq=v_mla_w_uq, mla_w_o=v_mla_w_o, mlp_w1=v_mlp_w1,
                mlp_w2=v_mlp_w2, final_norm=v_final_norm)
    shard_shapes = [weights[n].shape for n in BIG_WEIGHTS]
    small_shapes = [weights[n].shape for n in SMALL_WEIGHTS]

    first_name = BIG_WEIGHTS[0]
    qkv_first, qkv_later = weights[first_name][0], weights[first_name][1:]
    gathered_first = _all_gather_chips(_pack([qkv_first], BF16), name="first_weight_all_gather")
    qkv_w0 = _assemble([_unpack(gathered_first[j], [qkv_first.shape])[0] for j in range(N_CHIPS)], first_name, layer=True)
    ride_shapes = [qkv_later.shape] + shard_shapes[1:]
    early_names = ["sb_w_qkv", "sb_w_o", "mlp_w1", "mlp_w2"]
    early_blocks = [qkv_later, weights["sb_w_o"], weights["mlp_w1"][:1], weights["mlp_w2"][:1]]
    late_names = ["mla_w_dkv", "mla_w_ukv", "mla_w_dq", "mla_w_uq", "mla_w_o", "mlp_w1", "mlp_w2"]
    late_blocks = [weights[n] for n in late_names[:5]] + [weights["mlp_w1"][1:], weights["mlp_w2"][1:]]
    ride = _pack(early_blocks, BF16)

    def gathered_weights(gathered, names, blocks):
        per_chip = [_unpack(gathered[j], [b.shape for b in blocks]) for j in range(N_CHIPS)]
        return {n: _assemble([per_chip[j][i] for j in range(N_CHIPS)], n) for i, n in enumerate(names)}

    def rest_weights(gathered):
        full = gathered_weights(gathered, early_names, early_blocks)
        full[first_name] = jnp.concatenate([qkv_w0[None], full[first_name]], axis=0)
        full["later"] = (_pack(late_blocks, BF16), lambda g: gathered_weights(g, late_names, late_blocks))
        return full

    norms = {n: weights[n] for n in SMALL_WEIGHTS}

    def chip_parts(g, tag):
        parts = _pack_chips([[piece for n in BIG_WEIGHTS if n in g for piece in _chip_shard(g[n], n, j)]
                             for j in range(N_CHIPS)], BF16)
        mine, theirs = _pair_exchange(parts, name=f"grads_pair_exchange_{tag}")
        return _pair_sum(mine, theirs, name=f"grads_pair_sum_{tag}")

    def finish(received, chip_part, tag):
        g_half = _sum_chips(received, chip_part, name=f"grads_sum_chips_{tag}")
        return _join_cores(g_half, name=f"grads_join_cores_{tag}")

    loss, dx, grads, early_parts, early_got = _local_step(
        x[0], positions[0], loss_target, qkv_w0, norms, rest_weights, ride=ride,
        early_reduce=functools.partial(chip_parts, tag="early"))
    loss = lax.psum(loss, ("x", "y", "c"))
    early_sum = finish(early_got, early_parts, "early")
    last_parts = chip_parts({first_name: grads[first_name][:1]}, "last")
    last_sum = finish(_exchange_chips(last_parts, name="grads_exchange_last"), last_parts, "last")

    out_g = dict(zip(BIG_WEIGHTS, _unpack(early_sum, ride_shapes)))
    out_g[first_name] = jnp.concatenate([_unpack(last_sum, [qkv_first.shape])[0][None], out_g[first_name]], axis=0)
    out_d, out_m, out_v = {}, {}, {}
    for n in BIG_WEIGHTS:
        out_d[n], out_m[n], out_v[n] = _adamw(weights[n], out_g[n], m_in[n], v_in[n], name=f"adamw_{n}")

    small_sum = _all_reduce_small(_pack_small([grads[n] for n in SMALL_WEIGHTS]), name="gains_all_reduce")
    sd, sm, sv = _adamw(_pack_small([weights[n] for n in SMALL_WEIGHTS]), small_sum,
                        _pack_small([m_in[n] for n in SMALL_WEIGHTS]),
                        _pack_small([v_in[n] for n in SMALL_WEIGHTS]), name="adamw_gains")
    out_g.update(zip(SMALL_WEIGHTS, _unpack_small(small_sum, small_shapes)))
    out_d.update(zip(SMALL_WEIGHTS, _unpack_small(sd, small_shapes)))
    out_m.update(zip(SMALL_WEIGHTS, _unpack_small(sm, small_shapes)))
    out_v.update(zip(SMALL_WEIGHTS, _unpack_small(sv, small_shapes)))

    return (loss, dx, *[out_g[n] for n in ALL_WEIGHTS], *[out_d[n] for n in ALL_WEIGHTS],
            *[out_m[n] for n in ALL_WEIGHTS], *[out_v[n] for n in ALL_WEIGHTS])
```

```python
import functools

import jax
import jax.numpy as jnp
from jax import lax
from jax.experimental import pallas as pl
from jax.experimental.pallas import tpu as pltpu

F32 = jnp.float32
BF16 = jnp.bfloat16

LANES = 128
SB_HEAD_DIM = 64
MLA_NOPE = 64
MLA_ROPE = 32
MLA_V = 64
MLA_Q_RANK = 384
MLA_KV_RANK = 256
CHUNK = 64
ROPE_THETA = 10000.0
NORM_EPS = 1e-6
SB_SCALE = SB_HEAD_DIM ** -0.5
MLA_SCALE = (MLA_NOPE + MLA_ROPE) ** -0.5
ROPE_LO = MLA_NOPE
ROPE_HALF = MLA_ROPE // 2
ATT_Q_BLOCK = 1024
ATT_K_BLOCK = 256
MLA_FWD_K_BLOCK = 512
NEG_BIG = -1e30
SB_DEAD_LOG = -110.0
VMEM_LIMIT = 56 * 1024 * 1024

ADAM_LR = 0.001
ADAM_B1 = 0.9
ADAM_B2 = 0.999
ADAM_EPS = 1e-08
ADAM_WD = 0.01
ADAM_STEP = 10

FLAT_COLS = 1024
FLAT_ROW_BLOCK = 256
N_CHIPS = 4
MESH = pl.DeviceIdType.MESH

BIG_WEIGHTS = ["sb_w_qkv", "sb_w_o", "mla_w_dkv", "mla_w_ukv", "mla_w_dq", "mla_w_uq", "mla_w_o", "mlp_w1", "mlp_w2"]
SHARD_AXIS = {"sb_w_qkv": 2, "sb_w_o": 1, "mla_w_dkv": 0, "mla_w_ukv": 1, "mla_w_dq": 1, "mla_w_uq": 2,
              "mla_w_o": 1, "mlp_w1": 2, "mlp_w2": 1}
SMALL_WEIGHTS = ["attn_norm", "mlp_norm", "kv_norm", "mla_kv_lat_norm", "mla_q_lat_norm", "final_norm"]
ALL_WEIGHTS = ["attn_norm", "mlp_norm", "sb_w_qkv", "sb_w_o", "kv_norm", "mla_w_dkv", "mla_kv_lat_norm", "mla_w_ukv",
               "mla_w_dq", "mla_q_lat_norm", "mla_w_uq", "mla_w_o", "mlp_w1", "mlp_w2", "final_norm"]


def _dot(a, b, dims):
    return lax.dot_general(a, b, (dims, ((), ())), preferred_element_type=F32)


def _dot_nn(a, b):
    return _dot(a, b, ((1,), (0,)))


def _dot_nt(a, b):
    return _dot(a, b, ((1,), (1,)))


def _dot_tn(a, b):
    return _dot(a, b, ((0,), (0,)))


def _pick_block(n, target):
    if n <= target:
        return n
    best = max(b for b in range(LANES, target + 1, LANES) if n % b == 0)
    return best


MM_ROWS = 512
MM_COLS = 1024
MM_DEPTH = 4096
MM_DEPTH_TN = 2048


def _mm(a, b, *, name, dims="nn", epilogue=None, extras=(), out_dtypes=(BF16,), column_sum=False):
    if dims == "nn":
        (m, k), (k2, n) = a.shape, b.shape
    elif dims == "nt":
        (m, k), (n, k2) = a.shape, b.shape
    else:
        (k, m), (k2, n) = a.shape, b.shape
    assert k == k2, (name, a.shape, b.shape)
    if dims == "tn":
        bm, bn, bk = _pick_block(m, MM_COLS), _pick_block(n, MM_COLS), _pick_block(k, MM_DEPTH_TN)
    else:
        rows = MM_ROWS if k > MM_DEPTH // 2 else 2 * MM_ROWS
        cols = 2 * MM_COLS if (k <= MM_COLS and n >= 4 * MM_COLS) else MM_COLS
        bm, bn, bk = _pick_block(m, rows), _pick_block(n, cols), _pick_block(k, MM_DEPTH)
    nk = k // bk
    if dims == "tn":
        a_spec = pl.BlockSpec((bk, bm), lambda j, i, kk: (kk, i))
    else:
        a_spec = pl.BlockSpec((bm, bk), lambda j, i, kk: (i, kk))
    if dims == "nt":
        b_spec = pl.BlockSpec((bn, bk), lambda j, i, kk: (j, kk))
    else:
        b_spec = pl.BlockSpec((bk, bn), lambda j, i, kk: (kk, j))
    extra_specs = []
    for arr, kind in extras:
        if kind == "tile":
            assert arr.shape == (m, n), (name, arr.shape)
            extra_specs.append(pl.BlockSpec((bm, bn), lambda j, i, kk: (i, j)))
        elif kind == "vec":
            assert arr.shape == (1, n), (name, arr.shape)
            extra_specs.append(pl.BlockSpec((1, bn), lambda j, i, kk: (0, j)))
        else:
            assert arr.shape == (m, LANES), (name, arr.shape)
            extra_specs.append(pl.BlockSpec((bm, LANES), lambda j, i, kk: (i, 0)))
    n_extra = len(extras)
    n_out = len(out_dtypes)
    n_sum = int(column_sum)
    dot = {"nn": _dot_nn, "nt": _dot_nt, "tn": _dot_tn}[dims]

    def body(*refs):
        a_ref, b_ref = refs[0], refs[1]
        extra_refs = refs[2:2 + n_extra]
        out_refs = refs[2 + n_extra:2 + n_extra + n_out]

        def finish(acc):
            outs = (acc,) if epilogue is None else epilogue(acc, *[r[...] for r in extra_refs])
            for o_ref, o in zip(out_refs, outs):
                o_ref[...] = o.astype(o_ref.dtype)
            if column_sum:
                sum_ref = refs[2 + n_extra + n_out]
                first_rows = pl.program_id(1) == 0

                @pl.when(first_rows)
                def _():
                    sum_ref[...] = outs[n_out]

                @pl.when(jnp.logical_not(first_rows))
                def _():
                    sum_ref[...] += outs[n_out]

        part = dot(a_ref[...].astype(BF16), b_ref[...].astype(BF16))
        if nk == 1:
            finish(part)
            return
        acc_ref = refs[-1]
        kk = pl.program_id(2)

        @pl.when(kk == 0)
        def _():
            acc_ref[...] = part

        @pl.when(kk > 0)
        def _():
            acc_ref[...] += part

        @pl.when(kk == nk - 1)
        def _():
            finish(acc_ref[...])

    outs = pl.pallas_call(
        body, name=name, grid=(n // bn, m // bm, nk),
        in_specs=[a_spec, b_spec] + extra_specs,
        out_specs=[pl.BlockSpec((bm, bn), lambda j, i, kk: (i, j)) for _ in range(n_out)]
        + [pl.BlockSpec((1, bn), lambda j, i, kk: (0, j))] * n_sum,
        out_shape=[jax.ShapeDtypeStruct((m, n), dt) for dt in out_dtypes] + [jax.ShapeDtypeStruct((1, n), F32)] * n_sum,
        scratch_shapes=[pltpu.VMEM((bm, bn), F32)] if nk > 1 else [],
        compiler_params=pltpu.CompilerParams(
            dimension_semantics=("parallel", "arbitrary" if column_sum else "parallel", "arbitrary"),
            vmem_limit_bytes=VMEM_LIMIT),
    )(a, b, *[arr for arr, _ in extras])
    return outs[0] if n_out + n_sum == 1 else outs


def _epi_add(acc, res):
    return (res + acc,)


def _epi_relu2(acc):
    r = jnp.maximum(acc, 0.0)
    return acc, r * r


def _epi_relu2_grad(acc, u):
    return (acc * (2.0 * jnp.maximum(u.astype(F32), 0.0)),)


def _rope_slab(t, cos_t, sin_t):
    lane = lax.broadcasted_iota(jnp.int32, t.shape, 1)
    partner = jnp.where(lane < ROPE_LO + ROPE_HALF, pltpu.roll(t, LANES - ROPE_HALF, 1), pltpu.roll(t, ROPE_HALF, 1))
    return t * cos_t + partner * sin_t


def _rope_slab_bwd(d, cos_t, sin_t):
    ds = d * sin_t
    lane = lax.broadcasted_iota(jnp.int32, d.shape, 1)
    partner = jnp.where(lane < ROPE_LO + ROPE_HALF, pltpu.roll(ds, LANES - ROPE_HALF, 1), pltpu.roll(ds, ROPE_HALF, 1))
    in_rope = (lane >= ROPE_LO) & (lane < ROPE_LO + MLA_ROPE)
    return d * cos_t + jnp.where(in_rope, partner, 0.0)


def _epi_rope_heads(acc, cos_t, sin_t):
    slabs = [_rope_slab(acc[:, j * LANES:(j + 1) * LANES], cos_t, sin_t) for j in range(acc.shape[1] // LANES)]
    return (jnp.concatenate(slabs, axis=1) * MLA_SCALE,)


def _row_block(s):
    return min(512, s)


def _rms_fwd(x, g, *, name):
    s, d = x.shape
    bm = _row_block(s)

    def body(x_ref, g_ref, o_ref):
        xv = x_ref[...]
        r = lax.rsqrt(jnp.mean(xv * xv, axis=-1, keepdims=True) + NORM_EPS)
        o_ref[...] = ((xv * r) * g_ref[...]).astype(o_ref.dtype)

    return pl.pallas_call(
        body, name=name, grid=(s // bm,),
        in_specs=[pl.BlockSpec((bm, d), lambda i: (i, 0)), pl.BlockSpec((1, d), lambda i: (0, 0))],
        out_specs=pl.BlockSpec((bm, d), lambda i: (i, 0)),
        out_shape=jax.ShapeDtypeStruct((s, d), BF16),
        compiler_params=pltpu.CompilerParams(dimension_semantics=("parallel",), vmem_limit_bytes=VMEM_LIMIT),
    )(x, g.reshape(1, d))


def _rms_bwd_math(xv, gv, dy):
    r = lax.rsqrt(jnp.mean(xv * xv, axis=-1, keepdims=True) + NORM_EPS)
    xhat = xv * r
    dyg = dy * gv
    mdot = jnp.mean(dyg * xhat, axis=-1, keepdims=True)
    dx = r * (dyg - xhat * mdot)
    dg = jnp.sum(dy * xhat, axis=0, keepdims=True)
    return dx, dg


def _rms_bwd(x, g, dy, dres, *, name, lead_axis=False):
    s, d = x.shape
    bm = _row_block(s)
    has_res = dres is not None

    def body(*refs):
        x_ref, g_ref, dy_ref = refs[:3]
        dres_ref = refs[3] if has_res else None
        dx_ref, dxb_ref, dg_ref = refs[-3:]
        dx, dg = _rms_bwd_math(x_ref[...], g_ref[...], dy_ref[...].astype(F32))
        if has_res:
            dx = dx + dres_ref[...]
        dx_ref[...] = dx
        dxb_ref[...] = dx.astype(BF16)

        @pl.when(pl.program_id(0) == 0)
        def _():
            dg_ref[...] = jnp.zeros_like(dg_ref)

        dg_ref[...] += dg

    row = pl.BlockSpec((bm, d), lambda i: (i, 0))
    vec = pl.BlockSpec((1, d), lambda i: (0, 0))
    ins = [x, g.reshape(1, d), dy] + ([dres] if has_res else [])
    dx_spec, dx_shape = row, (s, d)
    if lead_axis:
        dx_spec, dx_shape = pl.BlockSpec((None, bm, d), lambda i: (0, i, 0)), (1, s, d)
    return pl.pallas_call(
        body, name=name, grid=(s // bm,),
        in_specs=[row, vec, row] + ([row] if has_res else []),
        out_specs=[dx_spec, row, vec],
        out_shape=[jax.ShapeDtypeStruct(dx_shape, F32), jax.ShapeDtypeStruct((s, d), BF16),
                   jax.ShapeDtypeStruct((1, d), F32)],
        compiler_params=pltpu.CompilerParams(dimension_semantics=("arbitrary",), vmem_limit_bytes=VMEM_LIMIT),
    )(*ins)


def _epi_rms_bwd(acc, x, g, dres=None):
    dx, dg = _rms_bwd_math(x, g, acc)
    if dres is not None:
        dx = dx + dres
    return dx, dx, dg


def _mm_rms_bwd(dy_src, w, x, g, dres, *, name):
    d = x.shape[1]
    assert w.shape[0] == d and d <= MM_COLS, (name, w.shape, x.shape)
    extras = [(x, "tile"), (g.reshape(1, d), "vec")] + ([(dres, "tile")] if dres is not None else [])
    return _mm(dy_src, w, name=name, dims="nt", epilogue=_epi_rms_bwd, extras=extras, out_dtypes=(F32, BF16),
               column_sum=True)


def _loss_bwd(x, g, target, *, name):
    s, d = x.shape
    bm = _row_block(s)

    def body(x_ref, g_ref, t_ref, loss_ref, dx_ref, dxb_ref, dg_ref):
        xv, gv = x_ref[...], g_ref[...]
        r = lax.rsqrt(jnp.mean(xv * xv, axis=-1, keepdims=True) + NORM_EPS)
        err = (xv * r) * gv - t_ref[...]
        dx, dg = _rms_bwd_math(xv, gv, err * (1.0 / d))
        dx_ref[...] = dx
        dxb_ref[...] = dx.astype(BF16)

        @pl.when(pl.program_id(0) == 0)
        def _():
            dg_ref[...] = jnp.zeros_like(dg_ref)
            loss_ref[...] = jnp.zeros_like(loss_ref)

        dg_ref[...] += dg
        loss_ref[...] += jnp.sum(jnp.mean(err * err, axis=-1, keepdims=True), axis=0, keepdims=True) * 0.5

    row = pl.BlockSpec((bm, d), lambda i: (i, 0))
    vec = pl.BlockSpec((1, d), lambda i: (0, 0))
    assert target.shape == (1, s, d), target.shape
    return pl.pallas_call(
        body, name=name, grid=(s // bm,),
        in_specs=[row, vec, pl.BlockSpec((None, bm, d), lambda i: (0, i, 0))],
        out_specs=[pl.BlockSpec((8, LANES), lambda i: (0, 0)), row, row, vec],
        out_shape=[jax.ShapeDtypeStruct((8, LANES), F32), jax.ShapeDtypeStruct((s, d), F32),
                   jax.ShapeDtypeStruct((s, d), BF16), jax.ShapeDtypeStruct((1, d), F32)],
        compiler_params=pltpu.CompilerParams(dimension_semantics=("arbitrary",), vmem_limit_bytes=VMEM_LIMIT),
    )(x, g.reshape(1, d), target)


def _kv_prep(down, g, cos_t, sin_t, *, name):
    s, w = down.shape
    bm = _row_block(s)

    def body(d_ref, g_ref, c_ref, s_ref, o_ref):
        lat = d_ref[:, :MLA_KV_RANK]
        r = lax.rsqrt(jnp.mean(lat * lat, axis=-1, keepdims=True) + NORM_EPS)
        o_ref[:, :MLA_KV_RANK] = ((lat * r) * g_ref[...]).astype(BF16)
        o_ref[:, MLA_KV_RANK:] = _rope_slab(d_ref[:, MLA_KV_RANK:], c_ref[...], s_ref[...]).astype(BF16)

    row = pl.BlockSpec((bm, w), lambda i: (i, 0))
    tab = pl.BlockSpec((bm, LANES), lambda i: (i, 0))
    return pl.pallas_call(
        body, name=name, grid=(s // bm,),
        in_specs=[row, pl.BlockSpec((1, MLA_KV_RANK), lambda i: (0, 0)), tab, tab],
        out_specs=row, out_shape=jax.ShapeDtypeStruct((s, w), BF16),
        compiler_params=pltpu.CompilerParams(dimension_semantics=("parallel",), vmem_limit_bytes=VMEM_LIMIT),
    )(down, g.reshape(1, MLA_KV_RANK), cos_t, sin_t)


def _kv_prep_bwd(down, g, cos_t, sin_t, dcat, *, name):
    s, w = down.shape
    bm = _row_block(s)

    def body(d_ref, g_ref, c_ref, s_ref, dc_ref, o_ref, dg_ref):
        dlat, dg = _rms_bwd_math(d_ref[:, :MLA_KV_RANK], g_ref[...], dc_ref[:, :MLA_KV_RANK])
        o_ref[:, :MLA_KV_RANK] = dlat.astype(BF16)
        o_ref[:, MLA_KV_RANK:] = _rope_slab_bwd(dc_ref[:, MLA_KV_RANK:], c_ref[...], s_ref[...]).astype(BF16)

        @pl.when(pl.program_id(0) == 0)
        def _():
            dg_ref[...] = jnp.zeros_like(dg_ref)

        dg_ref[...] += dg

    row = pl.BlockSpec((bm, w), lambda i: (i, 0))
    tab = pl.BlockSpec((bm, LANES), lambda i: (i, 0))
    vec = pl.BlockSpec((1, MLA_KV_RANK), lambda i: (0, 0))
    return pl.pallas_call(
        body, name=name, grid=(s // bm,),
        in_specs=[row, vec, tab, tab, row],
        out_specs=[row, vec],
        out_shape=[jax.ShapeDtypeStruct((s, w), BF16), jax.ShapeDtypeStruct((1, MLA_KV_RANK), F32)],
        compiler_params=pltpu.CompilerParams(dimension_semantics=("arbitrary",), vmem_limit_bytes=VMEM_LIMIT),
    )(down, g.reshape(1, MLA_KV_RANK), cos_t, sin_t, dcat)


def _split_bf16(v):
    hi = v.astype(BF16)
    lo = (v - hi.astype(F32)).astype(BF16)
    return hi, lo


def _suffix_matrices(n):
    row = lax.broadcasted_iota(jnp.int32, (n, n), 0)
    col = lax.broadcasted_iota(jnp.int32, (n, n), 1)
    incl = (row >= col).astype(BF16)
    return (row > col).astype(BF16), jnp.concatenate([incl, incl], axis=0)


def _suffix_sum(v, matrix):
    hi, lo = _split_bf16(v)
    return _dot_nn(jnp.concatenate([hi, lo], axis=1), matrix)


def _block_positions(qi, kb, bq, bk, r0, r1):
    row = qi * bq + r0 + lax.broadcasted_iota(jnp.int32, (r1 - r0, bk), 0)
    col = kb * bk + lax.broadcasted_iota(jnp.int32, (r1 - r0, bk), 1)
    return row, col


def _att_blocks(s, key_block=ATT_K_BLOCK):
    bq, bk = min(ATT_Q_BLOCK, s), min(key_block, s)
    return bq, bk, s // bq, bq // bk


def _sweep(qi, ratio, bk, step, unroll=2, alive=None):
    bq = ratio * bk
    for d in range(ratio):
        kb, r0 = (qi + 1) * ratio - 1 - d, (ratio - 1 - d) * bk
        near = bq if alive is None else min(r0 + 2 * bk, bq)
        step(kb, True, r0, near)
        if near < bq:
            pl.when(alive(near))(functools.partial(step, kb, False, near, bq))
    unroll = unroll if ratio % unroll == 0 else 1
    trips = qi * (ratio // unroll)

    def trip(i):
        for u in range(unroll):
            kb = qi * ratio - 1 - (i * unroll + u)
            if alive is None or ratio == 1:
                step(kb, False, 0, bq)
            else:
                step(kb, False, 0, bk)
                pl.when(alive(bk))(functools.partial(step, kb, False, bk, bq))

    if alive is None:
        lax.fori_loop(0, trips, lambda i, carry: (trip(i), carry)[1], 0)
    else:
        lax.while_loop(lambda i: jnp.logical_and(i < trips, alive(0)), lambda i: (trip(i), i + 1)[1], 0)


def _stick_left(c_ref, r0):
    return jnp.max(c_ref[r0:, :]) > SB_DEAD_LOG


def _sb_logs(q, k):
    z = _dot_nt(q, k)
    lb = jnp.minimum(z, 0.0) - jnp.log(1.0 + jnp.exp(-jnp.abs(z)))
    return lb, lb - z


def _sb_fwd(qkv, heads, *, name, ride=None):
    s = qkv.shape[0]
    bq, bk, nq, ratio = _att_blocks(s)
    riding = ride is not None

    def body(*refs):
        if riding:
            q_ref, k_ref, v_ref, w_ref, o_ref, gath_ref, acc_ref, c_ref = refs[:8]
            first = jnp.logical_and(pl.program_id(0) == 0, pl.program_id(1) == 0)
            last = jnp.logical_and(pl.program_id(0) == heads - 1, pl.program_id(1) == nq - 1)
            pl.when(first)(functools.partial(_all_gather_start, w_ref, gath_ref, refs[8:]))
        else:
            q_ref, k_ref, v_ref, o_ref, acc_ref, c_ref = refs
        qi = pl.program_id(1)
        q = q_ref[...] * SB_SCALE
        m_strict, _ = _suffix_matrices(bk)
        acc_ref[...] = jnp.zeros_like(acc_ref)
        c_ref[...] = jnp.zeros_like(c_ref)

        def step(kb, masked, r0, r1):
            rows = pl.ds(pl.multiple_of(kb * bk, bk), bk)
            mine = pl.ds(r0, r1 - r0)
            k, v = k_ref[rows, :], v_ref[rows, :]
            lb, lk = _sb_logs(q[r0:r1], k)
            if masked:
                row, col = _block_positions(qi, kb, bq, bk, r0, r1)
                causal = col < row
                lk = jnp.where(causal, lk, 0.0)
            c = c_ref[mine, :]
            w = jnp.exp(lb + _dot_nn(lk.astype(BF16), m_strict) + jnp.tile(c, (1, bk // LANES)))
            if masked:
                w = jnp.where(causal, w, 0.0)
            acc_ref[mine, :] += _dot_nn(w.astype(BF16), v)
            c_ref[mine, :] = c + jnp.sum(lk, axis=-1, keepdims=True)

        _sweep(qi, ratio, bk, step, unroll=1, alive=functools.partial(_stick_left, c_ref))
        o_ref[...] = acc_ref[...].astype(o_ref.dtype)
        if riding:
            pl.when(last)(functools.partial(_all_gather_finish, w_ref, gath_ref, refs[8:]))

    hbm = pl.BlockSpec(memory_space=pltpu.HBM)
    o_spec = pl.BlockSpec((bq, LANES), lambda h, i: (i, h))
    o_shape = jax.ShapeDtypeStruct((s, heads * LANES), F32)
    return pl.pallas_call(
        body, name=name, grid=(heads, nq),
        in_specs=[pl.BlockSpec((bq, LANES), lambda h, i: (i, h)),
                  pl.BlockSpec((s, LANES), lambda h, i: (0, heads + h)),
                  pl.BlockSpec((s, LANES), lambda h, i: (0, 2 * heads + h))] + ([hbm] if riding else []),
        out_specs=[o_spec, hbm] if riding else o_spec,
        out_shape=[o_shape, jax.ShapeDtypeStruct((N_CHIPS,) + ride.shape, ride.dtype)] if riding else o_shape,
        scratch_shapes=[pltpu.VMEM((bq, LANES), F32), pltpu.VMEM((bq, LANES), F32)]
        + (_all_gather_sems() if riding else []),
        compiler_params=pltpu.CompilerParams(dimension_semantics=("arbitrary", "arbitrary"),
                                             vmem_limit_bytes=VMEM_LIMIT, has_side_effects=riding),
    )(*([qkv, qkv, qkv] + ([ride] if riding else [])))


def _sb_bwd(qkv, o, do, heads, *, name, ride=None):
    s = qkv.shape[0]
    bq, bk, nq, ratio = _att_blocks(s)
    riding = ride is not None

    def body(*refs):
        if riding:
            (q_ref, k_ref, v_ref, o_ref, do_ref, g_ref, dq_ref, dk_ref, dv_ref, got_ref,
             dq_acc, dk_acc, dv_acc, c_ref, e_ref) = refs[:15]
            first = jnp.logical_and(pl.program_id(0) == 0, pl.program_id(1) == 0)
            last = jnp.logical_and(pl.program_id(0) == heads - 1, pl.program_id(1) == nq - 1)
            pl.when(first)(functools.partial(_exchange_start, g_ref, got_ref, refs[15:]))
        else:
            q_ref, k_ref, v_ref, o_ref, do_ref, dq_ref, dk_ref, dv_ref, dq_acc, dk_acc, dv_acc, c_ref, e_ref = refs
        qi = pl.program_id(1)

        @pl.when(qi == 0)
        def _():
            dk_acc[...] = jnp.zeros_like(dk_acc)
            dv_acc[...] = jnp.zeros_like(dv_acc)

        q = q_ref[...] * SB_SCALE
        do = do_ref[...]
        q_t, do_t = q.T, do.T
        total = jnp.sum(do.astype(F32) * o_ref[...].astype(F32), axis=-1, keepdims=True)
        m_strict, m_incl = _suffix_matrices(bk)
        dq_acc[...] = jnp.zeros_like(dq_acc)
        c_ref[...] = jnp.zeros_like(c_ref)
        e_ref[...] = jnp.broadcast_to(total, e_ref.shape)
        reps = (1, bk // LANES)

        def step(kb, masked, r0, r1):
            rows = pl.ds(pl.multiple_of(kb * bk, bk), bk)
            mine = pl.ds(r0, r1 - r0)
            k, v = k_ref[rows, :], v_ref[rows, :]
            qs, dos = q[r0:r1], do[r0:r1]
            lb, lk_all = _sb_logs(qs, k)
            lk = lk_all
            if masked:
                row, col = _block_positions(qi, kb, bq, bk, r0, r1)
                causal = col < row
                lk = jnp.where(causal, lk_all, 0.0)
            c = c_ref[mine, :]
            w = jnp.exp(lb + _dot_nn(lk.astype(BF16), m_strict) + jnp.tile(c, reps))
            if masked:
                w = jnp.where(causal, w, 0.0)
            wb = w.astype(BF16)
            g = wb.astype(F32) * _dot_nt(dos, v)
            e = e_ref[mine, :]
            g_left = jnp.tile(e, reps) - _suffix_sum(g, m_incl)
            da = g * jnp.exp(lk_all) - jnp.exp(lb) * g_left
            if masked:
                da = jnp.where(causal, da, 0.0)
            dab = da.astype(BF16)
            dq_acc[mine, :] += _dot_nn(dab, k)
            dk_acc[:, rows] += _dot_nn(q_t[:, r0:r1], dab)
            dv_acc[:, rows] += _dot_nn(do_t[:, r0:r1], wb)
            e_ref[mine, :] = e - jnp.sum(g, axis=-1, keepdims=True)
            c_ref[mine, :] = c + jnp.sum(lk, axis=-1, keepdims=True)

        _sweep(qi, ratio, bk, step, unroll=1, alive=functools.partial(_stick_left, c_ref))
        dq_ref[...] = (dq_acc[...] * SB_SCALE).astype(dq_ref.dtype)

        @pl.when(qi == nq - 1)
        def _():
            dk_ref[...] = dk_acc[...].T.astype(dk_ref.dtype)
            dv_ref[...] = dv_acc[...].T.astype(dv_ref.dtype)

        if riding:
            pl.when(last)(functools.partial(_exchange_finish, g_ref, got_ref, refs[15:]))

    blk = pl.BlockSpec((bq, LANES), lambda h, i: (i, h))
    full = pl.BlockSpec((s, LANES), lambda h, i: (0, h))
    hbm = pl.BlockSpec(memory_space=pltpu.HBM)
    shape = jax.ShapeDtypeStruct((s, heads * LANES), BF16)
    return pl.pallas_call(
        body, name=name, grid=(heads, nq),
        in_specs=[blk,
                  pl.BlockSpec((s, LANES), lambda h, i: (0, heads + h)),
                  pl.BlockSpec((s, LANES), lambda h, i: (0, 2 * heads + h)),
                  blk, blk] + ([hbm] if riding else []),
        out_specs=[blk, full, full] + ([hbm] if riding else []),
        out_shape=[shape, shape, shape] + ([jax.ShapeDtypeStruct(ride.shape, ride.dtype)] if riding else []),
        scratch_shapes=[pltpu.VMEM((bq, LANES), F32), pltpu.VMEM((LANES, s), F32), pltpu.VMEM((LANES, s), F32),
                        pltpu.VMEM((bq, LANES), F32), pltpu.VMEM((bq, LANES), F32)]
        + (_exchange_sems() if riding else []),
        compiler_params=pltpu.CompilerParams(dimension_semantics=("arbitrary", "arbitrary"),
                                             vmem_limit_bytes=VMEM_LIMIT, has_side_effects=riding),
    )(*([qkv, qkv, qkv, o, do] + ([ride] if riding else [])))


def _chunk_allowed(qi, kb, bq, bk, r0, r1):
    row, col = _block_positions(qi, kb, bq, bk, r0, r1)
    return (col // CHUNK) <= (row // CHUNK)


def _mla_fwd(q, kv, heads, *, name):
    s = q.shape[0]
    bq, bk, nq, ratio = _att_blocks(s, MLA_FWD_K_BLOCK)
    reps = (1, bk // LANES)

    def body(q_ref, k_ref, v_ref, o_ref, lse_ref, acc_ref, m_ref, l_ref):
        qi = pl.program_id(1)
        qv = q_ref[...]
        acc_ref[...] = jnp.zeros_like(acc_ref)
        m_ref[...] = jnp.full_like(m_ref, NEG_BIG)
        l_ref[...] = jnp.zeros_like(l_ref)

        def step(kb, masked, r0, r1):
            rows = pl.ds(pl.multiple_of(kb * bk, bk), bk)
            mine = pl.ds(r0, r1 - r0)
            k, v = k_ref[rows, :], v_ref[rows, :]
            sc = _dot_nt(qv[r0:r1], k)
            if masked:
                allowed = _chunk_allowed(qi, kb, bq, bk, r0, r1)
                sc = jnp.where(allowed, sc, NEG_BIG)
            m_old = m_ref[mine, :]
            m_new = jnp.maximum(m_old, jnp.max(sc, axis=-1, keepdims=True))
            p = jnp.exp(sc - jnp.tile(m_new, reps))
            alpha = jnp.exp(m_old - m_new)
            l_ref[mine, :] = alpha * l_ref[mine, :] + jnp.sum(p, axis=-1, keepdims=True)
            acc_ref[mine, :] = alpha * acc_ref[mine, :] + _dot_nn(p.astype(BF16), v)
            m_ref[mine, :] = m_new

        _sweep(qi, ratio, bk, step)
        o_ref[...] = (acc_ref[...] / l_ref[...]).astype(o_ref.dtype)
        lse_ref[...] = m_ref[...] + jnp.log(l_ref[...])

    blk = pl.BlockSpec((bq, LANES), lambda h, i: (i, h))
    return pl.pallas_call(
        body, name=name, grid=(heads, nq),
        in_specs=[blk,
                  pl.BlockSpec((s, LANES), lambda h, i: (0, h)),
                  pl.BlockSpec((s, LANES), lambda h, i: (0, heads + h))],
        out_specs=[blk, blk],
        out_shape=[jax.ShapeDtypeStruct((s, heads * LANES), BF16), jax.ShapeDtypeStruct((s, heads * LANES), F32)],
        scratch_shapes=[pltpu.VMEM((bq, LANES), F32), pltpu.VMEM((bq, LANES), F32), pltpu.VMEM((bq, LANES), F32)],
        compiler_params=pltpu.CompilerParams(dimension_semantics=("parallel", "arbitrary"),
                                             vmem_limit_bytes=VMEM_LIMIT),
    )(q, kv, kv)


def _mla_bwd(q, kv, o, do, lse, cos_t, sin_t, dkv_init, heads, *, name):
    s = q.shape[0]
    bq, bk, nq, ratio = _att_blocks(s)
    reps = (1, bk // LANES)
    has_init = dkv_init is not None

    def body(*refs):
        q_ref, k_ref, v_ref, o_ref, do_ref, lse_ref, c_ref, s_ref = refs[:8]
        ki_ref, vi_ref = (refs[8], refs[9]) if has_init else (None, None)
        dq_ref, dk_ref, dv_ref, dq_acc, dk_acc, dv_acc = refs[-6:]
        qi = pl.program_id(1)

        @pl.when(qi == 0)
        def _():
            if has_init:
                dk_acc[...] = ki_ref[...].astype(F32).T
                dv_acc[...] = vi_ref[...].astype(F32).T
            else:
                dk_acc[...] = jnp.zeros_like(dk_acc)
                dv_acc[...] = jnp.zeros_like(dv_acc)

        qv = q_ref[...]
        do = do_ref[...]
        q_t, do_t = qv.T, do.T
        delta = jnp.sum(do.astype(F32) * o_ref[...].astype(F32), axis=-1, keepdims=True)
        lse_wide = jnp.tile(lse_ref[...], reps)
        dq_acc[...] = jnp.zeros_like(dq_acc)

        def step(kb, masked, r0, r1):
            rows = pl.ds(pl.multiple_of(kb * bk, bk), bk)
            k, v = k_ref[rows, :], v_ref[rows, :]
            qs, dos = qv[r0:r1], do[r0:r1]
            p = jnp.exp(_dot_nt(qs, k) - lse_wide[r0:r1])
            if masked:
                p = jnp.where(_chunk_allowed(qi, kb, bq, bk, r0, r1), p, 0.0)
            ds = (p * (_dot_nt(dos, v) - delta[r0:r1])).astype(BF16)
            dq_acc[pl.ds(r0, r1 - r0), :] += _dot_nn(ds, k)
            dk_acc[:, rows] += _dot_nn(q_t[:, r0:r1], ds)
            dv_acc[:, rows] += _dot_nn(do_t[:, r0:r1], p.astype(BF16))

        _sweep(qi, ratio, bk, step, unroll=4)
        dq_ref[...] = _rope_slab_bwd(dq_acc[...] * MLA_SCALE, c_ref[...], s_ref[...]).astype(dq_ref.dtype)

        @pl.when(qi == nq - 1)
        def _():
            dk_ref[...] = dk_acc[...].T.astype(dk_ref.dtype)
            dv_ref[...] = dv_acc[...].T.astype(dv_ref.dtype)

    blk = pl.BlockSpec((bq, LANES), lambda h, i: (i, h))
    tab = pl.BlockSpec((bq, LANES), lambda h, i: (i, 0))
    k_full = pl.BlockSpec((s, LANES), lambda h, i: (0, h))
    v_full = pl.BlockSpec((s, LANES), lambda h, i: (0, heads + h))
    shape = jax.ShapeDtypeStruct((s, heads * LANES), BF16)
    ins = [q, kv, kv, o, do, lse, cos_t, sin_t] + ([dkv_init, dkv_init] if has_init else [])
    dq, dk, dv = pl.pallas_call(
        body, name=name, grid=(heads, nq),
        in_specs=[blk, k_full, v_full, blk, blk, blk, tab, tab] + ([k_full, v_full] if has_init else []),
        out_specs=[blk, k_full, k_full],
        out_shape=[shape, shape, shape],
        scratch_shapes=[pltpu.VMEM((bq, LANES), F32), pltpu.VMEM((LANES, s), F32), pltpu.VMEM((LANES, s), F32)],
        compiler_params=pltpu.CompilerParams(dimension_semantics=("arbitrary", "arbitrary"),
                                             vmem_limit_bytes=VMEM_LIMIT),
    )(*ins)
    return dq, jnp.concatenate([dk, dv], axis=1)


def _pad_last(a, width):
    return jnp.pad(a, [(0, 0)] * (a.ndim - 1) + [(0, width - a.shape[-1])])


def _pad_qkv(w, heads):
    d = w.shape[0]
    return _pad_last(w.reshape(d, 3 * heads, SB_HEAD_DIM), LANES).reshape(d, 3 * heads * LANES)


def _unpad_qkv(g, heads):
    d = g.shape[0]
    return g.reshape(d, 3 * heads, LANES)[:, :, :SB_HEAD_DIM].reshape(d, 3 * heads * SB_HEAD_DIM)


def _pad_o(w, heads):
    d = w.shape[1]
    w = w.reshape(heads, SB_HEAD_DIM, d)
    return jnp.pad(w, [(0, 0), (0, LANES - SB_HEAD_DIM), (0, 0)]).reshape(heads * LANES, d)


def _unpad_o(g, heads):
    d = g.shape[1]
    return g.reshape(heads, LANES, d)[:, :SB_HEAD_DIM, :].reshape(heads * SB_HEAD_DIM, d)


def _pad_uq(w, heads):
    r = w.shape[0]
    return _pad_last(w.reshape(r, heads, MLA_NOPE + MLA_ROPE), LANES).reshape(r, heads * LANES)


def _unpad_uq(g, heads):
    r = g.shape[0]
    return g.reshape(r, heads, LANES)[:, :, :MLA_NOPE + MLA_ROPE].reshape(r, heads * (MLA_NOPE + MLA_ROPE))


def _pad_dkv(w):
    d = w.shape[0]
    rope = jnp.zeros((d, LANES), w.dtype).at[:, ROPE_LO:ROPE_LO + MLA_ROPE].set(w[:, MLA_KV_RANK:])
    return jnp.concatenate([w[:, :MLA_KV_RANK], rope], axis=1)


def _unpad_dkv(g):
    return jnp.concatenate([g[:, :MLA_KV_RANK], g[:, MLA_KV_RANK + ROPE_LO:MLA_KV_RANK + ROPE_LO + MLA_ROPE]], axis=1)


def _pad_ukv(w, heads):
    w = w.reshape(MLA_KV_RANK, heads, 2, MLA_NOPE)
    k_part = _pad_last(w[:, :, 0, :], LANES).reshape(MLA_KV_RANK, heads * LANES)
    v_part = _pad_last(w[:, :, 1, :], LANES).reshape(MLA_KV_RANK, heads * LANES)
    lane = jnp.arange(LANES)
    place = ((lane[:, None] == lane[None, :]) & (lane[:, None] >= ROPE_LO) & (lane[:, None] < ROPE_LO + MLA_ROPE))
    place = jnp.tile(place.astype(w.dtype), (1, heads))
    top = jnp.concatenate([k_part, v_part], axis=1)
    bottom = jnp.concatenate([place, jnp.zeros_like(place)], axis=1)
    return jnp.concatenate([top, bottom], axis=0)


def _unpad_ukv(g, heads):
    g = g[:MLA_KV_RANK]
    k_part = g[:, :heads * LANES].reshape(MLA_KV_RANK, heads, LANES)[:, :, :MLA_NOPE]
    v_part = g[:, heads * LANES:].reshape(MLA_KV_RANK, heads, LANES)[:, :, :MLA_V]
    return jnp.stack([k_part, v_part], axis=2).reshape(MLA_KV_RANK, heads * (MLA_NOPE + MLA_V))


def _rope_tables(positions):
    inv_freq = ROPE_THETA ** (-jnp.arange(0, MLA_ROPE, 2, dtype=F32) / MLA_ROPE)
    ang = positions.astype(F32)[:, None] * inv_freq
    cos, sin = jnp.cos(ang), jnp.sin(ang)
    s = positions.shape[0]
    cos_t = jnp.ones((s, LANES), F32).at[:, ROPE_LO:ROPE_LO + MLA_ROPE].set(jnp.concatenate([cos, cos], axis=1))
    sin_t = jnp.zeros((s, LANES), F32).at[:, ROPE_LO:ROPE_LO + MLA_ROPE].set(jnp.concatenate([-sin, sin], axis=1))
    return cos_t, sin_t


def _local_step(x, positions, target, qkv_w0, norms, rest_weights, ride=None, early_reduce=None):
    s, d = x.shape
    heads = d // SB_HEAD_DIM
    cos_t, sin_t = _rope_tables(positions)

    h_first = _rms_fwd(x, norms["attn_norm"][0], name="l0_attn_norm")
    qkv_first = _mm(h_first, _pad_qkv(qkv_w0, heads), name="l0_qkv")
    if ride is None:
        o_first, gathered = _sb_fwd(qkv_first, heads, name="l0_sb_fwd"), None
    else:
        o_first, gathered = _sb_fwd(qkv_first, heads, name="l0_sb_fwd", ride=ride)
    w = rest_weights(gathered)
    later = w.pop("later", None)
    depth = norms["attn_norm"].shape[0]
    n_a = w["sb_w_qkv"].shape[0]
    n_b = depth - n_a
    assert later is None or n_a >= 2

    wqkv = [_pad_qkv(w["sb_w_qkv"][l], heads) for l in range(n_a)]
    wo_a = [_pad_o(w["sb_w_o"][l], heads) for l in range(n_a)]
    w1 = [w["mlp_w1"][l] for l in range(w["mlp_w1"].shape[0])]
    w2 = [w["mlp_w2"][l] for l in range(w["mlp_w2"].shape[0])]
    mla = {}

    def take_mla_weights(src):
        mla.update(wdkv=_pad_dkv(src["mla_w_dkv"]), wkv=_pad_ukv(src["mla_w_ukv"], heads),
                   wdq=[src["mla_w_dq"][j] for j in range(n_b)],
                   wuq=[_pad_uq(src["mla_w_uq"][j], heads) for j in range(n_b)],
                   wo_b=[_pad_o(src["mla_w_o"][j], heads) for j in range(n_b)])

    if later is None:
        take_mla_weights(w)

    saved = []
    kv_saved = None
    kv = None
    for l in range(depth):
        t = f"l{l}"
        sv = {"x_in": x}
        h = h_first if l == 0 else _rms_fwd(x, norms["attn_norm"][l], name=f"{t}_attn_norm")
        sv["h"] = h
        if l < n_a:
            if l == 0:
                qkv, o = qkv_first, o_first
            else:
                qkv = _mm(h, wqkv[l], name=f"{t}_qkv")
                if l == 1 and later is not None:
                    o, gathered_later = _sb_fwd(qkv, heads, name=f"{t}_sb_fwd", ride=later[0])
                    w_later = later[1](gathered_later)
                    take_mla_weights(w_later)
                    w1 += [w_later["mlp_w1"][i] for i in range(w_later["mlp_w1"].shape[0])]
                    w2 += [w_later["mlp_w2"][i] for i in range(w_later["mlp_w2"].shape[0])]
                else:
                    o = _sb_fwd(qkv, heads, name=f"{t}_sb_fwd")
            sv["qkv"], sv["o"] = qkv, o
            x = _mm(o, wo_a[l], name=f"{t}_attn_out", epilogue=_epi_add, extras=[(x, "tile")], out_dtypes=(F32,))
        else:
            j = l - n_a
            wdkv, wkv, wdq, wuq, wo_b = mla["wdkv"], mla["wkv"], mla["wdq"], mla["wuq"], mla["wo_b"]
            if j == 0:
                hk = _rms_fwd(x, norms["kv_norm"], name="kv_norm")
                down = _mm(hk, wdkv, name="kv_down", out_dtypes=(F32,))
                cat = _kv_prep(down, norms["mla_kv_lat_norm"], cos_t, sin_t, name="kv_prep")
                kv = _mm(cat, wkv, name="kv_up")
                kv_saved = {"x_in": x, "hk": hk, "down": down, "cat": cat}
            cq0 = _mm(h, wdq[j], name=f"{t}_q_down", out_dtypes=(F32,))
            cq = _rms_fwd(cq0, norms["mla_q_lat_norm"][j], name=f"{t}_q_lat_norm")
            q = _mm(cq, wuq[j], name=f"{t}_q_up", epilogue=_epi_rope_heads, extras=[(cos_t, "row"), (sin_t, "row")])
            o, lse = _mla_fwd(q, kv, heads, name=f"{t}_mla_fwd")
            sv.update(cq0=cq0, cq=cq, q=q, o=o, lse=lse)
            x = _mm(o, wo_b[j], name=f"{t}_attn_out", epilogue=_epi_add, extras=[(x, "tile")], out_dtypes=(F32,))
        sv["x_mid"] = x
        h2 = _rms_fwd(x, norms["mlp_norm"][l], name=f"{t}_mlp_norm")
        u, a = _mm(h2, w1[l], name=f"{t}_mlp_up", epilogue=_epi_relu2, out_dtypes=(BF16, BF16))
        sv.update(h2=h2, u=u, a=a)
        x = _mm(a, w2[l], name=f"{t}_mlp_down", epilogue=_epi_add, extras=[(x, "tile")], out_dtypes=(F32,))
        saved.append(sv)

    loss_slab, dx, dxb, dg_final = _loss_bwd(x, norms["final_norm"], target, name="loss")
    loss = loss_slab[0, 0]

    g_attn_norm, g_mlp_norm = [None] * depth, [None] * depth
    g_qkv, g_o_a = [None] * n_a, [None] * n_a
    g_dq, g_uq, g_o_b, g_qlat = [None] * n_b, [None] * n_b, [None] * n_b, [None] * n_b
    g_w1, g_w2 = [None] * depth, [None] * depth
    dkv = None
    g_kv_norm = g_kv_lat = g_dkv = g_ukv = None
    early_parts = early_got = None

    for l in reversed(range(depth)):
        t = f"l{l}"
        sv = saved[l]
        du = _mm(dxb, w2[l], name=f"{t}_mlp_down_dx", dims="nt", epilogue=_epi_relu2_grad, extras=[(sv["u"], "tile")])
        g_w2[l] = _mm(sv["a"], dxb, name=f"{t}_mlp_down_dw", dims="tn", out_dtypes=(F32,))
        g_w1[l] = _mm(sv["h2"], du, name=f"{t}_mlp_up_dw", dims="tn", out_dtypes=(F32,))
        dx, dxb, g_mlp_norm[l] = _mm_rms_bwd(du, w1[l], sv["x_mid"], norms["mlp_norm"][l], dx, name=f"{t}_mlp_up_dx")
        if l < n_a:
            do = _mm(dxb, wo_a[l], name=f"{t}_attn_out_dx", dims="nt")
            g_o_a[l] = _unpad_o(_mm(sv["o"], dxb, name=f"{t}_attn_out_dw", dims="tn", out_dtypes=(F32,)), heads)
            if l == 0 and early_reduce is not None:
                early_parts = early_reduce({
                    "sb_w_qkv": g_qkv[1:], "sb_w_o": g_o_a, "mla_w_dkv": g_dkv, "mla_w_ukv": g_ukv, "mla_w_dq": g_dq,
                    "mla_w_uq": g_uq, "mla_w_o": g_o_b, "mlp_w1": g_w1, "mlp_w2": g_w2})
                dq, dk, dv, early_got = _sb_bwd(sv["qkv"], sv["o"], do, heads, name=f"{t}_sb_bwd", ride=early_parts)
            else:
                dq, dk, dv = _sb_bwd(sv["qkv"], sv["o"], do, heads, name=f"{t}_sb_bwd")
            dqkv = jnp.concatenate([dq, dk, dv], axis=1)
            g_qkv[l] = _unpad_qkv(_mm(sv["h"], dqkv, name=f"{t}_qkv_dw", dims="tn", out_dtypes=(F32,)), heads)
            dh_src, dh_w = dqkv, wqkv[l]
        else:
            j = l - n_a
            do = _mm(dxb, wo_b[j], name=f"{t}_attn_out_dx", dims="nt")
            g_o_b[j] = _unpad_o(_mm(sv["o"], dxb, name=f"{t}_attn_out_dw", dims="tn", out_dtypes=(F32,)), heads)
            dq, dkv = _mla_bwd(sv["q"], kv, sv["o"], do, sv["lse"], cos_t, sin_t, dkv, heads, name=f"{t}_mla_bwd")
            g_uq[j] = _unpad_uq(_mm(sv["cq"], dq, name=f"{t}_q_up_dw", dims="tn", out_dtypes=(F32,)), heads)
            _, dcq0, g_qlat[j] = _mm_rms_bwd(dq, wuq[j], sv["cq0"], norms["mla_q_lat_norm"][j], None,
                                             name=f"{t}_q_up_dx")
            g_dq[j] = _mm(sv["h"], dcq0, name=f"{t}_q_down_dw", dims="tn", out_dtypes=(F32,))
            dh_src, dh_w = dcq0, wdq[j]
        if l == 0:
            dh = _mm(dh_src, dh_w, name=f"{t}_attn_in_dx", dims="nt", out_dtypes=(F32,))
            dx, dxb, g_attn_norm[l] = _rms_bwd(sv["x_in"], norms["attn_norm"][l], dh, dx, name=f"{t}_attn_norm_bwd",
                                               lead_axis=True)
        else:
            dx, dxb, g_attn_norm[l] = _mm_rms_bwd(dh_src, dh_w, sv["x_in"], norms["attn_norm"][l], dx,
                                                  name=f"{t}_attn_in_dx")
        if l == n_a:
            ks = kv_saved
            dcat = _mm(dkv, wkv, name="kv_up_dx", dims="nt", out_dtypes=(F32,))
            g_ukv = _unpad_ukv(_mm(ks["cat"], dkv, name="kv_up_dw", dims="tn", out_dtypes=(F32,)), heads)
            ddown, g_kv_lat = _kv_prep_bwd(ks["down"], norms["mla_kv_lat_norm"], cos_t, sin_t, dcat, name="kv_prep_bwd")
            g_dkv = _unpad_dkv(_mm(ks["hk"], ddown, name="kv_down_dw", dims="tn", out_dtypes=(F32,)))
            dx, dxb, g_kv_norm = _mm_rms_bwd(ddown, wdkv, ks["x_in"], norms["kv_norm"], dx, name="kv_down_dx")

    grads = {
        "attn_norm": jnp.concatenate(g_attn_norm, axis=0), "mlp_norm": jnp.concatenate(g_mlp_norm, axis=0),
        "sb_w_qkv": g_qkv, "sb_w_o": g_o_a,
        "kv_norm": g_kv_norm[0], "mla_w_dkv": g_dkv, "mla_kv_lat_norm": g_kv_lat[0], "mla_w_ukv": g_ukv,
        "mla_w_dq": g_dq, "mla_q_lat_norm": jnp.concatenate(g_qlat, axis=0),
        "mla_w_uq": g_uq, "mla_w_o": g_o_b,
        "mlp_w1": g_w1, "mlp_w2": g_w2, "final_norm": dg_final[0],
    }
    return loss, dx, grads, early_parts, early_got


def _flat_rows(n_elems):
    per_block = FLAT_COLS * FLAT_ROW_BLOCK * 2
    return -(-n_elems // per_block) * FLAT_ROW_BLOCK * 2


def _row_blocks(arrays, dtype):
    for a in arrays:
        assert a.size % FLAT_COLS == 0, a.shape
    blocks = [a.astype(dtype).reshape(-1, FLAT_COLS) for a in arrays]
    used = sum(b.shape[0] for b in blocks)
    rows = _flat_rows(used * FLAT_COLS)
    return blocks + [jnp.zeros((rows - used, FLAT_COLS), dtype)], rows


def _pack(arrays, dtype):
    blocks, _ = _row_blocks(arrays, dtype)
    return jnp.concatenate(blocks, axis=0)


def _pack_chips(per_chip, dtype):
    blocks, rows = [], 0
    for arrays in per_chip:
        chip_blocks, rows = _row_blocks(arrays, dtype)
        blocks += chip_blocks
    return jnp.concatenate(blocks, axis=0).reshape(len(per_chip), rows, FLAT_COLS)


def _unpack(flat, shapes):
    out, row = [], 0
    for shp in shapes:
        n = 1
        for v in shp:
            n *= v
        out.append(flat[row:row + n // FLAT_COLS].reshape(shp))
        row += n // FLAT_COLS
    return out


def _pack_small(arrays):
    rows = []
    for a in arrays:
        a = a.reshape(-1, a.shape[-1]) if a.shape[-1] == FLAT_COLS else a.reshape(1, -1)
        rows.append(_pad_last(a, FLAT_COLS))
    flat = jnp.concatenate(rows, axis=0)
    return jnp.pad(flat, [(0, -flat.shape[0] % 8), (0, 0)])


def _unpack_small(flat, shapes):
    out, row = [], 0
    for shp in shapes:
        if shp[-1] == FLAT_COLS:
            n = 1
            for v in shp[:-1]:
                n *= v
            out.append(flat[row:row + n].reshape(shp))
            row += n
        else:
            n = 1
            for v in shp:
                n *= v
            out.append(flat[row, :n].reshape(shp))
            row += 1
    return out


def _other_chips(x, y):
    return [(1 - x, y), (x, 1 - y), (1 - x, 1 - y)]


def _all_gather_chips(flat, *, name):
    rows, cols = flat.shape

    def body(x_ref, out_ref, *sems):
        _all_gather_start(x_ref, out_ref, sems)
        _all_gather_finish(x_ref, out_ref, sems)

    return pl.pallas_call(
        body, name=name,
        in_specs=[pl.BlockSpec(memory_space=pltpu.HBM)],
        out_specs=pl.BlockSpec(memory_space=pltpu.HBM),
        out_shape=jax.ShapeDtypeStruct((N_CHIPS, rows, cols), flat.dtype),
        scratch_shapes=_all_gather_sems(),
        compiler_params=pltpu.CompilerParams(has_side_effects=True),
    )(flat)


def _all_gather_sems():
    return [pltpu.SemaphoreType.DMA((3,)), pltpu.SemaphoreType.DMA((3,)), pltpu.SemaphoreType.DMA((3,)),
            pltpu.SemaphoreType.DMA((3,)), pltpu.SemaphoreType.DMA, pltpu.SemaphoreType.DMA]


def _all_gather_copies(x_ref, out_ref, sems, finishing):
    send_sems, recv_sems, pass_send_sems, pass_recv_sems, own_send_sem, own_recv_sem = sems
    x, y, c = lax.axis_index("x"), lax.axis_index("y"), lax.axis_index("c")
    me = 2 * x + y
    my_rows, sib_rows = _half_rows(x_ref.shape[0])

    def copy(src, dst, send_sem, recv_sem, to):
        return pltpu.make_async_remote_copy(src_ref=src, dst_ref=dst, send_sem=send_sem, recv_sem=recv_sem,
                                            device_id=to, device_id_type=MESH)

    own = copy(x_ref, out_ref.at[me], own_send_sem, own_recv_sem, _sibling())
    to_chips, landed, pass_on, passed = [], [], [], []
    for k, (px, py) in enumerate(_other_chips(x, y)):
        to_chips.append(copy(x_ref.at[my_rows, :], out_ref.at[me, my_rows, :], send_sems.at[k], recv_sems.at[k],
                             (px, py, c)))
        if finishing:
            mine, theirs = out_ref.at[2 * px + py, my_rows, :], out_ref.at[2 * px + py, sib_rows, :]
            landed.append(copy(mine, mine, send_sems.at[k], recv_sems.at[k], (px, py, c)))
            pass_on.append(copy(mine, mine, pass_send_sems.at[k], pass_recv_sems.at[k], _sibling()))
            passed.append(copy(theirs, theirs, pass_send_sems.at[k], pass_recv_sems.at[k], _sibling()))
    return own, to_chips, landed, pass_on, passed


def _all_gather_start(x_ref, out_ref, sems):
    own, to_chips, _, _, _ = _all_gather_copies(x_ref, out_ref, sems, finishing=False)
    own.start()
    for cp in to_chips:
        cp.start()


def _all_gather_finish(x_ref, out_ref, sems):
    own, to_chips, landed, pass_on, passed = _all_gather_copies(x_ref, out_ref, sems, finishing=True)
    for k in range(len(landed)):
        landed[k].wait_recv()
        pass_on[k].start()
    for cp in passed:
        cp.wait_recv()
    own.wait_recv()
    for cp in [own] + to_chips + pass_on:
        cp.wait_send()


def _exchange_chips(parts, *, name):
    def body(g_ref, out_ref, *sems):
        _exchange_start(g_ref, out_ref, sems)
        _exchange_finish(g_ref, out_ref, sems)

    return pl.pallas_call(
        body, name=name,
        in_specs=[pl.BlockSpec(memory_space=pltpu.HBM)],
        out_specs=pl.BlockSpec(memory_space=pltpu.HBM),
        out_shape=jax.ShapeDtypeStruct(parts.shape, parts.dtype),
        scratch_shapes=_exchange_sems(),
        compiler_params=pltpu.CompilerParams(has_side_effects=True),
    )(parts)


def _exchange_sems():
    return [pltpu.SemaphoreType.DMA((3,)), pltpu.SemaphoreType.DMA((3,))]


def _exchange_copies(g_ref, out_ref, sems, receiving):
    send_sems, recv_sems = sems
    x, y, c = lax.axis_index("x"), lax.axis_index("y"), lax.axis_index("c")
    me = 2 * x + y
    copies = []
    for k, (px, py) in enumerate(_other_chips(x, y)):
        src, dst = (g_ref.at[me], out_ref.at[2 * px + py]) if receiving else (g_ref.at[2 * px + py], out_ref.at[me])
        copies.append(pltpu.make_async_remote_copy(src_ref=src, dst_ref=dst, send_sem=send_sems.at[k],
                                                   recv_sem=recv_sems.at[k], device_id=(px, py, c),
                                                   device_id_type=MESH))
    return copies


def _exchange_start(g_ref, out_ref, sems):
    for cp in _exchange_copies(g_ref, out_ref, sems, receiving=False):
        cp.start()


def _exchange_finish(g_ref, out_ref, sems):
    for cp in _exchange_copies(g_ref, out_ref, sems, receiving=True):
        cp.wait_recv()
    for cp in _exchange_copies(g_ref, out_ref, sems, receiving=False):
        cp.wait_send()


def _my_chip():
    return 2 * lax.axis_index("x") + lax.axis_index("y")


def _half_rows(rows):
    c = lax.axis_index("c")
    half = rows // 2
    return pl.ds(pl.multiple_of(c * half, 8), half), pl.ds(pl.multiple_of((1 - c) * half, 8), half)


def _sibling():
    return (lax.axis_index("x"), lax.axis_index("y"), 1 - lax.axis_index("c"))


def _pair_exchange(parts, *, name):
    n, rows, cols = parts.shape

    def body(p_ref, theirs_ref, send_sem, recv_sem):
        _, sib_rows = _half_rows(rows)
        cp = pltpu.make_async_remote_copy(src_ref=p_ref.at[:, sib_rows, :], dst_ref=theirs_ref, send_sem=send_sem,
                                          recv_sem=recv_sem, device_id=_sibling(), device_id_type=MESH)
        cp.start()
        cp.wait()

    half = rows // 2
    theirs = pl.pallas_call(
        body, name=name,
        in_specs=[pl.BlockSpec(memory_space=pltpu.HBM)],
        out_specs=pl.BlockSpec(memory_space=pltpu.HBM),
        out_shape=jax.ShapeDtypeStruct((n, half, cols), parts.dtype),
        scratch_shapes=[pltpu.SemaphoreType.DMA, pltpu.SemaphoreType.DMA],
        compiler_params=pltpu.CompilerParams(has_side_effects=True),
    )(parts)
    mine = lax.dynamic_slice_in_dim(parts, lax.axis_index("c") * half, half, axis=1)
    return mine, theirs


def _pair_sum(mine, theirs, *, name):
    n, rows, cols = mine.shape

    def body(a_ref, b_ref, o_ref):
        o_ref[...] = (a_ref[...].astype(F32) + b_ref[...].astype(F32)).astype(o_ref.dtype)

    blk = pl.BlockSpec((n, FLAT_ROW_BLOCK, cols), lambda i: (0, i, 0))
    return pl.pallas_call(
        body, name=name, grid=(rows // FLAT_ROW_BLOCK,),
        in_specs=[blk, blk], out_specs=blk, out_shape=jax.ShapeDtypeStruct(mine.shape, mine.dtype),
        compiler_params=pltpu.CompilerParams(dimension_semantics=("parallel",), vmem_limit_bytes=VMEM_LIMIT),
    )(mine, theirs)


def _sum_chips(received, own, *, name):
    _, rows, cols = received.shape

    def body(p_ref, own_ref, o_ref):
        me = 2 * lax.axis_index("x") + lax.axis_index("y")
        slot = [jnp.where(me == j, own_ref[j], p_ref[j]).astype(F32) for j in range(N_CHIPS)]
        o_ref[...] = ((slot[0] + slot[1]) + slot[2]) + slot[3]

    blk = pl.BlockSpec((N_CHIPS, FLAT_ROW_BLOCK, cols), lambda i: (0, i, 0))
    return pl.pallas_call(
        body, name=name, grid=(rows // FLAT_ROW_BLOCK,),
        in_specs=[blk, blk],
        out_specs=pl.BlockSpec((FLAT_ROW_BLOCK, cols), lambda i: (i, 0)),
        out_shape=jax.ShapeDtypeStruct((rows, cols), F32),
        compiler_params=pltpu.CompilerParams(dimension_semantics=("parallel",), vmem_limit_bytes=VMEM_LIMIT),
    )(received, own)


def _join_cores(half, *, name):
    rows2, cols = half.shape

    def body(h_ref, out_ref, send_sem, recv_sem):
        my_rows, sib_rows = _half_rows(2 * rows2)
        cp = pltpu.make_async_remote_copy(src_ref=h_ref, dst_ref=out_ref.at[my_rows, :], send_sem=send_sem,
                                          recv_sem=recv_sem, device_id=_sibling(), device_id_type=MESH)
        cp.start()
        cp.wait_send()
        pltpu.make_async_remote_copy(src_ref=h_ref, dst_ref=out_ref.at[sib_rows, :], send_sem=send_sem,
                                     recv_sem=recv_sem, device_id=_sibling(), device_id_type=MESH).wait_recv()

    out = pl.pallas_call(
        body, name=name,
        in_specs=[pl.BlockSpec(memory_space=pltpu.HBM)],
        out_specs=pl.BlockSpec(memory_space=pltpu.HBM),
        out_shape=jax.ShapeDtypeStruct((2 * rows2, cols), half.dtype),
        scratch_shapes=[pltpu.SemaphoreType.DMA, pltpu.SemaphoreType.DMA],
        compiler_params=pltpu.CompilerParams(has_side_effects=True),
    )(half)
    return lax.dynamic_update_slice_in_dim(out, half, lax.axis_index("c") * rows2, axis=0)


def _all_reduce_small(v, *, name):
    rows, cols = v.shape
    flips = [(fx, fy, fc) for fx in (0, 1) for fy in (0, 1) for fc in (0, 1)][1:]

    def body(v_ref, out_ref, gath_ref, send_sems, recv_sems):
        x, y, c = lax.axis_index("x"), lax.axis_index("y"), lax.axis_index("c")
        me = 4 * x + 2 * y + c
        gath_ref[me] = v_ref[...]
        peers = [((1 - x) if fx else x, (1 - y) if fy else y, (1 - c) if fc else c) for fx, fy, fc in flips]
        sends = []
        for k, peer in enumerate(peers):
            cp = pltpu.make_async_remote_copy(src_ref=v_ref, dst_ref=gath_ref.at[me], send_sem=send_sems.at[k],
                                              recv_sem=recv_sems.at[k], device_id=peer, device_id_type=MESH)
            cp.start()
            sends.append(cp)
        for k, (px, py, pc) in enumerate(peers):
            pltpu.make_async_remote_copy(src_ref=v_ref, dst_ref=gath_ref.at[4 * px + 2 * py + pc],
                                         send_sem=send_sems.at[k], recv_sem=recv_sems.at[k],
                                         device_id=(px, py, pc), device_id_type=MESH).wait_recv()
        for cp in sends:
            cp.wait_send()
        total = gath_ref[0]
        for k in range(1, 8):
            total = total + gath_ref[k]
        out_ref[...] = total

    total, _ = pl.pallas_call(
        body, name=name,
        in_specs=[pl.BlockSpec(memory_space=pltpu.VMEM)],
        out_specs=[pl.BlockSpec(memory_space=pltpu.VMEM), pl.BlockSpec(memory_space=pltpu.VMEM)],
        out_shape=[jax.ShapeDtypeStruct((rows, cols), v.dtype), jax.ShapeDtypeStruct((8, rows, cols), v.dtype)],
        scratch_shapes=[pltpu.SemaphoreType.DMA((7,)), pltpu.SemaphoreType.DMA((7,))],
        compiler_params=pltpu.CompilerParams(has_side_effects=True),
    )(v)
    return total


def _adamw(w, g, m, v, *, name):
    shape = w.shape
    cols = shape[-1]
    w2, g2, m2, v2 = (a.reshape(-1, cols) for a in (w, g, m, v))
    rows = w2.shape[0]
    br = _pick_rows(rows, FLAT_ROW_BLOCK)

    def body(w_ref, g_ref, m_ref, v_ref, d_out, m_out, v_out):
        gv = g_ref[...]
        m_new = ADAM_B1 * m_ref[...] + (1.0 - ADAM_B1) * gv
        v_new = ADAM_B2 * v_ref[...] + (1.0 - ADAM_B2) * jnp.square(gv)
        m_hat = m_new / (1.0 - ADAM_B1 ** ADAM_STEP)
        v_hat = v_new / (1.0 - ADAM_B2 ** ADAM_STEP)
        d_out[...] = -ADAM_LR * (m_hat / (jnp.sqrt(v_hat) + ADAM_EPS) + ADAM_WD * w_ref[...])
        m_out[...] = m_new
        v_out[...] = v_new

    blk = pl.BlockSpec((br, cols), lambda i: (i, 0))
    out = jax.ShapeDtypeStruct((rows, cols), F32)
    outs = pl.pallas_call(
        body, name=name, grid=(rows // br,),
        in_specs=[blk] * 4, out_specs=[blk] * 3, out_shape=[out] * 3,
        compiler_params=pltpu.CompilerParams(dimension_semantics=("parallel",), vmem_limit_bytes=VMEM_LIMIT),
    )(w2, g2, m2, v2)
    return [o.reshape(shape) for o in outs]


def _pick_rows(rows, target):
    if rows <= target:
        return rows
    return max(b for b in range(8, target + 1, 8) if rows % b == 0)


def _assemble(gathered_shards, name, layer=False):
    return jnp.concatenate(gathered_shards, axis=SHARD_AXIS[name] - int(layer))


def _chip_shard(full, name, j):
    if isinstance(full, list):
        axis = SHARD_AXIS[name] - 1
        layers = full
    else:
        axis = SHARD_AXIS[name]
        layers = [full]
    n = layers[0].shape[axis] // N_CHIPS
    return [lax.slice_in_dim(g, j * n, (j + 1) * n, axis=axis) for g in layers]


def kernel(x, positions, attn_norm, mlp_norm, sb_w_qkv, sb_w_o, kv_norm, mla_w_dkv, mla_kv_lat_norm, mla_w_ukv, mla_w_dq, mla_q_lat_norm, mla_w_uq, mla_w_o, mlp_w1, mlp_w2, final_norm, loss_target, m_attn_norm, m_mlp_norm, m_sb_w_qkv, m_sb_w_o, m_kv_norm, m_mla_w_dkv, m_mla_kv_lat_norm, m_mla_w_ukv, m_mla_w_dq, m_mla_q_lat_norm, m_mla_w_uq, m_mla_w_o, m_mlp_w1, m_mlp_w2, m_final_norm, v_attn_norm, v_mlp_norm, v_sb_w_qkv, v_sb_w_o, v_kv_norm, v_mla_w_dkv, v_mla_kv_lat_norm, v_mla_w_ukv, v_mla_w_dq, v_mla_q_lat_norm, v_mla_w_uq, v_mla_w_o, v_mlp_w1, v_mlp_w2, v_final_norm):
    weights = dict(attn_norm=attn_norm, mlp_norm=mlp_norm, sb_w_qkv=sb_w_qkv, sb_w_o=sb_w_o, kv_norm=kv_norm,
                   mla_w_dkv=mla_w_dkv, mla_kv_lat_norm=mla_kv_lat_norm, mla_w_ukv=mla_w_ukv, mla_w_dq=mla_w_dq,
                   mla_q_lat_norm=mla_q_lat_norm, mla_w_uq=mla_w_uq, mla_w_o=mla_w_o, mlp_w1=mlp_w1, mlp_w2=mlp_w2,
                   final_norm=final_norm)
    m_in = dict(attn_norm=m_attn_norm, mlp_norm=m_mlp_norm, sb_w_qkv=m_sb_w_qkv, sb_w_o=m_sb_w_o, kv_norm=m_kv_norm,
                mla_w_dkv=m_mla_w_dkv, mla_kv_lat_norm=m_mla_kv_lat_norm, mla_w_ukv=m_mla_w_ukv, mla_w_dq=m_mla_w_dq,
                mla_q_lat_norm=m_mla_q_lat_norm, mla_w_uq=m_mla_w_uq, mla_w_o=m_mla_w_o, mlp_w1=m_mlp_w1,
                mlp_w2=m_mlp_w2, final_norm=m_final_norm)
    v_in = dict(attn_norm=v_attn_norm, mlp_norm=v_mlp_norm, sb_w_qkv=v_sb_w_qkv, sb_w_o=v_sb_w_o, kv_norm=v_kv_norm,
                mla_w_dkv=v_mla_w_dkv, mla_kv_lat_norm=v_mla_kv_lat_norm, mla_w_ukv=v_mla_w_ukv, mla_w_dq=v_mla_w_dq,
                mla_q_lat_norm=v_mla_q_lat_norm, mla_w_uq=v_mla_w_uq, mla_w_o=v_mla_w_o, mlp_w1=v_mlp_w1,
                mlp_w2=v_mlp_w2, final_norm=v_final_norm)
    shard_shapes = [weights[n].shape for n in BIG_WEIGHTS]
    small_shapes = [weights[n].shape for n in SMALL_WEIGHTS]

    first_name = BIG_WEIGHTS[0]
    qkv_first, qkv_later = weights[first_name][0], weights[first_name][1:]
    gathered_first = _all_gather_chips(_pack([qkv_first], BF16), name="first_weight_all_gather")
    qkv_w0 = _assemble([_unpack(gathered_first[j], [qkv_first.shape])[0] for j in range(N_CHIPS)], first_name, layer=True)
    ride_shapes = [qkv_later.shape] + shard_shapes[1:]
    early_names = ["sb_w_qkv", "sb_w_o", "mlp_w1", "mlp_w2"]
    n_sb = weights["sb_w_qkv"].shape[0]
    early_blocks = [qkv_later, weights["sb_w_o"], weights["mlp_w1"][:n_sb], weights["mlp_w2"][:n_sb]]
    late_names = ["mla_w_dkv", "mla_w_ukv", "mla_w_dq", "mla_w_uq", "mla_w_o", "mlp_w1", "mlp_w2"]
    late_blocks = [weights[n] for n in late_names[:5]] + [weights["mlp_w1"][n_sb:], weights["mlp_w2"][n_sb:]]
    ride = _pack(early_blocks, BF16)

    def gathered_weights(gathered, names, blocks):
        per_chip = [_unpack(gathered[j], [b.shape for b in blocks]) for j in range(N_CHIPS)]
        return {n: _assemble([per_chip[j][i] for j in range(N_CHIPS)], n) for i, n in enumerate(names)}

    def rest_weights(gathered):
        full = gathered_weights(gathered, early_names, early_blocks)
        full[first_name] = jnp.concatenate([qkv_w0[None], full[first_name]], axis=0)
        full["later"] = (_pack(late_blocks, BF16), lambda g: gathered_weights(g, late_names, late_blocks))
        return full

    norms = {n: weights[n] for n in SMALL_WEIGHTS}

    def chip_parts(g, tag):
        parts = _pack_chips([[piece for n in BIG_WEIGHTS if n in g for piece in _chip_shard(g[n], n, j)]
                             for j in range(N_CHIPS)], BF16)
        mine, theirs = _pair_exchange(parts, name=f"grads_pair_exchange_{tag}")
        return _pair_sum(mine, theirs, name=f"grads_pair_sum_{tag}")

    def finish(received, chip_part, tag):
        g_half = _sum_chips(received, chip_part, name=f"grads_sum_chips_{tag}")
        return _join_cores(g_half, name=f"grads_join_cores_{tag}")

    loss, dx, grads, early_parts, early_got = _local_step(
        x[0], positions[0], loss_target, qkv_w0, norms, rest_weights, ride=ride,
        early_reduce=functools.partial(chip_parts, tag="early"))
    loss = lax.psum(loss, ("x", "y", "c"))
    early_sum = finish(early_got, early_parts, "early")
    last_parts = chip_parts({first_name: grads[first_name][:1]}, "last")
    last_sum = finish(_exchange_chips(last_parts, name="grads_exchange_last"), last_parts, "last")

    out_g = dict(zip(BIG_WEIGHTS, _unpack(early_sum, ride_shapes)))
    out_g[first_name] = jnp.concatenate([_unpack(last_sum, [qkv_first.shape])[0][None], out_g[first_name]], axis=0)
    out_d, out_m, out_v = {}, {}, {}
    for n in BIG_WEIGHTS:
        out_d[n], out_m[n], out_v[n] = _adamw(weights[n], out_g[n], m_in[n], v_in[n], name=f"adamw_{n}")

    small_sum = _all_reduce_small(_pack_small([grads[n] for n in SMALL_WEIGHTS]), name="gains_all_reduce")
    sd, sm, sv = _adamw(_pack_small([weights[n] for n in SMALL_WEIGHTS]), small_sum,
                        _pack_small([m_in[n] for n in SMALL_WEIGHTS]),
                        _pack_small([v_in[n] for n in SMALL_WEIGHTS]), name="adamw_gains")
    out_g.update(zip(SMALL_WEIGHTS, _unpack_small(small_sum, small_shapes)))
    out_d.update(zip(SMALL_WEIGHTS, _unpack_small(sd, small_shapes)))
    out_m.update(zip(SMALL_WEIGHTS, _unpack_small(sm, small_shapes)))
    out_v.update(zip(SMALL_WEIGHTS, _unpack_small(sv, small_shapes)))

    return (loss, dx, *[out_g[n] for n in ALL_WEIGHTS], *[out_d[n] for n in ALL_WEIGHTS],
            *[out_m[n] for n in ALL_WEIGHTS], *[out_v[n] for n in ALL_WEIGHTS])
```
